```python
import functools
import jax, jax.numpy as jnp
from jax import lax
import numpy as np

D_MODEL = 1024
BATCH = 8
SEQ = 8192
DEPTH = 1

GRID_W = 64
CTX_LEN = 256
EPS = 1e-6

GLA_HEADS = 4
GLA_DK = D_MODEL // 2 // GLA_HEADS
GLA_DV = D_MODEL // GLA_HEADS
GLA_QK = GLA_HEADS * GLA_DK
GLA_V = GLA_HEADS * GLA_DV
GLA_RANK = 16
GLA_TAU = 16.0
GLA_CHUNK = 64

SSM_INNER = 2 * D_MODEL
SSM_HEADDIM = 64
SSM_HEADS = SSM_INNER // SSM_HEADDIM
SSM_GROUPS = 4
SSM_HPG = SSM_HEADS // SSM_GROUPS
SSM_STATE = 128
SSM_BC = SSM_GROUPS * SSM_STATE
SSM_CONV = 4
CONV_LEFT = 2
SSM_CONV_DIM = SSM_INNER + 2 * SSM_BC
SSM_CHUNK = 128

D_FF = ((8 * D_MODEL // 3 + 255) // 256) * 256

IN_WIDTHS = (GLA_QK, GLA_QK, GLA_V, GLA_V, GLA_RANK, GLA_RANK,
             SSM_INNER, SSM_INNER, SSM_BC, SSM_BC, SSM_HEADS, SSM_HEADS,
             D_MODEL, D_MODEL)
D_IN = sum(IN_WIDTHS)

kernel_name = "hybrid_gla_ssd_prefix_dit"


def rms_norm(x, w):
    xf = x.astype(jnp.float32)
    y = xf * lax.rsqrt(jnp.mean(xf * xf, axis=-1, keepdims=True) + EPS)
    return (y * w.astype(jnp.float32)).astype(x.dtype)


def modulate(h, shift, scale):
    return h * (1 + scale) + shift


def _split_in(proj):
    idx = np.cumsum(IN_WIDTHS)[:-1].tolist()
    return jnp.split(proj, idx, axis=-1)


def _chunks(t, size):
    Bn, L = t.shape[:2]
    return jnp.moveaxis(t.reshape(Bn, L // size, size, *t.shape[2:]), 1, 0)


def _unchunks(t):
    n, Bn, C = t.shape[:3]
    return jnp.moveaxis(t, 0, 1).reshape(Bn, n * C, *t.shape[3:])


def _dwconv_centred(u, w, b):
    W = u.shape[1]
    up = jnp.pad(u, ((0, 0), (CONV_LEFT, SSM_CONV - 1 - CONV_LEFT), (0, 0)))
    out = b + up[:, 0:W] * w[0]
    for j in range(1, SSM_CONV):
        out = out + up[:, j:j + W] * w[j]
    return out


def _seq_conv(u, w, b):
    return _dwconv_centred(u, w, b)


def _row_conv(u, w, b, rows):
    Bn, L, Cc = u.shape
    return _dwconv_centred(u.reshape(Bn * rows, GRID_W, Cc), w, b).reshape(Bn, L, Cc)


def _gla_scan(q, k, v, log_g, s0):
    out_dtype = v.dtype
    f32 = jnp.float32
    qc, kc, vc, gc = (_chunks(t.astype(f32), GLA_CHUNK) for t in (q, k, v, log_g))
    mask = jnp.tril(jnp.ones((GLA_CHUNK, GLA_CHUNK), bool))[None, :, :, None, None]

    def step(S, inp):
        qi, ki, vi, gi = inp
        b = jnp.cumsum(gi, axis=1)
        b_last = b[:, -1]
        diff = b[:, :, None] - b[:, None, :]
        decay = jnp.exp(jnp.where(mask, diff, -jnp.inf))
        att = jnp.einsum('bthk,bshk,btshk->bhts', qi, ki, decay)
        o = jnp.einsum('bhts,bshv->bthv', att, vi) + jnp.einsum('bthk,bhkv->bthv', qi * jnp.exp(b), S)
        S_new = jnp.exp(b_last)[..., None] * S + jnp.einsum(
            'bshk,bshv->bhkv', ki * jnp.exp(b_last[:, None] - b), vi)
        return S_new, o

    S_fin, o = lax.scan(step, s0, (qc, kc, vc, gc))
    return _unchunks(o).astype(out_dtype), S_fin


def _gla_state(k, v, log_g):
    b = jnp.cumsum(log_g.astype(jnp.float32), axis=1)
    return jnp.einsum('blhk,blhv->bhkv', k.astype(jnp.float32) * jnp.exp(b[:, -1:] - b), v.astype(jnp.float32))


def _ssd_scan(x, bm, cm, a, s0):
    out_dtype = bm.dtype
    f32 = jnp.float32
    xc, bc, cc, ac = (_chunks(t.astype(f32), SSM_CHUNK) for t in (x, bm, cm, a))
    mask = jnp.tril(jnp.ones((SSM_CHUNK, SSM_CHUNK), bool))[None, :, :, None, None]

    def step(S, inp):
        xi, bi, ci, ai = inp
        cum = jnp.cumsum(ai, axis=1)
        seg = cum[:, :, None] - cum[:, None, :]
        Lm = jnp.exp(jnp.where(mask, seg, -jnp.inf))
        cb = jnp.einsum('btgn,bsgn->btsg', ci, bi)
        y = jnp.einsum('btsg,btsge,bsgep->btgep', cb, Lm, xi)
        y = y + jnp.einsum('btgn,bgenp->btgep', ci, S) * jnp.exp(cum)[..., None]
        S_new = jnp.exp(cum[:, -1])[..., None, None] * S + jnp.einsum(
            'bsgn,bsge,bsgep->bgenp', bi, jnp.exp(cum[:, -1:] - cum), xi)
        return S_new, y

    S_fin, y = lax.scan(step, s0, (xc, bc, cc, ac))
    return _unchunks(y).astype(out_dtype), S_fin


def _ssd_state(x, bm, a):
    cum = jnp.cumsum(a.astype(jnp.float32), axis=1)
    return jnp.einsum('blgn,blge,blgep->bgenp', bm.astype(jnp.float32),
                      jnp.exp(cum[:, -1:] - cum), x.astype(jnp.float32))


def _flip(t):
    return jnp.flip(t, axis=1)


def _bidir(scan_fn, fwd_in, bwd_in, s_f, s_b):
    o_f, st_f = scan_fn(*fwd_in, s_f)
    o_b, st_b = scan_fn(*[_flip(t) for t in bwd_in], s_b)
    return o_f + _flip(o_b), st_f, st_b


def _gla_features(q, k, v, lr_f, lr_b, lp):
    Bn, L, _ = q.shape
    heads = lambda t, d: t.reshape(Bn, L, GLA_HEADS, d)
    q = heads(q, GLA_DK) * (GLA_DK ** -0.5)
    k = heads(k, GLA_DK)
    v = heads(v, GLA_DV)

    def log_gate(lr, up, bias):
        pre = (lr @ up + bias).astype(jnp.float32)
        return heads(jax.nn.log_sigmoid(pre) / GLA_TAU, GLA_DK)

    return (q, k, v, log_gate(lr_f, lp['gla_up_f'], lp['gla_bias_f']),
            log_gate(lr_b, lp['gla_up_b'], lp['gla_bias_b']))


def _ssm_features(xs, bm, cm, dt_f, dt_b, lp, conv_fn):
    xbc = jax.nn.silu(conv_fn(jnp.concatenate([xs, bm, cm], axis=-1), lp['conv_w'], lp['conv_b']))
    xs, bm, cm = jnp.split(xbc, [SSM_INNER, SSM_INNER + SSM_BC], axis=-1)
    Bn, L, _ = xs.shape
    xh = xs.reshape(Bn, L, SSM_GROUPS, SSM_HPG, SSM_HEADDIM)
    bm = bm.reshape(Bn, L, SSM_GROUPS, SSM_STATE)
    cm = cm.reshape(Bn, L, SSM_GROUPS, SSM_STATE)

    def direction(dt_raw, dt_bias, a_log):
        dt = jax.nn.softplus((dt_raw + dt_bias).astype(jnp.float32)).reshape(Bn, L, SSM_GROUPS, SSM_HPG)
        a = -jnp.exp(a_log.astype(jnp.float32)).reshape(SSM_GROUPS, SSM_HPG) * dt
        return xh * dt[..., None], a

    return (xh, bm, cm, direction(dt_f, lp['dt_bias_f'], lp['a_log_f']),
            direction(dt_b, lp['dt_bias_b'], lp['a_log_b']))


def _mixer(parts, lp, init, conv_fn):
    q, k, v, r, lr_f, lr_b, z, xs, bm, cm, dt_f, dt_b, gate_a, gate_b = parts
    Bn, L, _ = q.shape
    gq, gk, gv, lg_f, lg_b = _gla_features(q, k, v, lr_f, lr_b, lp)
    o_a, st_af, st_ab = _bidir(_gla_scan, (gq, gk, gv, lg_f), (gq, gk, gv, lg_b), init[0], init[1])
    o_a = rms_norm(o_a, lp['gla_norm_w']) * jax.nn.silu(r).reshape(Bn, L, GLA_HEADS, GLA_DV)
    y_a = o_a.reshape(Bn, L, GLA_V) @ lp['w_pa']
    xh, sb, sc, (x_f, a_f), (x_b, a_b) = _ssm_features(xs, bm, cm, dt_f, dt_b, lp, conv_fn)
    o_b, st_bf, st_bb = _bidir(_ssd_scan, (x_f, sb, sc, a_f), (x_b, sb, sc, a_b), init[2], init[3])
    o_b = o_b + lp['d_skip'].reshape(SSM_GROUPS, SSM_HPG, 1) * xh
    o_b = o_b.reshape(Bn, L, SSM_INNER) * jax.nn.silu(z)
    o_b = rms_norm(o_b.reshape(Bn, L, SSM_GROUPS, SSM_INNER // SSM_GROUPS),
                   lp['ssm_norm_w'].reshape(SSM_GROUPS, SSM_INNER // SSM_GROUPS)).reshape(Bn, L, SSM_INNER)
    y_b = o_b @ lp['w_pb']
    merged = jax.nn.sigmoid(gate_a) * y_a + jax.nn.sigmoid(gate_b) * y_b
    return merged @ lp['w_out'], (st_af, st_ab, st_bf, st_bb)


def _context_states(parts, lp):
    k, v, lr_f, lr_b = parts[1], parts[2], parts[4], parts[5]
    xs, bm, cm, dt_f, dt_b = parts[7], parts[8], parts[9], parts[10], parts[11]
    _, gk, gv, lg_f, lg_b = _gla_features(parts[0], k, v, lr_f, lr_b, lp)
    _, sb, _, (x_f, a_f), (x_b, a_b) = _ssm_features(xs, bm, cm, dt_f, dt_b, lp, _seq_conv)
    return (_gla_state(gk, gv, lg_f), _gla_state(_flip(gk), _flip(gv), _flip(lg_b)),
            _ssd_state(x_f, sb, a_f), _ssd_state(_flip(x_b), _flip(sb), _flip(a_b)))


def _zero_states(Bn):
    f32 = jnp.float32
    g = jnp.zeros((Bn, GLA_HEADS, GLA_DK, GLA_DV), f32)
    s = jnp.zeros((Bn, SSM_GROUPS, SSM_HPG, SSM_STATE, SSM_HEADDIM), f32)
    return (g, g, s, s)


def _swiglu(h, w_gate, w_up, w_down):
    return (jax.nn.silu(h @ w_gate) * (h @ w_up)) @ w_down


def _fwd_setup_inputs(seed: int = 0) -> dict:
    key = jax.random.key(seed)
    ks = iter(jax.random.split(key, 40))
    nrm = lambda shape, s: jax.random.normal(next(ks), shape, jnp.float32) * s
    L_ = DEPTH
    dt = jnp.exp(jax.random.uniform(next(ks), (2, L_, SSM_HEADS), jnp.float32,
                                    np.log(1e-3), np.log(1e-1)))
    dt_bias = dt + jnp.log(-jnp.expm1(-dt))
    a_log = jnp.log(jax.random.uniform(next(ks), (2, L_, SSM_HEADS), jnp.float32, 1.0, 16.0))
    return {
        'x': nrm((BATCH, SEQ, D_MODEL), 1.0),
        'c': nrm((BATCH, D_MODEL), 1.0),
        'ctx': nrm((BATCH, CTX_LEN, D_MODEL), 1.0),
        'c_ctx': nrm((D_MODEL,), 1.0),
        'w_ada': nrm((L_, D_MODEL, 6 * D_MODEL), 0.02),
        'b_ada': nrm((L_, 6 * D_MODEL), 0.01),
        'norm1_w': 1.0 + nrm((L_, D_MODEL), 0.02),
        'w_in': nrm((L_, D_MODEL, D_IN), D_MODEL ** -0.5),
        'gla_up_f': nrm((L_, GLA_RANK, GLA_QK), GLA_RANK ** -0.5),
        'gla_bias_f': nrm((L_, GLA_QK), 0.1),
        'gla_up_b': nrm((L_, GLA_RANK, GLA_QK), GLA_RANK ** -0.5),
        'gla_bias_b': nrm((L_, GLA_QK), 0.1),
        'gla_norm_w': 1.0 + nrm((L_, GLA_DV), 0.02),
        'conv_w': nrm((L_, SSM_CONV, SSM_CONV_DIM), SSM_CONV ** -0.5),
        'conv_b': nrm((L_, SSM_CONV_DIM), 0.02),
        'dt_bias_f': dt_bias[0],
        'dt_bias_b': dt_bias[1],
        'a_log_f': a_log[0],
        'a_log_b': a_log[1],
        'd_skip': 1.0 + nrm((L_, SSM_HEADS), 0.02),
        'ssm_norm_w': 1.0 + nrm((L_, SSM_INNER), 0.02),
        'w_pa': nrm((L_, GLA_V, D_MODEL), GLA_V ** -0.5),
        'w_pb': nrm((L_, SSM_INNER, D_MODEL), SSM_INNER ** -0.5),
        'w_out': nrm((L_, D_MODEL, D_MODEL), D_MODEL ** -0.5),
        'norm2_w': 1.0 + nrm((L_, D_MODEL), 0.02),
        'w_gate': nrm((L_, D_MODEL, D_FF), D_MODEL ** -0.5),
        'w_up': nrm((L_, D_MODEL, D_FF), D_MODEL ** -0.5),
        'w_down': nrm((L_, D_FF, D_MODEL), D_FF ** -0.5),
        'final_norm_w': 1.0 + nrm((D_MODEL,), 0.02),
    }


def _fwd_reference(x, c, ctx, c_ctx, w_ada, b_ada, norm1_w, w_in, gla_up_f, gla_bias_f, gla_up_b, gla_bias_b,
              gla_norm_w, conv_w, conv_b, dt_bias_f, dt_bias_b, a_log_f, a_log_b, d_skip, ssm_norm_w,
              w_pa, w_pb, w_out, norm2_w, w_gate, w_up, w_down, final_norm_w):
    rows = x.shape[1] // GRID_W
    latent_conv = functools.partial(_row_conv, rows=rows)
    h_lat, h_ctx = x, ctx
    for layer in range(DEPTH):
        lp = {
            'gla_up_f': gla_up_f[layer], 'gla_bias_f': gla_bias_f[layer],
            'gla_up_b': gla_up_b[layer], 'gla_bias_b': gla_bias_b[layer],
            'gla_norm_w': gla_norm_w[layer], 'conv_w': conv_w[layer], 'conv_b': conv_b[layer],
            'dt_bias_f': dt_bias_f[layer], 'dt_bias_b': dt_bias_b[layer],
            'a_log_f': a_log_f[layer], 'a_log_b': a_log_b[layer], 'd_skip': d_skip[layer],
            'ssm_norm_w': ssm_norm_w[layer], 'w_pa': w_pa[layer], 'w_pb': w_pb[layer], 'w_out': w_out[layer],
        }
        ada = jax.nn.silu(c)[:, None, :] @ w_ada[layer] + b_ada[layer]
        sh1, sc1, g1, sh2, sc2, g2 = jnp.split(ada, 6, axis=-1)
        ada_c = jax.nn.silu(c_ctx) @ w_ada[layer] + b_ada[layer]
        csh1, csc1, cg1, csh2, csc2, cg2 = jnp.split(ada_c, 6, axis=-1)

        parts_c = _split_in(modulate(rms_norm(h_ctx, norm1_w[layer]), csh1, csc1) @ w_in[layer])
        if layer == DEPTH - 1:
            states = _context_states(parts_c, lp)
        else:
            mix_c, states = _mixer(parts_c, lp, _zero_states(h_ctx.shape[0]), _seq_conv)
            h_ctx = h_ctx + cg1 * mix_c
            h_ctx = h_ctx + cg2 * _swiglu(modulate(rms_norm(h_ctx, norm2_w[layer]), csh2, csc2),
                                          w_gate[layer], w_up[layer], w_down[layer])

        parts = _split_in(modulate(rms_norm(h_lat, norm1_w[layer]), sh1, sc1) @ w_in[layer])
        mix, _ = _mixer(parts, lp, states, latent_conv)
        h_lat = h_lat + g1 * mix
        h_lat = h_lat + g2 * _swiglu(modulate(rms_norm(h_lat, norm2_w[layer]), sh2, sc2),
                                     w_gate[layer], w_up[layer], w_down[layer])
    return rms_norm(h_lat, final_norm_w)


import jax as _jax
import jax.numpy as _jnp

TWIN_FORMAT = 'train_step'
FWD_PARAMS = ['x', 'c', 'ctx', 'c_ctx', 'w_ada', 'b_ada', 'norm1_w', 'w_in', 'gla_up_f', 'gla_bias_f', 'gla_up_b', 'gla_bias_b', 'gla_norm_w', 'conv_w', 'conv_b', 'dt_bias_f', 'dt_bias_b', 'a_log_f', 'a_log_b', 'd_skip', 'ssm_norm_w', 'w_pa', 'w_pb', 'w_out', 'norm2_w', 'w_gate', 'w_up', 'w_down', 'final_norm_w']
TWIN_WEIGHTS = ['c_ctx', 'w_ada', 'b_ada', 'norm1_w', 'w_in', 'gla_up_f', 'gla_bias_f', 'gla_up_b', 'gla_bias_b', 'gla_norm_w', 'conv_w', 'conv_b', 'dt_bias_f', 'dt_bias_b', 'a_log_f', 'a_log_b', 'd_skip', 'ssm_norm_w', 'w_pa', 'w_pb', 'w_out', 'norm2_w', 'w_gate', 'w_up', 'w_down', 'final_norm_w']
TWIN_DIFF_INPUT = 'x'
TWIN_INPUTS = ['x', 'c', 'ctx', 'c_ctx', 'w_ada', 'b_ada', 'norm1_w', 'w_in', 'gla_up_f', 'gla_bias_f', 'gla_up_b', 'gla_bias_b', 'gla_norm_w', 'conv_w', 'conv_b', 'dt_bias_f', 'dt_bias_b', 'a_log_f', 'a_log_b', 'd_skip', 'ssm_norm_w', 'w_pa', 'w_pb', 'w_out', 'norm2_w', 'w_gate', 'w_up', 'w_down', 'final_norm_w', 'loss_target', 'm_c_ctx', 'm_w_ada', 'm_b_ada', 'm_norm1_w', 'm_w_in', 'm_gla_up_f', 'm_gla_bias_f', 'm_gla_up_b', 'm_gla_bias_b', 'm_gla_norm_w', 'm_conv_w', 'm_conv_b', 'm_dt_bias_f', 'm_dt_bias_b', 'm_a_log_f', 'm_a_log_b', 'm_d_skip', 'm_ssm_norm_w', 'm_w_pa', 'm_w_pb', 'm_w_out', 'm_norm2_w', 'm_w_gate', 'm_w_up', 'm_w_down', 'm_final_norm_w', 'v_c_ctx', 'v_w_ada', 'v_b_ada', 'v_norm1_w', 'v_w_in', 'v_gla_up_f', 'v_gla_bias_f', 'v_gla_up_b', 'v_gla_bias_b', 'v_gla_norm_w', 'v_conv_w', 'v_conv_b', 'v_dt_bias_f', 'v_dt_bias_b', 'v_a_log_f', 'v_a_log_b', 'v_d_skip', 'v_ssm_norm_w', 'v_w_pa', 'v_w_pb', 'v_w_out', 'v_norm2_w', 'v_w_gate', 'v_w_up', 'v_w_down', 'v_final_norm_w']
TWIN_OUTPUTS = ['loss', 'grad_x', 'grad_c_ctx', 'grad_w_ada', 'grad_b_ada', 'grad_norm1_w', 'grad_w_in', 'grad_gla_up_f', 'grad_gla_bias_f', 'grad_gla_up_b', 'grad_gla_bias_b', 'grad_gla_norm_w', 'grad_conv_w', 'grad_conv_b', 'grad_dt_bias_f', 'grad_dt_bias_b', 'grad_a_log_f', 'grad_a_log_b', 'grad_d_skip', 'grad_ssm_norm_w', 'grad_w_pa', 'grad_w_pb', 'grad_w_out', 'grad_norm2_w', 'grad_w_gate', 'grad_w_up', 'grad_w_down', 'grad_final_norm_w', 'delta_c_ctx', 'delta_w_ada', 'delta_b_ada', 'delta_norm1_w', 'delta_w_in', 'delta_gla_up_f', 'delta_gla_bias_f', 'delta_gla_up_b', 'delta_gla_bias_b', 'delta_gla_norm_w', 'delta_conv_w', 'delta_conv_b', 'delta_dt_bias_f', 'delta_dt_bias_b', 'delta_a_log_f', 'delta_a_log_b', 'delta_d_skip', 'delta_ssm_norm_w', 'delta_w_pa', 'delta_w_pb', 'delta_w_out', 'delta_norm2_w', 'delta_w_gate', 'delta_w_up', 'delta_w_down', 'delta_final_norm_w', 'new_m_c_ctx', 'new_m_w_ada', 'new_m_b_ada', 'new_m_norm1_w', 'new_m_w_in', 'new_m_gla_up_f', 'new_m_gla_bias_f', 'new_m_gla_up_b', 'new_m_gla_bias_b', 'new_m_gla_norm_w', 'new_m_conv_w', 'new_m_conv_b', 'new_m_dt_bias_f', 'new_m_dt_bias_b', 'new_m_a_log_f', 'new_m_a_log_b', 'new_m_d_skip', 'new_m_ssm_norm_w', 'new_m_w_pa', 'new_m_w_pb', 'new_m_w_out', 'new_m_norm2_w', 'new_m_w_gate', 'new_m_w_up', 'new_m_w_down', 'new_m_final_norm_w', 'new_v_c_ctx', 'new_v_w_ada', 'new_v_b_ada', 'new_v_norm1_w', 'new_v_w_in', 'new_v_gla_up_f', 'new_v_gla_bias_f', 'new_v_gla_up_b', 'new_v_gla_bias_b', 'new_v_gla_norm_w', 'new_v_conv_w', 'new_v_conv_b', 'new_v_dt_bias_f', 'new_v_dt_bias_b', 'new_v_a_log_f', 'new_v_a_log_b', 'new_v_d_skip', 'new_v_ssm_norm_w', 'new_v_w_pa', 'new_v_w_pb', 'new_v_w_out', 'new_v_norm2_w', 'new_v_w_gate', 'new_v_w_up', 'new_v_w_down', 'new_v_final_norm_w']
TWIN_LEAF_KINDS = {'loss': 'loss', 'grad_x': 'grad_x', 'grad_c_ctx': 'grad_w', 'grad_w_ada': 'grad_w', 'grad_b_ada': 'grad_w', 'grad_norm1_w': 'grad_w', 'grad_w_in': 'grad_w', 'grad_gla_up_f': 'grad_w', 'grad_gla_bias_f': 'grad_w', 'grad_gla_up_b': 'grad_w', 'grad_gla_bias_b': 'grad_w', 'grad_gla_norm_w': 'grad_w', 'grad_conv_w': 'grad_w', 'grad_conv_b': 'grad_w', 'grad_dt_bias_f': 'grad_w', 'grad_dt_bias_b': 'grad_w', 'grad_a_log_f': 'grad_w', 'grad_a_log_b': 'grad_w', 'grad_d_skip': 'grad_w', 'grad_ssm_norm_w': 'grad_w', 'grad_w_pa': 'grad_w', 'grad_w_pb': 'grad_w', 'grad_w_out': 'grad_w', 'grad_norm2_w': 'grad_w', 'grad_w_gate': 'grad_w', 'grad_w_up': 'grad_w', 'grad_w_down': 'grad_w', 'grad_final_norm_w': 'grad_w', 'delta_c_ctx': 'delta_w', 'delta_w_ada': 'delta_w', 'delta_b_ada': 'delta_w', 'delta_norm1_w': 'delta_w', 'delta_w_in': 'delta_w', 'delta_gla_up_f': 'delta_w', 'delta_gla_bias_f': 'delta_w', 'delta_gla_up_b': 'delta_w', 'delta_gla_bias_b': 'delta_w', 'delta_gla_norm_w': 'delta_w', 'delta_conv_w': 'delta_w', 'delta_conv_b': 'delta_w', 'delta_dt_bias_f': 'delta_w', 'delta_dt_bias_b': 'delta_w', 'delta_a_log_f': 'delta_w', 'delta_a_log_b': 'delta_w', 'delta_d_skip': 'delta_w', 'delta_ssm_norm_w': 'delta_w', 'delta_w_pa': 'delta_w', 'delta_w_pb': 'delta_w', 'delta_w_out': 'delta_w', 'delta_norm2_w': 'delta_w', 'delta_w_gate': 'delta_w', 'delta_w_up': 'delta_w', 'delta_w_down': 'delta_w', 'delta_final_norm_w': 'delta_w', 'new_m_c_ctx': 'new_m', 'new_m_w_ada': 'new_m', 'new_m_b_ada': 'new_m', 'new_m_norm1_w': 'new_m', 'new_m_w_in': 'new_m', 'new_m_gla_up_f': 'new_m', 'new_m_gla_bias_f': 'new_m', 'new_m_gla_up_b': 'new_m', 'new_m_gla_bias_b': 'new_m', 'new_m_gla_norm_w': 'new_m', 'new_m_conv_w': 'new_m', 'new_m_conv_b': 'new_m', 'new_m_dt_bias_f': 'new_m', 'new_m_dt_bias_b': 'new_m', 'new_m_a_log_f': 'new_m', 'new_m_a_log_b': 'new_m', 'new_m_d_skip': 'new_m', 'new_m_ssm_norm_w': 'new_m', 'new_m_w_pa': 'new_m', 'new_m_w_pb': 'new_m', 'new_m_w_out': 'new_m', 'new_m_norm2_w': 'new_m', 'new_m_w_gate': 'new_m', 'new_m_w_up': 'new_m', 'new_m_w_down': 'new_m', 'new_m_final_norm_w': 'new_m', 'new_v_c_ctx': 'new_v', 'new_v_w_ada': 'new_v', 'new_v_b_ada': 'new_v', 'new_v_norm1_w': 'new_v', 'new_v_w_in': 'new_v', 'new_v_gla_up_f': 'new_v', 'new_v_gla_bias_f': 'new_v', 'new_v_gla_up_b': 'new_v', 'new_v_gla_bias_b': 'new_v', 'new_v_gla_norm_w': 'new_v', 'new_v_conv_w': 'new_v', 'new_v_conv_b': 'new_v', 'new_v_dt_bias_f': 'new_v', 'new_v_dt_bias_b': 'new_v', 'new_v_a_log_f': 'new_v', 'new_v_a_log_b': 'new_v', 'new_v_d_skip': 'new_v', 'new_v_ssm_norm_w': 'new_v', 'new_v_w_pa': 'new_v', 'new_v_w_pb': 'new_v', 'new_v_w_out': 'new_v', 'new_v_norm2_w': 'new_v', 'new_v_w_gate': 'new_v', 'new_v_w_up': 'new_v', 'new_v_w_down': 'new_v', 'new_v_final_norm_w': 'new_v'}


def _forward(args):
    return _fwd_reference(*[args[k] for k in FWD_PARAMS])


def _output_shape():
    def fwd():
        inp = _fwd_setup_inputs(0)
        return _fwd_reference(*[inp[k] for k in FWD_PARAMS])
    out = _jax.eval_shape(fwd)
    return out.shape, out.dtype

N_MICROBATCH = 1
ADAM_LR = 0.001
ADAM_B1 = 0.9
ADAM_B2 = 0.999
ADAM_EPS = 1e-08
ADAM_WD = 0.01
ADAM_STEP = 10
PER_EXAMPLE_BATCH_AXIS = {'x': 0, 'c': 0, 'ctx': 0, 'loss_target': 0}
SHARED_INPUTS = []
_WEIGHT_DTYPES = {'c_ctx': _jnp.float32, 'w_ada': _jnp.float32, 'b_ada': _jnp.float32, 'norm1_w': _jnp.float32, 'w_in': _jnp.float32, 'gla_up_f': _jnp.float32, 'gla_bias_f': _jnp.float32, 'gla_up_b': _jnp.float32, 'gla_bias_b': _jnp.float32, 'gla_norm_w': _jnp.float32, 'conv_w': _jnp.float32, 'conv_b': _jnp.float32, 'dt_bias_f': _jnp.float32, 'dt_bias_b': _jnp.float32, 'a_log_f': _jnp.float32, 'a_log_b': _jnp.float32, 'd_skip': _jnp.float32, 'ssm_norm_w': _jnp.float32, 'w_pa': _jnp.float32, 'w_pb': _jnp.float32, 'w_out': _jnp.float32, 'norm2_w': _jnp.float32, 'w_gate': _jnp.float32, 'w_up': _jnp.float32, 'w_down': _jnp.float32, 'final_norm_w': _jnp.float32}
MOMENT_SCALE = {'c_ctx': 9.816526e-03, 'w_ada': 8.680668e-02, 'b_ada': 1.446365e-01, 'norm1_w': 1.013195e-01, 'w_in': 3.661008e-02, 'gla_up_f': 8.495184e-03, 'gla_bias_f': 1.925592e-02, 'gla_up_b': 7.824105e-03, 'gla_bias_b': 1.795626e-02, 'gla_norm_w': 8.554514e-02, 'conv_w': 3.233172e-02, 'conv_b': 4.178182e-02, 'dt_bias_f': 6.692011e-02, 'dt_bias_b': 8.408729e-02, 'a_log_f': 9.687870e-02, 'a_log_b': 1.044248e-01, 'd_skip': 1.065366e-01, 'ssm_norm_w': 3.769245e-02, 'w_pa': 3.696477e-02, 'w_pb': 5.354105e-02, 'w_out': 6.493367e-02, 'norm2_w': 9.813659e-02, 'w_gate': 4.490242e-02, 'w_up': 4.344391e-02, 'w_down': 7.194714e-02, 'final_norm_w': 6.401328e+01}


def _to_microbatches(a, axis):
    t = _jnp.moveaxis(a, axis, 0)
    t = t.reshape((N_MICROBATCH, t.shape[0] // N_MICROBATCH) + t.shape[1:])
    return _jnp.moveaxis(t, 1, axis + 1)


def setup_inputs(seed: int = 0) -> dict:
    inp = _fwd_setup_inputs(seed)
    key = _jax.random.fold_in(_jax.random.key(seed), 7919)
    shape, _ = _output_shape()
    out = dict(inp)
    out["loss_target"] = _jax.random.normal(_jax.random.fold_in(key, 0), shape, _jnp.float32)
    for i, name in enumerate(TWIN_WEIGHTS):
        w = inp[name].astype(_jnp.float32)
        if MOMENT_SCALE is None:
            s = _jnp.sqrt(_jnp.mean(_jnp.square(w)) + 1e-30)
        else:
            s = MOMENT_SCALE[name]
        km, kv = _jax.random.split(_jax.random.fold_in(key, i + 1))
        out[name] = w
        out["m_" + name] = s * _jax.random.normal(km, w.shape, _jnp.float32)
        out["v_" + name] = (s * s) * _jax.random.uniform(kv, w.shape, _jnp.float32, 0.5, 1.5)
    if N_MICROBATCH > 1:
        for name, axis in PER_EXAMPLE_BATCH_AXIS.items():
            out[name] = _to_microbatches(out[name], axis)
    return {'x': out['x'], 'c': out['c'], 'ctx': out['ctx'], 'c_ctx': out['c_ctx'], 'w_ada': out['w_ada'], 'b_ada': out['b_ada'], 'norm1_w': out['norm1_w'], 'w_in': out['w_in'], 'gla_up_f': out['gla_up_f'], 'gla_bias_f': out['gla_bias_f'], 'gla_up_b': out['gla_up_b'], 'gla_bias_b': out['gla_bias_b'], 'gla_norm_w': out['gla_norm_w'], 'conv_w': out['conv_w'], 'conv_b': out['conv_b'], 'dt_bias_f': out['dt_bias_f'], 'dt_bias_b': out['dt_bias_b'], 'a_log_f': out['a_log_f'], 'a_log_b': out['a_log_b'], 'd_skip': out['d_skip'], 'ssm_norm_w': out['ssm_norm_w'], 'w_pa': out['w_pa'], 'w_pb': out['w_pb'], 'w_out': out['w_out'], 'norm2_w': out['norm2_w'], 'w_gate': out['w_gate'], 'w_up': out['w_up'], 'w_down': out['w_down'], 'final_norm_w': out['final_norm_w'], 'loss_target': out['loss_target'], 'm_c_ctx': out['m_c_ctx'], 'm_w_ada': out['m_w_ada'], 'm_b_ada': out['m_b_ada'], 'm_norm1_w': out['m_norm1_w'], 'm_w_in': out['m_w_in'], 'm_gla_up_f': out['m_gla_up_f'], 'm_gla_bias_f': out['m_gla_bias_f'], 'm_gla_up_b': out['m_gla_up_b'], 'm_gla_bias_b': out['m_gla_bias_b'], 'm_gla_norm_w': out['m_gla_norm_w'], 'm_conv_w': out['m_conv_w'], 'm_conv_b': out['m_conv_b'], 'm_dt_bias_f': out['m_dt_bias_f'], 'm_dt_bias_b': out['m_dt_bias_b'], 'm_a_log_f': out['m_a_log_f'], 'm_a_log_b': out['m_a_log_b'], 'm_d_skip': out['m_d_skip'], 'm_ssm_norm_w': out['m_ssm_norm_w'], 'm_w_pa': out['m_w_pa'], 'm_w_pb': out['m_w_pb'], 'm_w_out': out['m_w_out'], 'm_norm2_w': out['m_norm2_w'], 'm_w_gate': out['m_w_gate'], 'm_w_up': out['m_w_up'], 'm_w_down': out['m_w_down'], 'm_final_norm_w': out['m_final_norm_w'], 'v_c_ctx': out['v_c_ctx'], 'v_w_ada': out['v_w_ada'], 'v_b_ada': out['v_b_ada'], 'v_norm1_w': out['v_norm1_w'], 'v_w_in': out['v_w_in'], 'v_gla_up_f': out['v_gla_up_f'], 'v_gla_bias_f': out['v_gla_bias_f'], 'v_gla_up_b': out['v_gla_up_b'], 'v_gla_bias_b': out['v_gla_bias_b'], 'v_gla_norm_w': out['v_gla_norm_w'], 'v_conv_w': out['v_conv_w'], 'v_conv_b': out['v_conv_b'], 'v_dt_bias_f': out['v_dt_bias_f'], 'v_dt_bias_b': out['v_dt_bias_b'], 'v_a_log_f': out['v_a_log_f'], 'v_a_log_b': out['v_a_log_b'], 'v_d_skip': out['v_d_skip'], 'v_ssm_norm_w': out['v_ssm_norm_w'], 'v_w_pa': out['v_w_pa'], 'v_w_pb': out['v_w_pb'], 'v_w_out': out['v_w_out'], 'v_norm2_w': out['v_norm2_w'], 'v_w_gate': out['v_w_gate'], 'v_w_up': out['v_w_up'], 'v_w_down': out['v_w_down'], 'v_final_norm_w': out['v_final_norm_w']}


def _loss(weights, diff, rest, loss_target):
    with _jax.named_scope("forward"):
        args = {**rest, TWIN_DIFF_INPUT: diff, **{k: w.astype(_WEIGHT_DTYPES[k]) for k, w in weights.items()}}
        y = _forward(args)
    with _jax.named_scope("loss_head"):
        err = _jnp.square(y.astype(_jnp.float32) - loss_target)
        return 0.5 * _jnp.sum(_jnp.mean(err, axis=-1)) if err.ndim else 0.5 * err


def _adamw(w, g, m, v):
    m = ADAM_B1 * m + (1.0 - ADAM_B1) * g
    v = ADAM_B2 * v + (1.0 - ADAM_B2) * _jnp.square(g)
    m_hat = m / (1.0 - ADAM_B1 ** ADAM_STEP)
    v_hat = v / (1.0 - ADAM_B2 ** ADAM_STEP)
    delta = -ADAM_LR * (m_hat / (_jnp.sqrt(v_hat) + ADAM_EPS) + ADAM_WD * w)
    return delta, m, v


def reference(x, c, ctx, c_ctx, w_ada, b_ada, norm1_w, w_in, gla_up_f, gla_bias_f, gla_up_b, gla_bias_b, gla_norm_w, conv_w, conv_b, dt_bias_f, dt_bias_b, a_log_f, a_log_b, d_skip, ssm_norm_w, w_pa, w_pb, w_out, norm2_w, w_gate, w_up, w_down, final_norm_w, loss_target, m_c_ctx, m_w_ada, m_b_ada, m_norm1_w, m_w_in, m_gla_up_f, m_gla_bias_f, m_gla_up_b, m_gla_bias_b, m_gla_norm_w, m_conv_w, m_conv_b, m_dt_bias_f, m_dt_bias_b, m_a_log_f, m_a_log_b, m_d_skip, m_ssm_norm_w, m_w_pa, m_w_pb, m_w_out, m_norm2_w, m_w_gate, m_w_up, m_w_down, m_final_norm_w, v_c_ctx, v_w_ada, v_b_ada, v_norm1_w, v_w_in, v_gla_up_f, v_gla_bias_f, v_gla_up_b, v_gla_bias_b, v_gla_norm_w, v_conv_w, v_conv_b, v_dt_bias_f, v_dt_bias_b, v_a_log_f, v_a_log_b, v_d_skip, v_ssm_norm_w, v_w_pa, v_w_pb, v_w_out, v_norm2_w, v_w_gate, v_w_up, v_w_down, v_final_norm_w):
    given = dict(x=x, c=c, ctx=ctx, c_ctx=c_ctx, w_ada=w_ada, b_ada=b_ada, norm1_w=norm1_w, w_in=w_in, gla_up_f=gla_up_f, gla_bias_f=gla_bias_f, gla_up_b=gla_up_b, gla_bias_b=gla_bias_b, gla_norm_w=gla_norm_w, conv_w=conv_w, conv_b=conv_b, dt_bias_f=dt_bias_f, dt_bias_b=dt_bias_b, a_log_f=a_log_f, a_log_b=a_log_b, d_skip=d_skip, ssm_norm_w=ssm_norm_w, w_pa=w_pa, w_pb=w_pb, w_out=w_out, norm2_w=norm2_w, w_gate=w_gate, w_up=w_up, w_down=w_down, final_norm_w=final_norm_w, loss_target=loss_target, m_c_ctx=m_c_ctx, m_w_ada=m_w_ada, m_b_ada=m_b_ada, m_norm1_w=m_norm1_w, m_w_in=m_w_in, m_gla_up_f=m_gla_up_f, m_gla_bias_f=m_gla_bias_f, m_gla_up_b=m_gla_up_b, m_gla_bias_b=m_gla_bias_b, m_gla_norm_w=m_gla_norm_w, m_conv_w=m_conv_w, m_conv_b=m_conv_b, m_dt_bias_f=m_dt_bias_f, m_dt_bias_b=m_dt_bias_b, m_a_log_f=m_a_log_f, m_a_log_b=m_a_log_b, m_d_skip=m_d_skip, m_ssm_norm_w=m_ssm_norm_w, m_w_pa=m_w_pa, m_w_pb=m_w_pb, m_w_out=m_w_out, m_norm2_w=m_norm2_w, m_w_gate=m_w_gate, m_w_up=m_w_up, m_w_down=m_w_down, m_final_norm_w=m_final_norm_w, v_c_ctx=v_c_ctx, v_w_ada=v_w_ada, v_b_ada=v_b_ada, v_norm1_w=v_norm1_w, v_w_in=v_w_in, v_gla_up_f=v_gla_up_f, v_gla_bias_f=v_gla_bias_f, v_gla_up_b=v_gla_up_b, v_gla_bias_b=v_gla_bias_b, v_gla_norm_w=v_gla_norm_w, v_conv_w=v_conv_w, v_conv_b=v_conv_b, v_dt_bias_f=v_dt_bias_f, v_dt_bias_b=v_dt_bias_b, v_a_log_f=v_a_log_f, v_a_log_b=v_a_log_b, v_d_skip=v_d_skip, v_ssm_norm_w=v_ssm_norm_w, v_w_pa=v_w_pa, v_w_pb=v_w_pb, v_w_out=v_w_out, v_norm2_w=v_norm2_w, v_w_gate=v_w_gate, v_w_up=v_w_up, v_w_down=v_w_down, v_final_norm_w=v_final_norm_w)
    weights = {n: given[n] for n in TWIN_WEIGHTS}
    shared = {n: given[n] for n in SHARED_INPUTS}
    per_example = {n: given[n] for n in ['x', 'c', 'ctx']}
    grad_fn = _jax.value_and_grad(_loss, argnums=(0, 1))

    def one_microbatch(ex, loss_target):
        ex = dict(ex)
        diff = ex.pop(TWIN_DIFF_INPUT)
        return grad_fn(weights, diff, {**shared, **ex}, loss_target)

    if N_MICROBATCH == 1:
        loss, (grad_w, grad_x) = one_microbatch(per_example, given["loss_target"])
    else:
        def body(carry, xs):
            loss_sum, grad_sum = carry
            l_k, (gw_k, gx_k) = one_microbatch(xs[0], xs[1])
            with _jax.named_scope("update"):
                return (loss_sum + l_k, _jax.tree.map(_jnp.add, grad_sum, gw_k)), gx_k

        init = (_jnp.zeros((), _jnp.float32), _jax.tree.map(_jnp.zeros_like, weights))
        (loss, grad_w), grad_x = _jax.lax.scan(body, init, (per_example, given["loss_target"]))
    with _jax.named_scope("update"):
        delta_w, new_m, new_v = {}, {}, {}
        for n in TWIN_WEIGHTS:
            delta_w[n], new_m[n], new_v[n] = _adamw(weights[n], grad_w[n], given["m_" + n], given["v_" + n])
    return (loss, grad_x, *[grad_w[n] for n in TWIN_WEIGHTS], *[delta_w[n] for n in TWIN_WEIGHTS],
            *[new_m[n] for n in TWIN_WEIGHTS], *[new_v[n] for n in TWIN_WEIGHTS])
```

```python
import functools

import jax
import jax.numpy as jnp
import numpy as np
from jax import lax
from jax.experimental import pallas as pl
from jax.experimental.pallas import tpu as pltpu

F32 = jnp.float32
BF16 = jnp.bfloat16
HI = lax.Precision.HIGHEST

N_DEV = 8
D = 1024
EPS = 1e-6
GRID_W = 64
GLA_H, GLA_DK, GLA_DV = 4, 128, 256
GLA_C = 64
GLA_TAU = 16.0
SSM_G, SSM_HPG, SSM_P, SSM_N = 4, 8, 64, 128
SSM_C = 128
SSM_INNER = 2048
D_FF = 2816
D_IN = 10336
D_INP = 10368
C_Q, C_K, C_V, C_R, C_U, C_Z, C_GA, C_GB, C_S = 0, 512, 1024, 2048, 3072, 6144, 8192, 9216, 10240
TM = 256
TS = 128

ADAM_LR, ADAM_B1, ADAM_B2, ADAM_EPS, ADAM_WD, ADAM_STEP = 0.001, 0.9, 0.999, 1e-08, 0.01, 10
VMEM_LIMIT = 56 << 20


def _cparams(sem):
    return pltpu.CompilerParams(dimension_semantics=sem, vmem_limit_bytes=VMEM_LIMIT)


def _pick(n, target, mult):
    best = None
    for t in range(mult, min(n, target) + 1, mult):
        if n % t == 0:
            best = t
    return best if best is not None else n


def _dot_impl(a, b, ta, tb):
    dims = (((0 if ta else 1,), (1 if tb else 0,)), ((), ()))
    return lax.dot_general(a.astype(BF16), b.astype(BF16), dims, preferred_element_type=F32)


@functools.partial(jax.custom_vjp, nondiff_argnums=(2, 3))
def _bdot(a, b, ta=False, tb=False):
    return _dot_impl(a, b, ta, tb)


def _bdot_fwd(a, b, ta, tb):
    return _dot_impl(a, b, ta, tb), (a, b)


def _bdot_bwd(ta, tb, res, g):
    a, b = res
    if not ta and not tb:
        return _dot_impl(g, b, False, True), _dot_impl(a, g, True, False)
    if not ta and tb:
        return _dot_impl(g, b, False, False), _dot_impl(g, a, True, False)
    if ta and not tb:
        return _dot_impl(b, g, False, True), _dot_impl(a, g, False, False)
    raise NotImplementedError


_bdot.defvjp(_bdot_fwd, _bdot_bwd)


def _hdot(a, b):
    return jnp.dot(a, b, precision=HI, preferred_element_type=F32)


def _shift_impl(u, pos, per, s):
    n = u.shape[0]
    rolled = u if s == 0 else pltpu.roll(u, (-s) % n, 0)
    ok = (pos + s >= 0) & (pos + s < per)
    return jnp.where(ok, rolled, 0.0)


@functools.partial(jax.custom_vjp, nondiff_argnums=(3,))
def _shift(u, pos, per, s):
    return _shift_impl(u, pos, per, s)


def _shift_fwd(u, pos, per, s):
    return _shift_impl(u, pos, per, s), (pos, per)


def _shift_bwd(s, res, g):
    pos, per = res
    return _shift_impl(g, pos, per, -s), None, None


_shift.defvjp(_shift_fwd, _shift_bwd)


def _rms(x, w):
    return x * lax.rsqrt(jnp.mean(x * x, axis=-1, keepdims=True) + EPS) * w


def _silu(x):
    return x * jax.nn.sigmoid(x)


def _softplus(x):
    return jnp.maximum(x, 0.0) + jnp.log(1.0 + jnp.exp(-jnp.abs(x)))


def _logsig(x):
    return jnp.minimum(x, 0.0) - jnp.log(1.0 + jnp.exp(-jnp.abs(x)))


def _tri(n, rev):
    t = lax.broadcasted_iota(jnp.int32, (n, n), 0)
    s = lax.broadcasted_iota(jnp.int32, (n, n), 1)
    return (s >= t) if rev else (t >= s)


def _head_expand(first_lane):
    l = lax.broadcasted_iota(jnp.int32, (128, SSM_INNER), 0)
    c = lax.broadcasted_iota(jnp.int32, (128, SSM_INNER), 1)
    return (l == first_lane + lax.shift_right_logical(c, 6)).astype(F32)


def _mm(a, b, *, trans_b, out_dtype, name, tm_t=768, tn_t=1024, tk_t=2816):
    M, K = a.shape
    N = b.shape[0] if trans_b else b.shape[1]
    tm, tn, tk = _pick(M, tm_t, 8), _pick(N, tn_t, 128), _pick(K, tk_t, 128)
    nk = K // tk
    dims = (((1,), (1,)), ((), ())) if trans_b else (((1,), (0,)), ((), ()))

    def body(a_ref, b_ref, o_ref, acc_ref):
        k = pl.program_id(2)
        p = lax.dot_general(a_ref[...], b_ref[...], dims, preferred_element_type=F32)

        @pl.when(k == 0)
        def _():
            acc_ref[...] = p

        @pl.when(k > 0)
        def _():
            acc_ref[...] += p

        @pl.when(k == nk - 1)
        def _():
            o_ref[...] = acc_ref[...].astype(out_dtype)

    b_spec = pl.BlockSpec((tn, tk), lambda j, i, k: (j, k)) if trans_b else pl.BlockSpec((tk, tn), lambda j, i, k: (k, j))
    return pl.pallas_call(
        body, grid=(N // tn, M // tm, nk),
        in_specs=[pl.BlockSpec((tm, tk), lambda j, i, k: (i, k)), b_spec],
        out_specs=pl.BlockSpec((tm, tn), lambda j, i, k: (i, j)),
        out_shape=jax.ShapeDtypeStruct((M, N), out_dtype),
        scratch_shapes=[pltpu.VMEM((tm, tn), F32)],
        compiler_params=_cparams(("parallel", "parallel", "arbitrary")), name=name,
    )(a, b)


def _mm_tn(a, b, *, name, tm_t=768, tr_t=1024, tn_t=1408):
    M, R = a.shape
    N = b.shape[1]
    tm, tr, tn = _pick(M, tm_t, 8), _pick(R, tr_t, 128), _pick(N, tn_t, 128)

    def body(a_ref, b_ref, o_ref):
        m = pl.program_id(2)
        p = lax.dot_general(a_ref[...], b_ref[...], (((0,), (0,)), ((), ())), preferred_element_type=F32)

        @pl.when(m == 0)
        def _():
            o_ref[...] = p

        @pl.when(m > 0)
        def _():
            o_ref[...] += p

    return pl.pallas_call(
        body, grid=(R // tr, N // tn, M // tm),
        in_specs=[pl.BlockSpec((tm, tr), lambda r, j, m: (m, r)), pl.BlockSpec((tm, tn), lambda r, j, m: (m, j))],
        out_specs=pl.BlockSpec((tr, tn), lambda r, j, m: (r, j)),
        out_shape=jax.ShapeDtypeStruct((R, N), F32),
        compiler_params=_cparams(("parallel", "parallel", "arbitrary")), name=name,
    )(a, b)


def _col(arr, tm, width=None, col=0):
    width = arr.shape[1] if width is None else width
    return (arr, (tm, width), lambda i: (i, col))


def _lat(arr, tm, nct):
    return (arr, (tm, arr.shape[1]), lambda i: (jnp.maximum(i - nct, 0), 0))


def _ctx(arr, tm, nct):
    return (arr, (tm, arr.shape[1]), lambda i: (jnp.minimum(i, nct - 1), 0))


def _whole(p):
    return pl.BlockSpec(p.shape, lambda i, nd=p.ndim: (0,) * nd)


def _stage_fwd(f, n_tiles, ins, params, outs, name):
    ni, npar = len(ins), len(params)

    def body(*refs):
        i = pl.program_id(0)
        xs = [r[...].astype(F32) for r in refs[:ni]]
        ps = [r[...] for r in refs[ni:ni + npar]]
        for r, v in zip(refs[ni + npar:], f(i, xs, ps, False)):
            r[...] = v.astype(r.dtype)

    return pl.pallas_call(
        body, grid=(n_tiles,),
        in_specs=[pl.BlockSpec(bs, fn) for _, bs, fn in ins] + [_whole(p) for p in params],
        out_specs=[pl.BlockSpec((tm, c), lambda i: (i, 0)) for _, c, _, tm in outs],
        out_shape=[jax.ShapeDtypeStruct((r, c), dt) for r, c, dt, _ in outs],
        compiler_params=_cparams(("parallel",)), name=name,
    )(*[a for a, _, _ in ins], *params)


def _stage_bwd(f, n_tiles, ins, params, cts, ct_fn, dins, name):
    ni, npar, nc = len(ins), len(params), len(cts)
    want = [k for k, d in enumerate(dins) if d is not None]
    extras = [dins[k][1] for k in want if dins[k][1] is not None]

    def body(*refs):
        i = pl.program_id(0)
        xs = [r[...].astype(F32) for r in refs[:ni]]
        ps = [r[...] for r in refs[ni:ni + npar]]
        ct_tiles = [r[...].astype(F32) for r in refs[ni + npar:ni + npar + nc]]
        ex_refs = list(refs[ni + npar + nc:ni + npar + nc + len(extras)])
        out_refs = refs[ni + npar + nc + len(extras):]
        _, vjp = jax.vjp(lambda xs_, ps_: tuple(f(i, xs_, ps_, True)), xs, ps)
        dxs, dps = vjp(tuple(ct_fn(i, ct_tiles)))
        for n, k in enumerate(want):
            v = dxs[k]
            if dins[k][1] is not None:
                v = v + ex_refs.pop(0)[...].astype(F32)
            out_refs[n][...] = v.astype(out_refs[n].dtype)
        for r, v in zip(out_refs[len(want):], dps):
            @pl.when(i == 0)
            def _(r=r, v=v):
                r[...] = v

            @pl.when(i > 0)
            def _(r=r, v=v):
                r[...] += v

    return pl.pallas_call(
        body, grid=(n_tiles,),
        in_specs=([pl.BlockSpec(bs, fn) for _, bs, fn in ins] + [_whole(p) for p in params]
                  + [pl.BlockSpec(bs, fn) for _, bs, fn in cts] + [pl.BlockSpec(bs, fn) for _, bs, fn in extras]),
        out_specs=([pl.BlockSpec(ins[k][1], lambda i, fn=ins[k][2]: (fn(i)[0], 0)) for k in want]
                   + [_whole(p) for p in params]),
        out_shape=([jax.ShapeDtypeStruct((ins[k][0].shape[0], ins[k][1][1]), dins[k][0]) for k in want]
                   + [jax.ShapeDtypeStruct(p.shape, F32) for p in params]),
        compiler_params=_cparams(("arbitrary",)), name=name,
    )(*[a for a, _, _ in ins], *params, *[a for a, _, _ in cts], *[a for a, _, _ in extras])


def _mod_rows(i, nct, ada, lo):
    return jnp.where(i < nct, ada[1:2, lo:lo + D], ada[0:1, lo:lo + D])


def _f_norm1(nct):
    def f(i, xs, ps, diff):
        ctx, x = xs
        nw, ada = ps
        xt = jnp.where(i < nct, ctx, x)
        return (_rms(xt, nw) * (1.0 + _mod_rows(i, nct, ada, D)) + _mod_rows(i, nct, ada, 0),)
    return f


def _f_gates_dt(i, xs, ps, diff):
    small, xc = xs
    up_f, up_b, gb_f, gb_b, dtb, alog = ps
    dot = _bdot if diff else _dot_impl_nn
    lg_f = _logsig(dot(small, up_f) + gb_f) * (1.0 / GLA_TAU)
    lg_b = _logsig(dot(small, up_b) + gb_b) * (1.0 / GLA_TAU)
    dtp = _softplus(small + dtb)
    aa = -jnp.exp(alog) * dtp
    e_f, e_b = _head_expand(32), _head_expand(64)
    return (lg_f, lg_b, xc * _hdot(dtp, e_f), xc * _hdot(dtp, e_b), _hdot(aa, e_f), _hdot(aa, e_b))


def _dot_impl_nn(a, b):
    return _dot_impl(a, b, False, False)


def _f_conv(nct, tc):
    def f(i, xs, ps, diff):
        (u,) = xs
        cw, cb = ps
        n = u.shape[0]
        t = lax.broadcasted_iota(jnp.int32, (n, 1), 0)
        per = jnp.where(i < nct, tc, GRID_W)
        pos = jnp.bitwise_and(t, per - 1)
        sh = _shift if diff else _shift_impl
        acc = cb + sh(u, pos, per, -2) * cw[0:1]
        for j in range(1, 4):
            acc = acc + sh(u, pos, per, j - 2) * cw[j:j + 1]
        return (_silu(acc),)
    return f


def _f_gla_post(i, xs, ps, diff):
    ogf, ogb, r = xs
    (gw,) = ps
    o = ogf + ogb
    parts = [_rms(o[:, h * GLA_DV:(h + 1) * GLA_DV], gw) for h in range(GLA_H)]
    return (jnp.concatenate(parts, axis=1) * _silu(r),)


def _f_ssd_post(i, xs, ps, diff):
    yf, yb, xc, z = xs
    dsk8, nw = ps
    dsk = _hdot(dsk8, _head_expand(0))[0:1]
    y = (yf + yb + dsk * xc) * _silu(z)
    w = SSM_INNER // SSM_G
    parts = [_rms(y[:, g * w:(g + 1) * w], nw[:, g * w:(g + 1) * w]) for g in range(SSM_G)]
    return (jnp.concatenate(parts, axis=1),)


def _f_merge(i, xs, ps, diff):
    ga, gb, ya, yb = xs
    return (jax.nn.sigmoid(ga) * ya + jax.nn.sigmoid(gb) * yb,)


def _f_res1(nct):
    def f(i, xs, ps, diff):
        ctx, x, mix = xs
        ada, nw = ps
        h2 = jnp.where(i < nct, ctx, x) + _mod_rows(i, nct, ada, 2 * D) * mix
        return (h2, _rms(h2, nw) * (1.0 + _mod_rows(i, nct, ada, 4 * D)) + _mod_rows(i, nct, ada, 3 * D))
    return f


def _f_swiglu(i, xs, ps, diff):
    (gu,) = xs
    return (_silu(gu[:, :D_FF]) * gu[:, D_FF:],)


def _loss_head(h2, dn, tgt, ada, fw, nct):
    A = h2.shape[0]
    n_tiles = A // TM

    def tile_loss(i, h2t, dnt, tg, ada_, fw_):
        h3 = h2t + ada_[0:1, 5 * D:6 * D] * dnt
        err = _rms(h3, fw_) - tg
        row = 0.5 * jnp.mean(err * err, axis=-1, keepdims=True)
        return jnp.sum(row, axis=0, keepdims=True) * jnp.where(i < nct, 0.0, 1.0)

    def body(h2_ref, dn_ref, tg_ref, ada_ref, fw_ref, loss_ref, dh_ref, ddn_ref, dada_ref, dfw_ref):
        i = pl.program_id(0)
        val, vjp = jax.vjp(functools.partial(tile_loss, i), h2_ref[...], dn_ref[...], tg_ref[...], ada_ref[...], fw_ref[...])
        dh, ddn, _, dada, dfw = vjp(jnp.ones((1, 1), F32))
        dh_ref[...] = dh
        ddn_ref[...] = ddn.astype(BF16)
        lv = jnp.broadcast_to(val, loss_ref.shape)
        for r, v in ((loss_ref, lv), (dada_ref, dada), (dfw_ref, dfw)):
            @pl.when(i == 0)
            def _(r=r, v=v):
                r[...] = v

            @pl.when(i > 0)
            def _(r=r, v=v):
                r[...] += v

    row = lambda i: (i, 0)
    return pl.pallas_call(
        body, grid=(n_tiles,),
        in_specs=[pl.BlockSpec((TM, D), row), pl.BlockSpec((TM, D), row),
                  pl.BlockSpec((TM, D), lambda i: (jnp.maximum(i - nct, 0), 0)), _whole(ada), _whole(fw)],
        out_specs=[pl.BlockSpec((8, 128), lambda i: (0, 0)), pl.BlockSpec((TM, D), row), pl.BlockSpec((TM, D), row),
                   _whole(ada), _whole(fw)],
        out_shape=[jax.ShapeDtypeStruct((8, 128), F32), jax.ShapeDtypeStruct((A, D), F32),
                   jax.ShapeDtypeStruct((A, D), BF16), jax.ShapeDtypeStruct(ada.shape, F32),
                   jax.ShapeDtypeStruct(fw.shape, F32)],
        compiler_params=_cparams(("arbitrary",)), name="loss_head",
    )(h2, dn, tgt, ada, fw)


def _chunk_of(step, n_chunks, n_ctx_chunks, rev):
    if not rev:
        return step
    return jnp.where(step < n_ctx_chunks, n_ctx_chunks - 1 - step, n_chunks - 1 - (step - n_ctx_chunks))


def _gla_step(st, q, k, v, g, rev):
    tri = _tri(GLA_C, rev).astype(F32)
    b = _hdot(tri, g)
    tot = jnp.sum(g, axis=0, keepdims=True)
    mid = b[GLA_C // 2:GLA_C // 2 + 1]
    qe = q * (GLA_DK ** -0.5) * jnp.exp(b - mid)
    ke = k * jnp.exp(mid - b)
    att = _bdot(qe, ke, False, True) * tri
    qs = q * (GLA_DK ** -0.5) * jnp.exp(b)
    o = _bdot(att, v) + _bdot(qs, st, False, True)
    kd = k * jnp.exp(tot - b)
    st_new = st * jnp.exp(tot) + _bdot(v, kd, True, False)
    return st_new, o


def _gla_specs(proj, lg, A, ncc, rev):
    nc = A // GLA_C
    ch = lambda s: _chunk_of(s, nc, ncc, rev)
    return nc, ch, [
        pl.BlockSpec((GLA_C, GLA_DK), lambda h, s: (ch(s), C_Q // GLA_DK + h)),
        pl.BlockSpec((GLA_C, GLA_DK), lambda h, s: (ch(s), C_K // GLA_DK + h)),
        pl.BlockSpec((GLA_C, GLA_DV), lambda h, s: (ch(s), C_V // GLA_DV + h)),
        pl.BlockSpec((GLA_C, GLA_DK), lambda h, s: (ch(s), h)),
    ]


def _gla_fwd(proj, lg, ncc, rev, name):
    A = proj.shape[0]
    nc, ch, in_specs = _gla_specs(proj, lg, A, ncc, rev)

    def body(q_ref, k_ref, v_ref, g_ref, o_ref, ss_ref, st_ref):
        @pl.when(pl.program_id(1) == 0)
        def _():
            st_ref[...] = jnp.zeros_like(st_ref)

        st = st_ref[...]
        ss_ref[0, 0] = st
        st_new, o = _gla_step(st, q_ref[...], k_ref[...], v_ref[...], g_ref[...], rev)
        o_ref[...] = o
        st_ref[...] = st_new

    return pl.pallas_call(
        body, grid=(GLA_H, nc), in_specs=in_specs,
        out_specs=[pl.BlockSpec((GLA_C, GLA_DV), lambda h, s: (ch(s), h)),
                   pl.BlockSpec((1, 1, GLA_DV, GLA_DK), lambda h, s: (h, s, 0, 0))],
        out_shape=[jax.ShapeDtypeStruct((A, GLA_H * GLA_DV), F32),
                   jax.ShapeDtypeStruct((GLA_H, nc, GLA_DV, GLA_DK), F32)],
        scratch_shapes=[pltpu.VMEM((GLA_DV, GLA_DK), F32)],
        compiler_params=_cparams(("parallel", "arbitrary")), name=name,
    )(proj, proj, proj, lg)


def _gla_bwd(proj, lg, saved, d_o, prev, ncc, rev, name):
    A = proj.shape[0]
    nc = A // GLA_C
    st_of = lambda r: nc - 1 - r
    ch = lambda r: _chunk_of(st_of(r), nc, ncc, rev)
    qs = pl.BlockSpec((GLA_C, GLA_DK), lambda h, r: (ch(r), h))
    vs = pl.BlockSpec((GLA_C, GLA_DV), lambda h, r: (ch(r), h))
    n_prev = 0 if prev is None else 3

    def body(*refs):
        q_ref, k_ref, v_ref, g_ref, ss_ref, do_ref = refs[:6]
        p_refs = refs[6:6 + n_prev]
        dq_ref, dk_ref, dv_ref, dg_ref, ds_ref = refs[6 + n_prev:]

        @pl.when(pl.program_id(1) == 0)
        def _():
            ds_ref[...] = jnp.zeros_like(ds_ref)

        _, vjp = jax.vjp(functools.partial(_gla_step, rev=rev), ss_ref[0, 0], q_ref[...], k_ref[...], v_ref[...], g_ref[...])
        ds, dq, dk, dv, dg = vjp((ds_ref[...], do_ref[...]))
        if n_prev:
            dq, dk, dv = dq + p_refs[0][...], dk + p_refs[1][...], dv + p_refs[2][...]
        dq_ref[...], dk_ref[...], dv_ref[...], dg_ref[...] = dq, dk, dv, dg
        ds_ref[...] = ds

    in_specs = [
        pl.BlockSpec((GLA_C, GLA_DK), lambda h, r: (ch(r), C_Q // GLA_DK + h)),
        pl.BlockSpec((GLA_C, GLA_DK), lambda h, r: (ch(r), C_K // GLA_DK + h)),
        pl.BlockSpec((GLA_C, GLA_DV), lambda h, r: (ch(r), C_V // GLA_DV + h)),
        qs,
        pl.BlockSpec((1, 1, GLA_DV, GLA_DK), lambda h, r: (h, st_of(r), 0, 0)),
        vs,
    ] + ([qs, qs, vs] if n_prev else [])
    return pl.pallas_call(
        body, grid=(GLA_H, nc), in_specs=in_specs, out_specs=[qs, qs, vs, qs],
        out_shape=[jax.ShapeDtypeStruct((A, GLA_H * GLA_DK), F32), jax.ShapeDtypeStruct((A, GLA_H * GLA_DK), F32),
                   jax.ShapeDtypeStruct((A, GLA_H * GLA_DV), F32), jax.ShapeDtypeStruct((A, GLA_H * GLA_DK), F32)],
        scratch_shapes=[pltpu.VMEM((GLA_DV, GLA_DK), F32)],
        compiler_params=_cparams(("parallel", "arbitrary")), name=name,
    )(proj, proj, proj, lg, saved, d_o, *(prev or ()))


def _ssd_step(st, x, bm, cm, ax, rev):
    mask = _tri(SSM_C, rev)
    cum = _hdot(mask.astype(F32), ax)
    tot = jnp.sum(ax, axis=0, keepdims=True)
    cb = _bdot(cm, bm, False, True)
    ones = jnp.ones((SSM_C, SSM_P), F32)
    ys = []
    for e in range(SSM_HPG):
        ce = cum[:, e * SSM_P:(e + 1) * SSM_P]
        row = lax.dot_general(ones, ce, (((1,), (1,)), ((), ())), precision=HI, preferred_element_type=F32) * (1.0 / SSM_P)
        lm = jnp.exp(jnp.where(mask, ce[:, 0:1] - row, -jnp.inf))
        ys.append(_bdot(cb * lm, x[:, e * SSM_P:(e + 1) * SSM_P]))
    y = jnp.concatenate(ys, axis=1) + _bdot(cm, st) * jnp.exp(cum)
    st_new = st * jnp.exp(tot) + _bdot(bm, x * jnp.exp(tot - cum), True, False)
    return st_new, y


_XBC_B = SSM_INNER // SSM_N
_XBC_C = _XBC_B + SSM_G


def _ssd_fwd(xd, xbc, ax, ncc, rev, name):
    A = xd.shape[0]
    nc = A // SSM_C
    W = SSM_HPG * SSM_P
    ch = lambda s: _chunk_of(s, nc, ncc, rev)

    def body(x_ref, b_ref, c_ref, a_ref, y_ref, ss_ref, st_ref):
        @pl.when(pl.program_id(1) == 0)
        def _():
            st_ref[...] = jnp.zeros_like(st_ref)

        st = st_ref[...]
        ss_ref[0, 0] = st
        st_new, y = _ssd_step(st, x_ref[...], b_ref[...], c_ref[...], a_ref[...], rev)
        y_ref[...] = y
        st_ref[...] = st_new

    gs = pl.BlockSpec((SSM_C, W), lambda g, s: (ch(s), g))
    return pl.pallas_call(
        body, grid=(SSM_G, nc),
        in_specs=[gs, pl.BlockSpec((SSM_C, SSM_N), lambda g, s: (ch(s), _XBC_B + g)),
                  pl.BlockSpec((SSM_C, SSM_N), lambda g, s: (ch(s), _XBC_C + g)), gs],
        out_specs=[gs, pl.BlockSpec((1, 1, SSM_N, W), lambda g, s: (g, s, 0, 0))],
        out_shape=[jax.ShapeDtypeStruct((A, SSM_INNER), F32), jax.ShapeDtypeStruct((SSM_G, nc, SSM_N, W), F32)],
        scratch_shapes=[pltpu.VMEM((SSM_N, W), F32)],
        compiler_params=_cparams(("parallel", "arbitrary")), name=name,
    )(xd, xbc, xbc, ax)


def _ssd_bwd(xd, xbc, ax, saved, d_y, prev, ncc, rev, name):
    A = xd.shape[0]
    nc = A // SSM_C
    W = SSM_HPG * SSM_P
    st_of = lambda r: nc - 1 - r
    ch = lambda r: _chunk_of(st_of(r), nc, ncc, rev)
    gs = pl.BlockSpec((SSM_C, W), lambda g, r: (ch(r), g))
    ns = pl.BlockSpec((SSM_C, SSM_N), lambda g, r: (ch(r), g))
    n_prev = 0 if prev is None else 2

    def body(*refs):
        x_ref, b_ref, c_ref, a_ref, ss_ref, dy_ref = refs[:6]
        p_refs = refs[6:6 + n_prev]
        dx_ref, db_ref, dc_ref, da_ref, ds_ref = refs[6 + n_prev:]

        @pl.when(pl.program_id(1) == 0)
        def _():
            ds_ref[...] = jnp.zeros_like(ds_ref)

        _, vjp = jax.vjp(functools.partial(_ssd_step, rev=rev), ss_ref[0, 0], x_ref[...], b_ref[...], c_ref[...], a_ref[...])
        ds, dx, db, dc, da = vjp((ds_ref[...], dy_ref[...]))
        if n_prev:
            db, dc = db + p_refs[0][...], dc + p_refs[1][...]
        dx_ref[...], db_ref[...], dc_ref[...], da_ref[...] = dx, db, dc, da
        ds_ref[...] = ds

    in_specs = [gs, pl.BlockSpec((SSM_C, SSM_N), lambda g, r: (ch(r), _XBC_B + g)),
                pl.BlockSpec((SSM_C, SSM_N), lambda g, r: (ch(r), _XBC_C + g)), gs,
                pl.BlockSpec((1, 1, SSM_N, W), lambda g, r: (g, st_of(r), 0, 0)), gs] + ([ns, ns] if n_prev else [])
    return pl.pallas_call(
        body, grid=(SSM_G, nc), in_specs=in_specs, out_specs=[gs, ns, ns, gs],
        out_shape=[jax.ShapeDtypeStruct((A, SSM_INNER), F32), jax.ShapeDtypeStruct((A, SSM_G * SSM_N), F32),
                   jax.ShapeDtypeStruct((A, SSM_G * SSM_N), F32), jax.ShapeDtypeStruct((A, SSM_INNER), F32)],
        scratch_shapes=[pltpu.VMEM((SSM_N, W), F32)],
        compiler_params=_cparams(("parallel", "arbitrary")), name=name,
    )(xd, xbc, xbc, ax, saved, d_y, *(prev or ()))


def _ada_fwd(cc, w_ada, b_ada):
    n = w_ada.shape[1]
    tn = _pick(n, 1024, 128)

    def body(cc_ref, w_ref, b_ref, o_ref):
        o_ref[...] = _dot_impl(_silu(cc_ref[...]), w_ref[...], False, False) + b_ref[...]

    return pl.pallas_call(
        body, grid=(n // tn,),
        in_specs=[_whole(cc), pl.BlockSpec((D, tn), lambda j: (0, j)), pl.BlockSpec((1, tn), lambda j: (0, j))],
        out_specs=pl.BlockSpec((8, tn), lambda j: (0, j)),
        out_shape=jax.ShapeDtypeStruct((8, n), F32),
        compiler_params=_cparams(("parallel",)), name="ada_fwd",
    )(cc, w_ada, b_ada)


def _ada_bwd(cc, w_ada, d1, d2, d3):
    n = w_ada.shape[1]
    tn = _pick(n, 1024, 128)
    nt = n // tn

    def body(cc_ref, w_ref, d1_ref, d2_ref, d3_ref, dw_ref, db_ref, dcc_ref, ds_ref):
        j = pl.program_id(0)
        d = d1_ref[...] + d2_ref[...] + d3_ref[...]
        s = _silu(cc_ref[...])
        dw_ref[...] = _dot_impl(s, d, True, False)
        db_ref[...] = jnp.sum(d, axis=0, keepdims=True)
        p = _dot_impl(d, w_ref[...], False, True)

        @pl.when(j == 0)
        def _():
            ds_ref[...] = p

        @pl.when(j > 0)
        def _():
            ds_ref[...] += p

        @pl.when(j == nt - 1)
        def _():
            _, vjp = jax.vjp(_silu, cc_ref[...])
            dcc_ref[...] = vjp(ds_ref[...])[0]

    col = lambda j: (0, j)
    return pl.pallas_call(
        body, grid=(nt,),
        in_specs=[_whole(cc), pl.BlockSpec((D, tn), col), pl.BlockSpec((8, tn), col), pl.BlockSpec((8, tn), col),
                  pl.BlockSpec((8, tn), col)],
        out_specs=[pl.BlockSpec((D, tn), col), pl.BlockSpec((1, tn), col), _whole(cc)],
        out_shape=[jax.ShapeDtypeStruct((D, n), F32), jax.ShapeDtypeStruct((1, n), F32), jax.ShapeDtypeStruct((8, D), F32)],
        scratch_shapes=[pltpu.VMEM((8, D), F32)],
        compiler_params=_cparams(("arbitrary",)), name="ada_bwd",
    )(cc, w_ada, d1, d2, d3)


def _local_step(x, c, ctx, c_ctx, target, w):
    T, Tc = x.shape[0], ctx.shape[0]
    assert Tc == TM and T % TM == 0 and GRID_W == GLA_C
    A = T + Tc
    nct = Tc // TM
    n_tm, n_ts = A // TM, A // TS
    g = {}

    cc = jnp.zeros((8, D), F32).at[0].set(c[0]).at[1].set(c_ctx)
    ada = _ada_fwd(cc, w["w_ada"], w["b_ada"])

    x_in = [_ctx(ctx, TM, nct), _lat(x, TM, nct)]
    f_norm1 = _f_norm1(nct)
    p_norm1 = [w["norm1_w"], ada]
    (h1,) = _stage_fwd(f_norm1, n_tm, x_in, p_norm1, [(A, D, BF16, TM)], "norm1")
    proj = _mm(h1, w["w_in"], trans_b=False, out_dtype=F32, name="mm_in", tn_t=1152)

    p_gd = [w["up_f"], w["up_b"], w["gla_bias_f"], w["gla_bias_b"], w["dtb"], w["alog"]]
    f_conv = _f_conv(nct, Tc)
    p_conv = [w["conv_w"], w["conv_b"]]
    in_conv = [_col(proj, TM, 3072, C_U // 3072)]
    (xbc,) = _stage_fwd(f_conv, n_tm, in_conv, p_conv, [(A, 3072, F32, TM)], "conv")
    in_gd = [_col(proj, TS, 128, C_S // 128), _col(xbc, TS, SSM_INNER, 0)]
    lg_f, lg_b, xf, xb, axf, axb = _stage_fwd(
        _f_gates_dt, n_ts, in_gd, p_gd,
        [(A, 512, F32, TS), (A, 512, F32, TS)] + [(A, SSM_INNER, F32, TS)] * 4, "gates_dt")

    ncc_g, ncc_s = Tc // GLA_C, Tc // SSM_C
    ogf, sv_gf = _gla_fwd(proj, lg_f, ncc_g, False, "gla_f")
    ogb, sv_gb = _gla_fwd(proj, lg_b, ncc_g, True, "gla_b")
    ysf, sv_sf = _ssd_fwd(xf, xbc, axf, ncc_s, False, "ssd_f")
    ysb, sv_sb = _ssd_fwd(xb, xbc, axb, ncc_s, True, "ssd_b")

    in_gp = [_col(ogf, TS), _col(ogb, TS), _col(proj, TS, 1024, C_R // 1024)]
    p_gp = [w["gla_norm_w"]]
    (oa,) = _stage_fwd(_f_gla_post, n_ts, in_gp, p_gp, [(A, D, BF16, TS)], "gla_post")
    in_sp = [_col(ysf, TS), _col(ysb, TS), _col(xbc, TS, SSM_INNER, 0), _col(proj, TS, SSM_INNER, C_Z // SSM_INNER)]
    p_sp = [w["dsk8"], w["ssm_norm_w"]]
    (ob,) = _stage_fwd(_f_ssd_post, n_ts, in_sp, p_sp, [(A, SSM_INNER, BF16, TS)], "ssd_post")
    ya = _mm(oa, w["w_pa"], trans_b=False, out_dtype=F32, name="mm_pa")
    yb = _mm(ob, w["w_pb"], trans_b=False, out_dtype=F32, name="mm_pb")
    in_mg = [_col(proj, TS, 1024, C_GA // 1024), _col(proj, TS, 1024, C_GB // 1024), _col(ya, TS), _col(yb, TS)]
    (merged,) = _stage_fwd(_f_merge, n_ts, in_mg, [], [(A, D, BF16, TS)], "merge")
    mix = _mm(merged, w["w_out"], trans_b=False, out_dtype=F32, name="mm_out")

    f_res1 = _f_res1(nct)
    in_r1 = x_in + [_col(mix, TM)]
    p_r1 = [ada, w["norm2_w"]]
    h2, hm2 = _stage_fwd(f_res1, n_tm, in_r1, p_r1, [(A, D, F32, TM), (A, D, BF16, TM)], "res1")
    gu = _mm(hm2, w["w_gu"], trans_b=False, out_dtype=F32, name="mm_gu", tn_t=1408)
    in_sw = [_col(gu, TS)]
    (act,) = _stage_fwd(_f_swiglu, n_ts, in_sw, [], [(A, D_FF, BF16, TS)], "swiglu")
    dn = _mm(act, w["w_down"], trans_b=False, out_dtype=F32, name="mm_down")

    loss_blk, d_h2a, d_dn, d_ada3, g["final_norm_w"] = _loss_head(h2, dn, target, ada, w["final_norm_w"], nct)
    g["w_down"] = _mm_tn(act, d_dn, name="dw_down", tr_t=1408)
    d_act = _mm(d_dn, w["w_down"], trans_b=True, out_dtype=F32, name="dx_down", tn_t=1408)
    (d_gu,) = _stage_bwd(_f_swiglu, n_ts, in_sw, [], [_col(d_act, TS)], lambda i, t: t, [(BF16, None)], "swiglu_b")
    g["w_gu"] = _mm_tn(hm2, d_gu, name="dw_gu")
    d_hm2 = _mm(d_gu, w["w_gu"], trans_b=True, out_dtype=F32, name="dx_gu")
    d_x1, d_mix, d_ada2, g["norm2_w"] = _stage_bwd(
        f_res1, n_tm, in_r1, p_r1, [_col(d_h2a, TM), _col(d_hm2, TM)], lambda i, t: t,
        [None, (F32, None), (BF16, None)], "res1_b")
    g["w_out"] = _mm_tn(merged, d_mix, name="dw_out")
    d_merged = _mm(d_mix, w["w_out"], trans_b=True, out_dtype=F32, name="dx_out")
    d_ga, d_gb, d_ya, d_yb = _stage_bwd(
        _f_merge, n_ts, in_mg, [], [_col(d_merged, TS)], lambda i, t: t, [(BF16, None)] * 4, "merge_b")
    g["w_pa"] = _mm_tn(oa, d_ya, name="dw_pa")
    g["w_pb"] = _mm_tn(ob, d_yb, name="dw_pb")
    d_oa = _mm(d_ya, w["w_pa"], trans_b=True, out_dtype=F32, name="dx_pa")
    d_ob = _mm(d_yb, w["w_pb"], trans_b=True, out_dtype=F32, name="dx_pb")
    d_og, d_r, g["gla_norm_w"] = _stage_bwd(
        _f_gla_post, n_ts, in_gp, p_gp, [_col(d_oa, TS)], lambda i, t: t, [(F32, None), None, (BF16, None)], "gla_post_b")
    d_ys, d_xs_skip, d_z, g["dsk8"], g["ssm_norm_w"] = _stage_bwd(
        _f_ssd_post, n_ts, in_sp, p_sp, [_col(d_ob, TS)], lambda i, t: t,
        [(F32, None), None, (F32, None), (BF16, None)], "ssd_post_b")

    dq, dk, dv, d_lgf = _gla_bwd(proj, lg_f, sv_gf, d_og, None, ncc_g, False, "gla_f_b")
    dq, dk, dv, d_lgb = _gla_bwd(proj, lg_b, sv_gb, d_og, (dq, dk, dv), ncc_g, True, "gla_b_b")
    d_xf, d_bm, d_cm, d_axf = _ssd_bwd(xf, xbc, axf, sv_sf, d_ys, None, ncc_s, False, "ssd_f_b")
    d_xb, d_bm, d_cm, d_axb = _ssd_bwd(xb, xbc, axb, sv_sb, d_ys, (d_bm, d_cm), ncc_s, True, "ssd_b_b")

    cts_gd = [_col(a, TS) for a in (d_lgf, d_lgb, d_xf, d_xb, d_axf, d_axb)]
    d_small, d_xs_dt, g["up_f"], g["up_b"], g["gla_bias_f"], g["gla_bias_b"], g["dtb"], g["alog"] = _stage_bwd(
        _f_gates_dt, n_ts, in_gd, p_gd, cts_gd, lambda i, t: t, [(BF16, None), (F32, None)], "gates_dt_b")
    cts_conv = [_col(d_xs_skip, TM), _col(d_xs_dt, TM), _col(d_bm, TM), _col(d_cm, TM)]
    d_u, g["conv_w"], g["conv_b"] = _stage_bwd(
        f_conv, n_tm, in_conv, p_conv, cts_conv,
        lambda i, t: [jnp.concatenate([t[0] + t[1], t[2], t[3]], axis=1)], [(BF16, None)], "conv_b")

    d_proj = jnp.concatenate([dq.astype(BF16), dk.astype(BF16), dv.astype(BF16), d_r, d_u, d_z, d_ga, d_gb, d_small], axis=1)
    g["w_in"] = _mm_tn(h1, d_proj, name="dw_in", tn_t=1152)
    d_h1 = _mm(d_proj, w["w_in"], trans_b=True, out_dtype=F32, name="dx_in", tk_t=1152)
    grad_x, g["norm1_w"], d_ada1 = _stage_bwd(
        f_norm1, n_tm, x_in, p_norm1, [_col(d_h1, TM)], lambda i, t: t,
        [None, (F32, (d_x1, (TM, D), x_in[1][2]))], "norm1_b")
    g["w_ada"], g["b_ada"], d_cc = _ada_bwd(cc, w["w_ada"], d_ada1, d_ada2, d_ada3)
    g["c_ctx"] = d_cc[1]
    return loss_blk, grad_x, g


def _exchange(x, a2a, name):
    shp = x.shape[1:] if a2a else x.shape

    def body(x_ref, o_ref, send_sems, recv_sems, local_sem):
        mx, my, mc = lax.axis_index("x"), lax.axis_index("y"), lax.axis_index("c")
        me = 4 * mx + 2 * my + mc
        copies = []
        for k in range(1, N_DEV):
            px = 1 - mx if k & 4 else mx
            py = 1 - my if k & 2 else my
            pc = 1 - mc if k & 1 else mc
            src = x_ref.at[4 * px + 2 * py + pc] if a2a else x_ref
            cp = pltpu.make_async_remote_copy(
                src_ref=src, dst_ref=o_ref.at[me], send_sem=send_sems.at[k - 1], recv_sem=recv_sems.at[k - 1],
                device_id=(px, py, pc), device_id_type=pl.DeviceIdType.MESH)
            cp.start()
            copies.append(cp)
        own = pltpu.make_async_copy(x_ref.at[me] if a2a else x_ref, o_ref.at[me], local_sem)
        own.start()
        for cp in copies:
            cp.wait()
        own.wait()

    return pl.pallas_call(
        body, out_shape=jax.ShapeDtypeStruct((N_DEV,) + tuple(shp), x.dtype),
        in_specs=[pl.BlockSpec(memory_space=pl.ANY)], out_specs=pl.BlockSpec(memory_space=pl.ANY),
        scratch_shapes=[pltpu.SemaphoreType.DMA((N_DEV - 1,)), pltpu.SemaphoreType.DMA((N_DEV - 1,)),
                        pltpu.SemaphoreType.DMA(())],
        name=name,
    )(x)


def _adamw_math(w, gr, m, v):
    m = ADAM_B1 * m + (1.0 - ADAM_B1) * gr
    v = ADAM_B2 * v + (1.0 - ADAM_B2) * (gr * gr)
    m_hat = m / np.float32(1.0 - ADAM_B1 ** ADAM_STEP)
    v_hat = v / np.float32(1.0 - ADAM_B2 ** ADAM_STEP)
    delta = -ADAM_LR * (m_hat / (jnp.sqrt(v_hat) + ADAM_EPS) + ADAM_WD * w)
    return delta, m, v


def _sum_adamw(parts, w, m, v, name):
    R, C = w.shape
    tr = _pick(R, max(8, (2 << 20) // (4 * C) // 8 * 8), 8)

    def body(p_ref, w_ref, m_ref, v_ref, g_ref, d_ref, mo_ref, vo_ref):
        gr = p_ref[0]
        for k in range(1, N_DEV):
            gr = gr + p_ref[k]
        g_ref[...] = gr
        d_ref[...], mo_ref[...], vo_ref[...] = _adamw_math(w_ref[...], gr, m_ref[...], v_ref[...])

    row = pl.BlockSpec((tr, C), lambda i: (i, 0))
    return pl.pallas_call(
        body, grid=(R // tr,),
        in_specs=[pl.BlockSpec((N_DEV, tr, C), lambda i: (0, i, 0)), row, row, row],
        out_specs=[row] * 4, out_shape=[jax.ShapeDtypeStruct((R, C), F32)] * 4,
        compiler_params=_cparams(("parallel",)), name=name,
    )(parts, w, m, v)


def _sum8(parts, name):
    _, R, C = parts.shape

    def body(p_ref, g_ref):
        gr = p_ref[0]
        for k in range(1, N_DEV):
            gr = gr + p_ref[k]
        g_ref[...] = gr

    return pl.pallas_call(body, out_shape=jax.ShapeDtypeStruct((R, C), F32), name=name)(parts)


def _adamw(w, gr, m, v, name):
    def body(w_ref, g_ref, m_ref, v_ref, d_ref, mo_ref, vo_ref):
        d_ref[...], mo_ref[...], vo_ref[...] = _adamw_math(w_ref[...], g_ref[...], m_ref[...], v_ref[...])

    return pl.pallas_call(body, out_shape=[jax.ShapeDtypeStruct(w.shape, F32)] * 3, name=name)(w, gr, m, v)


def _to_padded(w_in):
    z = jnp.zeros(w_in.shape[:-1] + (32,), w_in.dtype)
    return jnp.concatenate([w_in[..., 0:3072], w_in[..., 5152:8224], w_in[..., 3104:5152], w_in[..., 8288:10336],
                            w_in[..., 3072:3104], w_in[..., 8224:8288], z], axis=-1)


def _from_padded(p):
    return jnp.concatenate([p[..., 0:3072], p[..., 10240:10272], p[..., 6144:8192], p[..., 3072:6144],
                            p[..., 10272:10336], p[..., 8192:10240]], axis=-1)


def _unshard_cols(gathered):
    n, r, c = gathered.shape
    return jnp.transpose(gathered, (1, 0, 2)).reshape(r, n * c)


def _shard_cols(full):
    r, nc = full.shape
    return jnp.transpose(full.reshape(r, N_DEV, nc // N_DEV), (1, 0, 2))


def _lanes(vec, lo):
    return jnp.zeros((1, 128), F32).at[:, lo:lo + vec.shape[1]].set(vec)


_SMALL = (("b_ada", 6 * D), ("c_ctx", D), ("norm1_w", D), ("gla_bias_f", 512), ("gla_bias_b", 512), ("gla_norm_w", 256),
          ("conv_b", 3072), ("dt_bias_f", 32), ("dt_bias_b", 32), ("a_log_f", 32), ("a_log_b", 32), ("d_skip", 32),
          ("ssm_norm_w", 2048), ("norm2_w", D), ("final_norm_w", D))
_SHARDED_SMALL = (("gla_up_f", 16 * 512), ("gla_up_b", 16 * 512), ("conv_w", 4 * 3072))


def _pack(vals, names):
    flat = jnp.concatenate([vals[n].reshape(-1).astype(F32) for n, _ in names])
    pad = (-flat.shape[0]) % 1024
    return jnp.concatenate([flat, jnp.zeros((pad,), F32)]).reshape(-1, 128)


def _unpack(packed, names, shapes):
    flat, out, off = packed.reshape(-1), {}, 0
    for n, size in names:
        out[n] = flat[off:off + size].reshape(shapes[n])
        off += size
    return out


def kernel(x, c, ctx, c_ctx, w_ada, b_ada, norm1_w, w_in, gla_up_f, gla_bias_f, gla_up_b, gla_bias_b, gla_norm_w, conv_w, conv_b, dt_bias_f, dt_bias_b, a_log_f, a_log_b, d_skip, ssm_norm_w, w_pa, w_pb, w_out, norm2_w, w_gate, w_up, w_down, final_norm_w, loss_target, m_c_ctx, m_w_ada, m_b_ada, m_norm1_w, m_w_in, m_gla_up_f, m_gla_bias_f, m_gla_up_b, m_gla_bias_b, m_gla_norm_w, m_conv_w, m_conv_b, m_dt_bias_f, m_dt_bias_b, m_a_log_f, m_a_log_b, m_d_skip, m_ssm_norm_w, m_w_pa, m_w_pb, m_w_out, m_norm2_w, m_w_gate, m_w_up, m_w_down, m_final_norm_w, v_c_ctx, v_w_ada, v_b_ada, v_norm1_w, v_w_in, v_gla_up_f, v_gla_bias_f, v_gla_up_b, v_gla_bias_b, v_gla_norm_w, v_conv_w, v_conv_b, v_dt_bias_f, v_dt_bias_b, v_a_log_f, v_a_log_b, v_d_skip, v_ssm_norm_w, v_w_pa, v_w_pb, v_w_out, v_norm2_w, v_w_gate, v_w_up, v_w_down, v_final_norm_w):
    args = dict(locals())
    me = 4 * lax.axis_index("x") + 2 * lax.axis_index("y") + lax.axis_index("c")

    def gather16(a, name):
        return _exchange(a[0].astype(BF16), False, name)

    full = {
        "w_ada": _unshard_cols(gather16(w_ada, "ag_w_ada")),
        "w_in": _to_padded(_unshard_cols(gather16(w_in, "ag_w_in"))),
        "w_pa": gather16(w_pa, "ag_w_pa").reshape(D, D),
        "w_pb": gather16(w_pb, "ag_w_pb").reshape(SSM_INNER, D),
        "w_out": gather16(w_out, "ag_w_out").reshape(D, D),
        "w_gu": jnp.concatenate([_unshard_cols(gather16(w_gate, "ag_w_gate")), _unshard_cols(gather16(w_up, "ag_w_up"))], axis=1),
        "w_down": gather16(w_down, "ag_w_down").reshape(D_FF, D),
    }
    sm = _exchange(jnp.concatenate([gla_up_f.reshape(-1), gla_up_b.reshape(-1), conv_w.reshape(-1)]).reshape(-1, 128), False, "ag_small")
    sm = sm.reshape(N_DEV, -1)
    up_f = _unshard_cols(sm[:, 0:1024].reshape(N_DEV, 16, 64))
    up_b = _unshard_cols(sm[:, 1024:2048].reshape(N_DEV, 16, 64))
    full["conv_w"] = _unshard_cols(sm[:, 2048:3584].reshape(N_DEV, 4, 384))
    full["up_f"] = jnp.zeros((128, 512), F32).at[0:16].set(up_f)
    full["up_b"] = jnp.zeros((128, 512), F32).at[16:32].set(up_b)
    full["dtb"] = _lanes(dt_bias_f, 32) + _lanes(dt_bias_b, 64)
    full["alog"] = _lanes(a_log_f, 32) + _lanes(a_log_b, 64)
    full["dsk8"] = jnp.zeros((8, 128), F32).at[0:1, 0:32].set(d_skip)
    for n in ("b_ada", "norm1_w", "gla_bias_f", "gla_bias_b", "gla_norm_w", "conv_b", "ssm_norm_w", "norm2_w"):
        full[n] = args[n]
    full["final_norm_w"] = final_norm_w.reshape(1, D)

    loss_blk, grad_x, g = _local_step(x[0], c, ctx[0], c_ctx, loss_target[0], full)

    gs = dict(g)
    gs["dt_bias_f"], gs["dt_bias_b"] = g["dtb"][:, 32:64], g["dtb"][:, 64:96]
    gs["a_log_f"], gs["a_log_b"] = g["alog"][:, 32:64], g["alog"][:, 64:96]
    gs["d_skip"] = g["dsk8"][0:1, 0:32]
    gs["gla_up_f"], gs["gla_up_b"] = g["up_f"][0:16], g["up_b"][16:32]
    gs["loss"] = loss_blk[0:1, 0:1]
    names = _SMALL + _SHARDED_SMALL + (("loss", 1),)
    shapes = {n: (args[n].shape if n in args else (1, 1)) for n, _ in names}
    shapes.update({"gla_up_f": (16, 512), "gla_up_b": (16, 512), "conv_w": (4, 3072)})
    red = _unpack(_sum8(_exchange(_pack(gs, names), False, "ag_small_grads"), "sum_small_grads"), names, shapes)
    loss = red["loss"].reshape(())
    grads = {n: red[n] for n, _ in _SMALL}
    grads["gla_up_f"] = lax.dynamic_slice(red["gla_up_f"], (0, me * 64), (16, 64))[None]
    grads["gla_up_b"] = lax.dynamic_slice(red["gla_up_b"], (0, me * 64), (16, 64))[None]
    grads["conv_w"] = lax.dynamic_slice(red["conv_w"], (0, me * 384), (4, 384))[None]
    upd_names = tuple((n, s) for n, s in _SMALL) + (("gla_up_f", 1024), ("gla_up_b", 1024), ("conv_w", 1536))
    pk = lambda prefix, src: _pack({n: src[prefix + n] for n, _ in upd_names}, upd_names)
    d_s, m_s, v_s = _adamw(pk("", args), _pack(grads, upd_names), pk("m_", args), pk("v_", args), "adamw_small")
    upd_shapes = {n: args[n].shape for n, _ in upd_names}
    delta = _unpack(d_s, upd_names, upd_shapes)
    new_m = _unpack(m_s, upd_names, upd_shapes)
    new_v = _unpack(v_s, upd_names, upd_shapes)

    g_in = _from_padded(g["w_in"])
    big = {
        "w_ada": _shard_cols(g["w_ada"]), "w_in": _shard_cols(g_in),
        "w_pa": g["w_pa"].reshape(N_DEV, D // N_DEV, D), "w_pb": g["w_pb"].reshape(N_DEV, SSM_INNER // N_DEV, D),
        "w_out": g["w_out"].reshape(N_DEV, D // N_DEV, D),
        "w_gate": _shard_cols(g["w_gu"][:, :D_FF]), "w_up": _shard_cols(g["w_gu"][:, D_FF:]),
        "w_down": g["w_down"].reshape(N_DEV, D_FF // N_DEV, D),
    }
    for n, parts in big.items():
        got = _exchange(parts, True, "a2a_" + n)
        gr, dl, mo, vo = _sum_adamw(got, args[n][0], args["m_" + n][0], args["v_" + n][0], "adamw_" + n)
        grads[n], delta[n], new_m[n], new_v[n] = gr[None], dl[None], mo[None], vo[None]

    order = ["c_ctx", "w_ada", "b_ada", "norm1_w", "w_in", "gla_up_f", "gla_bias_f", "gla_up_b", "gla_bias_b", "gla_norm_w",
             "conv_w", "conv_b", "dt_bias_f", "dt_bias_b", "a_log_f", "a_log_b", "d_skip", "ssm_norm_w", "w_pa", "w_pb",
             "w_out", "norm2_w", "w_gate", "w_up", "w_down", "final_norm_w"]
    fix = lambda d: [d[n].reshape(args[n].shape) for n in order]
    return (loss, grad_x[None], *fix(grads), *fix(delta), *fix(new_m), *fix(new_v))
```

```python
import functools

import jax
import jax.numpy as jnp
import numpy as np
from jax import lax
from jax.experimental import pallas as pl
from jax.experimental.pallas import tpu as pltpu

F32 = jnp.float32
BF16 = jnp.bfloat16
HI = lax.Precision.HIGHEST

N_DEV = 8
D = 1024
EPS = 1e-6
GRID_W = 64
GLA_H, GLA_DK, GLA_DV = 4, 128, 256
GLA_C = 64
GLA_TAU = 16.0
SSM_G, SSM_HPG, SSM_P, SSM_N = 4, 8, 64, 128
SSM_C = 128
SSM_INNER = 2048
D_FF = 2816
D_IN = 10336
D_INP = 10368
C_Q, C_K, C_V, C_R, C_U, C_Z, C_GA, C_GB, C_S = 0, 512, 1024, 2048, 3072, 6144, 8192, 9216, 10240
TM = 256
TS = 128

ADAM_LR, ADAM_B1, ADAM_B2, ADAM_EPS, ADAM_WD, ADAM_STEP = 0.001, 0.9, 0.999, 1e-08, 0.01, 10
VMEM_LIMIT = 56 << 20


def _cparams(sem):
    return pltpu.CompilerParams(dimension_semantics=sem, vmem_limit_bytes=VMEM_LIMIT)


def _pick(n, target, mult):
    best = None
    for t in range(mult, min(n, target) + 1, mult):
        if n % t == 0:
            best = t
    return best if best is not None else n


def _dot_impl(a, b, ta, tb):
    dims = (((0 if ta else 1,), (1 if tb else 0,)), ((), ()))
    return lax.dot_general(a.astype(BF16), b.astype(BF16), dims, preferred_element_type=F32)


@functools.partial(jax.custom_vjp, nondiff_argnums=(2, 3))
def _bdot(a, b, ta=False, tb=False):
    return _dot_impl(a, b, ta, tb)


def _bdot_fwd(a, b, ta, tb):
    return _dot_impl(a, b, ta, tb), (a, b)


def _bdot_bwd(ta, tb, res, g):
    a, b = res
    if not ta and not tb:
        return _dot_impl(g, b, False, True), _dot_impl(a, g, True, False)
    if not ta and tb:
        return _dot_impl(g, b, False, False), _dot_impl(g, a, True, False)
    if ta and not tb:
        return _dot_impl(b, g, False, True), _dot_impl(a, g, False, False)
    raise NotImplementedError


_bdot.defvjp(_bdot_fwd, _bdot_bwd)


def _hdot(a, b):
    return jnp.dot(a, b, precision=HI, preferred_element_type=F32)


def _shift_impl(u, pos, per, s):
    n = u.shape[0]
    rolled = u if s == 0 else pltpu.roll(u, (-s) % n, 0)
    ok = (pos + s >= 0) & (pos + s < per)
    return jnp.where(ok, rolled, 0.0)


@functools.partial(jax.custom_vjp, nondiff_argnums=(3,))
def _shift(u, pos, per, s):
    return _shift_impl(u, pos, per, s)


def _shift_fwd(u, pos, per, s):
    return _shift_impl(u, pos, per, s), (pos, per)


def _shift_bwd(s, res, g):
    pos, per = res
    return _shift_impl(g, pos, per, -s), None, None


_shift.defvjp(_shift_fwd, _shift_bwd)


def _rms(x, w):
    return x * lax.rsqrt(jnp.mean(x * x, axis=-1, keepdims=True) + EPS) * w


def _silu(x):
    return x * jax.nn.sigmoid(x)


def _softplus(x):
    return jnp.maximum(x, 0.0) + jnp.log(1.0 + jnp.exp(-jnp.abs(x)))


def _logsig(x):
    return jnp.minimum(x, 0.0) - jnp.log(1.0 + jnp.exp(-jnp.abs(x)))


def _tri(n, rev):
    t = lax.broadcasted_iota(jnp.int32, (n, n), 0)
    s = lax.broadcasted_iota(jnp.int32, (n, n), 1)
    return (s >= t) if rev else (t >= s)


def _head_expand(first_lane):
    l = lax.broadcasted_iota(jnp.int32, (128, SSM_INNER), 0)
    c = lax.broadcasted_iota(jnp.int32, (128, SSM_INNER), 1)
    return (l == first_lane + lax.shift_right_logical(c, 6)).astype(F32)


def _mm(a, b, *, trans_b, out_dtype, name, tm_t=768, tn_t=1024, tk_t=2816):
    M, K = a.shape
    N = b.shape[0] if trans_b else b.shape[1]
    tm, tn, tk = _pick(M, tm_t, 8), _pick(N, tn_t, 128), _pick(K, tk_t, 128)
    nk = K // tk
    dims = (((1,), (1,)), ((), ())) if trans_b else (((1,), (0,)), ((), ()))

    def body(a_ref, b_ref, o_ref, acc_ref):
        k = pl.program_id(2)
        p = lax.dot_general(a_ref[...], b_ref[...], dims, preferred_element_type=F32)

        @pl.when(k == 0)
        def _():
            acc_ref[...] = p

        @pl.when(k > 0)
        def _():
            acc_ref[...] += p

        @pl.when(k == nk - 1)
        def _():
            o_ref[...] = acc_ref[...].astype(out_dtype)

    b_spec = pl.BlockSpec((tn, tk), lambda j, i, k: (j, k)) if trans_b else pl.BlockSpec((tk, tn), lambda j, i, k: (k, j))
    return pl.pallas_call(
        body, grid=(N // tn, M // tm, nk),
        in_specs=[pl.BlockSpec((tm, tk), lambda j, i, k: (i, k)), b_spec],
        out_specs=pl.BlockSpec((tm, tn), lambda j, i, k: (i, j)),
        out_shape=jax.ShapeDtypeStruct((M, N), out_dtype),
        scratch_shapes=[pltpu.VMEM((tm, tn), F32)],
        compiler_params=_cparams(("parallel", "parallel", "arbitrary")), name=name,
    )(a, b)


def _mm_tn(a, b, *, name, tm_t=768, tr_t=1024, tn_t=1408):
    M, R = a.shape
    N = b.shape[1]
    tm, tr, tn = _pick(M, tm_t, 8), _pick(R, tr_t, 128), _pick(N, tn_t, 128)

    def body(a_ref, b_ref, o_ref):
        m = pl.program_id(2)
        p = lax.dot_general(a_ref[...], b_ref[...], (((0,), (0,)), ((), ())), preferred_element_type=F32)

        @pl.when(m == 0)
        def _():
            o_ref[...] = p

        @pl.when(m > 0)
        def _():
            o_ref[...] += p

    return pl.pallas_call(
        body, grid=(R // tr, N // tn, M // tm),
        in_specs=[pl.BlockSpec((tm, tr), lambda r, j, m: (m, r)), pl.BlockSpec((tm, tn), lambda r, j, m: (m, j))],
        out_specs=pl.BlockSpec((tr, tn), lambda r, j, m: (r, j)),
        out_shape=jax.ShapeDtypeStruct((R, N), F32),
        compiler_params=_cparams(("parallel", "parallel", "arbitrary")), name=name,
    )(a, b)


def _col(arr, tm, width=None, col=0):
    width = arr.shape[1] if width is None else width
    return (arr, (tm, width), lambda i: (i, col))


def _lat(arr, tm, nct):
    return (arr, (tm, arr.shape[1]), lambda i: (jnp.maximum(i - nct, 0), 0))


def _ctx(arr, tm, nct):
    return (arr, (tm, arr.shape[1]), lambda i: (jnp.minimum(i, nct - 1), 0))


def _whole(p):
    return pl.BlockSpec(p.shape, lambda i, nd=p.ndim: (0,) * nd)


def _stage_fwd(f, n_tiles, ins, params, outs, name):
    ni, npar = len(ins), len(params)
    o_specs = [pl.BlockSpec((o[1], o[3]), lambda i: (0, i)) if len(o) > 4 else pl.BlockSpec((o[3], o[1]), lambda i: (i, 0))
               for o in outs]
    o_shapes = [jax.ShapeDtypeStruct((o[1], o[0]) if len(o) > 4 else (o[0], o[1]), o[2]) for o in outs]

    def body(*refs):
        i = pl.program_id(0)
        xs = [r[...].astype(F32) for r in refs[:ni]]
        ps = [r[...] for r in refs[ni:ni + npar]]
        for r, v in zip(refs[ni + npar:], f(i, xs, ps, False)):
            r[...] = v.astype(r.dtype)

    return pl.pallas_call(
        body, grid=(n_tiles,),
        in_specs=[pl.BlockSpec(bs, fn) for _, bs, fn in ins] + [_whole(p) for p in params],
        out_specs=o_specs, out_shape=o_shapes,
        compiler_params=_cparams(("parallel",)), name=name,
    )(*[a for a, _, _ in ins], *params)


def _stage_bwd(f, n_tiles, ins, params, cts, ct_fn, dins, name):
    ni, npar, nc = len(ins), len(params), len(cts)
    want = [k for k, d in enumerate(dins) if d is not None]
    extras = [dins[k][1] for k in want if dins[k][1] is not None]

    def body(*refs):
        i = pl.program_id(0)
        xs = [r[...].astype(F32) for r in refs[:ni]]
        ps = [r[...] for r in refs[ni:ni + npar]]
        ct_tiles = [r[...].astype(F32) for r in refs[ni + npar:ni + npar + nc]]
        ex_refs = list(refs[ni + npar + nc:ni + npar + nc + len(extras)])
        out_refs = refs[ni + npar + nc + len(extras):]
        _, vjp = jax.vjp(lambda xs_, ps_: tuple(f(i, xs_, ps_, True)), xs, ps)
        dxs, dps = vjp(tuple(ct_fn(i, ct_tiles)))
        for n, k in enumerate(want):
            v = dxs[k]
            if dins[k][1] is not None:
                v = v + ex_refs.pop(0)[...].astype(F32)
            out_refs[n][...] = v.astype(out_refs[n].dtype)
        for r, v in zip(out_refs[len(want):], dps):
            @pl.when(i == 0)
            def _(r=r, v=v):
                r[...] = v

            @pl.when(i > 0)
            def _(r=r, v=v):
                r[...] += v

    return pl.pallas_call(
        body, grid=(n_tiles,),
        in_specs=([pl.BlockSpec(bs, fn) for _, bs, fn in ins] + [_whole(p) for p in params]
                  + [pl.BlockSpec(bs, fn) for _, bs, fn in cts] + [pl.BlockSpec(bs, fn) for _, bs, fn in extras]),
        out_specs=([pl.BlockSpec(ins[k][1], lambda i, fn=ins[k][2]: (fn(i)[0], 0)) for k in want]
                   + [_whole(p) for p in params]),
        out_shape=([jax.ShapeDtypeStruct((ins[k][0].shape[0], ins[k][1][1]), dins[k][0]) for k in want]
                   + [jax.ShapeDtypeStruct(p.shape, F32) for p in params]),
        compiler_params=_cparams(("arbitrary",)), name=name,
    )(*[a for a, _, _ in ins], *params, *[a for a, _, _ in cts], *[a for a, _, _ in extras])


def _mod_rows(i, nct, ada, lo):
    return jnp.where(i < nct, ada[1:2, lo:lo + D], ada[0:1, lo:lo + D])


def _f_norm1(nct):
    def f(i, xs, ps, diff):
        ctx, x = xs
        nw, ada = ps
        xt = jnp.where(i < nct, ctx, x)
        return (_rms(xt, nw) * (1.0 + _mod_rows(i, nct, ada, D)) + _mod_rows(i, nct, ada, 0),)
    return f


def _f_gates_dt(i, xs, ps, diff):
    small, xc = xs
    up_f, up_b, gb_f, gb_b, dtb, alog = ps
    dot = _bdot if diff else _dot_impl_nn
    lg_f = _logsig(dot(small, up_f) + gb_f) * (1.0 / GLA_TAU)
    lg_b = _logsig(dot(small, up_b) + gb_b) * (1.0 / GLA_TAU)
    dtp = _softplus(small + dtb)
    aa = -jnp.exp(alog) * dtp
    e_f, e_b = _head_expand(32), _head_expand(64)
    n = small.shape[0]
    t = lax.broadcasted_iota(jnp.int32, (n, n), 0)
    s = lax.broadcasted_iota(jnp.int32, (n, n), 1)
    same = lax.shift_right_logical(t, 6) == lax.shift_right_logical(s, 6)
    pre_g, suf_g = ((t >= s) & same).astype(F32), ((s >= t) & same).astype(F32)
    pre_s, suf_s = (t >= s).astype(F32), (s >= t).astype(F32)
    cum_f, cum_b = _hdot(pre_s, aa), _hdot(suf_s, aa)
    h = lax.broadcasted_iota(jnp.int32, (SSM_G * SSM_HPG, 128), 0)
    l = lax.broadcasted_iota(jnp.int32, (SSM_G * SSM_HPG, 128), 1)
    nt = (((1,), (1,)), ((), ()))
    cum_ft = lax.dot_general((l == h + 32).astype(F32), cum_f, nt, precision=HI, preferred_element_type=F32)
    cum_bt = lax.dot_general((l == h + 64).astype(F32), cum_b, nt, precision=HI, preferred_element_type=F32)
    return (_hdot(pre_g, lg_f), _hdot(suf_g, lg_b), xc * _hdot(dtp, e_f), xc * _hdot(dtp, e_b),
            _hdot(cum_f, e_f), _hdot(cum_b, e_b), cum_ft, cum_bt)


def _dot_impl_nn(a, b):
    return _dot_impl(a, b, False, False)


def _f_conv(nct, tc):
    def f(i, xs, ps, diff):
        (u,) = xs
        cw, cb = ps
        n = u.shape[0]
        t = lax.broadcasted_iota(jnp.int32, (n, 1), 0)
        per = jnp.where(i < nct, tc, GRID_W)
        pos = jnp.bitwise_and(t, per - 1)
        sh = _shift if diff else _shift_impl
        acc = cb + sh(u, pos, per, -2) * cw[0:1]
        for j in range(1, 4):
            acc = acc + sh(u, pos, per, j - 2) * cw[j:j + 1]
        return (_silu(acc),)
    return f


def _f_gla_post(i, xs, ps, diff):
    ogf, ogb, r = xs
    (gw,) = ps
    o = ogf + ogb
    parts = [_rms(o[:, h * GLA_DV:(h + 1) * GLA_DV], gw) for h in range(GLA_H)]
    return (jnp.concatenate(parts, axis=1) * _silu(r),)


def _f_ssd_post(i, xs, ps, diff):
    yf, yb, xc, z = xs
    dsk8, nw = ps
    dsk = _hdot(dsk8, _head_expand(0))[0:1]
    y = (yf + yb + dsk * xc) * _silu(z)
    w = SSM_INNER // SSM_G
    parts = [_rms(y[:, g * w:(g + 1) * w], nw[:, g * w:(g + 1) * w]) for g in range(SSM_G)]
    return (jnp.concatenate(parts, axis=1),)


def _f_merge(i, xs, ps, diff):
    ga, gb, ya, yb = xs
    return (jax.nn.sigmoid(ga) * ya + jax.nn.sigmoid(gb) * yb,)


def _f_res1(nct):
    def f(i, xs, ps, diff):
        ctx, x, mix = xs
        ada, nw = ps
        h2 = jnp.where(i < nct, ctx, x) + _mod_rows(i, nct, ada, 2 * D) * mix
        return (h2, _rms(h2, nw) * (1.0 + _mod_rows(i, nct, ada, 4 * D)) + _mod_rows(i, nct, ada, 3 * D))
    return f


def _f_swiglu(i, xs, ps, diff):
    (gu,) = xs
    return (_silu(gu[:, :D_FF]) * gu[:, D_FF:],)


def _loss_head(h2, dn, tgt, ada, fw, nct):
    A = h2.shape[0]
    n_tiles = A // TM

    def tile_loss(i, h2t, dnt, tg, ada_, fw_):
        h3 = h2t + ada_[0:1, 5 * D:6 * D] * dnt
        err = _rms(h3, fw_) - tg
        row = 0.5 * jnp.mean(err * err, axis=-1, keepdims=True)
        return jnp.sum(row, axis=0, keepdims=True) * jnp.where(i < nct, 0.0, 1.0)

    def body(h2_ref, dn_ref, tg_ref, ada_ref, fw_ref, loss_ref, dh_ref, ddn_ref, dada_ref, dfw_ref):
        i = pl.program_id(0)
        val, vjp = jax.vjp(functools.partial(tile_loss, i), h2_ref[...], dn_ref[...], tg_ref[...], ada_ref[...], fw_ref[...])
        dh, ddn, _, dada, dfw = vjp(jnp.ones((1, 1), F32))
        dh_ref[...] = dh
        ddn_ref[...] = ddn.astype(BF16)
        lv = jnp.broadcast_to(val, loss_ref.shape)
        for r, v in ((loss_ref, lv), (dada_ref, dada), (dfw_ref, dfw)):
            @pl.when(i == 0)
            def _(r=r, v=v):
                r[...] = v

            @pl.when(i > 0)
            def _(r=r, v=v):
                r[...] += v

    row = lambda i: (i, 0)
    return pl.pallas_call(
        body, grid=(n_tiles,),
        in_specs=[pl.BlockSpec((TM, D), row), pl.BlockSpec((TM, D), row),
                  pl.BlockSpec((TM, D), lambda i: (jnp.maximum(i - nct, 0), 0)), _whole(ada), _whole(fw)],
        out_specs=[pl.BlockSpec((8, 128), lambda i: (0, 0)), pl.BlockSpec((TM, D), row), pl.BlockSpec((TM, D), row),
                   _whole(ada), _whole(fw)],
        out_shape=[jax.ShapeDtypeStruct((8, 128), F32), jax.ShapeDtypeStruct((A, D), F32),
                   jax.ShapeDtypeStruct((A, D), BF16), jax.ShapeDtypeStruct(ada.shape, F32),
                   jax.ShapeDtypeStruct(fw.shape, F32)],
        compiler_params=_cparams(("arbitrary",)), name="loss_head",
    )(h2, dn, tgt, ada, fw)


def _chunk_of(step, n_chunks, n_ctx_chunks, rev):
    if not rev:
        return step
    return jnp.where(step < n_ctx_chunks, n_ctx_chunks - 1 - step, n_chunks - 1 - (step - n_ctx_chunks))


def _gla_step(st, q, k, v, b, rev):
    tri = _tri(GLA_C, rev).astype(F32)
    last = 0 if rev else GLA_C - 1
    outs, sts = [], []
    for h in range(GLA_H):
        kk = slice(h * GLA_DK, (h + 1) * GLA_DK)
        vv = slice(h * GLA_DV, (h + 1) * GLA_DV)
        qh, kh, vh, bh, sh = q[:, kk] * (GLA_DK ** -0.5), k[:, kk], v[:, vv], b[:, kk], st[vv]
        tot = bh[last:last + 1]
        mid = bh[GLA_C // 2:GLA_C // 2 + 1]
        att = _bdot(qh * jnp.exp(bh - mid), kh * jnp.exp(mid - bh), False, True) * tri
        outs.append(_bdot(att, vh) + _bdot(qh * jnp.exp(bh), sh, False, True))
        sts.append(sh * jnp.exp(tot) + _bdot(vh, kh * jnp.exp(tot - bh), True, False))
    return jnp.concatenate(sts, axis=0), jnp.concatenate(outs, axis=1)


_GQK, _GV = GLA_H * GLA_DK, GLA_H * GLA_DV


def _gla_fwd(proj, bg, ncc, rev, name):
    A = proj.shape[0]
    nc = A // GLA_C
    ch = lambda s: _chunk_of(s, nc, ncc, rev)

    def body(q_ref, k_ref, v_ref, b_ref, o_ref, ss_ref, st_ref):
        @pl.when(pl.program_id(0) == 0)
        def _():
            st_ref[...] = jnp.zeros_like(st_ref)

        st = st_ref[...]
        ss_ref[0] = st
        st_new, o = _gla_step(st, q_ref[...], k_ref[...], v_ref[...], b_ref[...], rev)
        o_ref[...] = o
        st_ref[...] = st_new

    return pl.pallas_call(
        body, grid=(nc,),
        in_specs=[pl.BlockSpec((GLA_C, _GQK), lambda s: (ch(s), C_Q // _GQK)),
                  pl.BlockSpec((GLA_C, _GQK), lambda s: (ch(s), C_K // _GQK)),
                  pl.BlockSpec((GLA_C, _GV), lambda s: (ch(s), C_V // _GV)),
                  pl.BlockSpec((GLA_C, _GQK), lambda s: (ch(s), 0))],
        out_specs=[pl.BlockSpec((GLA_C, _GV), lambda s: (ch(s), 0)),
                   pl.BlockSpec((1, _GV, GLA_DK), lambda s: (s, 0, 0))],
        out_shape=[jax.ShapeDtypeStruct((A, _GV), F32), jax.ShapeDtypeStruct((nc, _GV, GLA_DK), F32)],
        scratch_shapes=[pltpu.VMEM((_GV, GLA_DK), F32)],
        compiler_params=_cparams(("arbitrary",)), name=name,
    )(proj, proj, proj, bg)


def _gla_bwd(proj, lg, saved, d_o, prev, ncc, rev, name):
    A = proj.shape[0]
    nc = A // GLA_C
    st_of = lambda r: nc - 1 - r
    ch = lambda r: _chunk_of(st_of(r), nc, ncc, rev)
    qs = pl.BlockSpec((GLA_C, _GQK), lambda r: (ch(r), 0))
    vs = pl.BlockSpec((GLA_C, _GV), lambda r: (ch(r), 0))
    n_prev = 0 if prev is None else 3

    def body(*refs):
        q_ref, k_ref, v_ref, b_ref, ss_ref, do_ref = refs[:6]
        p_refs = refs[6:6 + n_prev]
        dq_ref, dk_ref, dv_ref, db_ref, ds_ref = refs[6 + n_prev:]

        @pl.when(pl.program_id(0) == 0)
        def _():
            ds_ref[...] = jnp.zeros_like(ds_ref)

        _, vjp = jax.vjp(functools.partial(_gla_step, rev=rev), ss_ref[0], q_ref[...], k_ref[...], v_ref[...], b_ref[...])
        ds, dq, dk, dv, db = vjp((ds_ref[...], do_ref[...]))
        if n_prev:
            dq, dk, dv = dq + p_refs[0][...], dk + p_refs[1][...], dv + p_refs[2][...]
        dq_ref[...], dk_ref[...], dv_ref[...], db_ref[...] = dq, dk, dv, db
        ds_ref[...] = ds

    in_specs = [
        pl.BlockSpec((GLA_C, _GQK), lambda r: (ch(r), C_Q // _GQK)),
        pl.BlockSpec((GLA_C, _GQK), lambda r: (ch(r), C_K // _GQK)),
        pl.BlockSpec((GLA_C, _GV), lambda r: (ch(r), C_V // _GV)),
        qs,
        pl.BlockSpec((1, _GV, GLA_DK), lambda r: (st_of(r), 0, 0)),
        vs,
    ] + ([qs, qs, vs] if n_prev else [])
    return pl.pallas_call(
        body, grid=(nc,), in_specs=in_specs, out_specs=[qs, qs, vs, qs],
        out_shape=[jax.ShapeDtypeStruct((A, _GQK), F32), jax.ShapeDtypeStruct((A, _GQK), F32),
                   jax.ShapeDtypeStruct((A, _GV), F32), jax.ShapeDtypeStruct((A, _GQK), F32)],
        scratch_shapes=[pltpu.VMEM((_GV, GLA_DK), F32)],
        compiler_params=_cparams(("arbitrary",)), name=name,
    )(proj, proj, proj, lg, saved, d_o, *(prev or ()))


def _ssd_step(st, x, bm, cm, cum, cum_t, rev):
    mask = _tri(SSM_C, rev)
    last = 0 if rev else SSM_C - 1
    tot = cum[last:last + 1]
    cb = _bdot(cm, bm, False, True)
    ys = []
    for e in range(SSM_HPG):
        lm = jnp.exp(jnp.where(mask, cum[:, e * SSM_P:e * SSM_P + 1] - cum_t[e:e + 1], -jnp.inf))
        ys.append(_bdot(cb * lm, x[:, e * SSM_P:(e + 1) * SSM_P]))
    y = jnp.concatenate(ys, axis=1) + _bdot(cm, st) * jnp.exp(cum)
    st_new = st * jnp.exp(tot) + _bdot(bm, x * jnp.exp(tot - cum), True, False)
    return st_new, y


_XBC_B = SSM_INNER // SSM_N
_XBC_C = _XBC_B + SSM_G


def _ssd_fwd(xd, xbc, cum, cum_t, ncc, rev, name):
    A = xd.shape[0]
    nc = A // SSM_C
    W = SSM_HPG * SSM_P
    ch = lambda s: _chunk_of(s, nc, ncc, rev)

    def body(x_ref, b_ref, c_ref, a_ref, at_ref, y_ref, ss_ref, st_ref):
        @pl.when(pl.program_id(1) == 0)
        def _():
            st_ref[...] = jnp.zeros_like(st_ref)

        st = st_ref[...]
        ss_ref[0, 0] = st
        st_new, y = _ssd_step(st, x_ref[...], b_ref[...], c_ref[...], a_ref[...], at_ref[...], rev)
        y_ref[...] = y
        st_ref[...] = st_new

    gs = pl.BlockSpec((SSM_C, W), lambda g, s: (ch(s), g))
    return pl.pallas_call(
        body, grid=(SSM_G, nc),
        in_specs=[gs, pl.BlockSpec((SSM_C, SSM_N), lambda g, s: (ch(s), _XBC_B + g)),
                  pl.BlockSpec((SSM_C, SSM_N), lambda g, s: (ch(s), _XBC_C + g)), gs,
                  pl.BlockSpec((SSM_HPG, SSM_C), lambda g, s: (g, ch(s)))],
        out_specs=[gs, pl.BlockSpec((1, 1, SSM_N, W), lambda g, s: (g, s, 0, 0))],
        out_shape=[jax.ShapeDtypeStruct((A, SSM_INNER), F32), jax.ShapeDtypeStruct((SSM_G, nc, SSM_N, W), F32)],
        scratch_shapes=[pltpu.VMEM((SSM_N, W), F32)],
        compiler_params=_cparams(("parallel", "arbitrary")), name=name,
    )(xd, xbc, xbc, cum, cum_t)


def _ssd_bwd(xd, xbc, cum, cum_t, saved, d_y, prev, ncc, rev, name):
    A = xd.shape[0]
    nc = A // SSM_C
    W = SSM_HPG * SSM_P
    st_of = lambda r: nc - 1 - r
    ch = lambda r: _chunk_of(st_of(r), nc, ncc, rev)
    gs = pl.BlockSpec((SSM_C, W), lambda g, r: (ch(r), g))
    ns = pl.BlockSpec((SSM_C, SSM_N), lambda g, r: (ch(r), g))
    ts = pl.BlockSpec((SSM_HPG, SSM_C), lambda g, r: (g, ch(r)))
    n_prev = 0 if prev is None else 2

    def body(*refs):
        x_ref, b_ref, c_ref, a_ref, at_ref, ss_ref, dy_ref = refs[:7]
        p_refs = refs[7:7 + n_prev]
        dx_ref, db_ref, dc_ref, da_ref, dat_ref, ds_ref = refs[7 + n_prev:]

        @pl.when(pl.program_id(1) == 0)
        def _():
            ds_ref[...] = jnp.zeros_like(ds_ref)

        _, vjp = jax.vjp(functools.partial(_ssd_step, rev=rev), ss_ref[0, 0], x_ref[...], b_ref[...], c_ref[...],
                         a_ref[...], at_ref[...])
        ds, dx, db, dc, da, dat = vjp((ds_ref[...], dy_ref[...]))
        if n_prev:
            db, dc = db + p_refs[0][...], dc + p_refs[1][...]
        dx_ref[...], db_ref[...], dc_ref[...], da_ref[...], dat_ref[...] = dx, db, dc, da, dat
        ds_ref[...] = ds

    in_specs = [gs, pl.BlockSpec((SSM_C, SSM_N), lambda g, r: (ch(r), _XBC_B + g)),
                pl.BlockSpec((SSM_C, SSM_N), lambda g, r: (ch(r), _XBC_C + g)), gs, ts,
                pl.BlockSpec((1, 1, SSM_N, W), lambda g, r: (g, st_of(r), 0, 0)), gs] + ([ns, ns] if n_prev else [])
    return pl.pallas_call(
        body, grid=(SSM_G, nc), in_specs=in_specs, out_specs=[gs, ns, ns, gs, ts],
        out_shape=[jax.ShapeDtypeStruct((A, SSM_INNER), F32), jax.ShapeDtypeStruct((A, SSM_G * SSM_N), F32),
                   jax.ShapeDtypeStruct((A, SSM_G * SSM_N), F32), jax.ShapeDtypeStruct((A, SSM_INNER), F32),
                   jax.ShapeDtypeStruct((SSM_G * SSM_HPG, A), F32)],
        scratch_shapes=[pltpu.VMEM((SSM_N, W), F32)],
        compiler_params=_cparams(("parallel", "arbitrary")), name=name,
    )(xd, xbc, xbc, cum, cum_t, saved, d_y, *(prev or ()))


def _ada_fwd(cc, w_ada, b_ada):
    n = w_ada.shape[1]
    tn = _pick(n, 1024, 128)

    def body(cc_ref, w_ref, b_ref, o_ref):
        o_ref[...] = _dot_impl(_silu(cc_ref[...]), w_ref[...], False, False) + b_ref[...]

    return pl.pallas_call(
        body, grid=(n // tn,),
        in_specs=[_whole(cc), pl.BlockSpec((D, tn), lambda j: (0, j)), pl.BlockSpec((1, tn), lambda j: (0, j))],
        out_specs=pl.BlockSpec((8, tn), lambda j: (0, j)),
        out_shape=jax.ShapeDtypeStruct((8, n), F32),
        compiler_params=_cparams(("parallel",)), name="ada_fwd",
    )(cc, w_ada, b_ada)


def _ada_bwd(cc, w_ada, d1, d2, d3):
    n = w_ada.shape[1]
    tn = _pick(n, 1024, 128)
    nt = n // tn

    def body(cc_ref, w_ref, d1_ref, d2_ref, d3_ref, dw_ref, db_ref, dcc_ref, ds_ref):
        j = pl.program_id(0)
        d = d1_ref[...] + d2_ref[...] + d3_ref[...]
        s = _silu(cc_ref[...])
        dw_ref[...] = _dot_impl(s, d, True, False)
        db_ref[...] = jnp.sum(d, axis=0, keepdims=True)
        p = _dot_impl(d, w_ref[...], False, True)

        @pl.when(j == 0)
        def _():
            ds_ref[...] = p

        @pl.when(j > 0)
        def _():
            ds_ref[...] += p

        @pl.when(j == nt - 1)
        def _():
            _, vjp = jax.vjp(_silu, cc_ref[...])
            dcc_ref[...] = vjp(ds_ref[...])[0]

    col = lambda j: (0, j)
    return pl.pallas_call(
        body, grid=(nt,),
        in_specs=[_whole(cc), pl.BlockSpec((D, tn), col), pl.BlockSpec((8, tn), col), pl.BlockSpec((8, tn), col),
                  pl.BlockSpec((8, tn), col)],
        out_specs=[pl.BlockSpec((D, tn), col), pl.BlockSpec((1, tn), col), _whole(cc)],
        out_shape=[jax.ShapeDtypeStruct((D, n), F32), jax.ShapeDtypeStruct((1, n), F32), jax.ShapeDtypeStruct((8, D), F32)],
        scratch_shapes=[pltpu.VMEM((8, D), F32)],
        compiler_params=_cparams(("arbitrary",)), name="ada_bwd",
    )(cc, w_ada, d1, d2, d3)


def _local_step(x, c, ctx, c_ctx, target, w):
    T, Tc = x.shape[0], ctx.shape[0]
    assert Tc == TM and T % TM == 0 and GRID_W == GLA_C and TS == SSM_C == 2 * GLA_C
    A = T + Tc
    nct = Tc // TM
    n_tm, n_ts = A // TM, A // TS
    g = {}

    cc = jnp.zeros((8, D), F32).at[0].set(c[0]).at[1].set(c_ctx)
    ada = _ada_fwd(cc, w["w_ada"], w["b_ada"])

    x_in = [_ctx(ctx, TM, nct), _lat(x, TM, nct)]
    f_norm1 = _f_norm1(nct)
    p_norm1 = [w["norm1_w"], ada]
    (h1,) = _stage_fwd(f_norm1, n_tm, x_in, p_norm1, [(A, D, BF16, TM)], "norm1")
    proj = _mm(h1, w["w_in"], trans_b=False, out_dtype=F32, name="mm_in", tn_t=1152)

    p_gd = [w["up_f"], w["up_b"], w["gla_bias_f"], w["gla_bias_b"], w["dtb"], w["alog"]]
    f_conv = _f_conv(nct, Tc)
    p_conv = [w["conv_w"], w["conv_b"]]
    in_conv = [_col(proj, TM, 3072, C_U // 3072)]
    (xbc,) = _stage_fwd(f_conv, n_tm, in_conv, p_conv, [(A, 3072, F32, TM)], "conv")
    in_gd = [_col(proj, TS, 128, C_S // 128), _col(xbc, TS, SSM_INNER, 0)]
    n_heads = SSM_G * SSM_HPG
    lg_f, lg_b, xf, xb, axf, axb, atf, atb = _stage_fwd(
        _f_gates_dt, n_ts, in_gd, p_gd,
        [(A, 512, F32, TS), (A, 512, F32, TS)] + [(A, SSM_INNER, F32, TS)] * 4 + [(A, n_heads, F32, TS, True)] * 2,
        "gates_dt")

    ncc_g, ncc_s = Tc // GLA_C, Tc // SSM_C
    ogf, sv_gf = _gla_fwd(proj, lg_f, ncc_g, False, "gla_f")
    ogb, sv_gb = _gla_fwd(proj, lg_b, ncc_g, True, "gla_b")
    ysf, sv_sf = _ssd_fwd(xf, xbc, axf, atf, ncc_s, False, "ssd_f")
    ysb, sv_sb = _ssd_fwd(xb, xbc, axb, atb, ncc_s, True, "ssd_b")

    in_gp = [_col(ogf, TS), _col(ogb, TS), _col(proj, TS, 1024, C_R // 1024)]
    p_gp = [w["gla_norm_w"]]
    (oa,) = _stage_fwd(_f_gla_post, n_ts, in_gp, p_gp, [(A, D, BF16, TS)], "gla_post")
    in_sp = [_col(ysf, TS), _col(ysb, TS), _col(xbc, TS, SSM_INNER, 0), _col(proj, TS, SSM_INNER, C_Z // SSM_INNER)]
    p_sp = [w["dsk8"], w["ssm_norm_w"]]
    (ob,) = _stage_fwd(_f_ssd_post, n_ts, in_sp, p_sp, [(A, SSM_INNER, BF16, TS)], "ssd_post")
    ya = _mm(oa, w["w_pa"], trans_b=False, out_dtype=F32, name="mm_pa")
    yb = _mm(ob, w["w_pb"], trans_b=False, out_dtype=F32, name="mm_pb")
    in_mg = [_col(proj, TS, 1024, C_GA // 1024), _col(proj, TS, 1024, C_GB // 1024), _col(ya, TS), _col(yb, TS)]
    (merged,) = _stage_fwd(_f_merge, n_ts, in_mg, [], [(A, D, BF16, TS)], "merge")
    mix = _mm(merged, w["w_out"], trans_b=False, out_dtype=F32, name="mm_out")

    f_res1 = _f_res1(nct)
    in_r1 = x_in + [_col(mix, TM)]
    p_r1 = [ada, w["norm2_w"]]
    h2, hm2 = _stage_fwd(f_res1, n_tm, in_r1, p_r1, [(A, D, F32, TM), (A, D, BF16, TM)], "res1")
    gu = _mm(hm2, w["w_gu"], trans_b=False, out_dtype=F32, name="mm_gu", tn_t=1408)
    in_sw = [_col(gu, TS)]
    (act,) = _stage_fwd(_f_swiglu, n_ts, in_sw, [], [(A, D_FF, BF16, TS)], "swiglu")
    dn = _mm(act, w["w_down"], trans_b=False, out_dtype=F32, name="mm_down")

    loss_blk, d_h2a, d_dn, d_ada3, g["final_norm_w"] = _loss_head(h2, dn, target, ada, w["final_norm_w"], nct)
    g["w_down"] = _mm_tn(act, d_dn, name="dw_down", tr_t=1408)
    d_act = _mm(d_dn, w["w_down"], trans_b=True, out_dtype=F32, name="dx_down", tn_t=1408)
    (d_gu,) = _stage_bwd(_f_swiglu, n_ts, in_sw, [], [_col(d_act, TS)], lambda i, t: t, [(BF16, None)], "swiglu_b")
    g["w_gu"] = _mm_tn(hm2, d_gu, name="dw_gu")
    d_hm2 = _mm(d_gu, w["w_gu"], trans_b=True, out_dtype=F32, name="dx_gu")
    d_x1, d_mix, d_ada2, g["norm2_w"] = _stage_bwd(
        f_res1, n_tm, in_r1, p_r1, [_col(d_h2a, TM), _col(d_hm2, TM)], lambda i, t: t,
        [None, (F32, None), (BF16, None)], "res1_b")
    g["w_out"] = _mm_tn(merged, d_mix, name="dw_out")
    d_merged = _mm(d_mix, w["w_out"], trans_b=True, out_dtype=F32, name="dx_out")
    d_ga, d_gb, d_ya, d_yb = _stage_bwd(
        _f_merge, n_ts, in_mg, [], [_col(d_merged, TS)], lambda i, t: t, [(BF16, None)] * 4, "merge_b")
    g["w_pa"] = _mm_tn(oa, d_ya, name="dw_pa")
    g["w_pb"] = _mm_tn(ob, d_yb, name="dw_pb")
    d_oa = _mm(d_ya, w["w_pa"], trans_b=True, out_dtype=F32, name="dx_pa")
    d_ob = _mm(d_yb, w["w_pb"], trans_b=True, out_dtype=F32, name="dx_pb")
    d_og, d_r, g["gla_norm_w"] = _stage_bwd(
        _f_gla_post, n_ts, in_gp, p_gp, [_col(d_oa, TS)], lambda i, t: t, [(F32, None), None, (BF16, None)], "gla_post_b")
    d_ys, d_xs_skip, d_z, g["dsk8"], g["ssm_norm_w"] = _stage_bwd(
        _f_ssd_post, n_ts, in_sp, p_sp, [_col(d_ob, TS)], lambda i, t: t,
        [(F32, None), None, (F32, None), (BF16, None)], "ssd_post_b")

    dq, dk, dv, d_lgf = _gla_bwd(proj, lg_f, sv_gf, d_og, None, ncc_g, False, "gla_f_b")
    dq, dk, dv, d_lgb = _gla_bwd(proj, lg_b, sv_gb, d_og, (dq, dk, dv), ncc_g, True, "gla_b_b")
    d_xf, d_bm, d_cm, d_axf, d_atf = _ssd_bwd(xf, xbc, axf, atf, sv_sf, d_ys, None, ncc_s, False, "ssd_f_b")
    d_xb, d_bm, d_cm, d_axb, d_atb = _ssd_bwd(xb, xbc, axb, atb, sv_sb, d_ys, (d_bm, d_cm), ncc_s, True, "ssd_b_b")

    cts_gd = [_col(a, TS) for a in (d_lgf, d_lgb, d_xf, d_xb, d_axf, d_axb)]
    cts_gd += [(a, (n_heads, TS), lambda i: (0, i)) for a in (d_atf, d_atb)]
    d_small, d_xs_dt, g["up_f"], g["up_b"], g["gla_bias_f"], g["gla_bias_b"], g["dtb"], g["alog"] = _stage_bwd(
        _f_gates_dt, n_ts, in_gd, p_gd, cts_gd, lambda i, t: t, [(BF16, None), (F32, None)], "gates_dt_b")
    cts_conv = [_col(d_xs_skip, TM), _col(d_xs_dt, TM), _col(d_bm, TM), _col(d_cm, TM)]
    d_u, g["conv_w"], g["conv_b"] = _stage_bwd(
        f_conv, n_tm, in_conv, p_conv, cts_conv,
        lambda i, t: [jnp.concatenate([t[0] + t[1], t[2], t[3]], axis=1)], [(BF16, None)], "conv_b")

    d_proj = jnp.concatenate([dq.astype(BF16), dk.astype(BF16), dv.astype(BF16), d_r, d_u, d_z, d_ga, d_gb, d_small], axis=1)
    g["w_in"] = _mm_tn(h1, d_proj, name="dw_in", tn_t=1152)
    d_h1 = _mm(d_proj, w["w_in"], trans_b=True, out_dtype=F32, name="dx_in", tk_t=1152)
    grad_x, g["norm1_w"], d_ada1 = _stage_bwd(
        f_norm1, n_tm, x_in, p_norm1, [_col(d_h1, TM)], lambda i, t: t,
        [None, (F32, (d_x1, (TM, D), x_in[1][2]))], "norm1_b")
    g["w_ada"], g["b_ada"], d_cc = _ada_bwd(cc, w["w_ada"], d_ada1, d_ada2, d_ada3)
    g["c_ctx"] = d_cc[1]
    return loss_blk, grad_x, g


def _exchange(x, a2a, name):
    shp = x.shape[1:] if a2a else x.shape

    def body(x_ref, o_ref, send_sems, recv_sems, local_sem):
        mx, my, mc = lax.axis_index("x"), lax.axis_index("y"), lax.axis_index("c")
        me = 4 * mx + 2 * my + mc
        copies = []
        for k in range(1, N_DEV):
            px = 1 - mx if k & 4 else mx
            py = 1 - my if k & 2 else my
            pc = 1 - mc if k & 1 else mc
            src = x_ref.at[4 * px + 2 * py + pc] if a2a else x_ref
            cp = pltpu.make_async_remote_copy(
                src_ref=src, dst_ref=o_ref.at[me], send_sem=send_sems.at[k - 1], recv_sem=recv_sems.at[k - 1],
                device_id=(px, py, pc), device_id_type=pl.DeviceIdType.MESH)
            cp.start()
            copies.append(cp)
        own = pltpu.make_async_copy(x_ref.at[me] if a2a else x_ref, o_ref.at[me], local_sem)
        own.start()
        for cp in copies:
            cp.wait()
        own.wait()

    return pl.pallas_call(
        body, out_shape=jax.ShapeDtypeStruct((N_DEV,) + tuple(shp), x.dtype),
        in_specs=[pl.BlockSpec(memory_space=pl.ANY)], out_specs=pl.BlockSpec(memory_space=pl.ANY),
        scratch_shapes=[pltpu.SemaphoreType.DMA((N_DEV - 1,)), pltpu.SemaphoreType.DMA((N_DEV - 1,)),
                        pltpu.SemaphoreType.DMA(())],
        name=name,
    )(x)


def _adamw_math(w, gr, m, v):
    m = ADAM_B1 * m + (1.0 - ADAM_B1) * gr
    v = ADAM_B2 * v + (1.0 - ADAM_B2) * (gr * gr)
    m_hat = m / np.float32(1.0 - ADAM_B1 ** ADAM_STEP)
    v_hat = v / np.float32(1.0 - ADAM_B2 ** ADAM_STEP)
    delta = -ADAM_LR * (m_hat / (jnp.sqrt(v_hat) + ADAM_EPS) + ADAM_WD * w)
    return delta, m, v


def _sum_adamw(parts, w, m, v, name):
    R, C = w.shape
    tr = _pick(R, max(8, (2 << 20) // (4 * C) // 8 * 8), 8)

    def body(p_ref, w_ref, m_ref, v_ref, g_ref, d_ref, mo_ref, vo_ref):
        gr = p_ref[0]
        for k in range(1, N_DEV):
            gr = gr + p_ref[k]
        g_ref[...] = gr
        d_ref[...], mo_ref[...], vo_ref[...] = _adamw_math(w_ref[...], gr, m_ref[...], v_ref[...])

    row = pl.BlockSpec((tr, C), lambda i: (i, 0))
    return pl.pallas_call(
        body, grid=(R // tr,),
        in_specs=[pl.BlockSpec((N_DEV, tr, C), lambda i: (0, i, 0)), row, row, row],
        out_specs=[row] * 4, out_shape=[jax.ShapeDtypeStruct((R, C), F32)] * 4,
        compiler_params=_cparams(("parallel",)), name=name,
    )(parts, w, m, v)


def _sum8(parts, name):
    _, R, C = parts.shape

    def body(p_ref, g_ref):
        gr = p_ref[0]
        for k in range(1, N_DEV):
            gr = gr + p_ref[k]
        g_ref[...] = gr

    return pl.pallas_call(body, out_shape=jax.ShapeDtypeStruct((R, C), F32), name=name)(parts)


def _adamw(w, gr, m, v, name):
    def body(w_ref, g_ref, m_ref, v_ref, d_ref, mo_ref, vo_ref):
        d_ref[...], mo_ref[...], vo_ref[...] = _adamw_math(w_ref[...], g_ref[...], m_ref[...], v_ref[...])

    return pl.pallas_call(body, out_shape=[jax.ShapeDtypeStruct(w.shape, F32)] * 3, name=name)(w, gr, m, v)


def _to_padded(w_in):
    z = jnp.zeros(w_in.shape[:-1] + (32,), w_in.dtype)
    return jnp.concatenate([w_in[..., 0:3072], w_in[..., 5152:8224], w_in[..., 3104:5152], w_in[..., 8288:10336],
                            w_in[..., 3072:3104], w_in[..., 8224:8288], z], axis=-1)


def _from_padded(p):
    return jnp.concatenate([p[..., 0:3072], p[..., 10240:10272], p[..., 6144:8192], p[..., 3072:6144],
                            p[..., 10272:10336], p[..., 8192:10240]], axis=-1)


def _unshard_cols(gathered):
    n, r, c = gathered.shape
    return jnp.transpose(gathered, (1, 0, 2)).reshape(r, n * c)


def _shard_cols(full):
    r, nc = full.shape
    return jnp.transpose(full.reshape(r, N_DEV, nc // N_DEV), (1, 0, 2))


def _lanes(vec, lo):
    return jnp.zeros((1, 128), F32).at[:, lo:lo + vec.shape[1]].set(vec)


_SMALL = (("b_ada", 6 * D), ("c_ctx", D), ("norm1_w", D), ("gla_bias_f", 512), ("gla_bias_b", 512), ("gla_norm_w", 256),
          ("conv_b", 3072), ("dt_bias_f", 32), ("dt_bias_b", 32), ("a_log_f", 32), ("a_log_b", 32), ("d_skip", 32),
          ("ssm_norm_w", 2048), ("norm2_w", D), ("final_norm_w", D))
_SHARDED_SMALL = (("gla_up_f", 16 * 512), ("gla_up_b", 16 * 512), ("conv_w", 4 * 3072))


def _pack(vals, names):
    flat = jnp.concatenate([vals[n].reshape(-1).astype(F32) for n, _ in names])
    pad = (-flat.shape[0]) % 1024
    return jnp.concatenate([flat, jnp.zeros((pad,), F32)]).reshape(-1, 128)


def _unpack(packed, names, shapes):
    flat, out, off = packed.reshape(-1), {}, 0
    for n, size in names:
        out[n] = flat[off:off + size].reshape(shapes[n])
        off += size
    return out


def kernel(x, c, ctx, c_ctx, w_ada, b_ada, norm1_w, w_in, gla_up_f, gla_bias_f, gla_up_b, gla_bias_b, gla_norm_w, conv_w, conv_b, dt_bias_f, dt_bias_b, a_log_f, a_log_b, d_skip, ssm_norm_w, w_pa, w_pb, w_out, norm2_w, w_gate, w_up, w_down, final_norm_w, loss_target, m_c_ctx, m_w_ada, m_b_ada, m_norm1_w, m_w_in, m_gla_up_f, m_gla_bias_f, m_gla_up_b, m_gla_bias_b, m_gla_norm_w, m_conv_w, m_conv_b, m_dt_bias_f, m_dt_bias_b, m_a_log_f, m_a_log_b, m_d_skip, m_ssm_norm_w, m_w_pa, m_w_pb, m_w_out, m_norm2_w, m_w_gate, m_w_up, m_w_down, m_final_norm_w, v_c_ctx, v_w_ada, v_b_ada, v_norm1_w, v_w_in, v_gla_up_f, v_gla_bias_f, v_gla_up_b, v_gla_bias_b, v_gla_norm_w, v_conv_w, v_conv_b, v_dt_bias_f, v_dt_bias_b, v_a_log_f, v_a_log_b, v_d_skip, v_ssm_norm_w, v_w_pa, v_w_pb, v_w_out, v_norm2_w, v_w_gate, v_w_up, v_w_down, v_final_norm_w):
    args = dict(locals())
    me = 4 * lax.axis_index("x") + 2 * lax.axis_index("y") + lax.axis_index("c")

    def gather16(a, name):
        return _exchange(a[0].astype(BF16), False, name)

    full = {
        "w_ada": _unshard_cols(gather16(w_ada, "ag_w_ada")),
        "w_in": _to_padded(_unshard_cols(gather16(w_in, "ag_w_in"))),
        "w_pa": gather16(w_pa, "ag_w_pa").reshape(D, D),
        "w_pb": gather16(w_pb, "ag_w_pb").reshape(SSM_INNER, D),
        "w_out": gather16(w_out, "ag_w_out").reshape(D, D),
        "w_gu": jnp.concatenate([_unshard_cols(gather16(w_gate, "ag_w_gate")), _unshard_cols(gather16(w_up, "ag_w_up"))], axis=1),
        "w_down": gather16(w_down, "ag_w_down").reshape(D_FF, D),
    }
    sm = _exchange(jnp.concatenate([gla_up_f.reshape(-1), gla_up_b.reshape(-1), conv_w.reshape(-1)]).reshape(-1, 128), False, "ag_small")
    sm = sm.reshape(N_DEV, -1)
    up_f = _unshard_cols(sm[:, 0:1024].reshape(N_DEV, 16, 64))
    up_b = _unshard_cols(sm[:, 1024:2048].reshape(N_DEV, 16, 64))
    full["conv_w"] = _unshard_cols(sm[:, 2048:3584].reshape(N_DEV, 4, 384))
    full["up_f"] = jnp.zeros((128, 512), F32).at[0:16].set(up_f)
    full["up_b"] = jnp.zeros((128, 512), F32).at[16:32].set(up_b)
    full["dtb"] = _lanes(dt_bias_f, 32) + _lanes(dt_bias_b, 64)
    full["alog"] = _lanes(a_log_f, 32) + _lanes(a_log_b, 64)
    full["dsk8"] = jnp.zeros((8, 128), F32).at[0:1, 0:32].set(d_skip)
    for n in ("b_ada", "norm1_w", "gla_bias_f", "gla_bias_b", "gla_norm_w", "conv_b", "ssm_norm_w", "norm2_w"):
        full[n] = args[n]
    full["final_norm_w"] = final_norm_w.reshape(1, D)

    loss_blk, grad_x, g = _local_step(x[0], c, ctx[0], c_ctx, loss_target[0], full)

    gs = dict(g)
    gs["dt_bias_f"], gs["dt_bias_b"] = g["dtb"][:, 32:64], g["dtb"][:, 64:96]
    gs["a_log_f"], gs["a_log_b"] = g["alog"][:, 32:64], g["alog"][:, 64:96]
    gs["d_skip"] = g["dsk8"][0:1, 0:32]
    gs["gla_up_f"], gs["gla_up_b"] = g["up_f"][0:16], g["up_b"][16:32]
    gs["loss"] = loss_blk[0:1, 0:1]
    names = _SMALL + _SHARDED_SMALL + (("loss", 1),)
    shapes = {n: (args[n].shape if n in args else (1, 1)) for n, _ in names}
    shapes.update({"gla_up_f": (16, 512), "gla_up_b": (16, 512), "conv_w": (4, 3072)})
    red = _unpack(_sum8(_exchange(_pack(gs, names), False, "ag_small_grads"), "sum_small_grads"), names, shapes)
    loss = red["loss"].reshape(())
    grads = {n: red[n] for n, _ in _SMALL}
    grads["gla_up_f"] = lax.dynamic_slice(red["gla_up_f"], (0, me * 64), (16, 64))[None]
    grads["gla_up_b"] = lax.dynamic_slice(red["gla_up_b"], (0, me * 64), (16, 64))[None]
    grads["conv_w"] = lax.dynamic_slice(red["conv_w"], (0, me * 384), (4, 384))[None]
    upd_names = tuple((n, s) for n, s in _SMALL) + (("gla_up_f", 1024), ("gla_up_b", 1024), ("conv_w", 1536))
    pk = lambda prefix, src: _pack({n: src[prefix + n] for n, _ in upd_names}, upd_names)
    d_s, m_s, v_s = _adamw(pk("", args), _pack(grads, upd_names), pk("m_", args), pk("v_", args), "adamw_small")
    upd_shapes = {n: args[n].shape for n, _ in upd_names}
    delta = _unpack(d_s, upd_names, upd_shapes)
    new_m = _unpack(m_s, upd_names, upd_shapes)
    new_v = _unpack(v_s, upd_names, upd_shapes)

    g_in = _from_padded(g["w_in"])
    big = {
        "w_ada": _shard_cols(g["w_ada"]), "w_in": _shard_cols(g_in),
        "w_pa": g["w_pa"].reshape(N_DEV, D // N_DEV, D), "w_pb": g["w_pb"].reshape(N_DEV, SSM_INNER // N_DEV, D),
        "w_out": g["w_out"].reshape(N_DEV, D // N_DEV, D),
        "w_gate": _shard_cols(g["w_gu"][:, :D_FF]), "w_up": _shard_cols(g["w_gu"][:, D_FF:]),
        "w_down": g["w_down"].reshape(N_DEV, D_FF // N_DEV, D),
    }
    for n, parts in big.items():
        got = _exchange(parts, True, "a2a_" + n)
        gr, dl, mo, vo = _sum_adamw(got, args[n][0], args["m_" + n][0], args["v_" + n][0], "adamw_" + n)
        grads[n], delta[n], new_m[n], new_v[n] = gr[None], dl[None], mo[None], vo[None]

    order = ["c_ctx", "w_ada", "b_ada", "norm1_w", "w_in", "gla_up_f", "gla_bias_f", "gla_up_b", "gla_bias_b", "gla_norm_w",
             "conv_w", "conv_b", "dt_bias_f", "dt_bias_b", "a_log_f", "a_log_b", "d_skip", "ssm_norm_w", "w_pa", "w_pb",
             "w_out", "norm2_w", "w_gate", "w_up", "w_down", "final_norm_w"]
    fix = lambda d: [d[n].reshape(args[n].shape) for n in order]
    return (loss, grad_x[None], *fix(grads), *fix(delta), *fix(new_m), *fix(new_v))
```

```python
import functools

import jax
import jax.numpy as jnp
import numpy as np
from jax import lax
from jax.experimental import pallas as pl
from jax.experimental.pallas import tpu as pltpu

F32 = jnp.float32
BF16 = jnp.bfloat16
N_DEV = 8
D = 1024
EPS = 1e-6
GRID_W = 64
GLA_H, GLA_DK, GLA_DV = 4, 128, 256
GLA_C = 64
GLA_TAU = 16.0
SSM_G, SSM_HPG, SSM_P, SSM_N = 4, 8, 64, 128
SSM_C = 128
SSM_INNER = 2048
D_FF = 2816
D_IN = 10336
D_INP = 10368
C_Q, C_K, C_V, C_R, C_U, C_Z, C_GA, C_GB, C_S = 0, 512, 1024, 2048, 3072, 6144, 8192, 9216, 10240
TM = 256
TS = 128

ADAM_LR, ADAM_B1, ADAM_B2, ADAM_EPS, ADAM_WD, ADAM_STEP = 0.001, 0.9, 0.999, 1e-08, 0.01, 10
VMEM_LIMIT = 56 << 20


def _cparams(sem):
    return pltpu.CompilerParams(dimension_semantics=sem, vmem_limit_bytes=VMEM_LIMIT)


def _pick(n, target, mult):
    best = None
    for t in range(mult, min(n, target) + 1, mult):
        if n % t == 0:
            best = t
    return best if best is not None else n


def _dot_impl(a, b, ta, tb):
    dims = (((0 if ta else 1,), (1 if tb else 0,)), ((), ()))
    return lax.dot_general(a.astype(BF16), b.astype(BF16), dims, preferred_element_type=F32)


@functools.partial(jax.custom_vjp, nondiff_argnums=(2, 3))
def _bdot(a, b, ta=False, tb=False):
    return _dot_impl(a, b, ta, tb)


def _bdot_fwd(a, b, ta, tb):
    return _dot_impl(a, b, ta, tb), (a, b)


def _bdot_bwd(ta, tb, res, g):
    a, b = res
    if not ta and not tb:
        return _dot_impl(g, b, False, True), _dot_impl(a, g, True, False)
    if not ta and tb:
        return _dot_impl(g, b, False, False), _dot_impl(g, a, True, False)
    if ta and not tb:
        return _dot_impl(b, g, False, True), _dot_impl(a, g, False, False)
    raise NotImplementedError


_bdot.defvjp(_bdot_fwd, _bdot_bwd)


_NN = (((1,), (0,)), ((), ()))
_NT = (((1,), (1,)), ((), ()))
_TN = (((0,), (0,)), ((), ()))


def _dg2(x, e, x_is_lhs, dims):
    hi = x.astype(BF16)
    lo = (x - hi.astype(F32)).astype(BF16)
    e = e.astype(BF16)
    if x_is_lhs:
        return (lax.dot_general(hi, e, dims, preferred_element_type=F32)
                + lax.dot_general(lo, e, dims, preferred_element_type=F32))
    return (lax.dot_general(e, hi, dims, preferred_element_type=F32)
            + lax.dot_general(e, lo, dims, preferred_element_type=F32))


_EDOT_FWD = {"xe": (True, _NN), "ex": (False, _NN), "ext": (False, _NT)}
_EDOT_BWD = {"xe": (True, _NT), "ex": (False, _TN), "ext": (True, _TN)}


@functools.partial(jax.custom_vjp, nondiff_argnums=(2,))
def _edot(x, e, mode):
    return _dg2(x, e, *_EDOT_FWD[mode])


def _edot_fwd(x, e, mode):
    return _dg2(x, e, *_EDOT_FWD[mode]), e


def _edot_bwd(mode, e, g):
    return _dg2(g, e, *_EDOT_BWD[mode]), None


_edot.defvjp(_edot_fwd, _edot_bwd)


def _shift_impl(u, pos, per, s):
    n = u.shape[0]
    rolled = u if s == 0 else pltpu.roll(u, (-s) % n, 0)
    ok = (pos + s >= 0) & (pos + s < per)
    return jnp.where(ok, rolled, 0.0)


@functools.partial(jax.custom_vjp, nondiff_argnums=(3,))
def _shift(u, pos, per, s):
    return _shift_impl(u, pos, per, s)


def _shift_fwd(u, pos, per, s):
    return _shift_impl(u, pos, per, s), (pos, per)


def _shift_bwd(s, res, g):
    pos, per = res
    return _shift_impl(g, pos, per, -s), None, None


_shift.defvjp(_shift_fwd, _shift_bwd)


def _rms(x, w):
    return x * lax.rsqrt(jnp.mean(x * x, axis=-1, keepdims=True) + EPS) * w


def _silu(x):
    return x * jax.nn.sigmoid(x)


def _softplus(x):
    return jnp.maximum(x, 0.0) + jnp.log(1.0 + jnp.exp(-jnp.abs(x)))


def _logsig(x):
    return jnp.minimum(x, 0.0) - jnp.log(1.0 + jnp.exp(-jnp.abs(x)))


def _tri(n, rev):
    t = lax.broadcasted_iota(jnp.int32, (n, n), 0)
    s = lax.broadcasted_iota(jnp.int32, (n, n), 1)
    return (s >= t) if rev else (t >= s)


def _head_expand(first_lane):
    l = lax.broadcasted_iota(jnp.int32, (128, SSM_INNER), 0)
    c = lax.broadcasted_iota(jnp.int32, (128, SSM_INNER), 1)
    return (l == first_lane + lax.shift_right_logical(c, 6)).astype(F32)


def _mm(a, b, *, trans_b, out_dtype, name, tm_t=768, tn_t=1024, tk_t=2816):
    M, K = a.shape
    N = b.shape[0] if trans_b else b.shape[1]
    tm, tn, tk = _pick(M, tm_t, 8), _pick(N, tn_t, 128), _pick(K, tk_t, 128)
    nk = K // tk
    dims = (((1,), (1,)), ((), ())) if trans_b else (((1,), (0,)), ((), ()))

    def body(a_ref, b_ref, o_ref, acc_ref):
        k = pl.program_id(2)
        p = lax.dot_general(a_ref[...], b_ref[...], dims, preferred_element_type=F32)

        @pl.when(k == 0)
        def _():
            acc_ref[...] = p

        @pl.when(k > 0)
        def _():
            acc_ref[...] += p

        @pl.when(k == nk - 1)
        def _():
            o_ref[...] = acc_ref[...].astype(out_dtype)

    b_spec = pl.BlockSpec((tn, tk), lambda j, i, k: (j, k)) if trans_b else pl.BlockSpec((tk, tn), lambda j, i, k: (k, j))
    return pl.pallas_call(
        body, grid=(N // tn, M // tm, nk),
        in_specs=[pl.BlockSpec((tm, tk), lambda j, i, k: (i, k)), b_spec],
        out_specs=pl.BlockSpec((tm, tn), lambda j, i, k: (i, j)),
        out_shape=jax.ShapeDtypeStruct((M, N), out_dtype),
        scratch_shapes=[pltpu.VMEM((tm, tn), F32)],
        compiler_params=_cparams(("parallel", "parallel", "arbitrary")), name=name,
    )(a, b)


def _mm_tn(a, b, *, name, tm_t=768, tr_t=1024, tn_t=1408):
    M, R = a.shape
    N = b.shape[1]
    tm, tr, tn = _pick(M, tm_t, 8), _pick(R, tr_t, 128), _pick(N, tn_t, 128)

    def body(a_ref, b_ref, o_ref):
        m = pl.program_id(2)
        p = lax.dot_general(a_ref[...], b_ref[...], (((0,), (0,)), ((), ())), preferred_element_type=F32)

        @pl.when(m == 0)
        def _():
            o_ref[...] = p

        @pl.when(m > 0)
        def _():
            o_ref[...] += p

    return pl.pallas_call(
        body, grid=(R // tr, N // tn, M // tm),
        in_specs=[pl.BlockSpec((tm, tr), lambda r, j, m: (m, r)), pl.BlockSpec((tm, tn), lambda r, j, m: (m, j))],
        out_specs=pl.BlockSpec((tr, tn), lambda r, j, m: (r, j)),
        out_shape=jax.ShapeDtypeStruct((R, N), F32),
        compiler_params=_cparams(("parallel", "parallel", "arbitrary")), name=name,
    )(a, b)


def _col(arr, tm, width=None, col=0):
    width = arr.shape[1] if width is None else width
    return (arr, (tm, width), lambda i: (i, col))


def _lat(arr, tm, nct):
    return (arr, (tm, arr.shape[1]), lambda i: (jnp.maximum(i - nct, 0), 0))


def _ctx(arr, tm, nct):
    return (arr, (tm, arr.shape[1]), lambda i: (jnp.minimum(i, nct - 1), 0))


def _whole(p):
    return pl.BlockSpec(p.shape, lambda i, nd=p.ndim: (0,) * nd)


def _stage_fwd(f, n_tiles, ins, params, outs, name):
    ni, npar = len(ins), len(params)
    o_specs = [pl.BlockSpec((o[1], o[3]), lambda i: (0, i)) if len(o) > 4 else pl.BlockSpec((o[3], o[1]), lambda i: (i, 0))
               for o in outs]
    o_shapes = [jax.ShapeDtypeStruct((o[1], o[0]) if len(o) > 4 else (o[0], o[1]), o[2]) for o in outs]

    def body(*refs):
        i = pl.program_id(0)
        xs = [r[...].astype(F32) for r in refs[:ni]]
        ps = [r[...] for r in refs[ni:ni + npar]]
        for r, v in zip(refs[ni + npar:], f(i, xs, ps, False)):
            r[...] = v.astype(r.dtype)

    return pl.pallas_call(
        body, grid=(n_tiles,),
        in_specs=[pl.BlockSpec(bs, fn) for _, bs, fn in ins] + [_whole(p) for p in params],
        out_specs=o_specs, out_shape=o_shapes,
        compiler_params=_cparams(("parallel",)), name=name,
    )(*[a for a, _, _ in ins], *params)


def _stage_bwd(f, n_tiles, ins, params, cts, ct_fn, dins, name):
    ni, npar, nc = len(ins), len(params), len(cts)
    want = [k for k, d in enumerate(dins) if d is not None]
    extras = [dins[k][1] for k in want if dins[k][1] is not None]

    def body(*refs):
        i = pl.program_id(0)
        xs = [r[...].astype(F32) for r in refs[:ni]]
        ps = [r[...] for r in refs[ni:ni + npar]]
        ct_tiles = [r[...].astype(F32) for r in refs[ni + npar:ni + npar + nc]]
        ex_refs = list(refs[ni + npar + nc:ni + npar + nc + len(extras)])
        out_refs = refs[ni + npar + nc + len(extras):]
        _, vjp = jax.vjp(lambda xs_, ps_: tuple(f(i, xs_, ps_, True)), xs, ps)
        dxs, dps = vjp(tuple(ct_fn(i, ct_tiles)))
        for n, k in enumerate(want):
            v = dxs[k]
            if dins[k][1] is not None:
                v = v + ex_refs.pop(0)[...].astype(F32)
            out_refs[n][...] = v.astype(out_refs[n].dtype)
        for r, v in zip(out_refs[len(want):], dps):
            @pl.when(i == 0)
            def _(r=r, v=v):
                r[...] = v

            @pl.when(i > 0)
            def _(r=r, v=v):
                r[...] += v

    return pl.pallas_call(
        body, grid=(n_tiles,),
        in_specs=([pl.BlockSpec(bs, fn) for _, bs, fn in ins] + [_whole(p) for p in params]
                  + [pl.BlockSpec(bs, fn) for _, bs, fn in cts] + [pl.BlockSpec(bs, fn) for _, bs, fn in extras]),
        out_specs=([pl.BlockSpec(ins[k][1], lambda i, fn=ins[k][2]: (fn(i)[0], 0)) for k in want]
                   + [_whole(p) for p in params]),
        out_shape=([jax.ShapeDtypeStruct((ins[k][0].shape[0], ins[k][1][1]), dins[k][0]) for k in want]
                   + [jax.ShapeDtypeStruct(p.shape, F32) for p in params]),
        compiler_params=_cparams(("arbitrary",)), name=name,
    )(*[a for a, _, _ in ins], *params, *[a for a, _, _ in cts], *[a for a, _, _ in extras])


def _mod_rows(i, nct, ada, lo):
    return jnp.where(i < nct, ada[1:2, lo:lo + D], ada[0:1, lo:lo + D])


def _f_norm1(nct):
    def f(i, xs, ps, diff):
        ctx, x = xs
        nw, ada = ps
        xt = jnp.where(i < nct, ctx, x)
        return (_rms(xt, nw) * (1.0 + _mod_rows(i, nct, ada, D)) + _mod_rows(i, nct, ada, 0),)
    return f


def _f_gates_dt(i, xs, ps, diff):
    small, xc = xs
    up_f, up_b, gb_f, gb_b, dtb, alog = ps
    dot = _bdot if diff else _dot_impl_nn
    lg_f = _logsig(dot(small, up_f) + gb_f) * (1.0 / GLA_TAU)
    lg_b = _logsig(dot(small, up_b) + gb_b) * (1.0 / GLA_TAU)
    dtp = _softplus(small + dtb)
    aa = -jnp.exp(alog) * dtp
    e_f, e_b = _head_expand(32), _head_expand(64)
    n = small.shape[0]
    t = lax.broadcasted_iota(jnp.int32, (n, n), 0)
    s = lax.broadcasted_iota(jnp.int32, (n, n), 1)
    same = lax.shift_right_logical(t, 6) == lax.shift_right_logical(s, 6)
    pre_g, suf_g = ((t >= s) & same).astype(F32), ((s >= t) & same).astype(F32)
    pre_s, suf_s = (t >= s).astype(F32), (s >= t).astype(F32)
    cum_f, cum_b = _edot(aa, pre_s, "ex"), _edot(aa, suf_s, "ex")
    h = lax.broadcasted_iota(jnp.int32, (SSM_G * SSM_HPG, 128), 0)
    l = lax.broadcasted_iota(jnp.int32, (SSM_G * SSM_HPG, 128), 1)
    cum_ft = _edot(cum_f, (l == h + 32).astype(F32), "ext")
    cum_bt = _edot(cum_b, (l == h + 64).astype(F32), "ext")
    return (_edot(lg_f, pre_g, "ex"), _edot(lg_b, suf_g, "ex"), xc * _edot(dtp, e_f, "xe"), xc * _edot(dtp, e_b, "xe"),
            _edot(cum_f, e_f, "xe"), _edot(cum_b, e_b, "xe"), cum_ft, cum_bt)


def _dot_impl_nn(a, b):
    return _dot_impl(a, b, False, False)


def _f_conv(nct, tc):
    def f(i, xs, ps, diff):
        (u,) = xs
        cw, cb = ps
        n = u.shape[0]
        t = lax.broadcasted_iota(jnp.int32, (n, 1), 0)
        per = jnp.where(i < nct, tc, GRID_W)
        pos = jnp.bitwise_and(t, per - 1)
        sh = _shift if diff else _shift_impl
        acc = cb + sh(u, pos, per, -2) * cw[0:1]
        for j in range(1, 4):
            acc = acc + sh(u, pos, per, j - 2) * cw[j:j + 1]
        return (_silu(acc),)
    return f


def _f_gla_post(i, xs, ps, diff):
    ogf, ogb, r = xs
    (gw,) = ps
    o = ogf + ogb
    parts = [_rms(o[:, h * GLA_DV:(h + 1) * GLA_DV], gw) for h in range(GLA_H)]
    return (jnp.concatenate(parts, axis=1) * _silu(r),)


def _f_ssd_post(i, xs, ps, diff):
    yf, yb, xc, z = xs
    dsk8, nw = ps
    dsk = _edot(dsk8, _head_expand(0), "xe")[0:1]
    y = (yf + yb + dsk * xc) * _silu(z)
    w = SSM_INNER // SSM_G
    parts = [_rms(y[:, g * w:(g + 1) * w], nw[:, g * w:(g + 1) * w]) for g in range(SSM_G)]
    return (jnp.concatenate(parts, axis=1),)


def _f_merge(i, xs, ps, diff):
    ga, gb, ya, yb = xs
    return (jax.nn.sigmoid(ga) * ya + jax.nn.sigmoid(gb) * yb,)


def _f_res1(nct):
    def f(i, xs, ps, diff):
        ctx, x, mix = xs
        ada, nw = ps
        h2 = jnp.where(i < nct, ctx, x) + _mod_rows(i, nct, ada, 2 * D) * mix
        return (h2, _rms(h2, nw) * (1.0 + _mod_rows(i, nct, ada, 4 * D)) + _mod_rows(i, nct, ada, 3 * D))
    return f


def _f_swiglu(i, xs, ps, diff):
    (gu,) = xs
    return (_silu(gu[:, :D_FF]) * gu[:, D_FF:],)


def _loss_head(h2, dn, tgt, ada, fw, nct):
    A = h2.shape[0]
    n_tiles = A // TM

    def tile_loss(i, h2t, dnt, tg, ada_, fw_):
        h3 = h2t + ada_[0:1, 5 * D:6 * D] * dnt
        err = _rms(h3, fw_) - tg
        row = 0.5 * jnp.mean(err * err, axis=-1, keepdims=True)
        return jnp.sum(row, axis=0, keepdims=True) * jnp.where(i < nct, 0.0, 1.0)

    def body(h2_ref, dn_ref, tg_ref, ada_ref, fw_ref, loss_ref, dh_ref, ddn_ref, dada_ref, dfw_ref):
        i = pl.program_id(0)
        val, vjp = jax.vjp(functools.partial(tile_loss, i), h2_ref[...], dn_ref[...], tg_ref[...], ada_ref[...], fw_ref[...])
        dh, ddn, _, dada, dfw = vjp(jnp.ones((1, 1), F32))
        dh_ref[...] = dh
        ddn_ref[...] = ddn.astype(BF16)
        lv = jnp.broadcast_to(val, loss_ref.shape)
        for r, v in ((loss_ref, lv), (dada_ref, dada), (dfw_ref, dfw)):
            @pl.when(i == 0)
            def _(r=r, v=v):
                r[...] = v

            @pl.when(i > 0)
            def _(r=r, v=v):
                r[...] += v

    row = lambda i: (i, 0)
    return pl.pallas_call(
        body, grid=(n_tiles,),
        in_specs=[pl.BlockSpec((TM, D), row), pl.BlockSpec((TM, D), row),
                  pl.BlockSpec((TM, D), lambda i: (jnp.maximum(i - nct, 0), 0)), _whole(ada), _whole(fw)],
        out_specs=[pl.BlockSpec((8, 128), lambda i: (0, 0)), pl.BlockSpec((TM, D), row), pl.BlockSpec((TM, D), row),
                   _whole(ada), _whole(fw)],
        out_shape=[jax.ShapeDtypeStruct((8, 128), F32), jax.ShapeDtypeStruct((A, D), F32),
                   jax.ShapeDtypeStruct((A, D), BF16), jax.ShapeDtypeStruct(ada.shape, F32),
                   jax.ShapeDtypeStruct(fw.shape, F32)],
        compiler_params=_cparams(("arbitrary",)), name="loss_head",
    )(h2, dn, tgt, ada, fw)


def _chunk_of(step, n_chunks, n_ctx_chunks, rev):
    if not rev:
        return step
    return jnp.where(step < n_ctx_chunks, n_ctx_chunks - 1 - step, n_chunks - 1 - (step - n_ctx_chunks))


def _gla_step(st, q, k, v, b, rev):
    tri = _tri(GLA_C, rev).astype(F32)
    last = 0 if rev else GLA_C - 1
    outs, sts = [], []
    for h in range(GLA_H):
        kk = slice(h * GLA_DK, (h + 1) * GLA_DK)
        vv = slice(h * GLA_DV, (h + 1) * GLA_DV)
        qh, kh, vh, bh, sh = q[:, kk] * (GLA_DK ** -0.5), k[:, kk], v[:, vv], b[:, kk], st[vv]
        tot = bh[last:last + 1]
        mid = bh[GLA_C // 2:GLA_C // 2 + 1]
        att = _bdot(qh * jnp.exp(bh - mid), kh * jnp.exp(mid - bh), False, True) * tri
        outs.append(_bdot(att, vh) + _bdot(qh * jnp.exp(bh), sh, False, True))
        sts.append(sh * jnp.exp(tot) + _bdot(vh, kh * jnp.exp(tot - bh), True, False))
    return jnp.concatenate(sts, axis=0), jnp.concatenate(outs, axis=1)


_GQK, _GV = GLA_H * GLA_DK, GLA_H * GLA_DV


def _gla_fwd(proj, bg, ncc, rev, name):
    A = proj.shape[0]
    nc = A // GLA_C
    ch = lambda s: _chunk_of(s, nc, ncc, rev)

    def body(q_ref, k_ref, v_ref, b_ref, o_ref, ss_ref, st_ref):
        @pl.when(pl.program_id(0) == 0)
        def _():
            st_ref[...] = jnp.zeros_like(st_ref)

        st = st_ref[...]
        ss_ref[0] = st
        st_new, o = _gla_step(st, q_ref[...], k_ref[...], v_ref[...], b_ref[...], rev)
        o_ref[...] = o
        st_ref[...] = st_new

    return pl.pallas_call(
        body, grid=(nc,),
        in_specs=[pl.BlockSpec((GLA_C, _GQK), lambda s: (ch(s), C_Q // _GQK)),
                  pl.BlockSpec((GLA_C, _GQK), lambda s: (ch(s), C_K // _GQK)),
                  pl.BlockSpec((GLA_C, _GV), lambda s: (ch(s), C_V // _GV)),
                  pl.BlockSpec((GLA_C, _GQK), lambda s: (ch(s), 0))],
        out_specs=[pl.BlockSpec((GLA_C, _GV), lambda s: (ch(s), 0)),
                   pl.BlockSpec((1, _GV, GLA_DK), lambda s: (s, 0, 0))],
        out_shape=[jax.ShapeDtypeStruct((A, _GV), F32), jax.ShapeDtypeStruct((nc, _GV, GLA_DK), F32)],
        scratch_shapes=[pltpu.VMEM((_GV, GLA_DK), F32)],
        compiler_params=_cparams(("arbitrary",)), name=name,
    )(proj, proj, proj, bg)


def _gla_bwd(proj, lg, saved, d_o, prev, ncc, rev, name):
    A = proj.shape[0]
    nc = A // GLA_C
    st_of = lambda r: nc - 1 - r
    ch = lambda r: _chunk_of(st_of(r), nc, ncc, rev)
    qs = pl.BlockSpec((GLA_C, _GQK), lambda r: (ch(r), 0))
    vs = pl.BlockSpec((GLA_C, _GV), lambda r: (ch(r), 0))
    n_prev = 0 if prev is None else 3

    def body(*refs):
        q_ref, k_ref, v_ref, b_ref, ss_ref, do_ref = refs[:6]
        p_refs = refs[6:6 + n_prev]
        dq_ref, dk_ref, dv_ref, db_ref, ds_ref = refs[6 + n_prev:]

        @pl.when(pl.program_id(0) == 0)
        def _():
            ds_ref[...] = jnp.zeros_like(ds_ref)

        _, vjp = jax.vjp(functools.partial(_gla_step, rev=rev), ss_ref[0], q_ref[...], k_ref[...], v_ref[...], b_ref[...])
        ds, dq, dk, dv, db = vjp((ds_ref[...], do_ref[...]))
        if n_prev:
            dq, dk, dv = dq + p_refs[0][...], dk + p_refs[1][...], dv + p_refs[2][...]
        dq_ref[...], dk_ref[...], dv_ref[...], db_ref[...] = dq, dk, dv, db
        ds_ref[...] = ds

    in_specs = [
        pl.BlockSpec((GLA_C, _GQK), lambda r: (ch(r), C_Q // _GQK)),
        pl.BlockSpec((GLA_C, _GQK), lambda r: (ch(r), C_K // _GQK)),
        pl.BlockSpec((GLA_C, _GV), lambda r: (ch(r), C_V // _GV)),
        qs,
        pl.BlockSpec((1, _GV, GLA_DK), lambda r: (st_of(r), 0, 0)),
        vs,
    ] + ([qs, qs, vs] if n_prev else [])
    return pl.pallas_call(
        body, grid=(nc,), in_specs=in_specs, out_specs=[qs, qs, vs, qs],
        out_shape=[jax.ShapeDtypeStruct((A, _GQK), F32), jax.ShapeDtypeStruct((A, _GQK), F32),
                   jax.ShapeDtypeStruct((A, _GV), F32), jax.ShapeDtypeStruct((A, _GQK), F32)],
        scratch_shapes=[pltpu.VMEM((_GV, GLA_DK), F32)],
        compiler_params=_cparams(("arbitrary",)), name=name,
    )(proj, proj, proj, lg, saved, d_o, *(prev or ()))


def _ssd_step(st, x, bm, cm, cum, cum_t, rev):
    mask = _tri(SSM_C, rev)
    last = 0 if rev else SSM_C - 1
    tot = cum[last:last + 1]
    cb = _bdot(cm, bm, False, True)
    ys = []
    for e in range(SSM_HPG):
        lm = jnp.exp(jnp.where(mask, cum[:, e * SSM_P:e * SSM_P + 1] - cum_t[e:e + 1], -jnp.inf))
        ys.append(_bdot(cb * lm, x[:, e * SSM_P:(e + 1) * SSM_P]))
    y = jnp.concatenate(ys, axis=1) + _bdot(cm, st) * jnp.exp(cum)
    st_new = st * jnp.exp(tot) + _bdot(bm, x * jnp.exp(tot - cum), True, False)
    return st_new, y


_XBC_B = SSM_INNER // SSM_N
_XBC_C = _XBC_B + SSM_G


def _ssd_fwd(xd, xbc, cum, cum_t, ncc, rev, name):
    A = xd.shape[0]
    nc = A // SSM_C
    W = SSM_HPG * SSM_P
    ch = lambda s: _chunk_of(s, nc, ncc, rev)

    def body(x_ref, b_ref, c_ref, a_ref, at_ref, y_ref, ss_ref, st_ref):
        @pl.when(pl.program_id(1) == 0)
        def _():
            st_ref[...] = jnp.zeros_like(st_ref)

        st = st_ref[...]
        ss_ref[0, 0] = st
        st_new, y = _ssd_step(st, x_ref[...], b_ref[...], c_ref[...], a_ref[...], at_ref[...], rev)
        y_ref[...] = y
        st_ref[...] = st_new

    gs = pl.BlockSpec((SSM_C, W), lambda g, s: (ch(s), g))
    return pl.pallas_call(
        body, grid=(SSM_G, nc),
        in_specs=[gs, pl.BlockSpec((SSM_C, SSM_N), lambda g, s: (ch(s), _XBC_B + g)),
                  pl.BlockSpec((SSM_C, SSM_N), lambda g, s: (ch(s), _XBC_C + g)), gs,
                  pl.BlockSpec((SSM_HPG, SSM_C), lambda g, s: (g, ch(s)))],
        out_specs=[gs, pl.BlockSpec((1, 1, SSM_N, W), lambda g, s: (g, s, 0, 0))],
        out_shape=[jax.ShapeDtypeStruct((A, SSM_INNER), F32), jax.ShapeDtypeStruct((SSM_G, nc, SSM_N, W), F32)],
        scratch_shapes=[pltpu.VMEM((SSM_N, W), F32)],
        compiler_params=_cparams(("parallel", "arbitrary")), name=name,
    )(xd, xbc, xbc, cum, cum_t)


def _ssd_bwd(xd, xbc, cum, cum_t, saved, d_y, prev, ncc, rev, name):
    A = xd.shape[0]
    nc = A // SSM_C
    W = SSM_HPG * SSM_P
    st_of = lambda r: nc - 1 - r
    ch = lambda r: _chunk_of(st_of(r), nc, ncc, rev)
    gs = pl.BlockSpec((SSM_C, W), lambda g, r: (ch(r), g))
    ns = pl.BlockSpec((SSM_C, SSM_N), lambda g, r: (ch(r), g))
    ts = pl.BlockSpec((SSM_HPG, SSM_C), lambda g, r: (g, ch(r)))
    n_prev = 0 if prev is None else 2

    def body(*refs):
        x_ref, b_ref, c_ref, a_ref, at_ref, ss_ref, dy_ref = refs[:7]
        p_refs = refs[7:7 + n_prev]
        dx_ref, db_ref, dc_ref, da_ref, dat_ref, ds_ref = refs[7 + n_prev:]

        @pl.when(pl.program_id(1) == 0)
        def _():
            ds_ref[...] = jnp.zeros_like(ds_ref)

        _, vjp = jax.vjp(functools.partial(_ssd_step, rev=rev), ss_ref[0, 0], x_ref[...], b_ref[...], c_ref[...],
                         a_ref[...], at_ref[...])
        ds, dx, db, dc, da, dat = vjp((ds_ref[...], dy_ref[...]))
        if n_prev:
            db, dc = db + p_refs[0][...], dc + p_refs[1][...]
        dx_ref[...], db_ref[...], dc_ref[...], da_ref[...], dat_ref[...] = dx, db, dc, da, dat
        ds_ref[...] = ds

    in_specs = [gs, pl.BlockSpec((SSM_C, SSM_N), lambda g, r: (ch(r), _XBC_B + g)),
                pl.BlockSpec((SSM_C, SSM_N), lambda g, r: (ch(r), _XBC_C + g)), gs, ts,
                pl.BlockSpec((1, 1, SSM_N, W), lambda g, r: (g, st_of(r), 0, 0)), gs] + ([ns, ns] if n_prev else [])
    return pl.pallas_call(
        body, grid=(SSM_G, nc), in_specs=in_specs, out_specs=[gs, ns, ns, gs, ts],
        out_shape=[jax.ShapeDtypeStruct((A, SSM_INNER), F32), jax.ShapeDtypeStruct((A, SSM_G * SSM_N), F32),
                   jax.ShapeDtypeStruct((A, SSM_G * SSM_N), F32), jax.ShapeDtypeStruct((A, SSM_INNER), F32),
                   jax.ShapeDtypeStruct((SSM_G * SSM_HPG, A), F32)],
        scratch_shapes=[pltpu.VMEM((SSM_N, W), F32)],
        compiler_params=_cparams(("parallel", "arbitrary")), name=name,
    )(xd, xbc, xbc, cum, cum_t, saved, d_y, *(prev or ()))


def _ada_fwd(cc, w_ada, b_ada):
    n = w_ada.shape[1]
    tn = _pick(n, 1024, 128)

    def body(cc_ref, w_ref, b_ref, o_ref):
        o_ref[...] = _dot_impl(_silu(cc_ref[...]), w_ref[...], False, False) + b_ref[...]

    return pl.pallas_call(
        body, grid=(n // tn,),
        in_specs=[_whole(cc), pl.BlockSpec((D, tn), lambda j: (0, j)), pl.BlockSpec((1, tn), lambda j: (0, j))],
        out_specs=pl.BlockSpec((8, tn), lambda j: (0, j)),
        out_shape=jax.ShapeDtypeStruct((8, n), F32),
        compiler_params=_cparams(("parallel",)), name="ada_fwd",
    )(cc, w_ada, b_ada)


def _ada_bwd(cc, w_ada, d1, d2, d3):
    n = w_ada.shape[1]
    tn = _pick(n, 1024, 128)
    nt = n // tn

    def body(cc_ref, w_ref, d1_ref, d2_ref, d3_ref, dw_ref, db_ref, dcc_ref, ds_ref):
        j = pl.program_id(0)
        d = d1_ref[...] + d2_ref[...] + d3_ref[...]
        s = _silu(cc_ref[...])
        dw_ref[...] = _dot_impl(s, d, True, False)
        db_ref[...] = jnp.sum(d, axis=0, keepdims=True)
        p = _dot_impl(d, w_ref[...], False, True)

        @pl.when(j == 0)
        def _():
            ds_ref[...] = p

        @pl.when(j > 0)
        def _():
            ds_ref[...] += p

        @pl.when(j == nt - 1)
        def _():
            _, vjp = jax.vjp(_silu, cc_ref[...])
            dcc_ref[...] = vjp(ds_ref[...])[0]

    col = lambda j: (0, j)
    return pl.pallas_call(
        body, grid=(nt,),
        in_specs=[_whole(cc), pl.BlockSpec((D, tn), col), pl.BlockSpec((8, tn), col), pl.BlockSpec((8, tn), col),
                  pl.BlockSpec((8, tn), col)],
        out_specs=[pl.BlockSpec((D, tn), col), pl.BlockSpec((1, tn), col), _whole(cc)],
        out_shape=[jax.ShapeDtypeStruct((D, n), F32), jax.ShapeDtypeStruct((1, n), F32), jax.ShapeDtypeStruct((8, D), F32)],
        scratch_shapes=[pltpu.VMEM((8, D), F32)],
        compiler_params=_cparams(("arbitrary",)), name="ada_bwd",
    )(cc, w_ada, d1, d2, d3)


def _local_step(x, c, ctx, c_ctx, target, w):
    T, Tc = x.shape[0], ctx.shape[0]
    assert Tc == TM and T % TM == 0 and GRID_W == GLA_C and TS == SSM_C == 2 * GLA_C
    A = T + Tc
    nct = Tc // TM
    n_tm, n_ts = A // TM, A // TS
    g = {}

    cc = jnp.zeros((8, D), F32).at[0].set(c[0]).at[1].set(c_ctx)
    ada = _ada_fwd(cc, w["w_ada"], w["b_ada"])

    x_in = [_ctx(ctx, TM, nct), _lat(x, TM, nct)]
    f_norm1 = _f_norm1(nct)
    p_norm1 = [w["norm1_w"], ada]
    (h1,) = _stage_fwd(f_norm1, n_tm, x_in, p_norm1, [(A, D, BF16, TM)], "norm1")
    proj = _mm(h1, w["w_in"], trans_b=False, out_dtype=F32, name="mm_in", tn_t=1152)

    p_gd = [w["up_f"], w["up_b"], w["gla_bias_f"], w["gla_bias_b"], w["dtb"], w["alog"]]
    f_conv = _f_conv(nct, Tc)
    p_conv = [w["conv_w"], w["conv_b"]]
    in_conv = [_col(proj, TM, 3072, C_U // 3072)]
    (xbc,) = _stage_fwd(f_conv, n_tm, in_conv, p_conv, [(A, 3072, F32, TM)], "conv")
    in_gd = [_col(proj, TS, 128, C_S // 128), _col(xbc, TS, SSM_INNER, 0)]
    n_heads = SSM_G * SSM_HPG
    lg_f, lg_b, xf, xb, axf, axb, atf, atb = _stage_fwd(
        _f_gates_dt, n_ts, in_gd, p_gd,
        [(A, 512, F32, TS), (A, 512, F32, TS)] + [(A, SSM_INNER, F32, TS)] * 4 + [(A, n_heads, F32, TS, True)] * 2,
        "gates_dt")

    ncc_g, ncc_s = Tc // GLA_C, Tc // SSM_C
    ogf, sv_gf = _gla_fwd(proj, lg_f, ncc_g, False, "gla_f")
    ogb, sv_gb = _gla_fwd(proj, lg_b, ncc_g, True, "gla_b")
    ysf, sv_sf = _ssd_fwd(xf, xbc, axf, atf, ncc_s, False, "ssd_f")
    ysb, sv_sb = _ssd_fwd(xb, xbc, axb, atb, ncc_s, True, "ssd_b")

    in_gp = [_col(ogf, TS), _col(ogb, TS), _col(proj, TS, 1024, C_R // 1024)]
    p_gp = [w["gla_norm_w"]]
    (oa,) = _stage_fwd(_f_gla_post, n_ts, in_gp, p_gp, [(A, D, BF16, TS)], "gla_post")
    in_sp = [_col(ysf, TS), _col(ysb, TS), _col(xbc, TS, SSM_INNER, 0), _col(proj, TS, SSM_INNER, C_Z // SSM_INNER)]
    p_sp = [w["dsk8"], w["ssm_norm_w"]]
    (ob,) = _stage_fwd(_f_ssd_post, n_ts, in_sp, p_sp, [(A, SSM_INNER, BF16, TS)], "ssd_post")
    ya = _mm(oa, w["w_pa"], trans_b=False, out_dtype=F32, name="mm_pa")
    yb = _mm(ob, w["w_pb"], trans_b=False, out_dtype=F32, name="mm_pb")
    in_mg = [_col(proj, TS, 1024, C_GA // 1024), _col(proj, TS, 1024, C_GB // 1024), _col(ya, TS), _col(yb, TS)]
    (merged,) = _stage_fwd(_f_merge, n_ts, in_mg, [], [(A, D, BF16, TS)], "merge")
    mix = _mm(merged, w["w_out"], trans_b=False, out_dtype=F32, name="mm_out")

    f_res1 = _f_res1(nct)
    in_r1 = x_in + [_col(mix, TM)]
    p_r1 = [ada, w["norm2_w"]]
    h2, hm2 = _stage_fwd(f_res1, n_tm, in_r1, p_r1, [(A, D, F32, TM), (A, D, BF16, TM)], "res1")
    gu = _mm(hm2, w["w_gu"], trans_b=False, out_dtype=F32, name="mm_gu", tn_t=1408)
    in_sw = [_col(gu, TS)]
    (act,) = _stage_fwd(_f_swiglu, n_ts, in_sw, [], [(A, D_FF, BF16, TS)], "swiglu")
    dn = _mm(act, w["w_down"], trans_b=False, out_dtype=F32, name="mm_down")

    loss_blk, d_h2a, d_dn, d_ada3, g["final_norm_w"] = _loss_head(h2, dn, target, ada, w["final_norm_w"], nct)
    g["w_down"] = _mm_tn(act, d_dn, name="dw_down", tr_t=1408)
    d_act = _mm(d_dn, w["w_down"], trans_b=True, out_dtype=F32, name="dx_down", tn_t=1408)
    (d_gu,) = _stage_bwd(_f_swiglu, n_ts, in_sw, [], [_col(d_act, TS)], lambda i, t: t, [(BF16, None)], "swiglu_b")
    g["w_gu"] = _mm_tn(hm2, d_gu, name="dw_gu")
    d_hm2 = _mm(d_gu, w["w_gu"], trans_b=True, out_dtype=F32, name="dx_gu")
    d_x1, d_mix, d_ada2, g["norm2_w"] = _stage_bwd(
        f_res1, n_tm, in_r1, p_r1, [_col(d_h2a, TM), _col(d_hm2, TM)], lambda i, t: t,
        [None, (F32, None), (BF16, None)], "res1_b")
    g["w_out"] = _mm_tn(merged, d_mix, name="dw_out")
    d_merged = _mm(d_mix, w["w_out"], trans_b=True, out_dtype=F32, name="dx_out")
    d_ga, d_gb, d_ya, d_yb = _stage_bwd(
        _f_merge, n_ts, in_mg, [], [_col(d_merged, TS)], lambda i, t: t, [(BF16, None)] * 4, "merge_b")
    g["w_pa"] = _mm_tn(oa, d_ya, name="dw_pa")
    g["w_pb"] = _mm_tn(ob, d_yb, name="dw_pb")
    d_oa = _mm(d_ya, w["w_pa"], trans_b=True, out_dtype=F32, name="dx_pa")
    d_ob = _mm(d_yb, w["w_pb"], trans_b=True, out_dtype=F32, name="dx_pb")
    d_og, d_r, g["gla_norm_w"] = _stage_bwd(
        _f_gla_post, n_ts, in_gp, p_gp, [_col(d_oa, TS)], lambda i, t: t, [(F32, None), None, (BF16, None)], "gla_post_b")
    d_ys, d_xs_skip, d_z, g["dsk8"], g["ssm_norm_w"] = _stage_bwd(
        _f_ssd_post, n_ts, in_sp, p_sp, [_col(d_ob, TS)], lambda i, t: t,
        [(F32, None), None, (F32, None), (BF16, None)], "ssd_post_b")

    dq, dk, dv, d_lgf = _gla_bwd(proj, lg_f, sv_gf, d_og, None, ncc_g, False, "gla_f_b")
    dq, dk, dv, d_lgb = _gla_bwd(proj, lg_b, sv_gb, d_og, (dq, dk, dv), ncc_g, True, "gla_b_b")
    d_xf, d_bm, d_cm, d_axf, d_atf = _ssd_bwd(xf, xbc, axf, atf, sv_sf, d_ys, None, ncc_s, False, "ssd_f_b")
    d_xb, d_bm, d_cm, d_axb, d_atb = _ssd_bwd(xb, xbc, axb, atb, sv_sb, d_ys, (d_bm, d_cm), ncc_s, True, "ssd_b_b")

    cts_gd = [_col(a, TS) for a in (d_lgf, d_lgb, d_xf, d_xb, d_axf, d_axb)]
    cts_gd += [(a, (n_heads, TS), lambda i: (0, i)) for a in (d_atf, d_atb)]
    d_small, d_xs_dt, g["up_f"], g["up_b"], g["gla_bias_f"], g["gla_bias_b"], g["dtb"], g["alog"] = _stage_bwd(
        _f_gates_dt, n_ts, in_gd, p_gd, cts_gd, lambda i, t: t, [(BF16, None), (F32, None)], "gates_dt_b")
    cts_conv = [_col(d_xs_skip, TM), _col(d_xs_dt, TM), _col(d_bm, TM), _col(d_cm, TM)]
    d_u, g["conv_w"], g["conv_b"] = _stage_bwd(
        f_conv, n_tm, in_conv, p_conv, cts_conv,
        lambda i, t: [jnp.concatenate([t[0] + t[1], t[2], t[3]], axis=1)], [(BF16, None)], "conv_b")

    d_proj = jnp.concatenate([dq.astype(BF16), dk.astype(BF16), dv.astype(BF16), d_r, d_u, d_z, d_ga, d_gb, d_small], axis=1)
    g["w_in"] = _mm_tn(h1, d_proj, name="dw_in", tn_t=1152)
    d_h1 = _mm(d_proj, w["w_in"], trans_b=True, out_dtype=F32, name="dx_in", tk_t=1152)
    grad_x, g["norm1_w"], d_ada1 = _stage_bwd(
        f_norm1, n_tm, x_in, p_norm1, [_col(d_h1, TM)], lambda i, t: t,
        [None, (F32, (d_x1, (TM, D), x_in[1][2]))], "norm1_b")
    g["w_ada"], g["b_ada"], d_cc = _ada_bwd(cc, w["w_ada"], d_ada1, d_ada2, d_ada3)
    g["c_ctx"] = d_cc[1]
    return loss_blk, grad_x, g


def _exchange(x, a2a, name):
    shp = x.shape[1:] if a2a else x.shape

    def body(x_ref, o_ref, send_sems, recv_sems, local_sem):
        mx, my, mc = lax.axis_index("x"), lax.axis_index("y"), lax.axis_index("c")
        me = 4 * mx + 2 * my + mc
        copies = []
        for k in range(1, N_DEV):
            px = 1 - mx if k & 4 else mx
            py = 1 - my if k & 2 else my
            pc = 1 - mc if k & 1 else mc
            src = x_ref.at[4 * px + 2 * py + pc] if a2a else x_ref
            cp = pltpu.make_async_remote_copy(
                src_ref=src, dst_ref=o_ref.at[me], send_sem=send_sems.at[k - 1], recv_sem=recv_sems.at[k - 1],
                device_id=(px, py, pc), device_id_type=pl.DeviceIdType.MESH)
            cp.start()
            copies.append(cp)
        own = pltpu.make_async_copy(x_ref.at[me] if a2a else x_ref, o_ref.at[me], local_sem)
        own.start()
        for cp in copies:
            cp.wait()
        own.wait()

    return pl.pallas_call(
        body, out_shape=jax.ShapeDtypeStruct((N_DEV,) + tuple(shp), x.dtype),
        in_specs=[pl.BlockSpec(memory_space=pl.ANY)], out_specs=pl.BlockSpec(memory_space=pl.ANY),
        scratch_shapes=[pltpu.SemaphoreType.DMA((N_DEV - 1,)), pltpu.SemaphoreType.DMA((N_DEV - 1,)),
                        pltpu.SemaphoreType.DMA(())],
        name=name,
    )(x)


def _all_gather(x, name):
    def body(x_ref, o_ref, send_sems, recv_sems, local_sem):
        mx, my, mc = lax.axis_index("x"), lax.axis_index("y"), lax.axis_index("c")
        me, sibling = (mx, my, mc), (mx, my, 1 - mc)
        chips = [(1 - mx, my), (mx, 1 - my), (1 - mx, 1 - my)]

        def slot(px, py, pc):
            return o_ref.at[4 * px + 2 * py + pc]

        def copy(k, block, to, src=None):
            return pltpu.make_async_remote_copy(
                src_ref=slot(*block) if src is None else src, dst_ref=slot(*block),
                send_sem=send_sems.at[k], recv_sem=recv_sems.at[k], device_id=to, device_id_type=pl.DeviceIdType.MESH)

        own = pltpu.make_async_copy(x_ref, slot(*me), local_sem)
        own.start()
        first = [copy(0, me, sibling, src=x_ref)] + [copy(1 + j, me, (*chip, mc), src=x_ref) for j, chip in enumerate(chips)]
        for cp in first:
            cp.start()
        passed = [copy(4 + j, (*chip, mc), sibling) for j, chip in enumerate(chips)]
        for j, chip in enumerate(chips):
            copy(1 + j, (*chip, mc), me).wait_recv()
            passed[j].start()
        copy(0, sibling, me).wait_recv()
        for j, chip in enumerate(chips):
            copy(4 + j, (*chip, 1 - mc), me).wait_recv()
        for cp in first + passed:
            cp.wait_send()
        own.wait()

    return pl.pallas_call(
        body, out_shape=jax.ShapeDtypeStruct((N_DEV,) + tuple(x.shape), x.dtype),
        in_specs=[pl.BlockSpec(memory_space=pl.ANY)], out_specs=pl.BlockSpec(memory_space=pl.ANY),
        scratch_shapes=[pltpu.SemaphoreType.DMA((N_DEV - 1,)), pltpu.SemaphoreType.DMA((N_DEV - 1,)),
                        pltpu.SemaphoreType.DMA(())],
        name=name,
    )(x)


def _adamw_math(w, gr, m, v):
    m = ADAM_B1 * m + (1.0 - ADAM_B1) * gr
    v = ADAM_B2 * v + (1.0 - ADAM_B2) * (gr * gr)
    m_hat = m / np.float32(1.0 - ADAM_B1 ** ADAM_STEP)
    v_hat = v / np.float32(1.0 - ADAM_B2 ** ADAM_STEP)
    delta = -ADAM_LR * (m_hat / (jnp.sqrt(v_hat) + ADAM_EPS) + ADAM_WD * w)
    return delta, m, v


def _sum_adamw(parts, w, m, v, name):
    R, C = w.shape
    tr = _pick(R, max(16, (2 << 20) // (4 * C) // 16 * 16), 16)

    def body(p_ref, w_ref, m_ref, v_ref, g_ref, d_ref, mo_ref, vo_ref):
        gr = p_ref[0].astype(F32)
        for k in range(1, N_DEV):
            gr = gr + p_ref[k].astype(F32)
        g_ref[...] = gr
        d_ref[...], mo_ref[...], vo_ref[...] = _adamw_math(w_ref[...], gr, m_ref[...], v_ref[...])

    row = pl.BlockSpec((tr, C), lambda i: (i, 0))
    return pl.pallas_call(
        body, grid=(R // tr,),
        in_specs=[pl.BlockSpec((N_DEV, tr, C), lambda i: (0, i, 0)), row, row, row],
        out_specs=[row] * 4, out_shape=[jax.ShapeDtypeStruct((R, C), F32)] * 4,
        compiler_params=_cparams(("parallel",)), name=name,
    )(parts, w, m, v)


def _sum8(parts, name):
    _, R, C = parts.shape

    def body(p_ref, g_ref):
        gr = p_ref[0]
        for k in range(1, N_DEV):
            gr = gr + p_ref[k]
        g_ref[...] = gr

    return pl.pallas_call(body, out_shape=jax.ShapeDtypeStruct((R, C), F32), name=name)(parts)


def _adamw(w, gr, m, v, name):
    def body(w_ref, g_ref, m_ref, v_ref, d_ref, mo_ref, vo_ref):
        d_ref[...], mo_ref[...], vo_ref[...] = _adamw_math(w_ref[...], g_ref[...], m_ref[...], v_ref[...])

    return pl.pallas_call(body, out_shape=[jax.ShapeDtypeStruct(w.shape, F32)] * 3, name=name)(w, gr, m, v)


def _to_padded(w_in):
    z = jnp.zeros(w_in.shape[:-1] + (32,), w_in.dtype)
    return jnp.concatenate([w_in[..., 0:3072], w_in[..., 5152:8224], w_in[..., 3104:5152], w_in[..., 8288:10336],
                            w_in[..., 3072:3104], w_in[..., 8224:8288], z], axis=-1)


def _from_padded(p):
    return jnp.concatenate([p[..., 0:3072], p[..., 10240:10272], p[..., 6144:8192], p[..., 3072:6144],
                            p[..., 10272:10336], p[..., 8192:10240]], axis=-1)


def _unshard_cols(gathered):
    n, r, c = gathered.shape
    return jnp.transpose(gathered, (1, 0, 2)).reshape(r, n * c)


def _shard_cols(full):
    r, nc = full.shape
    return jnp.transpose(full.reshape(r, N_DEV, nc // N_DEV), (1, 0, 2))


def _lanes(vec, lo):
    return jnp.zeros((1, 128), F32).at[:, lo:lo + vec.shape[1]].set(vec)


_SMALL = (("b_ada", 6 * D), ("c_ctx", D), ("norm1_w", D), ("gla_bias_f", 512), ("gla_bias_b", 512), ("gla_norm_w", 256),
          ("conv_b", 3072), ("dt_bias_f", 32), ("dt_bias_b", 32), ("a_log_f", 32), ("a_log_b", 32), ("d_skip", 32),
          ("ssm_norm_w", 2048), ("norm2_w", D), ("final_norm_w", D))
_SHARDED_SMALL = (("gla_up_f", 16 * 512), ("gla_up_b", 16 * 512), ("conv_w", 4 * 3072))


def _pack(vals, names):
    flat = jnp.concatenate([vals[n].reshape(-1).astype(F32) for n, _ in names])
    pad = (-flat.shape[0]) % 1024
    return jnp.concatenate([flat, jnp.zeros((pad,), F32)]).reshape(-1, 128)


def _unpack(packed, names, shapes):
    flat, out, off = packed.reshape(-1), {}, 0
    for n, size in names:
        out[n] = flat[off:off + size].reshape(shapes[n])
        off += size
    return out


def kernel(x, c, ctx, c_ctx, w_ada, b_ada, norm1_w, w_in, gla_up_f, gla_bias_f, gla_up_b, gla_bias_b, gla_norm_w, conv_w, conv_b, dt_bias_f, dt_bias_b, a_log_f, a_log_b, d_skip, ssm_norm_w, w_pa, w_pb, w_out, norm2_w, w_gate, w_up, w_down, final_norm_w, loss_target, m_c_ctx, m_w_ada, m_b_ada, m_norm1_w, m_w_in, m_gla_up_f, m_gla_bias_f, m_gla_up_b, m_gla_bias_b, m_gla_norm_w, m_conv_w, m_conv_b, m_dt_bias_f, m_dt_bias_b, m_a_log_f, m_a_log_b, m_d_skip, m_ssm_norm_w, m_w_pa, m_w_pb, m_w_out, m_norm2_w, m_w_gate, m_w_up, m_w_down, m_final_norm_w, v_c_ctx, v_w_ada, v_b_ada, v_norm1_w, v_w_in, v_gla_up_f, v_gla_bias_f, v_gla_up_b, v_gla_bias_b, v_gla_norm_w, v_conv_w, v_conv_b, v_dt_bias_f, v_dt_bias_b, v_a_log_f, v_a_log_b, v_d_skip, v_ssm_norm_w, v_w_pa, v_w_pb, v_w_out, v_norm2_w, v_w_gate, v_w_up, v_w_down, v_final_norm_w):
    args = dict(locals())
    me = 4 * lax.axis_index("x") + 2 * lax.axis_index("y") + lax.axis_index("c")

    def gather16(a, name):
        return _all_gather(a[0].astype(BF16), name)

    full = {
        "w_ada": _unshard_cols(gather16(w_ada, "ag_w_ada")),
        "w_in": _to_padded(_unshard_cols(gather16(w_in, "ag_w_in"))),
        "w_pa": gather16(w_pa, "ag_w_pa").reshape(D, D),
        "w_pb": gather16(w_pb, "ag_w_pb").reshape(SSM_INNER, D),
        "w_out": gather16(w_out, "ag_w_out").reshape(D, D),
        "w_gu": jnp.concatenate([_unshard_cols(gather16(w_gate, "ag_w_gate")), _unshard_cols(gather16(w_up, "ag_w_up"))], axis=1),
        "w_down": gather16(w_down, "ag_w_down").reshape(D_FF, D),
    }
    sm = _exchange(jnp.concatenate([gla_up_f.reshape(-1), gla_up_b.reshape(-1), conv_w.reshape(-1)]).reshape(-1, 128), False, "ag_small")
    sm = sm.reshape(N_DEV, -1)
    up_f = _unshard_cols(sm[:, 0:1024].reshape(N_DEV, 16, 64))
    up_b = _unshard_cols(sm[:, 1024:2048].reshape(N_DEV, 16, 64))
    full["conv_w"] = _unshard_cols(sm[:, 2048:3584].reshape(N_DEV, 4, 384))
    full["up_f"] = jnp.zeros((128, 512), F32).at[0:16].set(up_f)
    full["up_b"] = jnp.zeros((128, 512), F32).at[16:32].set(up_b)
    full["dtb"] = _lanes(dt_bias_f, 32) + _lanes(dt_bias_b, 64)
    full["alog"] = _lanes(a_log_f, 32) + _lanes(a_log_b, 64)
    full["dsk8"] = jnp.zeros((8, 128), F32).at[0:1, 0:32].set(d_skip)
    for n in ("b_ada", "norm1_w", "gla_bias_f", "gla_bias_b", "gla_norm_w", "conv_b", "ssm_norm_w", "norm2_w"):
        full[n] = args[n]
    full["final_norm_w"] = final_norm_w.reshape(1, D)

    loss_blk, grad_x, g = _local_step(x[0], c, ctx[0], c_ctx, loss_target[0], full)

    gs = dict(g)
    gs["dt_bias_f"], gs["dt_bias_b"] = g["dtb"][:, 32:64], g["dtb"][:, 64:96]
    gs["a_log_f"], gs["a_log_b"] = g["alog"][:, 32:64], g["alog"][:, 64:96]
    gs["d_skip"] = g["dsk8"][0:1, 0:32]
    gs["gla_up_f"], gs["gla_up_b"] = g["up_f"][0:16], g["up_b"][16:32]
    gs["loss"] = loss_blk[0:1, 0:1]
    names = _SMALL + _SHARDED_SMALL + (("loss", 1),)
    shapes = {n: (args[n].shape if n in args else (1, 1)) for n, _ in names}
    shapes.update({"gla_up_f": (16, 512), "gla_up_b": (16, 512), "conv_w": (4, 3072)})
    red = _unpack(_sum8(_exchange(_pack(gs, names), False, "ag_small_grads"), "sum_small_grads"), names, shapes)
    loss = red["loss"].reshape(())
    grads = {n: red[n] for n, _ in _SMALL}
    grads["gla_up_f"] = lax.dynamic_slice(red["gla_up_f"], (0, me * 64), (16, 64))[None]
    grads["gla_up_b"] = lax.dynamic_slice(red["gla_up_b"], (0, me * 64), (16, 64))[None]
    grads["conv_w"] = lax.dynamic_slice(red["conv_w"], (0, me * 384), (4, 384))[None]
    upd_names = tuple((n, s) for n, s in _SMALL) + (("gla_up_f", 1024), ("gla_up_b", 1024), ("conv_w", 1536))
    pk = lambda prefix, src: _pack({n: src[prefix + n] for n, _ in upd_names}, upd_names)
    d_s, m_s, v_s = _adamw(pk("", args), _pack(grads, upd_names), pk("m_", args), pk("v_", args), "adamw_small")
    upd_shapes = {n: args[n].shape for n, _ in upd_names}
    delta = _unpack(d_s, upd_names, upd_shapes)
    new_m = _unpack(m_s, upd_names, upd_shapes)
    new_v = _unpack(v_s, upd_names, upd_shapes)

    g_in = _from_padded(g["w_in"])
    big = {
        "w_ada": _shard_cols(g["w_ada"]), "w_in": _shard_cols(g_in),
        "w_pa": g["w_pa"].reshape(N_DEV, D // N_DEV, D), "w_pb": g["w_pb"].reshape(N_DEV, SSM_INNER // N_DEV, D),
        "w_out": g["w_out"].reshape(N_DEV, D // N_DEV, D),
        "w_gate": _shard_cols(g["w_gu"][:, :D_FF]), "w_up": _shard_cols(g["w_gu"][:, D_FF:]),
        "w_down": g["w_down"].reshape(N_DEV, D_FF // N_DEV, D),
    }
    for n, parts in big.items():
        got = _exchange(parts.astype(BF16), True, "a2a_" + n)
        gr, dl, mo, vo = _sum_adamw(got, args[n][0], args["m_" + n][0], args["v_" + n][0], "adamw_" + n)
        grads[n], delta[n], new_m[n], new_v[n] = gr[None], dl[None], mo[None], vo[None]

    order = ["c_ctx", "w_ada", "b_ada", "norm1_w", "w_in", "gla_up_f", "gla_bias_f", "gla_up_b", "gla_bias_b", "gla_norm_w",
             "conv_w", "conv_b", "dt_bias_f", "dt_bias_b", "a_log_f", "a_log_b", "d_skip", "ssm_norm_w", "w_pa", "w_pb",
             "w_out", "norm2_w", "w_gate", "w_up", "w_down", "final_norm_w"]
    fix = lambda d: [d[n].reshape(args[n].shape) for n in order]
    return (loss, grad_x[None], *fix(grads), *fix(delta), *fix(new_m), *fix(new_v))
```

```python
import functools

import jax
import jax.numpy as jnp
import numpy as np
from jax import lax
from jax.experimental import pallas as pl
from jax.experimental.pallas import tpu as pltpu

F32 = jnp.float32
BF16 = jnp.bfloat16
N_DEV = 8
D = 1024
EPS = 1e-6
GRID_W = 64
GLA_H, GLA_DK, GLA_DV = 4, 128, 256
GLA_C = 64
GLA_TAU = 16.0
SSM_G, SSM_HPG, SSM_P, SSM_N = 4, 8, 64, 128
SSM_C = 128
SSM_INNER = 2048
D_FF = 2816
D_IN = 10336
D_INP = 10368
C_Q, C_K, C_V, C_R, C_U, C_Z, C_GA, C_GB, C_S = 0, 512, 1024, 2048, 3072, 6144, 8192, 9216, 10240
TM = 256
TS = 128

ADAM_LR, ADAM_B1, ADAM_B2, ADAM_EPS, ADAM_WD, ADAM_STEP = 0.001, 0.9, 0.999, 1e-08, 0.01, 10
VMEM_LIMIT = 56 << 20


def _cparams(sem):
    return pltpu.CompilerParams(dimension_semantics=sem, vmem_limit_bytes=VMEM_LIMIT)


def _pick(n, target, mult):
    best = None
    for t in range(mult, min(n, target) + 1, mult):
        if n % t == 0:
            best = t
    return best if best is not None else n


def _dot_impl(a, b, ta, tb):
    dims = (((0 if ta else 1,), (1 if tb else 0,)), ((), ()))
    return lax.dot_general(a.astype(BF16), b.astype(BF16), dims, preferred_element_type=F32)


@functools.partial(jax.custom_vjp, nondiff_argnums=(2, 3))
def _bdot(a, b, ta=False, tb=False):
    return _dot_impl(a, b, ta, tb)


def _bdot_fwd(a, b, ta, tb):
    return _dot_impl(a, b, ta, tb), (a, b)


def _bdot_bwd(ta, tb, res, g):
    a, b = res
    if not ta and not tb:
        return _dot_impl(g, b, False, True), _dot_impl(a, g, True, False)
    if not ta and tb:
        return _dot_impl(g, b, False, False), _dot_impl(g, a, True, False)
    if ta and not tb:
        return _dot_impl(b, g, False, True), _dot_impl(a, g, False, False)
    raise NotImplementedError


_bdot.defvjp(_bdot_fwd, _bdot_bwd)


_NN = (((1,), (0,)), ((), ()))
_NT = (((1,), (1,)), ((), ()))
_TN = (((0,), (0,)), ((), ()))


def _dg2(x, e, x_is_lhs, dims):
    hi = x.astype(BF16)
    lo = (x - hi.astype(F32)).astype(BF16)
    e = e.astype(BF16)
    if x_is_lhs:
        return (lax.dot_general(hi, e, dims, preferred_element_type=F32)
                + lax.dot_general(lo, e, dims, preferred_element_type=F32))
    return (lax.dot_general(e, hi, dims, preferred_element_type=F32)
            + lax.dot_general(e, lo, dims, preferred_element_type=F32))


_EDOT_FWD = {"xe": (True, _NN), "ex": (False, _NN), "ext": (False, _NT)}
_EDOT_BWD = {"xe": (True, _NT), "ex": (False, _TN), "ext": (True, _TN)}


@functools.partial(jax.custom_vjp, nondiff_argnums=(2,))
def _edot(x, e, mode):
    return _dg2(x, e, *_EDOT_FWD[mode])


def _edot_fwd(x, e, mode):
    return _dg2(x, e, *_EDOT_FWD[mode]), e


def _edot_bwd(mode, e, g):
    return _dg2(g, e, *_EDOT_BWD[mode]), None


_edot.defvjp(_edot_fwd, _edot_bwd)


def _shift_impl(u, pos, per, s):
    n = u.shape[0]
    rolled = u if s == 0 else pltpu.roll(u, (-s) % n, 0)
    ok = (pos + s >= 0) & (pos + s < per)
    return jnp.where(ok, rolled, 0.0)


@functools.partial(jax.custom_vjp, nondiff_argnums=(3,))
def _shift(u, pos, per, s):
    return _shift_impl(u, pos, per, s)


def _shift_fwd(u, pos, per, s):
    return _shift_impl(u, pos, per, s), (pos, per)


def _shift_bwd(s, res, g):
    pos, per = res
    return _shift_impl(g, pos, per, -s), None, None


_shift.defvjp(_shift_fwd, _shift_bwd)


def _rms(x, w):
    return x * lax.rsqrt(jnp.mean(x * x, axis=-1, keepdims=True) + EPS) * w


def _silu(x):
    return x * jax.nn.sigmoid(x)


def _softplus(x):
    return jnp.maximum(x, 0.0) + jnp.log(1.0 + jnp.exp(-jnp.abs(x)))


def _logsig(x):
    return jnp.minimum(x, 0.0) - jnp.log(1.0 + jnp.exp(-jnp.abs(x)))


def _tri(n, rev):
    t = lax.broadcasted_iota(jnp.int32, (n, n), 0)
    s = lax.broadcasted_iota(jnp.int32, (n, n), 1)
    return (s >= t) if rev else (t >= s)


def _head_expand(first_lane):
    l = lax.broadcasted_iota(jnp.int32, (128, SSM_INNER), 0)
    c = lax.broadcasted_iota(jnp.int32, (128, SSM_INNER), 1)
    return (l == first_lane + lax.shift_right_logical(c, 6)).astype(F32)


def _exchange_ops(x_ref, o_ref, send_sems, recv_sems, local_sem, a2a):
    mx, my, mc = lax.axis_index("x"), lax.axis_index("y"), lax.axis_index("c")
    me = 4 * mx + 2 * my + mc
    ops = []
    for k in range(1, N_DEV):
        px = 1 - mx if k & 4 else mx
        py = 1 - my if k & 2 else my
        pc = 1 - mc if k & 1 else mc
        ops.append(pltpu.make_async_remote_copy(
            src_ref=x_ref.at[4 * px + 2 * py + pc] if a2a else x_ref, dst_ref=o_ref.at[me],
            send_sem=send_sems.at[k - 1], recv_sem=recv_sems.at[k - 1],
            device_id=(px, py, pc), device_id_type=pl.DeviceIdType.MESH))
    ops.append(pltpu.make_async_copy(x_ref.at[me] if a2a else x_ref, o_ref.at[me], local_sem))
    return ops


def _call(body, *, grid, in_specs, out_specs, out_shape, scratch=(), sem, name, args, hosted=()):
    n_in, n_out, n_s, n_h = len(in_specs), len(out_specs), len(scratch), len(hosted)
    if not n_h:
        return pl.pallas_call(body, grid=grid, in_specs=list(in_specs), out_specs=list(out_specs),
                              out_shape=list(out_shape), scratch_shapes=list(scratch),
                              compiler_params=_cparams(sem), name=name)(*args)
    b0, c0 = n_in + n_h, n_in + n_h + n_out
    d0 = c0 + n_h

    def wrapped(*refs):
        hx, ho = refs[n_in:b0], refs[c0:d0]
        send_sems, recv_sems, local_sems = refs[d0 + n_s:]
        ids = [pl.program_id(ax) for ax in range(len(grid))]
        first = functools.reduce(jnp.logical_and, [i == 0 for i in ids])
        last = functools.reduce(jnp.logical_and, [i == n - 1 for i, n in zip(ids, grid)])

        def ops():
            return [op for n, (_, a2a) in enumerate(hosted)
                    for op in _exchange_ops(hx[n], ho[n], send_sems.at[n], recv_sems.at[n], local_sems.at[n], a2a)]

        @pl.when(first)
        def _():
            for op in ops():
                op.start()

        body(*refs[:n_in], *refs[b0:c0], *refs[d0:d0 + n_s])

        @pl.when(last)
        def _():
            for op in ops():
                op.wait()

    hbm = pl.BlockSpec(memory_space=pl.ANY)
    h_shapes = [jax.ShapeDtypeStruct((N_DEV,) + tuple(x.shape[1:] if a2a else x.shape), x.dtype) for x, a2a in hosted]
    return pl.pallas_call(
        wrapped, grid=grid, in_specs=list(in_specs) + [hbm] * n_h, out_specs=list(out_specs) + [hbm] * n_h,
        out_shape=list(out_shape) + h_shapes,
        scratch_shapes=list(scratch) + [pltpu.SemaphoreType.DMA((n_h, N_DEV - 1)), pltpu.SemaphoreType.DMA((n_h, N_DEV - 1)),
                                        pltpu.SemaphoreType.DMA((n_h,))],
        compiler_params=_cparams(("arbitrary",) * len(grid)), name=name,
    )(*args, *[x for x, _ in hosted])


def _mm(a, b, *, trans_b, out_dtype, name, tm_t=768, tn_t=1024, tk_t=2816, hosted=()):
    M, K = a.shape
    N = b.shape[0] if trans_b else b.shape[1]
    tm, tn, tk = _pick(M, tm_t, 8), _pick(N, tn_t, 128), _pick(K, tk_t, 128)
    nk = K // tk
    dims = (((1,), (1,)), ((), ())) if trans_b else (((1,), (0,)), ((), ()))

    def body(a_ref, b_ref, o_ref, acc_ref):
        k = pl.program_id(2)
        p = lax.dot_general(a_ref[...], b_ref[...], dims, preferred_element_type=F32)

        @pl.when(k == 0)
        def _():
            acc_ref[...] = p

        @pl.when(k > 0)
        def _():
            acc_ref[...] += p

        @pl.when(k == nk - 1)
        def _():
            o_ref[...] = acc_ref[...].astype(out_dtype)

    b_spec = pl.BlockSpec((tn, tk), lambda j, i, k: (j, k)) if trans_b else pl.BlockSpec((tk, tn), lambda j, i, k: (k, j))
    outs = _call(
        body, grid=(N // tn, M // tm, nk),
        in_specs=[pl.BlockSpec((tm, tk), lambda j, i, k: (i, k)), b_spec],
        out_specs=[pl.BlockSpec((tm, tn), lambda j, i, k: (i, j))],
        out_shape=[jax.ShapeDtypeStruct((M, N), out_dtype)],
        scratch=[pltpu.VMEM((tm, tn), F32)],
        sem=("parallel", "parallel", "arbitrary"), name=name, args=(a, b), hosted=hosted)
    return outs if hosted else outs[0]


def _mm_tn(a, b, *, name, tm_t=768, tr_t=1024, tn_t=1408):
    M, R = a.shape
    N = b.shape[1]
    tm, tr, tn = _pick(M, tm_t, 8), _pick(R, tr_t, 128), _pick(N, tn_t, 128)

    def body(a_ref, b_ref, o_ref):
        m = pl.program_id(2)
        p = lax.dot_general(a_ref[...], b_ref[...], (((0,), (0,)), ((), ())), preferred_element_type=F32)

        @pl.when(m == 0)
        def _():
            o_ref[...] = p

        @pl.when(m > 0)
        def _():
            o_ref[...] += p

    return pl.pallas_call(
        body, grid=(R // tr, N // tn, M // tm),
        in_specs=[pl.BlockSpec((tm, tr), lambda r, j, m: (m, r)), pl.BlockSpec((tm, tn), lambda r, j, m: (m, j))],
        out_specs=pl.BlockSpec((tr, tn), lambda r, j, m: (r, j)),
        out_shape=jax.ShapeDtypeStruct((R, N), F32),
        compiler_params=_cparams(("parallel", "parallel", "arbitrary")), name=name,
    )(a, b)


def _col(arr, tm, width=None, col=0):
    width = arr.shape[1] if width is None else width
    return (arr, (tm, width), lambda i: (i, col))


def _lat(arr, tm, nct):
    return (arr, (tm, arr.shape[1]), lambda i: (jnp.maximum(i - nct, 0), 0))


def _ctx(arr, tm, nct):
    return (arr, (tm, arr.shape[1]), lambda i: (jnp.minimum(i, nct - 1), 0))


def _whole(p):
    return pl.BlockSpec(p.shape, lambda i, nd=p.ndim: (0,) * nd)


def _stage_fwd(f, n_tiles, ins, params, outs, name):
    ni, npar = len(ins), len(params)
    o_specs = [pl.BlockSpec((o[1], o[3]), lambda i: (0, i)) if len(o) > 4 else pl.BlockSpec((o[3], o[1]), lambda i: (i, 0))
               for o in outs]
    o_shapes = [jax.ShapeDtypeStruct((o[1], o[0]) if len(o) > 4 else (o[0], o[1]), o[2]) for o in outs]

    def body(*refs):
        i = pl.program_id(0)
        xs = [r[...].astype(F32) for r in refs[:ni]]
        ps = [r[...] for r in refs[ni:ni + npar]]
        for r, v in zip(refs[ni + npar:], f(i, xs, ps, False)):
            r[...] = v.astype(r.dtype)

    return pl.pallas_call(
        body, grid=(n_tiles,),
        in_specs=[pl.BlockSpec(bs, fn) for _, bs, fn in ins] + [_whole(p) for p in params],
        out_specs=o_specs, out_shape=o_shapes,
        compiler_params=_cparams(("parallel",)), name=name,
    )(*[a for a, _, _ in ins], *params)


def _stage_bwd(f, n_tiles, ins, params, cts, ct_fn, dins, name, hosted=()):
    ni, npar, nc = len(ins), len(params), len(cts)
    want = [k for k, d in enumerate(dins) if d is not None]
    extras = [dins[k][1] for k in want if dins[k][1] is not None]

    def body(*refs):
        i = pl.program_id(0)
        xs = [r[...].astype(F32) for r in refs[:ni]]
        ps = [r[...] for r in refs[ni:ni + npar]]
        ct_tiles = [r[...].astype(F32) for r in refs[ni + npar:ni + npar + nc]]
        ex_refs = list(refs[ni + npar + nc:ni + npar + nc + len(extras)])
        out_refs = refs[ni + npar + nc + len(extras):]
        _, vjp = jax.vjp(lambda xs_, ps_: tuple(f(i, xs_, ps_, True)), xs, ps)
        dxs, dps = vjp(tuple(ct_fn(i, ct_tiles)))
        for n, k in enumerate(want):
            v = dxs[k]
            if dins[k][1] is not None:
                v = v + ex_refs.pop(0)[...].astype(F32)
            out_refs[n][...] = v.astype(out_refs[n].dtype)
        for r, v in zip(out_refs[len(want):], dps):
            @pl.when(i == 0)
            def _(r=r, v=v):
                r[...] = v

            @pl.when(i > 0)
            def _(r=r, v=v):
                r[...] += v

    return _call(
        body, grid=(n_tiles,),
        in_specs=([pl.BlockSpec(bs, fn) for _, bs, fn in ins] + [_whole(p) for p in params]
                  + [pl.BlockSpec(bs, fn) for _, bs, fn in cts] + [pl.BlockSpec(bs, fn) for _, bs, fn in extras]),
        out_specs=([pl.BlockSpec(ins[k][1], lambda i, fn=ins[k][2]: (fn(i)[0], 0)) for k in want]
                   + [_whole(p) for p in params]),
        out_shape=([jax.ShapeDtypeStruct((ins[k][0].shape[0], ins[k][1][1]), dins[k][0]) for k in want]
                   + [jax.ShapeDtypeStruct(p.shape, F32) for p in params]),
        sem=("arbitrary",), name=name,
        args=(*[a for a, _, _ in ins], *params, *[a for a, _, _ in cts], *[a for a, _, _ in extras]), hosted=hosted)


def _mod_rows(i, nct, ada, lo):
    return jnp.where(i < nct, ada[1:2, lo:lo + D], ada[0:1, lo:lo + D])


def _f_norm1(nct):
    def f(i, xs, ps, diff):
        ctx, x = xs
        nw, ada = ps
        xt = jnp.where(i < nct, ctx, x)
        return (_rms(xt, nw) * (1.0 + _mod_rows(i, nct, ada, D)) + _mod_rows(i, nct, ada, 0),)
    return f


def _f_gates_dt(i, xs, ps, diff):
    small, xc = xs
    up_f, up_b, gb_f, gb_b, dtb, alog = ps
    dot = _bdot if diff else _dot_impl_nn
    lg_f = _logsig(dot(small, up_f) + gb_f) * (1.0 / GLA_TAU)
    lg_b = _logsig(dot(small, up_b) + gb_b) * (1.0 / GLA_TAU)
    dtp = _softplus(small + dtb)
    aa = -jnp.exp(alog) * dtp
    e_f, e_b = _head_expand(32), _head_expand(64)
    n = small.shape[0]
    t = lax.broadcasted_iota(jnp.int32, (n, n), 0)
    s = lax.broadcasted_iota(jnp.int32, (n, n), 1)
    same = lax.shift_right_logical(t, 6) == lax.shift_right_logical(s, 6)
    pre_g, suf_g = ((t >= s) & same).astype(F32), ((s >= t) & same).astype(F32)
    pre_s, suf_s = (t >= s).astype(F32), (s >= t).astype(F32)
    cum_f, cum_b = _edot(aa, pre_s, "ex"), _edot(aa, suf_s, "ex")
    h = lax.broadcasted_iota(jnp.int32, (SSM_G * SSM_HPG, 128), 0)
    l = lax.broadcasted_iota(jnp.int32, (SSM_G * SSM_HPG, 128), 1)
    cum_ft = _edot(cum_f, (l == h + 32).astype(F32), "ext")
    cum_bt = _edot(cum_b, (l == h + 64).astype(F32), "ext")
    return (_edot(lg_f, pre_g, "ex"), _edot(lg_b, suf_g, "ex"), xc * _edot(dtp, e_f, "xe"), xc * _edot(dtp, e_b, "xe"),
            _edot(cum_f, e_f, "xe"), _edot(cum_b, e_b, "xe"), cum_ft, cum_bt)


def _dot_impl_nn(a, b):
    return _dot_impl(a, b, False, False)


def _f_conv(nct, tc):
    def f(i, xs, ps, diff):
        (u,) = xs
        cw, cb = ps
        n = u.shape[0]
        t = lax.broadcasted_iota(jnp.int32, (n, 1), 0)
        per = jnp.where(i < nct, tc, GRID_W)
        pos = jnp.bitwise_and(t, per - 1)
        sh = _shift if diff else _shift_impl
        acc = cb + sh(u, pos, per, -2) * cw[0:1]
        for j in range(1, 4):
            acc = acc + sh(u, pos, per, j - 2) * cw[j:j + 1]
        return (_silu(acc),)
    return f


def _f_gla_post(i, xs, ps, diff):
    ogf, ogb, r = xs
    (gw,) = ps
    o = ogf + ogb
    parts = [_rms(o[:, h * GLA_DV:(h + 1) * GLA_DV], gw) for h in range(GLA_H)]
    return (jnp.concatenate(parts, axis=1) * _silu(r),)


def _f_ssd_post(i, xs, ps, diff):
    yf, yb, xc, z = xs
    dsk8, nw = ps
    dsk = _edot(dsk8, _head_expand(0), "xe")[0:1]
    y = (yf + yb + dsk * xc) * _silu(z)
    w = SSM_INNER // SSM_G
    parts = [_rms(y[:, g * w:(g + 1) * w], nw[:, g * w:(g + 1) * w]) for g in range(SSM_G)]
    return (jnp.concatenate(parts, axis=1),)


def _f_merge(i, xs, ps, diff):
    ga, gb, ya, yb = xs
    return (jax.nn.sigmoid(ga) * ya + jax.nn.sigmoid(gb) * yb,)


def _f_res1(nct):
    def f(i, xs, ps, diff):
        ctx, x, mix = xs
        ada, nw = ps
        h2 = jnp.where(i < nct, ctx, x) + _mod_rows(i, nct, ada, 2 * D) * mix
        return (h2, _rms(h2, nw) * (1.0 + _mod_rows(i, nct, ada, 4 * D)) + _mod_rows(i, nct, ada, 3 * D))
    return f


def _f_swiglu(i, xs, ps, diff):
    (gu,) = xs
    return (_silu(gu[:, :D_FF]) * gu[:, D_FF:],)


def _loss_head(h2, dn, tgt, ada, fw, nct):
    A = h2.shape[0]
    n_tiles = A // TM

    def tile_loss(i, h2t, dnt, tg, ada_, fw_):
        h3 = h2t + ada_[0:1, 5 * D:6 * D] * dnt
        err = _rms(h3, fw_) - tg
        row = 0.5 * jnp.mean(err * err, axis=-1, keepdims=True)
        return jnp.sum(row, axis=0, keepdims=True) * jnp.where(i < nct, 0.0, 1.0)

    def body(h2_ref, dn_ref, tg_ref, ada_ref, fw_ref, loss_ref, dh_ref, ddn_ref, dada_ref, dfw_ref):
        i = pl.program_id(0)
        val, vjp = jax.vjp(functools.partial(tile_loss, i), h2_ref[...], dn_ref[...], tg_ref[...], ada_ref[...], fw_ref[...])
        dh, ddn, _, dada, dfw = vjp(jnp.ones((1, 1), F32))
        dh_ref[...] = dh
        ddn_ref[...] = ddn.astype(BF16)
        lv = jnp.broadcast_to(val, loss_ref.shape)
        for r, v in ((loss_ref, lv), (dada_ref, dada), (dfw_ref, dfw)):
            @pl.when(i == 0)
            def _(r=r, v=v):
                r[...] = v

            @pl.when(i > 0)
            def _(r=r, v=v):
                r[...] += v

    row = lambda i: (i, 0)
    return pl.pallas_call(
        body, grid=(n_tiles,),
        in_specs=[pl.BlockSpec((TM, D), row), pl.BlockSpec((TM, D), row),
                  pl.BlockSpec((TM, D), lambda i: (jnp.maximum(i - nct, 0), 0)), _whole(ada), _whole(fw)],
        out_specs=[pl.BlockSpec((8, 128), lambda i: (0, 0)), pl.BlockSpec((TM, D), row), pl.BlockSpec((TM, D), row),
                   _whole(ada), _whole(fw)],
        out_shape=[jax.ShapeDtypeStruct((8, 128), F32), jax.ShapeDtypeStruct((A, D), F32),
                   jax.ShapeDtypeStruct((A, D), BF16), jax.ShapeDtypeStruct(ada.shape, F32),
                   jax.ShapeDtypeStruct(fw.shape, F32)],
        compiler_params=_cparams(("arbitrary",)), name="loss_head",
    )(h2, dn, tgt, ada, fw)


def _chunk_of(step, n_chunks, n_ctx_chunks, rev):
    if not rev:
        return step
    return jnp.where(step < n_ctx_chunks, n_ctx_chunks - 1 - step, n_chunks - 1 - (step - n_ctx_chunks))


def _gla_step(st, q, k, v, b, rev):
    tri = _tri(GLA_C, rev).astype(F32)
    last = 0 if rev else GLA_C - 1
    outs, sts = [], []
    for h in range(GLA_H):
        kk = slice(h * GLA_DK, (h + 1) * GLA_DK)
        vv = slice(h * GLA_DV, (h + 1) * GLA_DV)
        qh, kh, vh, bh, sh = q[:, kk] * (GLA_DK ** -0.5), k[:, kk], v[:, vv], b[:, kk], st[vv]
        tot = bh[last:last + 1]
        mid = bh[GLA_C // 2:GLA_C // 2 + 1]
        att = _bdot(qh * jnp.exp(bh - mid), kh * jnp.exp(mid - bh), False, True) * tri
        outs.append(_bdot(att, vh) + _bdot(qh * jnp.exp(bh), sh, False, True))
        sts.append(sh * jnp.exp(tot) + _bdot(vh, kh * jnp.exp(tot - bh), True, False))
    return jnp.concatenate(sts, axis=0), jnp.concatenate(outs, axis=1)


_GQK, _GV = GLA_H * GLA_DK, GLA_H * GLA_DV


def _gla_fwd(proj, bg, ncc, rev, name):
    A = proj.shape[0]
    nc = A // GLA_C
    ch = lambda s: _chunk_of(s, nc, ncc, rev)

    def body(q_ref, k_ref, v_ref, b_ref, o_ref, ss_ref, st_ref):
        @pl.when(pl.program_id(0) == 0)
        def _():
            st_ref[...] = jnp.zeros_like(st_ref)

        st = st_ref[...]
        ss_ref[0] = st
        st_new, o = _gla_step(st, q_ref[...], k_ref[...], v_ref[...], b_ref[...], rev)
        o_ref[...] = o
        st_ref[...] = st_new

    return pl.pallas_call(
        body, grid=(nc,),
        in_specs=[pl.BlockSpec((GLA_C, _GQK), lambda s: (ch(s), C_Q // _GQK)),
                  pl.BlockSpec((GLA_C, _GQK), lambda s: (ch(s), C_K // _GQK)),
                  pl.BlockSpec((GLA_C, _GV), lambda s: (ch(s), C_V // _GV)),
                  pl.BlockSpec((GLA_C, _GQK), lambda s: (ch(s), 0))],
        out_specs=[pl.BlockSpec((GLA_C, _GV), lambda s: (ch(s), 0)),
                   pl.BlockSpec((1, _GV, GLA_DK), lambda s: (s, 0, 0))],
        out_shape=[jax.ShapeDtypeStruct((A, _GV), F32), jax.ShapeDtypeStruct((nc, _GV, GLA_DK), F32)],
        scratch_shapes=[pltpu.VMEM((_GV, GLA_DK), F32)],
        compiler_params=_cparams(("arbitrary",)), name=name,
    )(proj, proj, proj, bg)


def _gla_bwd(proj, lg, saved, d_o, prev, ncc, rev, name):
    A = proj.shape[0]
    nc = A // GLA_C
    st_of = lambda r: nc - 1 - r
    ch = lambda r: _chunk_of(st_of(r), nc, ncc, rev)
    qs = pl.BlockSpec((GLA_C, _GQK), lambda r: (ch(r), 0))
    vs = pl.BlockSpec((GLA_C, _GV), lambda r: (ch(r), 0))
    n_prev = 0 if prev is None else 3

    def body(*refs):
        q_ref, k_ref, v_ref, b_ref, ss_ref, do_ref = refs[:6]
        p_refs = refs[6:6 + n_prev]
        dq_ref, dk_ref, dv_ref, db_ref, ds_ref = refs[6 + n_prev:]

        @pl.when(pl.program_id(0) == 0)
        def _():
            ds_ref[...] = jnp.zeros_like(ds_ref)

        _, vjp = jax.vjp(functools.partial(_gla_step, rev=rev), ss_ref[0], q_ref[...], k_ref[...], v_ref[...], b_ref[...])
        ds, dq, dk, dv, db = vjp((ds_ref[...], do_ref[...]))
        if n_prev:
            dq, dk, dv = dq + p_refs[0][...], dk + p_refs[1][...], dv + p_refs[2][...]
        dq_ref[...], dk_ref[...], dv_ref[...], db_ref[...] = dq, dk, dv, db
        ds_ref[...] = ds

    in_specs = [
        pl.BlockSpec((GLA_C, _GQK), lambda r: (ch(r), C_Q // _GQK)),
        pl.BlockSpec((GLA_C, _GQK), lambda r: (ch(r), C_K // _GQK)),
        pl.BlockSpec((GLA_C, _GV), lambda r: (ch(r), C_V // _GV)),
        qs,
        pl.BlockSpec((1, _GV, GLA_DK), lambda r: (st_of(r), 0, 0)),
        vs,
    ] + ([qs, qs, vs] if n_prev else [])
    return pl.pallas_call(
        body, grid=(nc,), in_specs=in_specs, out_specs=[qs, qs, vs, qs],
        out_shape=[jax.ShapeDtypeStruct((A, _GQK), F32), jax.ShapeDtypeStruct((A, _GQK), F32),
                   jax.ShapeDtypeStruct((A, _GV), F32), jax.ShapeDtypeStruct((A, _GQK), F32)],
        scratch_shapes=[pltpu.VMEM((_GV, GLA_DK), F32)],
        compiler_params=_cparams(("arbitrary",)), name=name,
    )(proj, proj, proj, lg, saved, d_o, *(prev or ()))


def _ssd_step(st, x, bm, cm, cum, cum_t, rev):
    mask = _tri(SSM_C, rev)
    last = 0 if rev else SSM_C - 1
    tot = cum[last:last + 1]
    cb = _bdot(cm, bm, False, True)
    ys = []
    for e in range(SSM_HPG):
        lm = jnp.exp(jnp.where(mask, cum[:, e * SSM_P:e * SSM_P + 1] - cum_t[e:e + 1], -jnp.inf))
        ys.append(_bdot(cb * lm, x[:, e * SSM_P:(e + 1) * SSM_P]))
    y = jnp.concatenate(ys, axis=1) + _bdot(cm, st) * jnp.exp(cum)
    st_new = st * jnp.exp(tot) + _bdot(bm, x * jnp.exp(tot - cum), True, False)
    return st_new, y


_XBC_B = SSM_INNER // SSM_N
_XBC_C = _XBC_B + SSM_G


def _ssd_fwd(xd, xbc, cum, cum_t, ncc, rev, name, hosted=()):
    A = xd.shape[0]
    nc = A // SSM_C
    W = SSM_HPG * SSM_P
    ch = lambda s: _chunk_of(s, nc, ncc, rev)

    def body(x_ref, b_ref, c_ref, a_ref, at_ref, y_ref, ss_ref, st_ref):
        @pl.when(pl.program_id(1) == 0)
        def _():
            st_ref[...] = jnp.zeros_like(st_ref)

        st = st_ref[...]
        ss_ref[0, 0] = st
        st_new, y = _ssd_step(st, x_ref[...], b_ref[...], c_ref[...], a_ref[...], at_ref[...], rev)
        y_ref[...] = y
        st_ref[...] = st_new

    gs = pl.BlockSpec((SSM_C, W), lambda g, s: (ch(s), g))
    return _call(
        body, grid=(SSM_G, nc),
        in_specs=[gs, pl.BlockSpec((SSM_C, SSM_N), lambda g, s: (ch(s), _XBC_B + g)),
                  pl.BlockSpec((SSM_C, SSM_N), lambda g, s: (ch(s), _XBC_C + g)), gs,
                  pl.BlockSpec((SSM_HPG, SSM_C), lambda g, s: (g, ch(s)))],
        out_specs=[gs, pl.BlockSpec((1, 1, SSM_N, W), lambda g, s: (g, s, 0, 0))],
        out_shape=[jax.ShapeDtypeStruct((A, SSM_INNER), F32), jax.ShapeDtypeStruct((SSM_G, nc, SSM_N, W), F32)],
        scratch=[pltpu.VMEM((SSM_N, W), F32)],
        sem=("parallel", "arbitrary"), name=name, args=(xd, xbc, xbc, cum, cum_t), hosted=hosted)


def _ssd_bwd(xd, xbc, cum, cum_t, saved, d_y, prev, ncc, rev, name):
    A = xd.shape[0]
    nc = A // SSM_C
    W = SSM_HPG * SSM_P
    st_of = lambda r: nc - 1 - r
    ch = lambda r: _chunk_of(st_of(r), nc, ncc, rev)
    gs = pl.BlockSpec((SSM_C, W), lambda g, r: (ch(r), g))
    ns = pl.BlockSpec((SSM_C, SSM_N), lambda g, r: (ch(r), g))
    ts = pl.BlockSpec((SSM_HPG, SSM_C), lambda g, r: (g, ch(r)))
    n_prev = 0 if prev is None else 2

    def body(*refs):
        x_ref, b_ref, c_ref, a_ref, at_ref, ss_ref, dy_ref = refs[:7]
        p_refs = refs[7:7 + n_prev]
        dx_ref, db_ref, dc_ref, da_ref, dat_ref, ds_ref = refs[7 + n_prev:]

        @pl.when(pl.program_id(1) == 0)
        def _():
            ds_ref[...] = jnp.zeros_like(ds_ref)

        _, vjp = jax.vjp(functools.partial(_ssd_step, rev=rev), ss_ref[0, 0], x_ref[...], b_ref[...], c_ref[...],
                         a_ref[...], at_ref[...])
        ds, dx, db, dc, da, dat = vjp((ds_ref[...], dy_ref[...]))
        if n_prev:
            db, dc = db + p_refs[0][...], dc + p_refs[1][...]
        dx_ref[...], db_ref[...], dc_ref[...], da_ref[...], dat_ref[...] = dx, db, dc, da, dat
        ds_ref[...] = ds

    in_specs = [gs, pl.BlockSpec((SSM_C, SSM_N), lambda g, r: (ch(r), _XBC_B + g)),
                pl.BlockSpec((SSM_C, SSM_N), lambda g, r: (ch(r), _XBC_C + g)), gs, ts,
                pl.BlockSpec((1, 1, SSM_N, W), lambda g, r: (g, st_of(r), 0, 0)), gs] + ([ns, ns] if n_prev else [])
    return pl.pallas_call(
        body, grid=(SSM_G, nc), in_specs=in_specs, out_specs=[gs, ns, ns, gs, ts],
        out_shape=[jax.ShapeDtypeStruct((A, SSM_INNER), F32), jax.ShapeDtypeStruct((A, SSM_G * SSM_N), F32),
                   jax.ShapeDtypeStruct((A, SSM_G * SSM_N), F32), jax.ShapeDtypeStruct((A, SSM_INNER), F32),
                   jax.ShapeDtypeStruct((SSM_G * SSM_HPG, A), F32)],
        scratch_shapes=[pltpu.VMEM((SSM_N, W), F32)],
        compiler_params=_cparams(("parallel", "arbitrary")), name=name,
    )(xd, xbc, xbc, cum, cum_t, saved, d_y, *(prev or ()))


def _ada_fwd(cc, w_ada, b_ada):
    n = w_ada.shape[1]
    tn = _pick(n, 1024, 128)

    def body(cc_ref, w_ref, b_ref, o_ref):
        o_ref[...] = _dot_impl(_silu(cc_ref[...]), w_ref[...], False, False) + b_ref[...]

    return pl.pallas_call(
        body, grid=(n // tn,),
        in_specs=[_whole(cc), pl.BlockSpec((D, tn), lambda j: (0, j)), pl.BlockSpec((1, tn), lambda j: (0, j))],
        out_specs=pl.BlockSpec((8, tn), lambda j: (0, j)),
        out_shape=jax.ShapeDtypeStruct((8, n), F32),
        compiler_params=_cparams(("parallel",)), name="ada_fwd",
    )(cc, w_ada, b_ada)


def _ada_bwd(cc, w_ada, d1, d2, d3):
    n = w_ada.shape[1]
    tn = _pick(n, 1024, 128)
    nt = n // tn

    def body(cc_ref, w_ref, d1_ref, d2_ref, d3_ref, dw_ref, db_ref, dcc_ref, ds_ref):
        j = pl.program_id(0)
        d = d1_ref[...] + d2_ref[...] + d3_ref[...]
        s = _silu(cc_ref[...])
        dw_ref[...] = _dot_impl(s, d, True, False)
        db_ref[...] = jnp.sum(d, axis=0, keepdims=True)
        p = _dot_impl(d, w_ref[...], False, True)

        @pl.when(j == 0)
        def _():
            ds_ref[...] = p

        @pl.when(j > 0)
        def _():
            ds_ref[...] += p

        @pl.when(j == nt - 1)
        def _():
            _, vjp = jax.vjp(_silu, cc_ref[...])
            dcc_ref[...] = vjp(ds_ref[...])[0]

    col = lambda j: (0, j)
    return pl.pallas_call(
        body, grid=(nt,),
        in_specs=[_whole(cc), pl.BlockSpec((D, tn), col), pl.BlockSpec((8, tn), col), pl.BlockSpec((8, tn), col),
                  pl.BlockSpec((8, tn), col)],
        out_specs=[pl.BlockSpec((D, tn), col), pl.BlockSpec((1, tn), col), _whole(cc)],
        out_shape=[jax.ShapeDtypeStruct((D, n), F32), jax.ShapeDtypeStruct((1, n), F32), jax.ShapeDtypeStruct((8, D), F32)],
        scratch_shapes=[pltpu.VMEM((8, D), F32)],
        compiler_params=_cparams(("arbitrary",)), name="ada_bwd",
    )(cc, w_ada, d1, d2, d3)


def _local_step(x, c, ctx, c_ctx, target, w, shards):
    T, Tc = x.shape[0], ctx.shape[0]
    assert Tc == TM and T % TM == 0 and GRID_W == GLA_C and TS == SSM_C == 2 * GLA_C
    A = T + Tc
    nct = Tc // TM
    n_tm, n_ts = A // TM, A // TS
    g = {}

    cc = jnp.zeros((8, D), F32).at[0].set(c[0]).at[1].set(c_ctx)
    ada = _ada_fwd(cc, w["w_ada"], w["b_ada"])

    x_in = [_ctx(ctx, TM, nct), _lat(x, TM, nct)]
    f_norm1 = _f_norm1(nct)
    p_norm1 = [w["norm1_w"], ada]
    (h1,) = _stage_fwd(f_norm1, n_tm, x_in, p_norm1, [(A, D, BF16, TM)], "norm1")
    proj = _mm(h1, w["w_in"], trans_b=False, out_dtype=F32, name="mm_in", tn_t=1152)

    p_gd = [w["up_f"], w["up_b"], w["gla_bias_f"], w["gla_bias_b"], w["dtb"], w["alog"]]
    f_conv = _f_conv(nct, Tc)
    p_conv = [w["conv_w"], w["conv_b"]]
    in_conv = [_col(proj, TM, 3072, C_U // 3072)]
    (xbc,) = _stage_fwd(f_conv, n_tm, in_conv, p_conv, [(A, 3072, F32, TM)], "conv")
    in_gd = [_col(proj, TS, 128, C_S // 128), _col(xbc, TS, SSM_INNER, 0)]
    n_heads = SSM_G * SSM_HPG
    lg_f, lg_b, xf, xb, axf, axb, atf, atb = _stage_fwd(
        _f_gates_dt, n_ts, in_gd, p_gd,
        [(A, 512, F32, TS), (A, 512, F32, TS)] + [(A, SSM_INNER, F32, TS)] * 4 + [(A, n_heads, F32, TS, True)] * 2,
        "gates_dt")

    ncc_g, ncc_s = Tc // GLA_C, Tc // SSM_C
    ogf, sv_gf = _gla_fwd(proj, lg_f, ncc_g, False, "gla_f")
    ogb, sv_gb = _gla_fwd(proj, lg_b, ncc_g, True, "gla_b")
    ysf, sv_sf, ag_pa, ag_pb, ag_out = _ssd_fwd(
        xf, xbc, axf, atf, ncc_s, False, "ssd_f", hosted=[(shards[n], False) for n in ("w_pa", "w_pb", "w_out")])
    ysb, sv_sb, ag_gate, ag_up, ag_down = _ssd_fwd(
        xb, xbc, axb, atb, ncc_s, True, "ssd_b", hosted=[(shards[n], False) for n in ("w_gate", "w_up", "w_down")])
    w = dict(w, w_pa=ag_pa.reshape(D, D), w_pb=ag_pb.reshape(SSM_INNER, D), w_out=ag_out.reshape(D, D),
             w_gu=jnp.concatenate([_unshard_cols(ag_gate), _unshard_cols(ag_up)], axis=1), w_down=ag_down.reshape(D_FF, D))
    got = {}

    in_gp = [_col(ogf, TS), _col(ogb, TS), _col(proj, TS, 1024, C_R // 1024)]
    p_gp = [w["gla_norm_w"]]
    (oa,) = _stage_fwd(_f_gla_post, n_ts, in_gp, p_gp, [(A, D, BF16, TS)], "gla_post")
    in_sp = [_col(ysf, TS), _col(ysb, TS), _col(xbc, TS, SSM_INNER, 0), _col(proj, TS, SSM_INNER, C_Z // SSM_INNER)]
    p_sp = [w["dsk8"], w["ssm_norm_w"]]
    (ob,) = _stage_fwd(_f_ssd_post, n_ts, in_sp, p_sp, [(A, SSM_INNER, BF16, TS)], "ssd_post")
    ya = _mm(oa, w["w_pa"], trans_b=False, out_dtype=F32, name="mm_pa")
    yb = _mm(ob, w["w_pb"], trans_b=False, out_dtype=F32, name="mm_pb")
    in_mg = [_col(proj, TS, 1024, C_GA // 1024), _col(proj, TS, 1024, C_GB // 1024), _col(ya, TS), _col(yb, TS)]
    (merged,) = _stage_fwd(_f_merge, n_ts, in_mg, [], [(A, D, BF16, TS)], "merge")
    mix = _mm(merged, w["w_out"], trans_b=False, out_dtype=F32, name="mm_out")

    f_res1 = _f_res1(nct)
    in_r1 = x_in + [_col(mix, TM)]
    p_r1 = [ada, w["norm2_w"]]
    h2, hm2 = _stage_fwd(f_res1, n_tm, in_r1, p_r1, [(A, D, F32, TM), (A, D, BF16, TM)], "res1")
    gu = _mm(hm2, w["w_gu"], trans_b=False, out_dtype=F32, name="mm_gu", tn_t=1408)
    in_sw = [_col(gu, TS)]
    (act,) = _stage_fwd(_f_swiglu, n_ts, in_sw, [], [(A, D_FF, BF16, TS)], "swiglu")
    dn = _mm(act, w["w_down"], trans_b=False, out_dtype=F32, name="mm_down")

    loss_blk, d_h2a, d_dn, d_ada3, g["final_norm_w"] = _loss_head(h2, dn, target, ada, w["final_norm_w"], nct)
    rows16 = lambda gw: (gw.reshape(N_DEV, gw.shape[0] // N_DEV, gw.shape[1]).astype(BF16), True)
    cols16 = lambda gw: (_shard_cols(gw).astype(BF16), True)
    gw_down = _mm_tn(act, d_dn, name="dw_down", tr_t=1408)
    d_act = _mm(d_dn, w["w_down"], trans_b=True, out_dtype=F32, name="dx_down", tn_t=1408)
    d_gu, got["w_down"] = _stage_bwd(_f_swiglu, n_ts, in_sw, [], [_col(d_act, TS)], lambda i, t: t, [(BF16, None)],
                                     "swiglu_b", hosted=[rows16(gw_down)])
    gw_gu = _mm_tn(hm2, d_gu, name="dw_gu")
    d_hm2 = _mm(d_gu, w["w_gu"], trans_b=True, out_dtype=F32, name="dx_gu")
    d_x1, d_mix, d_ada2, g["norm2_w"], got["w_gate"] = _stage_bwd(
        f_res1, n_tm, in_r1, p_r1, [_col(d_h2a, TM), _col(d_hm2, TM)], lambda i, t: t,
        [None, (F32, None), (BF16, None)], "res1_b", hosted=[cols16(gw_gu[:, :D_FF])])
    gw_out = _mm_tn(merged, d_mix, name="dw_out")
    d_merged = _mm(d_mix, w["w_out"], trans_b=True, out_dtype=F32, name="dx_out")
    d_ga, d_gb, d_ya, d_yb, got["w_up"] = _stage_bwd(
        _f_merge, n_ts, in_mg, [], [_col(d_merged, TS)], lambda i, t: t, [(BF16, None)] * 4, "merge_b",
        hosted=[cols16(gw_gu[:, D_FF:])])
    gw_pa = _mm_tn(oa, d_ya, name="dw_pa")
    gw_pb = _mm_tn(ob, d_yb, name="dw_pb")
    d_oa = _mm(d_ya, w["w_pa"], trans_b=True, out_dtype=F32, name="dx_pa")
    d_ob = _mm(d_yb, w["w_pb"], trans_b=True, out_dtype=F32, name="dx_pb")
    d_og, d_r, g["gla_norm_w"], got["w_out"] = _stage_bwd(
        _f_gla_post, n_ts, in_gp, p_gp, [_col(d_oa, TS)], lambda i, t: t, [(F32, None), None, (BF16, None)], "gla_post_b",
        hosted=[rows16(gw_out)])
    d_ys, d_xs_skip, d_z, g["dsk8"], g["ssm_norm_w"], got["w_pa"], got["w_pb"] = _stage_bwd(
        _f_ssd_post, n_ts, in_sp, p_sp, [_col(d_ob, TS)], lambda i, t: t,
        [(F32, None), None, (F32, None), (BF16, None)], "ssd_post_b", hosted=[rows16(gw_pa), rows16(gw_pb)])

    dq, dk, dv, d_lgf = _gla_bwd(proj, lg_f, sv_gf, d_og, None, ncc_g, False, "gla_f_b")
    dq, dk, dv, d_lgb = _gla_bwd(proj, lg_b, sv_gb, d_og, (dq, dk, dv), ncc_g, True, "gla_b_b")
    d_xf, d_bm, d_cm, d_axf, d_atf = _ssd_bwd(xf, xbc, axf, atf, sv_sf, d_ys, None, ncc_s, False, "ssd_f_b")
    d_xb, d_bm, d_cm, d_axb, d_atb = _ssd_bwd(xb, xbc, axb, atb, sv_sb, d_ys, (d_bm, d_cm), ncc_s, True, "ssd_b_b")

    cts_gd = [_col(a, TS) for a in (d_lgf, d_lgb, d_xf, d_xb, d_axf, d_axb)]
    cts_gd += [(a, (n_heads, TS), lambda i: (0, i)) for a in (d_atf, d_atb)]
    d_small, d_xs_dt, g["up_f"], g["up_b"], g["gla_bias_f"], g["gla_bias_b"], g["dtb"], g["alog"] = _stage_bwd(
        _f_gates_dt, n_ts, in_gd, p_gd, cts_gd, lambda i, t: t, [(BF16, None), (F32, None)], "gates_dt_b")
    cts_conv = [_col(d_xs_skip, TM), _col(d_xs_dt, TM), _col(d_bm, TM), _col(d_cm, TM)]
    d_u, g["conv_w"], g["conv_b"] = _stage_bwd(
        f_conv, n_tm, in_conv, p_conv, cts_conv,
        lambda i, t: [jnp.concatenate([t[0] + t[1], t[2], t[3]], axis=1)], [(BF16, None)], "conv_b")

    d_proj = jnp.concatenate([dq.astype(BF16), dk.astype(BF16), dv.astype(BF16), d_r, d_u, d_z, d_ga, d_gb, d_small], axis=1)
    gw_in = _mm_tn(h1, d_proj, name="dw_in", tn_t=1152)
    d_h1, got["w_in"] = _mm(d_proj, w["w_in"], trans_b=True, out_dtype=F32, name="dx_in", tk_t=1152,
                            hosted=[cols16(_from_padded(gw_in))])
    grad_x, g["norm1_w"], d_ada1 = _stage_bwd(
        f_norm1, n_tm, x_in, p_norm1, [_col(d_h1, TM)], lambda i, t: t,
        [None, (F32, (d_x1, (TM, D), x_in[1][2]))], "norm1_b")
    g["w_ada"], g["b_ada"], d_cc = _ada_bwd(cc, w["w_ada"], d_ada1, d_ada2, d_ada3)
    g["c_ctx"] = d_cc[1]
    return loss_blk, grad_x, g, got


def _exchange(x, a2a, name):
    shp = x.shape[1:] if a2a else x.shape

    def body(x_ref, o_ref, send_sems, recv_sems, local_sem):
        ops = _exchange_ops(x_ref, o_ref, send_sems, recv_sems, local_sem, a2a)
        for op in ops:
            op.start()
        for op in ops:
            op.wait()

    return pl.pallas_call(
        body, out_shape=jax.ShapeDtypeStruct((N_DEV,) + tuple(shp), x.dtype),
        in_specs=[pl.BlockSpec(memory_space=pl.ANY)], out_specs=pl.BlockSpec(memory_space=pl.ANY),
        scratch_shapes=[pltpu.SemaphoreType.DMA((N_DEV - 1,)), pltpu.SemaphoreType.DMA((N_DEV - 1,)),
                        pltpu.SemaphoreType.DMA(())],
        name=name,
    )(x)


def _all_gather(x, name):
    def body(x_ref, o_ref, send_sems, recv_sems, local_sem):
        mx, my, mc = lax.axis_index("x"), lax.axis_index("y"), lax.axis_index("c")
        me, sibling = (mx, my, mc), (mx, my, 1 - mc)
        chips = [(1 - mx, my), (mx, 1 - my), (1 - mx, 1 - my)]

        def slot(px, py, pc):
            return o_ref.at[4 * px + 2 * py + pc]

        def copy(k, block, to, src=None):
            return pltpu.make_async_remote_copy(
                src_ref=slot(*block) if src is None else src, dst_ref=slot(*block),
                send_sem=send_sems.at[k], recv_sem=recv_sems.at[k], device_id=to, device_id_type=pl.DeviceIdType.MESH)

        own = pltpu.make_async_copy(x_ref, slot(*me), local_sem)
        own.start()
        first = [copy(0, me, sibling, src=x_ref)] + [copy(1 + j, me, (*chip, mc), src=x_ref) for j, chip in enumerate(chips)]
        for cp in first:
            cp.start()
        passed = [copy(4 + j, (*chip, mc), sibling) for j, chip in enumerate(chips)]
        for j, chip in enumerate(chips):
            copy(1 + j, (*chip, mc), me).wait_recv()
            passed[j].start()
        copy(0, sibling, me).wait_recv()
        for j, chip in enumerate(chips):
            copy(4 + j, (*chip, 1 - mc), me).wait_recv()
        for cp in first + passed:
            cp.wait_send()
        own.wait()

    return pl.pallas_call(
        body, out_shape=jax.ShapeDtypeStruct((N_DEV,) + tuple(x.shape), x.dtype),
        in_specs=[pl.BlockSpec(memory_space=pl.ANY)], out_specs=pl.BlockSpec(memory_space=pl.ANY),
        scratch_shapes=[pltpu.SemaphoreType.DMA((N_DEV - 1,)), pltpu.SemaphoreType.DMA((N_DEV - 1,)),
                        pltpu.SemaphoreType.DMA(())],
        name=name,
    )(x)


def _adamw_math(w, gr, m, v):
    m = ADAM_B1 * m + (1.0 - ADAM_B1) * gr
    v = ADAM_B2 * v + (1.0 - ADAM_B2) * (gr * gr)
    m_hat = m / np.float32(1.0 - ADAM_B1 ** ADAM_STEP)
    v_hat = v / np.float32(1.0 - ADAM_B2 ** ADAM_STEP)
    delta = -ADAM_LR * (m_hat / (jnp.sqrt(v_hat) + ADAM_EPS) + ADAM_WD * w)
    return delta, m, v


def _sum_adamw(parts, w, m, v, name):
    R, C = w.shape
    tr = _pick(R, max(16, (2 << 20) // (4 * C) // 16 * 16), 16)

    def body(p_ref, w_ref, m_ref, v_ref, g_ref, d_ref, mo_ref, vo_ref):
        gr = p_ref[0].astype(F32)
        for k in range(1, N_DEV):
            gr = gr + p_ref[k].astype(F32)
        g_ref[...] = gr
        d_ref[...], mo_ref[...], vo_ref[...] = _adamw_math(w_ref[...], gr, m_ref[...], v_ref[...])

    row = pl.BlockSpec((tr, C), lambda i: (i, 0))
    return pl.pallas_call(
        body, grid=(R // tr,),
        in_specs=[pl.BlockSpec((N_DEV, tr, C), lambda i: (0, i, 0)), row, row, row],
        out_specs=[row] * 4, out_shape=[jax.ShapeDtypeStruct((R, C), F32)] * 4,
        compiler_params=_cparams(("parallel",)), name=name,
    )(parts, w, m, v)


def _sum8(parts, name):
    _, R, C = parts.shape

    def body(p_ref, g_ref):
        gr = p_ref[0]
        for k in range(1, N_DEV):
            gr = gr + p_ref[k]
        g_ref[...] = gr

    return pl.pallas_call(body, out_shape=jax.ShapeDtypeStruct((R, C), F32), name=name)(parts)


def _adamw(w, gr, m, v, name):
    def body(w_ref, g_ref, m_ref, v_ref, d_ref, mo_ref, vo_ref):
        d_ref[...], mo_ref[...], vo_ref[...] = _adamw_math(w_ref[...], g_ref[...], m_ref[...], v_ref[...])

    return pl.pallas_call(body, out_shape=[jax.ShapeDtypeStruct(w.shape, F32)] * 3, name=name)(w, gr, m, v)


def _to_padded(w_in):
    z = jnp.zeros(w_in.shape[:-1] + (32,), w_in.dtype)
    return jnp.concatenate([w_in[..., 0:3072], w_in[..., 5152:8224], w_in[..., 3104:5152], w_in[..., 8288:10336],
                            w_in[..., 3072:3104], w_in[..., 8224:8288], z], axis=-1)


def _from_padded(p):
    return jnp.concatenate([p[..., 0:3072], p[..., 10240:10272], p[..., 6144:8192], p[..., 3072:6144],
                            p[..., 10272:10336], p[..., 8192:10240]], axis=-1)


def _unshard_cols(gathered):
    n, r, c = gathered.shape
    return jnp.transpose(gathered, (1, 0, 2)).reshape(r, n * c)


def _shard_cols(full):
    r, nc = full.shape
    return jnp.transpose(full.reshape(r, N_DEV, nc // N_DEV), (1, 0, 2))


def _lanes(vec, lo):
    return jnp.zeros((1, 128), F32).at[:, lo:lo + vec.shape[1]].set(vec)


_SMALL = (("b_ada", 6 * D), ("c_ctx", D), ("norm1_w", D), ("gla_bias_f", 512), ("gla_bias_b", 512), ("gla_norm_w", 256),
          ("conv_b", 3072), ("dt_bias_f", 32), ("dt_bias_b", 32), ("a_log_f", 32), ("a_log_b", 32), ("d_skip", 32),
          ("ssm_norm_w", 2048), ("norm2_w", D), ("final_norm_w", D))
_SHARDED_SMALL = (("gla_up_f", 16 * 512), ("gla_up_b", 16 * 512), ("conv_w", 4 * 3072))


def _pack(vals, names):
    flat = jnp.concatenate([vals[n].reshape(-1).astype(F32) for n, _ in names])
    pad = (-flat.shape[0]) % 1024
    return jnp.concatenate([flat, jnp.zeros((pad,), F32)]).reshape(-1, 128)


def _unpack(packed, names, shapes):
    flat, out, off = packed.reshape(-1), {}, 0
    for n, size in names:
        out[n] = flat[off:off + size].reshape(shapes[n])
        off += size
    return out


def kernel(x, c, ctx, c_ctx, w_ada, b_ada, norm1_w, w_in, gla_up_f, gla_bias_f, gla_up_b, gla_bias_b, gla_norm_w, conv_w, conv_b, dt_bias_f, dt_bias_b, a_log_f, a_log_b, d_skip, ssm_norm_w, w_pa, w_pb, w_out, norm2_w, w_gate, w_up, w_down, final_norm_w, loss_target, m_c_ctx, m_w_ada, m_b_ada, m_norm1_w, m_w_in, m_gla_up_f, m_gla_bias_f, m_gla_up_b, m_gla_bias_b, m_gla_norm_w, m_conv_w, m_conv_b, m_dt_bias_f, m_dt_bias_b, m_a_log_f, m_a_log_b, m_d_skip, m_ssm_norm_w, m_w_pa, m_w_pb, m_w_out, m_norm2_w, m_w_gate, m_w_up, m_w_down, m_final_norm_w, v_c_ctx, v_w_ada, v_b_ada, v_norm1_w, v_w_in, v_gla_up_f, v_gla_bias_f, v_gla_up_b, v_gla_bias_b, v_gla_norm_w, v_conv_w, v_conv_b, v_dt_bias_f, v_dt_bias_b, v_a_log_f, v_a_log_b, v_d_skip, v_ssm_norm_w, v_w_pa, v_w_pb, v_w_out, v_norm2_w, v_w_gate, v_w_up, v_w_down, v_final_norm_w):
    args = dict(locals())
    me = 4 * lax.axis_index("x") + 2 * lax.axis_index("y") + lax.axis_index("c")

    def gather16(a, name):
        return _all_gather(a[0].astype(BF16), name)

    full = {
        "w_ada": _unshard_cols(gather16(w_ada, "ag_w_ada")),
        "w_in": _to_padded(_unshard_cols(gather16(w_in, "ag_w_in"))),
    }
    shards = {n: args[n][0].astype(BF16) for n in ("w_pa", "w_pb", "w_out", "w_gate", "w_up", "w_down")}
    sm = _exchange(jnp.concatenate([gla_up_f.reshape(-1), gla_up_b.reshape(-1), conv_w.reshape(-1)]).reshape(-1, 128), False, "ag_small")
    sm = sm.reshape(N_DEV, -1)
    up_f = _unshard_cols(sm[:, 0:1024].reshape(N_DEV, 16, 64))
    up_b = _unshard_cols(sm[:, 1024:2048].reshape(N_DEV, 16, 64))
    full["conv_w"] = _unshard_cols(sm[:, 2048:3584].reshape(N_DEV, 4, 384))
    full["up_f"] = jnp.zeros((128, 512), F32).at[0:16].set(up_f)
    full["up_b"] = jnp.zeros((128, 512), F32).at[16:32].set(up_b)
    full["dtb"] = _lanes(dt_bias_f, 32) + _lanes(dt_bias_b, 64)
    full["alog"] = _lanes(a_log_f, 32) + _lanes(a_log_b, 64)
    full["dsk8"] = jnp.zeros((8, 128), F32).at[0:1, 0:32].set(d_skip)
    for n in ("b_ada", "norm1_w", "gla_bias_f", "gla_bias_b", "gla_norm_w", "conv_b", "ssm_norm_w", "norm2_w"):
        full[n] = args[n]
    full["final_norm_w"] = final_norm_w.reshape(1, D)

    loss_blk, grad_x, g, got = _local_step(x[0], c, ctx[0], c_ctx, loss_target[0], full, shards)

    gs = dict(g)
    gs["dt_bias_f"], gs["dt_bias_b"] = g["dtb"][:, 32:64], g["dtb"][:, 64:96]
    gs["a_log_f"], gs["a_log_b"] = g["alog"][:, 32:64], g["alog"][:, 64:96]
    gs["d_skip"] = g["dsk8"][0:1, 0:32]
    gs["gla_up_f"], gs["gla_up_b"] = g["up_f"][0:16], g["up_b"][16:32]
    gs["loss"] = loss_blk[0:1, 0:1]
    names = _SMALL + _SHARDED_SMALL + (("loss", 1),)
    shapes = {n: (args[n].shape if n in args else (1, 1)) for n, _ in names}
    shapes.update({"gla_up_f": (16, 512), "gla_up_b": (16, 512), "conv_w": (4, 3072)})
    red = _unpack(_sum8(_exchange(_pack(gs, names), False, "ag_small_grads"), "sum_small_grads"), names, shapes)
    loss = red["loss"].reshape(())
    grads = {n: red[n] for n, _ in _SMALL}
    grads["gla_up_f"] = lax.dynamic_slice(red["gla_up_f"], (0, me * 64), (16, 64))[None]
    grads["gla_up_b"] = lax.dynamic_slice(red["gla_up_b"], (0, me * 64), (16, 64))[None]
    grads["conv_w"] = lax.dynamic_slice(red["conv_w"], (0, me * 384), (4, 384))[None]
    upd_names = tuple((n, s) for n, s in _SMALL) + (("gla_up_f", 1024), ("gla_up_b", 1024), ("conv_w", 1536))
    pk = lambda prefix, src: _pack({n: src[prefix + n] for n, _ in upd_names}, upd_names)
    d_s, m_s, v_s = _adamw(pk("", args), _pack(grads, upd_names), pk("m_", args), pk("v_", args), "adamw_small")
    upd_shapes = {n: args[n].shape for n, _ in upd_names}
    delta = _unpack(d_s, upd_names, upd_shapes)
    new_m = _unpack(m_s, upd_names, upd_shapes)
    new_v = _unpack(v_s, upd_names, upd_shapes)

    got["w_ada"] = _exchange(_shard_cols(g["w_ada"]).astype(BF16), True, "a2a_w_ada")
    for n, parts in got.items():
        gr, dl, mo, vo = _sum_adamw(parts, args[n][0], args["m_" + n][0], args["v_" + n][0], "adamw_" + n)
        grads[n], delta[n], new_m[n], new_v[n] = gr[None], dl[None], mo[None], vo[None]

    order = ["c_ctx", "w_ada", "b_ada", "norm1_w", "w_in", "gla_up_f", "gla_bias_f", "gla_up_b", "gla_bias_b", "gla_norm_w",
             "conv_w", "conv_b", "dt_bias_f", "dt_bias_b", "a_log_f", "a_log_b", "d_skip", "ssm_norm_w", "w_pa", "w_pb",
             "w_out", "norm2_w", "w_gate", "w_up", "w_down", "final_norm_w"]
    fix = lambda d: [d[n].reshape(args[n].shape) for n in order]
    return (loss, grad_x[None], *fix(grads), *fix(delta), *fix(new_m), *fix(new_v))
```

```python
import functools

import jax
import jax.numpy as jnp
import numpy as np
from jax import lax
from jax.experimental import pallas as pl
from jax.experimental.pallas import tpu as pltpu

F32 = jnp.float32
BF16 = jnp.bfloat16
N_DEV = 8
D = 1024
EPS = 1e-6
GRID_W = 64
GLA_H, GLA_DK, GLA_DV = 4, 128, 256
GLA_C = 64
GLA_TAU = 16.0
SSM_G, SSM_HPG, SSM_P, SSM_N = 4, 8, 64, 128
SSM_C = 128
SSM_INNER = 2048
D_FF = 2816
D_IN = 10336
D_INP = 10368
C_Q, C_K, C_V, C_R, C_U, C_Z, C_GA, C_GB, C_S = 0, 512, 1024, 2048, 3072, 6144, 8192, 9216, 10240
TM = 256
TS = 128

ADAM_LR, ADAM_B1, ADAM_B2, ADAM_EPS, ADAM_WD, ADAM_STEP = 0.001, 0.9, 0.999, 1e-08, 0.01, 10
VMEM_LIMIT = 56 << 20


def _cparams(sem):
    return pltpu.CompilerParams(dimension_semantics=sem or None, vmem_limit_bytes=VMEM_LIMIT)


def _pick(n, target, mult):
    best = None
    for t in range(mult, min(n, target) + 1, mult):
        if n % t == 0:
            best = t
    return best if best is not None else n


def _dot_impl(a, b, ta, tb):
    dims = (((0 if ta else 1,), (1 if tb else 0,)), ((), ()))
    return lax.dot_general(a.astype(BF16), b.astype(BF16), dims, preferred_element_type=F32)


@functools.partial(jax.custom_vjp, nondiff_argnums=(2, 3))
def _bdot(a, b, ta=False, tb=False):
    return _dot_impl(a, b, ta, tb)


def _bdot_fwd(a, b, ta, tb):
    return _dot_impl(a, b, ta, tb), (a, b)


def _bdot_bwd(ta, tb, res, g):
    a, b = res
    if not ta and not tb:
        return _dot_impl(g, b, False, True), _dot_impl(a, g, True, False)
    if not ta and tb:
        return _dot_impl(g, b, False, False), _dot_impl(g, a, True, False)
    if ta and not tb:
        return _dot_impl(b, g, False, True), _dot_impl(a, g, False, False)
    raise NotImplementedError


_bdot.defvjp(_bdot_fwd, _bdot_bwd)


_NN = (((1,), (0,)), ((), ()))
_NT = (((1,), (1,)), ((), ()))
_TN = (((0,), (0,)), ((), ()))


def _dg2(x, e, x_is_lhs, dims):
    hi = x.astype(BF16)
    lo = (x - hi.astype(F32)).astype(BF16)
    e = e.astype(BF16)
    if x_is_lhs:
        return (lax.dot_general(hi, e, dims, preferred_element_type=F32)
                + lax.dot_general(lo, e, dims, preferred_element_type=F32))
    return (lax.dot_general(e, hi, dims, preferred_element_type=F32)
            + lax.dot_general(e, lo, dims, preferred_element_type=F32))


_EDOT_FWD = {"xe": (True, _NN), "ex": (False, _NN), "ext": (False, _NT)}
_EDOT_BWD = {"xe": (True, _NT), "ex": (False, _TN), "ext": (True, _TN)}


@functools.partial(jax.custom_vjp, nondiff_argnums=(2,))
def _edot(x, e, mode):
    return _dg2(x, e, *_EDOT_FWD[mode])


def _edot_fwd(x, e, mode):
    return _dg2(x, e, *_EDOT_FWD[mode]), e


def _edot_bwd(mode, e, g):
    return _dg2(g, e, *_EDOT_BWD[mode]), None


_edot.defvjp(_edot_fwd, _edot_bwd)


def _shift_impl(u, pos, per, s):
    n = u.shape[0]
    rolled = u if s == 0 else pltpu.roll(u, (-s) % n, 0)
    ok = (pos + s >= 0) & (pos + s < per)
    return jnp.where(ok, rolled, 0.0)


@functools.partial(jax.custom_vjp, nondiff_argnums=(3,))
def _shift(u, pos, per, s):
    return _shift_impl(u, pos, per, s)


def _shift_fwd(u, pos, per, s):
    return _shift_impl(u, pos, per, s), (pos, per)


def _shift_bwd(s, res, g):
    pos, per = res
    return _shift_impl(g, pos, per, -s), None, None


_shift.defvjp(_shift_fwd, _shift_bwd)


def _rms(x, w):
    return x * lax.rsqrt(jnp.mean(x * x, axis=-1, keepdims=True) + EPS) * w


def _silu(x):
    return x * jax.nn.sigmoid(x)


def _softplus(x):
    return jnp.maximum(x, 0.0) + jnp.log(1.0 + jnp.exp(-jnp.abs(x)))


def _logsig(x):
    return jnp.minimum(x, 0.0) - jnp.log(1.0 + jnp.exp(-jnp.abs(x)))


def _tri(n, rev):
    t = lax.broadcasted_iota(jnp.int32, (n, n), 0)
    s = lax.broadcasted_iota(jnp.int32, (n, n), 1)
    return (s >= t) if rev else (t >= s)


def _head_expand(first_lane):
    l = lax.broadcasted_iota(jnp.int32, (128, SSM_INNER), 0)
    c = lax.broadcasted_iota(jnp.int32, (128, SSM_INNER), 1)
    return (l == first_lane + lax.shift_right_logical(c, 6)).astype(F32)


def _exchange_ops(x_ref, o_ref, send_sems, recv_sems, local_sem, a2a):
    mx, my, mc = lax.axis_index("x"), lax.axis_index("y"), lax.axis_index("c")
    me = 4 * mx + 2 * my + mc
    ops = []
    for k in range(1, N_DEV):
        px = 1 - mx if k & 4 else mx
        py = 1 - my if k & 2 else my
        pc = 1 - mc if k & 1 else mc
        ops.append(pltpu.make_async_remote_copy(
            src_ref=x_ref.at[4 * px + 2 * py + pc] if a2a else x_ref, dst_ref=o_ref.at[me],
            send_sem=send_sems.at[k - 1], recv_sem=recv_sems.at[k - 1],
            device_id=(px, py, pc), device_id_type=pl.DeviceIdType.MESH))
    ops.append(pltpu.make_async_copy(x_ref.at[me] if a2a else x_ref, o_ref.at[me], local_sem))
    return ops


def _call(body, *, grid, in_specs, out_specs, out_shape, scratch=(), sem, name, args, hosted=()):
    n_in, n_out, n_s, n_h = len(in_specs), len(out_specs), len(scratch), len(hosted)
    if not n_h:
        return pl.pallas_call(body, grid=grid, in_specs=list(in_specs), out_specs=list(out_specs),
                              out_shape=list(out_shape), scratch_shapes=list(scratch),
                              compiler_params=_cparams(sem), name=name)(*args)
    b0, c0 = n_in + n_h, n_in + n_h + n_out
    d0 = c0 + n_h

    def wrapped(*refs):
        hx, ho = refs[n_in:b0], refs[c0:d0]
        send_sems, recv_sems, local_sems = refs[d0 + n_s:]
        ids = [pl.program_id(ax) for ax in range(len(grid))]
        first = functools.reduce(jnp.logical_and, [i == 0 for i in ids])
        last = functools.reduce(jnp.logical_and, [i == n - 1 for i, n in zip(ids, grid)])

        def ops():
            return [op for n, (_, a2a) in enumerate(hosted)
                    for op in _exchange_ops(hx[n], ho[n], send_sems.at[n], recv_sems.at[n], local_sems.at[n], a2a)]

        @pl.when(first)
        def _():
            for op in ops():
                op.start()

        body(*refs[:n_in], *refs[b0:c0], *refs[d0:d0 + n_s])

        @pl.when(last)
        def _():
            for op in ops():
                op.wait()

    hbm = pl.BlockSpec(memory_space=pl.ANY)
    h_shapes = [jax.ShapeDtypeStruct((N_DEV,) + tuple(x.shape[1:] if a2a else x.shape), x.dtype) for x, a2a in hosted]
    return pl.pallas_call(
        wrapped, grid=grid, in_specs=list(in_specs) + [hbm] * n_h, out_specs=list(out_specs) + [hbm] * n_h,
        out_shape=list(out_shape) + h_shapes,
        scratch_shapes=list(scratch) + [pltpu.SemaphoreType.DMA((n_h, N_DEV - 1)), pltpu.SemaphoreType.DMA((n_h, N_DEV - 1)),
                                        pltpu.SemaphoreType.DMA((n_h,))],
        compiler_params=_cparams(("arbitrary",) * len(grid)), name=name,
    )(*args, *[x for x, _ in hosted])


def _mm(a, b, *, trans_b, out_dtype, name, tm_t=768, tn_t=1024, tk_t=2816, hosted=()):
    M, K = a.shape
    N = b.shape[0] if trans_b else b.shape[1]
    tm, tn, tk = _pick(M, tm_t, 8), _pick(N, tn_t, 128), _pick(K, tk_t, 128)
    nk = K // tk
    dims = (((1,), (1,)), ((), ())) if trans_b else (((1,), (0,)), ((), ()))

    def body(a_ref, b_ref, o_ref, acc_ref):
        k = pl.program_id(2)
        p = lax.dot_general(a_ref[...], b_ref[...], dims, preferred_element_type=F32)

        @pl.when(k == 0)
        def _():
            acc_ref[...] = p

        @pl.when(k > 0)
        def _():
            acc_ref[...] += p

        @pl.when(k == nk - 1)
        def _():
            o_ref[...] = acc_ref[...].astype(out_dtype)

    b_spec = pl.BlockSpec((tn, tk), lambda j, i, k: (j, k)) if trans_b else pl.BlockSpec((tk, tn), lambda j, i, k: (k, j))
    outs = _call(
        body, grid=(N // tn, M // tm, nk),
        in_specs=[pl.BlockSpec((tm, tk), lambda j, i, k: (i, k)), b_spec],
        out_specs=[pl.BlockSpec((tm, tn), lambda j, i, k: (i, j))],
        out_shape=[jax.ShapeDtypeStruct((M, N), out_dtype)],
        scratch=[pltpu.VMEM((tm, tn), F32)],
        sem=("parallel", "parallel", "arbitrary"), name=name, args=(a, b), hosted=hosted)
    return outs if hosted else outs[0]


def _mm_tn(a, b, *, name, tm_t=768, tr_t=1024, tn_t=1408):
    M, R = a.shape
    N = b.shape[1]
    tm, tr, tn = _pick(M, tm_t, 8), _pick(R, tr_t, 128), _pick(N, tn_t, 128)

    def body(a_ref, b_ref, o_ref):
        m = pl.program_id(2)
        p = lax.dot_general(a_ref[...], b_ref[...], (((0,), (0,)), ((), ())), preferred_element_type=F32)

        @pl.when(m == 0)
        def _():
            o_ref[...] = p

        @pl.when(m > 0)
        def _():
            o_ref[...] += p

    return pl.pallas_call(
        body, grid=(R // tr, N // tn, M // tm),
        in_specs=[pl.BlockSpec((tm, tr), lambda r, j, m: (m, r)), pl.BlockSpec((tm, tn), lambda r, j, m: (m, j))],
        out_specs=pl.BlockSpec((tr, tn), lambda r, j, m: (r, j)),
        out_shape=jax.ShapeDtypeStruct((R, N), F32),
        compiler_params=_cparams(("parallel", "parallel", "arbitrary")), name=name,
    )(a, b)


def _col(arr, tm, width=None, col=0):
    width = arr.shape[1] if width is None else width
    return (arr, (tm, width), lambda i: (i, col))


def _lat(arr, tm, nct):
    return (arr, (tm, arr.shape[1]), lambda i: (jnp.maximum(i - nct, 0), 0))


def _ctx(arr, tm, nct):
    return (arr, (tm, arr.shape[1]), lambda i: (jnp.minimum(i, nct - 1), 0))


def _whole(p):
    return pl.BlockSpec(p.shape, lambda i, nd=p.ndim: (0,) * nd)


def _stage_fwd(f, n_tiles, ins, params, outs, name):
    ni, npar = len(ins), len(params)
    o_specs = [pl.BlockSpec((o[1], o[3]), lambda i: (0, i)) if len(o) > 4 else pl.BlockSpec((o[3], o[1]), lambda i: (i, 0))
               for o in outs]
    o_shapes = [jax.ShapeDtypeStruct((o[1], o[0]) if len(o) > 4 else (o[0], o[1]), o[2]) for o in outs]

    def body(*refs):
        i = pl.program_id(0)
        xs = [r[...].astype(F32) for r in refs[:ni]]
        ps = [r[...] for r in refs[ni:ni + npar]]
        for r, v in zip(refs[ni + npar:], f(i, xs, ps, False)):
            r[...] = v.astype(r.dtype)

    return pl.pallas_call(
        body, grid=(n_tiles,),
        in_specs=[pl.BlockSpec(bs, fn) for _, bs, fn in ins] + [_whole(p) for p in params],
        out_specs=o_specs, out_shape=o_shapes,
        compiler_params=_cparams(("parallel",)), name=name,
    )(*[a for a, _, _ in ins], *params)


def _stage_bwd(f, n_tiles, ins, params, cts, ct_fn, dins, name, hosted=()):
    ni, npar, nc = len(ins), len(params), len(cts)
    want = [k for k, d in enumerate(dins) if d is not None]
    extras = [dins[k][1] for k in want if dins[k][1] is not None]

    def body(*refs):
        i = pl.program_id(0)
        xs = [r[...].astype(F32) for r in refs[:ni]]
        ps = [r[...] for r in refs[ni:ni + npar]]
        ct_tiles = [r[...].astype(F32) for r in refs[ni + npar:ni + npar + nc]]
        ex_refs = list(refs[ni + npar + nc:ni + npar + nc + len(extras)])
        out_refs = refs[ni + npar + nc + len(extras):]
        _, vjp = jax.vjp(lambda xs_, ps_: tuple(f(i, xs_, ps_, True)), xs, ps)
        dxs, dps = vjp(tuple(ct_fn(i, ct_tiles)))
        for n, k in enumerate(want):
            v = dxs[k]
            if dins[k][1] is not None:
                v = v + ex_refs.pop(0)[...].astype(F32)
            out_refs[n][...] = v.astype(out_refs[n].dtype)
        for r, v in zip(out_refs[len(want):], dps):
            @pl.when(i == 0)
            def _(r=r, v=v):
                r[...] = v

            @pl.when(i > 0)
            def _(r=r, v=v):
                r[...] += v

    return _call(
        body, grid=(n_tiles,),
        in_specs=([pl.BlockSpec(bs, fn) for _, bs, fn in ins] + [_whole(p) for p in params]
                  + [pl.BlockSpec(bs, fn) for _, bs, fn in cts] + [pl.BlockSpec(bs, fn) for _, bs, fn in extras]),
        out_specs=([pl.BlockSpec(ins[k][1], lambda i, fn=ins[k][2]: (fn(i)[0], 0)) for k in want]
                   + [_whole(p) for p in params]),
        out_shape=([jax.ShapeDtypeStruct((ins[k][0].shape[0], ins[k][1][1]), dins[k][0]) for k in want]
                   + [jax.ShapeDtypeStruct(p.shape, F32) for p in params]),
        sem=("arbitrary",), name=name,
        args=(*[a for a, _, _ in ins], *params, *[a for a, _, _ in cts], *[a for a, _, _ in extras]), hosted=hosted)


def _mod_rows(i, nct, ada, lo):
    return jnp.where(i < nct, ada[1:2, lo:lo + D], ada[0:1, lo:lo + D])


def _f_norm1(nct):
    def f(i, xs, ps, diff):
        ctx, x = xs
        nw, ada = ps
        xt = jnp.where(i < nct, ctx, x)
        return (_rms(xt, nw) * (1.0 + _mod_rows(i, nct, ada, D)) + _mod_rows(i, nct, ada, 0),)
    return f


def _f_gates_dt(i, xs, ps, diff):
    small, xc = xs
    up_f, up_b, gb_f, gb_b, dtb, alog = ps
    dot = _bdot if diff else _dot_impl_nn
    lg_f = _logsig(dot(small, up_f) + gb_f) * (1.0 / GLA_TAU)
    lg_b = _logsig(dot(small, up_b) + gb_b) * (1.0 / GLA_TAU)
    dtp = _softplus(small + dtb)
    aa = -jnp.exp(alog) * dtp
    e_f, e_b = _head_expand(32), _head_expand(64)
    n = small.shape[0]
    t = lax.broadcasted_iota(jnp.int32, (n, n), 0)
    s = lax.broadcasted_iota(jnp.int32, (n, n), 1)
    same = lax.shift_right_logical(t, 6) == lax.shift_right_logical(s, 6)
    pre_g, suf_g = ((t >= s) & same).astype(F32), ((s >= t) & same).astype(F32)
    pre_s, suf_s = (t >= s).astype(F32), (s >= t).astype(F32)
    cum_f, cum_b = _edot(aa, pre_s, "ex"), _edot(aa, suf_s, "ex")
    h = lax.broadcasted_iota(jnp.int32, (SSM_G * SSM_HPG, 128), 0)
    l = lax.broadcasted_iota(jnp.int32, (SSM_G * SSM_HPG, 128), 1)
    cum_ft = _edot(cum_f, (l == h + 32).astype(F32), "ext")
    cum_bt = _edot(cum_b, (l == h + 64).astype(F32), "ext")
    return (_edot(lg_f, pre_g, "ex"), _edot(lg_b, suf_g, "ex"), xc * _edot(dtp, e_f, "xe"), xc * _edot(dtp, e_b, "xe"),
            _edot(cum_f, e_f, "xe"), _edot(cum_b, e_b, "xe"), cum_ft, cum_bt)


def _dot_impl_nn(a, b):
    return _dot_impl(a, b, False, False)


def _f_conv(nct, tc):
    def f(i, xs, ps, diff):
        (u,) = xs
        cw, cb = ps
        n = u.shape[0]
        t = lax.broadcasted_iota(jnp.int32, (n, 1), 0)
        per = jnp.where(i < nct, tc, GRID_W)
        pos = jnp.bitwise_and(t, per - 1)
        sh = _shift if diff else _shift_impl
        acc = cb + sh(u, pos, per, -2) * cw[0:1]
        for j in range(1, 4):
            acc = acc + sh(u, pos, per, j - 2) * cw[j:j + 1]
        return (_silu(acc),)
    return f


def _f_gla_post(i, xs, ps, diff):
    ogf, ogb, r = xs
    (gw,) = ps
    o = ogf + ogb
    parts = [_rms(o[:, h * GLA_DV:(h + 1) * GLA_DV], gw) for h in range(GLA_H)]
    return (jnp.concatenate(parts, axis=1) * _silu(r),)


def _f_ssd_post(i, xs, ps, diff):
    yf, yb, xc, z = xs
    dsk8, nw = ps
    dsk = _edot(dsk8, _head_expand(0), "xe")[0:1]
    y = (yf + yb + dsk * xc) * _silu(z)
    w = SSM_INNER // SSM_G
    parts = [_rms(y[:, g * w:(g + 1) * w], nw[:, g * w:(g + 1) * w]) for g in range(SSM_G)]
    return (jnp.concatenate(parts, axis=1),)


def _f_merge(i, xs, ps, diff):
    ga, gb, ya, yb = xs
    return (jax.nn.sigmoid(ga) * ya + jax.nn.sigmoid(gb) * yb,)


def _f_res1(nct):
    def f(i, xs, ps, diff):
        ctx, x, mix = xs
        ada, nw = ps
        h2 = jnp.where(i < nct, ctx, x) + _mod_rows(i, nct, ada, 2 * D) * mix
        return (h2, _rms(h2, nw) * (1.0 + _mod_rows(i, nct, ada, 4 * D)) + _mod_rows(i, nct, ada, 3 * D))
    return f


def _f_swiglu(i, xs, ps, diff):
    (gu,) = xs
    return (_silu(gu[:, :D_FF]) * gu[:, D_FF:],)


def _loss_head(h2, dn, tgt, ada, fw, nct):
    A = h2.shape[0]
    n_tiles = A // TM

    def tile_loss(i, h2t, dnt, tg, ada_, fw_):
        h3 = h2t + ada_[0:1, 5 * D:6 * D] * dnt
        err = _rms(h3, fw_) - tg
        row = 0.5 * jnp.mean(err * err, axis=-1, keepdims=True)
        return jnp.sum(row, axis=0, keepdims=True) * jnp.where(i < nct, 0.0, 1.0)

    def body(h2_ref, dn_ref, tg_ref, ada_ref, fw_ref, loss_ref, dh_ref, ddn_ref, dada_ref, dfw_ref):
        i = pl.program_id(0)
        val, vjp = jax.vjp(functools.partial(tile_loss, i), h2_ref[...], dn_ref[...].astype(F32), tg_ref[...], ada_ref[...],
                           fw_ref[...])
        dh, ddn, _, dada, dfw = vjp(jnp.ones((1, 1), F32))
        dh_ref[...] = dh
        ddn_ref[...] = ddn.astype(BF16)
        lv = jnp.broadcast_to(val, loss_ref.shape)
        for r, v in ((loss_ref, lv), (dada_ref, dada), (dfw_ref, dfw)):
            @pl.when(i == 0)
            def _(r=r, v=v):
                r[...] = v

            @pl.when(i > 0)
            def _(r=r, v=v):
                r[...] += v

    row = lambda i: (i, 0)
    return pl.pallas_call(
        body, grid=(n_tiles,),
        in_specs=[pl.BlockSpec((TM, D), row), pl.BlockSpec((TM, D), row),
                  pl.BlockSpec((TM, D), lambda i: (jnp.maximum(i - nct, 0), 0)), _whole(ada), _whole(fw)],
        out_specs=[pl.BlockSpec((8, 128), lambda i: (0, 0)), pl.BlockSpec((TM, D), row), pl.BlockSpec((TM, D), row),
                   _whole(ada), _whole(fw)],
        out_shape=[jax.ShapeDtypeStruct((8, 128), F32), jax.ShapeDtypeStruct((A, D), F32),
                   jax.ShapeDtypeStruct((A, D), BF16), jax.ShapeDtypeStruct(ada.shape, F32),
                   jax.ShapeDtypeStruct(fw.shape, F32)],
        compiler_params=_cparams(("arbitrary",)), name="loss_head",
    )(h2, dn, tgt, ada, fw)


def _chunk_of(step, n_chunks, n_ctx_chunks, rev):
    if not rev:
        return step
    return jnp.where(step < n_ctx_chunks, n_ctx_chunks - 1 - step, n_chunks - 1 - (step - n_ctx_chunks))


def _gla_step(st, q, k, v, b, rev):
    tri = _tri(GLA_C, rev).astype(F32)
    last = 0 if rev else GLA_C - 1
    outs, sts = [], []
    for h in range(GLA_H):
        kk = slice(h * GLA_DK, (h + 1) * GLA_DK)
        vv = slice(h * GLA_DV, (h + 1) * GLA_DV)
        qh, kh, vh, bh, sh = q[:, kk] * (GLA_DK ** -0.5), k[:, kk], v[:, vv], b[:, kk], st[vv]
        tot = bh[last:last + 1]
        mid = bh[GLA_C // 2:GLA_C // 2 + 1]
        att = _bdot(qh * jnp.exp(bh - mid), kh * jnp.exp(mid - bh), False, True) * tri
        outs.append(_bdot(att, vh) + _bdot(qh * jnp.exp(bh), sh, False, True))
        sts.append(sh * jnp.exp(tot) + _bdot(vh, kh * jnp.exp(tot - bh), True, False))
    return jnp.concatenate(sts, axis=0), jnp.concatenate(outs, axis=1)


_GQK, _GV = GLA_H * GLA_DK, GLA_H * GLA_DV


def _gla_fwd(proj, bg, ncc, rev, name):
    A = proj.shape[0]
    nc = A // GLA_C
    ch = lambda s: _chunk_of(s, nc, ncc, rev)

    def body(q_ref, k_ref, v_ref, b_ref, o_ref, ss_ref, st_ref):
        @pl.when(pl.program_id(0) == 0)
        def _():
            st_ref[...] = jnp.zeros_like(st_ref)

        st = st_ref[...]
        ss_ref[0] = st
        st_new, o = _gla_step(st, q_ref[...].astype(F32), k_ref[...].astype(F32), v_ref[...].astype(F32), b_ref[...], rev)
        o_ref[...] = o.astype(o_ref.dtype)
        st_ref[...] = st_new

    return pl.pallas_call(
        body, grid=(nc,),
        in_specs=[pl.BlockSpec((GLA_C, _GQK), lambda s: (ch(s), C_Q // _GQK)),
                  pl.BlockSpec((GLA_C, _GQK), lambda s: (ch(s), C_K // _GQK)),
                  pl.BlockSpec((GLA_C, _GV), lambda s: (ch(s), C_V // _GV)),
                  pl.BlockSpec((GLA_C, _GQK), lambda s: (ch(s), 0))],
        out_specs=[pl.BlockSpec((GLA_C, _GV), lambda s: (ch(s), 0)),
                   pl.BlockSpec((1, _GV, GLA_DK), lambda s: (s, 0, 0))],
        out_shape=[jax.ShapeDtypeStruct((A, _GV), BF16), jax.ShapeDtypeStruct((nc, _GV, GLA_DK), F32)],
        scratch_shapes=[pltpu.VMEM((_GV, GLA_DK), F32)],
        compiler_params=_cparams(("arbitrary",)), name=name,
    )(proj, proj, proj, bg)


def _gla_bwd(proj, lg, saved, d_o, prev, ncc, rev, name):
    A = proj.shape[0]
    nc = A // GLA_C
    st_of = lambda r: nc - 1 - r
    ch = lambda r: _chunk_of(st_of(r), nc, ncc, rev)
    qs = pl.BlockSpec((GLA_C, _GQK), lambda r: (ch(r), 0))
    vs = pl.BlockSpec((GLA_C, _GV), lambda r: (ch(r), 0))
    n_prev = 0 if prev is None else 3

    def body(*refs):
        q_ref, k_ref, v_ref, b_ref, ss_ref, do_ref = refs[:6]
        p_refs = refs[6:6 + n_prev]
        dq_ref, dk_ref, dv_ref, db_ref, ds_ref = refs[6 + n_prev:]

        @pl.when(pl.program_id(0) == 0)
        def _():
            ds_ref[...] = jnp.zeros_like(ds_ref)

        _, vjp = jax.vjp(functools.partial(_gla_step, rev=rev), ss_ref[0], q_ref[...].astype(F32), k_ref[...].astype(F32),
                         v_ref[...].astype(F32), b_ref[...])
        ds, dq, dk, dv, db = vjp((ds_ref[...], do_ref[...].astype(F32)))
        if n_prev:
            dq, dk, dv = [d + p[...].astype(F32) for d, p in zip((dq, dk, dv), p_refs)]
        for r, d in zip((dq_ref, dk_ref, dv_ref, db_ref), (dq, dk, dv, db)):
            r[...] = d.astype(r.dtype)
        ds_ref[...] = ds

    in_specs = [
        pl.BlockSpec((GLA_C, _GQK), lambda r: (ch(r), C_Q // _GQK)),
        pl.BlockSpec((GLA_C, _GQK), lambda r: (ch(r), C_K // _GQK)),
        pl.BlockSpec((GLA_C, _GV), lambda r: (ch(r), C_V // _GV)),
        qs,
        pl.BlockSpec((1, _GV, GLA_DK), lambda r: (st_of(r), 0, 0)),
        vs,
    ] + ([qs, qs, vs] if n_prev else [])
    return pl.pallas_call(
        body, grid=(nc,), in_specs=in_specs, out_specs=[qs, qs, vs, qs],
        out_shape=[jax.ShapeDtypeStruct((A, _GQK), BF16), jax.ShapeDtypeStruct((A, _GQK), BF16),
                   jax.ShapeDtypeStruct((A, _GV), BF16), jax.ShapeDtypeStruct((A, _GQK), F32)],
        scratch_shapes=[pltpu.VMEM((_GV, GLA_DK), F32)],
        compiler_params=_cparams(("arbitrary",)), name=name,
    )(proj, proj, proj, lg, saved, d_o, *(prev or ()))


def _ssd_step(st, x, bm, cm, cum, cum_t, rev):
    mask = _tri(SSM_C, rev)
    last = 0 if rev else SSM_C - 1
    tot = cum[last:last + 1]
    cb = _bdot(cm, bm, False, True)
    ys = []
    for e in range(SSM_HPG):
        lm = jnp.exp(jnp.where(mask, cum[:, e * SSM_P:e * SSM_P + 1] - cum_t[e:e + 1], -jnp.inf))
        ys.append(_bdot(cb * lm, x[:, e * SSM_P:(e + 1) * SSM_P]))
    y = jnp.concatenate(ys, axis=1) + _bdot(cm, st) * jnp.exp(cum)
    st_new = st * jnp.exp(tot) + _bdot(bm, x * jnp.exp(tot - cum), True, False)
    return st_new, y


_XBC_B = SSM_INNER // SSM_N
_XBC_C = _XBC_B + SSM_G


def _ssd_fwd(xd, xbc, cum, cum_t, ncc, rev, name, hosted=()):
    A = xd.shape[0]
    nc = A // SSM_C
    W = SSM_HPG * SSM_P
    ch = lambda s: _chunk_of(s, nc, ncc, rev)

    def body(x_ref, b_ref, c_ref, a_ref, at_ref, y_ref, ss_ref, st_ref):
        @pl.when(pl.program_id(1) == 0)
        def _():
            st_ref[...] = jnp.zeros_like(st_ref)

        st = st_ref[...]
        ss_ref[0, 0] = st
        st_new, y = _ssd_step(st, x_ref[...].astype(F32), b_ref[...].astype(F32), c_ref[...].astype(F32), a_ref[...],
                              at_ref[...], rev)
        y_ref[...] = y.astype(y_ref.dtype)
        st_ref[...] = st_new

    gs = pl.BlockSpec((SSM_C, W), lambda g, s: (ch(s), g))
    return _call(
        body, grid=(SSM_G, nc),
        in_specs=[gs, pl.BlockSpec((SSM_C, SSM_N), lambda g, s: (ch(s), _XBC_B + g)),
                  pl.BlockSpec((SSM_C, SSM_N), lambda g, s: (ch(s), _XBC_C + g)), gs,
                  pl.BlockSpec((SSM_HPG, SSM_C), lambda g, s: (g, ch(s)))],
        out_specs=[gs, pl.BlockSpec((1, 1, SSM_N, W), lambda g, s: (g, s, 0, 0))],
        out_shape=[jax.ShapeDtypeStruct((A, SSM_INNER), BF16), jax.ShapeDtypeStruct((SSM_G, nc, SSM_N, W), F32)],
        scratch=[pltpu.VMEM((SSM_N, W), F32)],
        sem=("parallel", "arbitrary"), name=name, args=(xd, xbc, xbc, cum, cum_t), hosted=hosted)


def _ssd_bwd(xd, xbc, cum, cum_t, saved, d_y, prev, ncc, rev, name):
    A = xd.shape[0]
    nc = A // SSM_C
    W = SSM_HPG * SSM_P
    st_of = lambda r: nc - 1 - r
    ch = lambda r: _chunk_of(st_of(r), nc, ncc, rev)
    gs = pl.BlockSpec((SSM_C, W), lambda g, r: (ch(r), g))
    ns = pl.BlockSpec((SSM_C, SSM_N), lambda g, r: (ch(r), g))
    ts = pl.BlockSpec((SSM_HPG, SSM_C), lambda g, r: (g, ch(r)))
    n_prev = 0 if prev is None else 2

    def body(*refs):
        x_ref, b_ref, c_ref, a_ref, at_ref, ss_ref, dy_ref = refs[:7]
        p_refs = refs[7:7 + n_prev]
        dx_ref, db_ref, dc_ref, da_ref, dat_ref, ds_ref = refs[7 + n_prev:]

        @pl.when(pl.program_id(1) == 0)
        def _():
            ds_ref[...] = jnp.zeros_like(ds_ref)

        _, vjp = jax.vjp(functools.partial(_ssd_step, rev=rev), ss_ref[0, 0], x_ref[...].astype(F32), b_ref[...].astype(F32),
                         c_ref[...].astype(F32), a_ref[...], at_ref[...])
        ds, dx, db, dc, da, dat = vjp((ds_ref[...], dy_ref[...].astype(F32)))
        if n_prev:
            db, dc = db + p_refs[0][...].astype(F32), dc + p_refs[1][...].astype(F32)
        for r, d in zip((dx_ref, db_ref, dc_ref, da_ref, dat_ref), (dx, db, dc, da, dat)):
            r[...] = d.astype(r.dtype)
        ds_ref[...] = ds

    in_specs = [gs, pl.BlockSpec((SSM_C, SSM_N), lambda g, r: (ch(r), _XBC_B + g)),
                pl.BlockSpec((SSM_C, SSM_N), lambda g, r: (ch(r), _XBC_C + g)), gs, ts,
                pl.BlockSpec((1, 1, SSM_N, W), lambda g, r: (g, st_of(r), 0, 0)), gs] + ([ns, ns] if n_prev else [])
    return pl.pallas_call(
        body, grid=(SSM_G, nc), in_specs=in_specs, out_specs=[gs, ns, ns, gs, ts],
        out_shape=[jax.ShapeDtypeStruct((A, SSM_INNER), BF16), jax.ShapeDtypeStruct((A, SSM_G * SSM_N), BF16),
                   jax.ShapeDtypeStruct((A, SSM_G * SSM_N), BF16), jax.ShapeDtypeStruct((A, SSM_INNER), F32),
                   jax.ShapeDtypeStruct((SSM_G * SSM_HPG, A), F32)],
        scratch_shapes=[pltpu.VMEM((SSM_N, W), F32)],
        compiler_params=_cparams(("parallel", "arbitrary")), name=name,
    )(xd, xbc, xbc, cum, cum_t, saved, d_y, *(prev or ()))


_ADA_ROWS = 16


def _ada_fwd(cc, w_shard, b_shard):
    n = w_shard.shape[1]

    def body(cc_ref, w_ref, b_ref, o_ref):
        o_ref[...] = _dot_impl(_silu(cc_ref[...]), w_ref[...], False, False) + b_ref[...]

    return pl.pallas_call(body, out_shape=jax.ShapeDtypeStruct((_ADA_ROWS, n), F32),
                          compiler_params=_cparams(()), name="ada_fwd")(cc, w_shard, b_shard)


def _ada_bwd(cc, w_shard, d_all, d_mine):
    n = w_shard.shape[1]

    def rows(ref, r):
        parts = []
        for s in range(N_DEV):
            parts.append(ref[6 * s + r:6 * s + r + 1] + ref[6 * s + 2 + r:6 * s + 3 + r] + ref[6 * s + 4 + r:6 * s + 5 + r])
        return parts

    def total(parts):
        t = parts[0]
        for p in parts[1:]:
            t = t + p
        return t

    def body(cc_ref, w_ref, da_ref, dm_ref, dw_ref, db_ref, dcc_ref):
        dd = jnp.concatenate(rows(dm_ref, 0) + [total(rows(dm_ref, 1)), jnp.zeros((_ADA_ROWS - N_DEV - 1, n), F32)], axis=0)
        cc = cc_ref[...]
        dw_ref[...] = _dot_impl(_silu(cc), dd, True, False)
        db_ref[...] = total(rows(da_ref, 0)) + total(rows(da_ref, 1))
        _, vjp = jax.vjp(_silu, cc)
        dcc_ref[...] = vjp(_dot_impl(dd, w_ref[...], False, True))[0]

    return pl.pallas_call(
        body, out_shape=[jax.ShapeDtypeStruct((D, n), F32), jax.ShapeDtypeStruct((1, d_all.shape[1]), F32),
                         jax.ShapeDtypeStruct((_ADA_ROWS, D), F32)],
        compiler_params=_cparams(()), name="ada_bwd")(cc, w_shard, d_all, d_mine)


def _local_step(x, ctx, target, ada, w, shards):
    T, Tc = x.shape[0], ctx.shape[0]
    assert Tc == TM and T % TM == 0 and GRID_W == GLA_C and TS == SSM_C == 2 * GLA_C
    A = T + Tc
    nct = Tc // TM
    n_tm, n_ts = A // TM, A // TS
    g = {}

    x_in = [_ctx(ctx, TM, nct), _lat(x, TM, nct)]
    f_norm1 = _f_norm1(nct)
    p_norm1 = [w["norm1_w"], ada]
    (h1,) = _stage_fwd(f_norm1, n_tm, x_in, p_norm1, [(A, D, BF16, TM)], "norm1")
    proj = _mm(h1, w["w_in"], trans_b=False, out_dtype=BF16, name="mm_in", tn_t=1152)
    small = _mm(h1, w["w_in"][:, C_S:], trans_b=False, out_dtype=F32, name="mm_in_small")

    p_gd = [w["up_f"], w["up_b"], w["gla_bias_f"], w["gla_bias_b"], w["dtb"], w["alog"]]
    f_conv = _f_conv(nct, Tc)
    p_conv = [w["conv_w"], w["conv_b"]]
    in_conv = [_col(proj, TM, 3072, C_U // 3072)]
    (xbc,) = _stage_fwd(f_conv, n_tm, in_conv, p_conv, [(A, 3072, BF16, TM)], "conv")
    in_gd = [_col(small, TS), _col(xbc, TS, SSM_INNER, 0)]
    n_heads = SSM_G * SSM_HPG
    lg_f, lg_b, xf, xb, axf, axb, atf, atb = _stage_fwd(
        _f_gates_dt, n_ts, in_gd, p_gd,
        [(A, 512, F32, TS)] * 2 + [(A, SSM_INNER, BF16, TS)] * 2 + [(A, SSM_INNER, F32, TS)] * 2
        + [(A, n_heads, F32, TS, True)] * 2,
        "gates_dt")

    ncc_g, ncc_s = Tc // GLA_C, Tc // SSM_C
    ogf, sv_gf = _gla_fwd(proj, lg_f, ncc_g, False, "gla_f")
    ogb, sv_gb = _gla_fwd(proj, lg_b, ncc_g, True, "gla_b")
    ysf, sv_sf, ag_pa, ag_pb, ag_out = _ssd_fwd(
        xf, xbc, axf, atf, ncc_s, False, "ssd_f", hosted=[(shards[n], False) for n in ("w_pa", "w_pb", "w_out")])
    ysb, sv_sb, ag_gate, ag_up, ag_down = _ssd_fwd(
        xb, xbc, axb, atb, ncc_s, True, "ssd_b", hosted=[(shards[n], False) for n in ("w_gate", "w_up", "w_down")])
    w = dict(w, w_pa=ag_pa.reshape(D, D), w_pb=ag_pb.reshape(SSM_INNER, D), w_out=ag_out.reshape(D, D),
             w_gu=jnp.concatenate([_unshard_cols(ag_gate), _unshard_cols(ag_up)], axis=1), w_down=ag_down.reshape(D_FF, D))
    got = {}

    in_gp = [_col(ogf, TS), _col(ogb, TS), _col(proj, TS, 1024, C_R // 1024)]
    p_gp = [w["gla_norm_w"]]
    (oa,) = _stage_fwd(_f_gla_post, n_ts, in_gp, p_gp, [(A, D, BF16, TS)], "gla_post")
    in_sp = [_col(ysf, TS), _col(ysb, TS), _col(xbc, TS, SSM_INNER, 0), _col(proj, TS, SSM_INNER, C_Z // SSM_INNER)]
    p_sp = [w["dsk8"], w["ssm_norm_w"]]
    (ob,) = _stage_fwd(_f_ssd_post, n_ts, in_sp, p_sp, [(A, SSM_INNER, BF16, TS)], "ssd_post")
    ya = _mm(oa, w["w_pa"], trans_b=False, out_dtype=BF16, name="mm_pa")
    yb = _mm(ob, w["w_pb"], trans_b=False, out_dtype=BF16, name="mm_pb")
    in_mg = [_col(proj, TS, 1024, C_GA // 1024), _col(proj, TS, 1024, C_GB // 1024), _col(ya, TS), _col(yb, TS)]
    (merged,) = _stage_fwd(_f_merge, n_ts, in_mg, [], [(A, D, BF16, TS)], "merge")
    mix = _mm(merged, w["w_out"], trans_b=False, out_dtype=BF16, name="mm_out")

    f_res1 = _f_res1(nct)
    in_r1 = x_in + [_col(mix, TM)]
    p_r1 = [ada, w["norm2_w"]]
    h2, hm2 = _stage_fwd(f_res1, n_tm, in_r1, p_r1, [(A, D, F32, TM), (A, D, BF16, TM)], "res1")
    gu = _mm(hm2, w["w_gu"], trans_b=False, out_dtype=BF16, name="mm_gu", tn_t=1408)
    in_sw = [_col(gu, TS)]
    (act,) = _stage_fwd(_f_swiglu, n_ts, in_sw, [], [(A, D_FF, BF16, TS)], "swiglu")
    dn = _mm(act, w["w_down"], trans_b=False, out_dtype=BF16, name="mm_down")

    loss_blk, d_h2a, d_dn, d_ada3, g["final_norm_w"] = _loss_head(h2, dn, target, ada, w["final_norm_w"], nct)
    rows16 = lambda gw: (gw.reshape(N_DEV, gw.shape[0] // N_DEV, gw.shape[1]).astype(BF16), True)
    cols16 = lambda gw: (_shard_cols(gw).astype(BF16), True)
    gw_down = _mm_tn(act, d_dn, name="dw_down", tr_t=1408)
    d_act = _mm(d_dn, w["w_down"], trans_b=True, out_dtype=BF16, name="dx_down", tn_t=1408)
    d_gu, got["w_down"] = _stage_bwd(_f_swiglu, n_ts, in_sw, [], [_col(d_act, TS)], lambda i, t: t, [(BF16, None)],
                                     "swiglu_b", hosted=[rows16(gw_down)])
    gw_gu = _mm_tn(hm2, d_gu, name="dw_gu")
    d_hm2 = _mm(d_gu, w["w_gu"], trans_b=True, out_dtype=BF16, name="dx_gu")
    d_x1, d_mix, d_ada2, g["norm2_w"], got["w_gate"] = _stage_bwd(
        f_res1, n_tm, in_r1, p_r1, [_col(d_h2a, TM), _col(d_hm2, TM)], lambda i, t: t,
        [None, (F32, None), (BF16, None)], "res1_b", hosted=[cols16(gw_gu[:, :D_FF])])
    gw_out = _mm_tn(merged, d_mix, name="dw_out")
    d_merged = _mm(d_mix, w["w_out"], trans_b=True, out_dtype=BF16, name="dx_out")
    d_ga, d_gb, d_ya, d_yb, got["w_up"] = _stage_bwd(
        _f_merge, n_ts, in_mg, [], [_col(d_merged, TS)], lambda i, t: t, [(BF16, None)] * 4, "merge_b",
        hosted=[cols16(gw_gu[:, D_FF:])])
    gw_pa = _mm_tn(oa, d_ya, name="dw_pa")
    gw_pb = _mm_tn(ob, d_yb, name="dw_pb")
    d_oa = _mm(d_ya, w["w_pa"], trans_b=True, out_dtype=BF16, name="dx_pa")
    d_ob = _mm(d_yb, w["w_pb"], trans_b=True, out_dtype=BF16, name="dx_pb")
    d_og, d_r, g["gla_norm_w"], got["w_out"] = _stage_bwd(
        _f_gla_post, n_ts, in_gp, p_gp, [_col(d_oa, TS)], lambda i, t: t, [(BF16, None), None, (BF16, None)], "gla_post_b",
        hosted=[rows16(gw_out)])
    d_ys, d_xs_skip, d_z, g["dsk8"], g["ssm_norm_w"], got["w_pa"], got["w_pb"] = _stage_bwd(
        _f_ssd_post, n_ts, in_sp, p_sp, [_col(d_ob, TS)], lambda i, t: t,
        [(BF16, None), None, (BF16, None), (BF16, None)], "ssd_post_b", hosted=[rows16(gw_pa), rows16(gw_pb)])

    dq, dk, dv, d_lgf = _gla_bwd(proj, lg_f, sv_gf, d_og, None, ncc_g, False, "gla_f_b")
    dq, dk, dv, d_lgb = _gla_bwd(proj, lg_b, sv_gb, d_og, (dq, dk, dv), ncc_g, True, "gla_b_b")
    d_xf, d_bm, d_cm, d_axf, d_atf = _ssd_bwd(xf, xbc, axf, atf, sv_sf, d_ys, None, ncc_s, False, "ssd_f_b")
    d_xb, d_bm, d_cm, d_axb, d_atb = _ssd_bwd(xb, xbc, axb, atb, sv_sb, d_ys, (d_bm, d_cm), ncc_s, True, "ssd_b_b")

    cts_gd = [_col(a, TS) for a in (d_lgf, d_lgb, d_xf, d_xb, d_axf, d_axb)]
    cts_gd += [(a, (n_heads, TS), lambda i: (0, i)) for a in (d_atf, d_atb)]
    d_small, d_xs_dt, g["up_f"], g["up_b"], g["gla_bias_f"], g["gla_bias_b"], g["dtb"], g["alog"] = _stage_bwd(
        _f_gates_dt, n_ts, in_gd, p_gd, cts_gd, lambda i, t: t, [(BF16, None), (BF16, None)], "gates_dt_b")
    cts_conv = [_col(d_xs_skip, TM), _col(d_xs_dt, TM), _col(d_bm, TM), _col(d_cm, TM)]
    d_u, g["conv_w"], g["conv_b"] = _stage_bwd(
        f_conv, n_tm, in_conv, p_conv, cts_conv,
        lambda i, t: [jnp.concatenate([t[0] + t[1], t[2], t[3]], axis=1)], [(BF16, None)], "conv_b")

    d_proj = jnp.concatenate([dq, dk, dv, d_r, d_u, d_z, d_ga, d_gb, d_small], axis=1)
    gw_in = _mm_tn(h1, d_proj, name="dw_in", tn_t=1152)
    d_h1, got["w_in"] = _mm(d_proj, w["w_in"], trans_b=True, out_dtype=BF16, name="dx_in", tk_t=1152,
                            hosted=[cols16(_from_padded(gw_in))])
    grad_x, g["norm1_w"], d_ada1 = _stage_bwd(
        f_norm1, n_tm, x_in, p_norm1, [_col(d_h1, TM)], lambda i, t: t,
        [None, (F32, (d_x1, (TM, D), x_in[1][2]))], "norm1_b")
    return loss_blk, grad_x, g, got, (d_ada1, d_ada2, d_ada3)


def _exchange(x, a2a, name):
    shp = x.shape[1:] if a2a else x.shape

    def body(x_ref, o_ref, send_sems, recv_sems, local_sem):
        ops = _exchange_ops(x_ref, o_ref, send_sems, recv_sems, local_sem, a2a)
        for op in ops:
            op.start()
        for op in ops:
            op.wait()

    return pl.pallas_call(
        body, out_shape=jax.ShapeDtypeStruct((N_DEV,) + tuple(shp), x.dtype),
        in_specs=[pl.BlockSpec(memory_space=pl.ANY)], out_specs=pl.BlockSpec(memory_space=pl.ANY),
        scratch_shapes=[pltpu.SemaphoreType.DMA((N_DEV - 1,)), pltpu.SemaphoreType.DMA((N_DEV - 1,)),
                        pltpu.SemaphoreType.DMA(())],
        name=name,
    )(x)


def _all_gather(x, name):
    def body(x_ref, o_ref, send_sems, recv_sems, local_sem):
        mx, my, mc = lax.axis_index("x"), lax.axis_index("y"), lax.axis_index("c")
        me, sibling = (mx, my, mc), (mx, my, 1 - mc)
        chips = [(1 - mx, my), (mx, 1 - my), (1 - mx, 1 - my)]

        def slot(px, py, pc):
            return o_ref.at[4 * px + 2 * py + pc]

        def copy(k, block, to, src=None):
            return pltpu.make_async_remote_copy(
                src_ref=slot(*block) if src is None else src, dst_ref=slot(*block),
                send_sem=send_sems.at[k], recv_sem=recv_sems.at[k], device_id=to, device_id_type=pl.DeviceIdType.MESH)

        own = pltpu.make_async_copy(x_ref, slot(*me), local_sem)
        own.start()
        first = [copy(0, me, sibling, src=x_ref)] + [copy(1 + j, me, (*chip, mc), src=x_ref) for j, chip in enumerate(chips)]
        for cp in first:
            cp.start()
        passed = [copy(4 + j, (*chip, mc), sibling) for j, chip in enumerate(chips)]
        for j, chip in enumerate(chips):
            copy(1 + j, (*chip, mc), me).wait_recv()
            passed[j].start()
        copy(0, sibling, me).wait_recv()
        for j, chip in enumerate(chips):
            copy(4 + j, (*chip, 1 - mc), me).wait_recv()
        for cp in first + passed:
            cp.wait_send()
        own.wait()

    return pl.pallas_call(
        body, out_shape=jax.ShapeDtypeStruct((N_DEV,) + tuple(x.shape), x.dtype),
        in_specs=[pl.BlockSpec(memory_space=pl.ANY)], out_specs=pl.BlockSpec(memory_space=pl.ANY),
        scratch_shapes=[pltpu.SemaphoreType.DMA((N_DEV - 1,)), pltpu.SemaphoreType.DMA((N_DEV - 1,)),
                        pltpu.SemaphoreType.DMA(())],
        name=name,
    )(x)


def _adamw_math(w, gr, m, v):
    m = ADAM_B1 * m + (1.0 - ADAM_B1) * gr
    v = ADAM_B2 * v + (1.0 - ADAM_B2) * (gr * gr)
    m_hat = m / np.float32(1.0 - ADAM_B1 ** ADAM_STEP)
    v_hat = v / np.float32(1.0 - ADAM_B2 ** ADAM_STEP)
    delta = -ADAM_LR * (m_hat / (jnp.sqrt(v_hat) + ADAM_EPS) + ADAM_WD * w)
    return delta, m, v


def _sum_adamw(parts, w, m, v, name):
    R, C = w.shape
    n_parts = parts.shape[0]
    tr = _pick(R, max(16, (2 << 20) // (4 * C) // 16 * 16), 16)

    def body(p_ref, w_ref, m_ref, v_ref, g_ref, d_ref, mo_ref, vo_ref):
        gr = p_ref[0].astype(F32)
        for k in range(1, n_parts):
            gr = gr + p_ref[k].astype(F32)
        g_ref[...] = gr
        d_ref[...], mo_ref[...], vo_ref[...] = _adamw_math(w_ref[...], gr, m_ref[...], v_ref[...])

    row = pl.BlockSpec((tr, C), lambda i: (i, 0))
    return pl.pallas_call(
        body, grid=(R // tr,),
        in_specs=[pl.BlockSpec((n_parts, tr, C), lambda i: (0, i, 0)), row, row, row],
        out_specs=[row] * 4, out_shape=[jax.ShapeDtypeStruct((R, C), F32)] * 4,
        compiler_params=_cparams(("parallel",)), name=name,
    )(parts, w, m, v)


def _sum8(parts, name):
    _, R, C = parts.shape

    def body(p_ref, g_ref):
        gr = p_ref[0]
        for k in range(1, N_DEV):
            gr = gr + p_ref[k]
        g_ref[...] = gr

    return pl.pallas_call(body, out_shape=jax.ShapeDtypeStruct((R, C), F32), name=name)(parts)


def _adamw(w, gr, m, v, name):
    def body(w_ref, g_ref, m_ref, v_ref, d_ref, mo_ref, vo_ref):
        d_ref[...], mo_ref[...], vo_ref[...] = _adamw_math(w_ref[...], g_ref[...], m_ref[...], v_ref[...])

    return pl.pallas_call(body, out_shape=[jax.ShapeDtypeStruct(w.shape, F32)] * 3, name=name)(w, gr, m, v)


def _to_padded(w_in):
    z = jnp.zeros(w_in.shape[:-1] + (32,), w_in.dtype)
    return jnp.concatenate([w_in[..., 0:3072], w_in[..., 5152:8224], w_in[..., 3104:5152], w_in[..., 8288:10336],
                            w_in[..., 3072:3104], w_in[..., 8224:8288], z], axis=-1)


def _from_padded(p):
    return jnp.concatenate([p[..., 0:3072], p[..., 10240:10272], p[..., 6144:8192], p[..., 3072:6144],
                            p[..., 10272:10336], p[..., 8192:10240]], axis=-1)


def _unshard_cols(gathered):
    n, r, c = gathered.shape
    return jnp.transpose(gathered, (1, 0, 2)).reshape(r, n * c)


def _shard_cols(full):
    r, nc = full.shape
    return jnp.transpose(full.reshape(r, N_DEV, nc // N_DEV), (1, 0, 2))


def _lanes(vec, lo):
    return jnp.zeros((1, 128), F32).at[:, lo:lo + vec.shape[1]].set(vec)


_SMALL = (("b_ada", 6 * D), ("c_ctx", D), ("norm1_w", D), ("gla_bias_f", 512), ("gla_bias_b", 512), ("gla_norm_w", 256),
          ("conv_b", 3072), ("dt_bias_f", 32), ("dt_bias_b", 32), ("a_log_f", 32), ("a_log_b", 32), ("d_skip", 32),
          ("ssm_norm_w", 2048), ("norm2_w", D), ("final_norm_w", D))
_SHARDED_SMALL = (("gla_up_f", 16 * 512), ("gla_up_b", 16 * 512), ("conv_w", 4 * 3072))


def _pack(vals, names):
    flat = jnp.concatenate([vals[n].reshape(-1).astype(F32) for n, _ in names])
    pad = (-flat.shape[0]) % 1024
    return jnp.concatenate([flat, jnp.zeros((pad,), F32)]).reshape(-1, 128)


def _unpack(packed, names, shapes):
    flat, out, off = packed.reshape(-1), {}, 0
    for n, size in names:
        out[n] = flat[off:off + size].reshape(shapes[n])
        off += size
    return out


def kernel(x, c, ctx, c_ctx, w_ada, b_ada, norm1_w, w_in, gla_up_f, gla_bias_f, gla_up_b, gla_bias_b, gla_norm_w, conv_w, conv_b, dt_bias_f, dt_bias_b, a_log_f, a_log_b, d_skip, ssm_norm_w, w_pa, w_pb, w_out, norm2_w, w_gate, w_up, w_down, final_norm_w, loss_target, m_c_ctx, m_w_ada, m_b_ada, m_norm1_w, m_w_in, m_gla_up_f, m_gla_bias_f, m_gla_up_b, m_gla_bias_b, m_gla_norm_w, m_conv_w, m_conv_b, m_dt_bias_f, m_dt_bias_b, m_a_log_f, m_a_log_b, m_d_skip, m_ssm_norm_w, m_w_pa, m_w_pb, m_w_out, m_norm2_w, m_w_gate, m_w_up, m_w_down, m_final_norm_w, v_c_ctx, v_w_ada, v_b_ada, v_norm1_w, v_w_in, v_gla_up_f, v_gla_bias_f, v_gla_up_b, v_gla_bias_b, v_gla_norm_w, v_conv_w, v_conv_b, v_dt_bias_f, v_dt_bias_b, v_a_log_f, v_a_log_b, v_d_skip, v_ssm_norm_w, v_w_pa, v_w_pb, v_w_out, v_norm2_w, v_w_gate, v_w_up, v_w_down, v_final_norm_w):
    args = dict(locals())
    me = 4 * lax.axis_index("x") + 2 * lax.axis_index("y") + lax.axis_index("c")

    def gather16(a, name):
        return _all_gather(a[0].astype(BF16), name)

    full = {"w_in": _to_padded(_unshard_cols(gather16(w_in, "ag_w_in")))}
    shards = {n: args[n][0].astype(BF16) for n in ("w_pa", "w_pb", "w_out", "w_gate", "w_up", "w_down")}
    sm = _exchange(jnp.concatenate([gla_up_f.reshape(-1), gla_up_b.reshape(-1), conv_w.reshape(-1), c.reshape(-1)]).reshape(-1, 128),
                   False, "ag_small")
    sm = sm.reshape(N_DEV, -1)
    n_ada = w_ada.shape[2]
    cc = jnp.zeros((_ADA_ROWS, D), F32).at[0:N_DEV].set(sm[:, 3584:3584 + D]).at[N_DEV].set(c_ctx)
    w_ada16 = w_ada[0].astype(BF16)
    ada_cols = _exchange(_ada_fwd(cc, w_ada16, lax.dynamic_slice(b_ada, (0, me * n_ada), (1, n_ada))), False, "ag_ada")
    ada = jnp.zeros((8, N_DEV * n_ada), F32)
    ada = ada.at[0].set(lax.dynamic_index_in_dim(ada_cols, me, axis=1, keepdims=False).reshape(-1))
    ada = ada.at[1].set(ada_cols[:, N_DEV].reshape(-1))
    up_f = _unshard_cols(sm[:, 0:1024].reshape(N_DEV, 16, 64))
    up_b = _unshard_cols(sm[:, 1024:2048].reshape(N_DEV, 16, 64))
    full["conv_w"] = _unshard_cols(sm[:, 2048:3584].reshape(N_DEV, 4, 384))
    full["up_f"] = jnp.zeros((128, 512), F32).at[0:16].set(up_f)
    full["up_b"] = jnp.zeros((128, 512), F32).at[16:32].set(up_b)
    full["dtb"] = _lanes(dt_bias_f, 32) + _lanes(dt_bias_b, 64)
    full["alog"] = _lanes(a_log_f, 32) + _lanes(a_log_b, 64)
    full["dsk8"] = jnp.zeros((8, 128), F32).at[0:1, 0:32].set(d_skip)
    for n in ("norm1_w", "gla_bias_f", "gla_bias_b", "gla_norm_w", "conv_b", "ssm_norm_w", "norm2_w"):
        full[n] = args[n]
    full["final_norm_w"] = final_norm_w.reshape(1, D)

    loss_blk, grad_x, g, got, d_ada = _local_step(x[0], ctx[0], loss_target[0], ada, full, shards)

    d_all = _exchange(jnp.concatenate([d[0:2] for d in d_ada], axis=0), False, "ag_d_ada").reshape(N_DEV * 6, N_DEV * n_ada)
    gw_ada, gb_ada, d_cc = _ada_bwd(cc, w_ada16, d_all, lax.dynamic_slice(d_all, (0, me * n_ada), (N_DEV * 6, n_ada)))

    gs = dict(g, c_ctx=d_cc[N_DEV], b_ada=jnp.zeros_like(b_ada))
    gs["dt_bias_f"], gs["dt_bias_b"] = g["dtb"][:, 32:64], g["dtb"][:, 64:96]
    gs["a_log_f"], gs["a_log_b"] = g["alog"][:, 32:64], g["alog"][:, 64:96]
    gs["d_skip"] = g["dsk8"][0:1, 0:32]
    gs["gla_up_f"], gs["gla_up_b"] = g["up_f"][0:16], g["up_b"][16:32]
    gs["loss"] = loss_blk[0:1, 0:1]
    names = _SMALL + _SHARDED_SMALL + (("loss", 1),)
    shapes = {n: (args[n].shape if n in args else (1, 1)) for n, _ in names}
    shapes.update({"gla_up_f": (16, 512), "gla_up_b": (16, 512), "conv_w": (4, 3072)})
    red = _unpack(_sum8(_exchange(_pack(gs, names), False, "ag_small_grads"), "sum_small_grads"), names, shapes)
    loss = red["loss"].reshape(())
    grads = {n: red[n] for n, _ in _SMALL}
    grads["b_ada"] = gb_ada
    grads["gla_up_f"] = lax.dynamic_slice(red["gla_up_f"], (0, me * 64), (16, 64))[None]
    grads["gla_up_b"] = lax.dynamic_slice(red["gla_up_b"], (0, me * 64), (16, 64))[None]
    grads["conv_w"] = lax.dynamic_slice(red["conv_w"], (0, me * 384), (4, 384))[None]
    upd_names = tuple((n, s) for n, s in _SMALL) + (("gla_up_f", 1024), ("gla_up_b", 1024), ("conv_w", 1536))
    pk = lambda prefix, src: _pack({n: src[prefix + n] for n, _ in upd_names}, upd_names)
    d_s, m_s, v_s = _adamw(pk("", args), _pack(grads, upd_names), pk("m_", args), pk("v_", args), "adamw_small")
    upd_shapes = {n: args[n].shape for n, _ in upd_names}
    delta = _unpack(d_s, upd_names, upd_shapes)
    new_m = _unpack(m_s, upd_names, upd_shapes)
    new_v = _unpack(v_s, upd_names, upd_shapes)

    got["w_ada"] = gw_ada[None]
    for n, parts in got.items():
        gr, dl, mo, vo = _sum_adamw(parts, args[n][0], args["m_" + n][0], args["v_" + n][0], "adamw_" + n)
        grads[n], delta[n], new_m[n], new_v[n] = gr[None], dl[None], mo[None], vo[None]

    order = ["c_ctx", "w_ada", "b_ada", "norm1_w", "w_in", "gla_up_f", "gla_bias_f", "gla_up_b", "gla_bias_b", "gla_norm_w",
             "conv_w", "conv_b", "dt_bias_f", "dt_bias_b", "a_log_f", "a_log_b", "d_skip", "ssm_norm_w", "w_pa", "w_pb",
             "w_out", "norm2_w", "w_gate", "w_up", "w_down", "final_norm_w"]
    fix = lambda d: [d[n].reshape(args[n].shape) for n in order]
    return (loss, grad_x[None], *fix(grads), *fix(delta), *fix(new_m), *fix(new_v))
```

```python
import functools

import jax
import jax.numpy as jnp
import numpy as np
from jax import lax
from jax.experimental import pallas as pl
from jax.experimental.pallas import tpu as pltpu

F32 = jnp.float32
BF16 = jnp.bfloat16
N_DEV = 8
D = 1024
EPS = 1e-6
GRID_W = 64
GLA_H, GLA_DK, GLA_DV = 4, 128, 256
GLA_C = 64
GLA_TAU = 16.0
SSM_G, SSM_HPG, SSM_P, SSM_N = 4, 8, 64, 128
SSM_C = 128
SSM_INNER = 2048
D_FF = 2816
D_IN = 10336
D_INP = 10368
C_Q, C_K, C_V, C_R, C_U, C_Z, C_GA, C_GB, C_S = 0, 512, 1024, 2048, 3072, 6144, 8192, 9216, 10240
TM = 256
TS = 128

ADAM_LR, ADAM_B1, ADAM_B2, ADAM_EPS, ADAM_WD, ADAM_STEP = 0.001, 0.9, 0.999, 1e-08, 0.01, 10
VMEM_LIMIT = 56 << 20


def _cparams(sem):
    return pltpu.CompilerParams(dimension_semantics=sem or None, vmem_limit_bytes=VMEM_LIMIT)


def _pick(n, target, mult):
    best = None
    for t in range(mult, min(n, target) + 1, mult):
        if n % t == 0:
            best = t
    return best if best is not None else n


def _dot_impl(a, b, ta, tb):
    dims = (((0 if ta else 1,), (1 if tb else 0,)), ((), ()))
    return lax.dot_general(a.astype(BF16), b.astype(BF16), dims, preferred_element_type=F32)


@functools.partial(jax.custom_vjp, nondiff_argnums=(2, 3))
def _bdot(a, b, ta=False, tb=False):
    return _dot_impl(a, b, ta, tb)


def _bdot_fwd(a, b, ta, tb):
    return _dot_impl(a, b, ta, tb), (a, b)


def _bdot_bwd(ta, tb, res, g):
    a, b = res
    if not ta and not tb:
        return _dot_impl(g, b, False, True), _dot_impl(a, g, True, False)
    if not ta and tb:
        return _dot_impl(g, b, False, False), _dot_impl(g, a, True, False)
    if ta and not tb:
        return _dot_impl(b, g, False, True), _dot_impl(a, g, False, False)
    raise NotImplementedError


_bdot.defvjp(_bdot_fwd, _bdot_bwd)


_NN = (((1,), (0,)), ((), ()))
_NT = (((1,), (1,)), ((), ()))
_TN = (((0,), (0,)), ((), ()))


def _dg2(x, e, x_is_lhs, dims):
    hi = x.astype(BF16)
    lo = (x - hi.astype(F32)).astype(BF16)
    e = e.astype(BF16)
    if x_is_lhs:
        return (lax.dot_general(hi, e, dims, preferred_element_type=F32)
                + lax.dot_general(lo, e, dims, preferred_element_type=F32))
    return (lax.dot_general(e, hi, dims, preferred_element_type=F32)
            + lax.dot_general(e, lo, dims, preferred_element_type=F32))


_EDOT_FWD = {"xe": (True, _NN), "ex": (False, _NN), "ext": (False, _NT)}
_EDOT_BWD = {"xe": (True, _NT), "ex": (False, _TN), "ext": (True, _TN)}


@functools.partial(jax.custom_vjp, nondiff_argnums=(2,))
def _edot(x, e, mode):
    return _dg2(x, e, *_EDOT_FWD[mode])


def _edot_fwd(x, e, mode):
    return _dg2(x, e, *_EDOT_FWD[mode]), e


def _edot_bwd(mode, e, g):
    return _dg2(g, e, *_EDOT_BWD[mode]), None


_edot.defvjp(_edot_fwd, _edot_bwd)


def _shift_impl(u, pos, per, s):
    n = u.shape[0]
    rolled = u if s == 0 else pltpu.roll(u, (-s) % n, 0)
    ok = (pos + s >= 0) & (pos + s < per)
    return jnp.where(ok, rolled, 0.0)


@functools.partial(jax.custom_vjp, nondiff_argnums=(3,))
def _shift(u, pos, per, s):
    return _shift_impl(u, pos, per, s)


def _shift_fwd(u, pos, per, s):
    return _shift_impl(u, pos, per, s), (pos, per)


def _shift_bwd(s, res, g):
    pos, per = res
    return _shift_impl(g, pos, per, -s), None, None


_shift.defvjp(_shift_fwd, _shift_bwd)


def _rms(x, w):
    return x * lax.rsqrt(jnp.mean(x * x, axis=-1, keepdims=True) + EPS) * w


def _silu(x):
    return x * jax.nn.sigmoid(x)


def _softplus(x):
    return jnp.maximum(x, 0.0) + jnp.log(1.0 + jnp.exp(-jnp.abs(x)))


def _logsig(x):
    return jnp.minimum(x, 0.0) - jnp.log(1.0 + jnp.exp(-jnp.abs(x)))


def _tri(n, rev):
    t = lax.broadcasted_iota(jnp.int32, (n, n), 0)
    s = lax.broadcasted_iota(jnp.int32, (n, n), 1)
    return (s >= t) if rev else (t >= s)


def _head_expand(first_lane):
    l = lax.broadcasted_iota(jnp.int32, (128, SSM_INNER), 0)
    c = lax.broadcasted_iota(jnp.int32, (128, SSM_INNER), 1)
    return (l == first_lane + lax.shift_right_logical(c, 6)).astype(F32)


def _exchange_ops(x_ref, o_ref, send_sems, recv_sems, local_sem, a2a):
    mx, my, mc = lax.axis_index("x"), lax.axis_index("y"), lax.axis_index("c")
    me = 4 * mx + 2 * my + mc
    ops = []
    for k in range(1, N_DEV):
        px = 1 - mx if k & 4 else mx
        py = 1 - my if k & 2 else my
        pc = 1 - mc if k & 1 else mc
        ops.append(pltpu.make_async_remote_copy(
            src_ref=x_ref.at[4 * px + 2 * py + pc] if a2a else x_ref, dst_ref=o_ref.at[me],
            send_sem=send_sems.at[k - 1], recv_sem=recv_sems.at[k - 1],
            device_id=(px, py, pc), device_id_type=pl.DeviceIdType.MESH))
    ops.append(pltpu.make_async_copy(x_ref.at[me] if a2a else x_ref, o_ref.at[me], local_sem))
    return ops


def _call(body, *, grid, in_specs, out_specs, out_shape, scratch=(), sem, name, args, hosted=()):
    n_in, n_out, n_s, n_h = len(in_specs), len(out_specs), len(scratch), len(hosted)
    if not n_h:
        return pl.pallas_call(body, grid=grid, in_specs=list(in_specs), out_specs=list(out_specs),
                              out_shape=list(out_shape), scratch_shapes=list(scratch),
                              compiler_params=_cparams(sem), name=name)(*args)
    b0, c0 = n_in + n_h, n_in + n_h + n_out
    d0 = c0 + n_h

    def wrapped(*refs):
        hx, ho = refs[n_in:b0], refs[c0:d0]
        send_sems, recv_sems, local_sems = refs[d0 + n_s:]
        ids = [pl.program_id(ax) for ax in range(len(grid))]
        first = functools.reduce(jnp.logical_and, [i == 0 for i in ids])
        last = functools.reduce(jnp.logical_and, [i == n - 1 for i, n in zip(ids, grid)])

        def ops():
            return [op for n, (_, a2a) in enumerate(hosted)
                    for op in _exchange_ops(hx[n], ho[n], send_sems.at[n], recv_sems.at[n], local_sems.at[n], a2a)]

        @pl.when(first)
        def _():
            for op in ops():
                op.start()

        body(*refs[:n_in], *refs[b0:c0], *refs[d0:d0 + n_s])

        @pl.when(last)
        def _():
            for op in ops():
                op.wait()

    hbm = pl.BlockSpec(memory_space=pl.ANY)
    h_shapes = [jax.ShapeDtypeStruct((N_DEV,) + tuple(x.shape[1:] if a2a else x.shape), x.dtype) for x, a2a in hosted]
    return pl.pallas_call(
        wrapped, grid=grid, in_specs=list(in_specs) + [hbm] * n_h, out_specs=list(out_specs) + [hbm] * n_h,
        out_shape=list(out_shape) + h_shapes,
        scratch_shapes=list(scratch) + [pltpu.SemaphoreType.DMA((n_h, N_DEV - 1)), pltpu.SemaphoreType.DMA((n_h, N_DEV - 1)),
                                        pltpu.SemaphoreType.DMA((n_h,))],
        compiler_params=_cparams(("arbitrary",) * len(grid)), name=name,
    )(*args, *[x for x, _ in hosted])


def _mm(a, b, *, trans_b, out_dtype, name, tm_t=1408, tn_t=1024, tk_t=2816, hosted=()):
    M, K = a.shape
    N = b.shape[0] if trans_b else b.shape[1]
    tm, tn, tk = _pick(M, tm_t, 8), _pick(N, tn_t, 128), _pick(K, tk_t, 128)
    nk = K // tk
    dims = (((1,), (1,)), ((), ())) if trans_b else (((1,), (0,)), ((), ()))

    def body_one(a_ref, b_ref, o_ref):
        o_ref[...] = lax.dot_general(a_ref[...], b_ref[...], dims, preferred_element_type=F32).astype(out_dtype)

    def body_acc(a_ref, b_ref, o_ref, acc_ref):
        k = pl.program_id(2)
        p = lax.dot_general(a_ref[...], b_ref[...], dims, preferred_element_type=F32)

        @pl.when(k == 0)
        def _():
            acc_ref[...] = p

        @pl.when(k > 0)
        def _():
            acc_ref[...] += p

        @pl.when(k == nk - 1)
        def _():
            o_ref[...] = acc_ref[...].astype(out_dtype)

    b_spec = pl.BlockSpec((tn, tk), lambda j, i, k: (j, k)) if trans_b else pl.BlockSpec((tk, tn), lambda j, i, k: (k, j))
    outs = _call(
        body_one if nk == 1 else body_acc, grid=(N // tn, M // tm, nk),
        in_specs=[pl.BlockSpec((tm, tk), lambda j, i, k: (i, k)), b_spec],
        out_specs=[pl.BlockSpec((tm, tn), lambda j, i, k: (i, j))],
        out_shape=[jax.ShapeDtypeStruct((M, N), out_dtype)],
        scratch=[] if nk == 1 else [pltpu.VMEM((tm, tn), F32)],
        sem=("parallel", "parallel", "arbitrary"), name=name, args=(a, b), hosted=hosted)
    return outs if hosted else outs[0]


def _mm_tn(a, b, *, name, tm_t=2816, tr_t=1024, tn_t=1408):
    M, R = a.shape
    N = b.shape[1]
    tm, tr, tn = _pick(M, tm_t, 8), _pick(R, tr_t, 128), _pick(N, tn_t, 128)

    def body(a_ref, b_ref, o_ref):
        m = pl.program_id(2)
        p = lax.dot_general(a_ref[...], b_ref[...], (((0,), (0,)), ((), ())), preferred_element_type=F32)

        @pl.when(m == 0)
        def _():
            o_ref[...] = p

        @pl.when(m > 0)
        def _():
            o_ref[...] += p

    return pl.pallas_call(
        body, grid=(R // tr, N // tn, M // tm),
        in_specs=[pl.BlockSpec((tm, tr), lambda r, j, m: (m, r)), pl.BlockSpec((tm, tn), lambda r, j, m: (m, j))],
        out_specs=pl.BlockSpec((tr, tn), lambda r, j, m: (r, j)),
        out_shape=jax.ShapeDtypeStruct((R, N), F32),
        compiler_params=_cparams(("parallel", "parallel", "arbitrary")), name=name,
    )(a, b)


def _col(arr, tm, width=None, col=0):
    width = arr.shape[1] if width is None else width
    return (arr, (tm, width), lambda i: (i, col))


def _lat(arr, tm, nct):
    return (arr, (tm, arr.shape[1]), lambda i: (jnp.maximum(i - nct, 0), 0))


def _ctx(arr, tm, nct):
    return (arr, (tm, arr.shape[1]), lambda i: (jnp.minimum(i, nct - 1), 0))


def _whole(p):
    return pl.BlockSpec(p.shape, lambda i, nd=p.ndim: (0,) * nd)


def _stage_fwd(f, n_tiles, ins, params, outs, name):
    ni, npar = len(ins), len(params)
    o_specs = [pl.BlockSpec((o[1], o[3]), lambda i: (0, i)) if len(o) > 4 else pl.BlockSpec((o[3], o[1]), lambda i: (i, 0))
               for o in outs]
    o_shapes = [jax.ShapeDtypeStruct((o[1], o[0]) if len(o) > 4 else (o[0], o[1]), o[2]) for o in outs]

    def body(*refs):
        i = pl.program_id(0)
        xs = [r[...].astype(F32) for r in refs[:ni]]
        ps = [r[...] for r in refs[ni:ni + npar]]
        for r, v in zip(refs[ni + npar:], f(i, xs, ps, False)):
            r[...] = v.astype(r.dtype)

    return pl.pallas_call(
        body, grid=(n_tiles,),
        in_specs=[pl.BlockSpec(bs, fn) for _, bs, fn in ins] + [_whole(p) for p in params],
        out_specs=o_specs, out_shape=o_shapes,
        compiler_params=_cparams(("parallel",)), name=name,
    )(*[a for a, _, _ in ins], *params)


def _stage_bwd(f, n_tiles, ins, params, cts, ct_fn, dins, name, hosted=()):
    ni, npar, nc = len(ins), len(params), len(cts)
    want = [k for k, d in enumerate(dins) if d is not None]
    extras = [dins[k][1] for k in want if dins[k][1] is not None]

    def body(*refs):
        i = pl.program_id(0)
        xs = [r[...].astype(F32) for r in refs[:ni]]
        ps = [r[...] for r in refs[ni:ni + npar]]
        ct_tiles = [r[...].astype(F32) for r in refs[ni + npar:ni + npar + nc]]
        ex_refs = list(refs[ni + npar + nc:ni + npar + nc + len(extras)])
        out_refs = refs[ni + npar + nc + len(extras):]
        _, vjp = jax.vjp(lambda xs_, ps_: tuple(f(i, xs_, ps_, True)), xs, ps)
        dxs, dps = vjp(tuple(ct_fn(i, ct_tiles)))
        for n, k in enumerate(want):
            v = dxs[k]
            if dins[k][1] is not None:
                v = v + ex_refs.pop(0)[...].astype(F32)
            out_refs[n][...] = v.astype(out_refs[n].dtype)
        for r, v in zip(out_refs[len(want):], dps):
            @pl.when(i == 0)
            def _(r=r, v=v):
                r[...] = v

            @pl.when(i > 0)
            def _(r=r, v=v):
                r[...] += v

    return _call(
        body, grid=(n_tiles,),
        in_specs=([pl.BlockSpec(bs, fn) for _, bs, fn in ins] + [_whole(p) for p in params]
                  + [pl.BlockSpec(bs, fn) for _, bs, fn in cts] + [pl.BlockSpec(bs, fn) for _, bs, fn in extras]),
        out_specs=([pl.BlockSpec(ins[k][1], lambda i, fn=ins[k][2]: (fn(i)[0], 0)) for k in want]
                   + [_whole(p) for p in params]),
        out_shape=([jax.ShapeDtypeStruct((ins[k][0].shape[0], ins[k][1][1]), dins[k][0]) for k in want]
                   + [jax.ShapeDtypeStruct(p.shape, F32) for p in params]),
        sem=("arbitrary",), name=name,
        args=(*[a for a, _, _ in ins], *params, *[a for a, _, _ in cts], *[a for a, _, _ in extras]), hosted=hosted)


def _mod_rows(i, nct, ada, lo):
    return jnp.where(i < nct, ada[1:2, lo:lo + D], ada[0:1, lo:lo + D])


def _f_norm1(nct):
    def f(i, xs, ps, diff):
        ctx, x = xs
        nw, ada = ps
        xt = jnp.where(i < nct, ctx, x)
        return (_rms(xt, nw) * (1.0 + _mod_rows(i, nct, ada, D)) + _mod_rows(i, nct, ada, 0),)
    return f


def _f_gates_dt(i, xs, ps, diff):
    small, xc = xs
    up_f, up_b, gb_f, gb_b, dtb, alog = ps
    dot = _bdot if diff else _dot_impl_nn
    lg_f = _logsig(dot(small, up_f) + gb_f) * (1.0 / GLA_TAU)
    lg_b = _logsig(dot(small, up_b) + gb_b) * (1.0 / GLA_TAU)
    dtp = _softplus(small + dtb)
    aa = -jnp.exp(alog) * dtp
    e_f, e_b = _head_expand(32), _head_expand(64)
    n = small.shape[0]
    t = lax.broadcasted_iota(jnp.int32, (n, n), 0)
    s = lax.broadcasted_iota(jnp.int32, (n, n), 1)
    same = lax.shift_right_logical(t, 6) == lax.shift_right_logical(s, 6)
    pre_g, suf_g = ((t >= s) & same).astype(F32), ((s >= t) & same).astype(F32)
    pre_s, suf_s = (t >= s).astype(F32), (s >= t).astype(F32)
    cum_f, cum_b = _edot(aa, pre_s, "ex"), _edot(aa, suf_s, "ex")
    h = lax.broadcasted_iota(jnp.int32, (SSM_G * SSM_HPG, 128), 0)
    l = lax.broadcasted_iota(jnp.int32, (SSM_G * SSM_HPG, 128), 1)
    cum_ft = _edot(cum_f, (l == h + 32).astype(F32), "ext")
    cum_bt = _edot(cum_b, (l == h + 64).astype(F32), "ext")
    return (_edot(lg_f, pre_g, "ex"), _edot(lg_b, suf_g, "ex"), xc * _edot(dtp, e_f, "xe"), xc * _edot(dtp, e_b, "xe"),
            _edot(cum_f, e_f, "xe"), _edot(cum_b, e_b, "xe"), cum_ft, cum_bt)


def _dot_impl_nn(a, b):
    return _dot_impl(a, b, False, False)


def _f_conv(nct, tc):
    def f(i, xs, ps, diff):
        (u,) = xs
        cw, cb = ps
        n = u.shape[0]
        t = lax.broadcasted_iota(jnp.int32, (n, 1), 0)
        per = jnp.where(i < nct, tc, GRID_W)
        pos = jnp.bitwise_and(t, per - 1)
        sh = _shift if diff else _shift_impl
        acc = cb + sh(u, pos, per, -2) * cw[0:1]
        for j in range(1, 4):
            acc = acc + sh(u, pos, per, j - 2) * cw[j:j + 1]
        return (_silu(acc),)
    return f


def _f_gla_post(i, xs, ps, diff):
    ogf, ogb, r = xs
    (gw,) = ps
    o = ogf + ogb
    parts = [_rms(o[:, h * GLA_DV:(h + 1) * GLA_DV], gw) for h in range(GLA_H)]
    return (jnp.concatenate(parts, axis=1) * _silu(r),)


def _f_ssd_post(i, xs, ps, diff):
    yf, yb, xc, z = xs
    dsk8, nw = ps
    dsk = _edot(dsk8, _head_expand(0), "xe")[0:1]
    y = (yf + yb + dsk * xc) * _silu(z)
    w = SSM_INNER // SSM_G
    parts = [_rms(y[:, g * w:(g + 1) * w], nw[:, g * w:(g + 1) * w]) for g in range(SSM_G)]
    return (jnp.concatenate(parts, axis=1),)


def _f_merge(i, xs, ps, diff):
    ga, gb, ya, yb = xs
    return (jax.nn.sigmoid(ga) * ya + jax.nn.sigmoid(gb) * yb,)


def _f_res1(nct):
    def f(i, xs, ps, diff):
        ctx, x, mix = xs
        ada, nw = ps
        h2 = jnp.where(i < nct, ctx, x) + _mod_rows(i, nct, ada, 2 * D) * mix
        return (h2, _rms(h2, nw) * (1.0 + _mod_rows(i, nct, ada, 4 * D)) + _mod_rows(i, nct, ada, 3 * D))
    return f


def _f_swiglu(i, xs, ps, diff):
    (gu,) = xs
    return (_silu(gu[:, :D_FF]) * gu[:, D_FF:],)


def _loss_head(h2, dn, tgt, ada, fw, nct):
    A = h2.shape[0]
    n_tiles = A // TM

    def tile_loss(i, h2t, dnt, tg, ada_, fw_):
        h3 = h2t + ada_[0:1, 5 * D:6 * D] * dnt
        err = _rms(h3, fw_) - tg
        row = 0.5 * jnp.mean(err * err, axis=-1, keepdims=True)
        return jnp.sum(row, axis=0, keepdims=True) * jnp.where(i < nct, 0.0, 1.0)

    def body(h2_ref, dn_ref, tg_ref, ada_ref, fw_ref, loss_ref, dh_ref, ddn_ref, dada_ref, dfw_ref):
        i = pl.program_id(0)
        val, vjp = jax.vjp(functools.partial(tile_loss, i), h2_ref[...], dn_ref[...].astype(F32), tg_ref[...], ada_ref[...],
                           fw_ref[...])
        dh, ddn, _, dada, dfw = vjp(jnp.ones((1, 1), F32))
        dh_ref[...] = dh
        ddn_ref[...] = ddn.astype(BF16)
        lv = jnp.broadcast_to(val, loss_ref.shape)
        for r, v in ((loss_ref, lv), (dada_ref, dada), (dfw_ref, dfw)):
            @pl.when(i == 0)
            def _(r=r, v=v):
                r[...] = v

            @pl.when(i > 0)
            def _(r=r, v=v):
                r[...] += v

    row = lambda i: (i, 0)
    return pl.pallas_call(
        body, grid=(n_tiles,),
        in_specs=[pl.BlockSpec((TM, D), row), pl.BlockSpec((TM, D), row),
                  pl.BlockSpec((TM, D), lambda i: (jnp.maximum(i - nct, 0), 0)), _whole(ada), _whole(fw)],
        out_specs=[pl.BlockSpec((8, 128), lambda i: (0, 0)), pl.BlockSpec((TM, D), row), pl.BlockSpec((TM, D), row),
                   _whole(ada), _whole(fw)],
        out_shape=[jax.ShapeDtypeStruct((8, 128), F32), jax.ShapeDtypeStruct((A, D), F32),
                   jax.ShapeDtypeStruct((A, D), BF16), jax.ShapeDtypeStruct(ada.shape, F32),
                   jax.ShapeDtypeStruct(fw.shape, F32)],
        compiler_params=_cparams(("arbitrary",)), name="loss_head",
    )(h2, dn, tgt, ada, fw)


def _chunk_of(step, n_chunks, n_ctx_chunks, rev):
    if not rev:
        return step
    return jnp.where(step < n_ctx_chunks, n_ctx_chunks - 1 - step, n_chunks - 1 - (step - n_ctx_chunks))


def _gla_step(st, q, k, v, b, rev):
    tri = _tri(GLA_C, rev).astype(F32)
    last = 0 if rev else GLA_C - 1
    outs, sts = [], []
    for h in range(GLA_H):
        kk = slice(h * GLA_DK, (h + 1) * GLA_DK)
        vv = slice(h * GLA_DV, (h + 1) * GLA_DV)
        qh, kh, vh, bh, sh = q[:, kk] * (GLA_DK ** -0.5), k[:, kk], v[:, vv], b[:, kk], st[vv]
        tot = bh[last:last + 1]
        mid = bh[GLA_C // 2:GLA_C // 2 + 1]
        att = _bdot(qh * jnp.exp(bh - mid), kh * jnp.exp(mid - bh), False, True) * tri
        outs.append(_bdot(att, vh) + _bdot(qh * jnp.exp(bh), sh, False, True))
        sts.append(sh * jnp.exp(tot) + _bdot(vh, kh * jnp.exp(tot - bh), True, False))
    return jnp.concatenate(sts, axis=0), jnp.concatenate(outs, axis=1)


_GQK, _GV = GLA_H * GLA_DK, GLA_H * GLA_DV


def _gla_fwd(proj, bg, ncc, rev, name):
    A = proj.shape[0]
    nc = A // GLA_C
    ch = lambda s: _chunk_of(s, nc, ncc, rev)

    def body(q_ref, k_ref, v_ref, b_ref, o_ref, ss_ref, st_ref):
        @pl.when(pl.program_id(0) == 0)
        def _():
            st_ref[...] = jnp.zeros_like(st_ref)

        st = st_ref[...]
        ss_ref[0] = st
        st_new, o = _gla_step(st, q_ref[...].astype(F32), k_ref[...].astype(F32), v_ref[...].astype(F32), b_ref[...], rev)
        o_ref[...] = o.astype(o_ref.dtype)
        st_ref[...] = st_new

    return pl.pallas_call(
        body, grid=(nc,),
        in_specs=[pl.BlockSpec((GLA_C, _GQK), lambda s: (ch(s), C_Q // _GQK)),
                  pl.BlockSpec((GLA_C, _GQK), lambda s: (ch(s), C_K // _GQK)),
                  pl.BlockSpec((GLA_C, _GV), lambda s: (ch(s), C_V // _GV)),
                  pl.BlockSpec((GLA_C, _GQK), lambda s: (ch(s), 0))],
        out_specs=[pl.BlockSpec((GLA_C, _GV), lambda s: (ch(s), 0)),
                   pl.BlockSpec((1, _GV, GLA_DK), lambda s: (s, 0, 0))],
        out_shape=[jax.ShapeDtypeStruct((A, _GV), BF16), jax.ShapeDtypeStruct((nc, _GV, GLA_DK), F32)],
        scratch_shapes=[pltpu.VMEM((_GV, GLA_DK), F32)],
        compiler_params=_cparams(("arbitrary",)), name=name,
    )(proj, proj, proj, bg)


def _gla_bwd(proj, lg, saved, d_o, prev, ncc, rev, name):
    A = proj.shape[0]
    nc = A // GLA_C
    st_of = lambda r: nc - 1 - r
    ch = lambda r: _chunk_of(st_of(r), nc, ncc, rev)
    qs = pl.BlockSpec((GLA_C, _GQK), lambda r: (ch(r), 0))
    vs = pl.BlockSpec((GLA_C, _GV), lambda r: (ch(r), 0))
    n_prev = 0 if prev is None else 3

    def body(*refs):
        q_ref, k_ref, v_ref, b_ref, ss_ref, do_ref = refs[:6]
        p_refs = refs[6:6 + n_prev]
        dq_ref, dk_ref, dv_ref, db_ref, ds_ref = refs[6 + n_prev:]

        @pl.when(pl.program_id(0) == 0)
        def _():
            ds_ref[...] = jnp.zeros_like(ds_ref)

        _, vjp = jax.vjp(functools.partial(_gla_step, rev=rev), ss_ref[0], q_ref[...].astype(F32), k_ref[...].astype(F32),
                         v_ref[...].astype(F32), b_ref[...])
        ds, dq, dk, dv, db = vjp((ds_ref[...], do_ref[...].astype(F32)))
        if n_prev:
            dq, dk, dv = [d + p[...].astype(F32) for d, p in zip((dq, dk, dv), p_refs)]
        for r, d in zip((dq_ref, dk_ref, dv_ref, db_ref), (dq, dk, dv, db)):
            r[...] = d.astype(r.dtype)
        ds_ref[...] = ds

    in_specs = [
        pl.BlockSpec((GLA_C, _GQK), lambda r: (ch(r), C_Q // _GQK)),
        pl.BlockSpec((GLA_C, _GQK), lambda r: (ch(r), C_K // _GQK)),
        pl.BlockSpec((GLA_C, _GV), lambda r: (ch(r), C_V // _GV)),
        qs,
        pl.BlockSpec((1, _GV, GLA_DK), lambda r: (st_of(r), 0, 0)),
        vs,
    ] + ([qs, qs, vs] if n_prev else [])
    return pl.pallas_call(
        body, grid=(nc,), in_specs=in_specs, out_specs=[qs, qs, vs, qs],
        out_shape=[jax.ShapeDtypeStruct((A, _GQK), BF16), jax.ShapeDtypeStruct((A, _GQK), BF16),
                   jax.ShapeDtypeStruct((A, _GV), BF16), jax.ShapeDtypeStruct((A, _GQK), F32)],
        scratch_shapes=[pltpu.VMEM((_GV, GLA_DK), F32)],
        compiler_params=_cparams(("arbitrary",)), name=name,
    )(proj, proj, proj, lg, saved, d_o, *(prev or ()))


def _ssd_step(st, x, bm, cm, cum, cum_t, rev):
    mask = _tri(SSM_C, rev)
    last = 0 if rev else SSM_C - 1
    tot = cum[last:last + 1]
    cb = _bdot(cm, bm, False, True)
    ys = []
    for e in range(SSM_HPG):
        lm = jnp.exp(jnp.where(mask, cum[:, e * SSM_P:e * SSM_P + 1] - cum_t[e:e + 1], -jnp.inf))
        ys.append(_bdot(cb * lm, x[:, e * SSM_P:(e + 1) * SSM_P]))
    y = jnp.concatenate(ys, axis=1) + _bdot(cm, st) * jnp.exp(cum)
    st_new = st * jnp.exp(tot) + _bdot(bm, x * jnp.exp(tot - cum), True, False)
    return st_new, y


SSM_GPS = 4
_W = SSM_HPG * SSM_P
_XBC_B = SSM_INNER // (SSM_N * SSM_GPS)
_XBC_C = _XBC_B + SSM_G // SSM_GPS


def _ssd_steps(st, x, bm, cm, cum, cum_t, rev):
    outs = [_ssd_step(st[j * SSM_N:(j + 1) * SSM_N], x[:, j * _W:(j + 1) * _W], bm[:, j * SSM_N:(j + 1) * SSM_N],
                      cm[:, j * SSM_N:(j + 1) * SSM_N], cum[:, j * _W:(j + 1) * _W],
                      cum_t[j * SSM_HPG:(j + 1) * SSM_HPG], rev) for j in range(SSM_GPS)]
    return jnp.concatenate([o[0] for o in outs], axis=0), jnp.concatenate([o[1] for o in outs], axis=1)


def _ssd_fwd(xd, xbc, cum, cum_t, ncc, rev, name, hosted=()):
    A = xd.shape[0]
    nc = A // SSM_C
    ng = SSM_G // SSM_GPS
    ch = lambda s: _chunk_of(s, nc, ncc, rev)

    def body(x_ref, b_ref, c_ref, a_ref, at_ref, y_ref, ss_ref, st_ref):
        @pl.when(pl.program_id(1) == 0)
        def _():
            st_ref[...] = jnp.zeros_like(st_ref)

        st = st_ref[...]
        ss_ref[0, 0] = st
        st_new, y = _ssd_steps(st, x_ref[...].astype(F32), b_ref[...].astype(F32), c_ref[...].astype(F32), a_ref[...],
                               at_ref[...], rev)
        y_ref[...] = y.astype(y_ref.dtype)
        st_ref[...] = st_new

    gs = pl.BlockSpec((SSM_C, _W * SSM_GPS), lambda g, s: (ch(s), g))
    return _call(
        body, grid=(ng, nc),
        in_specs=[gs, pl.BlockSpec((SSM_C, SSM_N * SSM_GPS), lambda g, s: (ch(s), _XBC_B + g)),
                  pl.BlockSpec((SSM_C, SSM_N * SSM_GPS), lambda g, s: (ch(s), _XBC_C + g)), gs,
                  pl.BlockSpec((SSM_HPG * SSM_GPS, SSM_C), lambda g, s: (g, ch(s)))],
        out_specs=[gs, pl.BlockSpec((1, 1, SSM_N * SSM_GPS, _W), lambda g, s: (g, s, 0, 0))],
        out_shape=[jax.ShapeDtypeStruct((A, SSM_INNER), BF16), jax.ShapeDtypeStruct((ng, nc, SSM_N * SSM_GPS, _W), F32)],
        scratch=[pltpu.VMEM((SSM_N * SSM_GPS, _W), F32)],
        sem=("parallel", "arbitrary"), name=name, args=(xd, xbc, xbc, cum, cum_t), hosted=hosted)


def _ssd_bwd(xd, xbc, cum, cum_t, saved, d_y, prev, ncc, rev, name):
    A = xd.shape[0]
    nc = A // SSM_C
    ng = SSM_G // SSM_GPS
    st_of = lambda r: nc - 1 - r
    ch = lambda r: _chunk_of(st_of(r), nc, ncc, rev)
    gs = pl.BlockSpec((SSM_C, _W * SSM_GPS), lambda g, r: (ch(r), g))
    ns = pl.BlockSpec((SSM_C, SSM_N * SSM_GPS), lambda g, r: (ch(r), g))
    ts = pl.BlockSpec((SSM_HPG * SSM_GPS, SSM_C), lambda g, r: (g, ch(r)))
    n_prev = 0 if prev is None else 2

    def body(*refs):
        x_ref, b_ref, c_ref, a_ref, at_ref, ss_ref, dy_ref = refs[:7]
        p_refs = refs[7:7 + n_prev]
        dx_ref, db_ref, dc_ref, da_ref, dat_ref, ds_ref = refs[7 + n_prev:]

        @pl.when(pl.program_id(1) == 0)
        def _():
            ds_ref[...] = jnp.zeros_like(ds_ref)

        _, vjp = jax.vjp(functools.partial(_ssd_steps, rev=rev), ss_ref[0, 0], x_ref[...].astype(F32), b_ref[...].astype(F32),
                         c_ref[...].astype(F32), a_ref[...], at_ref[...])
        ds, dx, db, dc, da, dat = vjp((ds_ref[...], dy_ref[...].astype(F32)))
        if n_prev:
            db, dc = db + p_refs[0][...].astype(F32), dc + p_refs[1][...].astype(F32)
        for r, d in zip((dx_ref, db_ref, dc_ref, da_ref, dat_ref), (dx, db, dc, da, dat)):
            r[...] = d.astype(r.dtype)
        ds_ref[...] = ds

    in_specs = [gs, pl.BlockSpec((SSM_C, SSM_N * SSM_GPS), lambda g, r: (ch(r), _XBC_B + g)),
                pl.BlockSpec((SSM_C, SSM_N * SSM_GPS), lambda g, r: (ch(r), _XBC_C + g)), gs, ts,
                pl.BlockSpec((1, 1, SSM_N * SSM_GPS, _W), lambda g, r: (g, st_of(r), 0, 0)), gs] + ([ns, ns] if n_prev else [])
    return pl.pallas_call(
        body, grid=(ng, nc), in_specs=in_specs, out_specs=[gs, ns, ns, gs, ts],
        out_shape=[jax.ShapeDtypeStruct((A, SSM_INNER), BF16), jax.ShapeDtypeStruct((A, SSM_G * SSM_N), BF16),
                   jax.ShapeDtypeStruct((A, SSM_G * SSM_N), BF16), jax.ShapeDtypeStruct((A, SSM_INNER), F32),
                   jax.ShapeDtypeStruct((SSM_G * SSM_HPG, A), F32)],
        scratch_shapes=[pltpu.VMEM((SSM_N * SSM_GPS, _W), F32)],
        compiler_params=_cparams(("parallel", "arbitrary")), name=name,
    )(xd, xbc, xbc, cum, cum_t, saved, d_y, *(prev or ()))


_ADA_ROWS = 16


def _ada_fwd(cc, w_shard, b_shard):
    n = w_shard.shape[1]

    def body(cc_ref, w_ref, b_ref, o_ref):
        o_ref[...] = _dot_impl(_silu(cc_ref[...]), w_ref[...], False, False) + b_ref[...]

    return pl.pallas_call(body, out_shape=jax.ShapeDtypeStruct((_ADA_ROWS, n), F32),
                          compiler_params=_cparams(()), name="ada_fwd")(cc, w_shard, b_shard)


def _ada_bwd(cc, w_shard, d_all, d_mine):
    n = w_shard.shape[1]

    def rows(ref, r):
        parts = []
        for s in range(N_DEV):
            parts.append(ref[6 * s + r:6 * s + r + 1] + ref[6 * s + 2 + r:6 * s + 3 + r] + ref[6 * s + 4 + r:6 * s + 5 + r])
        return parts

    def total(parts):
        t = parts[0]
        for p in parts[1:]:
            t = t + p
        return t

    def body(cc_ref, w_ref, da_ref, dm_ref, dw_ref, db_ref, dcc_ref):
        dd = jnp.concatenate(rows(dm_ref, 0) + [total(rows(dm_ref, 1)), jnp.zeros((_ADA_ROWS - N_DEV - 1, n), F32)], axis=0)
        cc = cc_ref[...]
        dw_ref[...] = _dot_impl(_silu(cc), dd, True, False)
        db_ref[...] = total(rows(da_ref, 0)) + total(rows(da_ref, 1))
        _, vjp = jax.vjp(_silu, cc)
        dcc_ref[...] = vjp(_dot_impl(dd, w_ref[...], False, True))[0]

    return pl.pallas_call(
        body, out_shape=[jax.ShapeDtypeStruct((D, n), F32), jax.ShapeDtypeStruct((1, d_all.shape[1]), F32),
                         jax.ShapeDtypeStruct((_ADA_ROWS, D), F32)],
        compiler_params=_cparams(()), name="ada_bwd")(cc, w_shard, d_all, d_mine)


def _local_step(x, ctx, target, ada, w, shards):
    T, Tc = x.shape[0], ctx.shape[0]
    assert Tc == TM and T % TM == 0 and GRID_W == GLA_C and TS == SSM_C == 2 * GLA_C
    A = T + Tc
    nct = Tc // TM
    n_tm, n_ts = A // TM, A // TS
    g = {}

    x_in = [_ctx(ctx, TM, nct), _lat(x, TM, nct)]
    f_norm1 = _f_norm1(nct)
    p_norm1 = [w["norm1_w"], ada]
    (h1,) = _stage_fwd(f_norm1, n_tm, x_in, p_norm1, [(A, D, BF16, TM)], "norm1")
    proj = _mm(h1, w["w_in"], trans_b=False, out_dtype=BF16, name="mm_in", tn_t=1152)
    small = _mm(h1, w["w_in"][:, C_S:], trans_b=False, out_dtype=F32, name="mm_in_small")

    p_gd = [w["up_f"], w["up_b"], w["gla_bias_f"], w["gla_bias_b"], w["dtb"], w["alog"]]
    f_conv = _f_conv(nct, Tc)
    p_conv = [w["conv_w"], w["conv_b"]]
    in_conv = [_col(proj, TM, 3072, C_U // 3072)]
    (xbc,) = _stage_fwd(f_conv, n_tm, in_conv, p_conv, [(A, 3072, BF16, TM)], "conv")
    in_gd = [_col(small, TS), _col(xbc, TS, SSM_INNER, 0)]
    n_heads = SSM_G * SSM_HPG
    lg_f, lg_b, xf, xb, axf, axb, atf, atb = _stage_fwd(
        _f_gates_dt, n_ts, in_gd, p_gd,
        [(A, 512, F32, TS)] * 2 + [(A, SSM_INNER, BF16, TS)] * 2 + [(A, SSM_INNER, F32, TS)] * 2
        + [(A, n_heads, F32, TS, True)] * 2,
        "gates_dt")

    ncc_g, ncc_s = Tc // GLA_C, Tc // SSM_C
    ogf, sv_gf = _gla_fwd(proj, lg_f, ncc_g, False, "gla_f")
    ogb, sv_gb = _gla_fwd(proj, lg_b, ncc_g, True, "gla_b")
    ysf, sv_sf, ag_pa, ag_pb, ag_out = _ssd_fwd(
        xf, xbc, axf, atf, ncc_s, False, "ssd_f", hosted=[(shards[n], False) for n in ("w_pa", "w_pb", "w_out")])
    ysb, sv_sb, ag_gate, ag_up, ag_down = _ssd_fwd(
        xb, xbc, axb, atb, ncc_s, True, "ssd_b", hosted=[(shards[n], False) for n in ("w_gate", "w_up", "w_down")])
    w = dict(w, w_pa=ag_pa.reshape(D, D), w_pb=ag_pb.reshape(SSM_INNER, D), w_out=ag_out.reshape(D, D),
             w_gu=jnp.concatenate([_unshard_cols(ag_gate), _unshard_cols(ag_up)], axis=1), w_down=ag_down.reshape(D_FF, D))
    got = {}

    in_gp = [_col(ogf, TS), _col(ogb, TS), _col(proj, TS, 1024, C_R // 1024)]
    p_gp = [w["gla_norm_w"]]
    (oa,) = _stage_fwd(_f_gla_post, n_ts, in_gp, p_gp, [(A, D, BF16, TS)], "gla_post")
    in_sp = [_col(ysf, TS), _col(ysb, TS), _col(xbc, TS, SSM_INNER, 0), _col(proj, TS, SSM_INNER, C_Z // SSM_INNER)]
    p_sp = [w["dsk8"], w["ssm_norm_w"]]
    (ob,) = _stage_fwd(_f_ssd_post, n_ts, in_sp, p_sp, [(A, SSM_INNER, BF16, TS)], "ssd_post")
    ya = _mm(oa, w["w_pa"], trans_b=False, out_dtype=BF16, name="mm_pa")
    yb = _mm(ob, w["w_pb"], trans_b=False, out_dtype=BF16, name="mm_pb")
    in_mg = [_col(proj, TS, 1024, C_GA // 1024), _col(proj, TS, 1024, C_GB // 1024), _col(ya, TS), _col(yb, TS)]
    (merged,) = _stage_fwd(_f_merge, n_ts, in_mg, [], [(A, D, BF16, TS)], "merge")
    mix = _mm(merged, w["w_out"], trans_b=False, out_dtype=BF16, name="mm_out")

    f_res1 = _f_res1(nct)
    in_r1 = x_in + [_col(mix, TM)]
    p_r1 = [ada, w["norm2_w"]]
    h2, hm2 = _stage_fwd(f_res1, n_tm, in_r1, p_r1, [(A, D, F32, TM), (A, D, BF16, TM)], "res1")
    gu = _mm(hm2, w["w_gu"], trans_b=False, out_dtype=BF16, name="mm_gu", tn_t=1408)
    in_sw = [_col(gu, TS)]
    (act,) = _stage_fwd(_f_swiglu, n_ts, in_sw, [], [(A, D_FF, BF16, TS)], "swiglu")
    dn = _mm(act, w["w_down"], trans_b=False, out_dtype=BF16, name="mm_down")

    loss_blk, d_h2a, d_dn, d_ada3, g["final_norm_w"] = _loss_head(h2, dn, target, ada, w["final_norm_w"], nct)
    rows16 = lambda gw: (gw.reshape(N_DEV, gw.shape[0] // N_DEV, gw.shape[1]).astype(BF16), True)
    cols16 = lambda gw: (_shard_cols(gw).astype(BF16), True)
    gw_down = _mm_tn(act, d_dn, name="dw_down", tr_t=1408)
    d_act = _mm(d_dn, w["w_down"], trans_b=True, out_dtype=BF16, name="dx_down", tn_t=1408)
    d_gu, got["w_down"] = _stage_bwd(_f_swiglu, n_ts, in_sw, [], [_col(d_act, TS)], lambda i, t: t, [(BF16, None)],
                                     "swiglu_b", hosted=[rows16(gw_down)])
    gw_gu = _mm_tn(hm2, d_gu, name="dw_gu", tm_t=2816)
    d_hm2 = _mm(d_gu, w["w_gu"], trans_b=True, out_dtype=BF16, name="dx_gu")
    d_x1, d_mix, d_ada2, g["norm2_w"], got["w_gate"] = _stage_bwd(
        f_res1, n_tm, in_r1, p_r1, [_col(d_h2a, TM), _col(d_hm2, TM)], lambda i, t: t,
        [None, (F32, None), (BF16, None)], "res1_b", hosted=[cols16(gw_gu[:, :D_FF])])
    gw_out = _mm_tn(merged, d_mix, name="dw_out")
    d_merged = _mm(d_mix, w["w_out"], trans_b=True, out_dtype=BF16, name="dx_out")
    d_ga, d_gb, d_ya, d_yb, got["w_up"] = _stage_bwd(
        _f_merge, n_ts, in_mg, [], [_col(d_merged, TS)], lambda i, t: t, [(BF16, None)] * 4, "merge_b",
        hosted=[cols16(gw_gu[:, D_FF:])])
    gw_pa = _mm_tn(oa, d_ya, name="dw_pa")
    gw_pb = _mm_tn(ob, d_yb, name="dw_pb")
    d_oa = _mm(d_ya, w["w_pa"], trans_b=True, out_dtype=BF16, name="dx_pa")
    d_ob = _mm(d_yb, w["w_pb"], trans_b=True, out_dtype=BF16, name="dx_pb")
    d_og, d_r, g["gla_norm_w"], got["w_out"] = _stage_bwd(
        _f_gla_post, n_ts, in_gp, p_gp, [_col(d_oa, TS)], lambda i, t: t, [(BF16, None), None, (BF16, None)], "gla_post_b",
        hosted=[rows16(gw_out)])
    d_ys, d_xs_skip, d_z, g["dsk8"], g["ssm_norm_w"], got["w_pa"], got["w_pb"] = _stage_bwd(
        _f_ssd_post, n_ts, in_sp, p_sp, [_col(d_ob, TS)], lambda i, t: t,
        [(BF16, None), None, (BF16, None), (BF16, None)], "ssd_post_b", hosted=[rows16(gw_pa), rows16(gw_pb)])

    dq, dk, dv, d_lgf = _gla_bwd(proj, lg_f, sv_gf, d_og, None, ncc_g, False, "gla_f_b")
    dq, dk, dv, d_lgb = _gla_bwd(proj, lg_b, sv_gb, d_og, (dq, dk, dv), ncc_g, True, "gla_b_b")
    d_xf, d_bm, d_cm, d_axf, d_atf = _ssd_bwd(xf, xbc, axf, atf, sv_sf, d_ys, None, ncc_s, False, "ssd_f_b")
    d_xb, d_bm, d_cm, d_axb, d_atb = _ssd_bwd(xb, xbc, axb, atb, sv_sb, d_ys, (d_bm, d_cm), ncc_s, True, "ssd_b_b")

    cts_gd = [_col(a, TS) for a in (d_lgf, d_lgb, d_xf, d_xb, d_axf, d_axb)]
    cts_gd += [(a, (n_heads, TS), lambda i: (0, i)) for a in (d_atf, d_atb)]
    d_small, d_xs_dt, g["up_f"], g["up_b"], g["gla_bias_f"], g["gla_bias_b"], g["dtb"], g["alog"] = _stage_bwd(
        _f_gates_dt, n_ts, in_gd, p_gd, cts_gd, lambda i, t: t, [(BF16, None), (BF16, None)], "gates_dt_b")
    cts_conv = [_col(d_xs_skip, TM), _col(d_xs_dt, TM), _col(d_bm, TM), _col(d_cm, TM)]
    d_u, g["conv_w"], g["conv_b"] = _stage_bwd(
        f_conv, n_tm, in_conv, p_conv, cts_conv,
        lambda i, t: [jnp.concatenate([t[0] + t[1], t[2], t[3]], axis=1)], [(BF16, None)], "conv_b")

    d_proj = jnp.concatenate([dq, dk, dv, d_r, d_u, d_z, d_ga, d_gb, d_small], axis=1)
    gw_in = _mm_tn(h1, d_proj, name="dw_in", tn_t=1152, tm_t=2816)
    d_h1, got["w_in"] = _mm(d_proj, w["w_in"], trans_b=True, out_dtype=BF16, name="dx_in", tm_t=768, tk_t=3456,
                            hosted=[cols16(_from_padded(gw_in))])
    grad_x, g["norm1_w"], d_ada1 = _stage_bwd(
        f_norm1, n_tm, x_in, p_norm1, [_col(d_h1, TM)], lambda i, t: t,
        [None, (F32, (d_x1, (TM, D), x_in[1][2]))], "norm1_b")
    return loss_blk, grad_x, g, got, (d_ada1, d_ada2, d_ada3)


def _exchange(x, a2a, name):
    shp = x.shape[1:] if a2a else x.shape

    def body(x_ref, o_ref, send_sems, recv_sems, local_sem):
        ops = _exchange_ops(x_ref, o_ref, send_sems, recv_sems, local_sem, a2a)
        for op in ops:
            op.start()
        for op in ops:
            op.wait()

    return pl.pallas_call(
        body, out_shape=jax.ShapeDtypeStruct((N_DEV,) + tuple(shp), x.dtype),
        in_specs=[pl.BlockSpec(memory_space=pl.ANY)], out_specs=pl.BlockSpec(memory_space=pl.ANY),
        scratch_shapes=[pltpu.SemaphoreType.DMA((N_DEV - 1,)), pltpu.SemaphoreType.DMA((N_DEV - 1,)),
                        pltpu.SemaphoreType.DMA(())],
        name=name,
    )(x)


def _all_gather(x, name):
    def body(x_ref, o_ref, send_sems, recv_sems, local_sem):
        mx, my, mc = lax.axis_index("x"), lax.axis_index("y"), lax.axis_index("c")
        me, sibling = (mx, my, mc), (mx, my, 1 - mc)
        chips = [(1 - mx, my), (mx, 1 - my), (1 - mx, 1 - my)]

        def slot(px, py, pc):
            return o_ref.at[4 * px + 2 * py + pc]

        def copy(k, block, to, src=None):
            return pltpu.make_async_remote_copy(
                src_ref=slot(*block) if src is None else src, dst_ref=slot(*block),
                send_sem=send_sems.at[k], recv_sem=recv_sems.at[k], device_id=to, device_id_type=pl.DeviceIdType.MESH)

        own = pltpu.make_async_copy(x_ref, slot(*me), local_sem)
        own.start()
        first = [copy(0, me, sibling, src=x_ref)] + [copy(1 + j, me, (*chip, mc), src=x_ref) for j, chip in enumerate(chips)]
        for cp in first:
            cp.start()
        passed = [copy(4 + j, (*chip, mc), sibling) for j, chip in enumerate(chips)]
        for j, chip in enumerate(chips):
            copy(1 + j, (*chip, mc), me).wait_recv()
            passed[j].start()
        copy(0, sibling, me).wait_recv()
        for j, chip in enumerate(chips):
            copy(4 + j, (*chip, 1 - mc), me).wait_recv()
        for cp in first + passed:
            cp.wait_send()
        own.wait()

    return pl.pallas_call(
        body, out_shape=jax.ShapeDtypeStruct((N_DEV,) + tuple(x.shape), x.dtype),
        in_specs=[pl.BlockSpec(memory_space=pl.ANY)], out_specs=pl.BlockSpec(memory_space=pl.ANY),
        scratch_shapes=[pltpu.SemaphoreType.DMA((N_DEV - 1,)), pltpu.SemaphoreType.DMA((N_DEV - 1,)),
                        pltpu.SemaphoreType.DMA(())],
        name=name,
    )(x)


def _adamw_math(w, gr, m, v):
    m = ADAM_B1 * m + (1.0 - ADAM_B1) * gr
    v = ADAM_B2 * v + (1.0 - ADAM_B2) * (gr * gr)
    m_hat = m / np.float32(1.0 - ADAM_B1 ** ADAM_STEP)
    v_hat = v / np.float32(1.0 - ADAM_B2 ** ADAM_STEP)
    delta = -ADAM_LR * (m_hat / (jnp.sqrt(v_hat) + ADAM_EPS) + ADAM_WD * w)
    return delta, m, v


def _sum_adamw(parts, w, m, v, name):
    R, C = w.shape
    n_parts = parts.shape[0]
    tr = _pick(R, max(16, (2 << 20) // (4 * C) // 16 * 16), 16)

    def body(p_ref, w_ref, m_ref, v_ref, g_ref, d_ref, mo_ref, vo_ref):
        gr = p_ref[0].astype(F32)
        for k in range(1, n_parts):
            gr = gr + p_ref[k].astype(F32)
        g_ref[...] = gr
        d_ref[...], mo_ref[...], vo_ref[...] = _adamw_math(w_ref[...], gr, m_ref[...], v_ref[...])

    row = pl.BlockSpec((tr, C), lambda i: (i, 0))
    return pl.pallas_call(
        body, grid=(R // tr,),
        in_specs=[pl.BlockSpec((n_parts, tr, C), lambda i: (0, i, 0)), row, row, row],
        out_specs=[row] * 4, out_shape=[jax.ShapeDtypeStruct((R, C), F32)] * 4,
        compiler_params=_cparams(("parallel",)), name=name,
    )(parts, w, m, v)


def _sum8(parts, name):
    _, R, C = parts.shape

    def body(p_ref, g_ref):
        gr = p_ref[0]
        for k in range(1, N_DEV):
            gr = gr + p_ref[k]
        g_ref[...] = gr

    return pl.pallas_call(body, out_shape=jax.ShapeDtypeStruct((R, C), F32), name=name)(parts)


def _adamw(w, gr, m, v, name):
    def body(w_ref, g_ref, m_ref, v_ref, d_ref, mo_ref, vo_ref):
        d_ref[...], mo_ref[...], vo_ref[...] = _adamw_math(w_ref[...], g_ref[...], m_ref[...], v_ref[...])

    return pl.pallas_call(body, out_shape=[jax.ShapeDtypeStruct(w.shape, F32)] * 3, name=name)(w, gr, m, v)


def _to_padded(w_in):
    z = jnp.zeros(w_in.shape[:-1] + (32,), w_in.dtype)
    return jnp.concatenate([w_in[..., 0:3072], w_in[..., 5152:8224], w_in[..., 3104:5152], w_in[..., 8288:10336],
                            w_in[..., 3072:3104], w_in[..., 8224:8288], z], axis=-1)


def _from_padded(p):
    return jnp.concatenate([p[..., 0:3072], p[..., 10240:10272], p[..., 6144:8192], p[..., 3072:6144],
                            p[..., 10272:10336], p[..., 8192:10240]], axis=-1)


def _unshard_cols(gathered):
    n, r, c = gathered.shape
    return jnp.transpose(gathered, (1, 0, 2)).reshape(r, n * c)


def _shard_cols(full):
    r, nc = full.shape
    return jnp.transpose(full.reshape(r, N_DEV, nc // N_DEV), (1, 0, 2))


def _lanes(vec, lo):
    return jnp.zeros((1, 128), F32).at[:, lo:lo + vec.shape[1]].set(vec)


_SMALL = (("b_ada", 6 * D), ("c_ctx", D), ("norm1_w", D), ("gla_bias_f", 512), ("gla_bias_b", 512), ("gla_norm_w", 256),
          ("conv_b", 3072), ("dt_bias_f", 32), ("dt_bias_b", 32), ("a_log_f", 32), ("a_log_b", 32), ("d_skip", 32),
          ("ssm_norm_w", 2048), ("norm2_w", D), ("final_norm_w", D))
_SHARDED_SMALL = (("gla_up_f", 16 * 512), ("gla_up_b", 16 * 512), ("conv_w", 4 * 3072))


def _pack(vals, names):
    flat = jnp.concatenate([vals[n].reshape(-1).astype(F32) for n, _ in names])
    pad = (-flat.shape[0]) % 1024
    return jnp.concatenate([flat, jnp.zeros((pad,), F32)]).reshape(-1, 128)


def _unpack(packed, names, shapes):
    flat, out, off = packed.reshape(-1), {}, 0
    for n, size in names:
        out[n] = flat[off:off + size].reshape(shapes[n])
        off += size
    return out


def kernel(x, c, ctx, c_ctx, w_ada, b_ada, norm1_w, w_in, gla_up_f, gla_bias_f, gla_up_b, gla_bias_b, gla_norm_w, conv_w, conv_b, dt_bias_f, dt_bias_b, a_log_f, a_log_b, d_skip, ssm_norm_w, w_pa, w_pb, w_out, norm2_w, w_gate, w_up, w_down, final_norm_w, loss_target, m_c_ctx, m_w_ada, m_b_ada, m_norm1_w, m_w_in, m_gla_up_f, m_gla_bias_f, m_gla_up_b, m_gla_bias_b, m_gla_norm_w, m_conv_w, m_conv_b, m_dt_bias_f, m_dt_bias_b, m_a_log_f, m_a_log_b, m_d_skip, m_ssm_norm_w, m_w_pa, m_w_pb, m_w_out, m_norm2_w, m_w_gate, m_w_up, m_w_down, m_final_norm_w, v_c_ctx, v_w_ada, v_b_ada, v_norm1_w, v_w_in, v_gla_up_f, v_gla_bias_f, v_gla_up_b, v_gla_bias_b, v_gla_norm_w, v_conv_w, v_conv_b, v_dt_bias_f, v_dt_bias_b, v_a_log_f, v_a_log_b, v_d_skip, v_ssm_norm_w, v_w_pa, v_w_pb, v_w_out, v_norm2_w, v_w_gate, v_w_up, v_w_down, v_final_norm_w):
    args = dict(locals())
    me = 4 * lax.axis_index("x") + 2 * lax.axis_index("y") + lax.axis_index("c")

    def gather16(a, name):
        return _all_gather(a[0].astype(BF16), name)

    full = {"w_in": _to_padded(_unshard_cols(gather16(w_in, "ag_w_in")))}
    shards = {n: args[n][0].astype(BF16) for n in ("w_pa", "w_pb", "w_out", "w_gate", "w_up", "w_down")}
    sm = _exchange(jnp.concatenate([gla_up_f.reshape(-1), gla_up_b.reshape(-1), conv_w.reshape(-1), c.reshape(-1)]).reshape(-1, 128),
                   False, "ag_small")
    sm = sm.reshape(N_DEV, -1)
    n_ada = w_ada.shape[2]
    cc = jnp.zeros((_ADA_ROWS, D), F32).at[0:N_DEV].set(sm[:, 3584:3584 + D]).at[N_DEV].set(c_ctx)
    w_ada16 = w_ada[0].astype(BF16)
    ada_cols = _exchange(_ada_fwd(cc, w_ada16, lax.dynamic_slice(b_ada, (0, me * n_ada), (1, n_ada))), False, "ag_ada")
    ada = jnp.zeros((8, N_DEV * n_ada), F32)
    ada = ada.at[0].set(lax.dynamic_index_in_dim(ada_cols, me, axis=1, keepdims=False).reshape(-1))
    ada = ada.at[1].set(ada_cols[:, N_DEV].reshape(-1))
    up_f = _unshard_cols(sm[:, 0:1024].reshape(N_DEV, 16, 64))
    up_b = _unshard_cols(sm[:, 1024:2048].reshape(N_DEV, 16, 64))
    full["conv_w"] = _unshard_cols(sm[:, 2048:3584].reshape(N_DEV, 4, 384))
    full["up_f"] = jnp.zeros((128, 512), F32).at[0:16].set(up_f)
    full["up_b"] = jnp.zeros((128, 512), F32).at[16:32].set(up_b)
    full["dtb"] = _lanes(dt_bias_f, 32) + _lanes(dt_bias_b, 64)
    full["alog"] = _lanes(a_log_f, 32) + _lanes(a_log_b, 64)
    full["dsk8"] = jnp.zeros((8, 128), F32).at[0:1, 0:32].set(d_skip)
    for n in ("norm1_w", "gla_bias_f", "gla_bias_b", "gla_norm_w", "conv_b", "ssm_norm_w", "norm2_w"):
        full[n] = args[n]
    full["final_norm_w"] = final_norm_w.reshape(1, D)

    loss_blk, grad_x, g, got, d_ada = _local_step(x[0], ctx[0], loss_target[0], ada, full, shards)

    d_all = _exchange(jnp.concatenate([d[0:2] for d in d_ada], axis=0), False, "ag_d_ada").reshape(N_DEV * 6, N_DEV * n_ada)
    gw_ada, gb_ada, d_cc = _ada_bwd(cc, w_ada16, d_all, lax.dynamic_slice(d_all, (0, me * n_ada), (N_DEV * 6, n_ada)))

    gs = dict(g, c_ctx=d_cc[N_DEV], b_ada=jnp.zeros_like(b_ada))
    gs["dt_bias_f"], gs["dt_bias_b"] = g["dtb"][:, 32:64], g["dtb"][:, 64:96]
    gs["a_log_f"], gs["a_log_b"] = g["alog"][:, 32:64], g["alog"][:, 64:96]
    gs["d_skip"] = g["dsk8"][0:1, 0:32]
    gs["gla_up_f"], gs["gla_up_b"] = g["up_f"][0:16], g["up_b"][16:32]
    gs["loss"] = loss_blk[0:1, 0:1]
    names = _SMALL + _SHARDED_SMALL + (("loss", 1),)
    shapes = {n: (args[n].shape if n in args else (1, 1)) for n, _ in names}
    shapes.update({"gla_up_f": (16, 512), "gla_up_b": (16, 512), "conv_w": (4, 3072)})
    red = _unpack(_sum8(_exchange(_pack(gs, names), False, "ag_small_grads"), "sum_small_grads"), names, shapes)
    loss = red["loss"].reshape(())
    grads = {n: red[n] for n, _ in _SMALL}
    grads["b_ada"] = gb_ada
    grads["gla_up_f"] = lax.dynamic_slice(red["gla_up_f"], (0, me * 64), (16, 64))[None]
    grads["gla_up_b"] = lax.dynamic_slice(red["gla_up_b"], (0, me * 64), (16, 64))[None]
    grads["conv_w"] = lax.dynamic_slice(red["conv_w"], (0, me * 384), (4, 384))[None]
    upd_names = tuple((n, s) for n, s in _SMALL) + (("gla_up_f", 1024), ("gla_up_b", 1024), ("conv_w", 1536))
    pk = lambda prefix, src: _pack({n: src[prefix + n] for n, _ in upd_names}, upd_names)
    d_s, m_s, v_s = _adamw(pk("", args), _pack(grads, upd_names), pk("m_", args), pk("v_", args), "adamw_small")
    upd_shapes = {n: args[n].shape for n, _ in upd_names}
    delta = _unpack(d_s, upd_names, upd_shapes)
    new_m = _unpack(m_s, upd_names, upd_shapes)
    new_v = _unpack(v_s, upd_names, upd_shapes)

    got["w_ada"] = gw_ada[None]
    for n, parts in got.items():
        gr, dl, mo, vo = _sum_adamw(parts, args[n][0], args["m_" + n][0], args["v_" + n][0], "adamw_" + n)
        grads[n], delta[n], new_m[n], new_v[n] = gr[None], dl[None], mo[None], vo[None]

    order = ["c_ctx", "w_ada", "b_ada", "norm1_w", "w_in", "gla_up_f", "gla_bias_f", "gla_up_b", "gla_bias_b", "gla_norm_w",
             "conv_w", "conv_b", "dt_bias_f", "dt_bias_b", "a_log_f", "a_log_b", "d_skip", "ssm_norm_w", "w_pa", "w_pb",
             "w_out", "norm2_w", "w_gate", "w_up", "w_down", "final_norm_w"]
    fix = lambda d: [d[n].reshape(args[n].shape) for n in order]
    return (loss, grad_x[None], *fix(grads), *fix(delta), *fix(new_m), *fix(new_v))
```

```python
import functools

import jax
import jax.numpy as jnp
import numpy as np
from jax import lax
from jax.experimental import pallas as pl
from jax.experimental.pallas import tpu as pltpu

F32 = jnp.float32
BF16 = jnp.bfloat16
N_DEV = 8
D = 1024
EPS = 1e-6
GRID_W = 64
GLA_H, GLA_DK, GLA_DV = 4, 128, 256
GLA_C = 64
GLA_TAU = 16.0
SSM_G, SSM_HPG, SSM_P, SSM_N = 4, 8, 64, 128
SSM_C = 128
SSM_INNER = 2048
D_FF = 2816
D_IN = 10336
D_INP = 10368
C_Q, C_K, C_V, C_R, C_U, C_Z, C_GA, C_GB, C_S = 0, 512, 1024, 2048, 3072, 6144, 8192, 9216, 10240
TM = 256
TS = 128

ADAM_LR, ADAM_B1, ADAM_B2, ADAM_EPS, ADAM_WD, ADAM_STEP = 0.001, 0.9, 0.999, 1e-08, 0.01, 10
VMEM_LIMIT = 56 << 20


def _cparams(sem):
    return pltpu.CompilerParams(dimension_semantics=sem or None, vmem_limit_bytes=VMEM_LIMIT)


def _pick(n, target, mult):
    best = None
    for t in range(mult, min(n, target) + 1, mult):
        if n % t == 0:
            best = t
    return best if best is not None else n


def _dot_impl(a, b, ta, tb):
    dims = (((0 if ta else 1,), (1 if tb else 0,)), ((), ()))
    return lax.dot_general(a.astype(BF16), b.astype(BF16), dims, preferred_element_type=F32)


@functools.partial(jax.custom_vjp, nondiff_argnums=(2, 3))
def _bdot(a, b, ta=False, tb=False):
    return _dot_impl(a, b, ta, tb)


def _bdot_fwd(a, b, ta, tb):
    return _dot_impl(a, b, ta, tb), (a, b)


def _bdot_bwd(ta, tb, res, g):
    a, b = res
    if not ta and not tb:
        return _dot_impl(g, b, False, True), _dot_impl(a, g, True, False)
    if not ta and tb:
        return _dot_impl(g, b, False, False), _dot_impl(g, a, True, False)
    if ta and not tb:
        return _dot_impl(b, g, False, True), _dot_impl(a, g, False, False)
    raise NotImplementedError


_bdot.defvjp(_bdot_fwd, _bdot_bwd)


_NN = (((1,), (0,)), ((), ()))
_NT = (((1,), (1,)), ((), ()))
_TN = (((0,), (0,)), ((), ()))


def _dg2(x, e, x_is_lhs, dims):
    hi = x.astype(BF16)
    lo = (x - hi.astype(F32)).astype(BF16)
    e = e.astype(BF16)
    if x_is_lhs:
        return (lax.dot_general(hi, e, dims, preferred_element_type=F32)
                + lax.dot_general(lo, e, dims, preferred_element_type=F32))
    return (lax.dot_general(e, hi, dims, preferred_element_type=F32)
            + lax.dot_general(e, lo, dims, preferred_element_type=F32))


_EDOT_FWD = {"xe": (True, _NN), "ex": (False, _NN), "ext": (False, _NT)}
_EDOT_BWD = {"xe": (True, _NT), "ex": (False, _TN), "ext": (True, _TN)}


@functools.partial(jax.custom_vjp, nondiff_argnums=(2,))
def _edot(x, e, mode):
    return _dg2(x, e, *_EDOT_FWD[mode])


def _edot_fwd(x, e, mode):
    return _dg2(x, e, *_EDOT_FWD[mode]), e


def _edot_bwd(mode, e, g):
    return _dg2(g, e, *_EDOT_BWD[mode]), None


_edot.defvjp(_edot_fwd, _edot_bwd)


def _shift_impl(u, pos, per, s):
    n = u.shape[0]
    rolled = u if s == 0 else pltpu.roll(u, (-s) % n, 0)
    ok = (pos + s >= 0) & (pos + s < per)
    return jnp.where(ok, rolled, 0.0)


@functools.partial(jax.custom_vjp, nondiff_argnums=(3,))
def _shift(u, pos, per, s):
    return _shift_impl(u, pos, per, s)


def _shift_fwd(u, pos, per, s):
    return _shift_impl(u, pos, per, s), (pos, per)


def _shift_bwd(s, res, g):
    pos, per = res
    return _shift_impl(g, pos, per, -s), None, None


_shift.defvjp(_shift_fwd, _shift_bwd)


def _rms(x, w):
    return x * lax.rsqrt(jnp.mean(x * x, axis=-1, keepdims=True) + EPS) * w


def _silu(x):
    return x * jax.nn.sigmoid(x)


def _softplus(x):
    return jnp.maximum(x, 0.0) + jnp.log(1.0 + jnp.exp(-jnp.abs(x)))


def _logsig(x):
    return jnp.minimum(x, 0.0) - jnp.log(1.0 + jnp.exp(-jnp.abs(x)))


def _tri(n, rev):
    t = lax.broadcasted_iota(jnp.int32, (n, n), 0)
    s = lax.broadcasted_iota(jnp.int32, (n, n), 1)
    return (s >= t) if rev else (t >= s)


def _head_expand(first_lane):
    l = lax.broadcasted_iota(jnp.int32, (128, SSM_INNER), 0)
    c = lax.broadcasted_iota(jnp.int32, (128, SSM_INNER), 1)
    return (l == first_lane + lax.shift_right_logical(c, 6)).astype(F32)


def _exchange_ops(x_ref, o_ref, send_sems, recv_sems, local_sem, a2a):
    mx, my, mc = lax.axis_index("x"), lax.axis_index("y"), lax.axis_index("c")
    me = 4 * mx + 2 * my + mc
    ops = []
    for k in range(1, N_DEV):
        px = 1 - mx if k & 4 else mx
        py = 1 - my if k & 2 else my
        pc = 1 - mc if k & 1 else mc
        ops.append(pltpu.make_async_remote_copy(
            src_ref=x_ref.at[4 * px + 2 * py + pc] if a2a else x_ref, dst_ref=o_ref.at[me],
            send_sem=send_sems.at[k - 1], recv_sem=recv_sems.at[k - 1],
            device_id=(px, py, pc), device_id_type=pl.DeviceIdType.MESH))
    ops.append(pltpu.make_async_copy(x_ref.at[me] if a2a else x_ref, o_ref.at[me], local_sem))
    return ops


def _call(body, *, grid, in_specs, out_specs, out_shape, scratch=(), sem, name, args, hosted=()):
    n_in, n_out, n_s, n_h = len(in_specs), len(out_specs), len(scratch), len(hosted)
    if not n_h:
        return pl.pallas_call(body, grid=grid, in_specs=list(in_specs), out_specs=list(out_specs),
                              out_shape=list(out_shape), scratch_shapes=list(scratch),
                              compiler_params=_cparams(sem), name=name)(*args)
    b0, c0 = n_in + n_h, n_in + n_h + n_out
    d0 = c0 + n_h

    def wrapped(*refs):
        hx, ho = refs[n_in:b0], refs[c0:d0]
        send_sems, recv_sems, local_sems = refs[d0 + n_s:]
        ids = [pl.program_id(ax) for ax in range(len(grid))]
        first = functools.reduce(jnp.logical_and, [i == 0 for i in ids])
        last = functools.reduce(jnp.logical_and, [i == n - 1 for i, n in zip(ids, grid)])

        def ops():
            return [op for n, (_, a2a) in enumerate(hosted)
                    for op in _exchange_ops(hx[n], ho[n], send_sems.at[n], recv_sems.at[n], local_sems.at[n], a2a)]

        @pl.when(first)
        def _():
            for op in ops():
                op.start()

        body(*refs[:n_in], *refs[b0:c0], *refs[d0:d0 + n_s])

        @pl.when(last)
        def _():
            for op in ops():
                op.wait()

    hbm = pl.BlockSpec(memory_space=pl.ANY)
    h_shapes = [jax.ShapeDtypeStruct((N_DEV,) + tuple(x.shape[1:] if a2a else x.shape), x.dtype) for x, a2a in hosted]
    return pl.pallas_call(
        wrapped, grid=grid, in_specs=list(in_specs) + [hbm] * n_h, out_specs=list(out_specs) + [hbm] * n_h,
        out_shape=list(out_shape) + h_shapes,
        scratch_shapes=list(scratch) + [pltpu.SemaphoreType.DMA((n_h, N_DEV - 1)), pltpu.SemaphoreType.DMA((n_h, N_DEV - 1)),
                                        pltpu.SemaphoreType.DMA((n_h,))],
        compiler_params=_cparams(("arbitrary",) * len(grid)), name=name,
    )(*args, *[x for x, _ in hosted])


def _mm(a, b, *, trans_b, out_dtype, name, tm_t=1408, tn_t=1024, tk_t=2816, hosted=()):
    M, K = a.shape
    N = b.shape[0] if trans_b else b.shape[1]
    tm, tn, tk = _pick(M, tm_t, 8), _pick(N, tn_t, 128), _pick(K, tk_t, 128)
    nk = K // tk
    dims = (((1,), (1,)), ((), ())) if trans_b else (((1,), (0,)), ((), ()))

    def body_one(a_ref, b_ref, o_ref):
        o_ref[...] = lax.dot_general(a_ref[...], b_ref[...], dims, preferred_element_type=F32).astype(out_dtype)

    def body_acc(a_ref, b_ref, o_ref, acc_ref):
        k = pl.program_id(2)
        p = lax.dot_general(a_ref[...], b_ref[...], dims, preferred_element_type=F32)

        @pl.when(k == 0)
        def _():
            acc_ref[...] = p

        @pl.when(k > 0)
        def _():
            acc_ref[...] += p

        @pl.when(k == nk - 1)
        def _():
            o_ref[...] = acc_ref[...].astype(out_dtype)

    b_spec = pl.BlockSpec((tn, tk), lambda j, i, k: (j, k)) if trans_b else pl.BlockSpec((tk, tn), lambda j, i, k: (k, j))
    outs = _call(
        body_one if nk == 1 else body_acc, grid=(N // tn, M // tm, nk),
        in_specs=[pl.BlockSpec((tm, tk), lambda j, i, k: (i, k)), b_spec],
        out_specs=[pl.BlockSpec((tm, tn), lambda j, i, k: (i, j))],
        out_shape=[jax.ShapeDtypeStruct((M, N), out_dtype)],
        scratch=[] if nk == 1 else [pltpu.VMEM((tm, tn), F32)],
        sem=("parallel", "parallel", "arbitrary"), name=name, args=(a, b), hosted=hosted)
    return outs if hosted else outs[0]


def _mm_tn(a, b, *, name, tm_t=2816, tr_t=1024, tn_t=1408):
    M, R = a.shape
    N = b.shape[1]
    tm, tr, tn = _pick(M, tm_t, 8), _pick(R, tr_t, 128), _pick(N, tn_t, 128)

    def body(a_ref, b_ref, o_ref):
        m = pl.program_id(2)
        p = lax.dot_general(a_ref[...], b_ref[...], (((0,), (0,)), ((), ())), preferred_element_type=F32)

        @pl.when(m == 0)
        def _():
            o_ref[...] = p

        @pl.when(m > 0)
        def _():
            o_ref[...] += p

    return pl.pallas_call(
        body, grid=(R // tr, N // tn, M // tm),
        in_specs=[pl.BlockSpec((tm, tr), lambda r, j, m: (m, r)), pl.BlockSpec((tm, tn), lambda r, j, m: (m, j))],
        out_specs=pl.BlockSpec((tr, tn), lambda r, j, m: (r, j)),
        out_shape=jax.ShapeDtypeStruct((R, N), F32),
        compiler_params=_cparams(("parallel", "parallel", "arbitrary")), name=name,
    )(a, b)


def _col(arr, tm, width=None, col=0):
    width = arr.shape[1] if width is None else width
    return (arr, (tm, width), lambda i: (i, col))


def _lat(arr, tm, nct):
    return (arr, (tm, arr.shape[1]), lambda i: (jnp.maximum(i - nct, 0), 0))


def _ctx(arr, tm, nct):
    return (arr, (tm, arr.shape[1]), lambda i: (jnp.minimum(i, nct - 1), 0))


def _whole(p):
    return pl.BlockSpec(p.shape, lambda i, nd=p.ndim: (0,) * nd)


def _stage_fwd(f, n_tiles, ins, params, outs, name):
    ni, npar = len(ins), len(params)
    o_specs = [pl.BlockSpec((o[1], o[3]), lambda i: (0, i)) if len(o) > 4 else pl.BlockSpec((o[3], o[1]), lambda i: (i, 0))
               for o in outs]
    o_shapes = [jax.ShapeDtypeStruct((o[1], o[0]) if len(o) > 4 else (o[0], o[1]), o[2]) for o in outs]

    def body(*refs):
        i = pl.program_id(0)
        xs = [r[...].astype(F32) for r in refs[:ni]]
        ps = [r[...] for r in refs[ni:ni + npar]]
        for r, v in zip(refs[ni + npar:], f(i, xs, ps, False)):
            r[...] = v.astype(r.dtype)

    return pl.pallas_call(
        body, grid=(n_tiles,),
        in_specs=[pl.BlockSpec(bs, fn) for _, bs, fn in ins] + [_whole(p) for p in params],
        out_specs=o_specs, out_shape=o_shapes,
        compiler_params=_cparams(("parallel",)), name=name,
    )(*[a for a, _, _ in ins], *params)


def _stage_bwd(f, n_tiles, ins, params, cts, ct_fn, dins, name, hosted=()):
    ni, npar, nc = len(ins), len(params), len(cts)
    want = [k for k, d in enumerate(dins) if d is not None]
    extras = [dins[k][1] for k in want if dins[k][1] is not None]

    def body(*refs):
        i = pl.program_id(0)
        xs = [r[...].astype(F32) for r in refs[:ni]]
        ps = [r[...] for r in refs[ni:ni + npar]]
        ct_tiles = [r[...].astype(F32) for r in refs[ni + npar:ni + npar + nc]]
        ex_refs = list(refs[ni + npar + nc:ni + npar + nc + len(extras)])
        out_refs = refs[ni + npar + nc + len(extras):]
        _, vjp = jax.vjp(lambda xs_, ps_: tuple(f(i, xs_, ps_, True)), xs, ps)
        dxs, dps = vjp(tuple(ct_fn(i, ct_tiles)))
        for n, k in enumerate(want):
            v = dxs[k]
            if dins[k][1] is not None:
                v = v + ex_refs.pop(0)[...].astype(F32)
            out_refs[n][...] = v.astype(out_refs[n].dtype)
        for r, v in zip(out_refs[len(want):], dps):
            @pl.when(i == 0)
            def _(r=r, v=v):
                r[...] = v

            @pl.when(i > 0)
            def _(r=r, v=v):
                r[...] += v

    return _call(
        body, grid=(n_tiles,),
        in_specs=([pl.BlockSpec(bs, fn) for _, bs, fn in ins] + [_whole(p) for p in params]
                  + [pl.BlockSpec(bs, fn) for _, bs, fn in cts] + [pl.BlockSpec(bs, fn) for _, bs, fn in extras]),
        out_specs=([pl.BlockSpec(ins[k][1], lambda i, fn=ins[k][2]: (fn(i)[0], 0)) for k in want]
                   + [_whole(p) for p in params]),
        out_shape=([jax.ShapeDtypeStruct((ins[k][0].shape[0], ins[k][1][1]), dins[k][0]) for k in want]
                   + [jax.ShapeDtypeStruct(p.shape, F32) for p in params]),
        sem=("arbitrary",), name=name,
        args=(*[a for a, _, _ in ins], *params, *[a for a, _, _ in cts], *[a for a, _, _ in extras]), hosted=hosted)


def _mod_rows(i, nct, ada, lo):
    return jnp.where(i < nct, ada[1:2, lo:lo + D], ada[0:1, lo:lo + D])


def _f_norm1(nct):
    def f(i, xs, ps, diff):
        ctx, x = xs
        nw, ada = ps
        xt = jnp.where(i < nct, ctx, x)
        return (_rms(xt, nw) * (1.0 + _mod_rows(i, nct, ada, D)) + _mod_rows(i, nct, ada, 0),)
    return f


def _f_gates_dt(i, xs, ps, diff):
    small, xc = xs
    up_f, up_b, gb_f, gb_b, dtb, alog = ps
    dot = _bdot if diff else _dot_impl_nn
    lg_f = _logsig(dot(small, up_f) + gb_f) * (1.0 / GLA_TAU)
    lg_b = _logsig(dot(small, up_b) + gb_b) * (1.0 / GLA_TAU)
    dtp = _softplus(small + dtb)
    aa = -jnp.exp(alog) * dtp
    e_f, e_b = _head_expand(32), _head_expand(64)
    n = small.shape[0]
    t = lax.broadcasted_iota(jnp.int32, (n, n), 0)
    s = lax.broadcasted_iota(jnp.int32, (n, n), 1)
    same = lax.shift_right_logical(t, 6) == lax.shift_right_logical(s, 6)
    pre_g, suf_g = ((t >= s) & same).astype(F32), ((s >= t) & same).astype(F32)
    pre_s, suf_s = (t >= s).astype(F32), (s >= t).astype(F32)
    cum_f, cum_b = _edot(aa, pre_s, "ex"), _edot(aa, suf_s, "ex")
    h = lax.broadcasted_iota(jnp.int32, (SSM_G * SSM_HPG, 128), 0)
    l = lax.broadcasted_iota(jnp.int32, (SSM_G * SSM_HPG, 128), 1)
    cum_ft = _edot(cum_f, (l == h + 32).astype(F32), "ext")
    cum_bt = _edot(cum_b, (l == h + 64).astype(F32), "ext")
    return (_edot(lg_f, pre_g, "ex"), _edot(lg_b, suf_g, "ex"), xc * _edot(dtp, e_f, "xe"), xc * _edot(dtp, e_b, "xe"),
            _edot(cum_f, e_f, "xe"), _edot(cum_b, e_b, "xe"), cum_ft, cum_bt)


def _dot_impl_nn(a, b):
    return _dot_impl(a, b, False, False)


def _f_conv(nct, tc):
    def f(i, xs, ps, diff):
        (u,) = xs
        cw, cb = ps
        n = u.shape[0]
        t = lax.broadcasted_iota(jnp.int32, (n, 1), 0)
        per = jnp.where(i < nct, tc, GRID_W)
        pos = jnp.bitwise_and(t, per - 1)
        sh = _shift if diff else _shift_impl
        acc = cb + sh(u, pos, per, -2) * cw[0:1]
        for j in range(1, 4):
            acc = acc + sh(u, pos, per, j - 2) * cw[j:j + 1]
        return (_silu(acc),)
    return f


def _f_gla_post(i, xs, ps, diff):
    ogf, ogb, r = xs
    (gw,) = ps
    o = ogf + ogb
    parts = [_rms(o[:, h * GLA_DV:(h + 1) * GLA_DV], gw) for h in range(GLA_H)]
    return (jnp.concatenate(parts, axis=1) * _silu(r),)


def _f_ssd_post(i, xs, ps, diff):
    yf, yb, xc, z = xs
    dsk8, nw = ps
    dsk = _edot(dsk8, _head_expand(0), "xe")[0:1]
    y = (yf + yb + dsk * xc) * _silu(z)
    w = SSM_INNER // SSM_G
    parts = [_rms(y[:, g * w:(g + 1) * w], nw[:, g * w:(g + 1) * w]) for g in range(SSM_G)]
    return (jnp.concatenate(parts, axis=1),)


def _f_merge(i, xs, ps, diff):
    ga, gb, ya, yb = xs
    return (jax.nn.sigmoid(ga) * ya + jax.nn.sigmoid(gb) * yb,)


def _f_res1(nct):
    def f(i, xs, ps, diff):
        ctx, x, mix = xs
        ada, nw = ps
        h2 = jnp.where(i < nct, ctx, x) + _mod_rows(i, nct, ada, 2 * D) * mix
        return (h2, _rms(h2, nw) * (1.0 + _mod_rows(i, nct, ada, 4 * D)) + _mod_rows(i, nct, ada, 3 * D))
    return f


def _f_swiglu(i, xs, ps, diff):
    (gu,) = xs
    return (_silu(gu[:, :D_FF]) * gu[:, D_FF:],)


def _loss_head(h2, dn, tgt, ada, fw, nct):
    A = h2.shape[0]
    n_tiles = A // TM

    def tile_loss(i, h2t, dnt, tg, ada_, fw_):
        h3 = h2t + ada_[0:1, 5 * D:6 * D] * dnt
        err = _rms(h3, fw_) - tg
        row = 0.5 * jnp.mean(err * err, axis=-1, keepdims=True)
        return jnp.sum(row, axis=0, keepdims=True) * jnp.where(i < nct, 0.0, 1.0)

    def body(h2_ref, dn_ref, tg_ref, ada_ref, fw_ref, loss_ref, dh_ref, ddn_ref, dada_ref, dfw_ref):
        i = pl.program_id(0)
        val, vjp = jax.vjp(functools.partial(tile_loss, i), h2_ref[...], dn_ref[...].astype(F32), tg_ref[...], ada_ref[...],
                           fw_ref[...])
        dh, ddn, _, dada, dfw = vjp(jnp.ones((1, 1), F32))
        dh_ref[...] = dh
        ddn_ref[...] = ddn.astype(BF16)
        lv = jnp.broadcast_to(val, loss_ref.shape)
        for r, v in ((loss_ref, lv), (dada_ref, dada), (dfw_ref, dfw)):
            @pl.when(i == 0)
            def _(r=r, v=v):
                r[...] = v

            @pl.when(i > 0)
            def _(r=r, v=v):
                r[...] += v

    row = lambda i: (i, 0)
    return pl.pallas_call(
        body, grid=(n_tiles,),
        in_specs=[pl.BlockSpec((TM, D), row), pl.BlockSpec((TM, D), row),
                  pl.BlockSpec((TM, D), lambda i: (jnp.maximum(i - nct, 0), 0)), _whole(ada), _whole(fw)],
        out_specs=[pl.BlockSpec((8, 128), lambda i: (0, 0)), pl.BlockSpec((TM, D), row), pl.BlockSpec((TM, D), row),
                   _whole(ada), _whole(fw)],
        out_shape=[jax.ShapeDtypeStruct((8, 128), F32), jax.ShapeDtypeStruct((A, D), F32),
                   jax.ShapeDtypeStruct((A, D), BF16), jax.ShapeDtypeStruct(ada.shape, F32),
                   jax.ShapeDtypeStruct(fw.shape, F32)],
        compiler_params=_cparams(("arbitrary",)), name="loss_head",
    )(h2, dn, tgt, ada, fw)


def _chunk_of(step, n_chunks, n_ctx_chunks, rev):
    if not rev:
        return step
    return jnp.where(step < n_ctx_chunks, n_ctx_chunks - 1 - step, n_chunks - 1 - (step - n_ctx_chunks))


def _gla_step(st, q, k, v, b, rev):
    tri = _tri(GLA_C, rev).astype(F32)
    last = 0 if rev else GLA_C - 1
    outs, sts = [], []
    for h in range(GLA_H):
        kk = slice(h * GLA_DK, (h + 1) * GLA_DK)
        vv = slice(h * GLA_DV, (h + 1) * GLA_DV)
        qh, kh, vh, bh, sh = q[:, kk] * (GLA_DK ** -0.5), k[:, kk], v[:, vv], b[:, kk], st[vv]
        tot = bh[last:last + 1]
        mid = bh[GLA_C // 2:GLA_C // 2 + 1]
        att = _bdot(qh * jnp.exp(bh - mid), kh * jnp.exp(mid - bh), False, True) * tri
        outs.append(_bdot(att, vh) + _bdot(qh * jnp.exp(bh), sh, False, True))
        sts.append(sh * jnp.exp(tot) + _bdot(vh, kh * jnp.exp(tot - bh), True, False))
    return jnp.concatenate(sts, axis=0), jnp.concatenate(outs, axis=1)


_GQK, _GV = GLA_H * GLA_DK, GLA_H * GLA_DV


GLA_CPS = 2
_GB = GLA_C * GLA_CPS


def _gla_order(rev):
    return list(reversed(range(GLA_CPS))) if rev else list(range(GLA_CPS))


def _gla_fwd(proj, bg, ncc, rev, name):
    A = proj.shape[0]
    nb = A // _GB
    ch = lambda s: _chunk_of(s, nb, ncc // GLA_CPS, rev)

    def body(q_ref, k_ref, v_ref, b_ref, o_ref, ss_ref, st_ref):
        @pl.when(pl.program_id(0) == 0)
        def _():
            st_ref[...] = jnp.zeros_like(st_ref)

        st = st_ref[...]
        for pos, sub in enumerate(_gla_order(rev)):
            rows = slice(sub * GLA_C, (sub + 1) * GLA_C)
            ss_ref[0, pos] = st
            st, o = _gla_step(st, q_ref[rows].astype(F32), k_ref[rows].astype(F32), v_ref[rows].astype(F32), b_ref[rows], rev)
            o_ref[rows] = o.astype(o_ref.dtype)
        st_ref[...] = st

    return pl.pallas_call(
        body, grid=(nb,),
        in_specs=[pl.BlockSpec((_GB, _GQK), lambda s: (ch(s), C_Q // _GQK)),
                  pl.BlockSpec((_GB, _GQK), lambda s: (ch(s), C_K // _GQK)),
                  pl.BlockSpec((_GB, _GV), lambda s: (ch(s), C_V // _GV)),
                  pl.BlockSpec((_GB, _GQK), lambda s: (ch(s), 0))],
        out_specs=[pl.BlockSpec((_GB, _GV), lambda s: (ch(s), 0)),
                   pl.BlockSpec((1, GLA_CPS, _GV, GLA_DK), lambda s: (s, 0, 0, 0))],
        out_shape=[jax.ShapeDtypeStruct((A, _GV), BF16), jax.ShapeDtypeStruct((nb, GLA_CPS, _GV, GLA_DK), F32)],
        scratch_shapes=[pltpu.VMEM((_GV, GLA_DK), F32)],
        compiler_params=_cparams(("arbitrary",)), name=name,
    )(proj, proj, proj, bg)


def _gla_bwd(proj, lg, saved, d_o, prev, ncc, rev, name):
    A = proj.shape[0]
    nb = A // _GB
    st_of = lambda r: nb - 1 - r
    ch = lambda r: _chunk_of(st_of(r), nb, ncc // GLA_CPS, rev)
    qs = pl.BlockSpec((_GB, _GQK), lambda r: (ch(r), 0))
    vs = pl.BlockSpec((_GB, _GV), lambda r: (ch(r), 0))
    n_prev = 0 if prev is None else 3

    def body(*refs):
        q_ref, k_ref, v_ref, b_ref, ss_ref, do_ref = refs[:6]
        p_refs = refs[6:6 + n_prev]
        dq_ref, dk_ref, dv_ref, db_ref, ds_ref = refs[6 + n_prev:]

        @pl.when(pl.program_id(0) == 0)
        def _():
            ds_ref[...] = jnp.zeros_like(ds_ref)

        ds = ds_ref[...]
        for pos, sub in reversed(list(enumerate(_gla_order(rev)))):
            rows = slice(sub * GLA_C, (sub + 1) * GLA_C)
            _, vjp = jax.vjp(functools.partial(_gla_step, rev=rev), ss_ref[0, pos], q_ref[rows].astype(F32),
                             k_ref[rows].astype(F32), v_ref[rows].astype(F32), b_ref[rows])
            ds, dq, dk, dv, db = vjp((ds, do_ref[rows].astype(F32)))
            if n_prev:
                dq, dk, dv = [d + p[rows].astype(F32) for d, p in zip((dq, dk, dv), p_refs)]
            for r, d in zip((dq_ref, dk_ref, dv_ref, db_ref), (dq, dk, dv, db)):
                r[rows] = d.astype(r.dtype)
        ds_ref[...] = ds

    in_specs = [
        pl.BlockSpec((_GB, _GQK), lambda r: (ch(r), C_Q // _GQK)),
        pl.BlockSpec((_GB, _GQK), lambda r: (ch(r), C_K // _GQK)),
        pl.BlockSpec((_GB, _GV), lambda r: (ch(r), C_V // _GV)),
        qs,
        pl.BlockSpec((1, GLA_CPS, _GV, GLA_DK), lambda r: (st_of(r), 0, 0, 0)),
        vs,
    ] + ([qs, qs, vs] if n_prev else [])
    return pl.pallas_call(
        body, grid=(nb,), in_specs=in_specs, out_specs=[qs, qs, vs, qs],
        out_shape=[jax.ShapeDtypeStruct((A, _GQK), BF16), jax.ShapeDtypeStruct((A, _GQK), BF16),
                   jax.ShapeDtypeStruct((A, _GV), BF16), jax.ShapeDtypeStruct((A, _GQK), F32)],
        scratch_shapes=[pltpu.VMEM((_GV, GLA_DK), F32)],
        compiler_params=_cparams(("arbitrary",)), name=name,
    )(proj, proj, proj, lg, saved, d_o, *(prev or ()))


def _ssd_step(st, x, bm, cm, cum, cum_t, rev):
    mask = _tri(SSM_C, rev)
    last = 0 if rev else SSM_C - 1
    tot = cum[last:last + 1]
    cb = _bdot(cm, bm, False, True)
    ys = []
    for e in range(SSM_HPG):
        lm = jnp.exp(jnp.where(mask, cum[:, e * SSM_P:e * SSM_P + 1] - cum_t[e:e + 1], -jnp.inf))
        ys.append(_bdot(cb * lm, x[:, e * SSM_P:(e + 1) * SSM_P]))
    y = jnp.concatenate(ys, axis=1) + _bdot(cm, st) * jnp.exp(cum)
    st_new = st * jnp.exp(tot) + _bdot(bm, x * jnp.exp(tot - cum), True, False)
    return st_new, y


SSM_GPS = 4
_W = SSM_HPG * SSM_P
_XBC_B = SSM_INNER // (SSM_N * SSM_GPS)
_XBC_C = _XBC_B + SSM_G // SSM_GPS


def _ssd_steps(st, x, bm, cm, cum, cum_t, rev):
    outs = [_ssd_step(st[j * SSM_N:(j + 1) * SSM_N], x[:, j * _W:(j + 1) * _W], bm[:, j * SSM_N:(j + 1) * SSM_N],
                      cm[:, j * SSM_N:(j + 1) * SSM_N], cum[:, j * _W:(j + 1) * _W],
                      cum_t[j * SSM_HPG:(j + 1) * SSM_HPG], rev) for j in range(SSM_GPS)]
    return jnp.concatenate([o[0] for o in outs], axis=0), jnp.concatenate([o[1] for o in outs], axis=1)


def _ssd_fwd(xd, xbc, cum, cum_t, ncc, rev, name, hosted=()):
    A = xd.shape[0]
    nc = A // SSM_C
    ng = SSM_G // SSM_GPS
    ch = lambda s: _chunk_of(s, nc, ncc, rev)

    def body(x_ref, b_ref, c_ref, a_ref, at_ref, y_ref, ss_ref, st_ref):
        @pl.when(pl.program_id(1) == 0)
        def _():
            st_ref[...] = jnp.zeros_like(st_ref)

        st = st_ref[...]
        ss_ref[0, 0] = st
        st_new, y = _ssd_steps(st, x_ref[...].astype(F32), b_ref[...].astype(F32), c_ref[...].astype(F32), a_ref[...],
                               at_ref[...], rev)
        y_ref[...] = y.astype(y_ref.dtype)
        st_ref[...] = st_new

    gs = pl.BlockSpec((SSM_C, _W * SSM_GPS), lambda g, s: (ch(s), g))
    return _call(
        body, grid=(ng, nc),
        in_specs=[gs, pl.BlockSpec((SSM_C, SSM_N * SSM_GPS), lambda g, s: (ch(s), _XBC_B + g)),
                  pl.BlockSpec((SSM_C, SSM_N * SSM_GPS), lambda g, s: (ch(s), _XBC_C + g)), gs,
                  pl.BlockSpec((SSM_HPG * SSM_GPS, SSM_C), lambda g, s: (g, ch(s)))],
        out_specs=[gs, pl.BlockSpec((1, 1, SSM_N * SSM_GPS, _W), lambda g, s: (g, s, 0, 0))],
        out_shape=[jax.ShapeDtypeStruct((A, SSM_INNER), BF16), jax.ShapeDtypeStruct((ng, nc, SSM_N * SSM_GPS, _W), F32)],
        scratch=[pltpu.VMEM((SSM_N * SSM_GPS, _W), F32)],
        sem=("parallel", "arbitrary"), name=name, args=(xd, xbc, xbc, cum, cum_t), hosted=hosted)


def _ssd_bwd(xd, xbc, cum, cum_t, saved, d_y, prev, ncc, rev, name):
    A = xd.shape[0]
    nc = A // SSM_C
    ng = SSM_G // SSM_GPS
    st_of = lambda r: nc - 1 - r
    ch = lambda r: _chunk_of(st_of(r), nc, ncc, rev)
    gs = pl.BlockSpec((SSM_C, _W * SSM_GPS), lambda g, r: (ch(r), g))
    ns = pl.BlockSpec((SSM_C, SSM_N * SSM_GPS), lambda g, r: (ch(r), g))
    ts = pl.BlockSpec((SSM_HPG * SSM_GPS, SSM_C), lambda g, r: (g, ch(r)))
    n_prev = 0 if prev is None else 2

    def body(*refs):
        x_ref, b_ref, c_ref, a_ref, at_ref, ss_ref, dy_ref = refs[:7]
        p_refs = refs[7:7 + n_prev]
        dx_ref, db_ref, dc_ref, da_ref, dat_ref, ds_ref = refs[7 + n_prev:]

        @pl.when(pl.program_id(1) == 0)
        def _():
            ds_ref[...] = jnp.zeros_like(ds_ref)

        _, vjp = jax.vjp(functools.partial(_ssd_steps, rev=rev), ss_ref[0, 0], x_ref[...].astype(F32), b_ref[...].astype(F32),
                         c_ref[...].astype(F32), a_ref[...], at_ref[...])
        ds, dx, db, dc, da, dat = vjp((ds_ref[...], dy_ref[...].astype(F32)))
        if n_prev:
            db, dc = db + p_refs[0][...].astype(F32), dc + p_refs[1][...].astype(F32)
        for r, d in zip((dx_ref, db_ref, dc_ref, da_ref, dat_ref), (dx, db, dc, da, dat)):
            r[...] = d.astype(r.dtype)
        ds_ref[...] = ds

    in_specs = [gs, pl.BlockSpec((SSM_C, SSM_N * SSM_GPS), lambda g, r: (ch(r), _XBC_B + g)),
                pl.BlockSpec((SSM_C, SSM_N * SSM_GPS), lambda g, r: (ch(r), _XBC_C + g)), gs, ts,
                pl.BlockSpec((1, 1, SSM_N * SSM_GPS, _W), lambda g, r: (g, st_of(r), 0, 0)), gs] + ([ns, ns] if n_prev else [])
    return pl.pallas_call(
        body, grid=(ng, nc), in_specs=in_specs, out_specs=[gs, ns, ns, gs, ts],
        out_shape=[jax.ShapeDtypeStruct((A, SSM_INNER), BF16), jax.ShapeDtypeStruct((A, SSM_G * SSM_N), BF16),
                   jax.ShapeDtypeStruct((A, SSM_G * SSM_N), BF16), jax.ShapeDtypeStruct((A, SSM_INNER), F32),
                   jax.ShapeDtypeStruct((SSM_G * SSM_HPG, A), F32)],
        scratch_shapes=[pltpu.VMEM((SSM_N * SSM_GPS, _W), F32)],
        compiler_params=_cparams(("parallel", "arbitrary")), name=name,
    )(xd, xbc, xbc, cum, cum_t, saved, d_y, *(prev or ()))


_ADA_ROWS = 16


def _ada_fwd(cc, w_shard, b_shard):
    n = w_shard.shape[1]

    def body(cc_ref, w_ref, b_ref, o_ref):
        o_ref[...] = _dot_impl(_silu(cc_ref[...]), w_ref[...], False, False) + b_ref[...]

    return pl.pallas_call(body, out_shape=jax.ShapeDtypeStruct((_ADA_ROWS, n), F32),
                          compiler_params=_cparams(()), name="ada_fwd")(cc, w_shard, b_shard)


def _ada_bwd(cc, w_shard, d_all, d_mine):
    n = w_shard.shape[1]

    def rows(ref, r):
        parts = []
        for s in range(N_DEV):
            parts.append(ref[6 * s + r:6 * s + r + 1] + ref[6 * s + 2 + r:6 * s + 3 + r] + ref[6 * s + 4 + r:6 * s + 5 + r])
        return parts

    def total(parts):
        t = parts[0]
        for p in parts[1:]:
            t = t + p
        return t

    def body(cc_ref, w_ref, da_ref, dm_ref, dw_ref, db_ref, dcc_ref):
        dd = jnp.concatenate(rows(dm_ref, 0) + [total(rows(dm_ref, 1)), jnp.zeros((_ADA_ROWS - N_DEV - 1, n), F32)], axis=0)
        cc = cc_ref[...]
        dw_ref[...] = _dot_impl(_silu(cc), dd, True, False)
        db_ref[...] = total(rows(da_ref, 0)) + total(rows(da_ref, 1))
        _, vjp = jax.vjp(_silu, cc)
        dcc_ref[...] = vjp(_dot_impl(dd, w_ref[...], False, True))[0]

    return pl.pallas_call(
        body, out_shape=[jax.ShapeDtypeStruct((D, n), F32), jax.ShapeDtypeStruct((1, d_all.shape[1]), F32),
                         jax.ShapeDtypeStruct((_ADA_ROWS, D), F32)],
        compiler_params=_cparams(()), name="ada_bwd")(cc, w_shard, d_all, d_mine)


def _local_step(x, ctx, target, ada, w, shards):
    T, Tc = x.shape[0], ctx.shape[0]
    assert Tc == TM and T % TM == 0 and GRID_W == GLA_C and TS == SSM_C == 2 * GLA_C
    A = T + Tc
    nct = Tc // TM
    n_tm, n_ts = A // TM, A // TS
    g = {}

    x_in = [_ctx(ctx, TM, nct), _lat(x, TM, nct)]
    f_norm1 = _f_norm1(nct)
    p_norm1 = [w["norm1_w"], ada]
    (h1,) = _stage_fwd(f_norm1, n_tm, x_in, p_norm1, [(A, D, BF16, TM)], "norm1")
    proj, ag_pa, ag_pb, ag_out = _mm(h1, w["w_in"], trans_b=False, out_dtype=BF16, name="mm_in", tn_t=1152,
                                     hosted=[(shards[n], False) for n in ("w_pa", "w_pb", "w_out")])
    small = _mm(h1, w["w_in"][:, C_S:], trans_b=False, out_dtype=F32, name="mm_in_small")

    p_gd = [w["up_f"], w["up_b"], w["gla_bias_f"], w["gla_bias_b"], w["dtb"], w["alog"]]
    f_conv = _f_conv(nct, Tc)
    p_conv = [w["conv_w"], w["conv_b"]]
    in_conv = [_col(proj, TM, 3072, C_U // 3072)]
    (xbc,) = _stage_fwd(f_conv, n_tm, in_conv, p_conv, [(A, 3072, BF16, TM)], "conv")
    in_gd = [_col(small, TS), _col(xbc, TS, SSM_INNER, 0)]
    n_heads = SSM_G * SSM_HPG
    lg_f, lg_b, xf, xb, axf, axb, atf, atb = _stage_fwd(
        _f_gates_dt, n_ts, in_gd, p_gd,
        [(A, 512, F32, TS)] * 2 + [(A, SSM_INNER, BF16, TS)] * 2 + [(A, SSM_INNER, F32, TS)] * 2
        + [(A, n_heads, F32, TS, True)] * 2,
        "gates_dt")

    ncc_g, ncc_s = Tc // GLA_C, Tc // SSM_C
    ogf, sv_gf = _gla_fwd(proj, lg_f, ncc_g, False, "gla_f")
    ogb, sv_gb = _gla_fwd(proj, lg_b, ncc_g, True, "gla_b")
    ysf, sv_sf, ag_gate, ag_up = _ssd_fwd(
        xf, xbc, axf, atf, ncc_s, False, "ssd_f", hosted=[(shards[n], False) for n in ("w_gate", "w_up")])
    ysb, sv_sb, ag_down = _ssd_fwd(xb, xbc, axb, atb, ncc_s, True, "ssd_b", hosted=[(shards["w_down"], False)])
    w = dict(w, w_pa=ag_pa.reshape(D, D), w_pb=ag_pb.reshape(SSM_INNER, D), w_out=ag_out.reshape(D, D),
             w_gu=jnp.concatenate([_unshard_cols(ag_gate), _unshard_cols(ag_up)], axis=1), w_down=ag_down.reshape(D_FF, D))
    got = {}

    in_gp = [_col(ogf, TM), _col(ogb, TM), _col(proj, TM, 1024, C_R // 1024)]
    p_gp = [w["gla_norm_w"]]
    (oa,) = _stage_fwd(_f_gla_post, n_tm, in_gp, p_gp, [(A, D, BF16, TM)], "gla_post")
    in_sp = [_col(ysf, TM), _col(ysb, TM), _col(xbc, TM, SSM_INNER, 0), _col(proj, TM, SSM_INNER, C_Z // SSM_INNER)]
    p_sp = [w["dsk8"], w["ssm_norm_w"]]
    (ob,) = _stage_fwd(_f_ssd_post, n_tm, in_sp, p_sp, [(A, SSM_INNER, BF16, TM)], "ssd_post")
    ya = _mm(oa, w["w_pa"], trans_b=False, out_dtype=BF16, name="mm_pa")
    yb = _mm(ob, w["w_pb"], trans_b=False, out_dtype=BF16, name="mm_pb")
    in_mg = [_col(proj, TM, 1024, C_GA // 1024), _col(proj, TM, 1024, C_GB // 1024), _col(ya, TM), _col(yb, TM)]
    (merged,) = _stage_fwd(_f_merge, n_tm, in_mg, [], [(A, D, BF16, TM)], "merge")
    mix = _mm(merged, w["w_out"], trans_b=False, out_dtype=BF16, name="mm_out")

    f_res1 = _f_res1(nct)
    in_r1 = x_in + [_col(mix, TM)]
    p_r1 = [ada, w["norm2_w"]]
    h2, hm2 = _stage_fwd(f_res1, n_tm, in_r1, p_r1, [(A, D, F32, TM), (A, D, BF16, TM)], "res1")
    gu = _mm(hm2, w["w_gu"], trans_b=False, out_dtype=BF16, name="mm_gu", tn_t=1408)
    in_sw = [_col(gu, TM)]
    (act,) = _stage_fwd(_f_swiglu, n_tm, in_sw, [], [(A, D_FF, BF16, TM)], "swiglu")
    dn = _mm(act, w["w_down"], trans_b=False, out_dtype=BF16, name="mm_down")

    loss_blk, d_h2a, d_dn, d_ada3, g["final_norm_w"] = _loss_head(h2, dn, target, ada, w["final_norm_w"], nct)
    rows16 = lambda gw: (gw.reshape(N_DEV, gw.shape[0] // N_DEV, gw.shape[1]).astype(BF16), True)
    cols16 = lambda gw: (_shard_cols(gw).astype(BF16), True)
    gw_down = _mm_tn(act, d_dn, name="dw_down", tr_t=1408)
    d_act = _mm(d_dn, w["w_down"], trans_b=True, out_dtype=BF16, name="dx_down", tn_t=1408)
    d_gu, got["w_down"] = _stage_bwd(_f_swiglu, n_tm, in_sw, [], [_col(d_act, TM)], lambda i, t: t, [(BF16, None)],
                                     "swiglu_b", hosted=[rows16(gw_down)])
    gw_gu = _mm_tn(hm2, d_gu, name="dw_gu", tm_t=2816)
    d_hm2 = _mm(d_gu, w["w_gu"], trans_b=True, out_dtype=BF16, name="dx_gu")
    d_x1, d_mix, d_ada2, g["norm2_w"], got["w_gate"] = _stage_bwd(
        f_res1, n_tm, in_r1, p_r1, [_col(d_h2a, TM), _col(d_hm2, TM)], lambda i, t: t,
        [None, (F32, None), (BF16, None)], "res1_b", hosted=[cols16(gw_gu[:, :D_FF])])
    gw_out = _mm_tn(merged, d_mix, name="dw_out")
    d_merged = _mm(d_mix, w["w_out"], trans_b=True, out_dtype=BF16, name="dx_out")
    d_ga, d_gb, d_ya, d_yb, got["w_up"] = _stage_bwd(
        _f_merge, n_tm, in_mg, [], [_col(d_merged, TM)], lambda i, t: t, [(BF16, None)] * 4, "merge_b",
        hosted=[cols16(gw_gu[:, D_FF:])])
    gw_pa = _mm_tn(oa, d_ya, name="dw_pa")
    gw_pb = _mm_tn(ob, d_yb, name="dw_pb")
    d_oa = _mm(d_ya, w["w_pa"], trans_b=True, out_dtype=BF16, name="dx_pa")
    d_ob = _mm(d_yb, w["w_pb"], trans_b=True, out_dtype=BF16, name="dx_pb")
    d_og, d_r, g["gla_norm_w"], got["w_out"] = _stage_bwd(
        _f_gla_post, n_tm, in_gp, p_gp, [_col(d_oa, TM)], lambda i, t: t, [(BF16, None), None, (BF16, None)], "gla_post_b",
        hosted=[rows16(gw_out)])
    d_ys, d_xs_skip, d_z, g["dsk8"], g["ssm_norm_w"], got["w_pa"], got["w_pb"] = _stage_bwd(
        _f_ssd_post, n_tm, in_sp, p_sp, [_col(d_ob, TM)], lambda i, t: t,
        [(BF16, None), None, (BF16, None), (BF16, None)], "ssd_post_b", hosted=[rows16(gw_pa), rows16(gw_pb)])

    dq, dk, dv, d_lgf = _gla_bwd(proj, lg_f, sv_gf, d_og, None, ncc_g, False, "gla_f_b")
    dq, dk, dv, d_lgb = _gla_bwd(proj, lg_b, sv_gb, d_og, (dq, dk, dv), ncc_g, True, "gla_b_b")
    d_xf, d_bm, d_cm, d_axf, d_atf = _ssd_bwd(xf, xbc, axf, atf, sv_sf, d_ys, None, ncc_s, False, "ssd_f_b")
    d_xb, d_bm, d_cm, d_axb, d_atb = _ssd_bwd(xb, xbc, axb, atb, sv_sb, d_ys, (d_bm, d_cm), ncc_s, True, "ssd_b_b")

    cts_gd = [_col(a, TS) for a in (d_lgf, d_lgb, d_xf, d_xb, d_axf, d_axb)]
    cts_gd += [(a, (n_heads, TS), lambda i: (0, i)) for a in (d_atf, d_atb)]
    d_small, d_xs_dt, g["up_f"], g["up_b"], g["gla_bias_f"], g["gla_bias_b"], g["dtb"], g["alog"] = _stage_bwd(
        _f_gates_dt, n_ts, in_gd, p_gd, cts_gd, lambda i, t: t, [(BF16, None), (BF16, None)], "gates_dt_b")
    cts_conv = [_col(d_xs_skip, TM), _col(d_xs_dt, TM), _col(d_bm, TM), _col(d_cm, TM)]
    d_u, g["conv_w"], g["conv_b"] = _stage_bwd(
        f_conv, n_tm, in_conv, p_conv, cts_conv,
        lambda i, t: [jnp.concatenate([t[0] + t[1], t[2], t[3]], axis=1)], [(BF16, None)], "conv_b")

    d_proj = jnp.concatenate([dq, dk, dv, d_r, d_u, d_z, d_ga, d_gb, d_small], axis=1)
    gw_in = _mm_tn(h1, d_proj, name="dw_in", tn_t=1152, tm_t=2816)
    d_h1, got["w_in"] = _mm(d_proj, w["w_in"], trans_b=True, out_dtype=BF16, name="dx_in", tm_t=768, tk_t=3456,
                            hosted=[cols16(_from_padded(gw_in))])
    grad_x, g["norm1_w"], d_ada1 = _stage_bwd(
        f_norm1, n_tm, x_in, p_norm1, [_col(d_h1, TM)], lambda i, t: t,
        [None, (F32, (d_x1, (TM, D), x_in[1][2]))], "norm1_b")
    return loss_blk, grad_x, g, got, (d_ada1, d_ada2, d_ada3)


def _exchange(x, a2a, name):
    shp = x.shape[1:] if a2a else x.shape

    def body(x_ref, o_ref, send_sems, recv_sems, local_sem):
        ops = _exchange_ops(x_ref, o_ref, send_sems, recv_sems, local_sem, a2a)
        for op in ops:
            op.start()
        for op in ops:
            op.wait()

    return pl.pallas_call(
        body, out_shape=jax.ShapeDtypeStruct((N_DEV,) + tuple(shp), x.dtype),
        in_specs=[pl.BlockSpec(memory_space=pl.ANY)], out_specs=pl.BlockSpec(memory_space=pl.ANY),
        scratch_shapes=[pltpu.SemaphoreType.DMA((N_DEV - 1,)), pltpu.SemaphoreType.DMA((N_DEV - 1,)),
                        pltpu.SemaphoreType.DMA(())],
        name=name,
    )(x)


def _all_gather(x, name):
    def body(x_ref, o_ref, send_sems, recv_sems, local_sem):
        mx, my, mc = lax.axis_index("x"), lax.axis_index("y"), lax.axis_index("c")
        me, sibling = (mx, my, mc), (mx, my, 1 - mc)
        chips = [(1 - mx, my), (mx, 1 - my), (1 - mx, 1 - my)]

        def slot(px, py, pc):
            return o_ref.at[4 * px + 2 * py + pc]

        def copy(k, block, to, src=None):
            return pltpu.make_async_remote_copy(
                src_ref=slot(*block) if src is None else src, dst_ref=slot(*block),
                send_sem=send_sems.at[k], recv_sem=recv_sems.at[k], device_id=to, device_id_type=pl.DeviceIdType.MESH)

        own = pltpu.make_async_copy(x_ref, slot(*me), local_sem)
        own.start()
        first = [copy(0, me, sibling, src=x_ref)] + [copy(1 + j, me, (*chip, mc), src=x_ref) for j, chip in enumerate(chips)]
        for cp in first:
            cp.start()
        passed = [copy(4 + j, (*chip, mc), sibling) for j, chip in enumerate(chips)]
        for j, chip in enumerate(chips):
            copy(1 + j, (*chip, mc), me).wait_recv()
            passed[j].start()
        copy(0, sibling, me).wait_recv()
        for j, chip in enumerate(chips):
            copy(4 + j, (*chip, 1 - mc), me).wait_recv()
        for cp in first + passed:
            cp.wait_send()
        own.wait()

    return pl.pallas_call(
        body, out_shape=jax.ShapeDtypeStruct((N_DEV,) + tuple(x.shape), x.dtype),
        in_specs=[pl.BlockSpec(memory_space=pl.ANY)], out_specs=pl.BlockSpec(memory_space=pl.ANY),
        scratch_shapes=[pltpu.SemaphoreType.DMA((N_DEV - 1,)), pltpu.SemaphoreType.DMA((N_DEV - 1,)),
                        pltpu.SemaphoreType.DMA(())],
        name=name,
    )(x)


def _adamw_math(w, gr, m, v):
    m = ADAM_B1 * m + (1.0 - ADAM_B1) * gr
    v = ADAM_B2 * v + (1.0 - ADAM_B2) * (gr * gr)
    m_hat = m / np.float32(1.0 - ADAM_B1 ** ADAM_STEP)
    v_hat = v / np.float32(1.0 - ADAM_B2 ** ADAM_STEP)
    delta = -ADAM_LR * (m_hat / (jnp.sqrt(v_hat) + ADAM_EPS) + ADAM_WD * w)
    return delta, m, v


def _sum_adamw(parts, w, m, v, name):
    R, C = w.shape
    n_parts = parts.shape[0]
    tr = _pick(R, max(16, (2 << 20) // (4 * C) // 16 * 16), 16)

    def body(p_ref, w_ref, m_ref, v_ref, g_ref, d_ref, mo_ref, vo_ref):
        gr = p_ref[0].astype(F32)
        for k in range(1, n_parts):
            gr = gr + p_ref[k].astype(F32)
        g_ref[...] = gr
        d_ref[...], mo_ref[...], vo_ref[...] = _adamw_math(w_ref[...], gr, m_ref[...], v_ref[...])

    row = pl.BlockSpec((tr, C), lambda i: (i, 0))
    return pl.pallas_call(
        body, grid=(R // tr,),
        in_specs=[pl.BlockSpec((n_parts, tr, C), lambda i: (0, i, 0)), row, row, row],
        out_specs=[row] * 4, out_shape=[jax.ShapeDtypeStruct((R, C), F32)] * 4,
        compiler_params=_cparams(("parallel",)), name=name,
    )(parts, w, m, v)


def _sum8(parts, name):
    _, R, C = parts.shape

    def body(p_ref, g_ref):
        gr = p_ref[0]
        for k in range(1, N_DEV):
            gr = gr + p_ref[k]
        g_ref[...] = gr

    return pl.pallas_call(body, out_shape=jax.ShapeDtypeStruct((R, C), F32), name=name)(parts)


def _adamw(w, gr, m, v, name):
    def body(w_ref, g_ref, m_ref, v_ref, d_ref, mo_ref, vo_ref):
        d_ref[...], mo_ref[...], vo_ref[...] = _adamw_math(w_ref[...], g_ref[...], m_ref[...], v_ref[...])

    return pl.pallas_call(body, out_shape=[jax.ShapeDtypeStruct(w.shape, F32)] * 3, name=name)(w, gr, m, v)


def _to_padded(w_in):
    z = jnp.zeros(w_in.shape[:-1] + (32,), w_in.dtype)
    return jnp.concatenate([w_in[..., 0:3072], w_in[..., 5152:8224], w_in[..., 3104:5152], w_in[..., 8288:10336],
                            w_in[..., 3072:3104], w_in[..., 8224:8288], z], axis=-1)


def _from_padded(p):
    return jnp.concatenate([p[..., 0:3072], p[..., 10240:10272], p[..., 6144:8192], p[..., 3072:6144],
                            p[..., 10272:10336], p[..., 8192:10240]], axis=-1)


def _unshard_cols(gathered):
    n, r, c = gathered.shape
    return jnp.transpose(gathered, (1, 0, 2)).reshape(r, n * c)


def _shard_cols(full):
    r, nc = full.shape
    return jnp.transpose(full.reshape(r, N_DEV, nc // N_DEV), (1, 0, 2))


def _lanes(vec, lo):
    return jnp.zeros((1, 128), F32).at[:, lo:lo + vec.shape[1]].set(vec)


_SMALL = (("b_ada", 6 * D), ("c_ctx", D), ("norm1_w", D), ("gla_bias_f", 512), ("gla_bias_b", 512), ("gla_norm_w", 256),
          ("conv_b", 3072), ("dt_bias_f", 32), ("dt_bias_b", 32), ("a_log_f", 32), ("a_log_b", 32), ("d_skip", 32),
          ("ssm_norm_w", 2048), ("norm2_w", D), ("final_norm_w", D))
_SHARDED_SMALL = (("gla_up_f", 16 * 512), ("gla_up_b", 16 * 512), ("conv_w", 4 * 3072))


def _pack(vals, names):
    flat = jnp.concatenate([vals[n].reshape(-1).astype(F32) for n, _ in names])
    pad = (-flat.shape[0]) % 1024
    return jnp.concatenate([flat, jnp.zeros((pad,), F32)]).reshape(-1, 128)


def _unpack(packed, names, shapes):
    flat, out, off = packed.reshape(-1), {}, 0
    for n, size in names:
        out[n] = flat[off:off + size].reshape(shapes[n])
        off += size
    return out


def kernel(x, c, ctx, c_ctx, w_ada, b_ada, norm1_w, w_in, gla_up_f, gla_bias_f, gla_up_b, gla_bias_b, gla_norm_w, conv_w, conv_b, dt_bias_f, dt_bias_b, a_log_f, a_log_b, d_skip, ssm_norm_w, w_pa, w_pb, w_out, norm2_w, w_gate, w_up, w_down, final_norm_w, loss_target, m_c_ctx, m_w_ada, m_b_ada, m_norm1_w, m_w_in, m_gla_up_f, m_gla_bias_f, m_gla_up_b, m_gla_bias_b, m_gla_norm_w, m_conv_w, m_conv_b, m_dt_bias_f, m_dt_bias_b, m_a_log_f, m_a_log_b, m_d_skip, m_ssm_norm_w, m_w_pa, m_w_pb, m_w_out, m_norm2_w, m_w_gate, m_w_up, m_w_down, m_final_norm_w, v_c_ctx, v_w_ada, v_b_ada, v_norm1_w, v_w_in, v_gla_up_f, v_gla_bias_f, v_gla_up_b, v_gla_bias_b, v_gla_norm_w, v_conv_w, v_conv_b, v_dt_bias_f, v_dt_bias_b, v_a_log_f, v_a_log_b, v_d_skip, v_ssm_norm_w, v_w_pa, v_w_pb, v_w_out, v_norm2_w, v_w_gate, v_w_up, v_w_down, v_final_norm_w):
    args = dict(locals())
    me = 4 * lax.axis_index("x") + 2 * lax.axis_index("y") + lax.axis_index("c")

    def gather16(a, name):
        return _all_gather(a[0].astype(BF16), name)

    full = {"w_in": _to_padded(_unshard_cols(gather16(w_in, "ag_w_in")))}
    shards = {n: args[n][0].astype(BF16) for n in ("w_pa", "w_pb", "w_out", "w_gate", "w_up", "w_down")}
    sm = _exchange(jnp.concatenate([gla_up_f.reshape(-1), gla_up_b.reshape(-1), conv_w.reshape(-1), c.reshape(-1)]).reshape(-1, 128),
                   False, "ag_small")
    sm = sm.reshape(N_DEV, -1)
    n_ada = w_ada.shape[2]
    cc = jnp.zeros((_ADA_ROWS, D), F32).at[0:N_DEV].set(sm[:, 3584:3584 + D]).at[N_DEV].set(c_ctx)
    w_ada16 = w_ada[0].astype(BF16)
    ada_cols = _exchange(_ada_fwd(cc, w_ada16, lax.dynamic_slice(b_ada, (0, me * n_ada), (1, n_ada))), False, "ag_ada")
    ada = jnp.zeros((8, N_DEV * n_ada), F32)
    ada = ada.at[0].set(lax.dynamic_index_in_dim(ada_cols, me, axis=1, keepdims=False).reshape(-1))
    ada = ada.at[1].set(ada_cols[:, N_DEV].reshape(-1))
    up_f = _unshard_cols(sm[:, 0:1024].reshape(N_DEV, 16, 64))
    up_b = _unshard_cols(sm[:, 1024:2048].reshape(N_DEV, 16, 64))
    full["conv_w"] = _unshard_cols(sm[:, 2048:3584].reshape(N_DEV, 4, 384))
    full["up_f"] = jnp.zeros((128, 512), F32).at[0:16].set(up_f)
    full["up_b"] = jnp.zeros((128, 512), F32).at[16:32].set(up_b)
    full["dtb"] = _lanes(dt_bias_f, 32) + _lanes(dt_bias_b, 64)
    full["alog"] = _lanes(a_log_f, 32) + _lanes(a_log_b, 64)
    full["dsk8"] = jnp.zeros((8, 128), F32).at[0:1, 0:32].set(d_skip)
    for n in ("norm1_w", "gla_bias_f", "gla_bias_b", "gla_norm_w", "conv_b", "ssm_norm_w", "norm2_w"):
        full[n] = args[n]
    full["final_norm_w"] = final_norm_w.reshape(1, D)

    loss_blk, grad_x, g, got, d_ada = _local_step(x[0], ctx[0], loss_target[0], ada, full, shards)

    d_all = _exchange(jnp.concatenate([d[0:2] for d in d_ada], axis=0), False, "ag_d_ada").reshape(N_DEV * 6, N_DEV * n_ada)
    gw_ada, gb_ada, d_cc = _ada_bwd(cc, w_ada16, d_all, lax.dynamic_slice(d_all, (0, me * n_ada), (N_DEV * 6, n_ada)))

    gs = dict(g, c_ctx=d_cc[N_DEV], b_ada=jnp.zeros_like(b_ada))
    gs["dt_bias_f"], gs["dt_bias_b"] = g["dtb"][:, 32:64], g["dtb"][:, 64:96]
    gs["a_log_f"], gs["a_log_b"] = g["alog"][:, 32:64], g["alog"][:, 64:96]
    gs["d_skip"] = g["dsk8"][0:1, 0:32]
    gs["gla_up_f"], gs["gla_up_b"] = g["up_f"][0:16], g["up_b"][16:32]
    gs["loss"] = loss_blk[0:1, 0:1]
    names = _SMALL + _SHARDED_SMALL + (("loss", 1),)
    shapes = {n: (args[n].shape if n in args else (1, 1)) for n, _ in names}
    shapes.update({"gla_up_f": (16, 512), "gla_up_b": (16, 512), "conv_w": (4, 3072)})
    red = _unpack(_sum8(_exchange(_pack(gs, names), False, "ag_small_grads"), "sum_small_grads"), names, shapes)
    loss = red["loss"].reshape(())
    grads = {n: red[n] for n, _ in _SMALL}
    grads["b_ada"] = gb_ada
    grads["gla_up_f"] = lax.dynamic_slice(red["gla_up_f"], (0, me * 64), (16, 64))[None]
    grads["gla_up_b"] = lax.dynamic_slice(red["gla_up_b"], (0, me * 64), (16, 64))[None]
    grads["conv_w"] = lax.dynamic_slice(red["conv_w"], (0, me * 384), (4, 384))[None]
    upd_names = tuple((n, s) for n, s in _SMALL) + (("gla_up_f", 1024), ("gla_up_b", 1024), ("conv_w", 1536))
    pk = lambda prefix, src: _pack({n: src[prefix + n] for n, _ in upd_names}, upd_names)
    d_s, m_s, v_s = _adamw(pk("", args), _pack(grads, upd_names), pk("m_", args), pk("v_", args), "adamw_small")
    upd_shapes = {n: args[n].shape for n, _ in upd_names}
    delta = _unpack(d_s, upd_names, upd_shapes)
    new_m = _unpack(m_s, upd_names, upd_shapes)
    new_v = _unpack(v_s, upd_names, upd_shapes)

    got["w_ada"] = gw_ada[None]
    for n, parts in got.items():
        gr, dl, mo, vo = _sum_adamw(parts, args[n][0], args["m_" + n][0], args["v_" + n][0], "adamw_" + n)
        grads[n], delta[n], new_m[n], new_v[n] = gr[None], dl[None], mo[None], vo[None]

    order = ["c_ctx", "w_ada", "b_ada", "norm1_w", "w_in", "gla_up_f", "gla_bias_f", "gla_up_b", "gla_bias_b", "gla_norm_w",
             "conv_w", "conv_b", "dt_bias_f", "dt_bias_b", "a_log_f", "a_log_b", "d_skip", "ssm_norm_w", "w_pa", "w_pb",
             "w_out", "norm2_w", "w_gate", "w_up", "w_down", "final_norm_w"]
    fix = lambda d: [d[n].reshape(args[n].shape) for n in order]
    return (loss, grad_x[None], *fix(grads), *fix(delta), *fix(new_m), *fix(new_v))
```

```python
import functools

import jax
import jax.numpy as jnp
import numpy as np
from jax import lax
from jax.experimental import pallas as pl
from jax.experimental.pallas import tpu as pltpu

F32 = jnp.float32
BF16 = jnp.bfloat16
N_DEV = 8
D = 1024
EPS = 1e-6
GRID_W = 64
GLA_H, GLA_DK, GLA_DV = 4, 128, 256
GLA_C = 64
GLA_TAU = 16.0
SSM_G, SSM_HPG, SSM_P, SSM_N = 4, 8, 64, 128
SSM_C = 128
SSM_INNER = 2048
D_FF = 2816
D_IN = 10336
D_INP = 10368
C_Q, C_K, C_V, C_R, C_U, C_Z, C_GA, C_GB, C_S = 0, 512, 1024, 2048, 3072, 6144, 8192, 9216, 10240
TM = 256
TS = 128

ADAM_LR, ADAM_B1, ADAM_B2, ADAM_EPS, ADAM_WD, ADAM_STEP = 0.001, 0.9, 0.999, 1e-08, 0.01, 10
VMEM_LIMIT = 56 << 20


def _cparams(sem):
    return pltpu.CompilerParams(dimension_semantics=sem or None, vmem_limit_bytes=VMEM_LIMIT)


def _pick(n, target, mult):
    best = None
    for t in range(mult, min(n, target) + 1, mult):
        if n % t == 0:
            best = t
    return best if best is not None else n


def _dot_impl(a, b, ta, tb):
    dims = (((0 if ta else 1,), (1 if tb else 0,)), ((), ()))
    return lax.dot_general(a.astype(BF16), b.astype(BF16), dims, preferred_element_type=F32)


@functools.partial(jax.custom_vjp, nondiff_argnums=(2, 3))
def _bdot(a, b, ta=False, tb=False):
    return _dot_impl(a, b, ta, tb)


def _bdot_fwd(a, b, ta, tb):
    return _dot_impl(a, b, ta, tb), (a, b)


def _bdot_bwd(ta, tb, res, g):
    a, b = res
    if not ta and not tb:
        return _dot_impl(g, b, False, True), _dot_impl(a, g, True, False)
    if not ta and tb:
        return _dot_impl(g, b, False, False), _dot_impl(g, a, True, False)
    if ta and not tb:
        return _dot_impl(b, g, False, True), _dot_impl(a, g, False, False)
    raise NotImplementedError


_bdot.defvjp(_bdot_fwd, _bdot_bwd)


_NN = (((1,), (0,)), ((), ()))
_NT = (((1,), (1,)), ((), ()))
_TN = (((0,), (0,)), ((), ()))


def _dg2(x, e, x_is_lhs, dims):
    hi = x.astype(BF16)
    lo = (x - hi.astype(F32)).astype(BF16)
    e = e.astype(BF16)
    if x_is_lhs:
        return (lax.dot_general(hi, e, dims, preferred_element_type=F32)
                + lax.dot_general(lo, e, dims, preferred_element_type=F32))
    return (lax.dot_general(e, hi, dims, preferred_element_type=F32)
            + lax.dot_general(e, lo, dims, preferred_element_type=F32))


_EDOT_FWD = {"xe": (True, _NN), "ex": (False, _NN), "ext": (False, _NT)}
_EDOT_BWD = {"xe": (True, _NT), "ex": (False, _TN), "ext": (True, _TN)}


@functools.partial(jax.custom_vjp, nondiff_argnums=(2,))
def _edot(x, e, mode):
    return _dg2(x, e, *_EDOT_FWD[mode])


def _edot_fwd(x, e, mode):
    return _dg2(x, e, *_EDOT_FWD[mode]), e


def _edot_bwd(mode, e, g):
    return _dg2(g, e, *_EDOT_BWD[mode]), None


_edot.defvjp(_edot_fwd, _edot_bwd)


def _shift_impl(u, pos, per, s):
    n = u.shape[0]
    rolled = u if s == 0 else pltpu.roll(u, (-s) % n, 0)
    ok = (pos + s >= 0) & (pos + s < per)
    return jnp.where(ok, rolled, 0.0)


@functools.partial(jax.custom_vjp, nondiff_argnums=(3,))
def _shift(u, pos, per, s):
    return _shift_impl(u, pos, per, s)


def _shift_fwd(u, pos, per, s):
    return _shift_impl(u, pos, per, s), (pos, per)


def _shift_bwd(s, res, g):
    pos, per = res
    return _shift_impl(g, pos, per, -s), None, None


_shift.defvjp(_shift_fwd, _shift_bwd)


def _rms(x, w):
    return x * lax.rsqrt(jnp.mean(x * x, axis=-1, keepdims=True) + EPS) * w


def _silu(x):
    return x * jax.nn.sigmoid(x)


def _softplus(x):
    return jnp.maximum(x, 0.0) + jnp.log(1.0 + jnp.exp(-jnp.abs(x)))


def _logsig(x):
    return jnp.minimum(x, 0.0) - jnp.log(1.0 + jnp.exp(-jnp.abs(x)))


def _tri(n, rev):
    t = lax.broadcasted_iota(jnp.int32, (n, n), 0)
    s = lax.broadcasted_iota(jnp.int32, (n, n), 1)
    return (s >= t) if rev else (t >= s)


def _head_expand(first_lane):
    l = lax.broadcasted_iota(jnp.int32, (128, SSM_INNER), 0)
    c = lax.broadcasted_iota(jnp.int32, (128, SSM_INNER), 1)
    return (l == first_lane + lax.shift_right_logical(c, 6)).astype(F32)


def _exchange_ops(x_ref, o_ref, send_sems, recv_sems, local_sem, a2a):
    mx, my, mc = lax.axis_index("x"), lax.axis_index("y"), lax.axis_index("c")
    me = 4 * mx + 2 * my + mc
    ops = []
    for k in range(1, N_DEV):
        px = 1 - mx if k & 4 else mx
        py = 1 - my if k & 2 else my
        pc = 1 - mc if k & 1 else mc
        ops.append(pltpu.make_async_remote_copy(
            src_ref=x_ref.at[4 * px + 2 * py + pc] if a2a else x_ref, dst_ref=o_ref.at[me],
            send_sem=send_sems.at[k - 1], recv_sem=recv_sems.at[k - 1],
            device_id=(px, py, pc), device_id_type=pl.DeviceIdType.MESH))
    ops.append(pltpu.make_async_copy(x_ref.at[me] if a2a else x_ref, o_ref.at[me], local_sem))
    return ops


def _call(body, *, grid, in_specs, out_specs, out_shape, scratch=(), sem, name, args, hosted=(), aliases=None):
    n_in, n_out, n_s, n_h = len(in_specs), len(out_specs), len(scratch), len(hosted)
    aliases = aliases or {}
    if not n_h:
        return pl.pallas_call(body, grid=grid, in_specs=list(in_specs), out_specs=list(out_specs),
                              out_shape=list(out_shape), scratch_shapes=list(scratch), input_output_aliases=aliases,
                              compiler_params=_cparams(sem), name=name)(*args)
    b0, c0 = n_in + n_h, n_in + n_h + n_out
    d0 = c0 + n_h

    def wrapped(*refs):
        hx, ho = refs[n_in:b0], refs[c0:d0]
        send_sems, recv_sems, local_sems = refs[d0 + n_s:]
        ids = [pl.program_id(ax) for ax in range(len(grid))]
        first = functools.reduce(jnp.logical_and, [i == 0 for i in ids])
        last = functools.reduce(jnp.logical_and, [i == n - 1 for i, n in zip(ids, grid)])

        def ops():
            return [op for n, (_, a2a) in enumerate(hosted)
                    for op in _exchange_ops(hx[n], ho[n], send_sems.at[n], recv_sems.at[n], local_sems.at[n], a2a)]

        @pl.when(first)
        def _():
            for op in ops():
                op.start()

        body(*refs[:n_in], *refs[b0:c0], *refs[d0:d0 + n_s])

        @pl.when(last)
        def _():
            for op in ops():
                op.wait()

    hbm = pl.BlockSpec(memory_space=pl.ANY)
    h_shapes = [jax.ShapeDtypeStruct((N_DEV,) + tuple(x.shape[1:] if a2a else x.shape), x.dtype) for x, a2a in hosted]
    return pl.pallas_call(
        wrapped, grid=grid, in_specs=list(in_specs) + [hbm] * n_h, out_specs=list(out_specs) + [hbm] * n_h,
        out_shape=list(out_shape) + h_shapes, input_output_aliases=aliases,
        scratch_shapes=list(scratch) + [pltpu.SemaphoreType.DMA((n_h, N_DEV - 1)), pltpu.SemaphoreType.DMA((n_h, N_DEV - 1)),
                                        pltpu.SemaphoreType.DMA((n_h,))],
        compiler_params=_cparams(("arbitrary",) * len(grid)), name=name,
    )(*args, *[x for x, _ in hosted])


def _mm(a, b, *, trans_b, out_dtype, name, tm_t=1408, tn_t=1024, tk_t=2816, hosted=()):
    M, K = a.shape
    N = b.shape[0] if trans_b else b.shape[1]
    tm, tn, tk = _pick(M, tm_t, 8), _pick(N, tn_t, 128), _pick(K, tk_t, 128)
    nk = K // tk
    dims = (((1,), (1,)), ((), ())) if trans_b else (((1,), (0,)), ((), ()))

    def body_one(a_ref, b_ref, o_ref):
        o_ref[...] = lax.dot_general(a_ref[...], b_ref[...], dims, preferred_element_type=F32).astype(out_dtype)

    def body_acc(a_ref, b_ref, o_ref, acc_ref):
        k = pl.program_id(2)
        p = lax.dot_general(a_ref[...], b_ref[...], dims, preferred_element_type=F32)

        @pl.when(k == 0)
        def _():
            acc_ref[...] = p

        @pl.when(k > 0)
        def _():
            acc_ref[...] += p

        @pl.when(k == nk - 1)
        def _():
            o_ref[...] = acc_ref[...].astype(out_dtype)

    b_spec = pl.BlockSpec((tn, tk), lambda j, i, k: (j, k)) if trans_b else pl.BlockSpec((tk, tn), lambda j, i, k: (k, j))
    outs = _call(
        body_one if nk == 1 else body_acc, grid=(N // tn, M // tm, nk),
        in_specs=[pl.BlockSpec((tm, tk), lambda j, i, k: (i, k)), b_spec],
        out_specs=[pl.BlockSpec((tm, tn), lambda j, i, k: (i, j))],
        out_shape=[jax.ShapeDtypeStruct((M, N), out_dtype)],
        scratch=[] if nk == 1 else [pltpu.VMEM((tm, tn), F32)],
        sem=("parallel", "parallel", "arbitrary"), name=name, args=(a, b), hosted=hosted)
    return outs if hosted else outs[0]


def _mm_tn(a, b, *, name, tm_t=2816, tr_t=1024, tn_t=1408):
    M, R = a.shape
    N = b.shape[1]
    tm, tr, tn = _pick(M, tm_t, 8), _pick(R, tr_t, 128), _pick(N, tn_t, 128)

    def body(a_ref, b_ref, o_ref):
        m = pl.program_id(2)
        p = lax.dot_general(a_ref[...], b_ref[...], (((0,), (0,)), ((), ())), preferred_element_type=F32)

        @pl.when(m == 0)
        def _():
            o_ref[...] = p

        @pl.when(m > 0)
        def _():
            o_ref[...] += p

    return pl.pallas_call(
        body, grid=(R // tr, N // tn, M // tm),
        in_specs=[pl.BlockSpec((tm, tr), lambda r, j, m: (m, r)), pl.BlockSpec((tm, tn), lambda r, j, m: (m, j))],
        out_specs=pl.BlockSpec((tr, tn), lambda r, j, m: (r, j)),
        out_shape=jax.ShapeDtypeStruct((R, N), F32),
        compiler_params=_cparams(("parallel", "parallel", "arbitrary")), name=name,
    )(a, b)


def _col(arr, tm, width=None, col=0):
    width = arr.shape[1] if width is None else width
    return (arr, (tm, width), lambda i: (i, col))


def _lat(arr, tm, nct):
    return (arr, (tm, arr.shape[1]), lambda i: (jnp.maximum(i - nct, 0), 0))


def _ctx(arr, tm, nct):
    return (arr, (tm, arr.shape[1]), lambda i: (jnp.minimum(i, nct - 1), 0))


def _whole(p):
    return pl.BlockSpec(p.shape, lambda i, nd=p.ndim: (0,) * nd)


def _stage_fwd(f, n_tiles, ins, params, outs, name):
    ni, npar = len(ins), len(params)
    o_specs = [pl.BlockSpec((o[1], o[3]), lambda i: (0, i)) if len(o) > 4 else pl.BlockSpec((o[3], o[1]), lambda i: (i, 0))
               for o in outs]
    o_shapes = [jax.ShapeDtypeStruct((o[1], o[0]) if len(o) > 4 else (o[0], o[1]), o[2]) for o in outs]

    def body(*refs):
        i = pl.program_id(0)
        xs = [r[...].astype(F32) for r in refs[:ni]]
        ps = [r[...] for r in refs[ni:ni + npar]]
        for r, v in zip(refs[ni + npar:], f(i, xs, ps, False)):
            r[...] = v.astype(r.dtype)

    return pl.pallas_call(
        body, grid=(n_tiles,),
        in_specs=[pl.BlockSpec(bs, fn) for _, bs, fn in ins] + [_whole(p) for p in params],
        out_specs=o_specs, out_shape=o_shapes,
        compiler_params=_cparams(("parallel",)), name=name,
    )(*[a for a, _, _ in ins], *params)


def _stage_bwd(f, n_tiles, ins, params, cts, ct_fn, dins, name, hosted=(), pack=None):
    ni, npar, nc = len(ins), len(params), len(cts)
    want = [k for k, d in enumerate(dins) if d is not None]
    extras = [dins[k][1] for k in want if dins[k][1] is not None]
    n_buf = 0 if pack is None or pack[1] is None else 1
    n_main_in = ni + npar + nc + len(extras)

    def body(*refs):
        i = pl.program_id(0)
        xs = [r[...].astype(F32) for r in refs[:ni]]
        ps = [r[...] for r in refs[ni:ni + npar]]
        ct_tiles = [r[...].astype(F32) for r in refs[ni + npar:ni + npar + nc]]
        ex_refs = list(refs[ni + npar + nc:n_main_in])
        out_refs = refs[n_main_in + n_buf:]
        _, vjp = jax.vjp(lambda xs_, ps_: tuple(f(i, xs_, ps_, True)), xs, ps)
        dxs, dps = vjp(tuple(ct_fn(i, ct_tiles)))
        for n, k in enumerate(want):
            v = dxs[k]
            if dins[k][1] is not None:
                v = v + ex_refs.pop(0)[...].astype(F32)
            out_refs[n][...] = v.astype(out_refs[n].dtype)
        for r, v in zip(out_refs[len(want):len(want) + npar], dps):
            @pl.when(i == 0)
            def _(r=r, v=v):
                r[...] = v

            @pl.when(i > 0)
            def _(r=r, v=v):
                r[...] += v
        if pack is not None:
            packed = [dxs[k] for k in pack[0]]
            r = out_refs[len(want) + npar]
            r[...] = (packed[0] if len(packed) == 1 else jnp.concatenate(packed, axis=1)).astype(r.dtype)

    in_specs = ([pl.BlockSpec(bs, fn) for _, bs, fn in ins] + [_whole(p) for p in params]
                + [pl.BlockSpec(bs, fn) for _, bs, fn in cts] + [pl.BlockSpec(bs, fn) for _, bs, fn in extras])
    out_specs = ([pl.BlockSpec(ins[k][1], lambda i, fn=ins[k][2]: (fn(i)[0], 0)) for k in want] + [_whole(p) for p in params])
    out_shape = ([jax.ShapeDtypeStruct((ins[k][0].shape[0], ins[k][1][1]), dins[k][0]) for k in want]
                 + [jax.ShapeDtypeStruct(p.shape, F32) for p in params])
    args = [*[a for a, _, _ in ins], *params, *[a for a, _, _ in cts], *[a for a, _, _ in extras]]
    aliases = None
    if pack is not None:
        ks, buf, total_cols, block = pack
        width = sum(ins[k][1][1] for k in ks)
        out_specs.append(pl.BlockSpec((ins[ks[0]][1][0], width), lambda i, fn=ins[ks[0]][2]: (fn(i)[0], block)))
        out_shape.append(jax.ShapeDtypeStruct((ins[ks[0]][0].shape[0], total_cols), BF16))
        if buf is not None:
            in_specs.append(pl.BlockSpec(memory_space=pl.ANY))
            args.append(buf)
            aliases = {len(args) - 1: len(out_specs) - 1}
    return _call(body, grid=(n_tiles,), in_specs=in_specs, out_specs=out_specs, out_shape=out_shape,
                 sem=("arbitrary",), name=name, args=args, hosted=hosted, aliases=aliases)


def _mod_rows(i, nct, ada, lo):
    return jnp.where(i < nct, ada[1:2, lo:lo + D], ada[0:1, lo:lo + D])


def _f_norm1(nct):
    def f(i, xs, ps, diff):
        ctx, x = xs
        nw, ada = ps
        xt = jnp.where(i < nct, ctx, x)
        return (_rms(xt, nw) * (1.0 + _mod_rows(i, nct, ada, D)) + _mod_rows(i, nct, ada, 0),)
    return f


def _f_gates_dt(i, xs, ps, diff):
    small, xc = xs
    up_f, up_b, gb_f, gb_b, dtb, alog = ps
    dot = _bdot if diff else _dot_impl_nn
    lg_f = _logsig(dot(small, up_f) + gb_f) * (1.0 / GLA_TAU)
    lg_b = _logsig(dot(small, up_b) + gb_b) * (1.0 / GLA_TAU)
    dtp = _softplus(small + dtb)
    aa = -jnp.exp(alog) * dtp
    e_f, e_b = _head_expand(32), _head_expand(64)
    n = small.shape[0]
    t = lax.broadcasted_iota(jnp.int32, (n, n), 0)
    s = lax.broadcasted_iota(jnp.int32, (n, n), 1)
    same = lax.shift_right_logical(t, 6) == lax.shift_right_logical(s, 6)
    pre_g, suf_g = ((t >= s) & same).astype(F32), ((s >= t) & same).astype(F32)
    pre_s, suf_s = (t >= s).astype(F32), (s >= t).astype(F32)
    cum_f, cum_b = _edot(aa, pre_s, "ex"), _edot(aa, suf_s, "ex")
    h = lax.broadcasted_iota(jnp.int32, (SSM_G * SSM_HPG, 128), 0)
    l = lax.broadcasted_iota(jnp.int32, (SSM_G * SSM_HPG, 128), 1)
    cum_ft = _edot(cum_f, (l == h + 32).astype(F32), "ext")
    cum_bt = _edot(cum_b, (l == h + 64).astype(F32), "ext")
    return (_edot(lg_f, pre_g, "ex"), _edot(lg_b, suf_g, "ex"), xc * _edot(dtp, e_f, "xe"), xc * _edot(dtp, e_b, "xe"),
            _edot(cum_f, e_f, "xe"), _edot(cum_b, e_b, "xe"), cum_ft, cum_bt)


def _dot_impl_nn(a, b):
    return _dot_impl(a, b, False, False)


def _f_conv(nct, tc):
    def f(i, xs, ps, diff):
        (u,) = xs
        cw, cb = ps
        n = u.shape[0]
        t = lax.broadcasted_iota(jnp.int32, (n, 1), 0)
        per = jnp.where(i < nct, tc, GRID_W)
        pos = jnp.bitwise_and(t, per - 1)
        sh = _shift if diff else _shift_impl
        acc = cb + sh(u, pos, per, -2) * cw[0:1]
        for j in range(1, 4):
            acc = acc + sh(u, pos, per, j - 2) * cw[j:j + 1]
        return (_silu(acc),)
    return f


def _f_gla_post(i, xs, ps, diff):
    ogf, ogb, r = xs
    (gw,) = ps
    o = ogf + ogb
    parts = [_rms(o[:, h * GLA_DV:(h + 1) * GLA_DV], gw) for h in range(GLA_H)]
    return (jnp.concatenate(parts, axis=1) * _silu(r),)


def _f_ssd_post(i, xs, ps, diff):
    yf, yb, xc, z = xs
    dsk8, nw = ps
    dsk = _edot(dsk8, _head_expand(0), "xe")[0:1]
    y = (yf + yb + dsk * xc) * _silu(z)
    w = SSM_INNER // SSM_G
    parts = [_rms(y[:, g * w:(g + 1) * w], nw[:, g * w:(g + 1) * w]) for g in range(SSM_G)]
    return (jnp.concatenate(parts, axis=1),)


def _f_merge(i, xs, ps, diff):
    ga, gb, ya, yb = xs
    return (jax.nn.sigmoid(ga) * ya + jax.nn.sigmoid(gb) * yb,)


def _f_res1(nct):
    def f(i, xs, ps, diff):
        ctx, x, mix = xs
        ada, nw = ps
        h2 = jnp.where(i < nct, ctx, x) + _mod_rows(i, nct, ada, 2 * D) * mix
        return (h2, _rms(h2, nw) * (1.0 + _mod_rows(i, nct, ada, 4 * D)) + _mod_rows(i, nct, ada, 3 * D))
    return f


def _f_swiglu(i, xs, ps, diff):
    (gu,) = xs
    return (_silu(gu[:, :D_FF]) * gu[:, D_FF:],)


def _loss_head(h2, dn, tgt, ada, fw, nct):
    A = h2.shape[0]
    n_tiles = A // TM

    def tile_loss(i, h2t, dnt, tg, ada_, fw_):
        h3 = h2t + ada_[0:1, 5 * D:6 * D] * dnt
        err = _rms(h3, fw_) - tg
        row = 0.5 * jnp.mean(err * err, axis=-1, keepdims=True)
        return jnp.sum(row, axis=0, keepdims=True) * jnp.where(i < nct, 0.0, 1.0)

    def body(h2_ref, dn_ref, tg_ref, ada_ref, fw_ref, loss_ref, dh_ref, ddn_ref, dada_ref, dfw_ref):
        i = pl.program_id(0)
        val, vjp = jax.vjp(functools.partial(tile_loss, i), h2_ref[...], dn_ref[...].astype(F32), tg_ref[...], ada_ref[...],
                           fw_ref[...])
        dh, ddn, _, dada, dfw = vjp(jnp.ones((1, 1), F32))
        dh_ref[...] = dh
        ddn_ref[...] = ddn.astype(BF16)
        lv = jnp.broadcast_to(val, loss_ref.shape)
        for r, v in ((loss_ref, lv), (dada_ref, dada), (dfw_ref, dfw)):
            @pl.when(i == 0)
            def _(r=r, v=v):
                r[...] = v

            @pl.when(i > 0)
            def _(r=r, v=v):
                r[...] += v

    row = lambda i: (i, 0)
    return pl.pallas_call(
        body, grid=(n_tiles,),
        in_specs=[pl.BlockSpec((TM, D), row), pl.BlockSpec((TM, D), row),
                  pl.BlockSpec((TM, D), lambda i: (jnp.maximum(i - nct, 0), 0)), _whole(ada), _whole(fw)],
        out_specs=[pl.BlockSpec((8, 128), lambda i: (0, 0)), pl.BlockSpec((TM, D), row), pl.BlockSpec((TM, D), row),
                   _whole(ada), _whole(fw)],
        out_shape=[jax.ShapeDtypeStruct((8, 128), F32), jax.ShapeDtypeStruct((A, D), F32),
                   jax.ShapeDtypeStruct((A, D), BF16), jax.ShapeDtypeStruct(ada.shape, F32),
                   jax.ShapeDtypeStruct(fw.shape, F32)],
        compiler_params=_cparams(("arbitrary",)), name="loss_head",
    )(h2, dn, tgt, ada, fw)


def _chunk_of(step, n_chunks, n_ctx_chunks, rev):
    if not rev:
        return step
    return jnp.where(step < n_ctx_chunks, n_ctx_chunks - 1 - step, n_chunks - 1 - (step - n_ctx_chunks))


def _gla_step(st, q, k, v, b, rev):
    tri = _tri(GLA_C, rev).astype(F32)
    last = 0 if rev else GLA_C - 1
    outs, sts = [], []
    for h in range(GLA_H):
        kk = slice(h * GLA_DK, (h + 1) * GLA_DK)
        vv = slice(h * GLA_DV, (h + 1) * GLA_DV)
        qh, kh, vh, bh, sh = q[:, kk] * (GLA_DK ** -0.5), k[:, kk], v[:, vv], b[:, kk], st[vv]
        tot = bh[last:last + 1]
        mid = bh[GLA_C // 2:GLA_C // 2 + 1]
        att = _bdot(qh * jnp.exp(bh - mid), kh * jnp.exp(mid - bh), False, True) * tri
        outs.append(_bdot(att, vh) + _bdot(qh * jnp.exp(bh), sh, False, True))
        sts.append(sh * jnp.exp(tot) + _bdot(vh, kh * jnp.exp(tot - bh), True, False))
    return jnp.concatenate(sts, axis=0), jnp.concatenate(outs, axis=1)


_GQK, _GV = GLA_H * GLA_DK, GLA_H * GLA_DV


GLA_CPS = 2
_GB = GLA_C * GLA_CPS


def _gla_order(rev):
    return list(reversed(range(GLA_CPS))) if rev else list(range(GLA_CPS))


def _gla_fwd(proj, bg, ncc, rev, name):
    A = proj.shape[0]
    nb = A // _GB
    ch = lambda s: _chunk_of(s, nb, ncc // GLA_CPS, rev)

    def body(q_ref, k_ref, v_ref, b_ref, o_ref, ss_ref, st_ref):
        @pl.when(pl.program_id(0) == 0)
        def _():
            st_ref[...] = jnp.zeros_like(st_ref)

        st = st_ref[...]
        for pos, sub in enumerate(_gla_order(rev)):
            rows = slice(sub * GLA_C, (sub + 1) * GLA_C)
            ss_ref[0, pos] = st
            st, o = _gla_step(st, q_ref[rows].astype(F32), k_ref[rows].astype(F32), v_ref[rows].astype(F32), b_ref[rows], rev)
            o_ref[rows] = o.astype(o_ref.dtype)
        st_ref[...] = st

    return pl.pallas_call(
        body, grid=(nb,),
        in_specs=[pl.BlockSpec((_GB, _GQK), lambda s: (ch(s), C_Q // _GQK)),
                  pl.BlockSpec((_GB, _GQK), lambda s: (ch(s), C_K // _GQK)),
                  pl.BlockSpec((_GB, _GV), lambda s: (ch(s), C_V // _GV)),
                  pl.BlockSpec((_GB, _GQK), lambda s: (ch(s), 0))],
        out_specs=[pl.BlockSpec((_GB, _GV), lambda s: (ch(s), 0)),
                   pl.BlockSpec((1, GLA_CPS, _GV, GLA_DK), lambda s: (s, 0, 0, 0))],
        out_shape=[jax.ShapeDtypeStruct((A, _GV), BF16), jax.ShapeDtypeStruct((nb, GLA_CPS, _GV, GLA_DK), F32)],
        scratch_shapes=[pltpu.VMEM((_GV, GLA_DK), F32)],
        compiler_params=_cparams(("arbitrary",)), name=name,
    )(proj, proj, proj, bg)


def _gla_bwd(proj, lg, saved, d_o, prev, ncc, rev, name, pack=None):
    A = proj.shape[0]
    nb = A // _GB
    st_of = lambda r: nb - 1 - r
    ch = lambda r: _chunk_of(st_of(r), nb, ncc // GLA_CPS, rev)
    qs = pl.BlockSpec((_GB, _GQK), lambda r: (ch(r), 0))
    vs = pl.BlockSpec((_GB, _GV), lambda r: (ch(r), 0))
    n_prev = 0 if prev is None else 3

    n_buf = 0 if pack is None else 1

    def body(*refs):
        q_ref, k_ref, v_ref, b_ref, ss_ref, do_ref = refs[:6]
        p_refs = refs[6:6 + n_prev]
        out_refs = refs[6 + n_prev + n_buf:]
        ds_ref = out_refs[-1]

        @pl.when(pl.program_id(0) == 0)
        def _():
            ds_ref[...] = jnp.zeros_like(ds_ref)

        ds = ds_ref[...]
        for pos, sub in reversed(list(enumerate(_gla_order(rev)))):
            rows = slice(sub * GLA_C, (sub + 1) * GLA_C)
            _, vjp = jax.vjp(functools.partial(_gla_step, rev=rev), ss_ref[0, pos], q_ref[rows].astype(F32),
                             k_ref[rows].astype(F32), v_ref[rows].astype(F32), b_ref[rows])
            ds, dq, dk, dv, db = vjp((ds, do_ref[rows].astype(F32)))
            if n_prev:
                dq, dk, dv = [d + p[rows].astype(F32) for d, p in zip((dq, dk, dv), p_refs)]
            if pack is None:
                for r, d in zip(out_refs[:4], (dq, dk, dv, db)):
                    r[rows] = d.astype(r.dtype)
            else:
                out_refs[0][rows] = jnp.concatenate([dq, dk, dv], axis=1).astype(out_refs[0].dtype)
                out_refs[1][rows] = db
        ds_ref[...] = ds

    in_specs = [
        pl.BlockSpec((_GB, _GQK), lambda r: (ch(r), C_Q // _GQK)),
        pl.BlockSpec((_GB, _GQK), lambda r: (ch(r), C_K // _GQK)),
        pl.BlockSpec((_GB, _GV), lambda r: (ch(r), C_V // _GV)),
        qs,
        pl.BlockSpec((1, GLA_CPS, _GV, GLA_DK), lambda r: (st_of(r), 0, 0, 0)),
        vs,
    ] + ([qs, qs, vs] if n_prev else [])
    args = [proj, proj, proj, lg, saved, d_o, *(prev or ())]
    if pack is None:
        out_specs = [qs, qs, vs, qs]
        out_shape = [jax.ShapeDtypeStruct((A, _GQK), BF16), jax.ShapeDtypeStruct((A, _GQK), BF16),
                     jax.ShapeDtypeStruct((A, _GV), BF16), jax.ShapeDtypeStruct((A, _GQK), F32)]
        aliases = None
    else:
        in_specs.append(pl.BlockSpec(memory_space=pl.ANY))
        args.append(pack[0])
        out_specs = [pl.BlockSpec((_GB, 2 * _GQK + _GV), lambda r: (ch(r), 0)), qs]
        out_shape = [jax.ShapeDtypeStruct((A, pack[1]), BF16), jax.ShapeDtypeStruct((A, _GQK), F32)]
        aliases = {len(args) - 1: 0}
    return _call(body, grid=(nb,), in_specs=in_specs, out_specs=out_specs, out_shape=out_shape,
                 scratch=[pltpu.VMEM((_GV, GLA_DK), F32)], sem=("arbitrary",), name=name, args=args, aliases=aliases)


def _ssd_step(st, x, bm, cm, cum, cum_t, rev):
    mask = _tri(SSM_C, rev)
    last = 0 if rev else SSM_C - 1
    tot = cum[last:last + 1]
    cb = _bdot(cm, bm, False, True)
    ys = []
    for e in range(SSM_HPG):
        lm = jnp.exp(jnp.where(mask, cum[:, e * SSM_P:e * SSM_P + 1] - cum_t[e:e + 1], -jnp.inf))
        ys.append(_bdot(cb * lm, x[:, e * SSM_P:(e + 1) * SSM_P]))
    y = jnp.concatenate(ys, axis=1) + _bdot(cm, st) * jnp.exp(cum)
    st_new = st * jnp.exp(tot) + _bdot(bm, x * jnp.exp(tot - cum), True, False)
    return st_new, y


SSM_GPS = 4
_W = SSM_HPG * SSM_P
_XBC_B = SSM_INNER // (SSM_N * SSM_GPS)
_XBC_C = _XBC_B + SSM_G // SSM_GPS


def _ssd_steps(st, x, bm, cm, cum, cum_t, rev):
    outs = [_ssd_step(st[j * SSM_N:(j + 1) * SSM_N], x[:, j * _W:(j + 1) * _W], bm[:, j * SSM_N:(j + 1) * SSM_N],
                      cm[:, j * SSM_N:(j + 1) * SSM_N], cum[:, j * _W:(j + 1) * _W],
                      cum_t[j * SSM_HPG:(j + 1) * SSM_HPG], rev) for j in range(SSM_GPS)]
    return jnp.concatenate([o[0] for o in outs], axis=0), jnp.concatenate([o[1] for o in outs], axis=1)


def _ssd_fwd(xd, xbc, cum, cum_t, ncc, rev, name, hosted=()):
    A = xd.shape[0]
    nc = A // SSM_C
    ng = SSM_G // SSM_GPS
    ch = lambda s: _chunk_of(s, nc, ncc, rev)

    def body(x_ref, b_ref, c_ref, a_ref, at_ref, y_ref, ss_ref, st_ref):
        @pl.when(pl.program_id(1) == 0)
        def _():
            st_ref[...] = jnp.zeros_like(st_ref)

        st = st_ref[...]
        ss_ref[0, 0] = st
        st_new, y = _ssd_steps(st, x_ref[...].astype(F32), b_ref[...].astype(F32), c_ref[...].astype(F32), a_ref[...],
                               at_ref[...], rev)
        y_ref[...] = y.astype(y_ref.dtype)
        st_ref[...] = st_new

    gs = pl.BlockSpec((SSM_C, _W * SSM_GPS), lambda g, s: (ch(s), g))
    return _call(
        body, grid=(ng, nc),
        in_specs=[gs, pl.BlockSpec((SSM_C, SSM_N * SSM_GPS), lambda g, s: (ch(s), _XBC_B + g)),
                  pl.BlockSpec((SSM_C, SSM_N * SSM_GPS), lambda g, s: (ch(s), _XBC_C + g)), gs,
                  pl.BlockSpec((SSM_HPG * SSM_GPS, SSM_C), lambda g, s: (g, ch(s)))],
        out_specs=[gs, pl.BlockSpec((1, 1, SSM_N * SSM_GPS, _W), lambda g, s: (g, s, 0, 0))],
        out_shape=[jax.ShapeDtypeStruct((A, SSM_INNER), BF16), jax.ShapeDtypeStruct((ng, nc, SSM_N * SSM_GPS, _W), F32)],
        scratch=[pltpu.VMEM((SSM_N * SSM_GPS, _W), F32)],
        sem=("parallel", "arbitrary"), name=name, args=(xd, xbc, xbc, cum, cum_t), hosted=hosted)


def _ssd_bwd(xd, xbc, cum, cum_t, saved, d_y, prev, ncc, rev, name):
    A = xd.shape[0]
    nc = A // SSM_C
    ng = SSM_G // SSM_GPS
    st_of = lambda r: nc - 1 - r
    ch = lambda r: _chunk_of(st_of(r), nc, ncc, rev)
    gs = pl.BlockSpec((SSM_C, _W * SSM_GPS), lambda g, r: (ch(r), g))
    ns = pl.BlockSpec((SSM_C, SSM_N * SSM_GPS), lambda g, r: (ch(r), g))
    ts = pl.BlockSpec((SSM_HPG * SSM_GPS, SSM_C), lambda g, r: (g, ch(r)))
    n_prev = 0 if prev is None else 2

    def body(*refs):
        x_ref, b_ref, c_ref, a_ref, at_ref, ss_ref, dy_ref = refs[:7]
        p_refs = refs[7:7 + n_prev]
        dx_ref, db_ref, dc_ref, da_ref, dat_ref, ds_ref = refs[7 + n_prev:]

        @pl.when(pl.program_id(1) == 0)
        def _():
            ds_ref[...] = jnp.zeros_like(ds_ref)

        _, vjp = jax.vjp(functools.partial(_ssd_steps, rev=rev), ss_ref[0, 0], x_ref[...].astype(F32), b_ref[...].astype(F32),
                         c_ref[...].astype(F32), a_ref[...], at_ref[...])
        ds, dx, db, dc, da, dat = vjp((ds_ref[...], dy_ref[...].astype(F32)))
        if n_prev:
            db, dc = db + p_refs[0][...].astype(F32), dc + p_refs[1][...].astype(F32)
        for r, d in zip((dx_ref, db_ref, dc_ref, da_ref, dat_ref), (dx, db, dc, da, dat)):
            r[...] = d.astype(r.dtype)
        ds_ref[...] = ds

    in_specs = [gs, pl.BlockSpec((SSM_C, SSM_N * SSM_GPS), lambda g, r: (ch(r), _XBC_B + g)),
                pl.BlockSpec((SSM_C, SSM_N * SSM_GPS), lambda g, r: (ch(r), _XBC_C + g)), gs, ts,
                pl.BlockSpec((1, 1, SSM_N * SSM_GPS, _W), lambda g, r: (g, st_of(r), 0, 0)), gs] + ([ns, ns] if n_prev else [])
    return pl.pallas_call(
        body, grid=(ng, nc), in_specs=in_specs, out_specs=[gs, ns, ns, gs, ts],
        out_shape=[jax.ShapeDtypeStruct((A, SSM_INNER), BF16), jax.ShapeDtypeStruct((A, SSM_G * SSM_N), BF16),
                   jax.ShapeDtypeStruct((A, SSM_G * SSM_N), BF16), jax.ShapeDtypeStruct((A, SSM_INNER), F32),
                   jax.ShapeDtypeStruct((SSM_G * SSM_HPG, A), F32)],
        scratch_shapes=[pltpu.VMEM((SSM_N * SSM_GPS, _W), F32)],
        compiler_params=_cparams(("parallel", "arbitrary")), name=name,
    )(xd, xbc, xbc, cum, cum_t, saved, d_y, *(prev or ()))


_ADA_ROWS = 16


def _ada_fwd(cc, w_shard, b_shard):
    n = w_shard.shape[1]

    def body(cc_ref, w_ref, b_ref, o_ref):
        o_ref[...] = _dot_impl(_silu(cc_ref[...]), w_ref[...], False, False) + b_ref[...]

    return pl.pallas_call(body, out_shape=jax.ShapeDtypeStruct((_ADA_ROWS, n), F32),
                          compiler_params=_cparams(()), name="ada_fwd")(cc, w_shard, b_shard)


def _ada_bwd(cc, w_shard, d_all, d_mine):
    n = w_shard.shape[1]

    def rows(ref, r):
        parts = []
        for s in range(N_DEV):
            parts.append(ref[6 * s + r:6 * s + r + 1] + ref[6 * s + 2 + r:6 * s + 3 + r] + ref[6 * s + 4 + r:6 * s + 5 + r])
        return parts

    def total(parts):
        t = parts[0]
        for p in parts[1:]:
            t = t + p
        return t

    def body(cc_ref, w_ref, da_ref, dm_ref, dw_ref, db_ref, dcc_ref):
        dd = jnp.concatenate(rows(dm_ref, 0) + [total(rows(dm_ref, 1)), jnp.zeros((_ADA_ROWS - N_DEV - 1, n), F32)], axis=0)
        cc = cc_ref[...]
        dw_ref[...] = _dot_impl(_silu(cc), dd, True, False)
        db_ref[...] = total(rows(da_ref, 0)) + total(rows(da_ref, 1))
        _, vjp = jax.vjp(_silu, cc)
        dcc_ref[...] = vjp(_dot_impl(dd, w_ref[...], False, True))[0]

    return pl.pallas_call(
        body, out_shape=[jax.ShapeDtypeStruct((D, n), F32), jax.ShapeDtypeStruct((1, d_all.shape[1]), F32),
                         jax.ShapeDtypeStruct((_ADA_ROWS, D), F32)],
        compiler_params=_cparams(()), name="ada_bwd")(cc, w_shard, d_all, d_mine)


def _local_step(x, ctx, target, ada, w, shards):
    T, Tc = x.shape[0], ctx.shape[0]
    assert Tc == TM and T % TM == 0 and GRID_W == GLA_C and TS == SSM_C == 2 * GLA_C
    A = T + Tc
    nct = Tc // TM
    n_tm, n_ts = A // TM, A // TS
    g = {}

    x_in = [_ctx(ctx, TM, nct), _lat(x, TM, nct)]
    f_norm1 = _f_norm1(nct)
    p_norm1 = [w["norm1_w"], ada]
    (h1,) = _stage_fwd(f_norm1, n_tm, x_in, p_norm1, [(A, D, BF16, TM)], "norm1")
    proj, ag_pa, ag_pb, ag_out = _mm(h1, w["w_in"], trans_b=False, out_dtype=BF16, name="mm_in", tn_t=1152,
                                     hosted=[(shards[n], False) for n in ("w_pa", "w_pb", "w_out")])
    small = _mm(h1, w["w_in"][:, C_S:], trans_b=False, out_dtype=F32, name="mm_in_small")

    p_gd = [w["up_f"], w["up_b"], w["gla_bias_f"], w["gla_bias_b"], w["dtb"], w["alog"]]
    f_conv = _f_conv(nct, Tc)
    p_conv = [w["conv_w"], w["conv_b"]]
    in_conv = [_col(proj, TM, 3072, C_U // 3072)]
    (xbc,) = _stage_fwd(f_conv, n_tm, in_conv, p_conv, [(A, 3072, BF16, TM)], "conv")
    in_gd = [_col(small, TS), _col(xbc, TS, SSM_INNER, 0)]
    n_heads = SSM_G * SSM_HPG
    lg_f, lg_b, xf, xb, axf, axb, atf, atb = _stage_fwd(
        _f_gates_dt, n_ts, in_gd, p_gd,
        [(A, 512, F32, TS)] * 2 + [(A, SSM_INNER, BF16, TS)] * 2 + [(A, SSM_INNER, F32, TS)] * 2
        + [(A, n_heads, F32, TS, True)] * 2,
        "gates_dt")

    ncc_g, ncc_s = Tc // GLA_C, Tc // SSM_C
    ogf, sv_gf = _gla_fwd(proj, lg_f, ncc_g, False, "gla_f")
    ogb, sv_gb = _gla_fwd(proj, lg_b, ncc_g, True, "gla_b")
    ysf, sv_sf, ag_gate, ag_up = _ssd_fwd(
        xf, xbc, axf, atf, ncc_s, False, "ssd_f", hosted=[(shards[n], False) for n in ("w_gate", "w_up")])
    ysb, sv_sb, ag_down = _ssd_fwd(xb, xbc, axb, atb, ncc_s, True, "ssd_b", hosted=[(shards["w_down"], False)])
    w = dict(w, w_pa=ag_pa.reshape(D, D), w_pb=ag_pb.reshape(SSM_INNER, D), w_out=ag_out.reshape(D, D),
             w_gu=jnp.concatenate([_unshard_cols(ag_gate), _unshard_cols(ag_up)], axis=1), w_down=ag_down.reshape(D_FF, D))
    got = {}

    in_gp = [_col(ogf, TM), _col(ogb, TM), _col(proj, TM, 1024, C_R // 1024)]
    p_gp = [w["gla_norm_w"]]
    (oa,) = _stage_fwd(_f_gla_post, n_tm, in_gp, p_gp, [(A, D, BF16, TM)], "gla_post")
    in_sp = [_col(ysf, TM), _col(ysb, TM), _col(xbc, TM, SSM_INNER, 0), _col(proj, TM, SSM_INNER, C_Z // SSM_INNER)]
    p_sp = [w["dsk8"], w["ssm_norm_w"]]
    (ob,) = _stage_fwd(_f_ssd_post, n_tm, in_sp, p_sp, [(A, SSM_INNER, BF16, TM)], "ssd_post")
    ya = _mm(oa, w["w_pa"], trans_b=False, out_dtype=BF16, name="mm_pa")
    yb = _mm(ob, w["w_pb"], trans_b=False, out_dtype=BF16, name="mm_pb")
    in_mg = [_col(proj, TM, 1024, C_GA // 1024), _col(proj, TM, 1024, C_GB // 1024), _col(ya, TM), _col(yb, TM)]
    (merged,) = _stage_fwd(_f_merge, n_tm, in_mg, [], [(A, D, BF16, TM)], "merge")
    mix = _mm(merged, w["w_out"], trans_b=False, out_dtype=BF16, name="mm_out")

    f_res1 = _f_res1(nct)
    in_r1 = x_in + [_col(mix, TM)]
    p_r1 = [ada, w["norm2_w"]]
    h2, hm2 = _stage_fwd(f_res1, n_tm, in_r1, p_r1, [(A, D, F32, TM), (A, D, BF16, TM)], "res1")
    gu = _mm(hm2, w["w_gu"], trans_b=False, out_dtype=BF16, name="mm_gu", tn_t=1408)
    in_sw = [_col(gu, TM)]
    (act,) = _stage_fwd(_f_swiglu, n_tm, in_sw, [], [(A, D_FF, BF16, TM)], "swiglu")
    dn = _mm(act, w["w_down"], trans_b=False, out_dtype=BF16, name="mm_down")

    loss_blk, d_h2a, d_dn, d_ada3, g["final_norm_w"] = _loss_head(h2, dn, target, ada, w["final_norm_w"], nct)
    rows16 = lambda gw: (gw.reshape(N_DEV, gw.shape[0] // N_DEV, gw.shape[1]).astype(BF16), True)
    cols16 = lambda gw: (_shard_cols(gw).astype(BF16), True)
    gw_down = _mm_tn(act, d_dn, name="dw_down", tr_t=1408)
    d_act = _mm(d_dn, w["w_down"], trans_b=True, out_dtype=BF16, name="dx_down", tn_t=1408)
    d_gu, got["w_down"] = _stage_bwd(_f_swiglu, n_tm, in_sw, [], [_col(d_act, TM)], lambda i, t: t, [(BF16, None)],
                                     "swiglu_b", hosted=[rows16(gw_down)])
    gw_gu = _mm_tn(hm2, d_gu, name="dw_gu", tm_t=2816)
    d_hm2 = _mm(d_gu, w["w_gu"], trans_b=True, out_dtype=BF16, name="dx_gu")
    d_x1, d_mix, d_ada2, g["norm2_w"], got["w_gate"] = _stage_bwd(
        f_res1, n_tm, in_r1, p_r1, [_col(d_h2a, TM), _col(d_hm2, TM)], lambda i, t: t,
        [None, (F32, None), (BF16, None)], "res1_b", hosted=[cols16(gw_gu[:, :D_FF])])
    gw_out = _mm_tn(merged, d_mix, name="dw_out")
    d_merged = _mm(d_mix, w["w_out"], trans_b=True, out_dtype=BF16, name="dx_out")
    d_ya, d_yb, d_proj, got["w_up"] = _stage_bwd(
        _f_merge, n_tm, in_mg, [], [_col(d_merged, TM)], lambda i, t: t, [None, None, (BF16, None), (BF16, None)], "merge_b",
        hosted=[cols16(gw_gu[:, D_FF:])], pack=([0, 1], None, D_INP, C_GA // (2 * D)))
    gw_pa = _mm_tn(oa, d_ya, name="dw_pa")
    gw_pb = _mm_tn(ob, d_yb, name="dw_pb")
    d_oa = _mm(d_ya, w["w_pa"], trans_b=True, out_dtype=BF16, name="dx_pa")
    d_ob = _mm(d_yb, w["w_pb"], trans_b=True, out_dtype=BF16, name="dx_pb")
    d_og, g["gla_norm_w"], d_proj, got["w_out"] = _stage_bwd(
        _f_gla_post, n_tm, in_gp, p_gp, [_col(d_oa, TM)], lambda i, t: t, [(BF16, None), None, None], "gla_post_b",
        hosted=[rows16(gw_out)], pack=([2], d_proj, D_INP, C_R // D))
    d_ys, d_xs_skip, g["dsk8"], g["ssm_norm_w"], d_proj, got["w_pa"], got["w_pb"] = _stage_bwd(
        _f_ssd_post, n_tm, in_sp, p_sp, [_col(d_ob, TM)], lambda i, t: t,
        [(BF16, None), None, (BF16, None), None], "ssd_post_b", hosted=[rows16(gw_pa), rows16(gw_pb)],
        pack=([3], d_proj, D_INP, C_Z // SSM_INNER))

    dq, dk, dv, d_lgf = _gla_bwd(proj, lg_f, sv_gf, d_og, None, ncc_g, False, "gla_f_b")
    d_proj, d_lgb = _gla_bwd(proj, lg_b, sv_gb, d_og, (dq, dk, dv), ncc_g, True, "gla_b_b", pack=(d_proj, D_INP))
    d_xf, d_bm, d_cm, d_axf, d_atf = _ssd_bwd(xf, xbc, axf, atf, sv_sf, d_ys, None, ncc_s, False, "ssd_f_b")
    d_xb, d_bm, d_cm, d_axb, d_atb = _ssd_bwd(xb, xbc, axb, atb, sv_sb, d_ys, (d_bm, d_cm), ncc_s, True, "ssd_b_b")

    cts_gd = [_col(a, TS) for a in (d_lgf, d_lgb, d_xf, d_xb, d_axf, d_axb)]
    cts_gd += [(a, (n_heads, TS), lambda i: (0, i)) for a in (d_atf, d_atb)]
    d_xs_dt, g["up_f"], g["up_b"], g["gla_bias_f"], g["gla_bias_b"], g["dtb"], g["alog"], d_proj = _stage_bwd(
        _f_gates_dt, n_ts, in_gd, p_gd, cts_gd, lambda i, t: t, [None, (BF16, None)], "gates_dt_b",
        pack=([0], d_proj, D_INP, C_S // 128))
    cts_conv = [_col(d_xs_skip, TM), _col(d_xs_dt, TM), _col(d_bm, TM), _col(d_cm, TM)]
    g["conv_w"], g["conv_b"], d_proj = _stage_bwd(
        f_conv, n_tm, in_conv, p_conv, cts_conv,
        lambda i, t: [jnp.concatenate([t[0] + t[1], t[2], t[3]], axis=1)], [None], "conv_b",
        pack=([0], d_proj, D_INP, C_U // 3072))
    gw_in = _mm_tn(h1, d_proj, name="dw_in", tn_t=1152, tm_t=2816)
    d_h1, got["w_in"] = _mm(d_proj, w["w_in"], trans_b=True, out_dtype=BF16, name="dx_in", tm_t=768, tk_t=3456,
                            hosted=[cols16(_from_padded(gw_in))])
    grad_x, g["norm1_w"], d_ada1 = _stage_bwd(
        f_norm1, n_tm, x_in, p_norm1, [_col(d_h1, TM)], lambda i, t: t,
        [None, (F32, (d_x1, (TM, D), x_in[1][2]))], "norm1_b")
    return loss_blk, grad_x, g, got, (d_ada1, d_ada2, d_ada3)


def _exchange(x, a2a, name):
    shp = x.shape[1:] if a2a else x.shape

    def body(x_ref, o_ref, send_sems, recv_sems, local_sem):
        ops = _exchange_ops(x_ref, o_ref, send_sems, recv_sems, local_sem, a2a)
        for op in ops:
            op.start()
        for op in ops:
            op.wait()

    return pl.pallas_call(
        body, out_shape=jax.ShapeDtypeStruct((N_DEV,) + tuple(shp), x.dtype),
        in_specs=[pl.BlockSpec(memory_space=pl.ANY)], out_specs=pl.BlockSpec(memory_space=pl.ANY),
        scratch_shapes=[pltpu.SemaphoreType.DMA((N_DEV - 1,)), pltpu.SemaphoreType.DMA((N_DEV - 1,)),
                        pltpu.SemaphoreType.DMA(())],
        name=name,
    )(x)


def _all_gather(x, name):
    def body(x_ref, o_ref, send_sems, recv_sems, local_sem):
        mx, my, mc = lax.axis_index("x"), lax.axis_index("y"), lax.axis_index("c")
        me, sibling = (mx, my, mc), (mx, my, 1 - mc)
        chips = [(1 - mx, my), (mx, 1 - my), (1 - mx, 1 - my)]

        def slot(px, py, pc):
            return o_ref.at[4 * px + 2 * py + pc]

        def copy(k, block, to, src=None):
            return pltpu.make_async_remote_copy(
                src_ref=slot(*block) if src is None else src, dst_ref=slot(*block),
                send_sem=send_sems.at[k], recv_sem=recv_sems.at[k], device_id=to, device_id_type=pl.DeviceIdType.MESH)

        own = pltpu.make_async_copy(x_ref, slot(*me), local_sem)
        own.start()
        first = [copy(0, me, sibling, src=x_ref)] + [copy(1 + j, me, (*chip, mc), src=x_ref) for j, chip in enumerate(chips)]
        for cp in first:
            cp.start()
        passed = [copy(4 + j, (*chip, mc), sibling) for j, chip in enumerate(chips)]
        for j, chip in enumerate(chips):
            copy(1 + j, (*chip, mc), me).wait_recv()
            passed[j].start()
        copy(0, sibling, me).wait_recv()
        for j, chip in enumerate(chips):
            copy(4 + j, (*chip, 1 - mc), me).wait_recv()
        for cp in first + passed:
            cp.wait_send()
        own.wait()

    return pl.pallas_call(
        body, out_shape=jax.ShapeDtypeStruct((N_DEV,) + tuple(x.shape), x.dtype),
        in_specs=[pl.BlockSpec(memory_space=pl.ANY)], out_specs=pl.BlockSpec(memory_space=pl.ANY),
        scratch_shapes=[pltpu.SemaphoreType.DMA((N_DEV - 1,)), pltpu.SemaphoreType.DMA((N_DEV - 1,)),
                        pltpu.SemaphoreType.DMA(())],
        name=name,
    )(x)


def _adamw_math(w, gr, m, v):
    m = ADAM_B1 * m + (1.0 - ADAM_B1) * gr
    v = ADAM_B2 * v + (1.0 - ADAM_B2) * (gr * gr)
    m_hat = m / np.float32(1.0 - ADAM_B1 ** ADAM_STEP)
    v_hat = v / np.float32(1.0 - ADAM_B2 ** ADAM_STEP)
    delta = -ADAM_LR * (m_hat / (jnp.sqrt(v_hat) + ADAM_EPS) + ADAM_WD * w)
    return delta, m, v


def _sum_adamw(parts, w, m, v, name):
    R, C = w.shape
    n_parts = parts.shape[0]
    tr = _pick(R, max(16, (2 << 20) // (4 * C) // 16 * 16), 16)

    def body(p_ref, w_ref, m_ref, v_ref, g_ref, d_ref, mo_ref, vo_ref):
        gr = p_ref[0].astype(F32)
        for k in range(1, n_parts):
            gr = gr + p_ref[k].astype(F32)
        g_ref[...] = gr
        d_ref[...], mo_ref[...], vo_ref[...] = _adamw_math(w_ref[...], gr, m_ref[...], v_ref[...])

    row = pl.BlockSpec((tr, C), lambda i: (i, 0))
    return pl.pallas_call(
        body, grid=(R // tr,),
        in_specs=[pl.BlockSpec((n_parts, tr, C), lambda i: (0, i, 0)), row, row, row],
        out_specs=[row] * 4, out_shape=[jax.ShapeDtypeStruct((R, C), F32)] * 4,
        compiler_params=_cparams(("parallel",)), name=name,
    )(parts, w, m, v)


def _sum8(parts, name):
    _, R, C = parts.shape

    def body(p_ref, g_ref):
        gr = p_ref[0]
        for k in range(1, N_DEV):
            gr = gr + p_ref[k]
        g_ref[...] = gr

    return pl.pallas_call(body, out_shape=jax.ShapeDtypeStruct((R, C), F32), name=name)(parts)


def _adamw(w, gr, m, v, name):
    def body(w_ref, g_ref, m_ref, v_ref, d_ref, mo_ref, vo_ref):
        d_ref[...], mo_ref[...], vo_ref[...] = _adamw_math(w_ref[...], g_ref[...], m_ref[...], v_ref[...])

    return pl.pallas_call(body, out_shape=[jax.ShapeDtypeStruct(w.shape, F32)] * 3, name=name)(w, gr, m, v)


def _to_padded(w_in):
    z = jnp.zeros(w_in.shape[:-1] + (32,), w_in.dtype)
    return jnp.concatenate([w_in[..., 0:3072], w_in[..., 5152:8224], w_in[..., 3104:5152], w_in[..., 8288:10336],
                            w_in[..., 3072:3104], w_in[..., 8224:8288], z], axis=-1)


def _from_padded(p):
    return jnp.concatenate([p[..., 0:3072], p[..., 10240:10272], p[..., 6144:8192], p[..., 3072:6144],
                            p[..., 10272:10336], p[..., 8192:10240]], axis=-1)


def _unshard_cols(gathered):
    n, r, c = gathered.shape
    return jnp.transpose(gathered, (1, 0, 2)).reshape(r, n * c)


def _shard_cols(full):
    r, nc = full.shape
    return jnp.transpose(full.reshape(r, N_DEV, nc // N_DEV), (1, 0, 2))


def _lanes(vec, lo):
    return jnp.zeros((1, 128), F32).at[:, lo:lo + vec.shape[1]].set(vec)


_SMALL = (("b_ada", 6 * D), ("c_ctx", D), ("norm1_w", D), ("gla_bias_f", 512), ("gla_bias_b", 512), ("gla_norm_w", 256),
          ("conv_b", 3072), ("dt_bias_f", 32), ("dt_bias_b", 32), ("a_log_f", 32), ("a_log_b", 32), ("d_skip", 32),
          ("ssm_norm_w", 2048), ("norm2_w", D), ("final_norm_w", D))
_SHARDED_SMALL = (("gla_up_f", 16 * 512), ("gla_up_b", 16 * 512), ("conv_w", 4 * 3072))


def _pack(vals, names):
    flat = jnp.concatenate([vals[n].reshape(-1).astype(F32) for n, _ in names])
    pad = (-flat.shape[0]) % 1024
    return jnp.concatenate([flat, jnp.zeros((pad,), F32)]).reshape(-1, 128)


def _unpack(packed, names, shapes):
    flat, out, off = packed.reshape(-1), {}, 0
    for n, size in names:
        out[n] = flat[off:off + size].reshape(shapes[n])
        off += size
    return out


def kernel(x, c, ctx, c_ctx, w_ada, b_ada, norm1_w, w_in, gla_up_f, gla_bias_f, gla_up_b, gla_bias_b, gla_norm_w, conv_w, conv_b, dt_bias_f, dt_bias_b, a_log_f, a_log_b, d_skip, ssm_norm_w, w_pa, w_pb, w_out, norm2_w, w_gate, w_up, w_down, final_norm_w, loss_target, m_c_ctx, m_w_ada, m_b_ada, m_norm1_w, m_w_in, m_gla_up_f, m_gla_bias_f, m_gla_up_b, m_gla_bias_b, m_gla_norm_w, m_conv_w, m_conv_b, m_dt_bias_f, m_dt_bias_b, m_a_log_f, m_a_log_b, m_d_skip, m_ssm_norm_w, m_w_pa, m_w_pb, m_w_out, m_norm2_w, m_w_gate, m_w_up, m_w_down, m_final_norm_w, v_c_ctx, v_w_ada, v_b_ada, v_norm1_w, v_w_in, v_gla_up_f, v_gla_bias_f, v_gla_up_b, v_gla_bias_b, v_gla_norm_w, v_conv_w, v_conv_b, v_dt_bias_f, v_dt_bias_b, v_a_log_f, v_a_log_b, v_d_skip, v_ssm_norm_w, v_w_pa, v_w_pb, v_w_out, v_norm2_w, v_w_gate, v_w_up, v_w_down, v_final_norm_w):
    args = dict(locals())
    me = 4 * lax.axis_index("x") + 2 * lax.axis_index("y") + lax.axis_index("c")

    def gather16(a, name):
        return _all_gather(a[0].astype(BF16), name)

    full = {"w_in": _to_padded(_unshard_cols(gather16(w_in, "ag_w_in")))}
    shards = {n: args[n][0].astype(BF16) for n in ("w_pa", "w_pb", "w_out", "w_gate", "w_up", "w_down")}
    sm = _exchange(jnp.concatenate([gla_up_f.reshape(-1), gla_up_b.reshape(-1), conv_w.reshape(-1), c.reshape(-1)]).reshape(-1, 128),
                   False, "ag_small")
    sm = sm.reshape(N_DEV, -1)
    n_ada = w_ada.shape[2]
    cc = jnp.zeros((_ADA_ROWS, D), F32).at[0:N_DEV].set(sm[:, 3584:3584 + D]).at[N_DEV].set(c_ctx)
    w_ada16 = w_ada[0].astype(BF16)
    ada_cols = _exchange(_ada_fwd(cc, w_ada16, lax.dynamic_slice(b_ada, (0, me * n_ada), (1, n_ada))), False, "ag_ada")
    ada = jnp.zeros((8, N_DEV * n_ada), F32)
    ada = ada.at[0].set(lax.dynamic_index_in_dim(ada_cols, me, axis=1, keepdims=False).reshape(-1))
    ada = ada.at[1].set(ada_cols[:, N_DEV].reshape(-1))
    up_f = _unshard_cols(sm[:, 0:1024].reshape(N_DEV, 16, 64))
    up_b = _unshard_cols(sm[:, 1024:2048].reshape(N_DEV, 16, 64))
    full["conv_w"] = _unshard_cols(sm[:, 2048:3584].reshape(N_DEV, 4, 384))
    full["up_f"] = jnp.zeros((128, 512), F32).at[0:16].set(up_f)
    full["up_b"] = jnp.zeros((128, 512), F32).at[16:32].set(up_b)
    full["dtb"] = _lanes(dt_bias_f, 32) + _lanes(dt_bias_b, 64)
    full["alog"] = _lanes(a_log_f, 32) + _lanes(a_log_b, 64)
    full["dsk8"] = jnp.zeros((8, 128), F32).at[0:1, 0:32].set(d_skip)
    for n in ("norm1_w", "gla_bias_f", "gla_bias_b", "gla_norm_w", "conv_b", "ssm_norm_w", "norm2_w"):
        full[n] = args[n]
    full["final_norm_w"] = final_norm_w.reshape(1, D)

    loss_blk, grad_x, g, got, d_ada = _local_step(x[0], ctx[0], loss_target[0], ada, full, shards)

    d_all = _exchange(jnp.concatenate([d[0:2] for d in d_ada], axis=0), False, "ag_d_ada").reshape(N_DEV * 6, N_DEV * n_ada)
    gw_ada, gb_ada, d_cc = _ada_bwd(cc, w_ada16, d_all, lax.dynamic_slice(d_all, (0, me * n_ada), (N_DEV * 6, n_ada)))

    gs = dict(g, c_ctx=d_cc[N_DEV], b_ada=jnp.zeros_like(b_ada))
    gs["dt_bias_f"], gs["dt_bias_b"] = g["dtb"][:, 32:64], g["dtb"][:, 64:96]
    gs["a_log_f"], gs["a_log_b"] = g["alog"][:, 32:64], g["alog"][:, 64:96]
    gs["d_skip"] = g["dsk8"][0:1, 0:32]
    gs["gla_up_f"], gs["gla_up_b"] = g["up_f"][0:16], g["up_b"][16:32]
    gs["loss"] = loss_blk[0:1, 0:1]
    names = _SMALL + _SHARDED_SMALL + (("loss", 1),)
    shapes = {n: (args[n].shape if n in args else (1, 1)) for n, _ in names}
    shapes.update({"gla_up_f": (16, 512), "gla_up_b": (16, 512), "conv_w": (4, 3072)})
    red = _unpack(_sum8(_exchange(_pack(gs, names), False, "ag_small_grads"), "sum_small_grads"), names, shapes)
    loss = red["loss"].reshape(())
    grads = {n: red[n] for n, _ in _SMALL}
    grads["b_ada"] = gb_ada
    grads["gla_up_f"] = lax.dynamic_slice(red["gla_up_f"], (0, me * 64), (16, 64))[None]
    grads["gla_up_b"] = lax.dynamic_slice(red["gla_up_b"], (0, me * 64), (16, 64))[None]
    grads["conv_w"] = lax.dynamic_slice(red["conv_w"], (0, me * 384), (4, 384))[None]
    upd_names = tuple((n, s) for n, s in _SMALL) + (("gla_up_f", 1024), ("gla_up_b", 1024), ("conv_w", 1536))
    pk = lambda prefix, src: _pack({n: src[prefix + n] for n, _ in upd_names}, upd_names)
    d_s, m_s, v_s = _adamw(pk("", args), _pack(grads, upd_names), pk("m_", args), pk("v_", args), "adamw_small")
    upd_shapes = {n: args[n].shape for n, _ in upd_names}
    delta = _unpack(d_s, upd_names, upd_shapes)
    new_m = _unpack(m_s, upd_names, upd_shapes)
    new_v = _unpack(v_s, upd_names, upd_shapes)

    got["w_ada"] = gw_ada[None]
    for n, parts in got.items():
        gr, dl, mo, vo = _sum_adamw(parts, args[n][0], args["m_" + n][0], args["v_" + n][0], "adamw_" + n)
        grads[n], delta[n], new_m[n], new_v[n] = gr[None], dl[None], mo[None], vo[None]

    order = ["c_ctx", "w_ada", "b_ada", "norm1_w", "w_in", "gla_up_f", "gla_bias_f", "gla_up_b", "gla_bias_b", "gla_norm_w",
             "conv_w", "conv_b", "dt_bias_f", "dt_bias_b", "a_log_f", "a_log_b", "d_skip", "ssm_norm_w", "w_pa", "w_pb",
             "w_out", "norm2_w", "w_gate", "w_up", "w_down", "final_norm_w"]
    fix = lambda d: [d[n].reshape(args[n].shape) for n in order]
    return (loss, grad_x[None], *fix(grads), *fix(delta), *fix(new_m), *fix(new_v))
```

```python
import functools

import jax
import jax.numpy as jnp
import numpy as np
from jax import lax
from jax.experimental import pallas as pl
from jax.experimental.pallas import tpu as pltpu

F32 = jnp.float32
BF16 = jnp.bfloat16
N_DEV = 8
D = 1024
EPS = 1e-6
GRID_W = 64
GLA_H, GLA_DK, GLA_DV = 4, 128, 256
GLA_C = 64
GLA_TAU = 16.0
SSM_G, SSM_HPG, SSM_P, SSM_N = 4, 8, 64, 128
SSM_C = 128
SSM_INNER = 2048
D_FF = 2816
D_IN = 10336
D_INP = 10368
C_Q, C_K, C_V, C_R, C_U, C_Z, C_GA, C_GB, C_S = 0, 512, 1024, 2048, 3072, 6144, 8192, 9216, 10240
TM = 256
TS = 128

ADAM_LR, ADAM_B1, ADAM_B2, ADAM_EPS, ADAM_WD, ADAM_STEP = 0.001, 0.9, 0.999, 1e-08, 0.01, 10
VMEM_LIMIT = 56 << 20


def _cparams(sem):
    return pltpu.CompilerParams(dimension_semantics=sem or None, vmem_limit_bytes=VMEM_LIMIT)


def _pick(n, target, mult):
    best = None
    for t in range(mult, min(n, target) + 1, mult):
        if n % t == 0:
            best = t
    return best if best is not None else n


def _dot_impl(a, b, ta, tb):
    dims = (((0 if ta else 1,), (1 if tb else 0,)), ((), ()))
    return lax.dot_general(a.astype(BF16), b.astype(BF16), dims, preferred_element_type=F32)


@functools.partial(jax.custom_vjp, nondiff_argnums=(2, 3))
def _bdot(a, b, ta=False, tb=False):
    return _dot_impl(a, b, ta, tb)


def _bdot_fwd(a, b, ta, tb):
    return _dot_impl(a, b, ta, tb), (a, b)


def _bdot_bwd(ta, tb, res, g):
    a, b = res
    if not ta and not tb:
        return _dot_impl(g, b, False, True), _dot_impl(a, g, True, False)
    if not ta and tb:
        return _dot_impl(g, b, False, False), _dot_impl(g, a, True, False)
    if ta and not tb:
        return _dot_impl(b, g, False, True), _dot_impl(a, g, False, False)
    raise NotImplementedError


_bdot.defvjp(_bdot_fwd, _bdot_bwd)


_NN = (((1,), (0,)), ((), ()))
_NT = (((1,), (1,)), ((), ()))
_TN = (((0,), (0,)), ((), ()))


def _dg2(x, e, x_is_lhs, dims):
    hi = x.astype(BF16)
    lo = (x - hi.astype(F32)).astype(BF16)
    e = e.astype(BF16)
    if x_is_lhs:
        return (lax.dot_general(hi, e, dims, preferred_element_type=F32)
                + lax.dot_general(lo, e, dims, preferred_element_type=F32))
    return (lax.dot_general(e, hi, dims, preferred_element_type=F32)
            + lax.dot_general(e, lo, dims, preferred_element_type=F32))


_EDOT_FWD = {"xe": (True, _NN), "ex": (False, _NN), "ext": (False, _NT)}
_EDOT_BWD = {"xe": (True, _NT), "ex": (False, _TN), "ext": (True, _TN)}


@functools.partial(jax.custom_vjp, nondiff_argnums=(2,))
def _edot(x, e, mode):
    return _dg2(x, e, *_EDOT_FWD[mode])


def _edot_fwd(x, e, mode):
    return _dg2(x, e, *_EDOT_FWD[mode]), e


def _edot_bwd(mode, e, g):
    return _dg2(g, e, *_EDOT_BWD[mode]), None


_edot.defvjp(_edot_fwd, _edot_bwd)


def _shift_impl(u, pos, per, s):
    n = u.shape[0]
    rolled = u if s == 0 else pltpu.roll(u, (-s) % n, 0)
    ok = (pos + s >= 0) & (pos + s < per)
    return jnp.where(ok, rolled, 0.0)


@functools.partial(jax.custom_vjp, nondiff_argnums=(3,))
def _shift(u, pos, per, s):
    return _shift_impl(u, pos, per, s)


def _shift_fwd(u, pos, per, s):
    return _shift_impl(u, pos, per, s), (pos, per)


def _shift_bwd(s, res, g):
    pos, per = res
    return _shift_impl(g, pos, per, -s), None, None


_shift.defvjp(_shift_fwd, _shift_bwd)


def _rms(x, w):
    return x * lax.rsqrt(jnp.mean(x * x, axis=-1, keepdims=True) + EPS) * w


def _silu(x):
    return x * jax.nn.sigmoid(x)


def _softplus(x):
    return jnp.maximum(x, 0.0) + jnp.log(1.0 + jnp.exp(-jnp.abs(x)))


def _logsig(x):
    return jnp.minimum(x, 0.0) - jnp.log(1.0 + jnp.exp(-jnp.abs(x)))


def _tri(n, rev):
    t = lax.broadcasted_iota(jnp.int32, (n, n), 0)
    s = lax.broadcasted_iota(jnp.int32, (n, n), 1)
    return (s >= t) if rev else (t >= s)


def _head_expand(first_lane):
    l = lax.broadcasted_iota(jnp.int32, (128, SSM_INNER), 0)
    c = lax.broadcasted_iota(jnp.int32, (128, SSM_INNER), 1)
    return (l == first_lane + lax.shift_right_logical(c, 6)).astype(F32)


def _exchange_ops(x_ref, o_ref, send_sems, recv_sems, local_sem, a2a):
    mx, my, mc = lax.axis_index("x"), lax.axis_index("y"), lax.axis_index("c")
    me = 4 * mx + 2 * my + mc
    ops = []
    for k in range(1, N_DEV):
        px = 1 - mx if k & 4 else mx
        py = 1 - my if k & 2 else my
        pc = 1 - mc if k & 1 else mc
        ops.append(pltpu.make_async_remote_copy(
            src_ref=x_ref.at[4 * px + 2 * py + pc] if a2a else x_ref, dst_ref=o_ref.at[me],
            send_sem=send_sems.at[k - 1], recv_sem=recv_sems.at[k - 1],
            device_id=(px, py, pc), device_id_type=pl.DeviceIdType.MESH))
    ops.append(pltpu.make_async_copy(x_ref.at[me] if a2a else x_ref, o_ref.at[me], local_sem))
    return ops


def _call(body, *, grid, in_specs, out_specs, out_shape, scratch=(), sem, name, args, hosted=(), aliases=None):
    n_in, n_out, n_s, n_h = len(in_specs), len(out_specs), len(scratch), len(hosted)
    aliases = aliases or {}
    if not n_h:
        return pl.pallas_call(body, grid=grid, in_specs=list(in_specs), out_specs=list(out_specs),
                              out_shape=list(out_shape), scratch_shapes=list(scratch), input_output_aliases=aliases,
                              compiler_params=_cparams(sem), name=name)(*args)
    b0, c0 = n_in + n_h, n_in + n_h + n_out
    d0 = c0 + n_h

    def wrapped(*refs):
        hx, ho = refs[n_in:b0], refs[c0:d0]
        send_sems, recv_sems, local_sems = refs[d0 + n_s:]
        ids = [pl.program_id(ax) for ax in range(len(grid))]
        first = functools.reduce(jnp.logical_and, [i == 0 for i in ids])
        last = functools.reduce(jnp.logical_and, [i == n - 1 for i, n in zip(ids, grid)])

        def ops():
            return [op for n, (_, a2a) in enumerate(hosted)
                    for op in _exchange_ops(hx[n], ho[n], send_sems.at[n], recv_sems.at[n], local_sems.at[n], a2a)]

        @pl.when(first)
        def _():
            for op in ops():
                op.start()

        body(*refs[:n_in], *refs[b0:c0], *refs[d0:d0 + n_s])

        @pl.when(last)
        def _():
            for op in ops():
                op.wait()

    hbm = pl.BlockSpec(memory_space=pl.ANY)
    h_shapes = [jax.ShapeDtypeStruct((N_DEV,) + tuple(x.shape[1:] if a2a else x.shape), x.dtype) for x, a2a in hosted]
    return pl.pallas_call(
        wrapped, grid=grid, in_specs=list(in_specs) + [hbm] * n_h, out_specs=list(out_specs) + [hbm] * n_h,
        out_shape=list(out_shape) + h_shapes, input_output_aliases=aliases,
        scratch_shapes=list(scratch) + [pltpu.SemaphoreType.DMA((n_h, N_DEV - 1)), pltpu.SemaphoreType.DMA((n_h, N_DEV - 1)),
                                        pltpu.SemaphoreType.DMA((n_h,))],
        compiler_params=_cparams(("arbitrary",) * len(grid)), name=name,
    )(*args, *[x for x, _ in hosted])


def _mm(a, b, *, trans_b, out_dtype, name, tm_t=1408, tn_t=1024, tk_t=2816, hosted=()):
    M, K = a.shape
    N = b.shape[0] if trans_b else b.shape[1]
    tm, tn, tk = _pick(M, tm_t, 8), _pick(N, tn_t, 128), _pick(K, tk_t, 128)
    nk = K // tk
    dims = (((1,), (1,)), ((), ())) if trans_b else (((1,), (0,)), ((), ()))

    def body_one(a_ref, b_ref, o_ref):
        o_ref[...] = lax.dot_general(a_ref[...], b_ref[...], dims, preferred_element_type=F32).astype(out_dtype)

    def body_acc(a_ref, b_ref, o_ref, acc_ref):
        k = pl.program_id(2)
        p = lax.dot_general(a_ref[...], b_ref[...], dims, preferred_element_type=F32)

        @pl.when(k == 0)
        def _():
            acc_ref[...] = p

        @pl.when(k > 0)
        def _():
            acc_ref[...] += p

        @pl.when(k == nk - 1)
        def _():
            o_ref[...] = acc_ref[...].astype(out_dtype)

    b_spec = pl.BlockSpec((tn, tk), lambda j, i, k: (j, k)) if trans_b else pl.BlockSpec((tk, tn), lambda j, i, k: (k, j))
    outs = _call(
        body_one if nk == 1 else body_acc, grid=(N // tn, M // tm, nk),
        in_specs=[pl.BlockSpec((tm, tk), lambda j, i, k: (i, k)), b_spec],
        out_specs=[pl.BlockSpec((tm, tn), lambda j, i, k: (i, j))],
        out_shape=[jax.ShapeDtypeStruct((M, N), out_dtype)],
        scratch=[] if nk == 1 else [pltpu.VMEM((tm, tn), F32)],
        sem=("parallel", "parallel", "arbitrary"), name=name, args=(a, b), hosted=hosted)
    return outs if hosted else outs[0]


def _mm_tn(a, b, *, name, tm_t=2816, tr_t=1024, tn_t=1408):
    M, R = a.shape
    N = b.shape[1]
    tm, tr, tn = _pick(M, tm_t, 8), _pick(R, tr_t, 128), _pick(N, tn_t, 128)

    def body(a_ref, b_ref, o_ref):
        m = pl.program_id(2)
        p = lax.dot_general(a_ref[...], b_ref[...], (((0,), (0,)), ((), ())), preferred_element_type=F32)

        @pl.when(m == 0)
        def _():
            o_ref[...] = p

        @pl.when(m > 0)
        def _():
            o_ref[...] += p

    return pl.pallas_call(
        body, grid=(R // tr, N // tn, M // tm),
        in_specs=[pl.BlockSpec((tm, tr), lambda r, j, m: (m, r)), pl.BlockSpec((tm, tn), lambda r, j, m: (m, j))],
        out_specs=pl.BlockSpec((tr, tn), lambda r, j, m: (r, j)),
        out_shape=jax.ShapeDtypeStruct((R, N), F32),
        compiler_params=_cparams(("parallel", "parallel", "arbitrary")), name=name,
    )(a, b)


def _col(arr, tm, width=None, col=0):
    width = arr.shape[1] if width is None else width
    return (arr, (tm, width), lambda i: (i, col))


def _lat(arr, tm, nct):
    return (arr, (tm, arr.shape[1]), lambda i: (jnp.maximum(i - nct, 0), 0))


def _ctx(arr, tm, nct):
    return (arr, (tm, arr.shape[1]), lambda i: (jnp.minimum(i, nct - 1), 0))


def _whole(p):
    return pl.BlockSpec(p.shape, lambda i, nd=p.ndim: (0,) * nd)


def _stage_fwd(f, n_tiles, ins, params, outs, name):
    ni, npar = len(ins), len(params)
    o_specs = [pl.BlockSpec((o[1], o[3]), lambda i: (0, i)) if len(o) > 4 else pl.BlockSpec((o[3], o[1]), lambda i: (i, 0))
               for o in outs]
    o_shapes = [jax.ShapeDtypeStruct((o[1], o[0]) if len(o) > 4 else (o[0], o[1]), o[2]) for o in outs]

    def body(*refs):
        i = pl.program_id(0)
        xs = [r[...].astype(F32) for r in refs[:ni]]
        ps = [r[...] for r in refs[ni:ni + npar]]
        for r, v in zip(refs[ni + npar:], f(i, xs, ps, False)):
            r[...] = v.astype(r.dtype)

    return pl.pallas_call(
        body, grid=(n_tiles,),
        in_specs=[pl.BlockSpec(bs, fn) for _, bs, fn in ins] + [_whole(p) for p in params],
        out_specs=o_specs, out_shape=o_shapes,
        compiler_params=_cparams(("parallel",)), name=name,
    )(*[a for a, _, _ in ins], *params)


def _stage_bwd(f, n_tiles, ins, params, cts, ct_fn, dins, name, hosted=(), pack=None):
    ni, npar, nc = len(ins), len(params), len(cts)
    want = [k for k, d in enumerate(dins) if d is not None]
    extras = [dins[k][1] for k in want if dins[k][1] is not None]
    n_buf = 0 if pack is None or pack[1] is None else 1
    n_main_in = ni + npar + nc + len(extras)

    def body(*refs):
        i = pl.program_id(0)
        xs = [r[...].astype(F32) for r in refs[:ni]]
        ps = [r[...] for r in refs[ni:ni + npar]]
        ct_tiles = [r[...].astype(F32) for r in refs[ni + npar:ni + npar + nc]]
        ex_refs = list(refs[ni + npar + nc:n_main_in])
        out_refs = refs[n_main_in + n_buf:]
        _, vjp = jax.vjp(lambda xs_, ps_: tuple(f(i, xs_, ps_, True)), xs, ps)
        dxs, dps = vjp(tuple(ct_fn(i, ct_tiles)))
        for n, k in enumerate(want):
            v = dxs[k]
            if dins[k][1] is not None:
                v = v + ex_refs.pop(0)[...].astype(F32)
            out_refs[n][...] = v.astype(out_refs[n].dtype)
        if pack is not None:
            packed = [dxs[k] for k in pack[0]]
            r = out_refs[len(want) + npar]
            r[...] = (packed[0] if len(packed) == 1 else jnp.concatenate(packed, axis=1)).astype(r.dtype)
        for r, v in zip(out_refs[len(want):len(want) + npar], dps):
            @pl.when(i == 0)
            def _(r=r, v=v):
                r[...] = v

            @pl.when(i > 0)
            def _(r=r, v=v):
                r[...] += v

    in_specs = ([pl.BlockSpec(bs, fn) for _, bs, fn in ins] + [_whole(p) for p in params]
                + [pl.BlockSpec(bs, fn) for _, bs, fn in cts] + [pl.BlockSpec(bs, fn) for _, bs, fn in extras])
    out_specs = ([pl.BlockSpec(ins[k][1], lambda i, fn=ins[k][2]: (fn(i)[0], 0)) for k in want] + [_whole(p) for p in params])
    out_shape = ([jax.ShapeDtypeStruct((ins[k][0].shape[0], ins[k][1][1]), dins[k][0]) for k in want]
                 + [jax.ShapeDtypeStruct(p.shape, F32) for p in params])
    args = [*[a for a, _, _ in ins], *params, *[a for a, _, _ in cts], *[a for a, _, _ in extras]]
    aliases = None
    if pack is not None:
        ks, buf, total_cols, block = pack
        width = sum(ins[k][1][1] for k in ks)
        out_specs.append(pl.BlockSpec((ins[ks[0]][1][0], width), lambda i, fn=ins[ks[0]][2]: (fn(i)[0], block)))
        out_shape.append(jax.ShapeDtypeStruct((ins[ks[0]][0].shape[0], total_cols), BF16))
        if buf is not None:
            in_specs.append(pl.BlockSpec(memory_space=pl.ANY))
            args.append(buf)
            aliases = {len(args) - 1: len(out_specs) - 1}
    return _call(body, grid=(n_tiles,), in_specs=in_specs, out_specs=out_specs, out_shape=out_shape,
                 sem=("arbitrary",), name=name, args=args, hosted=hosted, aliases=aliases)


def _mod_rows(i, nct, ada, lo):
    return jnp.where(i < nct, ada[1:2, lo:lo + D], ada[0:1, lo:lo + D])


def _f_norm1(nct):
    def f(i, xs, ps, diff):
        ctx, x = xs
        nw, ada = ps
        xt = jnp.where(i < nct, ctx, x)
        return (_rms(xt, nw) * (1.0 + _mod_rows(i, nct, ada, D)) + _mod_rows(i, nct, ada, 0),)
    return f


def _f_gates_dt(i, xs, ps, diff):
    small, xc = xs
    up_f, up_b, gb_f, gb_b, dtb, alog = ps
    dot = _bdot if diff else _dot_impl_nn
    lg_f = _logsig(dot(small, up_f) + gb_f) * (1.0 / GLA_TAU)
    lg_b = _logsig(dot(small, up_b) + gb_b) * (1.0 / GLA_TAU)
    dtp = _softplus(small + dtb)
    aa = -jnp.exp(alog) * dtp
    e_f, e_b = _head_expand(32), _head_expand(64)
    n = small.shape[0]
    t = lax.broadcasted_iota(jnp.int32, (n, n), 0)
    s = lax.broadcasted_iota(jnp.int32, (n, n), 1)
    same = lax.shift_right_logical(t, 6) == lax.shift_right_logical(s, 6)
    pre_g, suf_g = ((t >= s) & same).astype(F32), ((s >= t) & same).astype(F32)
    pre_s, suf_s = (t >= s).astype(F32), (s >= t).astype(F32)
    cum_f, cum_b = _edot(aa, pre_s, "ex"), _edot(aa, suf_s, "ex")
    h = lax.broadcasted_iota(jnp.int32, (SSM_G * SSM_HPG, 128), 0)
    l = lax.broadcasted_iota(jnp.int32, (SSM_G * SSM_HPG, 128), 1)
    cum_ft = _edot(cum_f, (l == h + 32).astype(F32), "ext")
    cum_bt = _edot(cum_b, (l == h + 64).astype(F32), "ext")
    return (_edot(lg_f, pre_g, "ex"), _edot(lg_b, suf_g, "ex"), xc * _bdot(dtp, e_f), xc * _bdot(dtp, e_b),
            _edot(cum_f, e_f, "xe"), _edot(cum_b, e_b, "xe"), cum_ft, cum_bt)


def _dot_impl_nn(a, b):
    return _dot_impl(a, b, False, False)


def _f_conv(nct, tc):
    def f(i, xs, ps, diff):
        (u,) = xs
        cw, cb = ps
        n = u.shape[0]
        t = lax.broadcasted_iota(jnp.int32, (n, 1), 0)
        per = jnp.where(i < nct, tc, GRID_W)
        pos = jnp.bitwise_and(t, per - 1)
        sh = _shift if diff else _shift_impl
        acc = cb + sh(u, pos, per, -2) * cw[0:1]
        for j in range(1, 4):
            acc = acc + sh(u, pos, per, j - 2) * cw[j:j + 1]
        return (_silu(acc),)
    return f


def _f_gla_post(i, xs, ps, diff):
    ogf, ogb, r = xs
    (gw,) = ps
    o = ogf + ogb
    parts = [_rms(o[:, h * GLA_DV:(h + 1) * GLA_DV], gw) for h in range(GLA_H)]
    return (jnp.concatenate(parts, axis=1) * _silu(r),)


def _f_ssd_post(i, xs, ps, diff):
    yf, yb, xc, z = xs
    dsk8, nw = ps
    dsk = _edot(dsk8, _head_expand(0), "xe")[0:1]
    y = (yf + yb + dsk * xc) * _silu(z)
    w = SSM_INNER // SSM_G
    parts = [_rms(y[:, g * w:(g + 1) * w], nw[:, g * w:(g + 1) * w]) for g in range(SSM_G)]
    return (jnp.concatenate(parts, axis=1),)


def _f_merge(i, xs, ps, diff):
    ga, gb, ya, yb = xs
    return (jax.nn.sigmoid(ga) * ya + jax.nn.sigmoid(gb) * yb,)


def _f_res1(nct):
    def f(i, xs, ps, diff):
        ctx, x, mix = xs
        ada, nw = ps
        h2 = jnp.where(i < nct, ctx, x) + _mod_rows(i, nct, ada, 2 * D) * mix
        return (h2, _rms(h2, nw) * (1.0 + _mod_rows(i, nct, ada, 4 * D)) + _mod_rows(i, nct, ada, 3 * D))
    return f


def _f_swiglu(i, xs, ps, diff):
    (gu,) = xs
    return (_silu(gu[:, :D_FF]) * gu[:, D_FF:],)


def _loss_head(h2, dn, tgt, ada, fw, nct):
    A = h2.shape[0]
    n_tiles = A // TM

    def tile_loss(i, h2t, dnt, tg, ada_, fw_):
        h3 = h2t + ada_[0:1, 5 * D:6 * D] * dnt
        err = _rms(h3, fw_) - tg
        row = 0.5 * jnp.mean(err * err, axis=-1, keepdims=True)
        return jnp.sum(row, axis=0, keepdims=True) * jnp.where(i < nct, 0.0, 1.0)

    def body(h2_ref, dn_ref, tg_ref, ada_ref, fw_ref, loss_ref, dh_ref, ddn_ref, dada_ref, dfw_ref):
        i = pl.program_id(0)
        val, vjp = jax.vjp(functools.partial(tile_loss, i), h2_ref[...], dn_ref[...].astype(F32), tg_ref[...], ada_ref[...],
                           fw_ref[...])
        dh, ddn, _, dada, dfw = vjp(jnp.ones((1, 1), F32))
        dh_ref[...] = dh
        ddn_ref[...] = ddn.astype(BF16)
        lv = jnp.broadcast_to(val, loss_ref.shape)
        for r, v in ((loss_ref, lv), (dada_ref, dada), (dfw_ref, dfw)):
            @pl.when(i == 0)
            def _(r=r, v=v):
                r[...] = v

            @pl.when(i > 0)
            def _(r=r, v=v):
                r[...] += v

    row = lambda i: (i, 0)
    return pl.pallas_call(
        body, grid=(n_tiles,),
        in_specs=[pl.BlockSpec((TM, D), row), pl.BlockSpec((TM, D), row),
                  pl.BlockSpec((TM, D), lambda i: (jnp.maximum(i - nct, 0), 0)), _whole(ada), _whole(fw)],
        out_specs=[pl.BlockSpec((8, 128), lambda i: (0, 0)), pl.BlockSpec((TM, D), row), pl.BlockSpec((TM, D), row),
                   _whole(ada), _whole(fw)],
        out_shape=[jax.ShapeDtypeStruct((8, 128), F32), jax.ShapeDtypeStruct((A, D), F32),
                   jax.ShapeDtypeStruct((A, D), BF16), jax.ShapeDtypeStruct(ada.shape, F32),
                   jax.ShapeDtypeStruct(fw.shape, F32)],
        compiler_params=_cparams(("arbitrary",)), name="loss_head",
    )(h2, dn, tgt, ada, fw)


def _chunk_of(step, n_chunks, n_ctx_chunks, rev):
    if not rev:
        return step
    return jnp.where(step < n_ctx_chunks, n_ctx_chunks - 1 - step, n_chunks - 1 - (step - n_ctx_chunks))


def _gla_step(st, q, k, v, b, rev):
    tri = _tri(GLA_C, rev).astype(F32)
    last = 0 if rev else GLA_C - 1
    outs, sts = [], []
    for h in range(GLA_H):
        kk = slice(h * GLA_DK, (h + 1) * GLA_DK)
        vv = slice(h * GLA_DV, (h + 1) * GLA_DV)
        qh, kh, vh, bh, sh = q[:, kk] * (GLA_DK ** -0.5), k[:, kk], v[:, vv], b[:, kk], st[vv]
        tot = bh[last:last + 1]
        mid = bh[GLA_C // 2:GLA_C // 2 + 1]
        att = _bdot(qh * jnp.exp(bh - mid), kh * jnp.exp(mid - bh), False, True) * tri
        outs.append(_bdot(att, vh) + _bdot(qh * jnp.exp(bh), sh, False, True))
        sts.append(sh * jnp.exp(tot) + _bdot(vh, kh * jnp.exp(tot - bh), True, False))
    return jnp.concatenate(sts, axis=0), jnp.concatenate(outs, axis=1)


_GQK, _GV = GLA_H * GLA_DK, GLA_H * GLA_DV


GLA_CPS = 2
_GB = GLA_C * GLA_CPS


def _gla_order(rev):
    return list(reversed(range(GLA_CPS))) if rev else list(range(GLA_CPS))


def _gla_fwd(proj, bg, ncc, rev, name):
    A = proj.shape[0]
    nb = A // _GB
    ch = lambda s: _chunk_of(s, nb, ncc // GLA_CPS, rev)

    def body(q_ref, k_ref, v_ref, b_ref, o_ref, ss_ref, st_ref):
        @pl.when(pl.program_id(0) == 0)
        def _():
            st_ref[...] = jnp.zeros_like(st_ref)

        st = st_ref[...]
        for pos, sub in enumerate(_gla_order(rev)):
            rows = slice(sub * GLA_C, (sub + 1) * GLA_C)
            ss_ref[0, pos] = st
            st, o = _gla_step(st, q_ref[rows].astype(F32), k_ref[rows].astype(F32), v_ref[rows].astype(F32), b_ref[rows], rev)
            o_ref[rows] = o.astype(o_ref.dtype)
        st_ref[...] = st

    return pl.pallas_call(
        body, grid=(nb,),
        in_specs=[pl.BlockSpec((_GB, _GQK), lambda s: (ch(s), C_Q // _GQK)),
                  pl.BlockSpec((_GB, _GQK), lambda s: (ch(s), C_K // _GQK)),
                  pl.BlockSpec((_GB, _GV), lambda s: (ch(s), C_V // _GV)),
                  pl.BlockSpec((_GB, _GQK), lambda s: (ch(s), 0))],
        out_specs=[pl.BlockSpec((_GB, _GV), lambda s: (ch(s), 0)),
                   pl.BlockSpec((1, GLA_CPS, _GV, GLA_DK), lambda s: (s, 0, 0, 0))],
        out_shape=[jax.ShapeDtypeStruct((A, _GV), BF16), jax.ShapeDtypeStruct((nb, GLA_CPS, _GV, GLA_DK), F32)],
        scratch_shapes=[pltpu.VMEM((_GV, GLA_DK), F32)],
        compiler_params=_cparams(("arbitrary",)), name=name,
    )(proj, proj, proj, bg)


def _gla_bwd(proj, lg, saved, d_o, prev, ncc, rev, name, pack=None):
    A = proj.shape[0]
    nb = A // _GB
    st_of = lambda r: nb - 1 - r
    ch = lambda r: _chunk_of(st_of(r), nb, ncc // GLA_CPS, rev)
    qs = pl.BlockSpec((_GB, _GQK), lambda r: (ch(r), 0))
    vs = pl.BlockSpec((_GB, _GV), lambda r: (ch(r), 0))
    n_prev = 0 if prev is None else 3

    n_buf = 0 if pack is None else 1

    def body(*refs):
        q_ref, k_ref, v_ref, b_ref, ss_ref, do_ref = refs[:6]
        p_refs = refs[6:6 + n_prev]
        out_refs = refs[6 + n_prev + n_buf:]
        ds_ref = out_refs[-1]

        @pl.when(pl.program_id(0) == 0)
        def _():
            ds_ref[...] = jnp.zeros_like(ds_ref)

        ds = ds_ref[...]
        for pos, sub in reversed(list(enumerate(_gla_order(rev)))):
            rows = slice(sub * GLA_C, (sub + 1) * GLA_C)
            _, vjp = jax.vjp(functools.partial(_gla_step, rev=rev), ss_ref[0, pos], q_ref[rows].astype(F32),
                             k_ref[rows].astype(F32), v_ref[rows].astype(F32), b_ref[rows])
            ds, dq, dk, dv, db = vjp((ds, do_ref[rows].astype(F32)))
            if n_prev:
                dq, dk, dv = [d + p[rows].astype(F32) for d, p in zip((dq, dk, dv), p_refs)]
            if pack is None:
                for r, d in zip(out_refs[:4], (dq, dk, dv, db)):
                    r[rows] = d.astype(r.dtype)
            else:
                out_refs[0][rows] = jnp.concatenate([dq, dk, dv], axis=1).astype(out_refs[0].dtype)
                out_refs[1][rows] = db
        ds_ref[...] = ds

    in_specs = [
        pl.BlockSpec((_GB, _GQK), lambda r: (ch(r), C_Q // _GQK)),
        pl.BlockSpec((_GB, _GQK), lambda r: (ch(r), C_K // _GQK)),
        pl.BlockSpec((_GB, _GV), lambda r: (ch(r), C_V // _GV)),
        qs,
        pl.BlockSpec((1, GLA_CPS, _GV, GLA_DK), lambda r: (st_of(r), 0, 0, 0)),
        vs,
    ] + ([qs, qs, vs] if n_prev else [])
    args = [proj, proj, proj, lg, saved, d_o, *(prev or ())]
    if pack is None:
        out_specs = [qs, qs, vs, qs]
        out_shape = [jax.ShapeDtypeStruct((A, _GQK), BF16), jax.ShapeDtypeStruct((A, _GQK), BF16),
                     jax.ShapeDtypeStruct((A, _GV), BF16), jax.ShapeDtypeStruct((A, _GQK), F32)]
        aliases = None
    else:
        in_specs.append(pl.BlockSpec(memory_space=pl.ANY))
        args.append(pack[0])
        out_specs = [pl.BlockSpec((_GB, 2 * _GQK + _GV), lambda r: (ch(r), 0)), qs]
        out_shape = [jax.ShapeDtypeStruct((A, pack[1]), BF16), jax.ShapeDtypeStruct((A, _GQK), F32)]
        aliases = {len(args) - 1: 0}
    return _call(body, grid=(nb,), in_specs=in_specs, out_specs=out_specs, out_shape=out_shape,
                 scratch=[pltpu.VMEM((_GV, GLA_DK), F32)], sem=("arbitrary",), name=name, args=args, aliases=aliases)


def _ssd_step(st, x, bm, cm, cum, cum_t, rev):
    mask = _tri(SSM_C, rev)
    last = 0 if rev else SSM_C - 1
    tot = cum[last:last + 1]
    cb = _bdot(cm, bm, False, True)
    ys = []
    for e in range(SSM_HPG):
        lm = jnp.exp(jnp.where(mask, cum[:, e * SSM_P:e * SSM_P + 1] - cum_t[e:e + 1], -jnp.inf))
        ys.append(_bdot(cb * lm, x[:, e * SSM_P:(e + 1) * SSM_P]))
    y = jnp.concatenate(ys, axis=1) + _bdot(cm, st) * jnp.exp(cum)
    st_new = st * jnp.exp(tot) + _bdot(bm, x * jnp.exp(tot - cum), True, False)
    return st_new, y


SSM_GPS = 4
_W = SSM_HPG * SSM_P
_XBC_B = SSM_INNER // (SSM_N * SSM_GPS)
_XBC_C = _XBC_B + SSM_G // SSM_GPS


def _ssd_steps(st, x, bm, cm, cum, cum_t, rev):
    outs = [_ssd_step(st[j * SSM_N:(j + 1) * SSM_N], x[:, j * _W:(j + 1) * _W], bm[:, j * SSM_N:(j + 1) * SSM_N],
                      cm[:, j * SSM_N:(j + 1) * SSM_N], cum[:, j * _W:(j + 1) * _W],
                      cum_t[j * SSM_HPG:(j + 1) * SSM_HPG], rev) for j in range(SSM_GPS)]
    return jnp.concatenate([o[0] for o in outs], axis=0), jnp.concatenate([o[1] for o in outs], axis=1)


def _ssd_fwd(xd, xbc, cum, cum_t, ncc, rev, name, hosted=()):
    A = xd.shape[0]
    nc = A // SSM_C
    ng = SSM_G // SSM_GPS
    ch = lambda s: _chunk_of(s, nc, ncc, rev)

    def body(x_ref, b_ref, c_ref, a_ref, at_ref, y_ref, ss_ref, st_ref):
        @pl.when(pl.program_id(1) == 0)
        def _():
            st_ref[...] = jnp.zeros_like(st_ref)

        st = st_ref[...]
        ss_ref[0, 0] = st
        st_new, y = _ssd_steps(st, x_ref[...].astype(F32), b_ref[...].astype(F32), c_ref[...].astype(F32), a_ref[...],
                               at_ref[...], rev)
        y_ref[...] = y.astype(y_ref.dtype)
        st_ref[...] = st_new

    gs = pl.BlockSpec((SSM_C, _W * SSM_GPS), lambda g, s: (ch(s), g))
    return _call(
        body, grid=(ng, nc),
        in_specs=[gs, pl.BlockSpec((SSM_C, SSM_N * SSM_GPS), lambda g, s: (ch(s), _XBC_B + g)),
                  pl.BlockSpec((SSM_C, SSM_N * SSM_GPS), lambda g, s: (ch(s), _XBC_C + g)), gs,
                  pl.BlockSpec((SSM_HPG * SSM_GPS, SSM_C), lambda g, s: (g, ch(s)))],
        out_specs=[gs, pl.BlockSpec((1, 1, SSM_N * SSM_GPS, _W), lambda g, s: (g, s, 0, 0))],
        out_shape=[jax.ShapeDtypeStruct((A, SSM_INNER), BF16), jax.ShapeDtypeStruct((ng, nc, SSM_N * SSM_GPS, _W), F32)],
        scratch=[pltpu.VMEM((SSM_N * SSM_GPS, _W), F32)],
        sem=("parallel", "arbitrary"), name=name, args=(xd, xbc, xbc, cum, cum_t), hosted=hosted)


def _ssd_bwd(xd, xbc, cum, cum_t, saved, d_y, prev, ncc, rev, name):
    A = xd.shape[0]
    nc = A // SSM_C
    ng = SSM_G // SSM_GPS
    st_of = lambda r: nc - 1 - r
    ch = lambda r: _chunk_of(st_of(r), nc, ncc, rev)
    gs = pl.BlockSpec((SSM_C, _W * SSM_GPS), lambda g, r: (ch(r), g))
    ns = pl.BlockSpec((SSM_C, SSM_N * SSM_GPS), lambda g, r: (ch(r), g))
    ts = pl.BlockSpec((SSM_HPG * SSM_GPS, SSM_C), lambda g, r: (g, ch(r)))
    n_prev = 0 if prev is None else 2

    def body(*refs):
        x_ref, b_ref, c_ref, a_ref, at_ref, ss_ref, dy_ref = refs[:7]
        p_refs = refs[7:7 + n_prev]
        dx_ref, db_ref, dc_ref, da_ref, dat_ref, ds_ref = refs[7 + n_prev:]

        @pl.when(pl.program_id(1) == 0)
        def _():
            ds_ref[...] = jnp.zeros_like(ds_ref)

        _, vjp = jax.vjp(functools.partial(_ssd_steps, rev=rev), ss_ref[0, 0], x_ref[...].astype(F32), b_ref[...].astype(F32),
                         c_ref[...].astype(F32), a_ref[...], at_ref[...])
        ds, dx, db, dc, da, dat = vjp((ds_ref[...], dy_ref[...].astype(F32)))
        if n_prev:
            db, dc = db + p_refs[0][...].astype(F32), dc + p_refs[1][...].astype(F32)
        for r, d in zip((dx_ref, db_ref, dc_ref, da_ref, dat_ref), (dx, db, dc, da, dat)):
            r[...] = d.astype(r.dtype)
        ds_ref[...] = ds

    in_specs = [gs, pl.BlockSpec((SSM_C, SSM_N * SSM_GPS), lambda g, r: (ch(r), _XBC_B + g)),
                pl.BlockSpec((SSM_C, SSM_N * SSM_GPS), lambda g, r: (ch(r), _XBC_C + g)), gs, ts,
                pl.BlockSpec((1, 1, SSM_N * SSM_GPS, _W), lambda g, r: (g, st_of(r), 0, 0)), gs] + ([ns, ns] if n_prev else [])
    return pl.pallas_call(
        body, grid=(ng, nc), in_specs=in_specs, out_specs=[gs, ns, ns, gs, ts],
        out_shape=[jax.ShapeDtypeStruct((A, SSM_INNER), BF16), jax.ShapeDtypeStruct((A, SSM_G * SSM_N), BF16),
                   jax.ShapeDtypeStruct((A, SSM_G * SSM_N), BF16), jax.ShapeDtypeStruct((A, SSM_INNER), F32),
                   jax.ShapeDtypeStruct((SSM_G * SSM_HPG, A), F32)],
        scratch_shapes=[pltpu.VMEM((SSM_N * SSM_GPS, _W), F32)],
        compiler_params=_cparams(("parallel", "arbitrary")), name=name,
    )(xd, xbc, xbc, cum, cum_t, saved, d_y, *(prev or ()))


_ADA_ROWS = 16


def _ada_fwd(cc, w_shard, b_shard):
    n = w_shard.shape[1]

    def body(cc_ref, w_ref, b_ref, o_ref):
        o_ref[...] = _dot_impl(_silu(cc_ref[...]), w_ref[...], False, False) + b_ref[...]

    return pl.pallas_call(body, out_shape=jax.ShapeDtypeStruct((_ADA_ROWS, n), F32),
                          compiler_params=_cparams(()), name="ada_fwd")(cc, w_shard, b_shard)


def _ada_bwd(cc, w_shard, d_all, d_mine):
    n = w_shard.shape[1]

    def rows(ref, r):
        parts = []
        for s in range(N_DEV):
            parts.append(ref[6 * s + r:6 * s + r + 1] + ref[6 * s + 2 + r:6 * s + 3 + r] + ref[6 * s + 4 + r:6 * s + 5 + r])
        return parts

    def total(parts):
        t = parts[0]
        for p in parts[1:]:
            t = t + p
        return t

    def body(cc_ref, w_ref, da_ref, dm_ref, dw_ref, db_ref, dcc_ref):
        dd = jnp.concatenate(rows(dm_ref, 0) + [total(rows(dm_ref, 1)), jnp.zeros((_ADA_ROWS - N_DEV - 1, n), F32)], axis=0)
        cc = cc_ref[...]
        dw_ref[...] = _dot_impl(_silu(cc), dd, True, False)
        db_ref[...] = total(rows(da_ref, 0)) + total(rows(da_ref, 1))
        _, vjp = jax.vjp(_silu, cc)
        dcc_ref[...] = vjp(_dot_impl(dd, w_ref[...], False, True))[0]

    return pl.pallas_call(
        body, out_shape=[jax.ShapeDtypeStruct((D, n), F32), jax.ShapeDtypeStruct((1, d_all.shape[1]), F32),
                         jax.ShapeDtypeStruct((_ADA_ROWS, D), F32)],
        compiler_params=_cparams(()), name="ada_bwd")(cc, w_shard, d_all, d_mine)


def _local_step(x, ctx, target, ada, w, shards):
    T, Tc = x.shape[0], ctx.shape[0]
    assert Tc == TM and T % TM == 0 and GRID_W == GLA_C and TS == SSM_C == 2 * GLA_C
    A = T + Tc
    nct = Tc // TM
    n_tm, n_ts = A // TM, A // TS
    g = {}

    x_in = [_ctx(ctx, TM, nct), _lat(x, TM, nct)]
    f_norm1 = _f_norm1(nct)
    p_norm1 = [w["norm1_w"], ada]
    (h1,) = _stage_fwd(f_norm1, n_tm, x_in, p_norm1, [(A, D, BF16, TM)], "norm1")
    proj, ag_pa, ag_pb, ag_out = _mm(h1, w["wt_in"], trans_b=True, out_dtype=BF16, name="mm_in", tn_t=1152,
                                     hosted=[(shards[n], False) for n in ("w_pa", "w_pb", "w_out")])
    small = _mm(h1, w["wt_in"][C_S:], trans_b=True, out_dtype=F32, name="mm_in_small")

    p_gd = [w["up_f"], w["up_b"], w["gla_bias_f"], w["gla_bias_b"], w["dtb"], w["alog"]]
    f_conv = _f_conv(nct, Tc)
    p_conv = [w["conv_w"], w["conv_b"]]
    in_conv = [_col(proj, TM, 3072, C_U // 3072)]
    (xbc,) = _stage_fwd(f_conv, n_tm, in_conv, p_conv, [(A, 3072, BF16, TM)], "conv")
    in_gd = [_col(small, TS), _col(xbc, TS, SSM_INNER, 0)]
    n_heads = SSM_G * SSM_HPG
    lg_f, lg_b, xf, xb, axf, axb, atf, atb = _stage_fwd(
        _f_gates_dt, n_ts, in_gd, p_gd,
        [(A, 512, F32, TS)] * 2 + [(A, SSM_INNER, BF16, TS)] * 2 + [(A, SSM_INNER, F32, TS)] * 2
        + [(A, n_heads, F32, TS, True)] * 2,
        "gates_dt")

    ncc_g, ncc_s = Tc // GLA_C, Tc // SSM_C
    ogf, sv_gf = _gla_fwd(proj, lg_f, ncc_g, False, "gla_f")
    ogb, sv_gb = _gla_fwd(proj, lg_b, ncc_g, True, "gla_b")
    ysf, sv_sf, ag_gate, ag_up = _ssd_fwd(
        xf, xbc, axf, atf, ncc_s, False, "ssd_f", hosted=[(shards[n], False) for n in ("w_gate", "w_up")])
    ysb, sv_sb, ag_down = _ssd_fwd(xb, xbc, axb, atb, ncc_s, True, "ssd_b", hosted=[(shards["w_down"], False)])
    w = dict(w, w_pa=ag_pa.reshape(D, D), w_pb=ag_pb.reshape(SSM_INNER, D), w_out=ag_out.reshape(D, D),
             wt_gu=jnp.concatenate([ag_gate.reshape(D_FF, D), ag_up.reshape(D_FF, D)], axis=0), w_down=ag_down.reshape(D_FF, D))
    got = {}

    in_gp = [_col(ogf, TM), _col(ogb, TM), _col(proj, TM, 1024, C_R // 1024)]
    p_gp = [w["gla_norm_w"]]
    (oa,) = _stage_fwd(_f_gla_post, n_tm, in_gp, p_gp, [(A, D, BF16, TM)], "gla_post")
    in_sp = [_col(ysf, TM), _col(ysb, TM), _col(xbc, TM, SSM_INNER, 0), _col(proj, TM, SSM_INNER, C_Z // SSM_INNER)]
    p_sp = [w["dsk8"], w["ssm_norm_w"]]
    (ob,) = _stage_fwd(_f_ssd_post, n_tm, in_sp, p_sp, [(A, SSM_INNER, BF16, TM)], "ssd_post")
    ya = _mm(oa, w["w_pa"], trans_b=False, out_dtype=BF16, name="mm_pa")
    yb = _mm(ob, w["w_pb"], trans_b=False, out_dtype=BF16, name="mm_pb")
    in_mg = [_col(proj, TM, 1024, C_GA // 1024), _col(proj, TM, 1024, C_GB // 1024), _col(ya, TM), _col(yb, TM)]
    (merged,) = _stage_fwd(_f_merge, n_tm, in_mg, [], [(A, D, BF16, TM)], "merge")
    mix = _mm(merged, w["w_out"], trans_b=False, out_dtype=BF16, name="mm_out")

    f_res1 = _f_res1(nct)
    in_r1 = x_in + [_col(mix, TM)]
    p_r1 = [ada, w["norm2_w"]]
    h2, hm2 = _stage_fwd(f_res1, n_tm, in_r1, p_r1, [(A, D, F32, TM), (A, D, BF16, TM)], "res1")
    gu = _mm(hm2, w["wt_gu"], trans_b=True, out_dtype=BF16, name="mm_gu", tn_t=1408)
    in_sw = [_col(gu, TM)]
    (act,) = _stage_fwd(_f_swiglu, n_tm, in_sw, [], [(A, D_FF, BF16, TM)], "swiglu")
    dn = _mm(act, w["w_down"], trans_b=False, out_dtype=BF16, name="mm_down")

    loss_blk, d_h2a, d_dn, d_ada3, g["final_norm_w"] = _loss_head(h2, dn, target, ada, w["final_norm_w"], nct)
    rows16 = lambda gw: (gw.reshape(N_DEV, gw.shape[0] // N_DEV, gw.shape[1]).astype(BF16), True)
    gw_down = _mm_tn(act, d_dn, name="dw_down", tr_t=1408)
    d_act = _mm(d_dn, w["w_down"], trans_b=True, out_dtype=BF16, name="dx_down", tn_t=1408)
    d_gu, got["w_down"] = _stage_bwd(_f_swiglu, n_tm, in_sw, [], [_col(d_act, TM)], lambda i, t: t, [(BF16, None)],
                                     "swiglu_b", hosted=[rows16(gw_down)])
    gwt_gu = _mm_tn(d_gu, hm2, name="dw_gu", tr_t=1408)
    d_hm2 = _mm(d_gu, w["wt_gu"], trans_b=False, out_dtype=BF16, name="dx_gu")
    d_x1, d_mix, d_ada2, g["norm2_w"], got["w_gate"] = _stage_bwd(
        f_res1, n_tm, in_r1, p_r1, [_col(d_h2a, TM), _col(d_hm2, TM)], lambda i, t: t,
        [None, (F32, None), (BF16, None)], "res1_b", hosted=[rows16(gwt_gu[:D_FF])])
    gw_out = _mm_tn(merged, d_mix, name="dw_out")
    d_merged = _mm(d_mix, w["w_out"], trans_b=True, out_dtype=BF16, name="dx_out")
    d_ya, d_yb, d_proj, got["w_up"] = _stage_bwd(
        _f_merge, n_tm, in_mg, [], [_col(d_merged, TM)], lambda i, t: t, [None, None, (BF16, None), (BF16, None)], "merge_b",
        hosted=[rows16(gwt_gu[D_FF:])], pack=([0, 1], None, D_INP, C_GA // (2 * D)))
    gw_pa = _mm_tn(oa, d_ya, name="dw_pa")
    gw_pb = _mm_tn(ob, d_yb, name="dw_pb")
    d_oa = _mm(d_ya, w["w_pa"], trans_b=True, out_dtype=BF16, name="dx_pa")
    d_ob = _mm(d_yb, w["w_pb"], trans_b=True, out_dtype=BF16, name="dx_pb")
    d_og, g["gla_norm_w"], d_proj, got["w_out"] = _stage_bwd(
        _f_gla_post, n_tm, in_gp, p_gp, [_col(d_oa, TM)], lambda i, t: t, [(BF16, None), None, None], "gla_post_b",
        hosted=[rows16(gw_out)], pack=([2], d_proj, D_INP, C_R // D))
    d_ys, d_xs_skip, g["dsk8"], g["ssm_norm_w"], d_proj, got["w_pa"], got["w_pb"] = _stage_bwd(
        _f_ssd_post, n_tm, in_sp, p_sp, [_col(d_ob, TM)], lambda i, t: t,
        [(BF16, None), None, (BF16, None), None], "ssd_post_b", hosted=[rows16(gw_pa), rows16(gw_pb)],
        pack=([3], d_proj, D_INP, C_Z // SSM_INNER))

    dq, dk, dv, d_lgf = _gla_bwd(proj, lg_f, sv_gf, d_og, None, ncc_g, False, "gla_f_b")
    d_proj, d_lgb = _gla_bwd(proj, lg_b, sv_gb, d_og, (dq, dk, dv), ncc_g, True, "gla_b_b", pack=(d_proj, D_INP))
    d_xf, d_bm, d_cm, d_axf, d_atf = _ssd_bwd(xf, xbc, axf, atf, sv_sf, d_ys, None, ncc_s, False, "ssd_f_b")
    d_xb, d_bm, d_cm, d_axb, d_atb = _ssd_bwd(xb, xbc, axb, atb, sv_sb, d_ys, (d_bm, d_cm), ncc_s, True, "ssd_b_b")

    cts_gd = [_col(a, TS) for a in (d_lgf, d_lgb, d_xf, d_xb, d_axf, d_axb)]
    cts_gd += [(a, (n_heads, TS), lambda i: (0, i)) for a in (d_atf, d_atb)]
    d_xs_dt, g["up_f"], g["up_b"], g["gla_bias_f"], g["gla_bias_b"], g["dtb"], g["alog"], d_proj = _stage_bwd(
        _f_gates_dt, n_ts, in_gd, p_gd, cts_gd, lambda i, t: t, [None, (BF16, None)], "gates_dt_b",
        pack=([0], d_proj, D_INP, C_S // 128))
    cts_conv = [_col(d_xs_skip, TM), _col(d_xs_dt, TM), _col(d_bm, TM), _col(d_cm, TM)]
    g["conv_w"], g["conv_b"], d_proj = _stage_bwd(
        f_conv, n_tm, in_conv, p_conv, cts_conv,
        lambda i, t: [jnp.concatenate([t[0] + t[1], t[2], t[3]], axis=1)], [None], "conv_b",
        pack=([0], d_proj, D_INP, C_U // 3072))
    gwt_in = _mm_tn(d_proj, h1, name="dw_in", tr_t=1152)
    d_h1, got["w_in"] = _mm(d_proj, w["wt_in"], trans_b=False, out_dtype=BF16, name="dx_in", tm_t=768, tk_t=3456,
                            hosted=[rows16(_from_padded(gwt_in))])
    grad_x, g["norm1_w"], d_ada1 = _stage_bwd(
        f_norm1, n_tm, x_in, p_norm1, [_col(d_h1, TM)], lambda i, t: t,
        [None, (F32, (d_x1, (TM, D), x_in[1][2]))], "norm1_b")
    return loss_blk, grad_x, g, got, (d_ada1, d_ada2, d_ada3)


def _exchange(x, a2a, name):
    shp = x.shape[1:] if a2a else x.shape

    def body(x_ref, o_ref, send_sems, recv_sems, local_sem):
        ops = _exchange_ops(x_ref, o_ref, send_sems, recv_sems, local_sem, a2a)
        for op in ops:
            op.start()
        for op in ops:
            op.wait()

    return pl.pallas_call(
        body, out_shape=jax.ShapeDtypeStruct((N_DEV,) + tuple(shp), x.dtype),
        in_specs=[pl.BlockSpec(memory_space=pl.ANY)], out_specs=pl.BlockSpec(memory_space=pl.ANY),
        scratch_shapes=[pltpu.SemaphoreType.DMA((N_DEV - 1,)), pltpu.SemaphoreType.DMA((N_DEV - 1,)),
                        pltpu.SemaphoreType.DMA(())],
        name=name,
    )(x)


def _all_gather(x, name):
    def body(x_ref, o_ref, send_sems, recv_sems, local_sem):
        mx, my, mc = lax.axis_index("x"), lax.axis_index("y"), lax.axis_index("c")
        me, sibling = (mx, my, mc), (mx, my, 1 - mc)
        chips = [(1 - mx, my), (mx, 1 - my), (1 - mx, 1 - my)]

        def slot(px, py, pc):
            return o_ref.at[4 * px + 2 * py + pc]

        def copy(k, block, to, src=None):
            return pltpu.make_async_remote_copy(
                src_ref=slot(*block) if src is None else src, dst_ref=slot(*block),
                send_sem=send_sems.at[k], recv_sem=recv_sems.at[k], device_id=to, device_id_type=pl.DeviceIdType.MESH)

        own = pltpu.make_async_copy(x_ref, slot(*me), local_sem)
        own.start()
        first = [copy(0, me, sibling, src=x_ref)] + [copy(1 + j, me, (*chip, mc), src=x_ref) for j, chip in enumerate(chips)]
        for cp in first:
            cp.start()
        passed = [copy(4 + j, (*chip, mc), sibling) for j, chip in enumerate(chips)]
        for j, chip in enumerate(chips):
            copy(1 + j, (*chip, mc), me).wait_recv()
            passed[j].start()
        copy(0, sibling, me).wait_recv()
        for j, chip in enumerate(chips):
            copy(4 + j, (*chip, 1 - mc), me).wait_recv()
        for cp in first + passed:
            cp.wait_send()
        own.wait()

    return pl.pallas_call(
        body, out_shape=jax.ShapeDtypeStruct((N_DEV,) + tuple(x.shape), x.dtype),
        in_specs=[pl.BlockSpec(memory_space=pl.ANY)], out_specs=pl.BlockSpec(memory_space=pl.ANY),
        scratch_shapes=[pltpu.SemaphoreType.DMA((N_DEV - 1,)), pltpu.SemaphoreType.DMA((N_DEV - 1,)),
                        pltpu.SemaphoreType.DMA(())],
        name=name,
    )(x)


def _adamw_math(w, gr, m, v):
    m = ADAM_B1 * m + (1.0 - ADAM_B1) * gr
    v = ADAM_B2 * v + (1.0 - ADAM_B2) * (gr * gr)
    m_hat = m / np.float32(1.0 - ADAM_B1 ** ADAM_STEP)
    v_hat = v / np.float32(1.0 - ADAM_B2 ** ADAM_STEP)
    delta = -ADAM_LR * (m_hat / (jnp.sqrt(v_hat) + ADAM_EPS) + ADAM_WD * w)
    return delta, m, v


def _sum_adamw(parts, w, m, v, name):
    R, C = w.shape
    n_parts = parts.shape[0]
    tr = _pick(R, max(16, (2 << 20) // (4 * C) // 16 * 16), 16)

    def body(p_ref, w_ref, m_ref, v_ref, g_ref, d_ref, mo_ref, vo_ref):
        gr = p_ref[0].astype(F32)
        for k in range(1, n_parts):
            gr = gr + p_ref[k].astype(F32)
        g_ref[...] = gr
        d_ref[...], mo_ref[...], vo_ref[...] = _adamw_math(w_ref[...], gr, m_ref[...], v_ref[...])

    if R % 16 == 0:
        grid = (R // tr,)
        tile, p_spec = pl.BlockSpec((tr, C), lambda i: (i, 0)), pl.BlockSpec((n_parts, tr, C), lambda i: (0, i, 0))
    else:
        tc = _pick(C, max(128, (2 << 20) // (4 * R) // 128 * 128), 128)
        grid = (C // tc,)
        tile, p_spec = pl.BlockSpec((R, tc), lambda i: (0, i)), pl.BlockSpec((n_parts, R, tc), lambda i: (0, 0, i))
    return pl.pallas_call(
        body, grid=grid, in_specs=[p_spec, tile, tile, tile],
        out_specs=[tile] * 4, out_shape=[jax.ShapeDtypeStruct((R, C), F32)] * 4,
        compiler_params=_cparams(("parallel",)), name=name,
    )(parts, w, m, v)


def _sum8(parts, name):
    _, R, C = parts.shape

    def body(p_ref, g_ref):
        gr = p_ref[0]
        for k in range(1, N_DEV):
            gr = gr + p_ref[k]
        g_ref[...] = gr

    return pl.pallas_call(body, out_shape=jax.ShapeDtypeStruct((R, C), F32), name=name)(parts)


def _adamw(w, gr, m, v, name):
    def body(w_ref, g_ref, m_ref, v_ref, d_ref, mo_ref, vo_ref):
        d_ref[...], mo_ref[...], vo_ref[...] = _adamw_math(w_ref[...], g_ref[...], m_ref[...], v_ref[...])

    return pl.pallas_call(body, out_shape=[jax.ShapeDtypeStruct(w.shape, F32)] * 3, name=name)(w, gr, m, v)


def _to_padded(wt_in):
    z = jnp.zeros((32,) + wt_in.shape[1:], wt_in.dtype)
    return jnp.concatenate([wt_in[0:3072], wt_in[5152:8224], wt_in[3104:5152], wt_in[8288:10336],
                            wt_in[3072:3104], wt_in[8224:8288], z], axis=0)


def _from_padded(p):
    return jnp.concatenate([p[0:3072], p[10240:10272], p[6144:8192], p[3072:6144], p[10272:10336], p[8192:10240]], axis=0)


def _unshard_cols(gathered):
    n, r, c = gathered.shape
    return jnp.transpose(gathered, (1, 0, 2)).reshape(r, n * c)


def _lanes(vec, lo):
    return jnp.zeros((1, 128), F32).at[:, lo:lo + vec.shape[1]].set(vec)


_SMALL = (("b_ada", 6 * D), ("c_ctx", D), ("norm1_w", D), ("gla_bias_f", 512), ("gla_bias_b", 512), ("gla_norm_w", 256),
          ("conv_b", 3072), ("dt_bias_f", 32), ("dt_bias_b", 32), ("a_log_f", 32), ("a_log_b", 32), ("d_skip", 32),
          ("ssm_norm_w", 2048), ("norm2_w", D), ("final_norm_w", D))
_SHARDED_SMALL = (("gla_up_f", 16 * 512), ("gla_up_b", 16 * 512), ("conv_w", 4 * 3072))


def _pack(vals, names):
    flat = jnp.concatenate([vals[n].reshape(-1).astype(F32) for n, _ in names])
    pad = (-flat.shape[0]) % 1024
    return jnp.concatenate([flat, jnp.zeros((pad,), F32)]).reshape(-1, 128)


def _unpack(packed, names, shapes):
    flat, out, off = packed.reshape(-1), {}, 0
    for n, size in names:
        out[n] = flat[off:off + size].reshape(shapes[n])
        off += size
    return out


def kernel(x, c, ctx, c_ctx, w_ada, b_ada, norm1_w, w_in, gla_up_f, gla_bias_f, gla_up_b, gla_bias_b, gla_norm_w, conv_w, conv_b, dt_bias_f, dt_bias_b, a_log_f, a_log_b, d_skip, ssm_norm_w, w_pa, w_pb, w_out, norm2_w, w_gate, w_up, w_down, final_norm_w, loss_target, m_c_ctx, m_w_ada, m_b_ada, m_norm1_w, m_w_in, m_gla_up_f, m_gla_bias_f, m_gla_up_b, m_gla_bias_b, m_gla_norm_w, m_conv_w, m_conv_b, m_dt_bias_f, m_dt_bias_b, m_a_log_f, m_a_log_b, m_d_skip, m_ssm_norm_w, m_w_pa, m_w_pb, m_w_out, m_norm2_w, m_w_gate, m_w_up, m_w_down, m_final_norm_w, v_c_ctx, v_w_ada, v_b_ada, v_norm1_w, v_w_in, v_gla_up_f, v_gla_bias_f, v_gla_up_b, v_gla_bias_b, v_gla_norm_w, v_conv_w, v_conv_b, v_dt_bias_f, v_dt_bias_b, v_a_log_f, v_a_log_b, v_d_skip, v_ssm_norm_w, v_w_pa, v_w_pb, v_w_out, v_norm2_w, v_w_gate, v_w_up, v_w_down, v_final_norm_w):
    args = dict(locals())
    me = 4 * lax.axis_index("x") + 2 * lax.axis_index("y") + lax.axis_index("c")

    t_names = ("w_in", "w_gate", "w_up")
    local = {n: (jnp.transpose(args[n][0]) if n in t_names else args[n][0]) for n in
             ("w_in", "w_pa", "w_pb", "w_out", "w_gate", "w_up", "w_down")}
    full = {"wt_in": _to_padded(_all_gather(local["w_in"].astype(BF16), "ag_w_in").reshape(D_IN, D))}
    shards = {n: local[n].astype(BF16) for n in ("w_pa", "w_pb", "w_out", "w_gate", "w_up", "w_down")}
    sm = _exchange(jnp.concatenate([gla_up_f.reshape(-1), gla_up_b.reshape(-1), conv_w.reshape(-1), c.reshape(-1)]).reshape(-1, 128),
                   False, "ag_small")
    sm = sm.reshape(N_DEV, -1)
    n_ada = w_ada.shape[2]
    cc = jnp.zeros((_ADA_ROWS, D), F32).at[0:N_DEV].set(sm[:, 3584:3584 + D]).at[N_DEV].set(c_ctx)
    w_ada16 = w_ada[0].astype(BF16)
    ada_cols = _exchange(_ada_fwd(cc, w_ada16, lax.dynamic_slice(b_ada, (0, me * n_ada), (1, n_ada))), False, "ag_ada")
    ada = jnp.zeros((8, N_DEV * n_ada), F32)
    ada = ada.at[0].set(lax.dynamic_index_in_dim(ada_cols, me, axis=1, keepdims=False).reshape(-1))
    ada = ada.at[1].set(ada_cols[:, N_DEV].reshape(-1))
    up_f = _unshard_cols(sm[:, 0:1024].reshape(N_DEV, 16, 64))
    up_b = _unshard_cols(sm[:, 1024:2048].reshape(N_DEV, 16, 64))
    full["conv_w"] = _unshard_cols(sm[:, 2048:3584].reshape(N_DEV, 4, 384))
    full["up_f"] = jnp.zeros((128, 512), F32).at[0:16].set(up_f)
    full["up_b"] = jnp.zeros((128, 512), F32).at[16:32].set(up_b)
    full["dtb"] = _lanes(dt_bias_f, 32) + _lanes(dt_bias_b, 64)
    full["alog"] = _lanes(a_log_f, 32) + _lanes(a_log_b, 64)
    full["dsk8"] = jnp.zeros((8, 128), F32).at[0:1, 0:32].set(d_skip)
    for n in ("norm1_w", "gla_bias_f", "gla_bias_b", "gla_norm_w", "conv_b", "ssm_norm_w", "norm2_w"):
        full[n] = args[n]
    full["final_norm_w"] = final_norm_w.reshape(1, D)

    loss_blk, grad_x, g, got, d_ada = _local_step(x[0], ctx[0], loss_target[0], ada, full, shards)

    d_all = _exchange(jnp.concatenate([d[0:2] for d in d_ada], axis=0), False, "ag_d_ada").reshape(N_DEV * 6, N_DEV * n_ada)
    gw_ada, gb_ada, d_cc = _ada_bwd(cc, w_ada16, d_all, lax.dynamic_slice(d_all, (0, me * n_ada), (N_DEV * 6, n_ada)))

    gs = dict(g, c_ctx=d_cc[N_DEV], b_ada=jnp.zeros_like(b_ada))
    gs["dt_bias_f"], gs["dt_bias_b"] = g["dtb"][:, 32:64], g["dtb"][:, 64:96]
    gs["a_log_f"], gs["a_log_b"] = g["alog"][:, 32:64], g["alog"][:, 64:96]
    gs["d_skip"] = g["dsk8"][0:1, 0:32]
    gs["gla_up_f"], gs["gla_up_b"] = g["up_f"][0:16], g["up_b"][16:32]
    gs["loss"] = loss_blk[0:1, 0:1]
    names = _SMALL + _SHARDED_SMALL + (("loss", 1),)
    shapes = {n: (args[n].shape if n in args else (1, 1)) for n, _ in names}
    shapes.update({"gla_up_f": (16, 512), "gla_up_b": (16, 512), "conv_w": (4, 3072)})
    red = _unpack(_sum8(_exchange(_pack(gs, names), False, "ag_small_grads"), "sum_small_grads"), names, shapes)
    loss = red["loss"].reshape(())
    grads = {n: red[n] for n, _ in _SMALL}
    grads["b_ada"] = gb_ada
    grads["gla_up_f"] = lax.dynamic_slice(red["gla_up_f"], (0, me * 64), (16, 64))[None]
    grads["gla_up_b"] = lax.dynamic_slice(red["gla_up_b"], (0, me * 64), (16, 64))[None]
    grads["conv_w"] = lax.dynamic_slice(red["conv_w"], (0, me * 384), (4, 384))[None]
    upd_names = tuple((n, s) for n, s in _SMALL) + (("gla_up_f", 1024), ("gla_up_b", 1024), ("conv_w", 1536))
    pk = lambda prefix, src: _pack({n: src[prefix + n] for n, _ in upd_names}, upd_names)
    d_s, m_s, v_s = _adamw(pk("", args), _pack(grads, upd_names), pk("m_", args), pk("v_", args), "adamw_small")
    upd_shapes = {n: args[n].shape for n, _ in upd_names}
    delta = _unpack(d_s, upd_names, upd_shapes)
    new_m = _unpack(m_s, upd_names, upd_shapes)
    new_v = _unpack(v_s, upd_names, upd_shapes)

    got["w_ada"] = gw_ada[None]
    for n, parts in got.items():
        orient = jnp.transpose if n in t_names else (lambda a: a)
        res = _sum_adamw(parts, local[n] if n in local else args[n][0], orient(args["m_" + n][0]), orient(args["v_" + n][0]),
                         "adamw_" + n)
        grads[n], delta[n], new_m[n], new_v[n] = [orient(r)[None] for r in res]

    order = ["c_ctx", "w_ada", "b_ada", "norm1_w", "w_in", "gla_up_f", "gla_bias_f", "gla_up_b", "gla_bias_b", "gla_norm_w",
             "conv_w", "conv_b", "dt_bias_f", "dt_bias_b", "a_log_f", "a_log_b", "d_skip", "ssm_norm_w", "w_pa", "w_pb",
             "w_out", "norm2_w", "w_gate", "w_up", "w_down", "final_norm_w"]
    fix = lambda d: [d[n].reshape(args[n].shape) for n in order]
    return (loss, grad_x[None], *fix(grads), *fix(delta), *fix(new_m), *fix(new_v))
```

```python
import functools

import jax
import jax.numpy as jnp
import numpy as np
from jax import lax
from jax.experimental import pallas as pl
from jax.experimental.pallas import tpu as pltpu

F32 = jnp.float32
BF16 = jnp.bfloat16
N_DEV = 8
D = 1024
EPS = 1e-6
GRID_W = 64
GLA_H, GLA_DK, GLA_DV = 4, 128, 256
GLA_C = 64
GLA_TAU = 16.0
SSM_G, SSM_HPG, SSM_P, SSM_N = 4, 8, 64, 128
SSM_C = 128
SSM_INNER = 2048
D_FF = 2816
D_IN = 10336
D_INP = 10368
C_Q, C_K, C_V, C_R, C_U, C_Z, C_GA, C_GB, C_S = 0, 512, 1024, 2048, 3072, 6144, 8192, 9216, 10240
TM = 256
TS = 128

ADAM_LR, ADAM_B1, ADAM_B2, ADAM_EPS, ADAM_WD, ADAM_STEP = 0.001, 0.9, 0.999, 1e-08, 0.01, 10
VMEM_LIMIT = 56 << 20


def _cparams(sem):
    return pltpu.CompilerParams(dimension_semantics=sem or None, vmem_limit_bytes=VMEM_LIMIT)


def _pick(n, target, mult):
    best = None
    for t in range(mult, min(n, target) + 1, mult):
        if n % t == 0:
            best = t
    return best if best is not None else n


def _dot_impl(a, b, ta, tb):
    dims = (((0 if ta else 1,), (1 if tb else 0,)), ((), ()))
    return lax.dot_general(a.astype(BF16), b.astype(BF16), dims, preferred_element_type=F32)


@functools.partial(jax.custom_vjp, nondiff_argnums=(2, 3))
def _bdot(a, b, ta=False, tb=False):
    return _dot_impl(a, b, ta, tb)


def _bdot_fwd(a, b, ta, tb):
    return _dot_impl(a, b, ta, tb), (a, b)


def _bdot_bwd(ta, tb, res, g):
    a, b = res
    if not ta and not tb:
        return _dot_impl(g, b, False, True), _dot_impl(a, g, True, False)
    if not ta and tb:
        return _dot_impl(g, b, False, False), _dot_impl(g, a, True, False)
    if ta and not tb:
        return _dot_impl(b, g, False, True), _dot_impl(a, g, False, False)
    raise NotImplementedError


_bdot.defvjp(_bdot_fwd, _bdot_bwd)


_NN = (((1,), (0,)), ((), ()))
_NT = (((1,), (1,)), ((), ()))
_TN = (((0,), (0,)), ((), ()))


def _dg2(x, e, x_is_lhs, dims):
    hi = x.astype(BF16)
    lo = (x - hi.astype(F32)).astype(BF16)
    e = e.astype(BF16)
    if x_is_lhs:
        return (lax.dot_general(hi, e, dims, preferred_element_type=F32)
                + lax.dot_general(lo, e, dims, preferred_element_type=F32))
    return (lax.dot_general(e, hi, dims, preferred_element_type=F32)
            + lax.dot_general(e, lo, dims, preferred_element_type=F32))


_EDOT_FWD = {"xe": (True, _NN), "ex": (False, _NN), "ext": (False, _NT)}
_EDOT_BWD = {"xe": (True, _NT), "ex": (False, _TN), "ext": (True, _TN)}


@functools.partial(jax.custom_vjp, nondiff_argnums=(2,))
def _edot(x, e, mode):
    return _dg2(x, e, *_EDOT_FWD[mode])


def _edot_fwd(x, e, mode):
    return _dg2(x, e, *_EDOT_FWD[mode]), e


def _edot_bwd(mode, e, g):
    return _dg2(g, e, *_EDOT_BWD[mode]), None


_edot.defvjp(_edot_fwd, _edot_bwd)


def _shift_impl(u, pos, per, s):
    n = u.shape[0]
    rolled = u if s == 0 else pltpu.roll(u, (-s) % n, 0)
    ok = (pos + s >= 0) & (pos + s < per)
    return jnp.where(ok, rolled, 0.0)


@functools.partial(jax.custom_vjp, nondiff_argnums=(3,))
def _shift(u, pos, per, s):
    return _shift_impl(u, pos, per, s)


def _shift_fwd(u, pos, per, s):
    return _shift_impl(u, pos, per, s), (pos, per)


def _shift_bwd(s, res, g):
    pos, per = res
    return _shift_impl(g, pos, per, -s), None, None


_shift.defvjp(_shift_fwd, _shift_bwd)


def _rms(x, w):
    return x * lax.rsqrt(jnp.mean(x * x, axis=-1, keepdims=True) + EPS) * w


def _silu(x):
    return x * jax.nn.sigmoid(x)


def _softplus(x):
    return jnp.maximum(x, 0.0) + jnp.log(1.0 + jnp.exp(-jnp.abs(x)))


def _logsig(x):
    return jnp.minimum(x, 0.0) - jnp.log(1.0 + jnp.exp(-jnp.abs(x)))


def _tri(n, rev):
    t = lax.broadcasted_iota(jnp.int32, (n, n), 0)
    s = lax.broadcasted_iota(jnp.int32, (n, n), 1)
    return (s >= t) if rev else (t >= s)


def _head_expand(first_lane):
    l = lax.broadcasted_iota(jnp.int32, (128, SSM_INNER), 0)
    c = lax.broadcasted_iota(jnp.int32, (128, SSM_INNER), 1)
    return (l == first_lane + lax.shift_right_logical(c, 6)).astype(F32)


def _exchange_ops(x_ref, o_ref, send_sems, recv_sems, local_sem, a2a):
    mx, my, mc = lax.axis_index("x"), lax.axis_index("y"), lax.axis_index("c")
    me = 4 * mx + 2 * my + mc
    ops = []
    for k in range(1, N_DEV):
        px = 1 - mx if k & 4 else mx
        py = 1 - my if k & 2 else my
        pc = 1 - mc if k & 1 else mc
        ops.append(pltpu.make_async_remote_copy(
            src_ref=x_ref.at[4 * px + 2 * py + pc] if a2a else x_ref, dst_ref=o_ref.at[me],
            send_sem=send_sems.at[k - 1], recv_sem=recv_sems.at[k - 1],
            device_id=(px, py, pc), device_id_type=pl.DeviceIdType.MESH))
    ops.append(pltpu.make_async_copy(x_ref.at[me] if a2a else x_ref, o_ref.at[me], local_sem))
    return ops


def _call(body, *, grid, in_specs, out_specs, out_shape, scratch=(), sem, name, args, hosted=(), aliases=None):
    n_in, n_out, n_s, n_h = len(in_specs), len(out_specs), len(scratch), len(hosted)
    aliases = aliases or {}
    if not n_h:
        return pl.pallas_call(body, grid=grid, in_specs=list(in_specs), out_specs=list(out_specs),
                              out_shape=list(out_shape), scratch_shapes=list(scratch), input_output_aliases=aliases,
                              compiler_params=_cparams(sem), name=name)(*args)
    b0, c0 = n_in + n_h, n_in + n_h + n_out
    d0 = c0 + n_h

    def wrapped(*refs):
        hx, ho = refs[n_in:b0], refs[c0:d0]
        send_sems, recv_sems, local_sems = refs[d0 + n_s:]
        ids = [pl.program_id(ax) for ax in range(len(grid))]
        first = functools.reduce(jnp.logical_and, [i == 0 for i in ids])
        last = functools.reduce(jnp.logical_and, [i == n - 1 for i, n in zip(ids, grid)])

        def ops():
            return [op for n, (_, a2a) in enumerate(hosted)
                    for op in _exchange_ops(hx[n], ho[n], send_sems.at[n], recv_sems.at[n], local_sems.at[n], a2a)]

        @pl.when(first)
        def _():
            for op in ops():
                op.start()

        body(*refs[:n_in], *refs[b0:c0], *refs[d0:d0 + n_s])

        @pl.when(last)
        def _():
            for op in ops():
                op.wait()

    hbm = pl.BlockSpec(memory_space=pl.ANY)
    h_shapes = [jax.ShapeDtypeStruct((N_DEV,) + tuple(x.shape[1:] if a2a else x.shape), x.dtype) for x, a2a in hosted]
    return pl.pallas_call(
        wrapped, grid=grid, in_specs=list(in_specs) + [hbm] * n_h, out_specs=list(out_specs) + [hbm] * n_h,
        out_shape=list(out_shape) + h_shapes, input_output_aliases=aliases,
        scratch_shapes=list(scratch) + [pltpu.SemaphoreType.DMA((n_h, N_DEV - 1)), pltpu.SemaphoreType.DMA((n_h, N_DEV - 1)),
                                        pltpu.SemaphoreType.DMA((n_h,))],
        compiler_params=_cparams(("arbitrary",) * len(grid)), name=name,
    )(*args, *[x for x, _ in hosted])


def _mm(a, b, *, trans_b, out_dtype, name, tm_t=1408, tn_t=1024, tk_t=2816, hosted=()):
    M, K = a.shape
    N = b.shape[0] if trans_b else b.shape[1]
    tm, tn, tk = _pick(M, tm_t, 8), _pick(N, tn_t, 128), _pick(K, tk_t, 128)
    nk = K // tk
    dims = (((1,), (1,)), ((), ())) if trans_b else (((1,), (0,)), ((), ()))

    def body_one(a_ref, b_ref, o_ref):
        o_ref[...] = lax.dot_general(a_ref[...], b_ref[...], dims, preferred_element_type=F32).astype(out_dtype)

    def body_acc(a_ref, b_ref, o_ref, acc_ref):
        k = pl.program_id(2)
        p = lax.dot_general(a_ref[...], b_ref[...], dims, preferred_element_type=F32)

        @pl.when(k == 0)
        def _():
            acc_ref[...] = p

        @pl.when(k > 0)
        def _():
            acc_ref[...] += p

        @pl.when(k == nk - 1)
        def _():
            o_ref[...] = acc_ref[...].astype(out_dtype)

    b_spec = pl.BlockSpec((tn, tk), lambda j, i, k: (j, k)) if trans_b else pl.BlockSpec((tk, tn), lambda j, i, k: (k, j))
    outs = _call(
        body_one if nk == 1 else body_acc, grid=(N // tn, M // tm, nk),
        in_specs=[pl.BlockSpec((tm, tk), lambda j, i, k: (i, k)), b_spec],
        out_specs=[pl.BlockSpec((tm, tn), lambda j, i, k: (i, j))],
        out_shape=[jax.ShapeDtypeStruct((M, N), out_dtype)],
        scratch=[] if nk == 1 else [pltpu.VMEM((tm, tn), F32)],
        sem=("parallel", "parallel", "arbitrary"), name=name, args=(a, b), hosted=hosted)
    return outs if hosted else outs[0]


def _mm_tn(a, b, *, name, tm_t=2816, tr_t=1024, tn_t=1408):
    M, R = a.shape
    N = b.shape[1]
    tm, tr, tn = _pick(M, tm_t, 8), _pick(R, tr_t, 128), _pick(N, tn_t, 128)

    def body(a_ref, b_ref, o_ref):
        m = pl.program_id(2)
        p = lax.dot_general(a_ref[...], b_ref[...], (((0,), (0,)), ((), ())), preferred_element_type=F32)

        @pl.when(m == 0)
        def _():
            o_ref[...] = p

        @pl.when(m > 0)
        def _():
            o_ref[...] += p

    return pl.pallas_call(
        body, grid=(R // tr, N // tn, M // tm),
        in_specs=[pl.BlockSpec((tm, tr), lambda r, j, m: (m, r)), pl.BlockSpec((tm, tn), lambda r, j, m: (m, j))],
        out_specs=pl.BlockSpec((tr, tn), lambda r, j, m: (r, j)),
        out_shape=jax.ShapeDtypeStruct((R, N), F32),
        compiler_params=_cparams(("parallel", "parallel", "arbitrary")), name=name,
    )(a, b)


def _col(arr, tm, width=None, col=0):
    width = arr.shape[1] if width is None else width
    return (arr, (tm, width), lambda i: (i, col))


def _lat(arr, tm, nct):
    return (arr, (tm, arr.shape[1]), lambda i: (jnp.maximum(i - nct, 0), 0))


def _ctx(arr, tm, nct):
    return (arr, (tm, arr.shape[1]), lambda i: (jnp.minimum(i, nct - 1), 0))


def _whole(p):
    return pl.BlockSpec(p.shape, lambda i, nd=p.ndim: (0,) * nd)


def _stage_fwd(f, n_tiles, ins, params, outs, name):
    ni, npar = len(ins), len(params)
    o_specs = [pl.BlockSpec((o[1], o[3]), lambda i: (0, i)) if len(o) > 4 else pl.BlockSpec((o[3], o[1]), lambda i: (i, 0))
               for o in outs]
    o_shapes = [jax.ShapeDtypeStruct((o[1], o[0]) if len(o) > 4 else (o[0], o[1]), o[2]) for o in outs]

    def body(*refs):
        i = pl.program_id(0)
        xs = [r[...].astype(F32) for r in refs[:ni]]
        ps = [r[...] for r in refs[ni:ni + npar]]
        for r, v in zip(refs[ni + npar:], f(i, xs, ps, False)):
            r[...] = v.astype(r.dtype)

    return pl.pallas_call(
        body, grid=(n_tiles,),
        in_specs=[pl.BlockSpec(bs, fn) for _, bs, fn in ins] + [_whole(p) for p in params],
        out_specs=o_specs, out_shape=o_shapes,
        compiler_params=_cparams(("parallel",)), name=name,
    )(*[a for a, _, _ in ins], *params)


def _stage_bwd(f, n_tiles, ins, params, cts, ct_fn, dins, name, hosted=(), pack=None):
    ni, npar, nc = len(ins), len(params), len(cts)
    want = [k for k, d in enumerate(dins) if d is not None]
    extras = [dins[k][1] for k in want if dins[k][1] is not None]
    n_buf = 0 if pack is None or pack[1] is None else 1
    n_main_in = ni + npar + nc + len(extras)

    def body(*refs):
        i = pl.program_id(0)
        xs = [r[...].astype(F32) for r in refs[:ni]]
        ps = [r[...] for r in refs[ni:ni + npar]]
        ct_tiles = [r[...].astype(F32) for r in refs[ni + npar:ni + npar + nc]]
        ex_refs = list(refs[ni + npar + nc:n_main_in])
        out_refs = refs[n_main_in + n_buf:]
        _, vjp = jax.vjp(lambda xs_, ps_: tuple(f(i, xs_, ps_, True)), xs, ps)
        dxs, dps = vjp(tuple(ct_fn(i, ct_tiles)))
        for n, k in enumerate(want):
            v = dxs[k]
            if dins[k][1] is not None:
                v = v + ex_refs.pop(0)[...].astype(F32)
            out_refs[n][...] = v.astype(out_refs[n].dtype)
        if pack is not None:
            packed = [dxs[k] for k in pack[0]]
            r = out_refs[len(want) + npar]
            r[...] = (packed[0] if len(packed) == 1 else jnp.concatenate(packed, axis=1)).astype(r.dtype)
        for r, v in zip(out_refs[len(want):len(want) + npar], dps):
            @pl.when(i == 0)
            def _(r=r, v=v):
                r[...] = v

            @pl.when(i > 0)
            def _(r=r, v=v):
                r[...] += v

    in_specs = ([pl.BlockSpec(bs, fn) for _, bs, fn in ins] + [_whole(p) for p in params]
                + [pl.BlockSpec(bs, fn) for _, bs, fn in cts] + [pl.BlockSpec(bs, fn) for _, bs, fn in extras])
    out_specs = ([pl.BlockSpec(ins[k][1], lambda i, fn=ins[k][2]: (fn(i)[0], 0)) for k in want] + [_whole(p) for p in params])
    out_shape = ([jax.ShapeDtypeStruct((ins[k][0].shape[0], ins[k][1][1]), dins[k][0]) for k in want]
                 + [jax.ShapeDtypeStruct(p.shape, F32) for p in params])
    args = [*[a for a, _, _ in ins], *params, *[a for a, _, _ in cts], *[a for a, _, _ in extras]]
    aliases = None
    if pack is not None:
        ks, buf, total_cols, block = pack
        width = sum(ins[k][1][1] for k in ks)
        out_specs.append(pl.BlockSpec((ins[ks[0]][1][0], width), lambda i, fn=ins[ks[0]][2]: (fn(i)[0], block)))
        out_shape.append(jax.ShapeDtypeStruct((ins[ks[0]][0].shape[0], total_cols), BF16))
        if buf is not None:
            in_specs.append(pl.BlockSpec(memory_space=pl.ANY))
            args.append(buf)
            aliases = {len(args) - 1: len(out_specs) - 1}
    return _call(body, grid=(n_tiles,), in_specs=in_specs, out_specs=out_specs, out_shape=out_shape,
                 sem=("arbitrary",), name=name, args=args, hosted=hosted, aliases=aliases)


def _mod_rows(i, nct, ada, lo):
    return jnp.where(i < nct, ada[1:2, lo:lo + D], ada[0:1, lo:lo + D])


def _f_norm1(nct):
    def f(i, xs, ps, diff):
        ctx, x = xs
        nw, ada = ps
        xt = jnp.where(i < nct, ctx, x)
        return (_rms(xt, nw) * (1.0 + _mod_rows(i, nct, ada, D)) + _mod_rows(i, nct, ada, 0),)
    return f


def _f_gates_dt(i, xs, ps, diff):
    small, xc = xs
    up_f, up_b, gb_f, gb_b, dtb, alog = ps
    dot = _bdot if diff else _dot_impl_nn
    lg_f = _logsig(dot(small, up_f) + gb_f) * (1.0 / GLA_TAU)
    lg_b = _logsig(dot(small, up_b) + gb_b) * (1.0 / GLA_TAU)
    dtp = _softplus(small + dtb)
    aa = -jnp.exp(alog) * dtp
    e_f, e_b = _head_expand(32), _head_expand(64)
    n = small.shape[0]
    t = lax.broadcasted_iota(jnp.int32, (n, n), 0)
    s = lax.broadcasted_iota(jnp.int32, (n, n), 1)
    same = lax.shift_right_logical(t, 6) == lax.shift_right_logical(s, 6)
    pre_g, suf_g = ((t >= s) & same).astype(F32), ((s >= t) & same).astype(F32)
    pre_s, suf_s = (t >= s).astype(F32), (s >= t).astype(F32)
    cum_f, cum_b = _edot(aa, pre_s, "ex"), _edot(aa, suf_s, "ex")
    h = lax.broadcasted_iota(jnp.int32, (SSM_G * SSM_HPG, 128), 0)
    l = lax.broadcasted_iota(jnp.int32, (SSM_G * SSM_HPG, 128), 1)
    cum_ft = _edot(cum_f, (l == h + 32).astype(F32), "ext")
    cum_bt = _edot(cum_b, (l == h + 64).astype(F32), "ext")
    return (_edot(lg_f, pre_g, "ex"), _edot(lg_b, suf_g, "ex"), xc * _bdot(dtp, e_f), xc * _bdot(dtp, e_b),
            _edot(cum_f, e_f, "xe"), _edot(cum_b, e_b, "xe"), cum_ft, cum_bt)


def _dot_impl_nn(a, b):
    return _dot_impl(a, b, False, False)


def _f_conv(nct, tc):
    def f(i, xs, ps, diff):
        (u,) = xs
        cw, cb = ps
        n = u.shape[0]
        t = lax.broadcasted_iota(jnp.int32, (n, 1), 0)
        per = jnp.where(i < nct, tc, GRID_W)
        pos = jnp.bitwise_and(t, per - 1)
        sh = _shift if diff else _shift_impl
        acc = cb + sh(u, pos, per, -2) * cw[0:1]
        for j in range(1, 4):
            acc = acc + sh(u, pos, per, j - 2) * cw[j:j + 1]
        return (_silu(acc),)
    return f


def _f_gla_post(i, xs, ps, diff):
    ogf, ogb, r = xs
    (gw,) = ps
    o = ogf + ogb
    parts = [_rms(o[:, h * GLA_DV:(h + 1) * GLA_DV], gw) for h in range(GLA_H)]
    return (jnp.concatenate(parts, axis=1) * _silu(r),)


def _f_ssd_post(i, xs, ps, diff):
    yf, yb, xc, z = xs
    dsk8, nw = ps
    dsk = _edot(dsk8, _head_expand(0), "xe")[0:1]
    y = (yf + yb + dsk * xc) * _silu(z)
    w = SSM_INNER // SSM_G
    parts = [_rms(y[:, g * w:(g + 1) * w], nw[:, g * w:(g + 1) * w]) for g in range(SSM_G)]
    return (jnp.concatenate(parts, axis=1),)


def _f_merge(i, xs, ps, diff):
    ga, gb, ya, yb = xs
    return (jax.nn.sigmoid(ga) * ya + jax.nn.sigmoid(gb) * yb,)


def _f_res1(nct):
    def f(i, xs, ps, diff):
        ctx, x, mix = xs
        ada, nw = ps
        h2 = jnp.where(i < nct, ctx, x) + _mod_rows(i, nct, ada, 2 * D) * mix
        return (h2, _rms(h2, nw) * (1.0 + _mod_rows(i, nct, ada, 4 * D)) + _mod_rows(i, nct, ada, 3 * D))
    return f


def _f_swiglu(i, xs, ps, diff):
    (gu,) = xs
    return (_silu(gu[:, :D_FF]) * gu[:, D_FF:],)


def _loss_head(h2, dn, tgt, ada, fw, nct):
    A = h2.shape[0]
    n_tiles = A // TM

    def tile_loss(i, h2t, dnt, tg, ada_, fw_):
        h3 = h2t + ada_[0:1, 5 * D:6 * D] * dnt
        err = _rms(h3, fw_) - tg
        row = 0.5 * jnp.mean(err * err, axis=-1, keepdims=True)
        return jnp.sum(row, axis=0, keepdims=True) * jnp.where(i < nct, 0.0, 1.0)

    def body(h2_ref, dn_ref, tg_ref, ada_ref, fw_ref, loss_ref, dh_ref, ddn_ref, dada_ref, dfw_ref):
        i = pl.program_id(0)
        val, vjp = jax.vjp(functools.partial(tile_loss, i), h2_ref[...], dn_ref[...].astype(F32), tg_ref[...], ada_ref[...],
                           fw_ref[...])
        dh, ddn, _, dada, dfw = vjp(jnp.ones((1, 1), F32))
        dh_ref[...] = dh
        ddn_ref[...] = ddn.astype(BF16)
        lv = jnp.broadcast_to(val, loss_ref.shape)
        for r, v in ((loss_ref, lv), (dada_ref, dada), (dfw_ref, dfw)):
            @pl.when(i == 0)
            def _(r=r, v=v):
                r[...] = v

            @pl.when(i > 0)
            def _(r=r, v=v):
                r[...] += v

    row = lambda i: (i, 0)
    return pl.pallas_call(
        body, grid=(n_tiles,),
        in_specs=[pl.BlockSpec((TM, D), row), pl.BlockSpec((TM, D), row),
                  pl.BlockSpec((TM, D), lambda i: (jnp.maximum(i - nct, 0), 0)), _whole(ada), _whole(fw)],
        out_specs=[pl.BlockSpec((8, 128), lambda i: (0, 0)), pl.BlockSpec((TM, D), row), pl.BlockSpec((TM, D), row),
                   _whole(ada), _whole(fw)],
        out_shape=[jax.ShapeDtypeStruct((8, 128), F32), jax.ShapeDtypeStruct((A, D), F32),
                   jax.ShapeDtypeStruct((A, D), BF16), jax.ShapeDtypeStruct(ada.shape, F32),
                   jax.ShapeDtypeStruct(fw.shape, F32)],
        compiler_params=_cparams(("arbitrary",)), name="loss_head",
    )(h2, dn, tgt, ada, fw)


def _chunk_of(step, n_chunks, n_ctx_chunks, rev):
    if not rev:
        return step
    return jnp.where(step < n_ctx_chunks, n_ctx_chunks - 1 - step, n_chunks - 1 - (step - n_ctx_chunks))


def _gla_step(st, q, k, v, b, rev):
    tri = _tri(GLA_C, rev).astype(F32)
    last = 0 if rev else GLA_C - 1
    outs, sts = [], []
    for h in range(GLA_H):
        kk = slice(h * GLA_DK, (h + 1) * GLA_DK)
        vv = slice(h * GLA_DV, (h + 1) * GLA_DV)
        qh, kh, vh, bh, sh = q[:, kk] * (GLA_DK ** -0.5), k[:, kk], v[:, vv], b[:, kk], st[vv]
        tot = bh[last:last + 1]
        mid = bh[GLA_C // 2:GLA_C // 2 + 1]
        att = _bdot(qh * jnp.exp(bh - mid), kh * jnp.exp(mid - bh), False, True) * tri
        outs.append(_bdot(att, vh) + _bdot(qh * jnp.exp(bh), sh, False, True))
        sts.append(sh * jnp.exp(tot) + _bdot(vh, kh * jnp.exp(tot - bh), True, False))
    return jnp.concatenate(sts, axis=0), jnp.concatenate(outs, axis=1)


_GQK, _GV = GLA_H * GLA_DK, GLA_H * GLA_DV


GLA_CPS = 2
_GB = GLA_C * GLA_CPS


def _gla_order(rev):
    return list(reversed(range(GLA_CPS))) if rev else list(range(GLA_CPS))


def _gla_fwd(proj, bg_f, bg_b, ncc, name):
    A = proj.shape[0]
    nb = A // _GB
    chs = [lambda s, rev=rev: _chunk_of(s, nb, ncc // GLA_CPS, rev) for rev in (False, True)]

    def body(*refs):
        ins, outs, sts = refs[:8], refs[8:12], refs[12:]

        for d, rev in enumerate((False, True)):
            q_ref, k_ref, v_ref, b_ref = ins[4 * d:4 * d + 4]
            o_ref, ss_ref = outs[2 * d:2 * d + 2]
            st_ref = sts[d]

            @pl.when(pl.program_id(0) == 0)
            def _(st_ref=st_ref):
                st_ref[...] = jnp.zeros_like(st_ref)

            st = st_ref[...]
            for pos, sub in enumerate(_gla_order(rev)):
                rows = slice(sub * GLA_C, (sub + 1) * GLA_C)
                ss_ref[0, pos] = st
                st, o = _gla_step(st, q_ref[rows].astype(F32), k_ref[rows].astype(F32), v_ref[rows].astype(F32),
                                  b_ref[rows], rev)
                o_ref[rows] = o.astype(o_ref.dtype)
            st_ref[...] = st

    in_specs, out_specs = [], []
    for ch in chs:
        in_specs += [pl.BlockSpec((_GB, _GQK), lambda s, ch=ch: (ch(s), C_Q // _GQK)),
                     pl.BlockSpec((_GB, _GQK), lambda s, ch=ch: (ch(s), C_K // _GQK)),
                     pl.BlockSpec((_GB, _GV), lambda s, ch=ch: (ch(s), C_V // _GV)),
                     pl.BlockSpec((_GB, _GQK), lambda s, ch=ch: (ch(s), 0))]
        out_specs += [pl.BlockSpec((_GB, _GV), lambda s, ch=ch: (ch(s), 0)),
                      pl.BlockSpec((1, GLA_CPS, _GV, GLA_DK), lambda s: (s, 0, 0, 0))]
    return pl.pallas_call(
        body, grid=(nb,), in_specs=in_specs, out_specs=out_specs,
        out_shape=[jax.ShapeDtypeStruct((A, _GV), BF16), jax.ShapeDtypeStruct((nb, GLA_CPS, _GV, GLA_DK), F32)] * 2,
        scratch_shapes=[pltpu.VMEM((_GV, GLA_DK), F32)] * 2,
        compiler_params=_cparams(("arbitrary",)), name=name,
    )(proj, proj, proj, bg_f, proj, proj, proj, bg_b)


def _gla_bwd(proj, lg, saved, d_o, prev, ncc, rev, name, pack=None):
    A = proj.shape[0]
    nb = A // _GB
    st_of = lambda r: nb - 1 - r
    ch = lambda r: _chunk_of(st_of(r), nb, ncc // GLA_CPS, rev)
    qs = pl.BlockSpec((_GB, _GQK), lambda r: (ch(r), 0))
    vs = pl.BlockSpec((_GB, _GV), lambda r: (ch(r), 0))
    n_prev = 0 if prev is None else 3

    n_buf = 0 if pack is None else 1

    def body(*refs):
        q_ref, k_ref, v_ref, b_ref, ss_ref, do_ref = refs[:6]
        p_refs = refs[6:6 + n_prev]
        out_refs = refs[6 + n_prev + n_buf:]
        ds_ref = out_refs[-1]

        @pl.when(pl.program_id(0) == 0)
        def _():
            ds_ref[...] = jnp.zeros_like(ds_ref)

        ds = ds_ref[...]
        for pos, sub in reversed(list(enumerate(_gla_order(rev)))):
            rows = slice(sub * GLA_C, (sub + 1) * GLA_C)
            _, vjp = jax.vjp(functools.partial(_gla_step, rev=rev), ss_ref[0, pos], q_ref[rows].astype(F32),
                             k_ref[rows].astype(F32), v_ref[rows].astype(F32), b_ref[rows])
            ds, dq, dk, dv, db = vjp((ds, do_ref[rows].astype(F32)))
            if n_prev:
                dq, dk, dv = [d + p[rows].astype(F32) for d, p in zip((dq, dk, dv), p_refs)]
            if pack is None:
                for r, d in zip(out_refs[:4], (dq, dk, dv, db)):
                    r[rows] = d.astype(r.dtype)
            else:
                out_refs[0][rows] = jnp.concatenate([dq, dk, dv], axis=1).astype(out_refs[0].dtype)
                out_refs[1][rows] = db
        ds_ref[...] = ds

    in_specs = [
        pl.BlockSpec((_GB, _GQK), lambda r: (ch(r), C_Q // _GQK)),
        pl.BlockSpec((_GB, _GQK), lambda r: (ch(r), C_K // _GQK)),
        pl.BlockSpec((_GB, _GV), lambda r: (ch(r), C_V // _GV)),
        qs,
        pl.BlockSpec((1, GLA_CPS, _GV, GLA_DK), lambda r: (st_of(r), 0, 0, 0)),
        vs,
    ] + ([qs, qs, vs] if n_prev else [])
    args = [proj, proj, proj, lg, saved, d_o, *(prev or ())]
    if pack is None:
        out_specs = [qs, qs, vs, qs]
        out_shape = [jax.ShapeDtypeStruct((A, _GQK), BF16), jax.ShapeDtypeStruct((A, _GQK), BF16),
                     jax.ShapeDtypeStruct((A, _GV), BF16), jax.ShapeDtypeStruct((A, _GQK), F32)]
        aliases = None
    else:
        in_specs.append(pl.BlockSpec(memory_space=pl.ANY))
        args.append(pack[0])
        out_specs = [pl.BlockSpec((_GB, 2 * _GQK + _GV), lambda r: (ch(r), 0)), qs]
        out_shape = [jax.ShapeDtypeStruct((A, pack[1]), BF16), jax.ShapeDtypeStruct((A, _GQK), F32)]
        aliases = {len(args) - 1: 0}
    return _call(body, grid=(nb,), in_specs=in_specs, out_specs=out_specs, out_shape=out_shape,
                 scratch=[pltpu.VMEM((_GV, GLA_DK), F32)], sem=("arbitrary",), name=name, args=args, aliases=aliases)


def _ssd_step(st, x, bm, cm, cum, cum_t, rev):
    mask = _tri(SSM_C, rev)
    last = 0 if rev else SSM_C - 1
    tot = cum[last:last + 1]
    cb = _bdot(cm, bm, False, True)
    ys = []
    for e in range(SSM_HPG):
        lm = jnp.exp(jnp.where(mask, cum[:, e * SSM_P:e * SSM_P + 1] - cum_t[e:e + 1], -jnp.inf))
        ys.append(_bdot(cb * lm, x[:, e * SSM_P:(e + 1) * SSM_P]))
    y = jnp.concatenate(ys, axis=1) + _bdot(cm, st) * jnp.exp(cum)
    st_new = st * jnp.exp(tot) + _bdot(bm, x * jnp.exp(tot - cum), True, False)
    return st_new, y


SSM_GPS = 4
_W = SSM_HPG * SSM_P
_XBC_B = SSM_INNER // (SSM_N * SSM_GPS)
_XBC_C = _XBC_B + SSM_G // SSM_GPS


def _ssd_steps(st, x, bm, cm, cum, cum_t, rev):
    outs = [_ssd_step(st[j * SSM_N:(j + 1) * SSM_N], x[:, j * _W:(j + 1) * _W], bm[:, j * SSM_N:(j + 1) * SSM_N],
                      cm[:, j * SSM_N:(j + 1) * SSM_N], cum[:, j * _W:(j + 1) * _W],
                      cum_t[j * SSM_HPG:(j + 1) * SSM_HPG], rev) for j in range(SSM_GPS)]
    return jnp.concatenate([o[0] for o in outs], axis=0), jnp.concatenate([o[1] for o in outs], axis=1)


def _ssd_fwd(x_f, x_b, xbc, cum_f, cum_b, cum_tf, cum_tb, ncc, name, hosted=()):
    A = x_f.shape[0]
    nc = A // SSM_C
    ng = SSM_G // SSM_GPS
    chs = [lambda s, rev=rev: _chunk_of(s, nc, ncc, rev) for rev in (False, True)]

    def body(*refs):
        ins, outs, sts = refs[:10], refs[10:14], refs[14:]
        for d, rev in enumerate((False, True)):
            x_ref, b_ref, c_ref, a_ref, at_ref = ins[5 * d:5 * d + 5]
            y_ref, ss_ref = outs[2 * d:2 * d + 2]
            st_ref = sts[d]

            @pl.when(pl.program_id(1) == 0)
            def _(st_ref=st_ref):
                st_ref[...] = jnp.zeros_like(st_ref)

            st = st_ref[...]
            ss_ref[0, 0] = st
            st_new, y = _ssd_steps(st, x_ref[...].astype(F32), b_ref[...].astype(F32), c_ref[...].astype(F32), a_ref[...],
                                   at_ref[...], rev)
            y_ref[...] = y.astype(y_ref.dtype)
            st_ref[...] = st_new

    in_specs, out_specs = [], []
    for ch in chs:
        gs = pl.BlockSpec((SSM_C, _W * SSM_GPS), lambda g, s, ch=ch: (ch(s), g))
        in_specs += [gs, pl.BlockSpec((SSM_C, SSM_N * SSM_GPS), lambda g, s, ch=ch: (ch(s), _XBC_B + g)),
                     pl.BlockSpec((SSM_C, SSM_N * SSM_GPS), lambda g, s, ch=ch: (ch(s), _XBC_C + g)), gs,
                     pl.BlockSpec((SSM_HPG * SSM_GPS, SSM_C), lambda g, s, ch=ch: (g, ch(s)))]
        out_specs += [gs, pl.BlockSpec((1, 1, SSM_N * SSM_GPS, _W), lambda g, s: (g, s, 0, 0))]
    return _call(
        body, grid=(ng, nc), in_specs=in_specs, out_specs=out_specs,
        out_shape=[jax.ShapeDtypeStruct((A, SSM_INNER), BF16),
                   jax.ShapeDtypeStruct((ng, nc, SSM_N * SSM_GPS, _W), F32)] * 2,
        scratch=[pltpu.VMEM((SSM_N * SSM_GPS, _W), F32)] * 2,
        sem=("parallel", "arbitrary"), name=name,
        args=(x_f, xbc, xbc, cum_f, cum_tf, x_b, xbc, xbc, cum_b, cum_tb), hosted=hosted)


def _ssd_bwd(xd, xbc, cum, cum_t, saved, d_y, prev, ncc, rev, name):
    A = xd.shape[0]
    nc = A // SSM_C
    ng = SSM_G // SSM_GPS
    st_of = lambda r: nc - 1 - r
    ch = lambda r: _chunk_of(st_of(r), nc, ncc, rev)
    gs = pl.BlockSpec((SSM_C, _W * SSM_GPS), lambda g, r: (ch(r), g))
    ns = pl.BlockSpec((SSM_C, SSM_N * SSM_GPS), lambda g, r: (ch(r), g))
    ts = pl.BlockSpec((SSM_HPG * SSM_GPS, SSM_C), lambda g, r: (g, ch(r)))
    n_prev = 0 if prev is None else 2

    def body(*refs):
        x_ref, b_ref, c_ref, a_ref, at_ref, ss_ref, dy_ref = refs[:7]
        p_refs = refs[7:7 + n_prev]
        dx_ref, db_ref, dc_ref, da_ref, dat_ref, ds_ref = refs[7 + n_prev:]

        @pl.when(pl.program_id(1) == 0)
        def _():
            ds_ref[...] = jnp.zeros_like(ds_ref)

        _, vjp = jax.vjp(functools.partial(_ssd_steps, rev=rev), ss_ref[0, 0], x_ref[...].astype(F32), b_ref[...].astype(F32),
                         c_ref[...].astype(F32), a_ref[...], at_ref[...])
        ds, dx, db, dc, da, dat = vjp((ds_ref[...], dy_ref[...].astype(F32)))
        if n_prev:
            db, dc = db + p_refs[0][...].astype(F32), dc + p_refs[1][...].astype(F32)
        for r, d in zip((dx_ref, db_ref, dc_ref, da_ref, dat_ref), (dx, db, dc, da, dat)):
            r[...] = d.astype(r.dtype)
        ds_ref[...] = ds

    in_specs = [gs, pl.BlockSpec((SSM_C, SSM_N * SSM_GPS), lambda g, r: (ch(r), _XBC_B + g)),
                pl.BlockSpec((SSM_C, SSM_N * SSM_GPS), lambda g, r: (ch(r), _XBC_C + g)), gs, ts,
                pl.BlockSpec((1, 1, SSM_N * SSM_GPS, _W), lambda g, r: (g, st_of(r), 0, 0)), gs] + ([ns, ns] if n_prev else [])
    return pl.pallas_call(
        body, grid=(ng, nc), in_specs=in_specs, out_specs=[gs, ns, ns, gs, ts],
        out_shape=[jax.ShapeDtypeStruct((A, SSM_INNER), BF16), jax.ShapeDtypeStruct((A, SSM_G * SSM_N), BF16),
                   jax.ShapeDtypeStruct((A, SSM_G * SSM_N), BF16), jax.ShapeDtypeStruct((A, SSM_INNER), F32),
                   jax.ShapeDtypeStruct((SSM_G * SSM_HPG, A), F32)],
        scratch_shapes=[pltpu.VMEM((SSM_N * SSM_GPS, _W), F32)],
        compiler_params=_cparams(("parallel", "arbitrary")), name=name,
    )(xd, xbc, xbc, cum, cum_t, saved, d_y, *(prev or ()))


_ADA_ROWS = 16


def _ada_fwd(cc, w_shard, b_shard):
    n = w_shard.shape[1]

    def body(cc_ref, w_ref, b_ref, o_ref):
        o_ref[...] = _dot_impl(_silu(cc_ref[...]), w_ref[...], False, False) + b_ref[...]

    return pl.pallas_call(body, out_shape=jax.ShapeDtypeStruct((_ADA_ROWS, n), F32),
                          compiler_params=_cparams(()), name="ada_fwd")(cc, w_shard, b_shard)


def _ada_bwd(cc, w_shard, d_all, d_mine):
    n = w_shard.shape[1]

    def rows(ref, r):
        parts = []
        for s in range(N_DEV):
            parts.append(ref[6 * s + r:6 * s + r + 1] + ref[6 * s + 2 + r:6 * s + 3 + r] + ref[6 * s + 4 + r:6 * s + 5 + r])
        return parts

    def total(parts):
        t = parts[0]
        for p in parts[1:]:
            t = t + p
        return t

    def body(cc_ref, w_ref, da_ref, dm_ref, dw_ref, db_ref, dcc_ref):
        dd = jnp.concatenate(rows(dm_ref, 0) + [total(rows(dm_ref, 1)), jnp.zeros((_ADA_ROWS - N_DEV - 1, n), F32)], axis=0)
        cc = cc_ref[...]
        dw_ref[...] = _dot_impl(_silu(cc), dd, True, False)
        db_ref[...] = total(rows(da_ref, 0)) + total(rows(da_ref, 1))
        _, vjp = jax.vjp(_silu, cc)
        dcc_ref[...] = vjp(_dot_impl(dd, w_ref[...], False, True))[0]

    return pl.pallas_call(
        body, out_shape=[jax.ShapeDtypeStruct((D, n), F32), jax.ShapeDtypeStruct((1, d_all.shape[1]), F32),
                         jax.ShapeDtypeStruct((_ADA_ROWS, D), F32)],
        compiler_params=_cparams(()), name="ada_bwd")(cc, w_shard, d_all, d_mine)


def _local_step(x, ctx, target, ada, w, shards):
    T, Tc = x.shape[0], ctx.shape[0]
    assert Tc == TM and T % TM == 0 and GRID_W == GLA_C and TS == SSM_C == 2 * GLA_C
    A = T + Tc
    nct = Tc // TM
    n_tm, n_ts = A // TM, A // TS
    g = {}

    x_in = [_ctx(ctx, TM, nct), _lat(x, TM, nct)]
    f_norm1 = _f_norm1(nct)
    p_norm1 = [w["norm1_w"], ada]
    (h1,) = _stage_fwd(f_norm1, n_tm, x_in, p_norm1, [(A, D, BF16, TM)], "norm1")
    proj, ag_pa, ag_pb, ag_out = _mm(h1, w["wt_in"], trans_b=True, out_dtype=BF16, name="mm_in", tn_t=1152,
                                     hosted=[(shards[n], False) for n in ("w_pa", "w_pb", "w_out")])
    small = _mm(h1, w["wt_in"][C_S:], trans_b=True, out_dtype=F32, name="mm_in_small")

    p_gd = [w["up_f"], w["up_b"], w["gla_bias_f"], w["gla_bias_b"], w["dtb"], w["alog"]]
    f_conv = _f_conv(nct, Tc)
    p_conv = [w["conv_w"], w["conv_b"]]
    in_conv = [_col(proj, TM, 3072, C_U // 3072)]
    (xbc,) = _stage_fwd(f_conv, n_tm, in_conv, p_conv, [(A, 3072, BF16, TM)], "conv")
    in_gd = [_col(small, TS), _col(xbc, TS, SSM_INNER, 0)]
    n_heads = SSM_G * SSM_HPG
    lg_f, lg_b, xf, xb, axf, axb, atf, atb = _stage_fwd(
        _f_gates_dt, n_ts, in_gd, p_gd,
        [(A, 512, F32, TS)] * 2 + [(A, SSM_INNER, BF16, TS)] * 2 + [(A, SSM_INNER, F32, TS)] * 2
        + [(A, n_heads, F32, TS, True)] * 2,
        "gates_dt")

    ncc_g, ncc_s = Tc // GLA_C, Tc // SSM_C
    ogf, sv_gf, ogb, sv_gb = _gla_fwd(proj, lg_f, lg_b, ncc_g, "gla_fb")
    ysf, sv_sf, ysb, sv_sb, ag_gate, ag_up, ag_down = _ssd_fwd(
        xf, xb, xbc, axf, axb, atf, atb, ncc_s, "ssd_fb", hosted=[(shards[n], False) for n in ("w_gate", "w_up", "w_down")])
    w = dict(w, w_pa=ag_pa.reshape(D, D), w_pb=ag_pb.reshape(SSM_INNER, D), w_out=ag_out.reshape(D, D),
             wt_gu=jnp.concatenate([ag_gate.reshape(D_FF, D), ag_up.reshape(D_FF, D)], axis=0), w_down=ag_down.reshape(D_FF, D))
    got = {}

    in_gp = [_col(ogf, TM), _col(ogb, TM), _col(proj, TM, 1024, C_R // 1024)]
    p_gp = [w["gla_norm_w"]]
    (oa,) = _stage_fwd(_f_gla_post, n_tm, in_gp, p_gp, [(A, D, BF16, TM)], "gla_post")
    in_sp = [_col(ysf, TM), _col(ysb, TM), _col(xbc, TM, SSM_INNER, 0), _col(proj, TM, SSM_INNER, C_Z // SSM_INNER)]
    p_sp = [w["dsk8"], w["ssm_norm_w"]]
    (ob,) = _stage_fwd(_f_ssd_post, n_tm, in_sp, p_sp, [(A, SSM_INNER, BF16, TM)], "ssd_post")
    ya = _mm(oa, w["w_pa"], trans_b=False, out_dtype=BF16, name="mm_pa")
    yb = _mm(ob, w["w_pb"], trans_b=False, out_dtype=BF16, name="mm_pb")
    in_mg = [_col(proj, TM, 1024, C_GA // 1024), _col(proj, TM, 1024, C_GB // 1024), _col(ya, TM), _col(yb, TM)]
    (merged,) = _stage_fwd(_f_merge, n_tm, in_mg, [], [(A, D, BF16, TM)], "merge")
    mix = _mm(merged, w["w_out"], trans_b=False, out_dtype=BF16, name="mm_out")

    f_res1 = _f_res1(nct)
    in_r1 = x_in + [_col(mix, TM)]
    p_r1 = [ada, w["norm2_w"]]
    h2, hm2 = _stage_fwd(f_res1, n_tm, in_r1, p_r1, [(A, D, F32, TM), (A, D, BF16, TM)], "res1")
    gu = _mm(hm2, w["wt_gu"], trans_b=True, out_dtype=BF16, name="mm_gu", tn_t=1408)
    in_sw = [_col(gu, TM)]
    (act,) = _stage_fwd(_f_swiglu, n_tm, in_sw, [], [(A, D_FF, BF16, TM)], "swiglu")
    dn = _mm(act, w["w_down"], trans_b=False, out_dtype=BF16, name="mm_down")

    loss_blk, d_h2a, d_dn, d_ada3, g["final_norm_w"] = _loss_head(h2, dn, target, ada, w["final_norm_w"], nct)
    rows16 = lambda gw: (gw.reshape(N_DEV, gw.shape[0] // N_DEV, gw.shape[1]).astype(BF16), True)
    gw_down = _mm_tn(act, d_dn, name="dw_down", tr_t=1408)
    d_act = _mm(d_dn, w["w_down"], trans_b=True, out_dtype=BF16, name="dx_down", tn_t=1408)
    d_gu, got["w_down"] = _stage_bwd(_f_swiglu, n_tm, in_sw, [], [_col(d_act, TM)], lambda i, t: t, [(BF16, None)],
                                     "swiglu_b", hosted=[rows16(gw_down)])
    gwt_gu = _mm_tn(d_gu, hm2, name="dw_gu", tr_t=1408)
    d_hm2 = _mm(d_gu, w["wt_gu"], trans_b=False, out_dtype=BF16, name="dx_gu")
    d_x1, d_mix, d_ada2, g["norm2_w"], got["w_gate"] = _stage_bwd(
        f_res1, n_tm, in_r1, p_r1, [_col(d_h2a, TM), _col(d_hm2, TM)], lambda i, t: t,
        [None, (F32, None), (BF16, None)], "res1_b", hosted=[rows16(gwt_gu[:D_FF])])
    gw_out = _mm_tn(merged, d_mix, name="dw_out")
    d_merged = _mm(d_mix, w["w_out"], trans_b=True, out_dtype=BF16, name="dx_out")
    d_ya, d_yb, d_proj, got["w_up"] = _stage_bwd(
        _f_merge, n_tm, in_mg, [], [_col(d_merged, TM)], lambda i, t: t, [None, None, (BF16, None), (BF16, None)], "merge_b",
        hosted=[rows16(gwt_gu[D_FF:])], pack=([0, 1], None, D_INP, C_GA // (2 * D)))
    gw_pa = _mm_tn(oa, d_ya, name="dw_pa")
    gw_pb = _mm_tn(ob, d_yb, name="dw_pb")
    d_oa = _mm(d_ya, w["w_pa"], trans_b=True, out_dtype=BF16, name="dx_pa")
    d_ob = _mm(d_yb, w["w_pb"], trans_b=True, out_dtype=BF16, name="dx_pb")
    d_og, g["gla_norm_w"], d_proj, got["w_out"] = _stage_bwd(
        _f_gla_post, n_tm, in_gp, p_gp, [_col(d_oa, TM)], lambda i, t: t, [(BF16, None), None, None], "gla_post_b",
        hosted=[rows16(gw_out)], pack=([2], d_proj, D_INP, C_R // D))
    d_ys, d_xs_skip, g["dsk8"], g["ssm_norm_w"], d_proj, got["w_pa"], got["w_pb"] = _stage_bwd(
        _f_ssd_post, n_tm, in_sp, p_sp, [_col(d_ob, TM)], lambda i, t: t,
        [(BF16, None), None, (BF16, None), None], "ssd_post_b", hosted=[rows16(gw_pa), rows16(gw_pb)],
        pack=([3], d_proj, D_INP, C_Z // SSM_INNER))

    dq, dk, dv, d_lgf = _gla_bwd(proj, lg_f, sv_gf, d_og, None, ncc_g, False, "gla_f_b")
    d_proj, d_lgb = _gla_bwd(proj, lg_b, sv_gb, d_og, (dq, dk, dv), ncc_g, True, "gla_b_b", pack=(d_proj, D_INP))
    d_xf, d_bm, d_cm, d_axf, d_atf = _ssd_bwd(xf, xbc, axf, atf, sv_sf, d_ys, None, ncc_s, False, "ssd_f_b")
    d_xb, d_bm, d_cm, d_axb, d_atb = _ssd_bwd(xb, xbc, axb, atb, sv_sb, d_ys, (d_bm, d_cm), ncc_s, True, "ssd_b_b")

    cts_gd = [_col(a, TS) for a in (d_lgf, d_lgb, d_xf, d_xb, d_axf, d_axb)]
    cts_gd += [(a, (n_heads, TS), lambda i: (0, i)) for a in (d_atf, d_atb)]
    d_xs_dt, g["up_f"], g["up_b"], g["gla_bias_f"], g["gla_bias_b"], g["dtb"], g["alog"], d_proj = _stage_bwd(
        _f_gates_dt, n_ts, in_gd, p_gd, cts_gd, lambda i, t: t, [None, (BF16, None)], "gates_dt_b",
        pack=([0], d_proj, D_INP, C_S // 128))
    cts_conv = [_col(d_xs_skip, TM), _col(d_xs_dt, TM), _col(d_bm, TM), _col(d_cm, TM)]
    g["conv_w"], g["conv_b"], d_proj = _stage_bwd(
        f_conv, n_tm, in_conv, p_conv, cts_conv,
        lambda i, t: [jnp.concatenate([t[0] + t[1], t[2], t[3]], axis=1)], [None], "conv_b",
        pack=([0], d_proj, D_INP, C_U // 3072))
    gwt_in = _mm_tn(d_proj, h1, name="dw_in", tr_t=1152)
    d_h1, got["w_in"] = _mm(d_proj, w["wt_in"], trans_b=False, out_dtype=BF16, name="dx_in", tm_t=768, tk_t=3456,
                            hosted=[rows16(_from_padded(gwt_in))])
    grad_x, g["norm1_w"], d_ada1 = _stage_bwd(
        f_norm1, n_tm, x_in, p_norm1, [_col(d_h1, TM)], lambda i, t: t,
        [None, (F32, (d_x1, (TM, D), x_in[1][2]))], "norm1_b")
    return loss_blk, grad_x, g, got, (d_ada1, d_ada2, d_ada3)


def _exchange(x, a2a, name):
    shp = x.shape[1:] if a2a else x.shape

    def body(x_ref, o_ref, send_sems, recv_sems, local_sem):
        ops = _exchange_ops(x_ref, o_ref, send_sems, recv_sems, local_sem, a2a)
        for op in ops:
            op.start()
        for op in ops:
            op.wait()

    return pl.pallas_call(
        body, out_shape=jax.ShapeDtypeStruct((N_DEV,) + tuple(shp), x.dtype),
        in_specs=[pl.BlockSpec(memory_space=pl.ANY)], out_specs=pl.BlockSpec(memory_space=pl.ANY),
        scratch_shapes=[pltpu.SemaphoreType.DMA((N_DEV - 1,)), pltpu.SemaphoreType.DMA((N_DEV - 1,)),
                        pltpu.SemaphoreType.DMA(())],
        name=name,
    )(x)


def _all_gather(x, name):
    def body(x_ref, o_ref, send_sems, recv_sems, local_sem):
        mx, my, mc = lax.axis_index("x"), lax.axis_index("y"), lax.axis_index("c")
        me, sibling = (mx, my, mc), (mx, my, 1 - mc)
        chips = [(1 - mx, my), (mx, 1 - my), (1 - mx, 1 - my)]

        def slot(px, py, pc):
            return o_ref.at[4 * px + 2 * py + pc]

        def copy(k, block, to, src=None):
            return pltpu.make_async_remote_copy(
                src_ref=slot(*block) if src is None else src, dst_ref=slot(*block),
                send_sem=send_sems.at[k], recv_sem=recv_sems.at[k], device_id=to, device_id_type=pl.DeviceIdType.MESH)

        own = pltpu.make_async_copy(x_ref, slot(*me), local_sem)
        own.start()
        first = [copy(0, me, sibling, src=x_ref)] + [copy(1 + j, me, (*chip, mc), src=x_ref) for j, chip in enumerate(chips)]
        for cp in first:
            cp.start()
        passed = [copy(4 + j, (*chip, mc), sibling) for j, chip in enumerate(chips)]
        for j, chip in enumerate(chips):
            copy(1 + j, (*chip, mc), me).wait_recv()
            passed[j].start()
        copy(0, sibling, me).wait_recv()
        for j, chip in enumerate(chips):
            copy(4 + j, (*chip, 1 - mc), me).wait_recv()
        for cp in first + passed:
            cp.wait_send()
        own.wait()

    return pl.pallas_call(
        body, out_shape=jax.ShapeDtypeStruct((N_DEV,) + tuple(x.shape), x.dtype),
        in_specs=[pl.BlockSpec(memory_space=pl.ANY)], out_specs=pl.BlockSpec(memory_space=pl.ANY),
        scratch_shapes=[pltpu.SemaphoreType.DMA((N_DEV - 1,)), pltpu.SemaphoreType.DMA((N_DEV - 1,)),
                        pltpu.SemaphoreType.DMA(())],
        name=name,
    )(x)


def _adamw_math(w, gr, m, v):
    m = ADAM_B1 * m + (1.0 - ADAM_B1) * gr
    v = ADAM_B2 * v + (1.0 - ADAM_B2) * (gr * gr)
    m_hat = m / np.float32(1.0 - ADAM_B1 ** ADAM_STEP)
    v_hat = v / np.float32(1.0 - ADAM_B2 ** ADAM_STEP)
    delta = -ADAM_LR * (m_hat / (jnp.sqrt(v_hat) + ADAM_EPS) + ADAM_WD * w)
    return delta, m, v


def _sum_adamw(parts, w, m, v, name):
    R, C = w.shape
    n_parts = parts.shape[0]
    tr = _pick(R, max(16, (2 << 20) // (4 * C) // 16 * 16), 16)

    def body(p_ref, w_ref, m_ref, v_ref, g_ref, d_ref, mo_ref, vo_ref):
        gr = p_ref[0].astype(F32)
        for k in range(1, n_parts):
            gr = gr + p_ref[k].astype(F32)
        g_ref[...] = gr
        d_ref[...], mo_ref[...], vo_ref[...] = _adamw_math(w_ref[...], gr, m_ref[...], v_ref[...])

    if R % 16 == 0:
        grid = (R // tr,)
        tile, p_spec = pl.BlockSpec((tr, C), lambda i: (i, 0)), pl.BlockSpec((n_parts, tr, C), lambda i: (0, i, 0))
    else:
        tc = _pick(C, max(128, (2 << 20) // (4 * R) // 128 * 128), 128)
        grid = (C // tc,)
        tile, p_spec = pl.BlockSpec((R, tc), lambda i: (0, i)), pl.BlockSpec((n_parts, R, tc), lambda i: (0, 0, i))
    return pl.pallas_call(
        body, grid=grid, in_specs=[p_spec, tile, tile, tile],
        out_specs=[tile] * 4, out_shape=[jax.ShapeDtypeStruct((R, C), F32)] * 4,
        compiler_params=_cparams(("parallel",)), name=name,
    )(parts, w, m, v)


def _sum8(parts, name):
    _, R, C = parts.shape

    def body(p_ref, g_ref):
        gr = p_ref[0]
        for k in range(1, N_DEV):
            gr = gr + p_ref[k]
        g_ref[...] = gr

    return pl.pallas_call(body, out_shape=jax.ShapeDtypeStruct((R, C), F32), name=name)(parts)


def _adamw(w, gr, m, v, name):
    def body(w_ref, g_ref, m_ref, v_ref, d_ref, mo_ref, vo_ref):
        d_ref[...], mo_ref[...], vo_ref[...] = _adamw_math(w_ref[...], g_ref[...], m_ref[...], v_ref[...])

    return pl.pallas_call(body, out_shape=[jax.ShapeDtypeStruct(w.shape, F32)] * 3, name=name)(w, gr, m, v)


def _to_padded(wt_in):
    z = jnp.zeros((32,) + wt_in.shape[1:], wt_in.dtype)
    return jnp.concatenate([wt_in[0:3072], wt_in[5152:8224], wt_in[3104:5152], wt_in[8288:10336],
                            wt_in[3072:3104], wt_in[8224:8288], z], axis=0)


def _from_padded(p):
    return jnp.concatenate([p[0:3072], p[10240:10272], p[6144:8192], p[3072:6144], p[10272:10336], p[8192:10240]], axis=0)


def _unshard_cols(gathered):
    n, r, c = gathered.shape
    return jnp.transpose(gathered, (1, 0, 2)).reshape(r, n * c)


def _lanes(vec, lo):
    return jnp.concatenate([jnp.zeros((1, lo), F32), vec, jnp.zeros((1, 128 - lo - vec.shape[1]), F32)], axis=1)


_SMALL = (("b_ada", 6 * D), ("c_ctx", D), ("norm1_w", D), ("gla_bias_f", 512), ("gla_bias_b", 512), ("gla_norm_w", 256),
          ("conv_b", 3072), ("dt_bias_f", 32), ("dt_bias_b", 32), ("a_log_f", 32), ("a_log_b", 32), ("d_skip", 32),
          ("ssm_norm_w", 2048), ("norm2_w", D), ("final_norm_w", D))
_SHARDED_SMALL = (("gla_up_f", 16 * 512), ("gla_up_b", 16 * 512), ("conv_w", 4 * 3072))


def _pack(vals, names):
    flat = jnp.concatenate([vals[n].reshape(-1).astype(F32) for n, _ in names])
    pad = (-flat.shape[0]) % 1024
    return jnp.concatenate([flat, jnp.zeros((pad,), F32)]).reshape(-1, 128)


def _unpack(packed, names, shapes):
    flat, out, off = packed.reshape(-1), {}, 0
    for n, size in names:
        out[n] = flat[off:off + size].reshape(shapes[n])
        off += size
    return out


def kernel(x, c, ctx, c_ctx, w_ada, b_ada, norm1_w, w_in, gla_up_f, gla_bias_f, gla_up_b, gla_bias_b, gla_norm_w, conv_w, conv_b, dt_bias_f, dt_bias_b, a_log_f, a_log_b, d_skip, ssm_norm_w, w_pa, w_pb, w_out, norm2_w, w_gate, w_up, w_down, final_norm_w, loss_target, m_c_ctx, m_w_ada, m_b_ada, m_norm1_w, m_w_in, m_gla_up_f, m_gla_bias_f, m_gla_up_b, m_gla_bias_b, m_gla_norm_w, m_conv_w, m_conv_b, m_dt_bias_f, m_dt_bias_b, m_a_log_f, m_a_log_b, m_d_skip, m_ssm_norm_w, m_w_pa, m_w_pb, m_w_out, m_norm2_w, m_w_gate, m_w_up, m_w_down, m_final_norm_w, v_c_ctx, v_w_ada, v_b_ada, v_norm1_w, v_w_in, v_gla_up_f, v_gla_bias_f, v_gla_up_b, v_gla_bias_b, v_gla_norm_w, v_conv_w, v_conv_b, v_dt_bias_f, v_dt_bias_b, v_a_log_f, v_a_log_b, v_d_skip, v_ssm_norm_w, v_w_pa, v_w_pb, v_w_out, v_norm2_w, v_w_gate, v_w_up, v_w_down, v_final_norm_w):
    args = dict(locals())
    me = 4 * lax.axis_index("x") + 2 * lax.axis_index("y") + lax.axis_index("c")

    t_names = ("w_in", "w_gate", "w_up")
    local = {n: (jnp.transpose(args[n][0]) if n in t_names else args[n][0]) for n in
             ("w_in", "w_pa", "w_pb", "w_out", "w_gate", "w_up", "w_down")}
    full = {"wt_in": _to_padded(_all_gather(local["w_in"].astype(BF16), "ag_w_in").reshape(D_IN, D))}
    shards = {n: local[n].astype(BF16) for n in ("w_pa", "w_pb", "w_out", "w_gate", "w_up", "w_down")}
    sm = _exchange(jnp.concatenate([gla_up_f.reshape(-1), gla_up_b.reshape(-1), conv_w.reshape(-1), c.reshape(-1)]).reshape(-1, 128),
                   False, "ag_small")
    sm = sm.reshape(N_DEV, -1)
    n_ada = w_ada.shape[2]
    cc = jnp.concatenate([sm[:, 3584:3584 + D], c_ctx[None], jnp.zeros((_ADA_ROWS - N_DEV - 1, D), F32)], axis=0)
    w_ada16 = w_ada[0].astype(BF16)
    ada_cols = _exchange(_ada_fwd(cc, w_ada16, lax.dynamic_slice(b_ada, (0, me * n_ada), (1, n_ada))), False, "ag_ada")
    ada = jnp.concatenate([lax.dynamic_index_in_dim(ada_cols, me, axis=1, keepdims=False).reshape(1, -1),
                           ada_cols[:, N_DEV].reshape(1, -1), jnp.zeros((6, N_DEV * n_ada), F32)], axis=0)
    up_f = _unshard_cols(sm[:, 0:1024].reshape(N_DEV, 16, 64))
    up_b = _unshard_cols(sm[:, 1024:2048].reshape(N_DEV, 16, 64))
    full["conv_w"] = _unshard_cols(sm[:, 2048:3584].reshape(N_DEV, 4, 384))
    full["up_f"] = jnp.concatenate([up_f, jnp.zeros((112, 512), F32)], axis=0)
    full["up_b"] = jnp.concatenate([jnp.zeros((16, 512), F32), up_b, jnp.zeros((96, 512), F32)], axis=0)
    full["dtb"] = jnp.concatenate([jnp.zeros((1, 32), F32), dt_bias_f, dt_bias_b, jnp.zeros((1, 32), F32)], axis=1)
    full["alog"] = jnp.concatenate([jnp.zeros((1, 32), F32), a_log_f, a_log_b, jnp.zeros((1, 32), F32)], axis=1)
    full["dsk8"] = jnp.concatenate([_lanes(d_skip, 0), jnp.zeros((7, 128), F32)], axis=0)
    for n in ("norm1_w", "gla_bias_f", "gla_bias_b", "gla_norm_w", "conv_b", "ssm_norm_w", "norm2_w"):
        full[n] = args[n]
    full["final_norm_w"] = final_norm_w.reshape(1, D)

    loss_blk, grad_x, g, got, d_ada = _local_step(x[0], ctx[0], loss_target[0], ada, full, shards)

    d_all = _exchange(jnp.concatenate([d[0:2] for d in d_ada], axis=0), False, "ag_d_ada").reshape(N_DEV * 6, N_DEV * n_ada)
    gw_ada, gb_ada, d_cc = _ada_bwd(cc, w_ada16, d_all, lax.dynamic_slice(d_all, (0, me * n_ada), (N_DEV * 6, n_ada)))

    gs = dict(g, c_ctx=d_cc[N_DEV], b_ada=jnp.zeros_like(b_ada))
    gs["dt_bias_f"], gs["dt_bias_b"] = g["dtb"][:, 32:64], g["dtb"][:, 64:96]
    gs["a_log_f"], gs["a_log_b"] = g["alog"][:, 32:64], g["alog"][:, 64:96]
    gs["d_skip"] = g["dsk8"][0:1, 0:32]
    gs["gla_up_f"], gs["gla_up_b"] = g["up_f"][0:16], g["up_b"][16:32]
    gs["loss"] = loss_blk[0:1, 0:1]
    names = _SMALL + _SHARDED_SMALL + (("loss", 1),)
    shapes = {n: (args[n].shape if n in args else (1, 1)) for n, _ in names}
    shapes.update({"gla_up_f": (16, 512), "gla_up_b": (16, 512), "conv_w": (4, 3072)})
    red = _unpack(_sum8(_exchange(_pack(gs, names), False, "ag_small_grads"), "sum_small_grads"), names, shapes)
    loss = red["loss"].reshape(())
    grads = {n: red[n] for n, _ in _SMALL}
    grads["b_ada"] = gb_ada
    grads["gla_up_f"] = lax.dynamic_slice(red["gla_up_f"], (0, me * 64), (16, 64))[None]
    grads["gla_up_b"] = lax.dynamic_slice(red["gla_up_b"], (0, me * 64), (16, 64))[None]
    grads["conv_w"] = lax.dynamic_slice(red["conv_w"], (0, me * 384), (4, 384))[None]
    upd_names = tuple((n, s) for n, s in _SMALL) + (("gla_up_f", 1024), ("gla_up_b", 1024), ("conv_w", 1536))
    pk = lambda prefix, src: _pack({n: src[prefix + n] for n, _ in upd_names}, upd_names)
    d_s, m_s, v_s = _adamw(pk("", args), _pack(grads, upd_names), pk("m_", args), pk("v_", args), "adamw_small")
    upd_shapes = {n: args[n].shape for n, _ in upd_names}
    delta = _unpack(d_s, upd_names, upd_shapes)
    new_m = _unpack(m_s, upd_names, upd_shapes)
    new_v = _unpack(v_s, upd_names, upd_shapes)

    got["w_ada"] = gw_ada[None]
    for n, parts in got.items():
        orient = jnp.transpose if n in t_names else (lambda a: a)
        res = _sum_adamw(parts, local[n] if n in local else args[n][0], orient(args["m_" + n][0]), orient(args["v_" + n][0]),
                         "adamw_" + n)
        grads[n], delta[n], new_m[n], new_v[n] = [orient(r)[None] for r in res]

    order = ["c_ctx", "w_ada", "b_ada", "norm1_w", "w_in", "gla_up_f", "gla_bias_f", "gla_up_b", "gla_bias_b", "gla_norm_w",
             "conv_w", "conv_b", "dt_bias_f", "dt_bias_b", "a_log_f", "a_log_b", "d_skip", "ssm_norm_w", "w_pa", "w_pb",
             "w_out", "norm2_w", "w_gate", "w_up", "w_down", "final_norm_w"]
    fix = lambda d: [d[n].reshape(args[n].shape) for n in order]
    return (loss, grad_x[None], *fix(grads), *fix(delta), *fix(new_m), *fix(new_v))
```

```python
import functools

import jax
import jax.numpy as jnp
import numpy as np
from jax import lax
from jax.experimental import pallas as pl
from jax.experimental.pallas import tpu as pltpu

F32 = jnp.float32
BF16 = jnp.bfloat16
N_DEV = 8
D = 1024
EPS = 1e-6
GRID_W = 64
GLA_H, GLA_DK, GLA_DV = 4, 128, 256
GLA_C = 64
GLA_TAU = 16.0
SSM_G, SSM_HPG, SSM_P, SSM_N = 4, 8, 64, 128
SSM_C = 128
SSM_INNER = 2048
D_FF = 2816
D_IN = 10336
D_INP = 10368
C_Q, C_K, C_V, C_R, C_U, C_Z, C_GA, C_GB, C_S = 0, 512, 1024, 2048, 3072, 6144, 8192, 9216, 10240
TM = 256
TS = 128

ADAM_LR, ADAM_B1, ADAM_B2, ADAM_EPS, ADAM_WD, ADAM_STEP = 0.001, 0.9, 0.999, 1e-08, 0.01, 10
VMEM_LIMIT = 56 << 20


def _cparams(sem):
    return pltpu.CompilerParams(dimension_semantics=sem or None, vmem_limit_bytes=VMEM_LIMIT)


def _pick(n, target, mult):
    best = None
    for t in range(mult, min(n, target) + 1, mult):
        if n % t == 0:
            best = t
    return best if best is not None else n


def _dot_impl(a, b, ta, tb):
    dims = (((0 if ta else 1,), (1 if tb else 0,)), ((), ()))
    return lax.dot_general(a.astype(BF16), b.astype(BF16), dims, preferred_element_type=F32)


@functools.partial(jax.custom_vjp, nondiff_argnums=(2, 3))
def _bdot(a, b, ta=False, tb=False):
    return _dot_impl(a, b, ta, tb)


def _bdot_fwd(a, b, ta, tb):
    return _dot_impl(a, b, ta, tb), (a, b)


def _bdot_bwd(ta, tb, res, g):
    a, b = res
    if not ta and not tb:
        return _dot_impl(g, b, False, True), _dot_impl(a, g, True, False)
    if not ta and tb:
        return _dot_impl(g, b, False, False), _dot_impl(g, a, True, False)
    if ta and not tb:
        return _dot_impl(b, g, False, True), _dot_impl(a, g, False, False)
    raise NotImplementedError


_bdot.defvjp(_bdot_fwd, _bdot_bwd)


_NN = (((1,), (0,)), ((), ()))
_NT = (((1,), (1,)), ((), ()))
_TN = (((0,), (0,)), ((), ()))


def _dg2(x, e, x_is_lhs, dims):
    hi = x.astype(BF16)
    lo = (x - hi.astype(F32)).astype(BF16)
    e = e.astype(BF16)
    if x_is_lhs:
        return (lax.dot_general(hi, e, dims, preferred_element_type=F32)
                + lax.dot_general(lo, e, dims, preferred_element_type=F32))
    return (lax.dot_general(e, hi, dims, preferred_element_type=F32)
            + lax.dot_general(e, lo, dims, preferred_element_type=F32))


_EDOT_FWD = {"xe": (True, _NN), "ex": (False, _NN), "ext": (False, _NT)}
_EDOT_BWD = {"xe": (True, _NT), "ex": (False, _TN), "ext": (True, _TN)}


@functools.partial(jax.custom_vjp, nondiff_argnums=(2,))
def _edot(x, e, mode):
    return _dg2(x, e, *_EDOT_FWD[mode])


def _edot_fwd(x, e, mode):
    return _dg2(x, e, *_EDOT_FWD[mode]), e


def _edot_bwd(mode, e, g):
    return _dg2(g, e, *_EDOT_BWD[mode]), None


_edot.defvjp(_edot_fwd, _edot_bwd)


def _shift_impl(u, pos, per, s):
    n = u.shape[0]
    rolled = u if s == 0 else pltpu.roll(u, (-s) % n, 0)
    ok = (pos + s >= 0) & (pos + s < per)
    return jnp.where(ok, rolled, 0.0)


@functools.partial(jax.custom_vjp, nondiff_argnums=(3,))
def _shift(u, pos, per, s):
    return _shift_impl(u, pos, per, s)


def _shift_fwd(u, pos, per, s):
    return _shift_impl(u, pos, per, s), (pos, per)


def _shift_bwd(s, res, g):
    pos, per = res
    return _shift_impl(g, pos, per, -s), None, None


_shift.defvjp(_shift_fwd, _shift_bwd)


def _rms(x, w):
    return x * lax.rsqrt(jnp.mean(x * x, axis=-1, keepdims=True) + EPS) * w


def _silu(x):
    return x * jax.nn.sigmoid(x)


def _softplus(x):
    return jnp.maximum(x, 0.0) + jnp.log(1.0 + jnp.exp(-jnp.abs(x)))


def _logsig(x):
    return jnp.minimum(x, 0.0) - jnp.log(1.0 + jnp.exp(-jnp.abs(x)))


def _tri(n, rev):
    t = lax.broadcasted_iota(jnp.int32, (n, n), 0)
    s = lax.broadcasted_iota(jnp.int32, (n, n), 1)
    return (s >= t) if rev else (t >= s)


def _head_expand(first_lane):
    l = lax.broadcasted_iota(jnp.int32, (128, SSM_INNER), 0)
    c = lax.broadcasted_iota(jnp.int32, (128, SSM_INNER), 1)
    return (l == first_lane + lax.shift_right_logical(c, 6)).astype(F32)


def _exchange_ops(x_ref, o_ref, send_sems, recv_sems, local_sem, a2a):
    mx, my, mc = lax.axis_index("x"), lax.axis_index("y"), lax.axis_index("c")
    me = 4 * mx + 2 * my + mc
    ops = []
    for k in range(1, N_DEV):
        px = 1 - mx if k & 4 else mx
        py = 1 - my if k & 2 else my
        pc = 1 - mc if k & 1 else mc
        ops.append(pltpu.make_async_remote_copy(
            src_ref=x_ref.at[4 * px + 2 * py + pc] if a2a else x_ref, dst_ref=o_ref.at[me],
            send_sem=send_sems.at[k - 1], recv_sem=recv_sems.at[k - 1],
            device_id=(px, py, pc), device_id_type=pl.DeviceIdType.MESH))
    ops.append(pltpu.make_async_copy(x_ref.at[me] if a2a else x_ref, o_ref.at[me], local_sem))
    return ops


def _call(body, *, grid, in_specs, out_specs, out_shape, scratch=(), sem, name, args, hosted=(), aliases=None):
    n_in, n_out, n_s, n_h = len(in_specs), len(out_specs), len(scratch), len(hosted)
    aliases = aliases or {}
    if not n_h:
        return pl.pallas_call(body, grid=grid, in_specs=list(in_specs), out_specs=list(out_specs),
                              out_shape=list(out_shape), scratch_shapes=list(scratch), input_output_aliases=aliases,
                              compiler_params=_cparams(sem), name=name)(*args)
    b0, c0 = n_in + n_h, n_in + n_h + n_out
    d0 = c0 + n_h

    def wrapped(*refs):
        hx, ho = refs[n_in:b0], refs[c0:d0]
        send_sems, recv_sems, local_sems = refs[d0 + n_s:]
        ids = [pl.program_id(ax) for ax in range(len(grid))]
        first = functools.reduce(jnp.logical_and, [i == 0 for i in ids])
        last = functools.reduce(jnp.logical_and, [i == n - 1 for i, n in zip(ids, grid)])

        def ops():
            return [op for n, (_, a2a) in enumerate(hosted)
                    for op in _exchange_ops(hx[n], ho[n], send_sems.at[n], recv_sems.at[n], local_sems.at[n], a2a)]

        @pl.when(first)
        def _():
            for op in ops():
                op.start()

        body(*refs[:n_in], *refs[b0:c0], *refs[d0:d0 + n_s])

        @pl.when(last)
        def _():
            for op in ops():
                op.wait()

    hbm = pl.BlockSpec(memory_space=pl.ANY)
    h_shapes = [jax.ShapeDtypeStruct((N_DEV,) + tuple(x.shape[1:] if a2a else x.shape), x.dtype) for x, a2a in hosted]
    return pl.pallas_call(
        wrapped, grid=grid, in_specs=list(in_specs) + [hbm] * n_h, out_specs=list(out_specs) + [hbm] * n_h,
        out_shape=list(out_shape) + h_shapes, input_output_aliases=aliases,
        scratch_shapes=list(scratch) + [pltpu.SemaphoreType.DMA((n_h, N_DEV - 1)), pltpu.SemaphoreType.DMA((n_h, N_DEV - 1)),
                                        pltpu.SemaphoreType.DMA((n_h,))],
        compiler_params=_cparams(("arbitrary",) * len(grid)), name=name,
    )(*args, *[x for x, _ in hosted])


def _mm(a, b, *, trans_b, out_dtype, name, tm_t=1408, tn_t=1024, tk_t=2816, hosted=()):
    M, K = a.shape
    N = b.shape[0] if trans_b else b.shape[1]
    tm, tn, tk = _pick(M, tm_t, 8), _pick(N, tn_t, 128), _pick(K, tk_t, 128)
    nk = K // tk
    dims = (((1,), (1,)), ((), ())) if trans_b else (((1,), (0,)), ((), ()))

    def body_one(a_ref, b_ref, o_ref):
        o_ref[...] = lax.dot_general(a_ref[...], b_ref[...], dims, preferred_element_type=F32).astype(out_dtype)

    def body_acc(a_ref, b_ref, o_ref, acc_ref):
        k = pl.program_id(2)
        p = lax.dot_general(a_ref[...], b_ref[...], dims, preferred_element_type=F32)

        @pl.when(k == 0)
        def _():
            acc_ref[...] = p

        @pl.when(k > 0)
        def _():
            acc_ref[...] += p

        @pl.when(k == nk - 1)
        def _():
            o_ref[...] = acc_ref[...].astype(out_dtype)

    b_spec = pl.BlockSpec((tn, tk), lambda j, i, k: (j, k)) if trans_b else pl.BlockSpec((tk, tn), lambda j, i, k: (k, j))
    outs = _call(
        body_one if nk == 1 else body_acc, grid=(N // tn, M // tm, nk),
        in_specs=[pl.BlockSpec((tm, tk), lambda j, i, k: (i, k)), b_spec],
        out_specs=[pl.BlockSpec((tm, tn), lambda j, i, k: (i, j))],
        out_shape=[jax.ShapeDtypeStruct((M, N), out_dtype)],
        scratch=[] if nk == 1 else [pltpu.VMEM((tm, tn), F32)],
        sem=("parallel", "parallel", "arbitrary"), name=name, args=(a, b), hosted=hosted)
    return outs if hosted else outs[0]


def _mm_tn(a, b, *, name, tm_t=2816, tr_t=1024, tn_t=1408):
    M, R = a.shape
    N = b.shape[1]
    tm, tr, tn = _pick(M, tm_t, 8), _pick(R, tr_t, 128), _pick(N, tn_t, 128)

    def body(a_ref, b_ref, o_ref):
        m = pl.program_id(2)
        p = lax.dot_general(a_ref[...], b_ref[...], (((0,), (0,)), ((), ())), preferred_element_type=F32)

        @pl.when(m == 0)
        def _():
            o_ref[...] = p

        @pl.when(m > 0)
        def _():
            o_ref[...] += p

    return pl.pallas_call(
        body, grid=(R // tr, N // tn, M // tm),
        in_specs=[pl.BlockSpec((tm, tr), lambda r, j, m: (m, r)), pl.BlockSpec((tm, tn), lambda r, j, m: (m, j))],
        out_specs=pl.BlockSpec((tr, tn), lambda r, j, m: (r, j)),
        out_shape=jax.ShapeDtypeStruct((R, N), F32),
        compiler_params=_cparams(("parallel", "parallel", "arbitrary")), name=name,
    )(a, b)


def _col(arr, tm, width=None, col=0):
    width = arr.shape[1] if width is None else width
    return (arr, (tm, width), lambda i: (i, col))


def _lat(arr, tm, nct):
    return (arr, (tm, arr.shape[1]), lambda i: (jnp.maximum(i - nct, 0), 0))


def _ctx(arr, tm, nct):
    return (arr, (tm, arr.shape[1]), lambda i: (jnp.minimum(i, nct - 1), 0))


def _whole(p):
    return pl.BlockSpec(p.shape, lambda i, nd=p.ndim: (0,) * nd)


def _stage_fwd(f, n_tiles, ins, params, outs, name):
    ni, npar = len(ins), len(params)
    o_specs = [pl.BlockSpec((o[1], o[3]), lambda i: (0, i)) if len(o) > 4 else pl.BlockSpec((o[3], o[1]), lambda i: (i, 0))
               for o in outs]
    o_shapes = [jax.ShapeDtypeStruct((o[1], o[0]) if len(o) > 4 else (o[0], o[1]), o[2]) for o in outs]

    def body(*refs):
        i = pl.program_id(0)
        xs = [r[...].astype(F32) for r in refs[:ni]]
        ps = [r[...] for r in refs[ni:ni + npar]]
        for r, v in zip(refs[ni + npar:], f(i, xs, ps, False)):
            r[...] = v.astype(r.dtype)

    return pl.pallas_call(
        body, grid=(n_tiles,),
        in_specs=[pl.BlockSpec(bs, fn) for _, bs, fn in ins] + [_whole(p) for p in params],
        out_specs=o_specs, out_shape=o_shapes,
        compiler_params=_cparams(("parallel",)), name=name,
    )(*[a for a, _, _ in ins], *params)


def _stage_bwd(f, n_tiles, ins, params, cts, ct_fn, dins, name, hosted=(), pack=None):
    ni, npar, nc = len(ins), len(params), len(cts)
    want = [k for k, d in enumerate(dins) if d is not None]
    extras = [dins[k][1] for k in want if dins[k][1] is not None]
    n_buf = 0 if pack is None or pack[1] is None else 1
    n_main_in = ni + npar + nc + len(extras)

    def body(*refs):
        i = pl.program_id(0)
        xs = [r[...].astype(F32) for r in refs[:ni]]
        ps = [r[...] for r in refs[ni:ni + npar]]
        ct_tiles = [r[...].astype(F32) for r in refs[ni + npar:ni + npar + nc]]
        ex_refs = list(refs[ni + npar + nc:n_main_in])
        out_refs = refs[n_main_in + n_buf:]
        _, vjp = jax.vjp(lambda xs_, ps_: tuple(f(i, xs_, ps_, True)), xs, ps)
        dxs, dps = vjp(tuple(ct_fn(i, ct_tiles)))
        for n, k in enumerate(want):
            v = dxs[k]
            if dins[k][1] is not None:
                v = v + ex_refs.pop(0)[...].astype(F32)
            out_refs[n][...] = v.astype(out_refs[n].dtype)
        if pack is not None:
            packed = [dxs[k] for k in pack[0]]
            r = out_refs[len(want) + npar]
            r[...] = (packed[0] if len(packed) == 1 else jnp.concatenate(packed, axis=1)).astype(r.dtype)
        for r, v in zip(out_refs[len(want):len(want) + npar], dps):
            @pl.when(i == 0)
            def _(r=r, v=v):
                r[...] = v

            @pl.when(i > 0)
            def _(r=r, v=v):
                r[...] += v

    in_specs = ([pl.BlockSpec(bs, fn) for _, bs, fn in ins] + [_whole(p) for p in params]
                + [pl.BlockSpec(bs, fn) for _, bs, fn in cts] + [pl.BlockSpec(bs, fn) for _, bs, fn in extras])
    out_specs = ([pl.BlockSpec(ins[k][1], lambda i, fn=ins[k][2]: (fn(i)[0], 0)) for k in want] + [_whole(p) for p in params])
    out_shape = ([jax.ShapeDtypeStruct((ins[k][0].shape[0], ins[k][1][1]), dins[k][0]) for k in want]
                 + [jax.ShapeDtypeStruct(p.shape, F32) for p in params])
    args = [*[a for a, _, _ in ins], *params, *[a for a, _, _ in cts], *[a for a, _, _ in extras]]
    aliases = None
    if pack is not None:
        ks, buf, total_cols, block = pack
        width = sum(ins[k][1][1] for k in ks)
        out_specs.append(pl.BlockSpec((ins[ks[0]][1][0], width), lambda i, fn=ins[ks[0]][2]: (fn(i)[0], block)))
        out_shape.append(jax.ShapeDtypeStruct((ins[ks[0]][0].shape[0], total_cols), BF16))
        if buf is not None:
            in_specs.append(pl.BlockSpec(memory_space=pl.ANY))
            args.append(buf)
            aliases = {len(args) - 1: len(out_specs) - 1}
    return _call(body, grid=(n_tiles,), in_specs=in_specs, out_specs=out_specs, out_shape=out_shape,
                 sem=("arbitrary",), name=name, args=args, hosted=hosted, aliases=aliases)


def _mod_rows(i, nct, ada, lo):
    return jnp.where(i < nct, ada[1:2, lo:lo + D], ada[0:1, lo:lo + D])


def _f_norm1(nct):
    def f(i, xs, ps, diff):
        ctx, x = xs
        nw, ada = ps
        xt = jnp.where(i < nct, ctx, x)
        return (_rms(xt, nw) * (1.0 + _mod_rows(i, nct, ada, D)) + _mod_rows(i, nct, ada, 0),)
    return f


def _f_gates_dt(i, xs, ps, diff):
    small, xc = xs
    up_f, up_b, gb_f, gb_b, dtb, alog = ps
    dot = _bdot if diff else _dot_impl_nn
    lg_f = _logsig(dot(small, up_f) + gb_f) * (1.0 / GLA_TAU)
    lg_b = _logsig(dot(small, up_b) + gb_b) * (1.0 / GLA_TAU)
    dtp = _softplus(small + dtb)
    aa = -jnp.exp(alog) * dtp
    e_f, e_b = _head_expand(32), _head_expand(64)
    n = small.shape[0]
    t = lax.broadcasted_iota(jnp.int32, (n, n), 0)
    s = lax.broadcasted_iota(jnp.int32, (n, n), 1)
    same = lax.shift_right_logical(t, 6) == lax.shift_right_logical(s, 6)
    pre_g, suf_g = ((t >= s) & same).astype(F32), ((s >= t) & same).astype(F32)
    pre_s, suf_s = (t >= s).astype(F32), (s >= t).astype(F32)
    cum_f, cum_b = _edot(aa, pre_s, "ex"), _edot(aa, suf_s, "ex")
    h = lax.broadcasted_iota(jnp.int32, (SSM_G * SSM_HPG, 128), 0)
    l = lax.broadcasted_iota(jnp.int32, (SSM_G * SSM_HPG, 128), 1)
    cum_ft = _edot(cum_f, (l == h + 32).astype(F32), "ext")
    cum_bt = _edot(cum_b, (l == h + 64).astype(F32), "ext")
    return (_edot(lg_f, pre_g, "ex"), _edot(lg_b, suf_g, "ex"), xc * _bdot(dtp, e_f), xc * _bdot(dtp, e_b),
            _edot(cum_f, e_f, "xe"), _edot(cum_b, e_b, "xe"), cum_ft, cum_bt)


def _dot_impl_nn(a, b):
    return _dot_impl(a, b, False, False)


def _f_conv(nct, tc):
    def f(i, xs, ps, diff):
        (u,) = xs
        cw, cb = ps
        n = u.shape[0]
        t = lax.broadcasted_iota(jnp.int32, (n, 1), 0)
        per = jnp.where(i < nct, tc, GRID_W)
        pos = jnp.bitwise_and(t, per - 1)
        sh = _shift if diff else _shift_impl
        acc = cb + sh(u, pos, per, -2) * cw[0:1]
        for j in range(1, 4):
            acc = acc + sh(u, pos, per, j - 2) * cw[j:j + 1]
        return (_silu(acc),)
    return f


def _f_gla_post(i, xs, ps, diff):
    ogf, ogb, r = xs
    (gw,) = ps
    o = ogf + ogb
    parts = [_rms(o[:, h * GLA_DV:(h + 1) * GLA_DV], gw) for h in range(GLA_H)]
    return (jnp.concatenate(parts, axis=1) * _silu(r),)


def _f_ssd_post(i, xs, ps, diff):
    yf, yb, xc, z = xs
    dsk8, nw = ps
    dsk = _edot(dsk8, _head_expand(0), "xe")[0:1]
    y = (yf + yb + dsk * xc) * _silu(z)
    w = SSM_INNER // SSM_G
    parts = [_rms(y[:, g * w:(g + 1) * w], nw[:, g * w:(g + 1) * w]) for g in range(SSM_G)]
    return (jnp.concatenate(parts, axis=1),)


def _f_merge(i, xs, ps, diff):
    ga, gb, ya, yb = xs
    return (jax.nn.sigmoid(ga) * ya + jax.nn.sigmoid(gb) * yb,)


def _f_res1(nct):
    def f(i, xs, ps, diff):
        ctx, x, mix = xs
        ada, nw = ps
        h2 = jnp.where(i < nct, ctx, x) + _mod_rows(i, nct, ada, 2 * D) * mix
        return (h2, _rms(h2, nw) * (1.0 + _mod_rows(i, nct, ada, 4 * D)) + _mod_rows(i, nct, ada, 3 * D))
    return f


def _f_swiglu(i, xs, ps, diff):
    (gu,) = xs
    return (_silu(gu[:, :D_FF]) * gu[:, D_FF:],)


def _loss_head(h2, dn, tgt, ada, fw, nct):
    A = h2.shape[0]
    n_tiles = A // TM

    def tile_loss(i, h2t, dnt, tg, ada_, fw_):
        h3 = h2t + ada_[0:1, 5 * D:6 * D] * dnt
        err = _rms(h3, fw_) - tg
        row = 0.5 * jnp.mean(err * err, axis=-1, keepdims=True)
        return jnp.sum(row, axis=0, keepdims=True) * jnp.where(i < nct, 0.0, 1.0)

    def body(h2_ref, dn_ref, tg_ref, ada_ref, fw_ref, loss_ref, dh_ref, ddn_ref, dada_ref, dfw_ref):
        i = pl.program_id(0)
        val, vjp = jax.vjp(functools.partial(tile_loss, i), h2_ref[...], dn_ref[...].astype(F32), tg_ref[...], ada_ref[...],
                           fw_ref[...])
        dh, ddn, _, dada, dfw = vjp(jnp.ones((1, 1), F32))
        dh_ref[...] = dh
        ddn_ref[...] = ddn.astype(BF16)
        lv = jnp.broadcast_to(val, loss_ref.shape)
        for r, v in ((loss_ref, lv), (dada_ref, dada), (dfw_ref, dfw)):
            @pl.when(i == 0)
            def _(r=r, v=v):
                r[...] = v

            @pl.when(i > 0)
            def _(r=r, v=v):
                r[...] += v

    row = lambda i: (i, 0)
    return pl.pallas_call(
        body, grid=(n_tiles,),
        in_specs=[pl.BlockSpec((TM, D), row), pl.BlockSpec((TM, D), row),
                  pl.BlockSpec((TM, D), lambda i: (jnp.maximum(i - nct, 0), 0)), _whole(ada), _whole(fw)],
        out_specs=[pl.BlockSpec((8, 128), lambda i: (0, 0)), pl.BlockSpec((TM, D), row), pl.BlockSpec((TM, D), row),
                   _whole(ada), _whole(fw)],
        out_shape=[jax.ShapeDtypeStruct((8, 128), F32), jax.ShapeDtypeStruct((A, D), F32),
                   jax.ShapeDtypeStruct((A, D), BF16), jax.ShapeDtypeStruct(ada.shape, F32),
                   jax.ShapeDtypeStruct(fw.shape, F32)],
        compiler_params=_cparams(("arbitrary",)), name="loss_head",
    )(h2, dn, tgt, ada, fw)


def _chunk_of(step, n_chunks, n_ctx_chunks, rev):
    if not rev:
        return step
    return jnp.where(step < n_ctx_chunks, n_ctx_chunks - 1 - step, n_chunks - 1 - (step - n_ctx_chunks))


def _gla_step(st, q, k, v, b, rev):
    tri = _tri(GLA_C, rev).astype(F32)
    last = 0 if rev else GLA_C - 1
    outs, sts = [], []
    for h in range(GLA_H):
        kk = slice(h * GLA_DK, (h + 1) * GLA_DK)
        vv = slice(h * GLA_DV, (h + 1) * GLA_DV)
        qh, kh, vh, bh, sh = q[:, kk] * (GLA_DK ** -0.5), k[:, kk], v[:, vv], b[:, kk], st[vv]
        tot = bh[last:last + 1]
        mid = bh[GLA_C // 2:GLA_C // 2 + 1]
        att = _bdot(qh * jnp.exp(bh - mid), kh * jnp.exp(mid - bh), False, True) * tri
        outs.append(_bdot(att, vh) + _bdot(qh * jnp.exp(bh), sh, False, True))
        sts.append(sh * jnp.exp(tot) + _bdot(vh, kh * jnp.exp(tot - bh), True, False))
    return jnp.concatenate(sts, axis=0), jnp.concatenate(outs, axis=1)


_GQK, _GV = GLA_H * GLA_DK, GLA_H * GLA_DV


GLA_CPS = 2
_GB = GLA_C * GLA_CPS


def _gla_order(rev):
    return list(reversed(range(GLA_CPS))) if rev else list(range(GLA_CPS))


def _gla_fwd(proj, bg_f, bg_b, ncc, name):
    A = proj.shape[0]
    nb = A // _GB
    chs = [lambda s, rev=rev: _chunk_of(s, nb, ncc // GLA_CPS, rev) for rev in (False, True)]

    def body(*refs):
        ins, outs, sts = refs[:8], refs[8:12], refs[12:]

        for d, rev in enumerate((False, True)):
            q_ref, k_ref, v_ref, b_ref = ins[4 * d:4 * d + 4]
            o_ref, ss_ref = outs[2 * d:2 * d + 2]
            st_ref = sts[d]

            @pl.when(pl.program_id(0) == 0)
            def _(st_ref=st_ref):
                st_ref[...] = jnp.zeros_like(st_ref)

            st = st_ref[...]
            for pos, sub in enumerate(_gla_order(rev)):
                rows = slice(sub * GLA_C, (sub + 1) * GLA_C)
                ss_ref[0, pos] = st
                st, o = _gla_step(st, q_ref[rows].astype(F32), k_ref[rows].astype(F32), v_ref[rows].astype(F32),
                                  b_ref[rows], rev)
                o_ref[rows] = o.astype(o_ref.dtype)
            st_ref[...] = st

    in_specs, out_specs = [], []
    for ch in chs:
        in_specs += [pl.BlockSpec((_GB, _GQK), lambda s, ch=ch: (ch(s), C_Q // _GQK)),
                     pl.BlockSpec((_GB, _GQK), lambda s, ch=ch: (ch(s), C_K // _GQK)),
                     pl.BlockSpec((_GB, _GV), lambda s, ch=ch: (ch(s), C_V // _GV)),
                     pl.BlockSpec((_GB, _GQK), lambda s, ch=ch: (ch(s), 0))]
        out_specs += [pl.BlockSpec((_GB, _GV), lambda s, ch=ch: (ch(s), 0)),
                      pl.BlockSpec((1, GLA_CPS, _GV, GLA_DK), lambda s: (s, 0, 0, 0))]
    return pl.pallas_call(
        body, grid=(nb,), in_specs=in_specs, out_specs=out_specs,
        out_shape=[jax.ShapeDtypeStruct((A, _GV), BF16), jax.ShapeDtypeStruct((nb, GLA_CPS, _GV, GLA_DK), F32)] * 2,
        scratch_shapes=[pltpu.VMEM((_GV, GLA_DK), F32)] * 2,
        compiler_params=_cparams(("arbitrary",)), name=name,
    )(proj, proj, proj, bg_f, proj, proj, proj, bg_b)


def _gla_bwd(proj, lg, saved, d_o, prev, ncc, rev, name, pack=None):
    A = proj.shape[0]
    nb = A // _GB
    st_of = lambda r: nb - 1 - r
    ch = lambda r: _chunk_of(st_of(r), nb, ncc // GLA_CPS, rev)
    qs = pl.BlockSpec((_GB, _GQK), lambda r: (ch(r), 0))
    vs = pl.BlockSpec((_GB, _GV), lambda r: (ch(r), 0))
    n_prev = 0 if prev is None else 3

    n_buf = 0 if pack is None else 1

    def body(*refs):
        q_ref, k_ref, v_ref, b_ref, ss_ref, do_ref = refs[:6]
        p_refs = refs[6:6 + n_prev]
        out_refs = refs[6 + n_prev + n_buf:]
        ds_ref = out_refs[-1]

        @pl.when(pl.program_id(0) == 0)
        def _():
            ds_ref[...] = jnp.zeros_like(ds_ref)

        ds = ds_ref[...]
        for pos, sub in reversed(list(enumerate(_gla_order(rev)))):
            rows = slice(sub * GLA_C, (sub + 1) * GLA_C)
            _, vjp = jax.vjp(functools.partial(_gla_step, rev=rev), ss_ref[0, pos], q_ref[rows].astype(F32),
                             k_ref[rows].astype(F32), v_ref[rows].astype(F32), b_ref[rows])
            ds, dq, dk, dv, db = vjp((ds, do_ref[rows].astype(F32)))
            if n_prev:
                dq, dk, dv = [d + p[rows].astype(F32) for d, p in zip((dq, dk, dv), p_refs)]
            if pack is None:
                for r, d in zip(out_refs[:4], (dq, dk, dv, db)):
                    r[rows] = d.astype(r.dtype)
            else:
                out_refs[0][rows] = jnp.concatenate([dq, dk, dv], axis=1).astype(out_refs[0].dtype)
                out_refs[1][rows] = db
        ds_ref[...] = ds

    in_specs = [
        pl.BlockSpec((_GB, _GQK), lambda r: (ch(r), C_Q // _GQK)),
        pl.BlockSpec((_GB, _GQK), lambda r: (ch(r), C_K // _GQK)),
        pl.BlockSpec((_GB, _GV), lambda r: (ch(r), C_V // _GV)),
        qs,
        pl.BlockSpec((1, GLA_CPS, _GV, GLA_DK), lambda r: (st_of(r), 0, 0, 0)),
        vs,
    ] + ([qs, qs, vs] if n_prev else [])
    args = [proj, proj, proj, lg, saved, d_o, *(prev or ())]
    if pack is None:
        out_specs = [qs, qs, vs, qs]
        out_shape = [jax.ShapeDtypeStruct((A, _GQK), BF16), jax.ShapeDtypeStruct((A, _GQK), BF16),
                     jax.ShapeDtypeStruct((A, _GV), BF16), jax.ShapeDtypeStruct((A, _GQK), F32)]
        aliases = None
    else:
        in_specs.append(pl.BlockSpec(memory_space=pl.ANY))
        args.append(pack[0])
        out_specs = [pl.BlockSpec((_GB, 2 * _GQK + _GV), lambda r: (ch(r), 0)), qs]
        out_shape = [jax.ShapeDtypeStruct((A, pack[1]), BF16), jax.ShapeDtypeStruct((A, _GQK), F32)]
        aliases = {len(args) - 1: 0}
    return _call(body, grid=(nb,), in_specs=in_specs, out_specs=out_specs, out_shape=out_shape,
                 scratch=[pltpu.VMEM((_GV, GLA_DK), F32)], sem=("arbitrary",), name=name, args=args, aliases=aliases)


def _ssd_step(st, x, bm, cm, cum, cum_t, rev):
    mask = _tri(SSM_C, rev)
    last = 0 if rev else SSM_C - 1
    tot = cum[last:last + 1]
    cb = _bdot(cm, bm, False, True)
    ys = []
    for e in range(SSM_HPG):
        lm = jnp.exp(jnp.where(mask, cum[:, e * SSM_P:e * SSM_P + 1] - cum_t[e:e + 1], -jnp.inf))
        ys.append(_bdot(cb * lm, x[:, e * SSM_P:(e + 1) * SSM_P]))
    y = jnp.concatenate(ys, axis=1) + _bdot(cm, st) * jnp.exp(cum)
    st_new = st * jnp.exp(tot) + _bdot(bm, x * jnp.exp(tot - cum), True, False)
    return st_new, y


SSM_GPS = 4
_W = SSM_HPG * SSM_P
_XBC_B = SSM_INNER // (SSM_N * SSM_GPS)
_XBC_C = _XBC_B + SSM_G // SSM_GPS


def _ssd_steps(st, x, bm, cm, cum, cum_t, rev):
    outs = [_ssd_step(st[j * SSM_N:(j + 1) * SSM_N], x[:, j * _W:(j + 1) * _W], bm[:, j * SSM_N:(j + 1) * SSM_N],
                      cm[:, j * SSM_N:(j + 1) * SSM_N], cum[:, j * _W:(j + 1) * _W],
                      cum_t[j * SSM_HPG:(j + 1) * SSM_HPG], rev) for j in range(SSM_GPS)]
    return jnp.concatenate([o[0] for o in outs], axis=0), jnp.concatenate([o[1] for o in outs], axis=1)


SSM_CPS = 2
_SB = SSM_C * SSM_CPS


def _ssd_order(rev):
    return list(reversed(range(SSM_CPS))) if rev else list(range(SSM_CPS))


def _ssd_fwd(x_f, x_b, xbc, cum_f, cum_b, cum_tf, cum_tb, ncc, name, hosted=()):
    A = x_f.shape[0]
    nb = A // _SB
    ng = SSM_G // SSM_GPS
    chs = [lambda s, rev=rev: _chunk_of(s, nb, ncc // SSM_CPS, rev) for rev in (False, True)]

    def body(*refs):
        ins, outs, sts = refs[:10], refs[10:14], refs[14:]
        for d, rev in enumerate((False, True)):
            x_ref, b_ref, c_ref, a_ref, at_ref = ins[5 * d:5 * d + 5]
            y_ref, ss_ref = outs[2 * d:2 * d + 2]
            st_ref = sts[d]

            @pl.when(pl.program_id(1) == 0)
            def _(st_ref=st_ref):
                st_ref[...] = jnp.zeros_like(st_ref)

            st = st_ref[...]
            for pos, sub in enumerate(_ssd_order(rev)):
                rows = slice(sub * SSM_C, (sub + 1) * SSM_C)
                ss_ref[0, 0, pos] = st
                st, y = _ssd_steps(st, x_ref[rows].astype(F32), b_ref[rows].astype(F32), c_ref[rows].astype(F32),
                                   a_ref[rows], at_ref[:, rows], rev)
                y_ref[rows] = y.astype(y_ref.dtype)
            st_ref[...] = st

    in_specs, out_specs = [], []
    for ch in chs:
        gs = pl.BlockSpec((_SB, _W * SSM_GPS), lambda g, s, ch=ch: (ch(s), g))
        in_specs += [gs, pl.BlockSpec((_SB, SSM_N * SSM_GPS), lambda g, s, ch=ch: (ch(s), _XBC_B + g)),
                     pl.BlockSpec((_SB, SSM_N * SSM_GPS), lambda g, s, ch=ch: (ch(s), _XBC_C + g)), gs,
                     pl.BlockSpec((SSM_HPG * SSM_GPS, _SB), lambda g, s, ch=ch: (g, ch(s)))]
        out_specs += [gs, pl.BlockSpec((1, 1, SSM_CPS, SSM_N * SSM_GPS, _W), lambda g, s: (g, s, 0, 0, 0))]
    return _call(
        body, grid=(ng, nb), in_specs=in_specs, out_specs=out_specs,
        out_shape=[jax.ShapeDtypeStruct((A, SSM_INNER), BF16),
                   jax.ShapeDtypeStruct((ng, nb, SSM_CPS, SSM_N * SSM_GPS, _W), F32)] * 2,
        scratch=[pltpu.VMEM((SSM_N * SSM_GPS, _W), F32)] * 2,
        sem=("parallel", "arbitrary"), name=name,
        args=(x_f, xbc, xbc, cum_f, cum_tf, x_b, xbc, xbc, cum_b, cum_tb), hosted=hosted)


def _ssd_bwd(xd, xbc, cum, cum_t, saved, d_y, prev, ncc, rev, name):
    A = xd.shape[0]
    nc = A // SSM_C
    ng = SSM_G // SSM_GPS
    st_of = lambda r: nc - 1 - r
    ch = lambda r: _chunk_of(st_of(r), nc, ncc, rev)
    gs = pl.BlockSpec((SSM_C, _W * SSM_GPS), lambda g, r: (ch(r), g))
    ns = pl.BlockSpec((SSM_C, SSM_N * SSM_GPS), lambda g, r: (ch(r), g))
    ts = pl.BlockSpec((SSM_HPG * SSM_GPS, SSM_C), lambda g, r: (g, ch(r)))
    n_prev = 0 if prev is None else 2

    def body(*refs):
        x_ref, b_ref, c_ref, a_ref, at_ref, ss_ref, dy_ref = refs[:7]
        p_refs = refs[7:7 + n_prev]
        dx_ref, db_ref, dc_ref, da_ref, dat_ref, ds_ref = refs[7 + n_prev:]

        @pl.when(pl.program_id(1) == 0)
        def _():
            ds_ref[...] = jnp.zeros_like(ds_ref)

        _, vjp = jax.vjp(functools.partial(_ssd_steps, rev=rev), ss_ref[0, 0, 0], x_ref[...].astype(F32),
                         b_ref[...].astype(F32), c_ref[...].astype(F32), a_ref[...], at_ref[...])
        ds, dx, db, dc, da, dat = vjp((ds_ref[...], dy_ref[...].astype(F32)))
        if n_prev:
            db, dc = db + p_refs[0][...].astype(F32), dc + p_refs[1][...].astype(F32)
        for r, d in zip((dx_ref, db_ref, dc_ref, da_ref, dat_ref), (dx, db, dc, da, dat)):
            r[...] = d.astype(r.dtype)
        ds_ref[...] = ds

    in_specs = [gs, pl.BlockSpec((SSM_C, SSM_N * SSM_GPS), lambda g, r: (ch(r), _XBC_B + g)),
                pl.BlockSpec((SSM_C, SSM_N * SSM_GPS), lambda g, r: (ch(r), _XBC_C + g)), gs, ts,
                pl.BlockSpec((1, 1, 1, SSM_N * SSM_GPS, _W),
                             lambda g, r: (g, st_of(r) // SSM_CPS, st_of(r) % SSM_CPS, 0, 0)), gs]
    in_specs += [ns, ns] if n_prev else []
    return pl.pallas_call(
        body, grid=(ng, nc), in_specs=in_specs, out_specs=[gs, ns, ns, gs, ts],
        out_shape=[jax.ShapeDtypeStruct((A, SSM_INNER), BF16), jax.ShapeDtypeStruct((A, SSM_G * SSM_N), BF16),
                   jax.ShapeDtypeStruct((A, SSM_G * SSM_N), BF16), jax.ShapeDtypeStruct((A, SSM_INNER), F32),
                   jax.ShapeDtypeStruct((SSM_G * SSM_HPG, A), F32)],
        scratch_shapes=[pltpu.VMEM((SSM_N * SSM_GPS, _W), F32)],
        compiler_params=_cparams(("parallel", "arbitrary")), name=name,
    )(xd, xbc, xbc, cum, cum_t, saved, d_y, *(prev or ()))


_ADA_ROWS = 16


def _ada_fwd(cc, w_shard, b_shard):
    n = w_shard.shape[1]

    def body(cc_ref, w_ref, b_ref, o_ref):
        o_ref[...] = _dot_impl(_silu(cc_ref[...]), w_ref[...], False, False) + b_ref[...]

    return pl.pallas_call(body, out_shape=jax.ShapeDtypeStruct((_ADA_ROWS, n), F32),
                          compiler_params=_cparams(()), name="ada_fwd")(cc, w_shard, b_shard)


def _ada_bwd(cc, w_shard, d_all, d_mine):
    n = w_shard.shape[1]

    def rows(ref, r):
        parts = []
        for s in range(N_DEV):
            parts.append(ref[6 * s + r:6 * s + r + 1] + ref[6 * s + 2 + r:6 * s + 3 + r] + ref[6 * s + 4 + r:6 * s + 5 + r])
        return parts

    def total(parts):
        t = parts[0]
        for p in parts[1:]:
            t = t + p
        return t

    def body(cc_ref, w_ref, da_ref, dm_ref, dw_ref, db_ref, dcc_ref):
        dd = jnp.concatenate(rows(dm_ref, 0) + [total(rows(dm_ref, 1)), jnp.zeros((_ADA_ROWS - N_DEV - 1, n), F32)], axis=0)
        cc = cc_ref[...]
        dw_ref[...] = _dot_impl(_silu(cc), dd, True, False)
        db_ref[...] = total(rows(da_ref, 0)) + total(rows(da_ref, 1))
        _, vjp = jax.vjp(_silu, cc)
        dcc_ref[...] = vjp(_dot_impl(dd, w_ref[...], False, True))[0]

    return pl.pallas_call(
        body, out_shape=[jax.ShapeDtypeStruct((D, n), F32), jax.ShapeDtypeStruct((1, d_all.shape[1]), F32),
                         jax.ShapeDtypeStruct((_ADA_ROWS, D), F32)],
        compiler_params=_cparams(()), name="ada_bwd")(cc, w_shard, d_all, d_mine)


def _local_step(x, ctx, target, ada, w, shards):
    T, Tc = x.shape[0], ctx.shape[0]
    assert Tc == TM and T % TM == 0 and GRID_W == GLA_C and TS == SSM_C == 2 * GLA_C
    A = T + Tc
    nct = Tc // TM
    n_tm, n_ts = A // TM, A // TS
    tl = _pick(A, 3 * TM, TM)
    n_tl = A // tl
    g = {}

    x_in = [_ctx(ctx, TM, nct), _lat(x, TM, nct)]
    f_norm1 = _f_norm1(nct)
    p_norm1 = [w["norm1_w"], ada]
    (h1,) = _stage_fwd(f_norm1, n_tm, x_in, p_norm1, [(A, D, BF16, TM)], "norm1")
    proj, ag_pa, ag_pb, ag_out = _mm(h1, w["wt_in"], trans_b=True, out_dtype=BF16, name="mm_in", tn_t=1152,
                                     hosted=[(shards[n], False) for n in ("w_pa", "w_pb", "w_out")])
    small = _mm(h1, w["wt_in"][C_S:], trans_b=True, out_dtype=F32, name="mm_in_small")

    p_gd = [w["up_f"], w["up_b"], w["gla_bias_f"], w["gla_bias_b"], w["dtb"], w["alog"]]
    f_conv = _f_conv(nct, Tc)
    p_conv = [w["conv_w"], w["conv_b"]]
    in_conv = [_col(proj, TM, 3072, C_U // 3072)]
    (xbc,) = _stage_fwd(f_conv, n_tm, in_conv, p_conv, [(A, 3072, BF16, TM)], "conv")
    in_gd = [_col(small, TS), _col(xbc, TS, SSM_INNER, 0)]
    n_heads = SSM_G * SSM_HPG
    lg_f, lg_b, xf, xb, axf, axb, atf, atb = _stage_fwd(
        _f_gates_dt, n_ts, in_gd, p_gd,
        [(A, 512, F32, TS)] * 2 + [(A, SSM_INNER, BF16, TS)] * 2 + [(A, SSM_INNER, F32, TS)] * 2
        + [(A, n_heads, F32, TS, True)] * 2,
        "gates_dt")

    ncc_g, ncc_s = Tc // GLA_C, Tc // SSM_C
    ogf, sv_gf, ogb, sv_gb = _gla_fwd(proj, lg_f, lg_b, ncc_g, "gla_fb")
    ysf, sv_sf, ysb, sv_sb, ag_gate, ag_up, ag_down = _ssd_fwd(
        xf, xb, xbc, axf, axb, atf, atb, ncc_s, "ssd_fb", hosted=[(shards[n], False) for n in ("w_gate", "w_up", "w_down")])
    w = dict(w, w_pa=ag_pa.reshape(D, D), w_pb=ag_pb.reshape(SSM_INNER, D), w_out=ag_out.reshape(D, D),
             wt_gu=jnp.concatenate([ag_gate.reshape(D_FF, D), ag_up.reshape(D_FF, D)], axis=0), w_down=ag_down.reshape(D_FF, D))
    got = {}

    in_gp = [_col(ogf, tl), _col(ogb, tl), _col(proj, tl, 1024, C_R // 1024)]
    p_gp = [w["gla_norm_w"]]
    (oa,) = _stage_fwd(_f_gla_post, n_tl, in_gp, p_gp, [(A, D, BF16, tl)], "gla_post")
    in_sp = [_col(ysf, TM), _col(ysb, TM), _col(xbc, TM, SSM_INNER, 0), _col(proj, TM, SSM_INNER, C_Z // SSM_INNER)]
    p_sp = [w["dsk8"], w["ssm_norm_w"]]
    (ob,) = _stage_fwd(_f_ssd_post, n_tm, in_sp, p_sp, [(A, SSM_INNER, BF16, TM)], "ssd_post")
    ya = _mm(oa, w["w_pa"], trans_b=False, out_dtype=BF16, name="mm_pa")
    yb = _mm(ob, w["w_pb"], trans_b=False, out_dtype=BF16, name="mm_pb")
    in_mg = [_col(proj, tl, 1024, C_GA // 1024), _col(proj, tl, 1024, C_GB // 1024), _col(ya, tl), _col(yb, tl)]
    (merged,) = _stage_fwd(_f_merge, n_tl, in_mg, [], [(A, D, BF16, tl)], "merge")
    mix = _mm(merged, w["w_out"], trans_b=False, out_dtype=BF16, name="mm_out")

    f_res1 = _f_res1(nct)
    in_r1 = x_in + [_col(mix, TM)]
    p_r1 = [ada, w["norm2_w"]]
    h2, hm2 = _stage_fwd(f_res1, n_tm, in_r1, p_r1, [(A, D, F32, TM), (A, D, BF16, TM)], "res1")
    gu = _mm(hm2, w["wt_gu"], trans_b=True, out_dtype=BF16, name="mm_gu", tn_t=1408)
    in_sw = [_col(gu, TM)]
    (act,) = _stage_fwd(_f_swiglu, n_tm, in_sw, [], [(A, D_FF, BF16, TM)], "swiglu")
    dn = _mm(act, w["w_down"], trans_b=False, out_dtype=BF16, name="mm_down")

    loss_blk, d_h2a, d_dn, d_ada3, g["final_norm_w"] = _loss_head(h2, dn, target, ada, w["final_norm_w"], nct)
    rows16 = lambda gw: (gw.reshape(N_DEV, gw.shape[0] // N_DEV, gw.shape[1]).astype(BF16), True)
    gw_down = _mm_tn(act, d_dn, name="dw_down", tr_t=1408)
    d_act = _mm(d_dn, w["w_down"], trans_b=True, out_dtype=BF16, name="dx_down", tn_t=1408)
    d_gu, got["w_down"] = _stage_bwd(_f_swiglu, n_tm, in_sw, [], [_col(d_act, TM)], lambda i, t: t, [(BF16, None)],
                                     "swiglu_b", hosted=[rows16(gw_down)])
    gwt_gu = _mm_tn(d_gu, hm2, name="dw_gu", tr_t=1408)
    d_hm2 = _mm(d_gu, w["wt_gu"], trans_b=False, out_dtype=BF16, name="dx_gu")
    d_x1, d_mix, d_ada2, g["norm2_w"], got["w_gate"] = _stage_bwd(
        f_res1, n_tm, in_r1, p_r1, [_col(d_h2a, TM), _col(d_hm2, TM)], lambda i, t: t,
        [None, (F32, None), (BF16, None)], "res1_b", hosted=[rows16(gwt_gu[:D_FF])])
    gw_out = _mm_tn(merged, d_mix, name="dw_out")
    d_merged = _mm(d_mix, w["w_out"], trans_b=True, out_dtype=BF16, name="dx_out")
    d_ya, d_yb, d_proj, got["w_up"] = _stage_bwd(
        _f_merge, n_tl, in_mg, [], [_col(d_merged, tl)], lambda i, t: t, [None, None, (BF16, None), (BF16, None)], "merge_b",
        hosted=[rows16(gwt_gu[D_FF:])], pack=([0, 1], None, D_INP, C_GA // (2 * D)))
    gw_pa = _mm_tn(oa, d_ya, name="dw_pa")
    gw_pb = _mm_tn(ob, d_yb, name="dw_pb")
    d_oa = _mm(d_ya, w["w_pa"], trans_b=True, out_dtype=BF16, name="dx_pa")
    d_ob = _mm(d_yb, w["w_pb"], trans_b=True, out_dtype=BF16, name="dx_pb")
    d_og, g["gla_norm_w"], d_proj, got["w_out"] = _stage_bwd(
        _f_gla_post, n_tl, in_gp, p_gp, [_col(d_oa, tl)], lambda i, t: t, [(BF16, None), None, None], "gla_post_b",
        hosted=[rows16(gw_out)], pack=([2], d_proj, D_INP, C_R // D))
    d_ys, d_xs_skip, g["dsk8"], g["ssm_norm_w"], d_proj, got["w_pa"], got["w_pb"] = _stage_bwd(
        _f_ssd_post, n_tm, in_sp, p_sp, [_col(d_ob, TM)], lambda i, t: t,
        [(BF16, None), None, (BF16, None), None], "ssd_post_b", hosted=[rows16(gw_pa), rows16(gw_pb)],
        pack=([3], d_proj, D_INP, C_Z // SSM_INNER))

    dq, dk, dv, d_lgf = _gla_bwd(proj, lg_f, sv_gf, d_og, None, ncc_g, False, "gla_f_b")
    d_proj, d_lgb = _gla_bwd(proj, lg_b, sv_gb, d_og, (dq, dk, dv), ncc_g, True, "gla_b_b", pack=(d_proj, D_INP))
    d_xf, d_bm, d_cm, d_axf, d_atf = _ssd_bwd(xf, xbc, axf, atf, sv_sf, d_ys, None, ncc_s, False, "ssd_f_b")
    d_xb, d_bm, d_cm, d_axb, d_atb = _ssd_bwd(xb, xbc, axb, atb, sv_sb, d_ys, (d_bm, d_cm), ncc_s, True, "ssd_b_b")

    cts_gd = [_col(a, TS) for a in (d_lgf, d_lgb, d_xf, d_xb, d_axf, d_axb)]
    cts_gd += [(a, (n_heads, TS), lambda i: (0, i)) for a in (d_atf, d_atb)]
    d_xs_dt, g["up_f"], g["up_b"], g["gla_bias_f"], g["gla_bias_b"], g["dtb"], g["alog"], d_proj = _stage_bwd(
        _f_gates_dt, n_ts, in_gd, p_gd, cts_gd, lambda i, t: t, [None, (BF16, None)], "gates_dt_b",
        pack=([0], d_proj, D_INP, C_S // 128))
    cts_conv = [_col(d_xs_skip, TM), _col(d_xs_dt, TM), _col(d_bm, TM), _col(d_cm, TM)]
    g["conv_w"], g["conv_b"], d_proj = _stage_bwd(
        f_conv, n_tm, in_conv, p_conv, cts_conv,
        lambda i, t: [jnp.concatenate([t[0] + t[1], t[2], t[3]], axis=1)], [None], "conv_b",
        pack=([0], d_proj, D_INP, C_U // 3072))
    gwt_in = _mm_tn(d_proj, h1, name="dw_in", tr_t=1152)
    d_h1, got["w_in"] = _mm(d_proj, w["wt_in"], trans_b=False, out_dtype=BF16, name="dx_in", tm_t=768, tk_t=3456,
                            hosted=[rows16(_from_padded(gwt_in))])
    grad_x, g["norm1_w"], d_ada1 = _stage_bwd(
        f_norm1, n_tm, x_in, p_norm1, [_col(d_h1, TM)], lambda i, t: t,
        [None, (F32, (d_x1, (TM, D), x_in[1][2]))], "norm1_b")
    return loss_blk, grad_x, g, got, (d_ada1, d_ada2, d_ada3)


def _exchange(x, a2a, name):
    shp = x.shape[1:] if a2a else x.shape

    def body(x_ref, o_ref, send_sems, recv_sems, local_sem):
        ops = _exchange_ops(x_ref, o_ref, send_sems, recv_sems, local_sem, a2a)
        for op in ops:
            op.start()
        for op in ops:
            op.wait()

    return pl.pallas_call(
        body, out_shape=jax.ShapeDtypeStruct((N_DEV,) + tuple(shp), x.dtype),
        in_specs=[pl.BlockSpec(memory_space=pl.ANY)], out_specs=pl.BlockSpec(memory_space=pl.ANY),
        scratch_shapes=[pltpu.SemaphoreType.DMA((N_DEV - 1,)), pltpu.SemaphoreType.DMA((N_DEV - 1,)),
                        pltpu.SemaphoreType.DMA(())],
        name=name,
    )(x)


def _all_gather(x, name):
    def body(x_ref, o_ref, send_sems, recv_sems, local_sem):
        mx, my, mc = lax.axis_index("x"), lax.axis_index("y"), lax.axis_index("c")
        me, sibling = (mx, my, mc), (mx, my, 1 - mc)
        chips = [(1 - mx, my), (mx, 1 - my), (1 - mx, 1 - my)]

        def slot(px, py, pc):
            return o_ref.at[4 * px + 2 * py + pc]

        def copy(k, block, to, src=None):
            return pltpu.make_async_remote_copy(
                src_ref=slot(*block) if src is None else src, dst_ref=slot(*block),
                send_sem=send_sems.at[k], recv_sem=recv_sems.at[k], device_id=to, device_id_type=pl.DeviceIdType.MESH)

        own = pltpu.make_async_copy(x_ref, slot(*me), local_sem)
        own.start()
        first = [copy(0, me, sibling, src=x_ref)] + [copy(1 + j, me, (*chip, mc), src=x_ref) for j, chip in enumerate(chips)]
        for cp in first:
            cp.start()
        passed = [copy(4 + j, (*chip, mc), sibling) for j, chip in enumerate(chips)]
        for j, chip in enumerate(chips):
            copy(1 + j, (*chip, mc), me).wait_recv()
            passed[j].start()
        copy(0, sibling, me).wait_recv()
        for j, chip in enumerate(chips):
            copy(4 + j, (*chip, 1 - mc), me).wait_recv()
        for cp in first + passed:
            cp.wait_send()
        own.wait()

    return pl.pallas_call(
        body, out_shape=jax.ShapeDtypeStruct((N_DEV,) + tuple(x.shape), x.dtype),
        in_specs=[pl.BlockSpec(memory_space=pl.ANY)], out_specs=pl.BlockSpec(memory_space=pl.ANY),
        scratch_shapes=[pltpu.SemaphoreType.DMA((N_DEV - 1,)), pltpu.SemaphoreType.DMA((N_DEV - 1,)),
                        pltpu.SemaphoreType.DMA(())],
        name=name,
    )(x)


def _adamw_math(w, gr, m, v):
    m = ADAM_B1 * m + (1.0 - ADAM_B1) * gr
    v = ADAM_B2 * v + (1.0 - ADAM_B2) * (gr * gr)
    m_hat = m / np.float32(1.0 - ADAM_B1 ** ADAM_STEP)
    v_hat = v / np.float32(1.0 - ADAM_B2 ** ADAM_STEP)
    delta = -ADAM_LR * (m_hat / (jnp.sqrt(v_hat) + ADAM_EPS) + ADAM_WD * w)
    return delta, m, v


def _sum_adamw(parts, w, m, v, name):
    R, C = w.shape
    n_parts = parts.shape[0]
    tr = _pick(R, max(16, (2 << 20) // (4 * C) // 16 * 16), 16)

    def body(p_ref, w_ref, m_ref, v_ref, g_ref, d_ref, mo_ref, vo_ref):
        gr = p_ref[0].astype(F32)
        for k in range(1, n_parts):
            gr = gr + p_ref[k].astype(F32)
        g_ref[...] = gr
        d_ref[...], mo_ref[...], vo_ref[...] = _adamw_math(w_ref[...], gr, m_ref[...], v_ref[...])

    if R % 16 == 0:
        grid = (R // tr,)
        tile, p_spec = pl.BlockSpec((tr, C), lambda i: (i, 0)), pl.BlockSpec((n_parts, tr, C), lambda i: (0, i, 0))
    else:
        tc = _pick(C, max(128, (2 << 20) // (4 * R) // 128 * 128), 128)
        grid = (C // tc,)
        tile, p_spec = pl.BlockSpec((R, tc), lambda i: (0, i)), pl.BlockSpec((n_parts, R, tc), lambda i: (0, 0, i))
    return pl.pallas_call(
        body, grid=grid, in_specs=[p_spec, tile, tile, tile],
        out_specs=[tile] * 4, out_shape=[jax.ShapeDtypeStruct((R, C), F32)] * 4,
        compiler_params=_cparams(("parallel",)), name=name,
    )(parts, w, m, v)


def _sum8(parts, name):
    _, R, C = parts.shape

    def body(p_ref, g_ref):
        gr = p_ref[0]
        for k in range(1, N_DEV):
            gr = gr + p_ref[k]
        g_ref[...] = gr

    return pl.pallas_call(body, out_shape=jax.ShapeDtypeStruct((R, C), F32), name=name)(parts)


def _adamw(w, gr, m, v, name):
    def body(w_ref, g_ref, m_ref, v_ref, d_ref, mo_ref, vo_ref):
        d_ref[...], mo_ref[...], vo_ref[...] = _adamw_math(w_ref[...], g_ref[...], m_ref[...], v_ref[...])

    return pl.pallas_call(body, out_shape=[jax.ShapeDtypeStruct(w.shape, F32)] * 3, name=name)(w, gr, m, v)


def _to_padded(wt_in):
    z = jnp.zeros((32,) + wt_in.shape[1:], wt_in.dtype)
    return jnp.concatenate([wt_in[0:3072], wt_in[5152:8224], wt_in[3104:5152], wt_in[8288:10336],
                            wt_in[3072:3104], wt_in[8224:8288], z], axis=0)


def _from_padded(p):
    return jnp.concatenate([p[0:3072], p[10240:10272], p[6144:8192], p[3072:6144], p[10272:10336], p[8192:10240]], axis=0)


def _unshard_cols(gathered):
    n, r, c = gathered.shape
    return jnp.transpose(gathered, (1, 0, 2)).reshape(r, n * c)


def _lanes(vec, lo):
    return jnp.concatenate([jnp.zeros((1, lo), F32), vec, jnp.zeros((1, 128 - lo - vec.shape[1]), F32)], axis=1)


_SMALL = (("b_ada", 6 * D), ("c_ctx", D), ("norm1_w", D), ("gla_bias_f", 512), ("gla_bias_b", 512), ("gla_norm_w", 256),
          ("conv_b", 3072), ("dt_bias_f", 32), ("dt_bias_b", 32), ("a_log_f", 32), ("a_log_b", 32), ("d_skip", 32),
          ("ssm_norm_w", 2048), ("norm2_w", D), ("final_norm_w", D))
_SHARDED_SMALL = (("gla_up_f", 16 * 512), ("gla_up_b", 16 * 512), ("conv_w", 4 * 3072))


def _pack(vals, names):
    flat = jnp.concatenate([vals[n].reshape(-1).astype(F32) for n, _ in names])
    pad = (-flat.shape[0]) % 1024
    return jnp.concatenate([flat, jnp.zeros((pad,), F32)]).reshape(-1, 128)


def _unpack(packed, names, shapes):
    flat, out, off = packed.reshape(-1), {}, 0
    for n, size in names:
        out[n] = flat[off:off + size].reshape(shapes[n])
        off += size
    return out


def kernel(x, c, ctx, c_ctx, w_ada, b_ada, norm1_w, w_in, gla_up_f, gla_bias_f, gla_up_b, gla_bias_b, gla_norm_w, conv_w, conv_b, dt_bias_f, dt_bias_b, a_log_f, a_log_b, d_skip, ssm_norm_w, w_pa, w_pb, w_out, norm2_w, w_gate, w_up, w_down, final_norm_w, loss_target, m_c_ctx, m_w_ada, m_b_ada, m_norm1_w, m_w_in, m_gla_up_f, m_gla_bias_f, m_gla_up_b, m_gla_bias_b, m_gla_norm_w, m_conv_w, m_conv_b, m_dt_bias_f, m_dt_bias_b, m_a_log_f, m_a_log_b, m_d_skip, m_ssm_norm_w, m_w_pa, m_w_pb, m_w_out, m_norm2_w, m_w_gate, m_w_up, m_w_down, m_final_norm_w, v_c_ctx, v_w_ada, v_b_ada, v_norm1_w, v_w_in, v_gla_up_f, v_gla_bias_f, v_gla_up_b, v_gla_bias_b, v_gla_norm_w, v_conv_w, v_conv_b, v_dt_bias_f, v_dt_bias_b, v_a_log_f, v_a_log_b, v_d_skip, v_ssm_norm_w, v_w_pa, v_w_pb, v_w_out, v_norm2_w, v_w_gate, v_w_up, v_w_down, v_final_norm_w):
    args = dict(locals())
    me = 4 * lax.axis_index("x") + 2 * lax.axis_index("y") + lax.axis_index("c")

    t_names = ("w_in", "w_gate", "w_up")
    local = {n: (jnp.transpose(args[n][0]) if n in t_names else args[n][0]) for n in
             ("w_in", "w_pa", "w_pb", "w_out", "w_gate", "w_up", "w_down")}
    full = {"wt_in": _to_padded(_all_gather(local["w_in"].astype(BF16), "ag_w_in").reshape(D_IN, D))}
    shards = {n: local[n].astype(BF16) for n in ("w_pa", "w_pb", "w_out", "w_gate", "w_up", "w_down")}
    sm = _exchange(jnp.concatenate([gla_up_f.reshape(-1), gla_up_b.reshape(-1), conv_w.reshape(-1), c.reshape(-1)]).reshape(-1, 128),
                   False, "ag_small")
    sm = sm.reshape(N_DEV, -1)
    n_ada = w_ada.shape[2]
    cc = jnp.concatenate([sm[:, 3584:3584 + D], c_ctx[None], jnp.zeros((_ADA_ROWS - N_DEV - 1, D), F32)], axis=0)
    w_ada16 = w_ada[0].astype(BF16)
    ada_cols = _exchange(_ada_fwd(cc, w_ada16, lax.dynamic_slice(b_ada, (0, me * n_ada), (1, n_ada))), False, "ag_ada")
    ada = jnp.concatenate([lax.dynamic_index_in_dim(ada_cols, me, axis=1, keepdims=False).reshape(1, -1),
                           ada_cols[:, N_DEV].reshape(1, -1), jnp.zeros((6, N_DEV * n_ada), F32)], axis=0)
    up_f = _unshard_cols(sm[:, 0:1024].reshape(N_DEV, 16, 64))
    up_b = _unshard_cols(sm[:, 1024:2048].reshape(N_DEV, 16, 64))
    full["conv_w"] = _unshard_cols(sm[:, 2048:3584].reshape(N_DEV, 4, 384))
    full["up_f"] = jnp.concatenate([up_f, jnp.zeros((112, 512), F32)], axis=0)
    full["up_b"] = jnp.concatenate([jnp.zeros((16, 512), F32), up_b, jnp.zeros((96, 512), F32)], axis=0)
    full["dtb"] = jnp.concatenate([jnp.zeros((1, 32), F32), dt_bias_f, dt_bias_b, jnp.zeros((1, 32), F32)], axis=1)
    full["alog"] = jnp.concatenate([jnp.zeros((1, 32), F32), a_log_f, a_log_b, jnp.zeros((1, 32), F32)], axis=1)
    full["dsk8"] = jnp.concatenate([_lanes(d_skip, 0), jnp.zeros((7, 128), F32)], axis=0)
    for n in ("norm1_w", "gla_bias_f", "gla_bias_b", "gla_norm_w", "conv_b", "ssm_norm_w", "norm2_w"):
        full[n] = args[n]
    full["final_norm_w"] = final_norm_w.reshape(1, D)

    loss_blk, grad_x, g, got, d_ada = _local_step(x[0], ctx[0], loss_target[0], ada, full, shards)

    d_all = _exchange(jnp.concatenate([d[0:2] for d in d_ada], axis=0), False, "ag_d_ada").reshape(N_DEV * 6, N_DEV * n_ada)
    gw_ada, gb_ada, d_cc = _ada_bwd(cc, w_ada16, d_all, lax.dynamic_slice(d_all, (0, me * n_ada), (N_DEV * 6, n_ada)))

    gs = dict(g, c_ctx=d_cc[N_DEV], b_ada=jnp.zeros_like(b_ada))
    gs["dt_bias_f"], gs["dt_bias_b"] = g["dtb"][:, 32:64], g["dtb"][:, 64:96]
    gs["a_log_f"], gs["a_log_b"] = g["alog"][:, 32:64], g["alog"][:, 64:96]
    gs["d_skip"] = g["dsk8"][0:1, 0:32]
    gs["gla_up_f"], gs["gla_up_b"] = g["up_f"][0:16], g["up_b"][16:32]
    gs["loss"] = loss_blk[0:1, 0:1]
    names = _SMALL + _SHARDED_SMALL + (("loss", 1),)
    shapes = {n: (args[n].shape if n in args else (1, 1)) for n, _ in names}
    shapes.update({"gla_up_f": (16, 512), "gla_up_b": (16, 512), "conv_w": (4, 3072)})
    red = _unpack(_sum8(_exchange(_pack(gs, names), False, "ag_small_grads"), "sum_small_grads"), names, shapes)
    loss = red["loss"].reshape(())
    grads = {n: red[n] for n, _ in _SMALL}
    grads["b_ada"] = gb_ada
    grads["gla_up_f"] = lax.dynamic_slice(red["gla_up_f"], (0, me * 64), (16, 64))[None]
    grads["gla_up_b"] = lax.dynamic_slice(red["gla_up_b"], (0, me * 64), (16, 64))[None]
    grads["conv_w"] = lax.dynamic_slice(red["conv_w"], (0, me * 384), (4, 384))[None]
    upd_names = tuple((n, s) for n, s in _SMALL) + (("gla_up_f", 1024), ("gla_up_b", 1024), ("conv_w", 1536))
    pk = lambda prefix, src: _pack({n: src[prefix + n] for n, _ in upd_names}, upd_names)
    d_s, m_s, v_s = _adamw(pk("", args), _pack(grads, upd_names), pk("m_", args), pk("v_", args), "adamw_small")
    upd_shapes = {n: args[n].shape for n, _ in upd_names}
    delta = _unpack(d_s, upd_names, upd_shapes)
    new_m = _unpack(m_s, upd_names, upd_shapes)
    new_v = _unpack(v_s, upd_names, upd_shapes)

    got["w_ada"] = gw_ada[None]
    for n, parts in got.items():
        orient = jnp.transpose if n in t_names else (lambda a: a)
        res = _sum_adamw(parts, local[n] if n in local else args[n][0], orient(args["m_" + n][0]), orient(args["v_" + n][0]),
                         "adamw_" + n)
        grads[n], delta[n], new_m[n], new_v[n] = [orient(r)[None] for r in res]

    order = ["c_ctx", "w_ada", "b_ada", "norm1_w", "w_in", "gla_up_f", "gla_bias_f", "gla_up_b", "gla_bias_b", "gla_norm_w",
             "conv_w", "conv_b", "dt_bias_f", "dt_bias_b", "a_log_f", "a_log_b", "d_skip", "ssm_norm_w", "w_pa", "w_pb",
             "w_out", "norm2_w", "w_gate", "w_up", "w_down", "final_norm_w"]
    fix = lambda d: [d[n].reshape(args[n].shape) for n in order]
    return (loss, grad_x[None], *fix(grads), *fix(delta), *fix(new_m), *fix(new_v))
```

```python
import functools

import jax
import jax.numpy as jnp
import numpy as np
from jax import lax
from jax.experimental import pallas as pl
from jax.experimental.pallas import tpu as pltpu

F32 = jnp.float32
BF16 = jnp.bfloat16
N_DEV = 8
D = 1024
EPS = 1e-6
GRID_W = 64
GLA_H, GLA_DK, GLA_DV = 4, 128, 256
GLA_C = 64
GLA_TAU = 16.0
SSM_G, SSM_HPG, SSM_P, SSM_N = 4, 8, 64, 128
SSM_C = 128
SSM_INNER = 2048
D_FF = 2816
D_IN = 10336
D_INP = 10368
C_Q, C_K, C_V, C_R, C_U, C_Z, C_GA, C_GB, C_S = 0, 512, 1024, 2048, 3072, 6144, 8192, 9216, 10240
TM = 256
TS = 128

ADAM_LR, ADAM_B1, ADAM_B2, ADAM_EPS, ADAM_WD, ADAM_STEP = 0.001, 0.9, 0.999, 1e-08, 0.01, 10
VMEM_LIMIT = 56 << 20


def _cparams(sem):
    return pltpu.CompilerParams(dimension_semantics=sem or None, vmem_limit_bytes=VMEM_LIMIT)


def _pick(n, target, mult):
    best = None
    for t in range(mult, min(n, target) + 1, mult):
        if n % t == 0:
            best = t
    return best if best is not None else n


def _dot_impl(a, b, ta, tb):
    dims = (((0 if ta else 1,), (1 if tb else 0,)), ((), ()))
    return lax.dot_general(a.astype(BF16), b.astype(BF16), dims, preferred_element_type=F32)


@functools.partial(jax.custom_vjp, nondiff_argnums=(2, 3))
def _bdot(a, b, ta=False, tb=False):
    return _dot_impl(a, b, ta, tb)


def _bdot_fwd(a, b, ta, tb):
    return _dot_impl(a, b, ta, tb), (a, b)


def _bdot_bwd(ta, tb, res, g):
    a, b = res
    if not ta and not tb:
        return _dot_impl(g, b, False, True), _dot_impl(a, g, True, False)
    if not ta and tb:
        return _dot_impl(g, b, False, False), _dot_impl(g, a, True, False)
    if ta and not tb:
        return _dot_impl(b, g, False, True), _dot_impl(a, g, False, False)
    raise NotImplementedError


_bdot.defvjp(_bdot_fwd, _bdot_bwd)


_NN = (((1,), (0,)), ((), ()))
_NT = (((1,), (1,)), ((), ()))
_TN = (((0,), (0,)), ((), ()))


def _dg2(x, e, x_is_lhs, dims):
    hi = x.astype(BF16)
    lo = (x - hi.astype(F32)).astype(BF16)
    e = e.astype(BF16)
    if x_is_lhs:
        return (lax.dot_general(hi, e, dims, preferred_element_type=F32)
                + lax.dot_general(lo, e, dims, preferred_element_type=F32))
    return (lax.dot_general(e, hi, dims, preferred_element_type=F32)
            + lax.dot_general(e, lo, dims, preferred_element_type=F32))


_EDOT_FWD = {"xe": (True, _NN), "ex": (False, _NN), "ext": (False, _NT)}
_EDOT_BWD = {"xe": (True, _NT), "ex": (False, _TN), "ext": (True, _TN)}


@functools.partial(jax.custom_vjp, nondiff_argnums=(2,))
def _edot(x, e, mode):
    return _dg2(x, e, *_EDOT_FWD[mode])


def _edot_fwd(x, e, mode):
    return _dg2(x, e, *_EDOT_FWD[mode]), e


def _edot_bwd(mode, e, g):
    return _dg2(g, e, *_EDOT_BWD[mode]), None


_edot.defvjp(_edot_fwd, _edot_bwd)


def _shift_impl(u, pos, per, s):
    n = u.shape[0]
    rolled = u if s == 0 else pltpu.roll(u, (-s) % n, 0)
    ok = (pos + s >= 0) & (pos + s < per)
    return jnp.where(ok, rolled, 0.0)


@functools.partial(jax.custom_vjp, nondiff_argnums=(3,))
def _shift(u, pos, per, s):
    return _shift_impl(u, pos, per, s)


def _shift_fwd(u, pos, per, s):
    return _shift_impl(u, pos, per, s), (pos, per)


def _shift_bwd(s, res, g):
    pos, per = res
    return _shift_impl(g, pos, per, -s), None, None


_shift.defvjp(_shift_fwd, _shift_bwd)


def _rms(x, w):
    return x * lax.rsqrt(jnp.mean(x * x, axis=-1, keepdims=True) + EPS) * w


def _silu(x):
    return x * jax.nn.sigmoid(x)


def _softplus(x):
    return jnp.maximum(x, 0.0) + jnp.log(1.0 + jnp.exp(-jnp.abs(x)))


def _logsig(x):
    return jnp.minimum(x, 0.0) - jnp.log(1.0 + jnp.exp(-jnp.abs(x)))


def _tri(n, rev):
    t = lax.broadcasted_iota(jnp.int32, (n, n), 0)
    s = lax.broadcasted_iota(jnp.int32, (n, n), 1)
    return (s >= t) if rev else (t >= s)


def _head_expand(first_lane):
    l = lax.broadcasted_iota(jnp.int32, (128, SSM_INNER), 0)
    c = lax.broadcasted_iota(jnp.int32, (128, SSM_INNER), 1)
    return (l == first_lane + lax.shift_right_logical(c, 6)).astype(F32)


def _exchange_ops(x_ref, o_ref, send_sems, recv_sems, local_sem, a2a):
    mx, my, mc = lax.axis_index("x"), lax.axis_index("y"), lax.axis_index("c")
    me = 4 * mx + 2 * my + mc
    ops = []
    for k in range(1, N_DEV):
        px = 1 - mx if k & 4 else mx
        py = 1 - my if k & 2 else my
        pc = 1 - mc if k & 1 else mc
        ops.append(pltpu.make_async_remote_copy(
            src_ref=x_ref.at[4 * px + 2 * py + pc] if a2a else x_ref, dst_ref=o_ref.at[me],
            send_sem=send_sems.at[k - 1], recv_sem=recv_sems.at[k - 1],
            device_id=(px, py, pc), device_id_type=pl.DeviceIdType.MESH))
    ops.append(pltpu.make_async_copy(x_ref.at[me] if a2a else x_ref, o_ref.at[me], local_sem))
    return ops


def _call(body, *, grid, in_specs, out_specs, out_shape, scratch=(), sem, name, args, hosted=(), aliases=None):
    n_in, n_out, n_s, n_h = len(in_specs), len(out_specs), len(scratch), len(hosted)
    aliases = aliases or {}
    if not n_h:
        return pl.pallas_call(body, grid=grid, in_specs=list(in_specs), out_specs=list(out_specs),
                              out_shape=list(out_shape), scratch_shapes=list(scratch), input_output_aliases=aliases,
                              compiler_params=_cparams(sem), name=name)(*args)
    b0, c0 = n_in + n_h, n_in + n_h + n_out
    d0 = c0 + n_h

    def wrapped(*refs):
        hx, ho = refs[n_in:b0], refs[c0:d0]
        send_sems, recv_sems, local_sems = refs[d0 + n_s:]
        ids = [pl.program_id(ax) for ax in range(len(grid))]
        first = functools.reduce(jnp.logical_and, [i == 0 for i in ids])
        last = functools.reduce(jnp.logical_and, [i == n - 1 for i, n in zip(ids, grid)])

        def ops():
            return [op for n, (_, a2a) in enumerate(hosted)
                    for op in _exchange_ops(hx[n], ho[n], send_sems.at[n], recv_sems.at[n], local_sems.at[n], a2a)]

        @pl.when(first)
        def _():
            for op in ops():
                op.start()

        body(*refs[:n_in], *refs[b0:c0], *refs[d0:d0 + n_s])

        @pl.when(last)
        def _():
            for op in ops():
                op.wait()

    hbm = pl.BlockSpec(memory_space=pl.ANY)
    h_shapes = [jax.ShapeDtypeStruct((N_DEV,) + tuple(x.shape[1:] if a2a else x.shape), x.dtype) for x, a2a in hosted]
    return pl.pallas_call(
        wrapped, grid=grid, in_specs=list(in_specs) + [hbm] * n_h, out_specs=list(out_specs) + [hbm] * n_h,
        out_shape=list(out_shape) + h_shapes, input_output_aliases=aliases,
        scratch_shapes=list(scratch) + [pltpu.SemaphoreType.DMA((n_h, N_DEV - 1)), pltpu.SemaphoreType.DMA((n_h, N_DEV - 1)),
                                        pltpu.SemaphoreType.DMA((n_h,))],
        compiler_params=_cparams(("arbitrary",) * len(grid)), name=name,
    )(*args, *[x for x, _ in hosted])


def _mm(a, b, *, trans_b, out_dtype, name, tm_t=1408, tn_t=1024, tk_t=2816, hosted=()):
    M, K = a.shape
    N = b.shape[0] if trans_b else b.shape[1]
    tm, tn, tk = _pick(M, tm_t, 8), _pick(N, tn_t, 128), _pick(K, tk_t, 128)
    nk = K // tk
    dims = (((1,), (1,)), ((), ())) if trans_b else (((1,), (0,)), ((), ()))

    def body_one(a_ref, b_ref, o_ref):
        o_ref[...] = lax.dot_general(a_ref[...], b_ref[...], dims, preferred_element_type=F32).astype(out_dtype)

    def body_acc(a_ref, b_ref, o_ref, acc_ref):
        k = pl.program_id(2)
        p = lax.dot_general(a_ref[...], b_ref[...], dims, preferred_element_type=F32)

        @pl.when(k == 0)
        def _():
            acc_ref[...] = p

        @pl.when(k > 0)
        def _():
            acc_ref[...] += p

        @pl.when(k == nk - 1)
        def _():
            o_ref[...] = acc_ref[...].astype(out_dtype)

    b_spec = pl.BlockSpec((tn, tk), lambda j, i, k: (j, k)) if trans_b else pl.BlockSpec((tk, tn), lambda j, i, k: (k, j))
    outs = _call(
        body_one if nk == 1 else body_acc, grid=(N // tn, M // tm, nk),
        in_specs=[pl.BlockSpec((tm, tk), lambda j, i, k: (i, k)), b_spec],
        out_specs=[pl.BlockSpec((tm, tn), lambda j, i, k: (i, j))],
        out_shape=[jax.ShapeDtypeStruct((M, N), out_dtype)],
        scratch=[] if nk == 1 else [pltpu.VMEM((tm, tn), F32)],
        sem=("parallel", "parallel", "arbitrary"), name=name, args=(a, b), hosted=hosted)
    return outs if hosted else outs[0]


def _mm_tn(a, b, *, name, tm_t=2816, tr_t=1024, tn_t=1408):
    M, R = a.shape
    N = b.shape[1]
    tm, tr, tn = _pick(M, tm_t, 8), _pick(R, tr_t, 128), _pick(N, tn_t, 128)

    def body(a_ref, b_ref, o_ref):
        m = pl.program_id(2)
        p = lax.dot_general(a_ref[...], b_ref[...], (((0,), (0,)), ((), ())), preferred_element_type=F32)

        @pl.when(m == 0)
        def _():
            o_ref[...] = p

        @pl.when(m > 0)
        def _():
            o_ref[...] += p

    return pl.pallas_call(
        body, grid=(R // tr, N // tn, M // tm),
        in_specs=[pl.BlockSpec((tm, tr), lambda r, j, m: (m, r)), pl.BlockSpec((tm, tn), lambda r, j, m: (m, j))],
        out_specs=pl.BlockSpec((tr, tn), lambda r, j, m: (r, j)),
        out_shape=jax.ShapeDtypeStruct((R, N), F32),
        compiler_params=_cparams(("parallel", "parallel", "arbitrary")), name=name,
    )(a, b)


def _col(arr, tm, width=None, col=0):
    width = arr.shape[1] if width is None else width
    return (arr, (tm, width), lambda i: (i, col))


def _lat(arr, tm, nct):
    return (arr, (tm, arr.shape[1]), lambda i: (jnp.maximum(i - nct, 0), 0))


def _ctx(arr, tm, nct):
    return (arr, (tm, arr.shape[1]), lambda i: (jnp.minimum(i, nct - 1), 0))


def _whole(p):
    return pl.BlockSpec(p.shape, lambda i, nd=p.ndim: (0,) * nd)


def _stage_fwd(f, n_tiles, ins, params, outs, name):
    ni, npar = len(ins), len(params)
    o_specs = [pl.BlockSpec((o[1], o[3]), lambda i: (0, i)) if len(o) > 4 else pl.BlockSpec((o[3], o[1]), lambda i: (i, 0))
               for o in outs]
    o_shapes = [jax.ShapeDtypeStruct((o[1], o[0]) if len(o) > 4 else (o[0], o[1]), o[2]) for o in outs]

    def body(*refs):
        i = pl.program_id(0)
        xs = [r[...].astype(F32) for r in refs[:ni]]
        ps = [r[...] for r in refs[ni:ni + npar]]
        for r, v in zip(refs[ni + npar:], f(i, xs, ps, False)):
            r[...] = v.astype(r.dtype)

    return pl.pallas_call(
        body, grid=(n_tiles,),
        in_specs=[pl.BlockSpec(bs, fn) for _, bs, fn in ins] + [_whole(p) for p in params],
        out_specs=o_specs, out_shape=o_shapes,
        compiler_params=_cparams(("parallel",)), name=name,
    )(*[a for a, _, _ in ins], *params)


def _stage_bwd(f, n_tiles, ins, params, cts, ct_fn, dins, name, hosted=(), pack=None):
    ni, npar, nc = len(ins), len(params), len(cts)
    want = [k for k, d in enumerate(dins) if d is not None]
    extras = [dins[k][1] for k in want if dins[k][1] is not None]
    n_buf = 0 if pack is None or pack[1] is None else 1
    n_main_in = ni + npar + nc + len(extras)

    def body(*refs):
        i = pl.program_id(0)
        xs = [r[...].astype(F32) for r in refs[:ni]]
        ps = [r[...] for r in refs[ni:ni + npar]]
        ct_tiles = [r[...].astype(F32) for r in refs[ni + npar:ni + npar + nc]]
        ex_refs = list(refs[ni + npar + nc:n_main_in])
        out_refs = refs[n_main_in + n_buf:]
        _, vjp = jax.vjp(lambda xs_, ps_: tuple(f(i, xs_, ps_, True)), xs, ps)
        dxs, dps = vjp(tuple(ct_fn(i, ct_tiles)))
        for n, k in enumerate(want):
            v = dxs[k]
            if dins[k][1] is not None:
                v = v + ex_refs.pop(0)[...].astype(F32)
            out_refs[n][...] = v.astype(out_refs[n].dtype)
        if pack is not None:
            packed = [dxs[k] for k in pack[0]]
            r = out_refs[len(want) + npar]
            r[...] = (packed[0] if len(packed) == 1 else jnp.concatenate(packed, axis=1)).astype(r.dtype)
        for r, v in zip(out_refs[len(want):len(want) + npar], dps):
            @pl.when(i == 0)
            def _(r=r, v=v):
                r[...] = v

            @pl.when(i > 0)
            def _(r=r, v=v):
                r[...] += v

    in_specs = ([pl.BlockSpec(bs, fn) for _, bs, fn in ins] + [_whole(p) for p in params]
                + [pl.BlockSpec(bs, fn) for _, bs, fn in cts] + [pl.BlockSpec(bs, fn) for _, bs, fn in extras])
    out_specs = ([pl.BlockSpec(ins[k][1], lambda i, fn=ins[k][2]: (fn(i)[0], 0)) for k in want] + [_whole(p) for p in params])
    out_shape = ([jax.ShapeDtypeStruct((ins[k][0].shape[0], ins[k][1][1]), dins[k][0]) for k in want]
                 + [jax.ShapeDtypeStruct(p.shape, F32) for p in params])
    args = [*[a for a, _, _ in ins], *params, *[a for a, _, _ in cts], *[a for a, _, _ in extras]]
    aliases = None
    if pack is not None:
        ks, buf, total_cols, block = pack
        width = sum(ins[k][1][1] for k in ks)
        out_specs.append(pl.BlockSpec((ins[ks[0]][1][0], width), lambda i, fn=ins[ks[0]][2]: (fn(i)[0], block)))
        out_shape.append(jax.ShapeDtypeStruct((ins[ks[0]][0].shape[0], total_cols), BF16))
        if buf is not None:
            in_specs.append(pl.BlockSpec(memory_space=pl.ANY))
            args.append(buf)
            aliases = {len(args) - 1: len(out_specs) - 1}
    return _call(body, grid=(n_tiles,), in_specs=in_specs, out_specs=out_specs, out_shape=out_shape,
                 sem=("arbitrary",), name=name, args=args, hosted=hosted, aliases=aliases)


def _mod_rows(i, nct, ada, lo):
    return jnp.where(i < nct, ada[1:2, lo:lo + D], ada[0:1, lo:lo + D])


def _f_norm1(nct):
    def f(i, xs, ps, diff):
        ctx, x = xs
        nw, ada = ps
        xt = jnp.where(i < nct, ctx, x)
        return (_rms(xt, nw) * (1.0 + _mod_rows(i, nct, ada, D)) + _mod_rows(i, nct, ada, 0),)
    return f


def _f_gates_dt(i, xs, ps, diff):
    small, xc = xs
    up_f, up_b, gb_f, gb_b, dtb, alog = ps
    dot = _bdot if diff else _dot_impl_nn
    lg_f = _logsig(dot(small, up_f) + gb_f) * (1.0 / GLA_TAU)
    lg_b = _logsig(dot(small, up_b) + gb_b) * (1.0 / GLA_TAU)
    dtp = _softplus(small + dtb)
    aa = -jnp.exp(alog) * dtp
    e_f, e_b = _head_expand(32), _head_expand(64)
    n = small.shape[0]
    t = lax.broadcasted_iota(jnp.int32, (n, n), 0)
    s = lax.broadcasted_iota(jnp.int32, (n, n), 1)
    same = lax.shift_right_logical(t, 6) == lax.shift_right_logical(s, 6)
    pre_g, suf_g = ((t >= s) & same).astype(F32), ((s >= t) & same).astype(F32)
    pre_s, suf_s = (t >= s).astype(F32), (s >= t).astype(F32)
    cum_f, cum_b = _edot(aa, pre_s, "ex"), _edot(aa, suf_s, "ex")
    h = lax.broadcasted_iota(jnp.int32, (SSM_G * SSM_HPG, 128), 0)
    l = lax.broadcasted_iota(jnp.int32, (SSM_G * SSM_HPG, 128), 1)
    cum_ft = _edot(cum_f, (l == h + 32).astype(F32), "ext")
    cum_bt = _edot(cum_b, (l == h + 64).astype(F32), "ext")
    return (_edot(lg_f, pre_g, "ex"), _edot(lg_b, suf_g, "ex"), xc * _bdot(dtp, e_f), xc * _bdot(dtp, e_b),
            _edot(cum_f, e_f, "xe"), _edot(cum_b, e_b, "xe"), cum_ft, cum_bt)


def _dot_impl_nn(a, b):
    return _dot_impl(a, b, False, False)


def _f_conv(nct, tc):
    def f(i, xs, ps, diff):
        (u,) = xs
        cw, cb = ps
        n = u.shape[0]
        t = lax.broadcasted_iota(jnp.int32, (n, 1), 0)
        per = jnp.where(i < nct, tc, GRID_W)
        pos = jnp.bitwise_and(t, per - 1)
        sh = _shift if diff else _shift_impl
        acc = cb + sh(u, pos, per, -2) * cw[0:1]
        for j in range(1, 4):
            acc = acc + sh(u, pos, per, j - 2) * cw[j:j + 1]
        return (_silu(acc),)
    return f


def _f_gla_post(i, xs, ps, diff):
    ogf, ogb, r = xs
    (gw,) = ps
    o = ogf + ogb
    parts = [_rms(o[:, h * GLA_DV:(h + 1) * GLA_DV], gw) for h in range(GLA_H)]
    return (jnp.concatenate(parts, axis=1) * _silu(r),)


def _f_ssd_post(i, xs, ps, diff):
    yf, yb, xc, z = xs
    dsk8, nw = ps
    dsk = _edot(dsk8, _head_expand(0), "xe")[0:1]
    y = (yf + yb + dsk * xc) * _silu(z)
    w = SSM_INNER // SSM_G
    parts = [_rms(y[:, g * w:(g + 1) * w], nw[:, g * w:(g + 1) * w]) for g in range(SSM_G)]
    return (jnp.concatenate(parts, axis=1),)


def _f_merge(i, xs, ps, diff):
    ga, gb, ya, yb = xs
    return (jax.nn.sigmoid(ga) * ya + jax.nn.sigmoid(gb) * yb,)


def _f_res1(nct):
    def f(i, xs, ps, diff):
        ctx, x, mix = xs
        ada, nw = ps
        h2 = jnp.where(i < nct, ctx, x) + _mod_rows(i, nct, ada, 2 * D) * mix
        return (h2, _rms(h2, nw) * (1.0 + _mod_rows(i, nct, ada, 4 * D)) + _mod_rows(i, nct, ada, 3 * D))
    return f


def _f_swiglu(i, xs, ps, diff):
    (gu,) = xs
    return (_silu(gu[:, :D_FF]) * gu[:, D_FF:],)


def _loss_head(h2, dn, tgt, ada, fw, nct):
    A = h2.shape[0]
    n_tiles = A // TM

    def tile_loss(i, h2t, dnt, tg, ada_, fw_):
        h3 = h2t + ada_[0:1, 5 * D:6 * D] * dnt
        err = _rms(h3, fw_) - tg
        row = 0.5 * jnp.mean(err * err, axis=-1, keepdims=True)
        return jnp.sum(row, axis=0, keepdims=True) * jnp.where(i < nct, 0.0, 1.0)

    def body(h2_ref, dn_ref, tg_ref, ada_ref, fw_ref, loss_ref, dh_ref, ddn_ref, dada_ref, dfw_ref):
        i = pl.program_id(0)
        val, vjp = jax.vjp(functools.partial(tile_loss, i), h2_ref[...], dn_ref[...].astype(F32), tg_ref[...], ada_ref[...],
                           fw_ref[...])
        dh, ddn, _, dada, dfw = vjp(jnp.ones((1, 1), F32))
        dh_ref[...] = dh
        ddn_ref[...] = ddn.astype(BF16)
        lv = jnp.broadcast_to(val, loss_ref.shape)
        for r, v in ((loss_ref, lv), (dada_ref, dada), (dfw_ref, dfw)):
            @pl.when(i == 0)
            def _(r=r, v=v):
                r[...] = v

            @pl.when(i > 0)
            def _(r=r, v=v):
                r[...] += v

    row = lambda i: (i, 0)
    return pl.pallas_call(
        body, grid=(n_tiles,),
        in_specs=[pl.BlockSpec((TM, D), row), pl.BlockSpec((TM, D), row),
                  pl.BlockSpec((TM, D), lambda i: (jnp.maximum(i - nct, 0), 0)), _whole(ada), _whole(fw)],
        out_specs=[pl.BlockSpec((8, 128), lambda i: (0, 0)), pl.BlockSpec((TM, D), row), pl.BlockSpec((TM, D), row),
                   _whole(ada), _whole(fw)],
        out_shape=[jax.ShapeDtypeStruct((8, 128), F32), jax.ShapeDtypeStruct((A, D), F32),
                   jax.ShapeDtypeStruct((A, D), BF16), jax.ShapeDtypeStruct(ada.shape, F32),
                   jax.ShapeDtypeStruct(fw.shape, F32)],
        compiler_params=_cparams(("arbitrary",)), name="loss_head",
    )(h2, dn, tgt, ada, fw)


def _chunk_of(step, n_chunks, n_ctx_chunks, rev):
    if not rev:
        return step
    return jnp.where(step < n_ctx_chunks, n_ctx_chunks - 1 - step, n_chunks - 1 - (step - n_ctx_chunks))


def _gla_step(st, q, k, v, b, rev):
    tri = _tri(GLA_C, rev).astype(F32)
    last = 0 if rev else GLA_C - 1
    outs, sts = [], []
    for h in range(GLA_H):
        kk = slice(h * GLA_DK, (h + 1) * GLA_DK)
        vv = slice(h * GLA_DV, (h + 1) * GLA_DV)
        qh, kh, vh, bh, sh = q[:, kk] * (GLA_DK ** -0.5), k[:, kk], v[:, vv], b[:, kk], st[vv]
        tot = bh[last:last + 1]
        mid = bh[GLA_C // 2:GLA_C // 2 + 1]
        att = _bdot(qh * jnp.exp(bh - mid), kh * jnp.exp(mid - bh), False, True) * tri
        outs.append(_bdot(att, vh) + _bdot(qh * jnp.exp(bh), sh, False, True))
        sts.append(sh * jnp.exp(tot) + _bdot(vh, kh * jnp.exp(tot - bh), True, False))
    return jnp.concatenate(sts, axis=0), jnp.concatenate(outs, axis=1)


_GQK, _GV = GLA_H * GLA_DK, GLA_H * GLA_DV


GLA_CPS = 4
_GB = GLA_C * GLA_CPS


def _gla_order(rev):
    return list(reversed(range(GLA_CPS))) if rev else list(range(GLA_CPS))


def _gla_fwd(proj, bg_f, bg_b, ncc, name):
    A = proj.shape[0]
    nb = A // _GB
    chs = [lambda s, rev=rev: _chunk_of(s, nb, ncc // GLA_CPS, rev) for rev in (False, True)]

    def body(*refs):
        ins, outs, sts = refs[:8], refs[8:12], refs[12:]

        for d, rev in enumerate((False, True)):
            q_ref, k_ref, v_ref, b_ref = ins[4 * d:4 * d + 4]
            o_ref, ss_ref = outs[2 * d:2 * d + 2]
            st_ref = sts[d]

            @pl.when(pl.program_id(0) == 0)
            def _(st_ref=st_ref):
                st_ref[...] = jnp.zeros_like(st_ref)

            st = st_ref[...]
            for pos, sub in enumerate(_gla_order(rev)):
                rows = slice(sub * GLA_C, (sub + 1) * GLA_C)
                ss_ref[0, pos] = st
                st, o = _gla_step(st, q_ref[rows].astype(F32), k_ref[rows].astype(F32), v_ref[rows].astype(F32),
                                  b_ref[rows], rev)
                o_ref[rows] = o.astype(o_ref.dtype)
            st_ref[...] = st

    in_specs, out_specs = [], []
    for ch in chs:
        in_specs += [pl.BlockSpec((_GB, _GQK), lambda s, ch=ch: (ch(s), C_Q // _GQK)),
                     pl.BlockSpec((_GB, _GQK), lambda s, ch=ch: (ch(s), C_K // _GQK)),
                     pl.BlockSpec((_GB, _GV), lambda s, ch=ch: (ch(s), C_V // _GV)),
                     pl.BlockSpec((_GB, _GQK), lambda s, ch=ch: (ch(s), 0))]
        out_specs += [pl.BlockSpec((_GB, _GV), lambda s, ch=ch: (ch(s), 0)),
                      pl.BlockSpec((1, GLA_CPS, _GV, GLA_DK), lambda s: (s, 0, 0, 0))]
    return pl.pallas_call(
        body, grid=(nb,), in_specs=in_specs, out_specs=out_specs,
        out_shape=[jax.ShapeDtypeStruct((A, _GV), BF16), jax.ShapeDtypeStruct((nb, GLA_CPS, _GV, GLA_DK), F32)] * 2,
        scratch_shapes=[pltpu.VMEM((_GV, GLA_DK), F32)] * 2,
        compiler_params=_cparams(("arbitrary",)), name=name,
    )(proj, proj, proj, bg_f, proj, proj, proj, bg_b)


def _gla_bwd(proj, lg, saved, d_o, prev, ncc, rev, name, pack=None):
    A = proj.shape[0]
    nb = A // _GB
    st_of = lambda r: nb - 1 - r
    ch = lambda r: _chunk_of(st_of(r), nb, ncc // GLA_CPS, rev)
    qs = pl.BlockSpec((_GB, _GQK), lambda r: (ch(r), 0))
    vs = pl.BlockSpec((_GB, _GV), lambda r: (ch(r), 0))
    n_prev = 0 if prev is None else 3

    n_buf = 0 if pack is None else 1

    def body(*refs):
        q_ref, k_ref, v_ref, b_ref, ss_ref, do_ref = refs[:6]
        p_refs = refs[6:6 + n_prev]
        out_refs = refs[6 + n_prev + n_buf:]
        ds_ref = out_refs[-1]

        @pl.when(pl.program_id(0) == 0)
        def _():
            ds_ref[...] = jnp.zeros_like(ds_ref)

        ds = ds_ref[...]
        for pos, sub in reversed(list(enumerate(_gla_order(rev)))):
            rows = slice(sub * GLA_C, (sub + 1) * GLA_C)
            _, vjp = jax.vjp(functools.partial(_gla_step, rev=rev), ss_ref[0, pos], q_ref[rows].astype(F32),
                             k_ref[rows].astype(F32), v_ref[rows].astype(F32), b_ref[rows])
            ds, dq, dk, dv, db = vjp((ds, do_ref[rows].astype(F32)))
            if n_prev:
                dq, dk, dv = [d + p[rows].astype(F32) for d, p in zip((dq, dk, dv), p_refs)]
            if pack is None:
                for r, d in zip(out_refs[:4], (dq, dk, dv, db)):
                    r[rows] = d.astype(r.dtype)
            else:
                out_refs[0][rows] = jnp.concatenate([dq, dk, dv], axis=1).astype(out_refs[0].dtype)
                out_refs[1][rows] = db
        ds_ref[...] = ds

    in_specs = [
        pl.BlockSpec((_GB, _GQK), lambda r: (ch(r), C_Q // _GQK)),
        pl.BlockSpec((_GB, _GQK), lambda r: (ch(r), C_K // _GQK)),
        pl.BlockSpec((_GB, _GV), lambda r: (ch(r), C_V // _GV)),
        qs,
        pl.BlockSpec((1, GLA_CPS, _GV, GLA_DK), lambda r: (st_of(r), 0, 0, 0)),
        vs,
    ] + ([qs, qs, vs] if n_prev else [])
    args = [proj, proj, proj, lg, saved, d_o, *(prev or ())]
    if pack is None:
        out_specs = [qs, qs, vs, qs]
        out_shape = [jax.ShapeDtypeStruct((A, _GQK), BF16), jax.ShapeDtypeStruct((A, _GQK), BF16),
                     jax.ShapeDtypeStruct((A, _GV), BF16), jax.ShapeDtypeStruct((A, _GQK), F32)]
        aliases = None
    else:
        in_specs.append(pl.BlockSpec(memory_space=pl.ANY))
        args.append(pack[0])
        out_specs = [pl.BlockSpec((_GB, 2 * _GQK + _GV), lambda r: (ch(r), 0)), qs]
        out_shape = [jax.ShapeDtypeStruct((A, pack[1]), BF16), jax.ShapeDtypeStruct((A, _GQK), F32)]
        aliases = {len(args) - 1: 0}
    return _call(body, grid=(nb,), in_specs=in_specs, out_specs=out_specs, out_shape=out_shape,
                 scratch=[pltpu.VMEM((_GV, GLA_DK), F32)], sem=("arbitrary",), name=name, args=args, aliases=aliases)


def _ssd_step(st, x, bm, cm, cum, cum_t, rev):
    mask = _tri(SSM_C, rev)
    last = 0 if rev else SSM_C - 1
    tot = cum[last:last + 1]
    cb = _bdot(cm, bm, False, True)
    ys = []
    for e in range(SSM_HPG):
        lm = jnp.exp(jnp.where(mask, cum[:, e * SSM_P:e * SSM_P + 1] - cum_t[e:e + 1], -jnp.inf))
        ys.append(_bdot(cb * lm, x[:, e * SSM_P:(e + 1) * SSM_P]))
    y = jnp.concatenate(ys, axis=1) + _bdot(cm, st) * jnp.exp(cum)
    st_new = st * jnp.exp(tot) + _bdot(bm, x * jnp.exp(tot - cum), True, False)
    return st_new, y


SSM_GPS = 4
_W = SSM_HPG * SSM_P
_XBC_B = SSM_INNER // (SSM_N * SSM_GPS)
_XBC_C = _XBC_B + SSM_G // SSM_GPS


def _ssd_steps(st, x, bm, cm, cum, cum_t, rev):
    outs = [_ssd_step(st[j * SSM_N:(j + 1) * SSM_N], x[:, j * _W:(j + 1) * _W], bm[:, j * SSM_N:(j + 1) * SSM_N],
                      cm[:, j * SSM_N:(j + 1) * SSM_N], cum[:, j * _W:(j + 1) * _W],
                      cum_t[j * SSM_HPG:(j + 1) * SSM_HPG], rev) for j in range(SSM_GPS)]
    return jnp.concatenate([o[0] for o in outs], axis=0), jnp.concatenate([o[1] for o in outs], axis=1)


SSM_CPS = 2
_SB = SSM_C * SSM_CPS


def _ssd_order(rev):
    return list(reversed(range(SSM_CPS))) if rev else list(range(SSM_CPS))


def _ssd_fwd(x_f, x_b, xbc, cum_f, cum_b, cum_tf, cum_tb, ncc, name, hosted=()):
    A = x_f.shape[0]
    nb = A // _SB
    ng = SSM_G // SSM_GPS
    chs = [lambda s, rev=rev: _chunk_of(s, nb, ncc // SSM_CPS, rev) for rev in (False, True)]

    def body(*refs):
        ins, outs, sts = refs[:10], refs[10:14], refs[14:]
        for d, rev in enumerate((False, True)):
            x_ref, b_ref, c_ref, a_ref, at_ref = ins[5 * d:5 * d + 5]
            y_ref, ss_ref = outs[2 * d:2 * d + 2]
            st_ref = sts[d]

            @pl.when(pl.program_id(1) == 0)
            def _(st_ref=st_ref):
                st_ref[...] = jnp.zeros_like(st_ref)

            st = st_ref[...]
            for pos, sub in enumerate(_ssd_order(rev)):
                rows = slice(sub * SSM_C, (sub + 1) * SSM_C)
                ss_ref[0, 0, pos] = st
                st, y = _ssd_steps(st, x_ref[rows].astype(F32), b_ref[rows].astype(F32), c_ref[rows].astype(F32),
                                   a_ref[rows], at_ref[:, rows], rev)
                y_ref[rows] = y.astype(y_ref.dtype)
            st_ref[...] = st

    in_specs, out_specs = [], []
    for ch in chs:
        gs = pl.BlockSpec((_SB, _W * SSM_GPS), lambda g, s, ch=ch: (ch(s), g))
        in_specs += [gs, pl.BlockSpec((_SB, SSM_N * SSM_GPS), lambda g, s, ch=ch: (ch(s), _XBC_B + g)),
                     pl.BlockSpec((_SB, SSM_N * SSM_GPS), lambda g, s, ch=ch: (ch(s), _XBC_C + g)), gs,
                     pl.BlockSpec((SSM_HPG * SSM_GPS, _SB), lambda g, s, ch=ch: (g, ch(s)))]
        out_specs += [gs, pl.BlockSpec((1, 1, SSM_CPS, SSM_N * SSM_GPS, _W), lambda g, s: (g, s, 0, 0, 0))]
    return _call(
        body, grid=(ng, nb), in_specs=in_specs, out_specs=out_specs,
        out_shape=[jax.ShapeDtypeStruct((A, SSM_INNER), BF16),
                   jax.ShapeDtypeStruct((ng, nb, SSM_CPS, SSM_N * SSM_GPS, _W), F32)] * 2,
        scratch=[pltpu.VMEM((SSM_N * SSM_GPS, _W), F32)] * 2,
        sem=("parallel", "arbitrary"), name=name,
        args=(x_f, xbc, xbc, cum_f, cum_tf, x_b, xbc, xbc, cum_b, cum_tb), hosted=hosted)


def _ssd_bwd(xd, xbc, cum, cum_t, saved, d_y, prev, ncc, rev, name):
    A = xd.shape[0]
    nc = A // SSM_C
    ng = SSM_G // SSM_GPS
    st_of = lambda r: nc - 1 - r
    ch = lambda r: _chunk_of(st_of(r), nc, ncc, rev)
    gs = pl.BlockSpec((SSM_C, _W * SSM_GPS), lambda g, r: (ch(r), g))
    ns = pl.BlockSpec((SSM_C, SSM_N * SSM_GPS), lambda g, r: (ch(r), g))
    ts = pl.BlockSpec((SSM_HPG * SSM_GPS, SSM_C), lambda g, r: (g, ch(r)))
    n_prev = 0 if prev is None else 2

    def body(*refs):
        x_ref, b_ref, c_ref, a_ref, at_ref, ss_ref, dy_ref = refs[:7]
        p_refs = refs[7:7 + n_prev]
        dx_ref, db_ref, dc_ref, da_ref, dat_ref, ds_ref = refs[7 + n_prev:]

        @pl.when(pl.program_id(1) == 0)
        def _():
            ds_ref[...] = jnp.zeros_like(ds_ref)

        _, vjp = jax.vjp(functools.partial(_ssd_steps, rev=rev), ss_ref[0, 0, 0], x_ref[...].astype(F32),
                         b_ref[...].astype(F32), c_ref[...].astype(F32), a_ref[...], at_ref[...])
        ds, dx, db, dc, da, dat = vjp((ds_ref[...], dy_ref[...].astype(F32)))
        if n_prev:
            db, dc = db + p_refs[0][...].astype(F32), dc + p_refs[1][...].astype(F32)
        for r, d in zip((dx_ref, db_ref, dc_ref, da_ref, dat_ref), (dx, db, dc, da, dat)):
            r[...] = d.astype(r.dtype)
        ds_ref[...] = ds

    in_specs = [gs, pl.BlockSpec((SSM_C, SSM_N * SSM_GPS), lambda g, r: (ch(r), _XBC_B + g)),
                pl.BlockSpec((SSM_C, SSM_N * SSM_GPS), lambda g, r: (ch(r), _XBC_C + g)), gs, ts,
                pl.BlockSpec((1, 1, 1, SSM_N * SSM_GPS, _W),
                             lambda g, r: (g, st_of(r) // SSM_CPS, st_of(r) % SSM_CPS, 0, 0)), gs]
    in_specs += [ns, ns] if n_prev else []
    return pl.pallas_call(
        body, grid=(ng, nc), in_specs=in_specs, out_specs=[gs, ns, ns, gs, ts],
        out_shape=[jax.ShapeDtypeStruct((A, SSM_INNER), BF16), jax.ShapeDtypeStruct((A, SSM_G * SSM_N), BF16),
                   jax.ShapeDtypeStruct((A, SSM_G * SSM_N), BF16), jax.ShapeDtypeStruct((A, SSM_INNER), F32),
                   jax.ShapeDtypeStruct((SSM_G * SSM_HPG, A), F32)],
        scratch_shapes=[pltpu.VMEM((SSM_N * SSM_GPS, _W), F32)],
        compiler_params=_cparams(("parallel", "arbitrary")), name=name,
    )(xd, xbc, xbc, cum, cum_t, saved, d_y, *(prev or ()))


_ADA_ROWS = 16


def _ada_fwd(cc, w_shard, b_shard):
    n = w_shard.shape[1]

    def body(cc_ref, w_ref, b_ref, o_ref):
        o_ref[...] = _dot_impl(_silu(cc_ref[...]), w_ref[...], False, False) + b_ref[...]

    return pl.pallas_call(body, out_shape=jax.ShapeDtypeStruct((_ADA_ROWS, n), F32),
                          compiler_params=_cparams(()), name="ada_fwd")(cc, w_shard, b_shard)


def _ada_bwd(cc, w_shard, d_all, d_mine):
    n = w_shard.shape[1]

    def rows(ref, r):
        parts = []
        for s in range(N_DEV):
            parts.append(ref[6 * s + r:6 * s + r + 1] + ref[6 * s + 2 + r:6 * s + 3 + r] + ref[6 * s + 4 + r:6 * s + 5 + r])
        return parts

    def total(parts):
        t = parts[0]
        for p in parts[1:]:
            t = t + p
        return t

    def body(cc_ref, w_ref, da_ref, dm_ref, dw_ref, db_ref, dcc_ref):
        dd = jnp.concatenate(rows(dm_ref, 0) + [total(rows(dm_ref, 1)), jnp.zeros((_ADA_ROWS - N_DEV - 1, n), F32)], axis=0)
        cc = cc_ref[...]
        dw_ref[...] = _dot_impl(_silu(cc), dd, True, False)
        db_ref[...] = total(rows(da_ref, 0)) + total(rows(da_ref, 1))
        _, vjp = jax.vjp(_silu, cc)
        dcc_ref[...] = vjp(_dot_impl(dd, w_ref[...], False, True))[0]

    return pl.pallas_call(
        body, out_shape=[jax.ShapeDtypeStruct((D, n), F32), jax.ShapeDtypeStruct((1, d_all.shape[1]), F32),
                         jax.ShapeDtypeStruct((_ADA_ROWS, D), F32)],
        compiler_params=_cparams(()), name="ada_bwd")(cc, w_shard, d_all, d_mine)


def _local_step(x, ctx, target, ada, w, shards):
    T, Tc = x.shape[0], ctx.shape[0]
    assert Tc == TM and T % TM == 0 and GRID_W == GLA_C and TS == SSM_C == 2 * GLA_C
    A = T + Tc
    nct = Tc // TM
    n_tm, n_ts = A // TM, A // TS
    tl = _pick(A, 3 * TM, TM)
    n_tl = A // tl
    g = {}

    x_in = [_ctx(ctx, TM, nct), _lat(x, TM, nct)]
    f_norm1 = _f_norm1(nct)
    p_norm1 = [w["norm1_w"], ada]
    (h1,) = _stage_fwd(f_norm1, n_tm, x_in, p_norm1, [(A, D, BF16, TM)], "norm1")
    proj, ag_pa, ag_pb, ag_out = _mm(h1, w["wt_in"], trans_b=True, out_dtype=BF16, name="mm_in", tn_t=1152,
                                     hosted=[(shards[n], False) for n in ("w_pa", "w_pb", "w_out")])
    small = _mm(h1, w["wt_in"][C_S:], trans_b=True, out_dtype=F32, name="mm_in_small")

    p_gd = [w["up_f"], w["up_b"], w["gla_bias_f"], w["gla_bias_b"], w["dtb"], w["alog"]]
    f_conv = _f_conv(nct, Tc)
    p_conv = [w["conv_w"], w["conv_b"]]
    in_conv = [_col(proj, TM, 3072, C_U // 3072)]
    (xbc,) = _stage_fwd(f_conv, n_tm, in_conv, p_conv, [(A, 3072, BF16, TM)], "conv")
    in_gd = [_col(small, TS), _col(xbc, TS, SSM_INNER, 0)]
    n_heads = SSM_G * SSM_HPG
    lg_f, lg_b, xf, xb, axf, axb, atf, atb = _stage_fwd(
        _f_gates_dt, n_ts, in_gd, p_gd,
        [(A, 512, F32, TS)] * 2 + [(A, SSM_INNER, BF16, TS)] * 2 + [(A, SSM_INNER, F32, TS)] * 2
        + [(A, n_heads, F32, TS, True)] * 2,
        "gates_dt")

    ncc_g, ncc_s = Tc // GLA_C, Tc // SSM_C
    ogf, sv_gf, ogb, sv_gb = _gla_fwd(proj, lg_f, lg_b, ncc_g, "gla_fb")
    ysf, sv_sf, ysb, sv_sb, ag_gate, ag_up, ag_down = _ssd_fwd(
        xf, xb, xbc, axf, axb, atf, atb, ncc_s, "ssd_fb", hosted=[(shards[n], False) for n in ("w_gate", "w_up", "w_down")])
    w = dict(w, w_pa=ag_pa.reshape(D, D), w_pb=ag_pb.reshape(SSM_INNER, D), w_out=ag_out.reshape(D, D),
             wt_gu=jnp.concatenate([ag_gate.reshape(D_FF, D), ag_up.reshape(D_FF, D)], axis=0), w_down=ag_down.reshape(D_FF, D))
    got = {}

    in_gp = [_col(ogf, tl), _col(ogb, tl), _col(proj, tl, 1024, C_R // 1024)]
    p_gp = [w["gla_norm_w"]]
    (oa,) = _stage_fwd(_f_gla_post, n_tl, in_gp, p_gp, [(A, D, BF16, tl)], "gla_post")
    in_sp = [_col(ysf, TM), _col(ysb, TM), _col(xbc, TM, SSM_INNER, 0), _col(proj, TM, SSM_INNER, C_Z // SSM_INNER)]
    p_sp = [w["dsk8"], w["ssm_norm_w"]]
    (ob,) = _stage_fwd(_f_ssd_post, n_tm, in_sp, p_sp, [(A, SSM_INNER, BF16, TM)], "ssd_post")
    ya = _mm(oa, w["w_pa"], trans_b=False, out_dtype=BF16, name="mm_pa")
    yb = _mm(ob, w["w_pb"], trans_b=False, out_dtype=BF16, name="mm_pb")
    in_mg = [_col(proj, tl, 1024, C_GA // 1024), _col(proj, tl, 1024, C_GB // 1024), _col(ya, tl), _col(yb, tl)]
    (merged,) = _stage_fwd(_f_merge, n_tl, in_mg, [], [(A, D, BF16, tl)], "merge")
    mix = _mm(merged, w["w_out"], trans_b=False, out_dtype=BF16, name="mm_out")

    f_res1 = _f_res1(nct)
    in_r1 = x_in + [_col(mix, TM)]
    p_r1 = [ada, w["norm2_w"]]
    h2, hm2 = _stage_fwd(f_res1, n_tm, in_r1, p_r1, [(A, D, F32, TM), (A, D, BF16, TM)], "res1")
    gu = _mm(hm2, w["wt_gu"], trans_b=True, out_dtype=BF16, name="mm_gu", tn_t=1408)
    in_sw = [_col(gu, TM)]
    (act,) = _stage_fwd(_f_swiglu, n_tm, in_sw, [], [(A, D_FF, BF16, TM)], "swiglu")
    dn = _mm(act, w["w_down"], trans_b=False, out_dtype=BF16, name="mm_down")

    loss_blk, d_h2a, d_dn, d_ada3, g["final_norm_w"] = _loss_head(h2, dn, target, ada, w["final_norm_w"], nct)
    rows16 = lambda gw: (gw.reshape(N_DEV, gw.shape[0] // N_DEV, gw.shape[1]).astype(BF16), True)
    gw_down = _mm_tn(act, d_dn, name="dw_down", tr_t=1408)
    d_act = _mm(d_dn, w["w_down"], trans_b=True, out_dtype=BF16, name="dx_down", tn_t=1408)
    d_gu, got["w_down"] = _stage_bwd(_f_swiglu, n_tm, in_sw, [], [_col(d_act, TM)], lambda i, t: t, [(BF16, None)],
                                     "swiglu_b", hosted=[rows16(gw_down)])
    gwt_gu = _mm_tn(d_gu, hm2, name="dw_gu", tr_t=1408)
    d_hm2 = _mm(d_gu, w["wt_gu"], trans_b=False, out_dtype=BF16, name="dx_gu")
    d_x1, d_mix, d_ada2, g["norm2_w"], got["w_gate"] = _stage_bwd(
        f_res1, n_tm, in_r1, p_r1, [_col(d_h2a, TM), _col(d_hm2, TM)], lambda i, t: t,
        [None, (F32, None), (BF16, None)], "res1_b", hosted=[rows16(gwt_gu[:D_FF])])
    gw_out = _mm_tn(merged, d_mix, name="dw_out")
    d_merged = _mm(d_mix, w["w_out"], trans_b=True, out_dtype=BF16, name="dx_out")
    d_ya, d_yb, d_proj, got["w_up"] = _stage_bwd(
        _f_merge, n_tl, in_mg, [], [_col(d_merged, tl)], lambda i, t: t, [None, None, (BF16, None), (BF16, None)], "merge_b",
        hosted=[rows16(gwt_gu[D_FF:])], pack=([0, 1], None, D_INP, C_GA // (2 * D)))
    gw_pa = _mm_tn(oa, d_ya, name="dw_pa")
    gw_pb = _mm_tn(ob, d_yb, name="dw_pb")
    d_oa = _mm(d_ya, w["w_pa"], trans_b=True, out_dtype=BF16, name="dx_pa")
    d_ob = _mm(d_yb, w["w_pb"], trans_b=True, out_dtype=BF16, name="dx_pb")
    d_og, g["gla_norm_w"], d_proj, got["w_out"] = _stage_bwd(
        _f_gla_post, n_tl, in_gp, p_gp, [_col(d_oa, tl)], lambda i, t: t, [(BF16, None), None, None], "gla_post_b",
        hosted=[rows16(gw_out)], pack=([2], d_proj, D_INP, C_R // D))
    d_ys, d_xs_skip, g["dsk8"], g["ssm_norm_w"], d_proj, got["w_pa"], got["w_pb"] = _stage_bwd(
        _f_ssd_post, n_tm, in_sp, p_sp, [_col(d_ob, TM)], lambda i, t: t,
        [(BF16, None), None, (BF16, None), None], "ssd_post_b", hosted=[rows16(gw_pa), rows16(gw_pb)],
        pack=([3], d_proj, D_INP, C_Z // SSM_INNER))

    dq, dk, dv, d_lgf = _gla_bwd(proj, lg_f, sv_gf, d_og, None, ncc_g, False, "gla_f_b")
    d_proj, d_lgb = _gla_bwd(proj, lg_b, sv_gb, d_og, (dq, dk, dv), ncc_g, True, "gla_b_b", pack=(d_proj, D_INP))
    d_xf, d_bm, d_cm, d_axf, d_atf = _ssd_bwd(xf, xbc, axf, atf, sv_sf, d_ys, None, ncc_s, False, "ssd_f_b")
    d_xb, d_bm, d_cm, d_axb, d_atb = _ssd_bwd(xb, xbc, axb, atb, sv_sb, d_ys, (d_bm, d_cm), ncc_s, True, "ssd_b_b")

    cts_gd = [_col(a, TS) for a in (d_lgf, d_lgb, d_xf, d_xb, d_axf, d_axb)]
    cts_gd += [(a, (n_heads, TS), lambda i: (0, i)) for a in (d_atf, d_atb)]
    d_xs_dt, g["up_f"], g["up_b"], g["gla_bias_f"], g["gla_bias_b"], g["dtb"], g["alog"], d_proj = _stage_bwd(
        _f_gates_dt, n_ts, in_gd, p_gd, cts_gd, lambda i, t: t, [None, (BF16, None)], "gates_dt_b",
        pack=([0], d_proj, D_INP, C_S // 128))
    cts_conv = [_col(d_xs_skip, TM), _col(d_xs_dt, TM), _col(d_bm, TM), _col(d_cm, TM)]
    g["conv_w"], g["conv_b"], d_proj = _stage_bwd(
        f_conv, n_tm, in_conv, p_conv, cts_conv,
        lambda i, t: [jnp.concatenate([t[0] + t[1], t[2], t[3]], axis=1)], [None], "conv_b",
        pack=([0], d_proj, D_INP, C_U // 3072))
    gwt_in = _mm_tn(d_proj, h1, name="dw_in", tr_t=1152)
    d_h1, got["w_in"] = _mm(d_proj, w["wt_in"], trans_b=False, out_dtype=BF16, name="dx_in", tm_t=768, tk_t=3456,
                            hosted=[rows16(_from_padded(gwt_in))])
    grad_x, g["norm1_w"], d_ada1 = _stage_bwd(
        f_norm1, n_tm, x_in, p_norm1, [_col(d_h1, TM)], lambda i, t: t,
        [None, (F32, (d_x1, (TM, D), x_in[1][2]))], "norm1_b")
    return loss_blk, grad_x, g, got, (d_ada1, d_ada2, d_ada3)


def _exchange(x, a2a, name):
    shp = x.shape[1:] if a2a else x.shape

    def body(x_ref, o_ref, send_sems, recv_sems, local_sem):
        ops = _exchange_ops(x_ref, o_ref, send_sems, recv_sems, local_sem, a2a)
        for op in ops:
            op.start()
        for op in ops:
            op.wait()

    return pl.pallas_call(
        body, out_shape=jax.ShapeDtypeStruct((N_DEV,) + tuple(shp), x.dtype),
        in_specs=[pl.BlockSpec(memory_space=pl.ANY)], out_specs=pl.BlockSpec(memory_space=pl.ANY),
        scratch_shapes=[pltpu.SemaphoreType.DMA((N_DEV - 1,)), pltpu.SemaphoreType.DMA((N_DEV - 1,)),
                        pltpu.SemaphoreType.DMA(())],
        name=name,
    )(x)


def _all_gather(x, name):
    def body(x_ref, o_ref, send_sems, recv_sems, local_sem):
        mx, my, mc = lax.axis_index("x"), lax.axis_index("y"), lax.axis_index("c")
        me, sibling = (mx, my, mc), (mx, my, 1 - mc)
        chips = [(1 - mx, my), (mx, 1 - my), (1 - mx, 1 - my)]

        def slot(px, py, pc):
            return o_ref.at[4 * px + 2 * py + pc]

        def copy(k, block, to, src=None):
            return pltpu.make_async_remote_copy(
                src_ref=slot(*block) if src is None else src, dst_ref=slot(*block),
                send_sem=send_sems.at[k], recv_sem=recv_sems.at[k], device_id=to, device_id_type=pl.DeviceIdType.MESH)

        own = pltpu.make_async_copy(x_ref, slot(*me), local_sem)
        own.start()
        first = [copy(0, me, sibling, src=x_ref)] + [copy(1 + j, me, (*chip, mc), src=x_ref) for j, chip in enumerate(chips)]
        for cp in first:
            cp.start()
        passed = [copy(4 + j, (*chip, mc), sibling) for j, chip in enumerate(chips)]
        for j, chip in enumerate(chips):
            copy(1 + j, (*chip, mc), me).wait_recv()
            passed[j].start()
        copy(0, sibling, me).wait_recv()
        for j, chip in enumerate(chips):
            copy(4 + j, (*chip, 1 - mc), me).wait_recv()
        for cp in first + passed:
            cp.wait_send()
        own.wait()

    return pl.pallas_call(
        body, out_shape=jax.ShapeDtypeStruct((N_DEV,) + tuple(x.shape), x.dtype),
        in_specs=[pl.BlockSpec(memory_space=pl.ANY)], out_specs=pl.BlockSpec(memory_space=pl.ANY),
        scratch_shapes=[pltpu.SemaphoreType.DMA((N_DEV - 1,)), pltpu.SemaphoreType.DMA((N_DEV - 1,)),
                        pltpu.SemaphoreType.DMA(())],
        name=name,
    )(x)


def _adamw_math(w, gr, m, v):
    m = ADAM_B1 * m + (1.0 - ADAM_B1) * gr
    v = ADAM_B2 * v + (1.0 - ADAM_B2) * (gr * gr)
    m_hat = m / np.float32(1.0 - ADAM_B1 ** ADAM_STEP)
    v_hat = v / np.float32(1.0 - ADAM_B2 ** ADAM_STEP)
    delta = -ADAM_LR * (m_hat / (jnp.sqrt(v_hat) + ADAM_EPS) + ADAM_WD * w)
    return delta, m, v


def _sum_adamw(parts, w, m, v, name):
    R, C = w.shape
    n_parts = parts.shape[0]
    tr = _pick(R, max(16, (2 << 20) // (4 * C) // 16 * 16), 16)

    def body(p_ref, w_ref, m_ref, v_ref, g_ref, d_ref, mo_ref, vo_ref):
        gr = p_ref[0].astype(F32)
        for k in range(1, n_parts):
            gr = gr + p_ref[k].astype(F32)
        g_ref[...] = gr
        d_ref[...], mo_ref[...], vo_ref[...] = _adamw_math(w_ref[...], gr, m_ref[...], v_ref[...])

    if R % 16 == 0:
        grid = (R // tr,)
        tile, p_spec = pl.BlockSpec((tr, C), lambda i: (i, 0)), pl.BlockSpec((n_parts, tr, C), lambda i: (0, i, 0))
    else:
        tc = _pick(C, max(128, (2 << 20) // (4 * R) // 128 * 128), 128)
        grid = (C // tc,)
        tile, p_spec = pl.BlockSpec((R, tc), lambda i: (0, i)), pl.BlockSpec((n_parts, R, tc), lambda i: (0, 0, i))
    return pl.pallas_call(
        body, grid=grid, in_specs=[p_spec, tile, tile, tile],
        out_specs=[tile] * 4, out_shape=[jax.ShapeDtypeStruct((R, C), F32)] * 4,
        compiler_params=_cparams(("parallel",)), name=name,
    )(parts, w, m, v)


def _sum8(parts, name):
    _, R, C = parts.shape

    def body(p_ref, g_ref):
        gr = p_ref[0]
        for k in range(1, N_DEV):
            gr = gr + p_ref[k]
        g_ref[...] = gr

    return pl.pallas_call(body, out_shape=jax.ShapeDtypeStruct((R, C), F32), name=name)(parts)


def _adamw(w, gr, m, v, name):
    def body(w_ref, g_ref, m_ref, v_ref, d_ref, mo_ref, vo_ref):
        d_ref[...], mo_ref[...], vo_ref[...] = _adamw_math(w_ref[...], g_ref[...], m_ref[...], v_ref[...])

    return pl.pallas_call(body, out_shape=[jax.ShapeDtypeStruct(w.shape, F32)] * 3, name=name)(w, gr, m, v)


def _to_padded(wt_in):
    z = jnp.zeros((32,) + wt_in.shape[1:], wt_in.dtype)
    return jnp.concatenate([wt_in[0:3072], wt_in[5152:8224], wt_in[3104:5152], wt_in[8288:10336],
                            wt_in[3072:3104], wt_in[8224:8288], z], axis=0)


def _from_padded(p):
    return jnp.concatenate([p[0:3072], p[10240:10272], p[6144:8192], p[3072:6144], p[10272:10336], p[8192:10240]], axis=0)


def _unshard_cols(gathered):
    n, r, c = gathered.shape
    return jnp.transpose(gathered, (1, 0, 2)).reshape(r, n * c)


def _lanes(vec, lo):
    return jnp.concatenate([jnp.zeros((1, lo), F32), vec, jnp.zeros((1, 128 - lo - vec.shape[1]), F32)], axis=1)


_SMALL = (("b_ada", 6 * D), ("c_ctx", D), ("norm1_w", D), ("gla_bias_f", 512), ("gla_bias_b", 512), ("gla_norm_w", 256),
          ("conv_b", 3072), ("dt_bias_f", 32), ("dt_bias_b", 32), ("a_log_f", 32), ("a_log_b", 32), ("d_skip", 32),
          ("ssm_norm_w", 2048), ("norm2_w", D), ("final_norm_w", D))
_SHARDED_SMALL = (("gla_up_f", 16 * 512), ("gla_up_b", 16 * 512), ("conv_w", 4 * 3072))


def _pack(vals, names):
    flat = jnp.concatenate([vals[n].reshape(-1).astype(F32) for n, _ in names])
    pad = (-flat.shape[0]) % 1024
    return jnp.concatenate([flat, jnp.zeros((pad,), F32)]).reshape(-1, 128)


def _unpack(packed, names, shapes):
    flat, out, off = packed.reshape(-1), {}, 0
    for n, size in names:
        out[n] = flat[off:off + size].reshape(shapes[n])
        off += size
    return out


def kernel(x, c, ctx, c_ctx, w_ada, b_ada, norm1_w, w_in, gla_up_f, gla_bias_f, gla_up_b, gla_bias_b, gla_norm_w, conv_w, conv_b, dt_bias_f, dt_bias_b, a_log_f, a_log_b, d_skip, ssm_norm_w, w_pa, w_pb, w_out, norm2_w, w_gate, w_up, w_down, final_norm_w, loss_target, m_c_ctx, m_w_ada, m_b_ada, m_norm1_w, m_w_in, m_gla_up_f, m_gla_bias_f, m_gla_up_b, m_gla_bias_b, m_gla_norm_w, m_conv_w, m_conv_b, m_dt_bias_f, m_dt_bias_b, m_a_log_f, m_a_log_b, m_d_skip, m_ssm_norm_w, m_w_pa, m_w_pb, m_w_out, m_norm2_w, m_w_gate, m_w_up, m_w_down, m_final_norm_w, v_c_ctx, v_w_ada, v_b_ada, v_norm1_w, v_w_in, v_gla_up_f, v_gla_bias_f, v_gla_up_b, v_gla_bias_b, v_gla_norm_w, v_conv_w, v_conv_b, v_dt_bias_f, v_dt_bias_b, v_a_log_f, v_a_log_b, v_d_skip, v_ssm_norm_w, v_w_pa, v_w_pb, v_w_out, v_norm2_w, v_w_gate, v_w_up, v_w_down, v_final_norm_w):
    args = dict(locals())
    me = 4 * lax.axis_index("x") + 2 * lax.axis_index("y") + lax.axis_index("c")

    t_names = ("w_in", "w_gate", "w_up")
    local = {n: (jnp.transpose(args[n][0]) if n in t_names else args[n][0]) for n in
             ("w_in", "w_pa", "w_pb", "w_out", "w_gate", "w_up", "w_down")}
    full = {"wt_in": _to_padded(_all_gather(local["w_in"].astype(BF16), "ag_w_in").reshape(D_IN, D))}
    shards = {n: local[n].astype(BF16) for n in ("w_pa", "w_pb", "w_out", "w_gate", "w_up", "w_down")}
    sm = _exchange(jnp.concatenate([gla_up_f.reshape(-1), gla_up_b.reshape(-1), conv_w.reshape(-1), c.reshape(-1)]).reshape(-1, 128),
                   False, "ag_small")
    sm = sm.reshape(N_DEV, -1)
    n_ada = w_ada.shape[2]
    cc = jnp.concatenate([sm[:, 3584:3584 + D], c_ctx[None], jnp.zeros((_ADA_ROWS - N_DEV - 1, D), F32)], axis=0)
    w_ada16 = w_ada[0].astype(BF16)
    ada_cols = _exchange(_ada_fwd(cc, w_ada16, lax.dynamic_slice(b_ada, (0, me * n_ada), (1, n_ada))), False, "ag_ada")
    ada = jnp.concatenate([lax.dynamic_index_in_dim(ada_cols, me, axis=1, keepdims=False).reshape(1, -1),
                           ada_cols[:, N_DEV].reshape(1, -1), jnp.zeros((6, N_DEV * n_ada), F32)], axis=0)
    up_f = _unshard_cols(sm[:, 0:1024].reshape(N_DEV, 16, 64))
    up_b = _unshard_cols(sm[:, 1024:2048].reshape(N_DEV, 16, 64))
    full["conv_w"] = _unshard_cols(sm[:, 2048:3584].reshape(N_DEV, 4, 384))
    full["up_f"] = jnp.concatenate([up_f, jnp.zeros((112, 512), F32)], axis=0)
    full["up_b"] = jnp.concatenate([jnp.zeros((16, 512), F32), up_b, jnp.zeros((96, 512), F32)], axis=0)
    full["dtb"] = jnp.concatenate([jnp.zeros((1, 32), F32), dt_bias_f, dt_bias_b, jnp.zeros((1, 32), F32)], axis=1)
    full["alog"] = jnp.concatenate([jnp.zeros((1, 32), F32), a_log_f, a_log_b, jnp.zeros((1, 32), F32)], axis=1)
    full["dsk8"] = jnp.concatenate([_lanes(d_skip, 0), jnp.zeros((7, 128), F32)], axis=0)
    for n in ("norm1_w", "gla_bias_f", "gla_bias_b", "gla_norm_w", "conv_b", "ssm_norm_w", "norm2_w"):
        full[n] = args[n]
    full["final_norm_w"] = final_norm_w.reshape(1, D)

    loss_blk, grad_x, g, got, d_ada = _local_step(x[0], ctx[0], loss_target[0], ada, full, shards)

    d_all = _exchange(jnp.concatenate([d[0:2] for d in d_ada], axis=0), False, "ag_d_ada").reshape(N_DEV * 6, N_DEV * n_ada)
    gw_ada, gb_ada, d_cc = _ada_bwd(cc, w_ada16, d_all, lax.dynamic_slice(d_all, (0, me * n_ada), (N_DEV * 6, n_ada)))

    gs = dict(g, c_ctx=d_cc[N_DEV], b_ada=jnp.zeros_like(b_ada))
    gs["dt_bias_f"], gs["dt_bias_b"] = g["dtb"][:, 32:64], g["dtb"][:, 64:96]
    gs["a_log_f"], gs["a_log_b"] = g["alog"][:, 32:64], g["alog"][:, 64:96]
    gs["d_skip"] = g["dsk8"][0:1, 0:32]
    gs["gla_up_f"], gs["gla_up_b"] = g["up_f"][0:16], g["up_b"][16:32]
    gs["loss"] = loss_blk[0:1, 0:1]
    names = _SMALL + _SHARDED_SMALL + (("loss", 1),)
    shapes = {n: (args[n].shape if n in args else (1, 1)) for n, _ in names}
    shapes.update({"gla_up_f": (16, 512), "gla_up_b": (16, 512), "conv_w": (4, 3072)})
    red = _unpack(_sum8(_exchange(_pack(gs, names), False, "ag_small_grads"), "sum_small_grads"), names, shapes)
    loss = red["loss"].reshape(())
    grads = {n: red[n] for n, _ in _SMALL}
    grads["b_ada"] = gb_ada
    grads["gla_up_f"] = lax.dynamic_slice(red["gla_up_f"], (0, me * 64), (16, 64))[None]
    grads["gla_up_b"] = lax.dynamic_slice(red["gla_up_b"], (0, me * 64), (16, 64))[None]
    grads["conv_w"] = lax.dynamic_slice(red["conv_w"], (0, me * 384), (4, 384))[None]
    upd_names = tuple((n, s) for n, s in _SMALL) + (("gla_up_f", 1024), ("gla_up_b", 1024), ("conv_w", 1536))
    pk = lambda prefix, src: _pack({n: src[prefix + n] for n, _ in upd_names}, upd_names)
    d_s, m_s, v_s = _adamw(pk("", args), _pack(grads, upd_names), pk("m_", args), pk("v_", args), "adamw_small")
    upd_shapes = {n: args[n].shape for n, _ in upd_names}
    delta = _unpack(d_s, upd_names, upd_shapes)
    new_m = _unpack(m_s, upd_names, upd_shapes)
    new_v = _unpack(v_s, upd_names, upd_shapes)

    got["w_ada"] = gw_ada[None]
    for n, parts in got.items():
        orient = jnp.transpose if n in t_names else (lambda a: a)
        res = _sum_adamw(parts, local[n] if n in local else args[n][0], orient(args["m_" + n][0]), orient(args["v_" + n][0]),
                         "adamw_" + n)
        grads[n], delta[n], new_m[n], new_v[n] = [orient(r)[None] for r in res]

    order = ["c_ctx", "w_ada", "b_ada", "norm1_w", "w_in", "gla_up_f", "gla_bias_f", "gla_up_b", "gla_bias_b", "gla_norm_w",
             "conv_w", "conv_b", "dt_bias_f", "dt_bias_b", "a_log_f", "a_log_b", "d_skip", "ssm_norm_w", "w_pa", "w_pb",
             "w_out", "norm2_w", "w_gate", "w_up", "w_down", "final_norm_w"]
    fix = lambda d: [d[n].reshape(args[n].shape) for n in order]
    return (loss, grad_x[None], *fix(grads), *fix(delta), *fix(new_m), *fix(new_v))
```

```python
import functools

import jax
import jax.numpy as jnp
import numpy as np
from jax import lax
from jax.experimental import pallas as pl
from jax.experimental.pallas import tpu as pltpu

F32 = jnp.float32
BF16 = jnp.bfloat16
N_DEV = 8
D = 1024
EPS = 1e-6
GRID_W = 64
GLA_H, GLA_DK, GLA_DV = 4, 128, 256
GLA_C = 64
GLA_TAU = 16.0
SSM_G, SSM_HPG, SSM_P, SSM_N = 4, 8, 64, 128
SSM_C = 128
SSM_INNER = 2048
D_FF = 2816
D_IN = 10336
D_INP = 10368
C_Q, C_K, C_V, C_R, C_U, C_Z, C_GA, C_GB, C_S = 0, 512, 1024, 2048, 3072, 6144, 8192, 9216, 10240
TM = 256
TS = 128

ADAM_LR, ADAM_B1, ADAM_B2, ADAM_EPS, ADAM_WD, ADAM_STEP = 0.001, 0.9, 0.999, 1e-08, 0.01, 10
VMEM_LIMIT = 56 << 20


def _cparams(sem):
    return pltpu.CompilerParams(dimension_semantics=sem or None, vmem_limit_bytes=VMEM_LIMIT)


def _pick(n, target, mult):
    best = None
    for t in range(mult, min(n, target) + 1, mult):
        if n % t == 0:
            best = t
    return best if best is not None else n


def _dot_impl(a, b, ta, tb):
    dims = (((0 if ta else 1,), (1 if tb else 0,)), ((), ()))
    return lax.dot_general(a.astype(BF16), b.astype(BF16), dims, preferred_element_type=F32)


@functools.partial(jax.custom_vjp, nondiff_argnums=(2, 3))
def _bdot(a, b, ta=False, tb=False):
    return _dot_impl(a, b, ta, tb)


def _bdot_fwd(a, b, ta, tb):
    return _dot_impl(a, b, ta, tb), (a, b)


def _bdot_bwd(ta, tb, res, g):
    a, b = res
    if not ta and not tb:
        return _dot_impl(g, b, False, True), _dot_impl(a, g, True, False)
    if not ta and tb:
        return _dot_impl(g, b, False, False), _dot_impl(g, a, True, False)
    if ta and not tb:
        return _dot_impl(b, g, False, True), _dot_impl(a, g, False, False)
    raise NotImplementedError


_bdot.defvjp(_bdot_fwd, _bdot_bwd)


_NN = (((1,), (0,)), ((), ()))
_NT = (((1,), (1,)), ((), ()))
_TN = (((0,), (0,)), ((), ()))


def _dg2(x, e, x_is_lhs, dims):
    hi = x.astype(BF16)
    lo = (x - hi.astype(F32)).astype(BF16)
    e = e.astype(BF16)
    if x_is_lhs:
        return (lax.dot_general(hi, e, dims, preferred_element_type=F32)
                + lax.dot_general(lo, e, dims, preferred_element_type=F32))
    return (lax.dot_general(e, hi, dims, preferred_element_type=F32)
            + lax.dot_general(e, lo, dims, preferred_element_type=F32))


_EDOT_FWD = {"xe": (True, _NN), "ex": (False, _NN), "ext": (False, _NT)}
_EDOT_BWD = {"xe": (True, _NT), "ex": (False, _TN), "ext": (True, _TN)}


@functools.partial(jax.custom_vjp, nondiff_argnums=(2,))
def _edot(x, e, mode):
    return _dg2(x, e, *_EDOT_FWD[mode])


def _edot_fwd(x, e, mode):
    return _dg2(x, e, *_EDOT_FWD[mode]), e


def _edot_bwd(mode, e, g):
    return _dg2(g, e, *_EDOT_BWD[mode]), None


_edot.defvjp(_edot_fwd, _edot_bwd)


def _shift_impl(u, pos, per, s):
    n = u.shape[0]
    rolled = u if s == 0 else pltpu.roll(u, (-s) % n, 0)
    ok = (pos + s >= 0) & (pos + s < per)
    return jnp.where(ok, rolled, 0.0)


def _rms(x, w):
    return x * lax.rsqrt(jnp.mean(x * x, axis=-1, keepdims=True) + EPS) * w


def _silu(x):
    return x * jax.nn.sigmoid(x)


def _softplus(x):
    return jnp.maximum(x, 0.0) + jnp.log(1.0 + jnp.exp(-jnp.abs(x)))


def _logsig(x):
    return jnp.minimum(x, 0.0) - jnp.log(1.0 + jnp.exp(-jnp.abs(x)))


def _tri(n, rev):
    t = lax.broadcasted_iota(jnp.int32, (n, n), 0)
    s = lax.broadcasted_iota(jnp.int32, (n, n), 1)
    return (s >= t) if rev else (t >= s)


def _head_expand(first_lane):
    l = lax.broadcasted_iota(jnp.int32, (128, SSM_INNER), 0)
    c = lax.broadcasted_iota(jnp.int32, (128, SSM_INNER), 1)
    return (l == first_lane + lax.shift_right_logical(c, 6)).astype(F32)


def _exchange_ops(x_ref, o_ref, send_sems, recv_sems, local_sem, a2a):
    mx, my, mc = lax.axis_index("x"), lax.axis_index("y"), lax.axis_index("c")
    me = 4 * mx + 2 * my + mc
    ops = []
    for k in range(1, N_DEV):
        px = 1 - mx if k & 4 else mx
        py = 1 - my if k & 2 else my
        pc = 1 - mc if k & 1 else mc
        ops.append(pltpu.make_async_remote_copy(
            src_ref=x_ref.at[4 * px + 2 * py + pc] if a2a else x_ref, dst_ref=o_ref.at[me],
            send_sem=send_sems.at[k - 1], recv_sem=recv_sems.at[k - 1],
            device_id=(px, py, pc), device_id_type=pl.DeviceIdType.MESH))
    ops.append(pltpu.make_async_copy(x_ref.at[me] if a2a else x_ref, o_ref.at[me], local_sem))
    return ops


def _call(body, *, grid, in_specs, out_specs, out_shape, scratch=(), sem, name, args, hosted=(), aliases=None):
    n_in, n_out, n_s, n_h = len(in_specs), len(out_specs), len(scratch), len(hosted)
    aliases = aliases or {}
    if not n_h:
        return pl.pallas_call(body, grid=grid, in_specs=list(in_specs), out_specs=list(out_specs),
                              out_shape=list(out_shape), scratch_shapes=list(scratch), input_output_aliases=aliases,
                              compiler_params=_cparams(sem), name=name)(*args)
    b0, c0 = n_in + n_h, n_in + n_h + n_out
    d0 = c0 + n_h

    def wrapped(*refs):
        hx, ho = refs[n_in:b0], refs[c0:d0]
        send_sems, recv_sems, local_sems = refs[d0 + n_s:]
        ids = [pl.program_id(ax) for ax in range(len(grid))]
        first = functools.reduce(jnp.logical_and, [i == 0 for i in ids])
        last = functools.reduce(jnp.logical_and, [i == n - 1 for i, n in zip(ids, grid)])

        def ops():
            return [op for n, (_, a2a) in enumerate(hosted)
                    for op in _exchange_ops(hx[n], ho[n], send_sems.at[n], recv_sems.at[n], local_sems.at[n], a2a)]

        @pl.when(first)
        def _():
            for op in ops():
                op.start()

        body(*refs[:n_in], *refs[b0:c0], *refs[d0:d0 + n_s])

        @pl.when(last)
        def _():
            for op in ops():
                op.wait()

    hbm = pl.BlockSpec(memory_space=pl.ANY)
    h_shapes = [jax.ShapeDtypeStruct((N_DEV,) + tuple(x.shape[1:] if a2a else x.shape), x.dtype) for x, a2a in hosted]
    return pl.pallas_call(
        wrapped, grid=grid, in_specs=list(in_specs) + [hbm] * n_h, out_specs=list(out_specs) + [hbm] * n_h,
        out_shape=list(out_shape) + h_shapes, input_output_aliases=aliases,
        scratch_shapes=list(scratch) + [pltpu.SemaphoreType.DMA((n_h, N_DEV - 1)), pltpu.SemaphoreType.DMA((n_h, N_DEV - 1)),
                                        pltpu.SemaphoreType.DMA((n_h,))],
        compiler_params=_cparams(("arbitrary",) * len(grid)), name=name,
    )(*args, *[x for x, _ in hosted])


def _mm(a, b, *, trans_b, out_dtype, name, tm_t=1408, tn_t=1024, tk_t=2816, hosted=()):
    M, K = a.shape
    N = b.shape[0] if trans_b else b.shape[1]
    tm, tn, tk = _pick(M, tm_t, 8), _pick(N, tn_t, 128), _pick(K, tk_t, 128)
    nk = K // tk
    dims = (((1,), (1,)), ((), ())) if trans_b else (((1,), (0,)), ((), ()))

    def body_one(a_ref, b_ref, o_ref):
        o_ref[...] = lax.dot_general(a_ref[...], b_ref[...], dims, preferred_element_type=F32).astype(out_dtype)

    def body_acc(a_ref, b_ref, o_ref, acc_ref):
        k = pl.program_id(2)
        p = lax.dot_general(a_ref[...], b_ref[...], dims, preferred_element_type=F32)

        @pl.when(k == 0)
        def _():
            acc_ref[...] = p

        @pl.when(k > 0)
        def _():
            acc_ref[...] += p

        @pl.when(k == nk - 1)
        def _():
            o_ref[...] = acc_ref[...].astype(out_dtype)

    b_spec = pl.BlockSpec((tn, tk), lambda j, i, k: (j, k)) if trans_b else pl.BlockSpec((tk, tn), lambda j, i, k: (k, j))
    outs = _call(
        body_one if nk == 1 else body_acc, grid=(N // tn, M // tm, nk),
        in_specs=[pl.BlockSpec((tm, tk), lambda j, i, k: (i, k)), b_spec],
        out_specs=[pl.BlockSpec((tm, tn), lambda j, i, k: (i, j))],
        out_shape=[jax.ShapeDtypeStruct((M, N), out_dtype)],
        scratch=[] if nk == 1 else [pltpu.VMEM((tm, tn), F32)],
        sem=("parallel", "parallel", "arbitrary"), name=name, args=(a, b), hosted=hosted)
    return outs if hosted else outs[0]


def _mm_tn(a, b, *, name, tm_t=2816, tr_t=1024, tn_t=1408):
    M, R = a.shape
    N = b.shape[1]
    tm, tr, tn = _pick(M, tm_t, 8), _pick(R, tr_t, 128), _pick(N, tn_t, 128)

    def body(a_ref, b_ref, o_ref):
        m = pl.program_id(2)
        p = lax.dot_general(a_ref[...], b_ref[...], (((0,), (0,)), ((), ())), preferred_element_type=F32)

        @pl.when(m == 0)
        def _():
            o_ref[...] = p

        @pl.when(m > 0)
        def _():
            o_ref[...] += p

    return pl.pallas_call(
        body, grid=(R // tr, N // tn, M // tm),
        in_specs=[pl.BlockSpec((tm, tr), lambda r, j, m: (m, r)), pl.BlockSpec((tm, tn), lambda r, j, m: (m, j))],
        out_specs=pl.BlockSpec((tr, tn), lambda r, j, m: (r, j)),
        out_shape=jax.ShapeDtypeStruct((R, N), F32),
        compiler_params=_cparams(("parallel", "parallel", "arbitrary")), name=name,
    )(a, b)


def _col(arr, tm, width=None, col=0):
    width = arr.shape[1] if width is None else width
    return (arr, (tm, width), lambda i: (i, col))


def _lat(arr, tm, nct):
    return (arr, (tm, arr.shape[1]), lambda i: (jnp.maximum(i - nct, 0), 0))


def _ctx(arr, tm, nct):
    return (arr, (tm, arr.shape[1]), lambda i: (jnp.minimum(i, nct - 1), 0))


def _whole(p):
    return pl.BlockSpec(p.shape, lambda i, nd=p.ndim: (0,) * nd)


def _stage_fwd(f, n_tiles, ins, params, outs, name):
    ni, npar = len(ins), len(params)
    o_specs = [pl.BlockSpec((o[1], o[3]), lambda i: (0, i)) if len(o) > 4 else pl.BlockSpec((o[3], o[1]), lambda i: (i, 0))
               for o in outs]
    o_shapes = [jax.ShapeDtypeStruct((o[1], o[0]) if len(o) > 4 else (o[0], o[1]), o[2]) for o in outs]

    def body(*refs):
        i = pl.program_id(0)
        xs = [r[...].astype(F32) for r in refs[:ni]]
        ps = [r[...] for r in refs[ni:ni + npar]]
        for r, v in zip(refs[ni + npar:], f(i, xs, ps, False)):
            r[...] = v.astype(r.dtype)

    return pl.pallas_call(
        body, grid=(n_tiles,),
        in_specs=[pl.BlockSpec(bs, fn) for _, bs, fn in ins] + [_whole(p) for p in params],
        out_specs=o_specs, out_shape=o_shapes,
        compiler_params=_cparams(("parallel",)), name=name,
    )(*[a for a, _, _ in ins], *params)


def _stage_bwd(f, n_tiles, ins, params, cts, ct_fn, dins, name, hosted=(), pack=None):
    ni, npar, nc = len(ins), len(params), len(cts)
    want = [k for k, d in enumerate(dins) if d is not None]
    extras = [dins[k][1] for k in want if dins[k][1] is not None]
    n_buf = 0 if pack is None or pack[1] is None else 1
    n_main_in = ni + npar + nc + len(extras)

    def body(*refs):
        i = pl.program_id(0)
        xs = [r[...].astype(F32) for r in refs[:ni]]
        ps = [r[...] for r in refs[ni:ni + npar]]
        ct_tiles = [r[...].astype(F32) for r in refs[ni + npar:ni + npar + nc]]
        ex_refs = list(refs[ni + npar + nc:n_main_in])
        out_refs = refs[n_main_in + n_buf:]
        _, vjp = jax.vjp(lambda xs_, ps_: tuple(f(i, xs_, ps_, True)), xs, ps)
        dxs, dps = vjp(tuple(ct_fn(i, ct_tiles)))
        for n, k in enumerate(want):
            v = dxs[k]
            if dins[k][1] is not None:
                v = v + ex_refs.pop(0)[...].astype(F32)
            out_refs[n][...] = v.astype(out_refs[n].dtype)
        if pack is not None:
            packed = [dxs[k] for k in pack[0]]
            r = out_refs[len(want) + npar]
            r[...] = (packed[0] if len(packed) == 1 else jnp.concatenate(packed, axis=1)).astype(r.dtype)
        for r, v in zip(out_refs[len(want):len(want) + npar], dps):
            @pl.when(i == 0)
            def _(r=r, v=v):
                r[...] = v

            @pl.when(i > 0)
            def _(r=r, v=v):
                r[...] += v

    in_specs = ([pl.BlockSpec(bs, fn) for _, bs, fn in ins] + [_whole(p) for p in params]
                + [pl.BlockSpec(bs, fn) for _, bs, fn in cts] + [pl.BlockSpec(bs, fn) for _, bs, fn in extras])
    out_specs = ([pl.BlockSpec(ins[k][1], lambda i, fn=ins[k][2]: (fn(i)[0], 0)) for k in want] + [_whole(p) for p in params])
    out_shape = ([jax.ShapeDtypeStruct((ins[k][0].shape[0], ins[k][1][1]), dins[k][0]) for k in want]
                 + [jax.ShapeDtypeStruct(p.shape, F32) for p in params])
    args = [*[a for a, _, _ in ins], *params, *[a for a, _, _ in cts], *[a for a, _, _ in extras]]
    aliases = None
    if pack is not None:
        ks, buf, total_cols, block = pack
        width = sum(ins[k][1][1] for k in ks)
        out_specs.append(pl.BlockSpec((ins[ks[0]][1][0], width), lambda i, fn=ins[ks[0]][2]: (fn(i)[0], block)))
        out_shape.append(jax.ShapeDtypeStruct((ins[ks[0]][0].shape[0], total_cols), BF16))
        if buf is not None:
            in_specs.append(pl.BlockSpec(memory_space=pl.ANY))
            args.append(buf)
            aliases = {len(args) - 1: len(out_specs) - 1}
    return _call(body, grid=(n_tiles,), in_specs=in_specs, out_specs=out_specs, out_shape=out_shape,
                 sem=("arbitrary",), name=name, args=args, hosted=hosted, aliases=aliases)


def _mod_rows(i, nct, ada, lo):
    return jnp.where(i < nct, ada[1:2, lo:lo + D], ada[0:1, lo:lo + D])


def _f_norm1(nct):
    def f(i, xs, ps, diff):
        ctx, x = xs
        nw, ada = ps
        xt = jnp.where(i < nct, ctx, x)
        return (_rms(xt, nw) * (1.0 + _mod_rows(i, nct, ada, D)) + _mod_rows(i, nct, ada, 0),)
    return f


def _f_gates_dt(i, xs, ps, diff):
    small, xc = xs
    up_f, up_b, gb_f, gb_b, dtb, alog = ps
    dot = _bdot if diff else _dot_impl_nn
    lg_f = _logsig(dot(small, up_f) + gb_f) * (1.0 / GLA_TAU)
    lg_b = _logsig(dot(small, up_b) + gb_b) * (1.0 / GLA_TAU)
    dtp = _softplus(small + dtb)
    aa = -jnp.exp(alog) * dtp
    e_f, e_b = _head_expand(32), _head_expand(64)
    n = small.shape[0]
    t = lax.broadcasted_iota(jnp.int32, (n, n), 0)
    s = lax.broadcasted_iota(jnp.int32, (n, n), 1)
    same = lax.shift_right_logical(t, 6) == lax.shift_right_logical(s, 6)
    pre_g, suf_g = ((t >= s) & same).astype(F32), ((s >= t) & same).astype(F32)
    pre_s, suf_s = (t >= s).astype(F32), (s >= t).astype(F32)
    cum_f, cum_b = _edot(aa, pre_s, "ex"), _edot(aa, suf_s, "ex")
    h = lax.broadcasted_iota(jnp.int32, (SSM_G * SSM_HPG, 128), 0)
    l = lax.broadcasted_iota(jnp.int32, (SSM_G * SSM_HPG, 128), 1)
    cum_ft = _edot(cum_f, (l == h + 32).astype(F32), "ext")
    cum_bt = _edot(cum_b, (l == h + 64).astype(F32), "ext")
    return (_edot(lg_f, pre_g, "ex"), _edot(lg_b, suf_g, "ex"), xc * _bdot(dtp, e_f), xc * _bdot(dtp, e_b),
            _edot(cum_f, e_f, "xe"), _edot(cum_b, e_b, "xe"), cum_ft, cum_bt)


def _dot_impl_nn(a, b):
    return _dot_impl(a, b, False, False)


def _f_conv(nct, tc):
    def f(i, xs, ps, diff):
        (u,) = xs
        cw, cb = ps
        n = u.shape[0]
        t = lax.broadcasted_iota(jnp.int32, (n, 1), 0)
        per = jnp.where(i < nct, tc, GRID_W)
        pos = jnp.bitwise_and(t, per - 1)
        acc = cb + _shift_impl(u, pos, per, -2) * cw[0:1]
        for j in range(1, 4):
            acc = acc + _shift_impl(u, pos, per, j - 2) * cw[j:j + 1]
        return (_silu(acc), acc)
    return f


def _conv_bwd(proj, pre, cts, cw, buf, nct, tc, name):
    A = proj.shape[0]
    width = cw.shape[1]

    def body(u_ref, p_ref, c0, c1, c2, c3, cw_ref, buf_ref, dcw_ref, dcb_ref, o_ref):
        i = pl.program_id(0)
        t = lax.broadcasted_iota(jnp.int32, (TM, 1), 0)
        per = jnp.where(i < nct, tc, GRID_W)
        pos = jnp.bitwise_and(t, per - 1)
        pre_ = p_ref[...].astype(F32)
        s = jax.nn.sigmoid(pre_)
        g = jnp.concatenate([c0[...].astype(F32) + c1[...].astype(F32), c2[...].astype(F32), c3[...].astype(F32)], axis=1)
        g = g * (s * (1.0 + pre_ * (1.0 - s)))
        u = u_ref[...].astype(F32)
        w_ = cw_ref[...]
        du, rows = None, []
        for j in range(4):
            gj = g if j == 2 else _shift_impl(g, pos, per, 2 - j)
            du = gj * w_[j:j + 1] if du is None else du + gj * w_[j:j + 1]
            rows.append(jnp.sum(gj * u, axis=0, keepdims=True))
        o_ref[...] = du.astype(o_ref.dtype)
        for r, v in ((dcw_ref, jnp.concatenate(rows, axis=0)), (dcb_ref, jnp.sum(g, axis=0, keepdims=True))):
            @pl.when(i == 0)
            def _(r=r, v=v):
                r[...] = v

            @pl.when(i > 0)
            def _(r=r, v=v):
                r[...] += v

    blk = C_U // width
    tile = lambda a: pl.BlockSpec((TM, a.shape[1]), lambda i: (i, 0))
    return _call(
        body, grid=(A // TM,),
        in_specs=[pl.BlockSpec((TM, width), lambda i: (i, blk)), tile(pre)] + [tile(c) for c in cts]
        + [_whole(cw), pl.BlockSpec(memory_space=pl.ANY)],
        out_specs=[_whole(cw), pl.BlockSpec((1, width), lambda i: (0, 0)), pl.BlockSpec((TM, width), lambda i: (i, blk))],
        out_shape=[jax.ShapeDtypeStruct(cw.shape, F32), jax.ShapeDtypeStruct((1, width), F32),
                   jax.ShapeDtypeStruct(buf.shape, buf.dtype)],
        sem=("arbitrary",), name=name, args=(proj, pre, *cts, cw, buf), aliases={7: 2})


def _f_gla_post(i, xs, ps, diff):
    ogf, ogb, r = xs
    (gw,) = ps
    o = ogf + ogb
    parts = [_rms(o[:, h * GLA_DV:(h + 1) * GLA_DV], gw) for h in range(GLA_H)]
    return (jnp.concatenate(parts, axis=1) * _silu(r),)


def _f_ssd_post(i, xs, ps, diff):
    yf, yb, xc, z = xs
    dsk8, nw = ps
    dsk = _edot(dsk8, _head_expand(0), "xe")[0:1]
    y = (yf + yb + dsk * xc) * _silu(z)
    w = SSM_INNER // SSM_G
    parts = [_rms(y[:, g * w:(g + 1) * w], nw[:, g * w:(g + 1) * w]) for g in range(SSM_G)]
    return (jnp.concatenate(parts, axis=1),)


def _f_merge(i, xs, ps, diff):
    ga, gb, ya, yb = xs
    return (jax.nn.sigmoid(ga) * ya + jax.nn.sigmoid(gb) * yb,)


def _f_res1(nct):
    def f(i, xs, ps, diff):
        ctx, x, mix = xs
        ada, nw = ps
        h2 = jnp.where(i < nct, ctx, x) + _mod_rows(i, nct, ada, 2 * D) * mix
        return (h2, _rms(h2, nw) * (1.0 + _mod_rows(i, nct, ada, 4 * D)) + _mod_rows(i, nct, ada, 3 * D))
    return f


def _f_swiglu(i, xs, ps, diff):
    (gu,) = xs
    return (_silu(gu[:, :D_FF]) * gu[:, D_FF:],)


def _loss_head(h2, dn, tgt, ada, fw, nct):
    A = h2.shape[0]
    n_tiles = A // TM

    def tile_loss(i, h2t, dnt, tg, ada_, fw_):
        h3 = h2t + ada_[0:1, 5 * D:6 * D] * dnt
        err = _rms(h3, fw_) - tg
        row = 0.5 * jnp.mean(err * err, axis=-1, keepdims=True)
        return jnp.sum(row, axis=0, keepdims=True) * jnp.where(i < nct, 0.0, 1.0)

    def body(h2_ref, dn_ref, tg_ref, ada_ref, fw_ref, loss_ref, dh_ref, ddn_ref, dada_ref, dfw_ref):
        i = pl.program_id(0)
        val, vjp = jax.vjp(functools.partial(tile_loss, i), h2_ref[...], dn_ref[...].astype(F32), tg_ref[...], ada_ref[...],
                           fw_ref[...])
        dh, ddn, _, dada, dfw = vjp(jnp.ones((1, 1), F32))
        dh_ref[...] = dh
        ddn_ref[...] = ddn.astype(BF16)
        lv = jnp.broadcast_to(val, loss_ref.shape)
        for r, v in ((loss_ref, lv), (dada_ref, dada), (dfw_ref, dfw)):
            @pl.when(i == 0)
            def _(r=r, v=v):
                r[...] = v

            @pl.when(i > 0)
            def _(r=r, v=v):
                r[...] += v

    row = lambda i: (i, 0)
    return pl.pallas_call(
        body, grid=(n_tiles,),
        in_specs=[pl.BlockSpec((TM, D), row), pl.BlockSpec((TM, D), row),
                  pl.BlockSpec((TM, D), lambda i: (jnp.maximum(i - nct, 0), 0)), _whole(ada), _whole(fw)],
        out_specs=[pl.BlockSpec((8, 128), lambda i: (0, 0)), pl.BlockSpec((TM, D), row), pl.BlockSpec((TM, D), row),
                   _whole(ada), _whole(fw)],
        out_shape=[jax.ShapeDtypeStruct((8, 128), F32), jax.ShapeDtypeStruct((A, D), F32),
                   jax.ShapeDtypeStruct((A, D), BF16), jax.ShapeDtypeStruct(ada.shape, F32),
                   jax.ShapeDtypeStruct(fw.shape, F32)],
        compiler_params=_cparams(("arbitrary",)), name="loss_head",
    )(h2, dn, tgt, ada, fw)


def _chunk_of(step, n_chunks, n_ctx_chunks, rev):
    if not rev:
        return step
    return jnp.where(step < n_ctx_chunks, n_ctx_chunks - 1 - step, n_chunks - 1 - (step - n_ctx_chunks))


def _gla_step(st, q, k, v, b, rev):
    tri = _tri(GLA_C, rev).astype(F32)
    last = 0 if rev else GLA_C - 1
    outs, sts = [], []
    for h in range(GLA_H):
        kk = slice(h * GLA_DK, (h + 1) * GLA_DK)
        vv = slice(h * GLA_DV, (h + 1) * GLA_DV)
        qh, kh, vh, bh, sh = q[:, kk] * (GLA_DK ** -0.5), k[:, kk], v[:, vv], b[:, kk], st[vv]
        tot = bh[last:last + 1]
        mid = bh[GLA_C // 2:GLA_C // 2 + 1]
        att = _bdot(qh * jnp.exp(bh - mid), kh * jnp.exp(mid - bh), False, True) * tri
        outs.append(_bdot(att, vh) + _bdot(qh * jnp.exp(bh), sh, False, True))
        sts.append(sh * jnp.exp(tot) + _bdot(vh, kh * jnp.exp(tot - bh), True, False))
    return jnp.concatenate(sts, axis=0), jnp.concatenate(outs, axis=1)


_GQK, _GV = GLA_H * GLA_DK, GLA_H * GLA_DV


GLA_CPS = 4
_GB = GLA_C * GLA_CPS


def _gla_order(rev):
    return list(reversed(range(GLA_CPS))) if rev else list(range(GLA_CPS))


def _gla_fwd(proj, bg_f, bg_b, ncc, name):
    A = proj.shape[0]
    nb = A // _GB
    chs = [lambda s, rev=rev: _chunk_of(s, nb, ncc // GLA_CPS, rev) for rev in (False, True)]

    def body(*refs):
        ins, outs, sts = refs[:8], refs[8:12], refs[12:]

        for d, rev in enumerate((False, True)):
            q_ref, k_ref, v_ref, b_ref = ins[4 * d:4 * d + 4]
            o_ref, ss_ref = outs[2 * d:2 * d + 2]
            st_ref = sts[d]

            @pl.when(pl.program_id(0) == 0)
            def _(st_ref=st_ref):
                st_ref[...] = jnp.zeros_like(st_ref)

            st = st_ref[...]
            for pos, sub in enumerate(_gla_order(rev)):
                rows = slice(sub * GLA_C, (sub + 1) * GLA_C)
                ss_ref[0, pos] = st
                st, o = _gla_step(st, q_ref[rows].astype(F32), k_ref[rows].astype(F32), v_ref[rows].astype(F32),
                                  b_ref[rows], rev)
                o_ref[rows] = o.astype(o_ref.dtype)
            st_ref[...] = st

    in_specs, out_specs = [], []
    for ch in chs:
        in_specs += [pl.BlockSpec((_GB, _GQK), lambda s, ch=ch: (ch(s), C_Q // _GQK)),
                     pl.BlockSpec((_GB, _GQK), lambda s, ch=ch: (ch(s), C_K // _GQK)),
                     pl.BlockSpec((_GB, _GV), lambda s, ch=ch: (ch(s), C_V // _GV)),
                     pl.BlockSpec((_GB, _GQK), lambda s, ch=ch: (ch(s), 0))]
        out_specs += [pl.BlockSpec((_GB, _GV), lambda s, ch=ch: (ch(s), 0)),
                      pl.BlockSpec((1, GLA_CPS, _GV, GLA_DK), lambda s: (s, 0, 0, 0))]
    return pl.pallas_call(
        body, grid=(nb,), in_specs=in_specs, out_specs=out_specs,
        out_shape=[jax.ShapeDtypeStruct((A, _GV), BF16), jax.ShapeDtypeStruct((nb, GLA_CPS, _GV, GLA_DK), F32)] * 2,
        scratch_shapes=[pltpu.VMEM((_GV, GLA_DK), F32)] * 2,
        compiler_params=_cparams(("arbitrary",)), name=name,
    )(proj, proj, proj, bg_f, proj, proj, proj, bg_b)


def _gla_bwd(proj, lg, saved, d_o, prev, ncc, rev, name, pack=None):
    A = proj.shape[0]
    nb = A // _GB
    st_of = lambda r: nb - 1 - r
    ch = lambda r: _chunk_of(st_of(r), nb, ncc // GLA_CPS, rev)
    qs = pl.BlockSpec((_GB, _GQK), lambda r: (ch(r), 0))
    vs = pl.BlockSpec((_GB, _GV), lambda r: (ch(r), 0))
    n_prev = 0 if prev is None else 3

    n_buf = 0 if pack is None else 1

    def body(*refs):
        q_ref, k_ref, v_ref, b_ref, ss_ref, do_ref = refs[:6]
        p_refs = refs[6:6 + n_prev]
        out_refs = refs[6 + n_prev + n_buf:]
        ds_ref = out_refs[-1]

        @pl.when(pl.program_id(0) == 0)
        def _():
            ds_ref[...] = jnp.zeros_like(ds_ref)

        ds = ds_ref[...]
        for pos, sub in reversed(list(enumerate(_gla_order(rev)))):
            rows = slice(sub * GLA_C, (sub + 1) * GLA_C)
            _, vjp = jax.vjp(functools.partial(_gla_step, rev=rev), ss_ref[0, pos], q_ref[rows].astype(F32),
                             k_ref[rows].astype(F32), v_ref[rows].astype(F32), b_ref[rows])
            ds, dq, dk, dv, db = vjp((ds, do_ref[rows].astype(F32)))
            if n_prev:
                dq, dk, dv = [d + p[rows].astype(F32) for d, p in zip((dq, dk, dv), p_refs)]
            if pack is None:
                for r, d in zip(out_refs[:4], (dq, dk, dv, db)):
                    r[rows] = d.astype(r.dtype)
            else:
                out_refs[0][rows] = jnp.concatenate([dq, dk, dv], axis=1).astype(out_refs[0].dtype)
                out_refs[1][rows] = db
        ds_ref[...] = ds

    in_specs = [
        pl.BlockSpec((_GB, _GQK), lambda r: (ch(r), C_Q // _GQK)),
        pl.BlockSpec((_GB, _GQK), lambda r: (ch(r), C_K // _GQK)),
        pl.BlockSpec((_GB, _GV), lambda r: (ch(r), C_V // _GV)),
        qs,
        pl.BlockSpec((1, GLA_CPS, _GV, GLA_DK), lambda r: (st_of(r), 0, 0, 0)),
        vs,
    ] + ([qs, qs, vs] if n_prev else [])
    args = [proj, proj, proj, lg, saved, d_o, *(prev or ())]
    if pack is None:
        out_specs = [qs, qs, vs, qs]
        out_shape = [jax.ShapeDtypeStruct((A, _GQK), BF16), jax.ShapeDtypeStruct((A, _GQK), BF16),
                     jax.ShapeDtypeStruct((A, _GV), BF16), jax.ShapeDtypeStruct((A, _GQK), F32)]
        aliases = None
    else:
        in_specs.append(pl.BlockSpec(memory_space=pl.ANY))
        args.append(pack[0])
        out_specs = [pl.BlockSpec((_GB, 2 * _GQK + _GV), lambda r: (ch(r), 0)), qs]
        out_shape = [jax.ShapeDtypeStruct((A, pack[1]), BF16), jax.ShapeDtypeStruct((A, _GQK), F32)]
        aliases = {len(args) - 1: 0}
    return _call(body, grid=(nb,), in_specs=in_specs, out_specs=out_specs, out_shape=out_shape,
                 scratch=[pltpu.VMEM((_GV, GLA_DK), F32)], sem=("arbitrary",), name=name, args=args, aliases=aliases)


def _ssd_step(st, x, bm, cm, cum, cum_t, rev):
    mask = _tri(SSM_C, rev)
    last = 0 if rev else SSM_C - 1
    tot = cum[last:last + 1]
    cb = _bdot(cm, bm, False, True)
    ys = []
    for e in range(SSM_HPG):
        lm = jnp.exp(jnp.where(mask, cum[:, e * SSM_P:e * SSM_P + 1] - cum_t[e:e + 1], -jnp.inf))
        ys.append(_bdot(cb * lm, x[:, e * SSM_P:(e + 1) * SSM_P]))
    y = jnp.concatenate(ys, axis=1) + _bdot(cm, st) * jnp.exp(cum)
    st_new = st * jnp.exp(tot) + _bdot(bm, x * jnp.exp(tot - cum), True, False)
    return st_new, y


SSM_GPS = 4
_W = SSM_HPG * SSM_P
_XBC_B = SSM_INNER // (SSM_N * SSM_GPS)
_XBC_C = _XBC_B + SSM_G // SSM_GPS


def _ssd_steps(st, x, bm, cm, cum, cum_t, rev):
    outs = [_ssd_step(st[j * SSM_N:(j + 1) * SSM_N], x[:, j * _W:(j + 1) * _W], bm[:, j * SSM_N:(j + 1) * SSM_N],
                      cm[:, j * SSM_N:(j + 1) * SSM_N], cum[:, j * _W:(j + 1) * _W],
                      cum_t[j * SSM_HPG:(j + 1) * SSM_HPG], rev) for j in range(SSM_GPS)]
    return jnp.concatenate([o[0] for o in outs], axis=0), jnp.concatenate([o[1] for o in outs], axis=1)


SSM_CPS = 2
_SB = SSM_C * SSM_CPS


def _ssd_order(rev):
    return list(reversed(range(SSM_CPS))) if rev else list(range(SSM_CPS))


def _ssd_fwd(x_f, x_b, xbc, cum_f, cum_b, cum_tf, cum_tb, ncc, name, hosted=()):
    A = x_f.shape[0]
    nb = A // _SB
    ng = SSM_G // SSM_GPS
    chs = [lambda s, rev=rev: _chunk_of(s, nb, ncc // SSM_CPS, rev) for rev in (False, True)]

    def body(*refs):
        ins, outs, sts = refs[:10], refs[10:14], refs[14:]
        for d, rev in enumerate((False, True)):
            x_ref, b_ref, c_ref, a_ref, at_ref = ins[5 * d:5 * d + 5]
            y_ref, ss_ref = outs[2 * d:2 * d + 2]
            st_ref = sts[d]

            @pl.when(pl.program_id(1) == 0)
            def _(st_ref=st_ref):
                st_ref[...] = jnp.zeros_like(st_ref)

            st = st_ref[...]
            for pos, sub in enumerate(_ssd_order(rev)):
                rows = slice(sub * SSM_C, (sub + 1) * SSM_C)
                ss_ref[0, 0, pos] = st
                st, y = _ssd_steps(st, x_ref[rows].astype(F32), b_ref[rows].astype(F32), c_ref[rows].astype(F32),
                                   a_ref[rows], at_ref[:, rows], rev)
                y_ref[rows] = y.astype(y_ref.dtype)
            st_ref[...] = st

    in_specs, out_specs = [], []
    for ch in chs:
        gs = pl.BlockSpec((_SB, _W * SSM_GPS), lambda g, s, ch=ch: (ch(s), g))
        in_specs += [gs, pl.BlockSpec((_SB, SSM_N * SSM_GPS), lambda g, s, ch=ch: (ch(s), _XBC_B + g)),
                     pl.BlockSpec((_SB, SSM_N * SSM_GPS), lambda g, s, ch=ch: (ch(s), _XBC_C + g)), gs,
                     pl.BlockSpec((SSM_HPG * SSM_GPS, _SB), lambda g, s, ch=ch: (g, ch(s)))]
        out_specs += [gs, pl.BlockSpec((1, 1, SSM_CPS, SSM_N * SSM_GPS, _W), lambda g, s: (g, s, 0, 0, 0))]
    return _call(
        body, grid=(ng, nb), in_specs=in_specs, out_specs=out_specs,
        out_shape=[jax.ShapeDtypeStruct((A, SSM_INNER), BF16),
                   jax.ShapeDtypeStruct((ng, nb, SSM_CPS, SSM_N * SSM_GPS, _W), F32)] * 2,
        scratch=[pltpu.VMEM((SSM_N * SSM_GPS, _W), F32)] * 2,
        sem=("parallel", "arbitrary"), name=name,
        args=(x_f, xbc, xbc, cum_f, cum_tf, x_b, xbc, xbc, cum_b, cum_tb), hosted=hosted)


def _ssd_bwd(xd, xbc, cum, cum_t, saved, d_y, prev, ncc, rev, name):
    A = xd.shape[0]
    nc = A // SSM_C
    ng = SSM_G // SSM_GPS
    st_of = lambda r: nc - 1 - r
    ch = lambda r: _chunk_of(st_of(r), nc, ncc, rev)
    gs = pl.BlockSpec((SSM_C, _W * SSM_GPS), lambda g, r: (ch(r), g))
    ns = pl.BlockSpec((SSM_C, SSM_N * SSM_GPS), lambda g, r: (ch(r), g))
    ts = pl.BlockSpec((SSM_HPG * SSM_GPS, SSM_C), lambda g, r: (g, ch(r)))
    n_prev = 0 if prev is None else 2

    def body(*refs):
        x_ref, b_ref, c_ref, a_ref, at_ref, ss_ref, dy_ref = refs[:7]
        p_refs = refs[7:7 + n_prev]
        dx_ref, db_ref, dc_ref, da_ref, dat_ref, ds_ref = refs[7 + n_prev:]

        @pl.when(pl.program_id(1) == 0)
        def _():
            ds_ref[...] = jnp.zeros_like(ds_ref)

        _, vjp = jax.vjp(functools.partial(_ssd_steps, rev=rev), ss_ref[0, 0, 0], x_ref[...].astype(F32),
                         b_ref[...].astype(F32), c_ref[...].astype(F32), a_ref[...], at_ref[...])
        ds, dx, db, dc, da, dat = vjp((ds_ref[...], dy_ref[...].astype(F32)))
        if n_prev:
            db, dc = db + p_refs[0][...].astype(F32), dc + p_refs[1][...].astype(F32)
        for r, d in zip((dx_ref, db_ref, dc_ref, da_ref, dat_ref), (dx, db, dc, da, dat)):
            r[...] = d.astype(r.dtype)
        ds_ref[...] = ds

    in_specs = [gs, pl.BlockSpec((SSM_C, SSM_N * SSM_GPS), lambda g, r: (ch(r), _XBC_B + g)),
                pl.BlockSpec((SSM_C, SSM_N * SSM_GPS), lambda g, r: (ch(r), _XBC_C + g)), gs, ts,
                pl.BlockSpec((1, 1, 1, SSM_N * SSM_GPS, _W),
                             lambda g, r: (g, st_of(r) // SSM_CPS, st_of(r) % SSM_CPS, 0, 0)), gs]
    in_specs += [ns, ns] if n_prev else []
    return pl.pallas_call(
        body, grid=(ng, nc), in_specs=in_specs, out_specs=[gs, ns, ns, gs, ts],
        out_shape=[jax.ShapeDtypeStruct((A, SSM_INNER), BF16), jax.ShapeDtypeStruct((A, SSM_G * SSM_N), BF16),
                   jax.ShapeDtypeStruct((A, SSM_G * SSM_N), BF16), jax.ShapeDtypeStruct((A, SSM_INNER), F32),
                   jax.ShapeDtypeStruct((SSM_G * SSM_HPG, A), F32)],
        scratch_shapes=[pltpu.VMEM((SSM_N * SSM_GPS, _W), F32)],
        compiler_params=_cparams(("parallel", "arbitrary")), name=name,
    )(xd, xbc, xbc, cum, cum_t, saved, d_y, *(prev or ()))


_ADA_ROWS = 16


def _ada_fwd(cc, w_shard, b_shard):
    n = w_shard.shape[1]

    def body(cc_ref, w_ref, b_ref, o_ref):
        o_ref[...] = _dot_impl(_silu(cc_ref[...]), w_ref[...], False, False) + b_ref[...]

    return pl.pallas_call(body, out_shape=jax.ShapeDtypeStruct((_ADA_ROWS, n), F32),
                          compiler_params=_cparams(()), name="ada_fwd")(cc, w_shard, b_shard)


def _ada_bwd(cc, w_shard, d_all, d_mine):
    n = w_shard.shape[1]

    def rows(ref, r):
        parts = []
        for s in range(N_DEV):
            parts.append(ref[6 * s + r:6 * s + r + 1] + ref[6 * s + 2 + r:6 * s + 3 + r] + ref[6 * s + 4 + r:6 * s + 5 + r])
        return parts

    def total(parts):
        t = parts[0]
        for p in parts[1:]:
            t = t + p
        return t

    def body(cc_ref, w_ref, da_ref, dm_ref, dw_ref, db_ref, dcc_ref):
        dd = jnp.concatenate(rows(dm_ref, 0) + [total(rows(dm_ref, 1)), jnp.zeros((_ADA_ROWS - N_DEV - 1, n), F32)], axis=0)
        cc = cc_ref[...]
        dw_ref[...] = _dot_impl(_silu(cc), dd, True, False)
        db_ref[...] = total(rows(da_ref, 0)) + total(rows(da_ref, 1))
        _, vjp = jax.vjp(_silu, cc)
        dcc_ref[...] = vjp(_dot_impl(dd, w_ref[...], False, True))[0]

    return pl.pallas_call(
        body, out_shape=[jax.ShapeDtypeStruct((D, n), F32), jax.ShapeDtypeStruct((1, d_all.shape[1]), F32),
                         jax.ShapeDtypeStruct((_ADA_ROWS, D), F32)],
        compiler_params=_cparams(()), name="ada_bwd")(cc, w_shard, d_all, d_mine)


def _local_step(x, ctx, target, ada, w, shards):
    T, Tc = x.shape[0], ctx.shape[0]
    assert Tc == TM and T % TM == 0 and GRID_W == GLA_C and TS == SSM_C == 2 * GLA_C
    A = T + Tc
    nct = Tc // TM
    n_tm, n_ts = A // TM, A // TS
    tl = _pick(A, 3 * TM, TM)
    n_tl = A // tl
    g = {}

    x_in = [_ctx(ctx, TM, nct), _lat(x, TM, nct)]
    f_norm1 = _f_norm1(nct)
    p_norm1 = [w["norm1_w"], ada]
    (h1,) = _stage_fwd(f_norm1, n_tm, x_in, p_norm1, [(A, D, BF16, TM)], "norm1")
    proj, ag_pa, ag_pb, ag_out = _mm(h1, w["wt_in"], trans_b=True, out_dtype=BF16, name="mm_in", tn_t=1152,
                                     hosted=[(shards[n], False) for n in ("w_pa", "w_pb", "w_out")])
    small = _mm(h1, w["wt_in"][C_S:], trans_b=True, out_dtype=F32, name="mm_in_small")

    p_gd = [w["up_f"], w["up_b"], w["gla_bias_f"], w["gla_bias_b"], w["dtb"], w["alog"]]
    f_conv = _f_conv(nct, Tc)
    p_conv = [w["conv_w"], w["conv_b"]]
    in_conv = [_col(proj, TM, 3072, C_U // 3072)]
    xbc, conv_pre = _stage_fwd(f_conv, n_tm, in_conv, p_conv, [(A, 3072, BF16, TM)] * 2, "conv")
    in_gd = [_col(small, TS), _col(xbc, TS, SSM_INNER, 0)]
    n_heads = SSM_G * SSM_HPG
    lg_f, lg_b, xf, xb, axf, axb, atf, atb = _stage_fwd(
        _f_gates_dt, n_ts, in_gd, p_gd,
        [(A, 512, F32, TS)] * 2 + [(A, SSM_INNER, BF16, TS)] * 2 + [(A, SSM_INNER, F32, TS)] * 2
        + [(A, n_heads, F32, TS, True)] * 2,
        "gates_dt")

    ncc_g, ncc_s = Tc // GLA_C, Tc // SSM_C
    ogf, sv_gf, ogb, sv_gb = _gla_fwd(proj, lg_f, lg_b, ncc_g, "gla_fb")
    ysf, sv_sf, ysb, sv_sb, ag_gate, ag_up, ag_down = _ssd_fwd(
        xf, xb, xbc, axf, axb, atf, atb, ncc_s, "ssd_fb", hosted=[(shards[n], False) for n in ("w_gate", "w_up", "w_down")])
    w = dict(w, w_pa=ag_pa.reshape(D, D), w_pb=ag_pb.reshape(SSM_INNER, D), w_out=ag_out.reshape(D, D),
             wt_gu=jnp.concatenate([ag_gate.reshape(D_FF, D), ag_up.reshape(D_FF, D)], axis=0), w_down=ag_down.reshape(D_FF, D))
    got = {}

    in_gp = [_col(ogf, tl), _col(ogb, tl), _col(proj, tl, 1024, C_R // 1024)]
    p_gp = [w["gla_norm_w"]]
    (oa,) = _stage_fwd(_f_gla_post, n_tl, in_gp, p_gp, [(A, D, BF16, tl)], "gla_post")
    in_sp = [_col(ysf, TM), _col(ysb, TM), _col(xbc, TM, SSM_INNER, 0), _col(proj, TM, SSM_INNER, C_Z // SSM_INNER)]
    p_sp = [w["dsk8"], w["ssm_norm_w"]]
    (ob,) = _stage_fwd(_f_ssd_post, n_tm, in_sp, p_sp, [(A, SSM_INNER, BF16, TM)], "ssd_post")
    ya = _mm(oa, w["w_pa"], trans_b=False, out_dtype=BF16, name="mm_pa")
    yb = _mm(ob, w["w_pb"], trans_b=False, out_dtype=BF16, name="mm_pb")
    in_mg = [_col(proj, tl, 1024, C_GA // 1024), _col(proj, tl, 1024, C_GB // 1024), _col(ya, tl), _col(yb, tl)]
    (merged,) = _stage_fwd(_f_merge, n_tl, in_mg, [], [(A, D, BF16, tl)], "merge")
    mix = _mm(merged, w["w_out"], trans_b=False, out_dtype=BF16, name="mm_out")

    f_res1 = _f_res1(nct)
    in_r1 = x_in + [_col(mix, TM)]
    p_r1 = [ada, w["norm2_w"]]
    h2, hm2 = _stage_fwd(f_res1, n_tm, in_r1, p_r1, [(A, D, F32, TM), (A, D, BF16, TM)], "res1")
    gu = _mm(hm2, w["wt_gu"], trans_b=True, out_dtype=BF16, name="mm_gu", tn_t=1408)
    in_sw = [_col(gu, TM)]
    (act,) = _stage_fwd(_f_swiglu, n_tm, in_sw, [], [(A, D_FF, BF16, TM)], "swiglu")
    dn = _mm(act, w["w_down"], trans_b=False, out_dtype=BF16, name="mm_down")

    loss_blk, d_h2a, d_dn, d_ada3, g["final_norm_w"] = _loss_head(h2, dn, target, ada, w["final_norm_w"], nct)
    rows16 = lambda gw: (gw.reshape(N_DEV, gw.shape[0] // N_DEV, gw.shape[1]).astype(BF16), True)
    gw_down = _mm_tn(act, d_dn, name="dw_down", tr_t=1408)
    d_act = _mm(d_dn, w["w_down"], trans_b=True, out_dtype=BF16, name="dx_down", tn_t=1408)
    d_gu, got["w_down"] = _stage_bwd(_f_swiglu, n_tm, in_sw, [], [_col(d_act, TM)], lambda i, t: t, [(BF16, None)],
                                     "swiglu_b", hosted=[rows16(gw_down)])
    gwt_gu = _mm_tn(d_gu, hm2, name="dw_gu", tr_t=1408)
    d_hm2 = _mm(d_gu, w["wt_gu"], trans_b=False, out_dtype=BF16, name="dx_gu")
    d_x1, d_mix, d_ada2, g["norm2_w"], got["w_gate"] = _stage_bwd(
        f_res1, n_tm, in_r1, p_r1, [_col(d_h2a, TM), _col(d_hm2, TM)], lambda i, t: t,
        [None, (F32, None), (BF16, None)], "res1_b", hosted=[rows16(gwt_gu[:D_FF])])
    gw_out = _mm_tn(merged, d_mix, name="dw_out")
    d_merged = _mm(d_mix, w["w_out"], trans_b=True, out_dtype=BF16, name="dx_out")
    d_ya, d_yb, d_proj, got["w_up"] = _stage_bwd(
        _f_merge, n_tl, in_mg, [], [_col(d_merged, tl)], lambda i, t: t, [None, None, (BF16, None), (BF16, None)], "merge_b",
        hosted=[rows16(gwt_gu[D_FF:])], pack=([0, 1], None, D_INP, C_GA // (2 * D)))
    gw_pa = _mm_tn(oa, d_ya, name="dw_pa")
    gw_pb = _mm_tn(ob, d_yb, name="dw_pb")
    d_oa = _mm(d_ya, w["w_pa"], trans_b=True, out_dtype=BF16, name="dx_pa")
    d_ob = _mm(d_yb, w["w_pb"], trans_b=True, out_dtype=BF16, name="dx_pb")
    d_og, g["gla_norm_w"], d_proj, got["w_out"] = _stage_bwd(
        _f_gla_post, n_tl, in_gp, p_gp, [_col(d_oa, tl)], lambda i, t: t, [(BF16, None), None, None], "gla_post_b",
        hosted=[rows16(gw_out)], pack=([2], d_proj, D_INP, C_R // D))
    d_ys, d_xs_skip, g["dsk8"], g["ssm_norm_w"], d_proj, got["w_pa"], got["w_pb"] = _stage_bwd(
        _f_ssd_post, n_tm, in_sp, p_sp, [_col(d_ob, TM)], lambda i, t: t,
        [(BF16, None), None, (BF16, None), None], "ssd_post_b", hosted=[rows16(gw_pa), rows16(gw_pb)],
        pack=([3], d_proj, D_INP, C_Z // SSM_INNER))

    dq, dk, dv, d_lgf = _gla_bwd(proj, lg_f, sv_gf, d_og, None, ncc_g, False, "gla_f_b")
    d_proj, d_lgb = _gla_bwd(proj, lg_b, sv_gb, d_og, (dq, dk, dv), ncc_g, True, "gla_b_b", pack=(d_proj, D_INP))
    d_xf, d_bm, d_cm, d_axf, d_atf = _ssd_bwd(xf, xbc, axf, atf, sv_sf, d_ys, None, ncc_s, False, "ssd_f_b")
    d_xb, d_bm, d_cm, d_axb, d_atb = _ssd_bwd(xb, xbc, axb, atb, sv_sb, d_ys, (d_bm, d_cm), ncc_s, True, "ssd_b_b")

    cts_gd = [_col(a, TS) for a in (d_lgf, d_lgb, d_xf, d_xb, d_axf, d_axb)]
    cts_gd += [(a, (n_heads, TS), lambda i: (0, i)) for a in (d_atf, d_atb)]
    d_xs_dt, g["up_f"], g["up_b"], g["gla_bias_f"], g["gla_bias_b"], g["dtb"], g["alog"], d_proj = _stage_bwd(
        _f_gates_dt, n_ts, in_gd, p_gd, cts_gd, lambda i, t: t, [None, (BF16, None)], "gates_dt_b",
        pack=([0], d_proj, D_INP, C_S // 128))
    g["conv_w"], g["conv_b"], d_proj = _conv_bwd(proj, conv_pre, (d_xs_skip, d_xs_dt, d_bm, d_cm), w["conv_w"], d_proj,
                                                 nct, Tc, "conv_b")
    gwt_in = _mm_tn(d_proj, h1, name="dw_in", tr_t=1152)
    d_h1, got["w_in"] = _mm(d_proj, w["wt_in"], trans_b=False, out_dtype=BF16, name="dx_in", tm_t=768, tk_t=3456,
                            hosted=[rows16(_from_padded(gwt_in))])
    grad_x, g["norm1_w"], d_ada1 = _stage_bwd(
        f_norm1, n_tm, x_in, p_norm1, [_col(d_h1, TM)], lambda i, t: t,
        [None, (F32, (d_x1, (TM, D), x_in[1][2]))], "norm1_b")
    return loss_blk, grad_x, g, got, (d_ada1, d_ada2, d_ada3)


def _exchange(x, a2a, name):
    shp = x.shape[1:] if a2a else x.shape

    def body(x_ref, o_ref, send_sems, recv_sems, local_sem):
        ops = _exchange_ops(x_ref, o_ref, send_sems, recv_sems, local_sem, a2a)
        for op in ops:
            op.start()
        for op in ops:
            op.wait()

    return pl.pallas_call(
        body, out_shape=jax.ShapeDtypeStruct((N_DEV,) + tuple(shp), x.dtype),
        in_specs=[pl.BlockSpec(memory_space=pl.ANY)], out_specs=pl.BlockSpec(memory_space=pl.ANY),
        scratch_shapes=[pltpu.SemaphoreType.DMA((N_DEV - 1,)), pltpu.SemaphoreType.DMA((N_DEV - 1,)),
                        pltpu.SemaphoreType.DMA(())],
        name=name,
    )(x)


def _all_gather(x, name):
    def body(x_ref, o_ref, send_sems, recv_sems, local_sem):
        mx, my, mc = lax.axis_index("x"), lax.axis_index("y"), lax.axis_index("c")
        me, sibling = (mx, my, mc), (mx, my, 1 - mc)
        chips = [(1 - mx, my), (mx, 1 - my), (1 - mx, 1 - my)]

        def slot(px, py, pc):
            return o_ref.at[4 * px + 2 * py + pc]

        def copy(k, block, to, src=None):
            return pltpu.make_async_remote_copy(
                src_ref=slot(*block) if src is None else src, dst_ref=slot(*block),
                send_sem=send_sems.at[k], recv_sem=recv_sems.at[k], device_id=to, device_id_type=pl.DeviceIdType.MESH)

        own = pltpu.make_async_copy(x_ref, slot(*me), local_sem)
        own.start()
        first = [copy(0, me, sibling, src=x_ref)] + [copy(1 + j, me, (*chip, mc), src=x_ref) for j, chip in enumerate(chips)]
        for cp in first:
            cp.start()
        passed = [copy(4 + j, (*chip, mc), sibling) for j, chip in enumerate(chips)]
        for j, chip in enumerate(chips):
            copy(1 + j, (*chip, mc), me).wait_recv()
            passed[j].start()
        copy(0, sibling, me).wait_recv()
        for j, chip in enumerate(chips):
            copy(4 + j, (*chip, 1 - mc), me).wait_recv()
        for cp in first + passed:
            cp.wait_send()
        own.wait()

    return pl.pallas_call(
        body, out_shape=jax.ShapeDtypeStruct((N_DEV,) + tuple(x.shape), x.dtype),
        in_specs=[pl.BlockSpec(memory_space=pl.ANY)], out_specs=pl.BlockSpec(memory_space=pl.ANY),
        scratch_shapes=[pltpu.SemaphoreType.DMA((N_DEV - 1,)), pltpu.SemaphoreType.DMA((N_DEV - 1,)),
                        pltpu.SemaphoreType.DMA(())],
        name=name,
    )(x)


def _adamw_math(w, gr, m, v):
    m = ADAM_B1 * m + (1.0 - ADAM_B1) * gr
    v = ADAM_B2 * v + (1.0 - ADAM_B2) * (gr * gr)
    m_hat = m / np.float32(1.0 - ADAM_B1 ** ADAM_STEP)
    v_hat = v / np.float32(1.0 - ADAM_B2 ** ADAM_STEP)
    delta = -ADAM_LR * (m_hat / (jnp.sqrt(v_hat) + ADAM_EPS) + ADAM_WD * w)
    return delta, m, v


def _sum_adamw(parts, w, m, v, name):
    R, C = w.shape
    n_parts = parts.shape[0]
    tr = _pick(R, max(16, (2 << 20) // (4 * C) // 16 * 16), 16)

    def body(p_ref, w_ref, m_ref, v_ref, g_ref, d_ref, mo_ref, vo_ref):
        gr = p_ref[0].astype(F32)
        for k in range(1, n_parts):
            gr = gr + p_ref[k].astype(F32)
        g_ref[...] = gr
        d_ref[...], mo_ref[...], vo_ref[...] = _adamw_math(w_ref[...], gr, m_ref[...], v_ref[...])

    if R % 16 == 0:
        grid = (R // tr,)
        tile, p_spec = pl.BlockSpec((tr, C), lambda i: (i, 0)), pl.BlockSpec((n_parts, tr, C), lambda i: (0, i, 0))
    else:
        tc = _pick(C, max(128, (2 << 20) // (4 * R) // 128 * 128), 128)
        grid = (C // tc,)
        tile, p_spec = pl.BlockSpec((R, tc), lambda i: (0, i)), pl.BlockSpec((n_parts, R, tc), lambda i: (0, 0, i))
    return pl.pallas_call(
        body, grid=grid, in_specs=[p_spec, tile, tile, tile],
        out_specs=[tile] * 4, out_shape=[jax.ShapeDtypeStruct((R, C), F32)] * 4,
        compiler_params=_cparams(("parallel",)), name=name,
    )(parts, w, m, v)


def _sum8(parts, name):
    _, R, C = parts.shape

    def body(p_ref, g_ref):
        gr = p_ref[0]
        for k in range(1, N_DEV):
            gr = gr + p_ref[k]
        g_ref[...] = gr

    return pl.pallas_call(body, out_shape=jax.ShapeDtypeStruct((R, C), F32), name=name)(parts)


def _adamw(w, gr, m, v, name):
    def body(w_ref, g_ref, m_ref, v_ref, d_ref, mo_ref, vo_ref):
        d_ref[...], mo_ref[...], vo_ref[...] = _adamw_math(w_ref[...], g_ref[...], m_ref[...], v_ref[...])

    return pl.pallas_call(body, out_shape=[jax.ShapeDtypeStruct(w.shape, F32)] * 3, name=name)(w, gr, m, v)


def _to_padded(wt_in):
    z = jnp.zeros((32,) + wt_in.shape[1:], wt_in.dtype)
    return jnp.concatenate([wt_in[0:3072], wt_in[5152:8224], wt_in[3104:5152], wt_in[8288:10336],
                            wt_in[3072:3104], wt_in[8224:8288], z], axis=0)


def _from_padded(p):
    return jnp.concatenate([p[0:3072], p[10240:10272], p[6144:8192], p[3072:6144], p[10272:10336], p[8192:10240]], axis=0)


def _unshard_cols(gathered):
    n, r, c = gathered.shape
    return jnp.transpose(gathered, (1, 0, 2)).reshape(r, n * c)


def _lanes(vec, lo):
    return jnp.concatenate([jnp.zeros((1, lo), F32), vec, jnp.zeros((1, 128 - lo - vec.shape[1]), F32)], axis=1)


_SMALL = (("b_ada", 6 * D), ("c_ctx", D), ("norm1_w", D), ("gla_bias_f", 512), ("gla_bias_b", 512), ("gla_norm_w", 256),
          ("conv_b", 3072), ("dt_bias_f", 32), ("dt_bias_b", 32), ("a_log_f", 32), ("a_log_b", 32), ("d_skip", 32),
          ("ssm_norm_w", 2048), ("norm2_w", D), ("final_norm_w", D))
_SHARDED_SMALL = (("gla_up_f", 16 * 512), ("gla_up_b", 16 * 512), ("conv_w", 4 * 3072))


def _pack(vals, names):
    flat = jnp.concatenate([vals[n].reshape(-1).astype(F32) for n, _ in names])
    pad = (-flat.shape[0]) % 1024
    return jnp.concatenate([flat, jnp.zeros((pad,), F32)]).reshape(-1, 128)


def _unpack(packed, names, shapes):
    flat, out, off = packed.reshape(-1), {}, 0
    for n, size in names:
        out[n] = flat[off:off + size].reshape(shapes[n])
        off += size
    return out


def kernel(x, c, ctx, c_ctx, w_ada, b_ada, norm1_w, w_in, gla_up_f, gla_bias_f, gla_up_b, gla_bias_b, gla_norm_w, conv_w, conv_b, dt_bias_f, dt_bias_b, a_log_f, a_log_b, d_skip, ssm_norm_w, w_pa, w_pb, w_out, norm2_w, w_gate, w_up, w_down, final_norm_w, loss_target, m_c_ctx, m_w_ada, m_b_ada, m_norm1_w, m_w_in, m_gla_up_f, m_gla_bias_f, m_gla_up_b, m_gla_bias_b, m_gla_norm_w, m_conv_w, m_conv_b, m_dt_bias_f, m_dt_bias_b, m_a_log_f, m_a_log_b, m_d_skip, m_ssm_norm_w, m_w_pa, m_w_pb, m_w_out, m_norm2_w, m_w_gate, m_w_up, m_w_down, m_final_norm_w, v_c_ctx, v_w_ada, v_b_ada, v_norm1_w, v_w_in, v_gla_up_f, v_gla_bias_f, v_gla_up_b, v_gla_bias_b, v_gla_norm_w, v_conv_w, v_conv_b, v_dt_bias_f, v_dt_bias_b, v_a_log_f, v_a_log_b, v_d_skip, v_ssm_norm_w, v_w_pa, v_w_pb, v_w_out, v_norm2_w, v_w_gate, v_w_up, v_w_down, v_final_norm_w):
    args = dict(locals())
    me = 4 * lax.axis_index("x") + 2 * lax.axis_index("y") + lax.axis_index("c")

    t_names = ("w_in", "w_gate", "w_up")
    local = {n: (jnp.transpose(args[n][0]) if n in t_names else args[n][0]) for n in
             ("w_in", "w_pa", "w_pb", "w_out", "w_gate", "w_up", "w_down")}
    full = {"wt_in": _to_padded(_all_gather(local["w_in"].astype(BF16), "ag_w_in").reshape(D_IN, D))}
    shards = {n: local[n].astype(BF16) for n in ("w_pa", "w_pb", "w_out", "w_gate", "w_up", "w_down")}
    sm = _exchange(jnp.concatenate([gla_up_f.reshape(-1), gla_up_b.reshape(-1), conv_w.reshape(-1), c.reshape(-1)]).reshape(-1, 128),
                   False, "ag_small")
    sm = sm.reshape(N_DEV, -1)
    n_ada = w_ada.shape[2]
    cc = jnp.concatenate([sm[:, 3584:3584 + D], c_ctx[None], jnp.zeros((_ADA_ROWS - N_DEV - 1, D), F32)], axis=0)
    w_ada16 = w_ada[0].astype(BF16)
    ada_cols = _exchange(_ada_fwd(cc, w_ada16, lax.dynamic_slice(b_ada, (0, me * n_ada), (1, n_ada))), False, "ag_ada")
    ada = jnp.concatenate([lax.dynamic_index_in_dim(ada_cols, me, axis=1, keepdims=False).reshape(1, -1),
                           ada_cols[:, N_DEV].reshape(1, -1), jnp.zeros((6, N_DEV * n_ada), F32)], axis=0)
    up_f = _unshard_cols(sm[:, 0:1024].reshape(N_DEV, 16, 64))
    up_b = _unshard_cols(sm[:, 1024:2048].reshape(N_DEV, 16, 64))
    full["conv_w"] = _unshard_cols(sm[:, 2048:3584].reshape(N_DEV, 4, 384))
    full["up_f"] = jnp.concatenate([up_f, jnp.zeros((112, 512), F32)], axis=0)
    full["up_b"] = jnp.concatenate([jnp.zeros((16, 512), F32), up_b, jnp.zeros((96, 512), F32)], axis=0)
    full["dtb"] = jnp.concatenate([jnp.zeros((1, 32), F32), dt_bias_f, dt_bias_b, jnp.zeros((1, 32), F32)], axis=1)
    full["alog"] = jnp.concatenate([jnp.zeros((1, 32), F32), a_log_f, a_log_b, jnp.zeros((1, 32), F32)], axis=1)
    full["dsk8"] = jnp.concatenate([_lanes(d_skip, 0), jnp.zeros((7, 128), F32)], axis=0)
    for n in ("norm1_w", "gla_bias_f", "gla_bias_b", "gla_norm_w", "conv_b", "ssm_norm_w", "norm2_w"):
        full[n] = args[n]
    full["final_norm_w"] = final_norm_w.reshape(1, D)

    loss_blk, grad_x, g, got, d_ada = _local_step(x[0], ctx[0], loss_target[0], ada, full, shards)

    d_all = _exchange(jnp.concatenate([d[0:2] for d in d_ada], axis=0), False, "ag_d_ada").reshape(N_DEV * 6, N_DEV * n_ada)
    gw_ada, gb_ada, d_cc = _ada_bwd(cc, w_ada16, d_all, lax.dynamic_slice(d_all, (0, me * n_ada), (N_DEV * 6, n_ada)))

    gs = dict(g, c_ctx=d_cc[N_DEV], b_ada=jnp.zeros_like(b_ada))
    gs["dt_bias_f"], gs["dt_bias_b"] = g["dtb"][:, 32:64], g["dtb"][:, 64:96]
    gs["a_log_f"], gs["a_log_b"] = g["alog"][:, 32:64], g["alog"][:, 64:96]
    gs["d_skip"] = g["dsk8"][0:1, 0:32]
    gs["gla_up_f"], gs["gla_up_b"] = g["up_f"][0:16], g["up_b"][16:32]
    gs["loss"] = loss_blk[0:1, 0:1]
    names = _SMALL + _SHARDED_SMALL + (("loss", 1),)
    shapes = {n: (args[n].shape if n in args else (1, 1)) for n, _ in names}
    shapes.update({"gla_up_f": (16, 512), "gla_up_b": (16, 512), "conv_w": (4, 3072)})
    red = _unpack(_sum8(_exchange(_pack(gs, names), False, "ag_small_grads"), "sum_small_grads"), names, shapes)
    loss = red["loss"].reshape(())
    grads = {n: red[n] for n, _ in _SMALL}
    grads["b_ada"] = gb_ada
    grads["gla_up_f"] = lax.dynamic_slice(red["gla_up_f"], (0, me * 64), (16, 64))[None]
    grads["gla_up_b"] = lax.dynamic_slice(red["gla_up_b"], (0, me * 64), (16, 64))[None]
    grads["conv_w"] = lax.dynamic_slice(red["conv_w"], (0, me * 384), (4, 384))[None]
    upd_names = tuple((n, s) for n, s in _SMALL) + (("gla_up_f", 1024), ("gla_up_b", 1024), ("conv_w", 1536))
    pk = lambda prefix, src: _pack({n: src[prefix + n] for n, _ in upd_names}, upd_names)
    d_s, m_s, v_s = _adamw(pk("", args), _pack(grads, upd_names), pk("m_", args), pk("v_", args), "adamw_small")
    upd_shapes = {n: args[n].shape for n, _ in upd_names}
    delta = _unpack(d_s, upd_names, upd_shapes)
    new_m = _unpack(m_s, upd_names, upd_shapes)
    new_v = _unpack(v_s, upd_names, upd_shapes)

    got["w_ada"] = gw_ada[None]
    for n, parts in got.items():
        orient = jnp.transpose if n in t_names else (lambda a: a)
        res = _sum_adamw(parts, local[n] if n in local else args[n][0], orient(args["m_" + n][0]), orient(args["v_" + n][0]),
                         "adamw_" + n)
        grads[n], delta[n], new_m[n], new_v[n] = [orient(r)[None] for r in res]

    order = ["c_ctx", "w_ada", "b_ada", "norm1_w", "w_in", "gla_up_f", "gla_bias_f", "gla_up_b", "gla_bias_b", "gla_norm_w",
             "conv_w", "conv_b", "dt_bias_f", "dt_bias_b", "a_log_f", "a_log_b", "d_skip", "ssm_norm_w", "w_pa", "w_pb",
             "w_out", "norm2_w", "w_gate", "w_up", "w_down", "final_norm_w"]
    fix = lambda d: [d[n].reshape(args[n].shape) for n in order]
    return (loss, grad_x[None], *fix(grads), *fix(delta), *fix(new_m), *fix(new_v))
```

```python
import functools

import jax
import jax.numpy as jnp
import numpy as np
from jax import lax
from jax.experimental import pallas as pl
from jax.experimental.pallas import tpu as pltpu

F32 = jnp.float32
BF16 = jnp.bfloat16
N_DEV = 8
D = 1024
EPS = 1e-6
GRID_W = 64
GLA_H, GLA_DK, GLA_DV = 4, 128, 256
GLA_C = 64
GLA_TAU = 16.0
SSM_G, SSM_HPG, SSM_P, SSM_N = 4, 8, 64, 128
SSM_C = 128
SSM_INNER = 2048
D_FF = 2816
D_IN = 10336
D_INP = 10368
C_Q, C_K, C_V, C_R, C_U, C_Z, C_GA, C_GB, C_S = 0, 512, 1024, 2048, 3072, 6144, 8192, 9216, 10240
TM = 256
TS = 128

ADAM_LR, ADAM_B1, ADAM_B2, ADAM_EPS, ADAM_WD, ADAM_STEP = 0.001, 0.9, 0.999, 1e-08, 0.01, 10
VMEM_LIMIT = 56 << 20


def _cparams(sem):
    return pltpu.CompilerParams(dimension_semantics=sem or None, vmem_limit_bytes=VMEM_LIMIT)


def _pick(n, target, mult):
    best = None
    for t in range(mult, min(n, target) + 1, mult):
        if n % t == 0:
            best = t
    return best if best is not None else n


def _dot_impl(a, b, ta, tb):
    dims = (((0 if ta else 1,), (1 if tb else 0,)), ((), ()))
    return lax.dot_general(a.astype(BF16), b.astype(BF16), dims, preferred_element_type=F32)


@functools.partial(jax.custom_vjp, nondiff_argnums=(2, 3))
def _bdot(a, b, ta=False, tb=False):
    return _dot_impl(a, b, ta, tb)


def _bdot_fwd(a, b, ta, tb):
    return _dot_impl(a, b, ta, tb), (a, b)


def _bdot_bwd(ta, tb, res, g):
    a, b = res
    if not ta and not tb:
        return _dot_impl(g, b, False, True), _dot_impl(a, g, True, False)
    if not ta and tb:
        return _dot_impl(g, b, False, False), _dot_impl(g, a, True, False)
    if ta and not tb:
        return _dot_impl(b, g, False, True), _dot_impl(a, g, False, False)
    raise NotImplementedError


_bdot.defvjp(_bdot_fwd, _bdot_bwd)


_NN = (((1,), (0,)), ((), ()))
_NT = (((1,), (1,)), ((), ()))
_TN = (((0,), (0,)), ((), ()))


def _dg2(x, e, x_is_lhs, dims):
    hi = x.astype(BF16)
    lo = (x - hi.astype(F32)).astype(BF16)
    e = e.astype(BF16)
    if x_is_lhs:
        return (lax.dot_general(hi, e, dims, preferred_element_type=F32)
                + lax.dot_general(lo, e, dims, preferred_element_type=F32))
    return (lax.dot_general(e, hi, dims, preferred_element_type=F32)
            + lax.dot_general(e, lo, dims, preferred_element_type=F32))


_EDOT_FWD = {"xe": (True, _NN), "ex": (False, _NN), "ext": (False, _NT)}
_EDOT_BWD = {"xe": (True, _NT), "ex": (False, _TN), "ext": (True, _TN)}


@functools.partial(jax.custom_vjp, nondiff_argnums=(2,))
def _edot(x, e, mode):
    return _dg2(x, e, *_EDOT_FWD[mode])


def _edot_fwd(x, e, mode):
    return _dg2(x, e, *_EDOT_FWD[mode]), e


def _edot_bwd(mode, e, g):
    return _dg2(g, e, *_EDOT_BWD[mode]), None


_edot.defvjp(_edot_fwd, _edot_bwd)


def _shift_impl(u, pos, per, s):
    n = u.shape[0]
    rolled = u if s == 0 else pltpu.roll(u, (-s) % n, 0)
    ok = (pos + s >= 0) & (pos + s < per)
    return jnp.where(ok, rolled, 0.0)


def _rms(x, w):
    return x * lax.rsqrt(jnp.mean(x * x, axis=-1, keepdims=True) + EPS) * w


def _silu(x):
    return x * jax.nn.sigmoid(x)


def _softplus(x):
    return jnp.maximum(x, 0.0) + jnp.log(1.0 + jnp.exp(-jnp.abs(x)))


def _logsig(x):
    return jnp.minimum(x, 0.0) - jnp.log(1.0 + jnp.exp(-jnp.abs(x)))


def _tri(n, rev):
    t = lax.broadcasted_iota(jnp.int32, (n, n), 0)
    s = lax.broadcasted_iota(jnp.int32, (n, n), 1)
    return (s >= t) if rev else (t >= s)


def _head_expand(first_lane):
    l = lax.broadcasted_iota(jnp.int32, (128, SSM_INNER), 0)
    c = lax.broadcasted_iota(jnp.int32, (128, SSM_INNER), 1)
    return (l == first_lane + lax.shift_right_logical(c, 6)).astype(F32)


def _exchange_ops(x_ref, o_ref, send_sems, recv_sems, local_sem, a2a):
    mx, my, mc = lax.axis_index("x"), lax.axis_index("y"), lax.axis_index("c")
    me = 4 * mx + 2 * my + mc
    ops = []
    for k in range(1, N_DEV):
        px = 1 - mx if k & 4 else mx
        py = 1 - my if k & 2 else my
        pc = 1 - mc if k & 1 else mc
        ops.append(pltpu.make_async_remote_copy(
            src_ref=x_ref.at[4 * px + 2 * py + pc] if a2a else x_ref, dst_ref=o_ref.at[me],
            send_sem=send_sems.at[k - 1], recv_sem=recv_sems.at[k - 1],
            device_id=(px, py, pc), device_id_type=pl.DeviceIdType.MESH))
    ops.append(pltpu.make_async_copy(x_ref.at[me] if a2a else x_ref, o_ref.at[me], local_sem))
    return ops


def _call(body, *, grid, in_specs, out_specs, out_shape, scratch=(), sem, name, args, hosted=(), aliases=None):
    n_in, n_out, n_s, n_h = len(in_specs), len(out_specs), len(scratch), len(hosted)
    aliases = aliases or {}
    if not n_h:
        return pl.pallas_call(body, grid=grid, in_specs=list(in_specs), out_specs=list(out_specs),
                              out_shape=list(out_shape), scratch_shapes=list(scratch), input_output_aliases=aliases,
                              compiler_params=_cparams(sem), name=name)(*args)
    b0, c0 = n_in + n_h, n_in + n_h + n_out
    d0 = c0 + n_h

    def wrapped(*refs):
        hx, ho = refs[n_in:b0], refs[c0:d0]
        send_sems, recv_sems, local_sems = refs[d0 + n_s:]
        ids = [pl.program_id(ax) for ax in range(len(grid))]
        first = functools.reduce(jnp.logical_and, [i == 0 for i in ids])
        last = functools.reduce(jnp.logical_and, [i == n - 1 for i, n in zip(ids, grid)])

        def ops():
            return [op for n, (_, a2a) in enumerate(hosted)
                    for op in _exchange_ops(hx[n], ho[n], send_sems.at[n], recv_sems.at[n], local_sems.at[n], a2a)]

        @pl.when(first)
        def _():
            for op in ops():
                op.start()

        body(*refs[:n_in], *refs[b0:c0], *refs[d0:d0 + n_s])

        @pl.when(last)
        def _():
            for op in ops():
                op.wait()

    hbm = pl.BlockSpec(memory_space=pl.ANY)
    h_shapes = [jax.ShapeDtypeStruct((N_DEV,) + tuple(x.shape[1:] if a2a else x.shape), x.dtype) for x, a2a in hosted]
    return pl.pallas_call(
        wrapped, grid=grid, in_specs=list(in_specs) + [hbm] * n_h, out_specs=list(out_specs) + [hbm] * n_h,
        out_shape=list(out_shape) + h_shapes, input_output_aliases=aliases,
        scratch_shapes=list(scratch) + [pltpu.SemaphoreType.DMA((n_h, N_DEV - 1)), pltpu.SemaphoreType.DMA((n_h, N_DEV - 1)),
                                        pltpu.SemaphoreType.DMA((n_h,))],
        compiler_params=_cparams(("arbitrary",) * len(grid)), name=name,
    )(*args, *[x for x, _ in hosted])


def _mm(a, b, *, trans_b, out_dtype, name, tm_t=1408, tn_t=1024, tk_t=2816, hosted=()):
    M, K = a.shape
    N = b.shape[0] if trans_b else b.shape[1]
    tm, tn, tk = _pick(M, tm_t, 8), _pick(N, tn_t, 128), _pick(K, tk_t, 128)
    nk = K // tk
    dims = (((1,), (1,)), ((), ())) if trans_b else (((1,), (0,)), ((), ()))

    def body_one(a_ref, b_ref, o_ref):
        o_ref[...] = lax.dot_general(a_ref[...], b_ref[...], dims, preferred_element_type=F32).astype(out_dtype)

    def body_acc(a_ref, b_ref, o_ref, acc_ref):
        k = pl.program_id(2)
        p = lax.dot_general(a_ref[...], b_ref[...], dims, preferred_element_type=F32)

        @pl.when(k == 0)
        def _():
            acc_ref[...] = p

        @pl.when(k > 0)
        def _():
            acc_ref[...] += p

        @pl.when(k == nk - 1)
        def _():
            o_ref[...] = acc_ref[...].astype(out_dtype)

    b_spec = pl.BlockSpec((tn, tk), lambda j, i, k: (j, k)) if trans_b else pl.BlockSpec((tk, tn), lambda j, i, k: (k, j))
    outs = _call(
        body_one if nk == 1 else body_acc, grid=(N // tn, M // tm, nk),
        in_specs=[pl.BlockSpec((tm, tk), lambda j, i, k: (i, k)), b_spec],
        out_specs=[pl.BlockSpec((tm, tn), lambda j, i, k: (i, j))],
        out_shape=[jax.ShapeDtypeStruct((M, N), out_dtype)],
        scratch=[] if nk == 1 else [pltpu.VMEM((tm, tn), F32)],
        sem=("parallel", "parallel", "arbitrary"), name=name, args=(a, b), hosted=hosted)
    return outs if hosted else outs[0]


def _mm_tn(a, b, *, name, tm_t=2816, tr_t=1024, tn_t=1408):
    M, R = a.shape
    N = b.shape[1]
    tm, tr, tn = _pick(M, tm_t, 8), _pick(R, tr_t, 128), _pick(N, tn_t, 128)

    def body(a_ref, b_ref, o_ref):
        m = pl.program_id(2)
        p = lax.dot_general(a_ref[...], b_ref[...], (((0,), (0,)), ((), ())), preferred_element_type=F32)

        @pl.when(m == 0)
        def _():
            o_ref[...] = p

        @pl.when(m > 0)
        def _():
            o_ref[...] += p

    return pl.pallas_call(
        body, grid=(R // tr, N // tn, M // tm),
        in_specs=[pl.BlockSpec((tm, tr), lambda r, j, m: (m, r)), pl.BlockSpec((tm, tn), lambda r, j, m: (m, j))],
        out_specs=pl.BlockSpec((tr, tn), lambda r, j, m: (r, j)),
        out_shape=jax.ShapeDtypeStruct((R, N), F32),
        compiler_params=_cparams(("parallel", "parallel", "arbitrary")), name=name,
    )(a, b)


def _col(arr, tm, width=None, col=0):
    width = arr.shape[1] if width is None else width
    return (arr, (tm, width), lambda i: (i, col))


def _lat(arr, tm, nct):
    return (arr, (tm, arr.shape[1]), lambda i: (jnp.maximum(i - nct, 0), 0))


def _ctx(arr, tm, nct):
    return (arr, (tm, arr.shape[1]), lambda i: (jnp.minimum(i, nct - 1), 0))


def _whole(p):
    return pl.BlockSpec(p.shape, lambda i, nd=p.ndim: (0,) * nd)


def _stage_fwd(f, n_tiles, ins, params, outs, name):
    ni, npar = len(ins), len(params)
    o_specs = [pl.BlockSpec((o[1], o[3]), lambda i: (0, i)) if len(o) > 4 else pl.BlockSpec((o[3], o[1]), lambda i: (i, 0))
               for o in outs]
    o_shapes = [jax.ShapeDtypeStruct((o[1], o[0]) if len(o) > 4 else (o[0], o[1]), o[2]) for o in outs]

    def body(*refs):
        i = pl.program_id(0)
        xs = [r[...].astype(F32) for r in refs[:ni]]
        ps = [r[...] for r in refs[ni:ni + npar]]
        for r, v in zip(refs[ni + npar:], f(i, xs, ps, False)):
            r[...] = v.astype(r.dtype)

    return pl.pallas_call(
        body, grid=(n_tiles,),
        in_specs=[pl.BlockSpec(bs, fn) for _, bs, fn in ins] + [_whole(p) for p in params],
        out_specs=o_specs, out_shape=o_shapes,
        compiler_params=_cparams(("parallel",)), name=name,
    )(*[a for a, _, _ in ins], *params)


def _stage_bwd(f, n_tiles, ins, params, cts, ct_fn, dins, name, hosted=(), pack=None):
    ni, npar, nc = len(ins), len(params), len(cts)
    want = [k for k, d in enumerate(dins) if d is not None]
    extras = [dins[k][1] for k in want if dins[k][1] is not None]
    n_buf = 0 if pack is None or pack[1] is None else 1
    n_main_in = ni + npar + nc + len(extras)

    def body(*refs):
        i = pl.program_id(0)
        xs = [r[...].astype(F32) for r in refs[:ni]]
        ps = [r[...] for r in refs[ni:ni + npar]]
        ct_tiles = [r[...].astype(F32) for r in refs[ni + npar:ni + npar + nc]]
        ex_refs = list(refs[ni + npar + nc:n_main_in])
        out_refs = refs[n_main_in + n_buf:]
        _, vjp = jax.vjp(lambda xs_, ps_: tuple(f(i, xs_, ps_, True)), xs, ps)
        dxs, dps = vjp(tuple(ct_fn(i, ct_tiles)))
        for n, k in enumerate(want):
            v = dxs[k]
            if dins[k][1] is not None:
                v = v + ex_refs.pop(0)[...].astype(F32)
            out_refs[n][...] = v.astype(out_refs[n].dtype)
        if pack is not None:
            packed = [dxs[k] for k in pack[0]]
            r = out_refs[len(want) + npar]
            r[...] = (packed[0] if len(packed) == 1 else jnp.concatenate(packed, axis=1)).astype(r.dtype)
        for r, v in zip(out_refs[len(want):len(want) + npar], dps):
            @pl.when(i == 0)
            def _(r=r, v=v):
                r[...] = v

            @pl.when(i > 0)
            def _(r=r, v=v):
                r[...] += v

    in_specs = ([pl.BlockSpec(bs, fn) for _, bs, fn in ins] + [_whole(p) for p in params]
                + [pl.BlockSpec(bs, fn) for _, bs, fn in cts] + [pl.BlockSpec(bs, fn) for _, bs, fn in extras])
    out_specs = ([pl.BlockSpec(ins[k][1], lambda i, fn=ins[k][2]: (fn(i)[0], 0)) for k in want] + [_whole(p) for p in params])
    out_shape = ([jax.ShapeDtypeStruct((ins[k][0].shape[0], ins[k][1][1]), dins[k][0]) for k in want]
                 + [jax.ShapeDtypeStruct(p.shape, F32) for p in params])
    args = [*[a for a, _, _ in ins], *params, *[a for a, _, _ in cts], *[a for a, _, _ in extras]]
    aliases = None
    if pack is not None:
        ks, buf, total_cols, block = pack
        width = sum(ins[k][1][1] for k in ks)
        out_specs.append(pl.BlockSpec((ins[ks[0]][1][0], width), lambda i, fn=ins[ks[0]][2]: (fn(i)[0], block)))
        out_shape.append(jax.ShapeDtypeStruct((ins[ks[0]][0].shape[0], total_cols), BF16))
        if buf is not None:
            in_specs.append(pl.BlockSpec(memory_space=pl.ANY))
            args.append(buf)
            aliases = {len(args) - 1: len(out_specs) - 1}
    return _call(body, grid=(n_tiles,), in_specs=in_specs, out_specs=out_specs, out_shape=out_shape,
                 sem=("arbitrary",), name=name, args=args, hosted=hosted, aliases=aliases)


def _mod_rows(i, nct, ada, lo):
    return jnp.where(i < nct, ada[1:2, lo:lo + D], ada[0:1, lo:lo + D])


def _f_norm1(nct):
    def f(i, xs, ps, diff):
        ctx, x = xs
        nw, ada = ps
        xt = jnp.where(i < nct, ctx, x)
        return (_rms(xt, nw) * (1.0 + _mod_rows(i, nct, ada, D)) + _mod_rows(i, nct, ada, 0),)
    return f


def _f_gates_dt(i, xs, ps, diff):
    small, xc = xs
    up_f, up_b, gb_f, gb_b, dtb, alog = ps
    dot = _bdot if diff else _dot_impl_nn
    lg_f = _logsig(dot(small, up_f) + gb_f) * (1.0 / GLA_TAU)
    lg_b = _logsig(dot(small, up_b) + gb_b) * (1.0 / GLA_TAU)
    dtp = _softplus(small + dtb)
    aa = -jnp.exp(alog) * dtp
    e_f, e_b = _head_expand(32), _head_expand(64)
    n = small.shape[0]
    t = lax.broadcasted_iota(jnp.int32, (n, n), 0)
    s = lax.broadcasted_iota(jnp.int32, (n, n), 1)
    same = lax.shift_right_logical(t, 6) == lax.shift_right_logical(s, 6)
    pre_g, suf_g = ((t >= s) & same).astype(F32), ((s >= t) & same).astype(F32)
    pre_s, suf_s = (t >= s).astype(F32), (s >= t).astype(F32)
    cum_f, cum_b = _edot(aa, pre_s, "ex"), _edot(aa, suf_s, "ex")
    h = lax.broadcasted_iota(jnp.int32, (SSM_G * SSM_HPG, 128), 0)
    l = lax.broadcasted_iota(jnp.int32, (SSM_G * SSM_HPG, 128), 1)
    cum_ft = _edot(cum_f, (l == h + 32).astype(F32), "ext")
    cum_bt = _edot(cum_b, (l == h + 64).astype(F32), "ext")
    return (_edot(lg_f, pre_g, "ex"), _edot(lg_b, suf_g, "ex"), xc * _bdot(dtp, e_f), xc * _bdot(dtp, e_b),
            _edot(cum_f, e_f, "xe"), _edot(cum_b, e_b, "xe"), cum_ft, cum_bt)


def _dot_impl_nn(a, b):
    return _dot_impl(a, b, False, False)


def _f_conv(nct, tc):
    def f(i, xs, ps, diff):
        (u,) = xs
        cw, cb = ps
        n = u.shape[0]
        t = lax.broadcasted_iota(jnp.int32, (n, 1), 0)
        per = jnp.where(i < nct, tc, GRID_W)
        pos = jnp.bitwise_and(t, per - 1)
        acc = cb + _shift_impl(u, pos, per, -2) * cw[0:1]
        for j in range(1, 4):
            acc = acc + _shift_impl(u, pos, per, j - 2) * cw[j:j + 1]
        return (_silu(acc), acc)
    return f


def _conv_bwd(proj, pre, cts, cw, buf, nct, tc, name):
    A = proj.shape[0]
    width = cw.shape[1]

    def body(u_ref, p_ref, c0, c1, c2, c3, cw_ref, buf_ref, dcw_ref, dcb_ref, o_ref):
        i = pl.program_id(0)
        t = lax.broadcasted_iota(jnp.int32, (TM, 1), 0)
        per = jnp.where(i < nct, tc, GRID_W)
        pos = jnp.bitwise_and(t, per - 1)
        pre_ = p_ref[...].astype(F32)
        s = jax.nn.sigmoid(pre_)
        g = jnp.concatenate([c0[...].astype(F32) + c1[...].astype(F32), c2[...].astype(F32), c3[...].astype(F32)], axis=1)
        g = g * (s * (1.0 + pre_ * (1.0 - s)))
        u = u_ref[...].astype(F32)
        w_ = cw_ref[...]
        du, rows = None, []
        for j in range(4):
            gj = g if j == 2 else _shift_impl(g, pos, per, 2 - j)
            du = gj * w_[j:j + 1] if du is None else du + gj * w_[j:j + 1]
            rows.append(jnp.sum(gj * u, axis=0, keepdims=True))
        o_ref[...] = du.astype(o_ref.dtype)
        for r, v in ((dcw_ref, jnp.concatenate(rows, axis=0)), (dcb_ref, jnp.sum(g, axis=0, keepdims=True))):
            @pl.when(i == 0)
            def _(r=r, v=v):
                r[...] = v

            @pl.when(i > 0)
            def _(r=r, v=v):
                r[...] += v

    blk = C_U // width
    tile = lambda a: pl.BlockSpec((TM, a.shape[1]), lambda i: (i, 0))
    return _call(
        body, grid=(A // TM,),
        in_specs=[pl.BlockSpec((TM, width), lambda i: (i, blk)), tile(pre)] + [tile(c) for c in cts]
        + [_whole(cw), pl.BlockSpec(memory_space=pl.ANY)],
        out_specs=[_whole(cw), pl.BlockSpec((1, width), lambda i: (0, 0)), pl.BlockSpec((TM, width), lambda i: (i, blk))],
        out_shape=[jax.ShapeDtypeStruct(cw.shape, F32), jax.ShapeDtypeStruct((1, width), F32),
                   jax.ShapeDtypeStruct(buf.shape, buf.dtype)],
        sem=("arbitrary",), name=name, args=(proj, pre, *cts, cw, buf), aliases={7: 2})


def _f_gla_post(i, xs, ps, diff):
    ogf, ogb, r = xs
    (gw,) = ps
    o = ogf + ogb
    parts = [_rms(o[:, h * GLA_DV:(h + 1) * GLA_DV], gw) for h in range(GLA_H)]
    return (jnp.concatenate(parts, axis=1) * _silu(r),)


def _f_ssd_post(i, xs, ps, diff):
    yf, yb, xc, z = xs
    dsk8, nw = ps
    dsk = _edot(dsk8, _head_expand(0), "xe")[0:1]
    y = (yf + yb + dsk * xc) * _silu(z)
    w = SSM_INNER // SSM_G
    parts = [_rms(y[:, g * w:(g + 1) * w], nw[:, g * w:(g + 1) * w]) for g in range(SSM_G)]
    return (jnp.concatenate(parts, axis=1),)


def _f_merge(i, xs, ps, diff):
    ga, gb, ya, yb = xs
    return (jax.nn.sigmoid(ga) * ya + jax.nn.sigmoid(gb) * yb,)


def _f_res1(nct):
    def f(i, xs, ps, diff):
        ctx, x, mix = xs
        ada, nw = ps
        h2 = jnp.where(i < nct, ctx, x) + _mod_rows(i, nct, ada, 2 * D) * mix
        return (h2, _rms(h2, nw) * (1.0 + _mod_rows(i, nct, ada, 4 * D)) + _mod_rows(i, nct, ada, 3 * D))
    return f


def _f_swiglu(i, xs, ps, diff):
    (gu,) = xs
    return (_silu(gu[:, :D_FF]) * gu[:, D_FF:],)


def _loss_head(h2, dn, tgt, ada, fw, nct):
    A = h2.shape[0]
    n_tiles = A // TM

    def tile_loss(i, h2t, dnt, tg, ada_, fw_):
        h3 = h2t + ada_[0:1, 5 * D:6 * D] * dnt
        err = _rms(h3, fw_) - tg
        row = 0.5 * jnp.mean(err * err, axis=-1, keepdims=True)
        return jnp.sum(row, axis=0, keepdims=True) * jnp.where(i < nct, 0.0, 1.0)

    def body(h2_ref, dn_ref, tg_ref, ada_ref, fw_ref, loss_ref, dh_ref, ddn_ref, dada_ref, dfw_ref):
        i = pl.program_id(0)
        val, vjp = jax.vjp(functools.partial(tile_loss, i), h2_ref[...], dn_ref[...].astype(F32), tg_ref[...], ada_ref[...],
                           fw_ref[...])
        dh, ddn, _, dada, dfw = vjp(jnp.ones((1, 1), F32))
        dh_ref[...] = dh
        ddn_ref[...] = ddn.astype(BF16)
        lv = jnp.broadcast_to(val, loss_ref.shape)
        for r, v in ((loss_ref, lv), (dada_ref, dada), (dfw_ref, dfw)):
            @pl.when(i == 0)
            def _(r=r, v=v):
                r[...] = v

            @pl.when(i > 0)
            def _(r=r, v=v):
                r[...] += v

    row = lambda i: (i, 0)
    return pl.pallas_call(
        body, grid=(n_tiles,),
        in_specs=[pl.BlockSpec((TM, D), row), pl.BlockSpec((TM, D), row),
                  pl.BlockSpec((TM, D), lambda i: (jnp.maximum(i - nct, 0), 0)), _whole(ada), _whole(fw)],
        out_specs=[pl.BlockSpec((8, 128), lambda i: (0, 0)), pl.BlockSpec((TM, D), row), pl.BlockSpec((TM, D), row),
                   _whole(ada), _whole(fw)],
        out_shape=[jax.ShapeDtypeStruct((8, 128), F32), jax.ShapeDtypeStruct((A, D), F32),
                   jax.ShapeDtypeStruct((A, D), BF16), jax.ShapeDtypeStruct(ada.shape, F32),
                   jax.ShapeDtypeStruct(fw.shape, F32)],
        compiler_params=_cparams(("arbitrary",)), name="loss_head",
    )(h2, dn, tgt, ada, fw)


def _chunk_of(step, n_chunks, n_ctx_chunks, rev):
    if not rev:
        return step
    return jnp.where(step < n_ctx_chunks, n_ctx_chunks - 1 - step, n_chunks - 1 - (step - n_ctx_chunks))


def _gla_step(st, q, k, v, b, rev):
    tri = _tri(GLA_C, rev).astype(F32)
    last = 0 if rev else GLA_C - 1
    outs, sts = [], []
    for h in range(GLA_H):
        kk = slice(h * GLA_DK, (h + 1) * GLA_DK)
        vv = slice(h * GLA_DV, (h + 1) * GLA_DV)
        qh, kh, vh, bh, sh = q[:, kk] * (GLA_DK ** -0.5), k[:, kk], v[:, vv], b[:, kk], st[vv]
        tot = bh[last:last + 1]
        mid = bh[GLA_C // 2:GLA_C // 2 + 1]
        att = _bdot(qh * jnp.exp(bh - mid), kh * jnp.exp(mid - bh), False, True) * tri
        outs.append(_bdot(att, vh) + _bdot(qh * jnp.exp(bh), sh, False, True))
        sts.append(sh * jnp.exp(tot) + _bdot(vh, kh * jnp.exp(tot - bh), True, False))
    return jnp.concatenate(sts, axis=0), jnp.concatenate(outs, axis=1)


_GQK, _GV = GLA_H * GLA_DK, GLA_H * GLA_DV


GLA_CPS = 4
_GB = GLA_C * GLA_CPS


def _gla_order(rev):
    return list(reversed(range(GLA_CPS))) if rev else list(range(GLA_CPS))


def _gla_fwd(proj, bg_f, bg_b, ncc, name):
    A = proj.shape[0]
    nb = A // _GB
    chs = [lambda s, rev=rev: _chunk_of(s, nb, ncc // GLA_CPS, rev) for rev in (False, True)]

    def body(*refs):
        ins, outs, sts = refs[:8], refs[8:12], refs[12:]

        for d, rev in enumerate((False, True)):
            q_ref, k_ref, v_ref, b_ref = ins[4 * d:4 * d + 4]
            o_ref, ss_ref = outs[2 * d:2 * d + 2]
            st_ref = sts[d]

            @pl.when(pl.program_id(0) == 0)
            def _(st_ref=st_ref):
                st_ref[...] = jnp.zeros_like(st_ref)

            st = st_ref[...]
            for pos, sub in enumerate(_gla_order(rev)):
                rows = slice(sub * GLA_C, (sub + 1) * GLA_C)
                ss_ref[0, pos] = st
                st, o = _gla_step(st, q_ref[rows].astype(F32), k_ref[rows].astype(F32), v_ref[rows].astype(F32),
                                  b_ref[rows], rev)
                o_ref[rows] = o.astype(o_ref.dtype)
            st_ref[...] = st

    in_specs, out_specs = [], []
    for ch in chs:
        in_specs += [pl.BlockSpec((_GB, _GQK), lambda s, ch=ch: (ch(s), C_Q // _GQK)),
                     pl.BlockSpec((_GB, _GQK), lambda s, ch=ch: (ch(s), C_K // _GQK)),
                     pl.BlockSpec((_GB, _GV), lambda s, ch=ch: (ch(s), C_V // _GV)),
                     pl.BlockSpec((_GB, _GQK), lambda s, ch=ch: (ch(s), 0))]
        out_specs += [pl.BlockSpec((_GB, _GV), lambda s, ch=ch: (ch(s), 0)),
                      pl.BlockSpec((1, GLA_CPS, _GV, GLA_DK), lambda s: (s, 0, 0, 0))]
    return pl.pallas_call(
        body, grid=(nb,), in_specs=in_specs, out_specs=out_specs,
        out_shape=[jax.ShapeDtypeStruct((A, _GV), BF16), jax.ShapeDtypeStruct((nb, GLA_CPS, _GV, GLA_DK), F32)] * 2,
        scratch_shapes=[pltpu.VMEM((_GV, GLA_DK), F32)] * 2,
        compiler_params=_cparams(("arbitrary",)), name=name,
    )(proj, proj, proj, bg_f, proj, proj, proj, bg_b)


def _gla_bwd(proj, lg, saved, d_o, prev, ncc, rev, name, pack=None):
    A = proj.shape[0]
    nb = A // _GB
    st_of = lambda r: nb - 1 - r
    ch = lambda r: _chunk_of(st_of(r), nb, ncc // GLA_CPS, rev)
    qs = pl.BlockSpec((_GB, _GQK), lambda r: (ch(r), 0))
    vs = pl.BlockSpec((_GB, _GV), lambda r: (ch(r), 0))
    n_prev = 0 if prev is None else 3

    n_buf = 0 if pack is None else 1

    def body(*refs):
        q_ref, k_ref, v_ref, b_ref, ss_ref, do_ref = refs[:6]
        p_refs = refs[6:6 + n_prev]
        out_refs = refs[6 + n_prev + n_buf:]
        ds_ref = out_refs[-1]

        @pl.when(pl.program_id(0) == 0)
        def _():
            ds_ref[...] = jnp.zeros_like(ds_ref)

        ds = ds_ref[...]
        for pos, sub in reversed(list(enumerate(_gla_order(rev)))):
            rows = slice(sub * GLA_C, (sub + 1) * GLA_C)
            _, vjp = jax.vjp(functools.partial(_gla_step, rev=rev), ss_ref[0, pos], q_ref[rows].astype(F32),
                             k_ref[rows].astype(F32), v_ref[rows].astype(F32), b_ref[rows])
            ds, dq, dk, dv, db = vjp((ds, do_ref[rows].astype(F32)))
            if n_prev:
                dq, dk, dv = [d + p[rows].astype(F32) for d, p in zip((dq, dk, dv), p_refs)]
            if pack is None:
                for r, d in zip(out_refs[:4], (dq, dk, dv, db)):
                    r[rows] = d.astype(r.dtype)
            else:
                out_refs[0][rows] = jnp.concatenate([dq, dk, dv], axis=1).astype(out_refs[0].dtype)
                out_refs[1][rows] = db
        ds_ref[...] = ds

    in_specs = [
        pl.BlockSpec((_GB, _GQK), lambda r: (ch(r), C_Q // _GQK)),
        pl.BlockSpec((_GB, _GQK), lambda r: (ch(r), C_K // _GQK)),
        pl.BlockSpec((_GB, _GV), lambda r: (ch(r), C_V // _GV)),
        qs,
        pl.BlockSpec((1, GLA_CPS, _GV, GLA_DK), lambda r: (st_of(r), 0, 0, 0)),
        vs,
    ] + ([qs, qs, vs] if n_prev else [])
    args = [proj, proj, proj, lg, saved, d_o, *(prev or ())]
    if pack is None:
        out_specs = [qs, qs, vs, qs]
        out_shape = [jax.ShapeDtypeStruct((A, _GQK), BF16), jax.ShapeDtypeStruct((A, _GQK), BF16),
                     jax.ShapeDtypeStruct((A, _GV), BF16), jax.ShapeDtypeStruct((A, _GQK), F32)]
        aliases = None
    else:
        in_specs.append(pl.BlockSpec(memory_space=pl.ANY))
        args.append(pack[0])
        out_specs = [pl.BlockSpec((_GB, 2 * _GQK + _GV), lambda r: (ch(r), 0)), qs]
        out_shape = [jax.ShapeDtypeStruct((A, pack[1]), BF16), jax.ShapeDtypeStruct((A, _GQK), F32)]
        aliases = {len(args) - 1: 0}
    return _call(body, grid=(nb,), in_specs=in_specs, out_specs=out_specs, out_shape=out_shape,
                 scratch=[pltpu.VMEM((_GV, GLA_DK), F32)], sem=("arbitrary",), name=name, args=args, aliases=aliases)


def _ssd_step(st, x, bm, cm, cum, cum_t, rev):
    mask = _tri(SSM_C, rev)
    last = 0 if rev else SSM_C - 1
    tot = cum[last:last + 1]
    cb = _bdot(cm, bm, False, True)
    ys = []
    for e in range(SSM_HPG):
        lm = jnp.exp(jnp.where(mask, cum[:, e * SSM_P:e * SSM_P + 1] - cum_t[e:e + 1], -jnp.inf))
        ys.append(_bdot(cb * lm, x[:, e * SSM_P:(e + 1) * SSM_P]))
    y = jnp.concatenate(ys, axis=1) + _bdot(cm, st) * jnp.exp(cum)
    st_new = st * jnp.exp(tot) + _bdot(bm, x * jnp.exp(tot - cum), True, False)
    return st_new, y


SSM_GPS = 4
_W = SSM_HPG * SSM_P
_XBC_B = SSM_INNER // (SSM_N * SSM_GPS)
_XBC_C = _XBC_B + SSM_G // SSM_GPS


def _ssd_steps(st, x, bm, cm, cum, cum_t, rev):
    outs = [_ssd_step(st[j * SSM_N:(j + 1) * SSM_N], x[:, j * _W:(j + 1) * _W], bm[:, j * SSM_N:(j + 1) * SSM_N],
                      cm[:, j * SSM_N:(j + 1) * SSM_N], cum[:, j * _W:(j + 1) * _W],
                      cum_t[j * SSM_HPG:(j + 1) * SSM_HPG], rev) for j in range(SSM_GPS)]
    return jnp.concatenate([o[0] for o in outs], axis=0), jnp.concatenate([o[1] for o in outs], axis=1)


SSM_CPS = 2
_SB = SSM_C * SSM_CPS


def _ssd_order(rev):
    return list(reversed(range(SSM_CPS))) if rev else list(range(SSM_CPS))


def _ssd_fwd(x_f, x_b, xbc, cum_f, cum_b, cum_tf, cum_tb, ncc, name, hosted=()):
    A = x_f.shape[0]
    nb = A // _SB
    ng = SSM_G // SSM_GPS
    chs = [lambda s, rev=rev: _chunk_of(s, nb, ncc // SSM_CPS, rev) for rev in (False, True)]

    def body(*refs):
        ins, outs, sts = refs[:10], refs[10:14], refs[14:]
        for d, rev in enumerate((False, True)):
            x_ref, b_ref, c_ref, a_ref, at_ref = ins[5 * d:5 * d + 5]
            y_ref, ss_ref = outs[2 * d:2 * d + 2]
            st_ref = sts[d]

            @pl.when(pl.program_id(1) == 0)
            def _(st_ref=st_ref):
                st_ref[...] = jnp.zeros_like(st_ref)

            st = st_ref[...]
            for pos, sub in enumerate(_ssd_order(rev)):
                rows = slice(sub * SSM_C, (sub + 1) * SSM_C)
                ss_ref[0, 0, pos] = st
                st, y = _ssd_steps(st, x_ref[rows].astype(F32), b_ref[rows].astype(F32), c_ref[rows].astype(F32),
                                   a_ref[rows], at_ref[:, rows], rev)
                y_ref[rows] = y.astype(y_ref.dtype)
            st_ref[...] = st

    in_specs, out_specs = [], []
    for ch in chs:
        gs = pl.BlockSpec((_SB, _W * SSM_GPS), lambda g, s, ch=ch: (ch(s), g))
        in_specs += [gs, pl.BlockSpec((_SB, SSM_N * SSM_GPS), lambda g, s, ch=ch: (ch(s), _XBC_B + g)),
                     pl.BlockSpec((_SB, SSM_N * SSM_GPS), lambda g, s, ch=ch: (ch(s), _XBC_C + g)), gs,
                     pl.BlockSpec((SSM_HPG * SSM_GPS, _SB), lambda g, s, ch=ch: (g, ch(s)))]
        out_specs += [gs, pl.BlockSpec((1, 1, SSM_CPS, SSM_N * SSM_GPS, _W), lambda g, s: (g, s, 0, 0, 0))]
    return _call(
        body, grid=(ng, nb), in_specs=in_specs, out_specs=out_specs,
        out_shape=[jax.ShapeDtypeStruct((A, SSM_INNER), BF16),
                   jax.ShapeDtypeStruct((ng, nb, SSM_CPS, SSM_N * SSM_GPS, _W), F32)] * 2,
        scratch=[pltpu.VMEM((SSM_N * SSM_GPS, _W), F32)] * 2,
        sem=("parallel", "arbitrary"), name=name,
        args=(x_f, xbc, xbc, cum_f, cum_tf, x_b, xbc, xbc, cum_b, cum_tb), hosted=hosted)


def _ssd_bwd(xd, xbc, cum, cum_t, saved, d_y, prev, ncc, rev, name):
    A = xd.shape[0]
    nc = A // SSM_C
    ng = SSM_G // SSM_GPS
    st_of = lambda r: nc - 1 - r
    ch = lambda r: _chunk_of(st_of(r), nc, ncc, rev)
    gs = pl.BlockSpec((SSM_C, _W * SSM_GPS), lambda g, r: (ch(r), g))
    ns = pl.BlockSpec((SSM_C, SSM_N * SSM_GPS), lambda g, r: (ch(r), g))
    ts = pl.BlockSpec((SSM_HPG * SSM_GPS, SSM_C), lambda g, r: (g, ch(r)))
    n_prev = 0 if prev is None else 2

    def body(*refs):
        x_ref, b_ref, c_ref, a_ref, at_ref, ss_ref, dy_ref = refs[:7]
        p_refs = refs[7:7 + n_prev]
        dx_ref, db_ref, dc_ref, da_ref, dat_ref, ds_ref = refs[7 + n_prev:]

        @pl.when(pl.program_id(1) == 0)
        def _():
            ds_ref[...] = jnp.zeros_like(ds_ref)

        _, vjp = jax.vjp(functools.partial(_ssd_steps, rev=rev), ss_ref[0, 0, 0], x_ref[...].astype(F32),
                         b_ref[...].astype(F32), c_ref[...].astype(F32), a_ref[...], at_ref[...])
        ds, dx, db, dc, da, dat = vjp((ds_ref[...], dy_ref[...].astype(F32)))
        if n_prev:
            db, dc = db + p_refs[0][...].astype(F32), dc + p_refs[1][...].astype(F32)
        for r, d in zip((dx_ref, db_ref, dc_ref, da_ref, dat_ref), (dx, db, dc, da, dat)):
            r[...] = d.astype(r.dtype)
        ds_ref[...] = ds

    in_specs = [gs, pl.BlockSpec((SSM_C, SSM_N * SSM_GPS), lambda g, r: (ch(r), _XBC_B + g)),
                pl.BlockSpec((SSM_C, SSM_N * SSM_GPS), lambda g, r: (ch(r), _XBC_C + g)), gs, ts,
                pl.BlockSpec((1, 1, 1, SSM_N * SSM_GPS, _W),
                             lambda g, r: (g, st_of(r) // SSM_CPS, st_of(r) % SSM_CPS, 0, 0)), gs]
    in_specs += [ns, ns] if n_prev else []
    return pl.pallas_call(
        body, grid=(ng, nc), in_specs=in_specs, out_specs=[gs, ns, ns, gs, ts],
        out_shape=[jax.ShapeDtypeStruct((A, SSM_INNER), BF16), jax.ShapeDtypeStruct((A, SSM_G * SSM_N), BF16),
                   jax.ShapeDtypeStruct((A, SSM_G * SSM_N), BF16), jax.ShapeDtypeStruct((A, SSM_INNER), F32),
                   jax.ShapeDtypeStruct((SSM_G * SSM_HPG, A), F32)],
        scratch_shapes=[pltpu.VMEM((SSM_N * SSM_GPS, _W), F32)],
        compiler_params=_cparams(("parallel", "arbitrary")), name=name,
    )(xd, xbc, xbc, cum, cum_t, saved, d_y, *(prev or ()))


_ADA_ROWS = 16


def _ada_fwd(cc, w_shard, b_shard):
    n = w_shard.shape[1]

    def body(cc_ref, w_ref, b_ref, o_ref):
        o_ref[...] = _dot_impl(_silu(cc_ref[...]), w_ref[...], False, False) + b_ref[...]

    return pl.pallas_call(body, out_shape=jax.ShapeDtypeStruct((_ADA_ROWS, n), F32),
                          compiler_params=_cparams(()), name="ada_fwd")(cc, w_shard, b_shard)


def _ada_bwd(cc, w_shard, d_all, d_mine):
    n = w_shard.shape[1]

    def rows(ref, r):
        parts = []
        for s in range(N_DEV):
            parts.append(ref[6 * s + r:6 * s + r + 1] + ref[6 * s + 2 + r:6 * s + 3 + r] + ref[6 * s + 4 + r:6 * s + 5 + r])
        return parts

    def total(parts):
        t = parts[0]
        for p in parts[1:]:
            t = t + p
        return t

    def body(cc_ref, w_ref, da_ref, dm_ref, dw_ref, db_ref, dcc_ref):
        dd = jnp.concatenate(rows(dm_ref, 0) + [total(rows(dm_ref, 1)), jnp.zeros((_ADA_ROWS - N_DEV - 1, n), F32)], axis=0)
        cc = cc_ref[...]
        dw_ref[...] = _dot_impl(_silu(cc), dd, True, False)
        db_ref[...] = total(rows(da_ref, 0)) + total(rows(da_ref, 1))
        _, vjp = jax.vjp(_silu, cc)
        dcc_ref[...] = vjp(_dot_impl(dd, w_ref[...], False, True))[0]

    return pl.pallas_call(
        body, out_shape=[jax.ShapeDtypeStruct((D, n), F32), jax.ShapeDtypeStruct((1, d_all.shape[1]), F32),
                         jax.ShapeDtypeStruct((_ADA_ROWS, D), F32)],
        compiler_params=_cparams(()), name="ada_bwd")(cc, w_shard, d_all, d_mine)


def _local_step(x, ctx, target, ada, w, shards):
    T, Tc = x.shape[0], ctx.shape[0]
    assert Tc == TM and T % TM == 0 and GRID_W == GLA_C and TS == SSM_C == 2 * GLA_C
    A = T + Tc
    nct = Tc // TM
    n_tm, n_ts = A // TM, A // TS
    tl = _pick(A, 3 * TM, TM)
    n_tl = A // tl
    g = {}

    x_in = [_ctx(ctx, TM, nct), _lat(x, TM, nct)]
    f_norm1 = _f_norm1(nct)
    p_norm1 = [w["norm1_w"], ada]
    (h1,) = _stage_fwd(f_norm1, n_tm, x_in, p_norm1, [(A, D, BF16, TM)], "norm1")
    proj, ag_pa, ag_pb, ag_out = _mm(h1, w["wt_in"], trans_b=True, out_dtype=BF16, name="mm_in", tn_t=1152, tm_t=2816,
                                     hosted=[(shards[n], False) for n in ("w_pa", "w_pb", "w_out")])
    small = _mm(h1, w["wt_in"][C_S:], trans_b=True, out_dtype=F32, name="mm_in_small")

    p_gd = [w["up_f"], w["up_b"], w["gla_bias_f"], w["gla_bias_b"], w["dtb"], w["alog"]]
    f_conv = _f_conv(nct, Tc)
    p_conv = [w["conv_w"], w["conv_b"]]
    in_conv = [_col(proj, TM, 3072, C_U // 3072)]
    xbc, conv_pre = _stage_fwd(f_conv, n_tm, in_conv, p_conv, [(A, 3072, BF16, TM)] * 2, "conv")
    in_gd = [_col(small, TS), _col(xbc, TS, SSM_INNER, 0)]
    n_heads = SSM_G * SSM_HPG
    lg_f, lg_b, xf, xb, axf, axb, atf, atb = _stage_fwd(
        _f_gates_dt, n_ts, in_gd, p_gd,
        [(A, 512, F32, TS)] * 2 + [(A, SSM_INNER, BF16, TS)] * 2 + [(A, SSM_INNER, F32, TS)] * 2
        + [(A, n_heads, F32, TS, True)] * 2,
        "gates_dt")

    ncc_g, ncc_s = Tc // GLA_C, Tc // SSM_C
    ogf, sv_gf, ogb, sv_gb = _gla_fwd(proj, lg_f, lg_b, ncc_g, "gla_fb")
    ysf, sv_sf, ysb, sv_sb, ag_gate, ag_up, ag_down = _ssd_fwd(
        xf, xb, xbc, axf, axb, atf, atb, ncc_s, "ssd_fb", hosted=[(shards[n], False) for n in ("w_gate", "w_up", "w_down")])
    w = dict(w, w_pa=ag_pa.reshape(D, D), w_pb=ag_pb.reshape(SSM_INNER, D), w_out=ag_out.reshape(D, D),
             wt_gu=jnp.concatenate([ag_gate.reshape(D_FF, D), ag_up.reshape(D_FF, D)], axis=0), w_down=ag_down.reshape(D_FF, D))
    got = {}

    in_gp = [_col(ogf, tl), _col(ogb, tl), _col(proj, tl, 1024, C_R // 1024)]
    p_gp = [w["gla_norm_w"]]
    (oa,) = _stage_fwd(_f_gla_post, n_tl, in_gp, p_gp, [(A, D, BF16, tl)], "gla_post")
    in_sp = [_col(ysf, TM), _col(ysb, TM), _col(xbc, TM, SSM_INNER, 0), _col(proj, TM, SSM_INNER, C_Z // SSM_INNER)]
    p_sp = [w["dsk8"], w["ssm_norm_w"]]
    (ob,) = _stage_fwd(_f_ssd_post, n_tm, in_sp, p_sp, [(A, SSM_INNER, BF16, TM)], "ssd_post")
    ya = _mm(oa, w["w_pa"], trans_b=False, out_dtype=BF16, name="mm_pa")
    yb = _mm(ob, w["w_pb"], trans_b=False, out_dtype=BF16, name="mm_pb")
    in_mg = [_col(proj, tl, 1024, C_GA // 1024), _col(proj, tl, 1024, C_GB // 1024), _col(ya, tl), _col(yb, tl)]
    (merged,) = _stage_fwd(_f_merge, n_tl, in_mg, [], [(A, D, BF16, tl)], "merge")
    mix = _mm(merged, w["w_out"], trans_b=False, out_dtype=BF16, name="mm_out")

    f_res1 = _f_res1(nct)
    in_r1 = x_in + [_col(mix, TM)]
    p_r1 = [ada, w["norm2_w"]]
    h2, hm2 = _stage_fwd(f_res1, n_tm, in_r1, p_r1, [(A, D, F32, TM), (A, D, BF16, TM)], "res1")
    gu = _mm(hm2, w["wt_gu"], trans_b=True, out_dtype=BF16, name="mm_gu", tn_t=1408)
    in_sw = [_col(gu, TM)]
    (act,) = _stage_fwd(_f_swiglu, n_tm, in_sw, [], [(A, D_FF, BF16, TM)], "swiglu")
    dn = _mm(act, w["w_down"], trans_b=False, out_dtype=BF16, name="mm_down")

    loss_blk, d_h2a, d_dn, d_ada3, g["final_norm_w"] = _loss_head(h2, dn, target, ada, w["final_norm_w"], nct)
    rows16 = lambda gw: (gw.reshape(N_DEV, gw.shape[0] // N_DEV, gw.shape[1]).astype(BF16), True)
    gw_down = _mm_tn(act, d_dn, name="dw_down", tr_t=1408)
    d_act = _mm(d_dn, w["w_down"], trans_b=True, out_dtype=BF16, name="dx_down", tn_t=1408)
    d_gu, got["w_down"] = _stage_bwd(_f_swiglu, n_tm, in_sw, [], [_col(d_act, TM)], lambda i, t: t, [(BF16, None)],
                                     "swiglu_b", hosted=[rows16(gw_down)])
    gwt_gu = _mm_tn(d_gu, hm2, name="dw_gu", tr_t=1408)
    d_hm2 = _mm(d_gu, w["wt_gu"], trans_b=False, out_dtype=BF16, name="dx_gu")
    d_x1, d_mix, d_ada2, g["norm2_w"], got["w_gate"] = _stage_bwd(
        f_res1, n_tm, in_r1, p_r1, [_col(d_h2a, TM), _col(d_hm2, TM)], lambda i, t: t,
        [None, (F32, None), (BF16, None)], "res1_b", hosted=[rows16(gwt_gu[:D_FF])])
    gw_out = _mm_tn(merged, d_mix, name="dw_out")
    d_merged = _mm(d_mix, w["w_out"], trans_b=True, out_dtype=BF16, name="dx_out")
    d_ya, d_yb, d_proj, got["w_up"] = _stage_bwd(
        _f_merge, n_tl, in_mg, [], [_col(d_merged, tl)], lambda i, t: t, [None, None, (BF16, None), (BF16, None)], "merge_b",
        hosted=[rows16(gwt_gu[D_FF:])], pack=([0, 1], None, D_INP, C_GA // (2 * D)))
    gw_pa = _mm_tn(oa, d_ya, name="dw_pa")
    gw_pb = _mm_tn(ob, d_yb, name="dw_pb")
    d_oa = _mm(d_ya, w["w_pa"], trans_b=True, out_dtype=BF16, name="dx_pa")
    d_ob = _mm(d_yb, w["w_pb"], trans_b=True, out_dtype=BF16, name="dx_pb")
    d_og, g["gla_norm_w"], d_proj, got["w_out"] = _stage_bwd(
        _f_gla_post, n_tl, in_gp, p_gp, [_col(d_oa, tl)], lambda i, t: t, [(BF16, None), None, None], "gla_post_b",
        hosted=[rows16(gw_out)], pack=([2], d_proj, D_INP, C_R // D))
    d_ys, d_xs_skip, g["dsk8"], g["ssm_norm_w"], d_proj, got["w_pa"], got["w_pb"] = _stage_bwd(
        _f_ssd_post, n_tm, in_sp, p_sp, [_col(d_ob, TM)], lambda i, t: t,
        [(BF16, None), None, (BF16, None), None], "ssd_post_b", hosted=[rows16(gw_pa), rows16(gw_pb)],
        pack=([3], d_proj, D_INP, C_Z // SSM_INNER))

    dq, dk, dv, d_lgf = _gla_bwd(proj, lg_f, sv_gf, d_og, None, ncc_g, False, "gla_f_b")
    d_proj, d_lgb = _gla_bwd(proj, lg_b, sv_gb, d_og, (dq, dk, dv), ncc_g, True, "gla_b_b", pack=(d_proj, D_INP))
    d_xf, d_bm, d_cm, d_axf, d_atf = _ssd_bwd(xf, xbc, axf, atf, sv_sf, d_ys, None, ncc_s, False, "ssd_f_b")
    d_xb, d_bm, d_cm, d_axb, d_atb = _ssd_bwd(xb, xbc, axb, atb, sv_sb, d_ys, (d_bm, d_cm), ncc_s, True, "ssd_b_b")

    cts_gd = [_col(a, TS) for a in (d_lgf, d_lgb, d_xf, d_xb, d_axf, d_axb)]
    cts_gd += [(a, (n_heads, TS), lambda i: (0, i)) for a in (d_atf, d_atb)]
    d_xs_dt, g["up_f"], g["up_b"], g["gla_bias_f"], g["gla_bias_b"], g["dtb"], g["alog"], d_proj = _stage_bwd(
        _f_gates_dt, n_ts, in_gd, p_gd, cts_gd, lambda i, t: t, [None, (BF16, None)], "gates_dt_b",
        pack=([0], d_proj, D_INP, C_S // 128))
    g["conv_w"], g["conv_b"], d_proj = _conv_bwd(proj, conv_pre, (d_xs_skip, d_xs_dt, d_bm, d_cm), w["conv_w"], d_proj,
                                                 nct, Tc, "conv_b")
    gwt_in = _mm_tn(d_proj, h1, name="dw_in", tr_t=1152)
    d_h1, got["w_in"] = _mm(d_proj, w["wt_in"], trans_b=False, out_dtype=BF16, name="dx_in", tm_t=768, tk_t=3456,
                            hosted=[rows16(_from_padded(gwt_in))])
    grad_x, g["norm1_w"], d_ada1 = _stage_bwd(
        f_norm1, n_tm, x_in, p_norm1, [_col(d_h1, TM)], lambda i, t: t,
        [None, (F32, (d_x1, (TM, D), x_in[1][2]))], "norm1_b")
    return loss_blk, grad_x, g, got, (d_ada1, d_ada2, d_ada3)


def _exchange(x, a2a, name):
    shp = x.shape[1:] if a2a else x.shape

    def body(x_ref, o_ref, send_sems, recv_sems, local_sem):
        ops = _exchange_ops(x_ref, o_ref, send_sems, recv_sems, local_sem, a2a)
        for op in ops:
            op.start()
        for op in ops:
            op.wait()

    return pl.pallas_call(
        body, out_shape=jax.ShapeDtypeStruct((N_DEV,) + tuple(shp), x.dtype),
        in_specs=[pl.BlockSpec(memory_space=pl.ANY)], out_specs=pl.BlockSpec(memory_space=pl.ANY),
        scratch_shapes=[pltpu.SemaphoreType.DMA((N_DEV - 1,)), pltpu.SemaphoreType.DMA((N_DEV - 1,)),
                        pltpu.SemaphoreType.DMA(())],
        name=name,
    )(x)


def _all_gather(x, name):
    def body(x_ref, o_ref, send_sems, recv_sems, local_sem):
        mx, my, mc = lax.axis_index("x"), lax.axis_index("y"), lax.axis_index("c")
        me, sibling = (mx, my, mc), (mx, my, 1 - mc)
        chips = [(1 - mx, my), (mx, 1 - my), (1 - mx, 1 - my)]

        def slot(px, py, pc):
            return o_ref.at[4 * px + 2 * py + pc]

        def copy(k, block, to, src=None):
            return pltpu.make_async_remote_copy(
                src_ref=slot(*block) if src is None else src, dst_ref=slot(*block),
                send_sem=send_sems.at[k], recv_sem=recv_sems.at[k], device_id=to, device_id_type=pl.DeviceIdType.MESH)

        own = pltpu.make_async_copy(x_ref, slot(*me), local_sem)
        own.start()
        first = [copy(0, me, sibling, src=x_ref)] + [copy(1 + j, me, (*chip, mc), src=x_ref) for j, chip in enumerate(chips)]
        for cp in first:
            cp.start()
        passed = [copy(4 + j, (*chip, mc), sibling) for j, chip in enumerate(chips)]
        for j, chip in enumerate(chips):
            copy(1 + j, (*chip, mc), me).wait_recv()
            passed[j].start()
        copy(0, sibling, me).wait_recv()
        for j, chip in enumerate(chips):
            copy(4 + j, (*chip, 1 - mc), me).wait_recv()
        for cp in first + passed:
            cp.wait_send()
        own.wait()

    return pl.pallas_call(
        body, out_shape=jax.ShapeDtypeStruct((N_DEV,) + tuple(x.shape), x.dtype),
        in_specs=[pl.BlockSpec(memory_space=pl.ANY)], out_specs=pl.BlockSpec(memory_space=pl.ANY),
        scratch_shapes=[pltpu.SemaphoreType.DMA((N_DEV - 1,)), pltpu.SemaphoreType.DMA((N_DEV - 1,)),
                        pltpu.SemaphoreType.DMA(())],
        name=name,
    )(x)


def _adamw_math(w, gr, m, v):
    m = ADAM_B1 * m + (1.0 - ADAM_B1) * gr
    v = ADAM_B2 * v + (1.0 - ADAM_B2) * (gr * gr)
    m_hat = m / np.float32(1.0 - ADAM_B1 ** ADAM_STEP)
    v_hat = v / np.float32(1.0 - ADAM_B2 ** ADAM_STEP)
    delta = -ADAM_LR * (m_hat / (jnp.sqrt(v_hat) + ADAM_EPS) + ADAM_WD * w)
    return delta, m, v


def _sum_adamw(parts, w, m, v, name):
    R, C = w.shape
    n_parts = parts.shape[0]
    tr = _pick(R, max(16, (2 << 20) // (4 * C) // 16 * 16), 16)

    def body(p_ref, w_ref, m_ref, v_ref, g_ref, d_ref, mo_ref, vo_ref):
        gr = p_ref[0].astype(F32)
        for k in range(1, n_parts):
            gr = gr + p_ref[k].astype(F32)
        g_ref[...] = gr
        d_ref[...], mo_ref[...], vo_ref[...] = _adamw_math(w_ref[...], gr, m_ref[...], v_ref[...])

    if R % 16 == 0:
        grid = (R // tr,)
        tile, p_spec = pl.BlockSpec((tr, C), lambda i: (i, 0)), pl.BlockSpec((n_parts, tr, C), lambda i: (0, i, 0))
    else:
        tc = _pick(C, max(128, (2 << 20) // (4 * R) // 128 * 128), 128)
        grid = (C // tc,)
        tile, p_spec = pl.BlockSpec((R, tc), lambda i: (0, i)), pl.BlockSpec((n_parts, R, tc), lambda i: (0, 0, i))
    return pl.pallas_call(
        body, grid=grid, in_specs=[p_spec, tile, tile, tile],
        out_specs=[tile] * 4, out_shape=[jax.ShapeDtypeStruct((R, C), F32)] * 4,
        compiler_params=_cparams(("parallel",)), name=name,
    )(parts, w, m, v)


def _sum8(parts, name):
    _, R, C = parts.shape

    def body(p_ref, g_ref):
        gr = p_ref[0]
        for k in range(1, N_DEV):
            gr = gr + p_ref[k]
        g_ref[...] = gr

    return pl.pallas_call(body, out_shape=jax.ShapeDtypeStruct((R, C), F32), name=name)(parts)


def _adamw(w, gr, m, v, name):
    def body(w_ref, g_ref, m_ref, v_ref, d_ref, mo_ref, vo_ref):
        d_ref[...], mo_ref[...], vo_ref[...] = _adamw_math(w_ref[...], g_ref[...], m_ref[...], v_ref[...])

    return pl.pallas_call(body, out_shape=[jax.ShapeDtypeStruct(w.shape, F32)] * 3, name=name)(w, gr, m, v)


def _to_padded(wt_in):
    z = jnp.zeros((32,) + wt_in.shape[1:], wt_in.dtype)
    return jnp.concatenate([wt_in[0:3072], wt_in[5152:8224], wt_in[3104:5152], wt_in[8288:10336],
                            wt_in[3072:3104], wt_in[8224:8288], z], axis=0)


def _from_padded(p):
    return jnp.concatenate([p[0:3072], p[10240:10272], p[6144:8192], p[3072:6144], p[10272:10336], p[8192:10240]], axis=0)


def _unshard_cols(gathered):
    n, r, c = gathered.shape
    return jnp.transpose(gathered, (1, 0, 2)).reshape(r, n * c)


def _lanes(vec, lo):
    return jnp.concatenate([jnp.zeros((1, lo), F32), vec, jnp.zeros((1, 128 - lo - vec.shape[1]), F32)], axis=1)


_SMALL = (("b_ada", 6 * D), ("c_ctx", D), ("norm1_w", D), ("gla_bias_f", 512), ("gla_bias_b", 512), ("gla_norm_w", 256),
          ("conv_b", 3072), ("dt_bias_f", 32), ("dt_bias_b", 32), ("a_log_f", 32), ("a_log_b", 32), ("d_skip", 32),
          ("ssm_norm_w", 2048), ("norm2_w", D), ("final_norm_w", D))
_SHARDED_SMALL = (("gla_up_f", 16 * 512), ("gla_up_b", 16 * 512), ("conv_w", 4 * 3072))


def _pack(vals, names):
    flat = jnp.concatenate([vals[n].reshape(-1).astype(F32) for n, _ in names])
    pad = (-flat.shape[0]) % 1024
    return jnp.concatenate([flat, jnp.zeros((pad,), F32)]).reshape(-1, 128)


def _unpack(packed, names, shapes):
    flat, out, off = packed.reshape(-1), {}, 0
    for n, size in names:
        out[n] = flat[off:off + size].reshape(shapes[n])
        off += size
    return out


def kernel(x, c, ctx, c_ctx, w_ada, b_ada, norm1_w, w_in, gla_up_f, gla_bias_f, gla_up_b, gla_bias_b, gla_norm_w, conv_w, conv_b, dt_bias_f, dt_bias_b, a_log_f, a_log_b, d_skip, ssm_norm_w, w_pa, w_pb, w_out, norm2_w, w_gate, w_up, w_down, final_norm_w, loss_target, m_c_ctx, m_w_ada, m_b_ada, m_norm1_w, m_w_in, m_gla_up_f, m_gla_bias_f, m_gla_up_b, m_gla_bias_b, m_gla_norm_w, m_conv_w, m_conv_b, m_dt_bias_f, m_dt_bias_b, m_a_log_f, m_a_log_b, m_d_skip, m_ssm_norm_w, m_w_pa, m_w_pb, m_w_out, m_norm2_w, m_w_gate, m_w_up, m_w_down, m_final_norm_w, v_c_ctx, v_w_ada, v_b_ada, v_norm1_w, v_w_in, v_gla_up_f, v_gla_bias_f, v_gla_up_b, v_gla_bias_b, v_gla_norm_w, v_conv_w, v_conv_b, v_dt_bias_f, v_dt_bias_b, v_a_log_f, v_a_log_b, v_d_skip, v_ssm_norm_w, v_w_pa, v_w_pb, v_w_out, v_norm2_w, v_w_gate, v_w_up, v_w_down, v_final_norm_w):
    args = dict(locals())
    me = 4 * lax.axis_index("x") + 2 * lax.axis_index("y") + lax.axis_index("c")

    t_names = ("w_in", "w_gate", "w_up")
    local = {n: (jnp.transpose(args[n][0]) if n in t_names else args[n][0]) for n in
             ("w_in", "w_pa", "w_pb", "w_out", "w_gate", "w_up", "w_down")}
    full = {"wt_in": _to_padded(_all_gather(local["w_in"].astype(BF16), "ag_w_in").reshape(D_IN, D))}
    shards = {n: local[n].astype(BF16) for n in ("w_pa", "w_pb", "w_out", "w_gate", "w_up", "w_down")}
    sm = _exchange(jnp.concatenate([gla_up_f.reshape(-1), gla_up_b.reshape(-1), conv_w.reshape(-1), c.reshape(-1)]).reshape(-1, 128),
                   False, "ag_small")
    sm = sm.reshape(N_DEV, -1)
    n_ada = w_ada.shape[2]
    cc = jnp.concatenate([sm[:, 3584:3584 + D], c_ctx[None], jnp.zeros((_ADA_ROWS - N_DEV - 1, D), F32)], axis=0)
    w_ada16 = w_ada[0].astype(BF16)
    ada_cols = _exchange(_ada_fwd(cc, w_ada16, lax.dynamic_slice(b_ada, (0, me * n_ada), (1, n_ada))), False, "ag_ada")
    ada = jnp.concatenate([lax.dynamic_index_in_dim(ada_cols, me, axis=1, keepdims=False).reshape(1, -1),
                           ada_cols[:, N_DEV].reshape(1, -1), jnp.zeros((6, N_DEV * n_ada), F32)], axis=0)
    up_f = _unshard_cols(sm[:, 0:1024].reshape(N_DEV, 16, 64))
    up_b = _unshard_cols(sm[:, 1024:2048].reshape(N_DEV, 16, 64))
    full["conv_w"] = _unshard_cols(sm[:, 2048:3584].reshape(N_DEV, 4, 384))
    full["up_f"] = jnp.concatenate([up_f, jnp.zeros((112, 512), F32)], axis=0)
    full["up_b"] = jnp.concatenate([jnp.zeros((16, 512), F32), up_b, jnp.zeros((96, 512), F32)], axis=0)
    full["dtb"] = jnp.concatenate([jnp.zeros((1, 32), F32), dt_bias_f, dt_bias_b, jnp.zeros((1, 32), F32)], axis=1)
    full["alog"] = jnp.concatenate([jnp.zeros((1, 32), F32), a_log_f, a_log_b, jnp.zeros((1, 32), F32)], axis=1)
    full["dsk8"] = jnp.concatenate([_lanes(d_skip, 0), jnp.zeros((7, 128), F32)], axis=0)
    for n in ("norm1_w", "gla_bias_f", "gla_bias_b", "gla_norm_w", "conv_b", "ssm_norm_w", "norm2_w"):
        full[n] = args[n]
    full["final_norm_w"] = final_norm_w.reshape(1, D)

    loss_blk, grad_x, g, got, d_ada = _local_step(x[0], ctx[0], loss_target[0], ada, full, shards)

    d_all = _exchange(jnp.concatenate([d[0:2] for d in d_ada], axis=0), False, "ag_d_ada").reshape(N_DEV * 6, N_DEV * n_ada)
    gw_ada, gb_ada, d_cc = _ada_bwd(cc, w_ada16, d_all, lax.dynamic_slice(d_all, (0, me * n_ada), (N_DEV * 6, n_ada)))

    gs = dict(g, c_ctx=d_cc[N_DEV], b_ada=jnp.zeros_like(b_ada))
    gs["dt_bias_f"], gs["dt_bias_b"] = g["dtb"][:, 32:64], g["dtb"][:, 64:96]
    gs["a_log_f"], gs["a_log_b"] = g["alog"][:, 32:64], g["alog"][:, 64:96]
    gs["d_skip"] = g["dsk8"][0:1, 0:32]
    gs["gla_up_f"], gs["gla_up_b"] = g["up_f"][0:16], g["up_b"][16:32]
    gs["loss"] = loss_blk[0:1, 0:1]
    names = _SMALL + _SHARDED_SMALL + (("loss", 1),)
    shapes = {n: (args[n].shape if n in args else (1, 1)) for n, _ in names}
    shapes.update({"gla_up_f": (16, 512), "gla_up_b": (16, 512), "conv_w": (4, 3072)})
    red = _unpack(_sum8(_exchange(_pack(gs, names), False, "ag_small_grads"), "sum_small_grads"), names, shapes)
    loss = red["loss"].reshape(())
    grads = {n: red[n] for n, _ in _SMALL}
    grads["b_ada"] = gb_ada
    grads["gla_up_f"] = lax.dynamic_slice(red["gla_up_f"], (0, me * 64), (16, 64))[None]
    grads["gla_up_b"] = lax.dynamic_slice(red["gla_up_b"], (0, me * 64), (16, 64))[None]
    grads["conv_w"] = lax.dynamic_slice(red["conv_w"], (0, me * 384), (4, 384))[None]
    upd_names = tuple((n, s) for n, s in _SMALL) + (("gla_up_f", 1024), ("gla_up_b", 1024), ("conv_w", 1536))
    pk = lambda prefix, src: _pack({n: src[prefix + n] for n, _ in upd_names}, upd_names)
    d_s, m_s, v_s = _adamw(pk("", args), _pack(grads, upd_names), pk("m_", args), pk("v_", args), "adamw_small")
    upd_shapes = {n: args[n].shape for n, _ in upd_names}
    delta = _unpack(d_s, upd_names, upd_shapes)
    new_m = _unpack(m_s, upd_names, upd_shapes)
    new_v = _unpack(v_s, upd_names, upd_shapes)

    got["w_ada"] = gw_ada[None]
    for n, parts in got.items():
        orient = jnp.transpose if n in t_names else (lambda a: a)
        res = _sum_adamw(parts, local[n] if n in local else args[n][0], orient(args["m_" + n][0]), orient(args["v_" + n][0]),
                         "adamw_" + n)
        grads[n], delta[n], new_m[n], new_v[n] = [orient(r)[None] for r in res]

    order = ["c_ctx", "w_ada", "b_ada", "norm1_w", "w_in", "gla_up_f", "gla_bias_f", "gla_up_b", "gla_bias_b", "gla_norm_w",
             "conv_w", "conv_b", "dt_bias_f", "dt_bias_b", "a_log_f", "a_log_b", "d_skip", "ssm_norm_w", "w_pa", "w_pb",
             "w_out", "norm2_w", "w_gate", "w_up", "w_down", "final_norm_w"]
    fix = lambda d: [d[n].reshape(args[n].shape) for n in order]
    return (loss, grad_x[None], *fix(grads), *fix(delta), *fix(new_m), *fix(new_v))
```

```python
import functools

import jax
import jax.numpy as jnp
import numpy as np
from jax import lax
from jax.experimental import pallas as pl
from jax.experimental.pallas import tpu as pltpu

F32 = jnp.float32
BF16 = jnp.bfloat16
N_DEV = 8
D = 1024
EPS = 1e-6
GRID_W = 64
GLA_H, GLA_DK, GLA_DV = 4, 128, 256
GLA_C = 64
GLA_TAU = 16.0
SSM_G, SSM_HPG, SSM_P, SSM_N = 4, 8, 64, 128
SSM_C = 128
SSM_INNER = 2048
D_FF = 2816
D_IN = 10336
D_INP = 10368
C_Q, C_K, C_V, C_R, C_U, C_Z, C_GA, C_GB, C_S = 0, 512, 1024, 2048, 3072, 6144, 8192, 9216, 10240
TM = 256
TS = 128

ADAM_LR, ADAM_B1, ADAM_B2, ADAM_EPS, ADAM_WD, ADAM_STEP = 0.001, 0.9, 0.999, 1e-08, 0.01, 10
VMEM_LIMIT = 56 << 20


def _cparams(sem):
    return pltpu.CompilerParams(dimension_semantics=sem or None, vmem_limit_bytes=VMEM_LIMIT)


def _pick(n, target, mult):
    best = None
    for t in range(mult, min(n, target) + 1, mult):
        if n % t == 0:
            best = t
    return best if best is not None else n


def _dot_impl(a, b, ta, tb):
    dims = (((0 if ta else 1,), (1 if tb else 0,)), ((), ()))
    return lax.dot_general(a.astype(BF16), b.astype(BF16), dims, preferred_element_type=F32)


@functools.partial(jax.custom_vjp, nondiff_argnums=(2, 3))
def _bdot(a, b, ta=False, tb=False):
    return _dot_impl(a, b, ta, tb)


def _bdot_fwd(a, b, ta, tb):
    return _dot_impl(a, b, ta, tb), (a, b)


def _bdot_bwd(ta, tb, res, g):
    a, b = res
    if not ta and not tb:
        return _dot_impl(g, b, False, True), _dot_impl(a, g, True, False)
    if not ta and tb:
        return _dot_impl(g, b, False, False), _dot_impl(g, a, True, False)
    if ta and not tb:
        return _dot_impl(b, g, False, True), _dot_impl(a, g, False, False)
    raise NotImplementedError


_bdot.defvjp(_bdot_fwd, _bdot_bwd)


_NN = (((1,), (0,)), ((), ()))
_NT = (((1,), (1,)), ((), ()))
_TN = (((0,), (0,)), ((), ()))


def _dg2(x, e, x_is_lhs, dims):
    hi = x.astype(BF16)
    lo = (x - hi.astype(F32)).astype(BF16)
    e = e.astype(BF16)
    if x_is_lhs:
        return (lax.dot_general(hi, e, dims, preferred_element_type=F32)
                + lax.dot_general(lo, e, dims, preferred_element_type=F32))
    return (lax.dot_general(e, hi, dims, preferred_element_type=F32)
            + lax.dot_general(e, lo, dims, preferred_element_type=F32))


_EDOT_FWD = {"xe": (True, _NN), "ex": (False, _NN), "ext": (False, _NT)}
_EDOT_BWD = {"xe": (True, _NT), "ex": (False, _TN), "ext": (True, _TN)}


@functools.partial(jax.custom_vjp, nondiff_argnums=(2,))
def _edot(x, e, mode):
    return _dg2(x, e, *_EDOT_FWD[mode])


def _edot_fwd(x, e, mode):
    return _dg2(x, e, *_EDOT_FWD[mode]), e


def _edot_bwd(mode, e, g):
    return _dg2(g, e, *_EDOT_BWD[mode]), None


_edot.defvjp(_edot_fwd, _edot_bwd)


def _shift_impl(u, pos, per, s):
    n = u.shape[0]
    rolled = u if s == 0 else pltpu.roll(u, (-s) % n, 0)
    ok = (pos + s >= 0) & (pos + s < per)
    return jnp.where(ok, rolled, 0.0)


def _rms(x, w):
    return x * lax.rsqrt(jnp.mean(x * x, axis=-1, keepdims=True) + EPS) * w


def _silu(x):
    return x * jax.nn.sigmoid(x)


def _softplus(x):
    return jnp.maximum(x, 0.0) + jnp.log(1.0 + jnp.exp(-jnp.abs(x)))


def _logsig(x):
    return jnp.minimum(x, 0.0) - jnp.log(1.0 + jnp.exp(-jnp.abs(x)))


def _tri(n, rev):
    t = lax.broadcasted_iota(jnp.int32, (n, n), 0)
    s = lax.broadcasted_iota(jnp.int32, (n, n), 1)
    return (s >= t) if rev else (t >= s)


def _head_expand(first_lane):
    l = lax.broadcasted_iota(jnp.int32, (128, SSM_INNER), 0)
    c = lax.broadcasted_iota(jnp.int32, (128, SSM_INNER), 1)
    return (l == first_lane + lax.shift_right_logical(c, 6)).astype(F32)


def _exchange_ops(x_ref, o_ref, send_sems, recv_sems, local_sem, a2a):
    mx, my, mc = lax.axis_index("x"), lax.axis_index("y"), lax.axis_index("c")
    me = 4 * mx + 2 * my + mc
    ops = []
    for k in range(1, N_DEV):
        px = 1 - mx if k & 4 else mx
        py = 1 - my if k & 2 else my
        pc = 1 - mc if k & 1 else mc
        ops.append(pltpu.make_async_remote_copy(
            src_ref=x_ref.at[4 * px + 2 * py + pc] if a2a else x_ref, dst_ref=o_ref.at[me],
            send_sem=send_sems.at[k - 1], recv_sem=recv_sems.at[k - 1],
            device_id=(px, py, pc), device_id_type=pl.DeviceIdType.MESH))
    ops.append(pltpu.make_async_copy(x_ref.at[me] if a2a else x_ref, o_ref.at[me], local_sem))
    return ops


def _call(body, *, grid, in_specs, out_specs, out_shape, scratch=(), sem, name, args, hosted=(), aliases=None):
    n_in, n_out, n_s, n_h = len(in_specs), len(out_specs), len(scratch), len(hosted)
    aliases = aliases or {}
    if not n_h:
        return pl.pallas_call(body, grid=grid, in_specs=list(in_specs), out_specs=list(out_specs),
                              out_shape=list(out_shape), scratch_shapes=list(scratch), input_output_aliases=aliases,
                              compiler_params=_cparams(sem), name=name)(*args)
    b0, c0 = n_in + n_h, n_in + n_h + n_out
    d0 = c0 + n_h

    def wrapped(*refs):
        hx, ho = refs[n_in:b0], refs[c0:d0]
        send_sems, recv_sems, local_sems = refs[d0 + n_s:]
        ids = [pl.program_id(ax) for ax in range(len(grid))]
        first = functools.reduce(jnp.logical_and, [i == 0 for i in ids])
        last = functools.reduce(jnp.logical_and, [i == n - 1 for i, n in zip(ids, grid)])

        def ops():
            return [op for n, (_, a2a) in enumerate(hosted)
                    for op in _exchange_ops(hx[n], ho[n], send_sems.at[n], recv_sems.at[n], local_sems.at[n], a2a)]

        @pl.when(first)
        def _():
            for op in ops():
                op.start()

        body(*refs[:n_in], *refs[b0:c0], *refs[d0:d0 + n_s])

        @pl.when(last)
        def _():
            for op in ops():
                op.wait()

    hbm = pl.BlockSpec(memory_space=pl.ANY)
    h_shapes = [jax.ShapeDtypeStruct((N_DEV,) + tuple(x.shape[1:] if a2a else x.shape), x.dtype) for x, a2a in hosted]
    return pl.pallas_call(
        wrapped, grid=grid, in_specs=list(in_specs) + [hbm] * n_h, out_specs=list(out_specs) + [hbm] * n_h,
        out_shape=list(out_shape) + h_shapes, input_output_aliases=aliases,
        scratch_shapes=list(scratch) + [pltpu.SemaphoreType.DMA((n_h, N_DEV - 1)), pltpu.SemaphoreType.DMA((n_h, N_DEV - 1)),
                                        pltpu.SemaphoreType.DMA((n_h,))],
        compiler_params=_cparams(("arbitrary",) * len(grid)), name=name,
    )(*args, *[x for x, _ in hosted])


def _mm(a, b, *, trans_b, out_dtype, name, tm_t=1408, tn_t=1024, tk_t=2816, hosted=()):
    M, K = a.shape
    N = b.shape[0] if trans_b else b.shape[1]
    tm, tn, tk = _pick(M, tm_t, 8), _pick(N, tn_t, 128), _pick(K, tk_t, 128)
    nk = K // tk
    dims = (((1,), (1,)), ((), ())) if trans_b else (((1,), (0,)), ((), ()))

    def body_one(a_ref, b_ref, o_ref):
        o_ref[...] = lax.dot_general(a_ref[...], b_ref[...], dims, preferred_element_type=F32).astype(out_dtype)

    def body_acc(a_ref, b_ref, o_ref, acc_ref):
        k = pl.program_id(2)
        p = lax.dot_general(a_ref[...], b_ref[...], dims, preferred_element_type=F32)

        @pl.when(k == 0)
        def _():
            acc_ref[...] = p

        @pl.when(k > 0)
        def _():
            acc_ref[...] += p

        @pl.when(k == nk - 1)
        def _():
            o_ref[...] = acc_ref[...].astype(out_dtype)

    b_spec = pl.BlockSpec((tn, tk), lambda j, i, k: (j, k)) if trans_b else pl.BlockSpec((tk, tn), lambda j, i, k: (k, j))
    outs = _call(
        body_one if nk == 1 else body_acc, grid=(N // tn, M // tm, nk),
        in_specs=[pl.BlockSpec((tm, tk), lambda j, i, k: (i, k)), b_spec],
        out_specs=[pl.BlockSpec((tm, tn), lambda j, i, k: (i, j))],
        out_shape=[jax.ShapeDtypeStruct((M, N), out_dtype)],
        scratch=[] if nk == 1 else [pltpu.VMEM((tm, tn), F32)],
        sem=("parallel", "parallel", "arbitrary"), name=name, args=(a, b), hosted=hosted)
    return outs if hosted else outs[0]


def _mm_tn(a, b, *, name, tm_t=2816, tr_t=1024, tn_t=1408, out_dtype=BF16):
    M, R = a.shape
    N = b.shape[1]
    tm, tr, tn = _pick(M, tm_t, 8), _pick(R, tr_t, 128), _pick(N, tn_t, 128)
    nm = M // tm

    def body(a_ref, b_ref, o_ref, acc_ref):
        m = pl.program_id(2)
        p = lax.dot_general(a_ref[...], b_ref[...], (((0,), (0,)), ((), ())), preferred_element_type=F32)

        @pl.when(m == 0)
        def _():
            acc_ref[...] = p

        @pl.when(m > 0)
        def _():
            acc_ref[...] += p

        @pl.when(m == nm - 1)
        def _():
            o_ref[...] = acc_ref[...].astype(out_dtype)

    return pl.pallas_call(
        body, grid=(R // tr, N // tn, nm),
        in_specs=[pl.BlockSpec((tm, tr), lambda r, j, m: (m, r)), pl.BlockSpec((tm, tn), lambda r, j, m: (m, j))],
        out_specs=pl.BlockSpec((tr, tn), lambda r, j, m: (r, j)),
        out_shape=jax.ShapeDtypeStruct((R, N), out_dtype),
        scratch_shapes=[pltpu.VMEM((tr, tn), F32)],
        compiler_params=_cparams(("parallel", "parallel", "arbitrary")), name=name,
    )(a, b)


def _col(arr, tm, width=None, col=0):
    width = arr.shape[1] if width is None else width
    return (arr, (tm, width), lambda i: (i, col))


def _lat(arr, tm, nct):
    return (arr, (tm, arr.shape[1]), lambda i: (jnp.maximum(i - nct, 0), 0))


def _ctx(arr, tm, nct):
    return (arr, (tm, arr.shape[1]), lambda i: (jnp.minimum(i, nct - 1), 0))


def _whole(p):
    return pl.BlockSpec(p.shape, lambda i, nd=p.ndim: (0,) * nd)


def _stage_fwd(f, n_tiles, ins, params, outs, name):
    ni, npar = len(ins), len(params)
    o_specs = [pl.BlockSpec((o[1], o[3]), lambda i: (0, i)) if len(o) > 4 else pl.BlockSpec((o[3], o[1]), lambda i: (i, 0))
               for o in outs]
    o_shapes = [jax.ShapeDtypeStruct((o[1], o[0]) if len(o) > 4 else (o[0], o[1]), o[2]) for o in outs]

    def body(*refs):
        i = pl.program_id(0)
        xs = [r[...].astype(F32) for r in refs[:ni]]
        ps = [r[...] for r in refs[ni:ni + npar]]
        for r, v in zip(refs[ni + npar:], f(i, xs, ps, False)):
            r[...] = v.astype(r.dtype)

    return pl.pallas_call(
        body, grid=(n_tiles,),
        in_specs=[pl.BlockSpec(bs, fn) for _, bs, fn in ins] + [_whole(p) for p in params],
        out_specs=o_specs, out_shape=o_shapes,
        compiler_params=_cparams(("parallel",)), name=name,
    )(*[a for a, _, _ in ins], *params)


def _stage_bwd(f, n_tiles, ins, params, cts, ct_fn, dins, name, hosted=(), pack=None):
    ni, npar, nc = len(ins), len(params), len(cts)
    want = [k for k, d in enumerate(dins) if d is not None]
    extras = [dins[k][1] for k in want if dins[k][1] is not None]
    n_buf = 0 if pack is None or pack[1] is None else 1
    n_main_in = ni + npar + nc + len(extras)

    def body(*refs):
        i = pl.program_id(0)
        xs = [r[...].astype(F32) for r in refs[:ni]]
        ps = [r[...] for r in refs[ni:ni + npar]]
        ct_tiles = [r[...].astype(F32) for r in refs[ni + npar:ni + npar + nc]]
        ex_refs = list(refs[ni + npar + nc:n_main_in])
        out_refs = refs[n_main_in + n_buf:]
        _, vjp = jax.vjp(lambda xs_, ps_: tuple(f(i, xs_, ps_, True)), xs, ps)
        dxs, dps = vjp(tuple(ct_fn(i, ct_tiles)))
        for n, k in enumerate(want):
            v = dxs[k]
            if dins[k][1] is not None:
                v = v + ex_refs.pop(0)[...].astype(F32)
            out_refs[n][...] = v.astype(out_refs[n].dtype)
        if pack is not None:
            packed = [dxs[k] for k in pack[0]]
            r = out_refs[len(want) + npar]
            r[...] = (packed[0] if len(packed) == 1 else jnp.concatenate(packed, axis=1)).astype(r.dtype)
        for r, v in zip(out_refs[len(want):len(want) + npar], dps):
            @pl.when(i == 0)
            def _(r=r, v=v):
                r[...] = v

            @pl.when(i > 0)
            def _(r=r, v=v):
                r[...] += v

    in_specs = ([pl.BlockSpec(bs, fn) for _, bs, fn in ins] + [_whole(p) for p in params]
                + [pl.BlockSpec(bs, fn) for _, bs, fn in cts] + [pl.BlockSpec(bs, fn) for _, bs, fn in extras])
    out_specs = ([pl.BlockSpec(ins[k][1], lambda i, fn=ins[k][2]: (fn(i)[0], 0)) for k in want] + [_whole(p) for p in params])
    out_shape = ([jax.ShapeDtypeStruct((ins[k][0].shape[0], ins[k][1][1]), dins[k][0]) for k in want]
                 + [jax.ShapeDtypeStruct(p.shape, F32) for p in params])
    args = [*[a for a, _, _ in ins], *params, *[a for a, _, _ in cts], *[a for a, _, _ in extras]]
    aliases = None
    if pack is not None:
        ks, buf, total_cols, block = pack
        width = sum(ins[k][1][1] for k in ks)
        out_specs.append(pl.BlockSpec((ins[ks[0]][1][0], width), lambda i, fn=ins[ks[0]][2]: (fn(i)[0], block)))
        out_shape.append(jax.ShapeDtypeStruct((ins[ks[0]][0].shape[0], total_cols), BF16))
        if buf is not None:
            in_specs.append(pl.BlockSpec(memory_space=pl.ANY))
            args.append(buf)
            aliases = {len(args) - 1: len(out_specs) - 1}
    return _call(body, grid=(n_tiles,), in_specs=in_specs, out_specs=out_specs, out_shape=out_shape,
                 sem=("arbitrary",), name=name, args=args, hosted=hosted, aliases=aliases)


def _mod_rows(i, nct, ada, lo):
    return jnp.where(i < nct, ada[1:2, lo:lo + D], ada[0:1, lo:lo + D])


def _f_norm1(nct):
    def f(i, xs, ps, diff):
        ctx, x = xs
        nw, ada = ps
        xt = jnp.where(i < nct, ctx, x)
        return (_rms(xt, nw) * (1.0 + _mod_rows(i, nct, ada, D)) + _mod_rows(i, nct, ada, 0),)
    return f


def _f_gates_dt(i, xs, ps, diff):
    small, xc = xs
    up_f, up_b, gb_f, gb_b, dtb, alog = ps
    dot = _bdot if diff else _dot_impl_nn
    lg_f = _logsig(dot(small, up_f) + gb_f) * (1.0 / GLA_TAU)
    lg_b = _logsig(dot(small, up_b) + gb_b) * (1.0 / GLA_TAU)
    dtp = _softplus(small + dtb)
    aa = -jnp.exp(alog) * dtp
    e_f, e_b = _head_expand(32), _head_expand(64)
    n = small.shape[0]
    t = lax.broadcasted_iota(jnp.int32, (n, n), 0)
    s = lax.broadcasted_iota(jnp.int32, (n, n), 1)
    same = lax.shift_right_logical(t, 6) == lax.shift_right_logical(s, 6)
    pre_g, suf_g = ((t >= s) & same).astype(F32), ((s >= t) & same).astype(F32)
    pre_s, suf_s = (t >= s).astype(F32), (s >= t).astype(F32)
    cum_f, cum_b = _edot(aa, pre_s, "ex"), _edot(aa, suf_s, "ex")
    h = lax.broadcasted_iota(jnp.int32, (SSM_G * SSM_HPG, 128), 0)
    l = lax.broadcasted_iota(jnp.int32, (SSM_G * SSM_HPG, 128), 1)
    cum_ft = _edot(cum_f, (l == h + 32).astype(F32), "ext")
    cum_bt = _edot(cum_b, (l == h + 64).astype(F32), "ext")
    return (_edot(lg_f, pre_g, "ex"), _edot(lg_b, suf_g, "ex"), xc * _bdot(dtp, e_f), xc * _bdot(dtp, e_b),
            _edot(cum_f, e_f, "xe"), _edot(cum_b, e_b, "xe"), cum_ft, cum_bt)


def _dot_impl_nn(a, b):
    return _dot_impl(a, b, False, False)


def _f_conv(nct, tc):
    def f(i, xs, ps, diff):
        (u,) = xs
        cw, cb = ps
        n = u.shape[0]
        t = lax.broadcasted_iota(jnp.int32, (n, 1), 0)
        per = jnp.where(i < nct, tc, GRID_W)
        pos = jnp.bitwise_and(t, per - 1)
        acc = cb + _shift_impl(u, pos, per, -2) * cw[0:1]
        for j in range(1, 4):
            acc = acc + _shift_impl(u, pos, per, j - 2) * cw[j:j + 1]
        return (_silu(acc), acc)
    return f


def _conv_bwd(proj, pre, cts, cw, buf, nct, tc, name):
    A = proj.shape[0]
    width = cw.shape[1]

    def body(u_ref, p_ref, c0, c1, c2, c3, cw_ref, buf_ref, dcw_ref, dcb_ref, o_ref):
        i = pl.program_id(0)
        t = lax.broadcasted_iota(jnp.int32, (TM, 1), 0)
        per = jnp.where(i < nct, tc, GRID_W)
        pos = jnp.bitwise_and(t, per - 1)
        pre_ = p_ref[...].astype(F32)
        s = jax.nn.sigmoid(pre_)
        g = jnp.concatenate([c0[...].astype(F32) + c1[...].astype(F32), c2[...].astype(F32), c3[...].astype(F32)], axis=1)
        g = g * (s * (1.0 + pre_ * (1.0 - s)))
        u = u_ref[...].astype(F32)
        w_ = cw_ref[...]
        du, rows = None, []
        for j in range(4):
            gj = g if j == 2 else _shift_impl(g, pos, per, 2 - j)
            du = gj * w_[j:j + 1] if du is None else du + gj * w_[j:j + 1]
            rows.append(jnp.sum(gj * u, axis=0, keepdims=True))
        o_ref[...] = du.astype(o_ref.dtype)
        for r, v in ((dcw_ref, jnp.concatenate(rows, axis=0)), (dcb_ref, jnp.sum(g, axis=0, keepdims=True))):
            @pl.when(i == 0)
            def _(r=r, v=v):
                r[...] = v

            @pl.when(i > 0)
            def _(r=r, v=v):
                r[...] += v

    blk = C_U // width
    tile = lambda a: pl.BlockSpec((TM, a.shape[1]), lambda i: (i, 0))
    return _call(
        body, grid=(A // TM,),
        in_specs=[pl.BlockSpec((TM, width), lambda i: (i, blk)), tile(pre)] + [tile(c) for c in cts]
        + [_whole(cw), pl.BlockSpec(memory_space=pl.ANY)],
        out_specs=[_whole(cw), pl.BlockSpec((1, width), lambda i: (0, 0)), pl.BlockSpec((TM, width), lambda i: (i, blk))],
        out_shape=[jax.ShapeDtypeStruct(cw.shape, F32), jax.ShapeDtypeStruct((1, width), F32),
                   jax.ShapeDtypeStruct(buf.shape, buf.dtype)],
        sem=("arbitrary",), name=name, args=(proj, pre, *cts, cw, buf), aliases={7: 2})


def _f_gla_post(i, xs, ps, diff):
    ogf, ogb, r = xs
    (gw,) = ps
    o = ogf + ogb
    parts = [_rms(o[:, h * GLA_DV:(h + 1) * GLA_DV], gw) for h in range(GLA_H)]
    return (jnp.concatenate(parts, axis=1) * _silu(r),)


def _f_ssd_post(i, xs, ps, diff):
    yf, yb, xc, z = xs
    dsk8, nw = ps
    dsk = _edot(dsk8, _head_expand(0), "xe")[0:1]
    y = (yf + yb + dsk * xc) * _silu(z)
    w = SSM_INNER // SSM_G
    parts = [_rms(y[:, g * w:(g + 1) * w], nw[:, g * w:(g + 1) * w]) for g in range(SSM_G)]
    return (jnp.concatenate(parts, axis=1),)


def _f_merge(i, xs, ps, diff):
    ga, gb, ya, yb = xs
    return (jax.nn.sigmoid(ga) * ya + jax.nn.sigmoid(gb) * yb,)


def _f_res1(nct):
    def f(i, xs, ps, diff):
        ctx, x, mix = xs
        ada, nw = ps
        h2 = jnp.where(i < nct, ctx, x) + _mod_rows(i, nct, ada, 2 * D) * mix
        return (h2, _rms(h2, nw) * (1.0 + _mod_rows(i, nct, ada, 4 * D)) + _mod_rows(i, nct, ada, 3 * D))
    return f


def _f_swiglu(i, xs, ps, diff):
    (gu,) = xs
    return (_silu(gu[:, :D_FF]) * gu[:, D_FF:],)


def _loss_head(h2, dn, tgt, ada, fw, nct):
    A = h2.shape[0]
    n_tiles = A // TM

    def tile_loss(i, h2t, dnt, tg, ada_, fw_):
        h3 = h2t + ada_[0:1, 5 * D:6 * D] * dnt
        err = _rms(h3, fw_) - tg
        row = 0.5 * jnp.mean(err * err, axis=-1, keepdims=True)
        return jnp.sum(row, axis=0, keepdims=True) * jnp.where(i < nct, 0.0, 1.0)

    def body(h2_ref, dn_ref, tg_ref, ada_ref, fw_ref, loss_ref, dh_ref, ddn_ref, dada_ref, dfw_ref):
        i = pl.program_id(0)
        val, vjp = jax.vjp(functools.partial(tile_loss, i), h2_ref[...], dn_ref[...].astype(F32), tg_ref[...], ada_ref[...],
                           fw_ref[...])
        dh, ddn, _, dada, dfw = vjp(jnp.ones((1, 1), F32))
        dh_ref[...] = dh
        ddn_ref[...] = ddn.astype(BF16)
        lv = jnp.broadcast_to(val, loss_ref.shape)
        for r, v in ((loss_ref, lv), (dada_ref, dada), (dfw_ref, dfw)):
            @pl.when(i == 0)
            def _(r=r, v=v):
                r[...] = v

            @pl.when(i > 0)
            def _(r=r, v=v):
                r[...] += v

    row = lambda i: (i, 0)
    return pl.pallas_call(
        body, grid=(n_tiles,),
        in_specs=[pl.BlockSpec((TM, D), row), pl.BlockSpec((TM, D), row),
                  pl.BlockSpec((TM, D), lambda i: (jnp.maximum(i - nct, 0), 0)), _whole(ada), _whole(fw)],
        out_specs=[pl.BlockSpec((8, 128), lambda i: (0, 0)), pl.BlockSpec((TM, D), row), pl.BlockSpec((TM, D), row),
                   _whole(ada), _whole(fw)],
        out_shape=[jax.ShapeDtypeStruct((8, 128), F32), jax.ShapeDtypeStruct((A, D), F32),
                   jax.ShapeDtypeStruct((A, D), BF16), jax.ShapeDtypeStruct(ada.shape, F32),
                   jax.ShapeDtypeStruct(fw.shape, F32)],
        compiler_params=_cparams(("arbitrary",)), name="loss_head",
    )(h2, dn, tgt, ada, fw)


def _chunk_of(step, n_chunks, n_ctx_chunks, rev):
    if not rev:
        return step
    return jnp.where(step < n_ctx_chunks, n_ctx_chunks - 1 - step, n_chunks - 1 - (step - n_ctx_chunks))


def _gla_step(st, q, k, v, b, rev):
    tri = _tri(GLA_C, rev).astype(F32)
    last = 0 if rev else GLA_C - 1
    outs, sts = [], []
    for h in range(GLA_H):
        kk = slice(h * GLA_DK, (h + 1) * GLA_DK)
        vv = slice(h * GLA_DV, (h + 1) * GLA_DV)
        qh, kh, vh, bh, sh = q[:, kk] * (GLA_DK ** -0.5), k[:, kk], v[:, vv], b[:, kk], st[vv]
        tot = bh[last:last + 1]
        mid = bh[GLA_C // 2:GLA_C // 2 + 1]
        att = _bdot(qh * jnp.exp(bh - mid), kh * jnp.exp(mid - bh), False, True) * tri
        outs.append(_bdot(att, vh) + _bdot(qh * jnp.exp(bh), sh, False, True))
        sts.append(sh * jnp.exp(tot) + _bdot(vh, kh * jnp.exp(tot - bh), True, False))
    return jnp.concatenate(sts, axis=0), jnp.concatenate(outs, axis=1)


_GQK, _GV = GLA_H * GLA_DK, GLA_H * GLA_DV


GLA_CPS = 4
_GB = GLA_C * GLA_CPS


def _gla_order(rev):
    return list(reversed(range(GLA_CPS))) if rev else list(range(GLA_CPS))


def _gla_fwd(proj, bg_f, bg_b, ncc, name):
    A = proj.shape[0]
    nb = A // _GB
    chs = [lambda s, rev=rev: _chunk_of(s, nb, ncc // GLA_CPS, rev) for rev in (False, True)]

    def body(*refs):
        ins, outs, sts = refs[:8], refs[8:12], refs[12:]

        for d, rev in enumerate((False, True)):
            q_ref, k_ref, v_ref, b_ref = ins[4 * d:4 * d + 4]
            o_ref, ss_ref = outs[2 * d:2 * d + 2]
            st_ref = sts[d]

            @pl.when(pl.program_id(0) == 0)
            def _(st_ref=st_ref):
                st_ref[...] = jnp.zeros_like(st_ref)

            st = st_ref[...]
            for pos, sub in enumerate(_gla_order(rev)):
                rows = slice(sub * GLA_C, (sub + 1) * GLA_C)
                ss_ref[0, pos] = st
                st, o = _gla_step(st, q_ref[rows].astype(F32), k_ref[rows].astype(F32), v_ref[rows].astype(F32),
                                  b_ref[rows], rev)
                o_ref[rows] = o.astype(o_ref.dtype)
            st_ref[...] = st

    in_specs, out_specs = [], []
    for ch in chs:
        in_specs += [pl.BlockSpec((_GB, _GQK), lambda s, ch=ch: (ch(s), C_Q // _GQK)),
                     pl.BlockSpec((_GB, _GQK), lambda s, ch=ch: (ch(s), C_K // _GQK)),
                     pl.BlockSpec((_GB, _GV), lambda s, ch=ch: (ch(s), C_V // _GV)),
                     pl.BlockSpec((_GB, _GQK), lambda s, ch=ch: (ch(s), 0))]
        out_specs += [pl.BlockSpec((_GB, _GV), lambda s, ch=ch: (ch(s), 0)),
                      pl.BlockSpec((1, GLA_CPS, _GV, GLA_DK), lambda s: (s, 0, 0, 0))]
    return pl.pallas_call(
        body, grid=(nb,), in_specs=in_specs, out_specs=out_specs,
        out_shape=[jax.ShapeDtypeStruct((A, _GV), BF16), jax.ShapeDtypeStruct((nb, GLA_CPS, _GV, GLA_DK), F32)] * 2,
        scratch_shapes=[pltpu.VMEM((_GV, GLA_DK), F32)] * 2,
        compiler_params=_cparams(("arbitrary",)), name=name,
    )(proj, proj, proj, bg_f, proj, proj, proj, bg_b)


def _gla_bwd(proj, lg, saved, d_o, prev, ncc, rev, name, pack=None):
    A = proj.shape[0]
    nb = A // _GB
    st_of = lambda r: nb - 1 - r
    ch = lambda r: _chunk_of(st_of(r), nb, ncc // GLA_CPS, rev)
    qs = pl.BlockSpec((_GB, _GQK), lambda r: (ch(r), 0))
    vs = pl.BlockSpec((_GB, _GV), lambda r: (ch(r), 0))
    n_prev = 0 if prev is None else 3

    n_buf = 0 if pack is None else 1

    def body(*refs):
        q_ref, k_ref, v_ref, b_ref, ss_ref, do_ref = refs[:6]
        p_refs = refs[6:6 + n_prev]
        out_refs = refs[6 + n_prev + n_buf:]
        ds_ref = out_refs[-1]

        @pl.when(pl.program_id(0) == 0)
        def _():
            ds_ref[...] = jnp.zeros_like(ds_ref)

        ds = ds_ref[...]
        for pos, sub in reversed(list(enumerate(_gla_order(rev)))):
            rows = slice(sub * GLA_C, (sub + 1) * GLA_C)
            _, vjp = jax.vjp(functools.partial(_gla_step, rev=rev), ss_ref[0, pos], q_ref[rows].astype(F32),
                             k_ref[rows].astype(F32), v_ref[rows].astype(F32), b_ref[rows])
            ds, dq, dk, dv, db = vjp((ds, do_ref[rows].astype(F32)))
            if n_prev:
                dq, dk, dv = [d + p[rows].astype(F32) for d, p in zip((dq, dk, dv), p_refs)]
            if pack is None:
                for r, d in zip(out_refs[:4], (dq, dk, dv, db)):
                    r[rows] = d.astype(r.dtype)
            else:
                out_refs[0][rows] = jnp.concatenate([dq, dk, dv], axis=1).astype(out_refs[0].dtype)
                out_refs[1][rows] = db
        ds_ref[...] = ds

    in_specs = [
        pl.BlockSpec((_GB, _GQK), lambda r: (ch(r), C_Q // _GQK)),
        pl.BlockSpec((_GB, _GQK), lambda r: (ch(r), C_K // _GQK)),
        pl.BlockSpec((_GB, _GV), lambda r: (ch(r), C_V // _GV)),
        qs,
        pl.BlockSpec((1, GLA_CPS, _GV, GLA_DK), lambda r: (st_of(r), 0, 0, 0)),
        vs,
    ] + ([qs, qs, vs] if n_prev else [])
    args = [proj, proj, proj, lg, saved, d_o, *(prev or ())]
    if pack is None:
        out_specs = [qs, qs, vs, qs]
        out_shape = [jax.ShapeDtypeStruct((A, _GQK), BF16), jax.ShapeDtypeStruct((A, _GQK), BF16),
                     jax.ShapeDtypeStruct((A, _GV), BF16), jax.ShapeDtypeStruct((A, _GQK), F32)]
        aliases = None
    else:
        in_specs.append(pl.BlockSpec(memory_space=pl.ANY))
        args.append(pack[0])
        out_specs = [pl.BlockSpec((_GB, 2 * _GQK + _GV), lambda r: (ch(r), 0)), qs]
        out_shape = [jax.ShapeDtypeStruct((A, pack[1]), BF16), jax.ShapeDtypeStruct((A, _GQK), F32)]
        aliases = {len(args) - 1: 0}
    return _call(body, grid=(nb,), in_specs=in_specs, out_specs=out_specs, out_shape=out_shape,
                 scratch=[pltpu.VMEM((_GV, GLA_DK), F32)], sem=("arbitrary",), name=name, args=args, aliases=aliases)


def _ssd_step(st, x, bm, cm, cum, cum_t, rev):
    mask = _tri(SSM_C, rev)
    last = 0 if rev else SSM_C - 1
    tot = cum[last:last + 1]
    cb = _bdot(cm, bm, False, True)
    ys = []
    for e in range(SSM_HPG):
        lm = jnp.exp(jnp.where(mask, cum[:, e * SSM_P:e * SSM_P + 1] - cum_t[e:e + 1], -jnp.inf))
        ys.append(_bdot(cb * lm, x[:, e * SSM_P:(e + 1) * SSM_P]))
    y = jnp.concatenate(ys, axis=1) + _bdot(cm, st) * jnp.exp(cum)
    st_new = st * jnp.exp(tot) + _bdot(bm, x * jnp.exp(tot - cum), True, False)
    return st_new, y


SSM_GPS = 4
_W = SSM_HPG * SSM_P
_XBC_B = SSM_INNER // (SSM_N * SSM_GPS)
_XBC_C = _XBC_B + SSM_G // SSM_GPS


def _ssd_steps(st, x, bm, cm, cum, cum_t, rev):
    outs = [_ssd_step(st[j * SSM_N:(j + 1) * SSM_N], x[:, j * _W:(j + 1) * _W], bm[:, j * SSM_N:(j + 1) * SSM_N],
                      cm[:, j * SSM_N:(j + 1) * SSM_N], cum[:, j * _W:(j + 1) * _W],
                      cum_t[j * SSM_HPG:(j + 1) * SSM_HPG], rev) for j in range(SSM_GPS)]
    return jnp.concatenate([o[0] for o in outs], axis=0), jnp.concatenate([o[1] for o in outs], axis=1)


SSM_CPS = 2
_SB = SSM_C * SSM_CPS


def _ssd_order(rev):
    return list(reversed(range(SSM_CPS))) if rev else list(range(SSM_CPS))


def _ssd_fwd(x_f, x_b, xbc, cum_f, cum_b, cum_tf, cum_tb, ncc, name, hosted=()):
    A = x_f.shape[0]
    nb = A // _SB
    ng = SSM_G // SSM_GPS
    chs = [lambda s, rev=rev: _chunk_of(s, nb, ncc // SSM_CPS, rev) for rev in (False, True)]

    def body(*refs):
        ins, outs, sts = refs[:10], refs[10:14], refs[14:]
        for d, rev in enumerate((False, True)):
            x_ref, b_ref, c_ref, a_ref, at_ref = ins[5 * d:5 * d + 5]
            y_ref, ss_ref = outs[2 * d:2 * d + 2]
            st_ref = sts[d]

            @pl.when(pl.program_id(1) == 0)
            def _(st_ref=st_ref):
                st_ref[...] = jnp.zeros_like(st_ref)

            st = st_ref[...]
            for pos, sub in enumerate(_ssd_order(rev)):
                rows = slice(sub * SSM_C, (sub + 1) * SSM_C)
                ss_ref[0, 0, pos] = st
                st, y = _ssd_steps(st, x_ref[rows].astype(F32), b_ref[rows].astype(F32), c_ref[rows].astype(F32),
                                   a_ref[rows], at_ref[:, rows], rev)
                y_ref[rows] = y.astype(y_ref.dtype)
            st_ref[...] = st

    in_specs, out_specs = [], []
    for ch in chs:
        gs = pl.BlockSpec((_SB, _W * SSM_GPS), lambda g, s, ch=ch: (ch(s), g))
        in_specs += [gs, pl.BlockSpec((_SB, SSM_N * SSM_GPS), lambda g, s, ch=ch: (ch(s), _XBC_B + g)),
                     pl.BlockSpec((_SB, SSM_N * SSM_GPS), lambda g, s, ch=ch: (ch(s), _XBC_C + g)), gs,
                     pl.BlockSpec((SSM_HPG * SSM_GPS, _SB), lambda g, s, ch=ch: (g, ch(s)))]
        out_specs += [gs, pl.BlockSpec((1, 1, SSM_CPS, SSM_N * SSM_GPS, _W), lambda g, s: (g, s, 0, 0, 0))]
    return _call(
        body, grid=(ng, nb), in_specs=in_specs, out_specs=out_specs,
        out_shape=[jax.ShapeDtypeStruct((A, SSM_INNER), BF16),
                   jax.ShapeDtypeStruct((ng, nb, SSM_CPS, SSM_N * SSM_GPS, _W), F32)] * 2,
        scratch=[pltpu.VMEM((SSM_N * SSM_GPS, _W), F32)] * 2,
        sem=("parallel", "arbitrary"), name=name,
        args=(x_f, xbc, xbc, cum_f, cum_tf, x_b, xbc, xbc, cum_b, cum_tb), hosted=hosted)


def _ssd_bwd(xd, xbc, cum, cum_t, saved, d_y, prev, ncc, rev, name):
    A = xd.shape[0]
    nc = A // SSM_C
    ng = SSM_G // SSM_GPS
    st_of = lambda r: nc - 1 - r
    ch = lambda r: _chunk_of(st_of(r), nc, ncc, rev)
    gs = pl.BlockSpec((SSM_C, _W * SSM_GPS), lambda g, r: (ch(r), g))
    ns = pl.BlockSpec((SSM_C, SSM_N * SSM_GPS), lambda g, r: (ch(r), g))
    ts = pl.BlockSpec((SSM_HPG * SSM_GPS, SSM_C), lambda g, r: (g, ch(r)))
    n_prev = 0 if prev is None else 2

    def body(*refs):
        x_ref, b_ref, c_ref, a_ref, at_ref, ss_ref, dy_ref = refs[:7]
        p_refs = refs[7:7 + n_prev]
        dx_ref, db_ref, dc_ref, da_ref, dat_ref, ds_ref = refs[7 + n_prev:]

        @pl.when(pl.program_id(1) == 0)
        def _():
            ds_ref[...] = jnp.zeros_like(ds_ref)

        _, vjp = jax.vjp(functools.partial(_ssd_steps, rev=rev), ss_ref[0, 0, 0], x_ref[...].astype(F32),
                         b_ref[...].astype(F32), c_ref[...].astype(F32), a_ref[...], at_ref[...])
        ds, dx, db, dc, da, dat = vjp((ds_ref[...], dy_ref[...].astype(F32)))
        if n_prev:
            db, dc = db + p_refs[0][...].astype(F32), dc + p_refs[1][...].astype(F32)
        for r, d in zip((dx_ref, db_ref, dc_ref, da_ref, dat_ref), (dx, db, dc, da, dat)):
            r[...] = d.astype(r.dtype)
        ds_ref[...] = ds

    in_specs = [gs, pl.BlockSpec((SSM_C, SSM_N * SSM_GPS), lambda g, r: (ch(r), _XBC_B + g)),
                pl.BlockSpec((SSM_C, SSM_N * SSM_GPS), lambda g, r: (ch(r), _XBC_C + g)), gs, ts,
                pl.BlockSpec((1, 1, 1, SSM_N * SSM_GPS, _W),
                             lambda g, r: (g, st_of(r) // SSM_CPS, st_of(r) % SSM_CPS, 0, 0)), gs]
    in_specs += [ns, ns] if n_prev else []
    return pl.pallas_call(
        body, grid=(ng, nc), in_specs=in_specs, out_specs=[gs, ns, ns, gs, ts],
        out_shape=[jax.ShapeDtypeStruct((A, SSM_INNER), BF16), jax.ShapeDtypeStruct((A, SSM_G * SSM_N), BF16),
                   jax.ShapeDtypeStruct((A, SSM_G * SSM_N), BF16), jax.ShapeDtypeStruct((A, SSM_INNER), F32),
                   jax.ShapeDtypeStruct((SSM_G * SSM_HPG, A), F32)],
        scratch_shapes=[pltpu.VMEM((SSM_N * SSM_GPS, _W), F32)],
        compiler_params=_cparams(("parallel", "arbitrary")), name=name,
    )(xd, xbc, xbc, cum, cum_t, saved, d_y, *(prev or ()))


_ADA_ROWS = 16


def _ada_fwd(cc, w_shard, b_shard):
    n = w_shard.shape[1]

    def body(cc_ref, w_ref, b_ref, o_ref):
        o_ref[...] = _dot_impl(_silu(cc_ref[...]), w_ref[...], False, False) + b_ref[...]

    return pl.pallas_call(body, out_shape=jax.ShapeDtypeStruct((_ADA_ROWS, n), F32),
                          compiler_params=_cparams(()), name="ada_fwd")(cc, w_shard, b_shard)


def _ada_bwd(cc, w_shard, d_all, d_mine):
    n = w_shard.shape[1]

    def rows(ref, r):
        parts = []
        for s in range(N_DEV):
            parts.append(ref[6 * s + r:6 * s + r + 1] + ref[6 * s + 2 + r:6 * s + 3 + r] + ref[6 * s + 4 + r:6 * s + 5 + r])
        return parts

    def total(parts):
        t = parts[0]
        for p in parts[1:]:
            t = t + p
        return t

    def body(cc_ref, w_ref, da_ref, dm_ref, dw_ref, db_ref, dcc_ref):
        dd = jnp.concatenate(rows(dm_ref, 0) + [total(rows(dm_ref, 1)), jnp.zeros((_ADA_ROWS - N_DEV - 1, n), F32)], axis=0)
        cc = cc_ref[...]
        dw_ref[...] = _dot_impl(_silu(cc), dd, True, False)
        db_ref[...] = total(rows(da_ref, 0)) + total(rows(da_ref, 1))
        _, vjp = jax.vjp(_silu, cc)
        dcc_ref[...] = vjp(_dot_impl(dd, w_ref[...], False, True))[0]

    return pl.pallas_call(
        body, out_shape=[jax.ShapeDtypeStruct((D, n), F32), jax.ShapeDtypeStruct((1, d_all.shape[1]), F32),
                         jax.ShapeDtypeStruct((_ADA_ROWS, D), F32)],
        compiler_params=_cparams(()), name="ada_bwd")(cc, w_shard, d_all, d_mine)


def _local_step(x, ctx, target, ada, w, shards):
    T, Tc = x.shape[0], ctx.shape[0]
    assert Tc == TM and T % TM == 0 and GRID_W == GLA_C and TS == SSM_C == 2 * GLA_C
    A = T + Tc
    nct = Tc // TM
    n_tm, n_ts = A // TM, A // TS
    tl = _pick(A, 3 * TM, TM)
    n_tl = A // tl
    g = {}

    x_in = [_ctx(ctx, TM, nct), _lat(x, TM, nct)]
    f_norm1 = _f_norm1(nct)
    p_norm1 = [w["norm1_w"], ada]
    (h1,) = _stage_fwd(f_norm1, n_tm, x_in, p_norm1, [(A, D, BF16, TM)], "norm1")
    proj, ag_pa, ag_pb, ag_out = _mm(h1, w["wt_in"], trans_b=True, out_dtype=BF16, name="mm_in", tn_t=1152, tm_t=2816,
                                     hosted=[(shards[n], False) for n in ("w_pa", "w_pb", "w_out")])
    small = _mm(h1, w["wt_in"][C_S:], trans_b=True, out_dtype=F32, name="mm_in_small")

    p_gd = [w["up_f"], w["up_b"], w["gla_bias_f"], w["gla_bias_b"], w["dtb"], w["alog"]]
    f_conv = _f_conv(nct, Tc)
    p_conv = [w["conv_w"], w["conv_b"]]
    in_conv = [_col(proj, TM, 3072, C_U // 3072)]
    xbc, conv_pre = _stage_fwd(f_conv, n_tm, in_conv, p_conv, [(A, 3072, BF16, TM)] * 2, "conv")
    in_gd = [_col(small, TS), _col(xbc, TS, SSM_INNER, 0)]
    n_heads = SSM_G * SSM_HPG
    lg_f, lg_b, xf, xb, axf, axb, atf, atb = _stage_fwd(
        _f_gates_dt, n_ts, in_gd, p_gd,
        [(A, 512, F32, TS)] * 2 + [(A, SSM_INNER, BF16, TS)] * 2 + [(A, SSM_INNER, F32, TS)] * 2
        + [(A, n_heads, F32, TS, True)] * 2,
        "gates_dt")

    ncc_g, ncc_s = Tc // GLA_C, Tc // SSM_C
    ogf, sv_gf, ogb, sv_gb = _gla_fwd(proj, lg_f, lg_b, ncc_g, "gla_fb")
    ysf, sv_sf, ysb, sv_sb, ag_gate, ag_up, ag_down = _ssd_fwd(
        xf, xb, xbc, axf, axb, atf, atb, ncc_s, "ssd_fb", hosted=[(shards[n], False) for n in ("w_gate", "w_up", "w_down")])
    w = dict(w, w_pa=ag_pa.reshape(D, D), w_pb=ag_pb.reshape(SSM_INNER, D), w_out=ag_out.reshape(D, D),
             wt_gu=jnp.concatenate([ag_gate.reshape(D_FF, D), ag_up.reshape(D_FF, D)], axis=0), w_down=ag_down.reshape(D_FF, D))
    got = {}

    in_gp = [_col(ogf, tl), _col(ogb, tl), _col(proj, tl, 1024, C_R // 1024)]
    p_gp = [w["gla_norm_w"]]
    (oa,) = _stage_fwd(_f_gla_post, n_tl, in_gp, p_gp, [(A, D, BF16, tl)], "gla_post")
    in_sp = [_col(ysf, TM), _col(ysb, TM), _col(xbc, TM, SSM_INNER, 0), _col(proj, TM, SSM_INNER, C_Z // SSM_INNER)]
    p_sp = [w["dsk8"], w["ssm_norm_w"]]
    (ob,) = _stage_fwd(_f_ssd_post, n_tm, in_sp, p_sp, [(A, SSM_INNER, BF16, TM)], "ssd_post")
    ya = _mm(oa, w["w_pa"], trans_b=False, out_dtype=BF16, name="mm_pa")
    yb = _mm(ob, w["w_pb"], trans_b=False, out_dtype=BF16, name="mm_pb")
    in_mg = [_col(proj, tl, 1024, C_GA // 1024), _col(proj, tl, 1024, C_GB // 1024), _col(ya, tl), _col(yb, tl)]
    (merged,) = _stage_fwd(_f_merge, n_tl, in_mg, [], [(A, D, BF16, tl)], "merge")
    mix = _mm(merged, w["w_out"], trans_b=False, out_dtype=BF16, name="mm_out")

    f_res1 = _f_res1(nct)
    in_r1 = x_in + [_col(mix, TM)]
    p_r1 = [ada, w["norm2_w"]]
    h2, hm2 = _stage_fwd(f_res1, n_tm, in_r1, p_r1, [(A, D, F32, TM), (A, D, BF16, TM)], "res1")
    gu = _mm(hm2, w["wt_gu"], trans_b=True, out_dtype=BF16, name="mm_gu", tn_t=1408)
    in_sw = [_col(gu, TM)]
    (act,) = _stage_fwd(_f_swiglu, n_tm, in_sw, [], [(A, D_FF, BF16, TM)], "swiglu")
    dn = _mm(act, w["w_down"], trans_b=False, out_dtype=BF16, name="mm_down")

    loss_blk, d_h2a, d_dn, d_ada3, g["final_norm_w"] = _loss_head(h2, dn, target, ada, w["final_norm_w"], nct)
    rows16 = lambda gw: (gw.reshape(N_DEV, gw.shape[0] // N_DEV, gw.shape[1]).astype(BF16), True)
    gw_down = _mm_tn(act, d_dn, name="dw_down", tr_t=1408)
    d_act = _mm(d_dn, w["w_down"], trans_b=True, out_dtype=BF16, name="dx_down", tn_t=1408)
    d_gu, got["w_down"] = _stage_bwd(_f_swiglu, n_tm, in_sw, [], [_col(d_act, TM)], lambda i, t: t, [(BF16, None)],
                                     "swiglu_b", hosted=[rows16(gw_down)])
    gwt_gu = _mm_tn(d_gu, hm2, name="dw_gu", tr_t=1408)
    d_hm2 = _mm(d_gu, w["wt_gu"], trans_b=False, out_dtype=BF16, name="dx_gu")
    d_x1, d_mix, d_ada2, g["norm2_w"], got["w_gate"] = _stage_bwd(
        f_res1, n_tm, in_r1, p_r1, [_col(d_h2a, TM), _col(d_hm2, TM)], lambda i, t: t,
        [None, (F32, None), (BF16, None)], "res1_b", hosted=[rows16(gwt_gu[:D_FF])])
    gw_out = _mm_tn(merged, d_mix, name="dw_out")
    d_merged = _mm(d_mix, w["w_out"], trans_b=True, out_dtype=BF16, name="dx_out")
    d_ya, d_yb, d_proj, got["w_up"] = _stage_bwd(
        _f_merge, n_tl, in_mg, [], [_col(d_merged, tl)], lambda i, t: t, [None, None, (BF16, None), (BF16, None)], "merge_b",
        hosted=[rows16(gwt_gu[D_FF:])], pack=([0, 1], None, D_INP, C_GA // (2 * D)))
    gw_pa = _mm_tn(oa, d_ya, name="dw_pa")
    gw_pb = _mm_tn(ob, d_yb, name="dw_pb")
    d_oa = _mm(d_ya, w["w_pa"], trans_b=True, out_dtype=BF16, name="dx_pa")
    d_ob = _mm(d_yb, w["w_pb"], trans_b=True, out_dtype=BF16, name="dx_pb")
    d_og, g["gla_norm_w"], d_proj, got["w_out"] = _stage_bwd(
        _f_gla_post, n_tl, in_gp, p_gp, [_col(d_oa, tl)], lambda i, t: t, [(BF16, None), None, None], "gla_post_b",
        hosted=[rows16(gw_out)], pack=([2], d_proj, D_INP, C_R // D))
    d_ys, d_xs_skip, g["dsk8"], g["ssm_norm_w"], d_proj, got["w_pa"], got["w_pb"] = _stage_bwd(
        _f_ssd_post, n_tm, in_sp, p_sp, [_col(d_ob, TM)], lambda i, t: t,
        [(BF16, None), None, (BF16, None), None], "ssd_post_b", hosted=[rows16(gw_pa), rows16(gw_pb)],
        pack=([3], d_proj, D_INP, C_Z // SSM_INNER))

    dq, dk, dv, d_lgf = _gla_bwd(proj, lg_f, sv_gf, d_og, None, ncc_g, False, "gla_f_b")
    d_proj, d_lgb = _gla_bwd(proj, lg_b, sv_gb, d_og, (dq, dk, dv), ncc_g, True, "gla_b_b", pack=(d_proj, D_INP))
    d_xf, d_bm, d_cm, d_axf, d_atf = _ssd_bwd(xf, xbc, axf, atf, sv_sf, d_ys, None, ncc_s, False, "ssd_f_b")
    d_xb, d_bm, d_cm, d_axb, d_atb = _ssd_bwd(xb, xbc, axb, atb, sv_sb, d_ys, (d_bm, d_cm), ncc_s, True, "ssd_b_b")

    cts_gd = [_col(a, TS) for a in (d_lgf, d_lgb, d_xf, d_xb, d_axf, d_axb)]
    cts_gd += [(a, (n_heads, TS), lambda i: (0, i)) for a in (d_atf, d_atb)]
    d_xs_dt, g["up_f"], g["up_b"], g["gla_bias_f"], g["gla_bias_b"], g["dtb"], g["alog"], d_proj = _stage_bwd(
        _f_gates_dt, n_ts, in_gd, p_gd, cts_gd, lambda i, t: t, [None, (BF16, None)], "gates_dt_b",
        pack=([0], d_proj, D_INP, C_S // 128))
    g["conv_w"], g["conv_b"], d_proj = _conv_bwd(proj, conv_pre, (d_xs_skip, d_xs_dt, d_bm, d_cm), w["conv_w"], d_proj,
                                                 nct, Tc, "conv_b")
    gwt_in = _mm_tn(d_proj, h1, name="dw_in", tr_t=1152)
    d_h1, got["w_in"] = _mm(d_proj, w["wt_in"], trans_b=False, out_dtype=BF16, name="dx_in", tm_t=768, tk_t=3456,
                            hosted=[rows16(_from_padded(gwt_in))])
    grad_x, g["norm1_w"], d_ada1 = _stage_bwd(
        f_norm1, n_tm, x_in, p_norm1, [_col(d_h1, TM)], lambda i, t: t,
        [None, (F32, (d_x1, (TM, D), x_in[1][2]))], "norm1_b")
    return loss_blk, grad_x, g, got, (d_ada1, d_ada2, d_ada3)


def _exchange(x, a2a, name):
    shp = x.shape[1:] if a2a else x.shape

    def body(x_ref, o_ref, send_sems, recv_sems, local_sem):
        ops = _exchange_ops(x_ref, o_ref, send_sems, recv_sems, local_sem, a2a)
        for op in ops:
            op.start()
        for op in ops:
            op.wait()

    return pl.pallas_call(
        body, out_shape=jax.ShapeDtypeStruct((N_DEV,) + tuple(shp), x.dtype),
        in_specs=[pl.BlockSpec(memory_space=pl.ANY)], out_specs=pl.BlockSpec(memory_space=pl.ANY),
        scratch_shapes=[pltpu.SemaphoreType.DMA((N_DEV - 1,)), pltpu.SemaphoreType.DMA((N_DEV - 1,)),
                        pltpu.SemaphoreType.DMA(())],
        name=name,
    )(x)


def _all_gather(x, name):
    def body(x_ref, o_ref, send_sems, recv_sems, local_sem):
        mx, my, mc = lax.axis_index("x"), lax.axis_index("y"), lax.axis_index("c")
        me, sibling = (mx, my, mc), (mx, my, 1 - mc)
        chips = [(1 - mx, my), (mx, 1 - my), (1 - mx, 1 - my)]

        def slot(px, py, pc):
            return o_ref.at[4 * px + 2 * py + pc]

        def copy(k, block, to, src=None):
            return pltpu.make_async_remote_copy(
                src_ref=slot(*block) if src is None else src, dst_ref=slot(*block),
                send_sem=send_sems.at[k], recv_sem=recv_sems.at[k], device_id=to, device_id_type=pl.DeviceIdType.MESH)

        own = pltpu.make_async_copy(x_ref, slot(*me), local_sem)
        own.start()
        first = [copy(0, me, sibling, src=x_ref)] + [copy(1 + j, me, (*chip, mc), src=x_ref) for j, chip in enumerate(chips)]
        for cp in first:
            cp.start()
        passed = [copy(4 + j, (*chip, mc), sibling) for j, chip in enumerate(chips)]
        for j, chip in enumerate(chips):
            copy(1 + j, (*chip, mc), me).wait_recv()
            passed[j].start()
        copy(0, sibling, me).wait_recv()
        for j, chip in enumerate(chips):
            copy(4 + j, (*chip, 1 - mc), me).wait_recv()
        for cp in first + passed:
            cp.wait_send()
        own.wait()

    return pl.pallas_call(
        body, out_shape=jax.ShapeDtypeStruct((N_DEV,) + tuple(x.shape), x.dtype),
        in_specs=[pl.BlockSpec(memory_space=pl.ANY)], out_specs=pl.BlockSpec(memory_space=pl.ANY),
        scratch_shapes=[pltpu.SemaphoreType.DMA((N_DEV - 1,)), pltpu.SemaphoreType.DMA((N_DEV - 1,)),
                        pltpu.SemaphoreType.DMA(())],
        name=name,
    )(x)


def _adamw_math(w, gr, m, v):
    m = ADAM_B1 * m + (1.0 - ADAM_B1) * gr
    v = ADAM_B2 * v + (1.0 - ADAM_B2) * (gr * gr)
    m_hat = m / np.float32(1.0 - ADAM_B1 ** ADAM_STEP)
    v_hat = v / np.float32(1.0 - ADAM_B2 ** ADAM_STEP)
    delta = -ADAM_LR * (m_hat / (jnp.sqrt(v_hat) + ADAM_EPS) + ADAM_WD * w)
    return delta, m, v


def _sum_adamw(parts, w, m, v, name):
    R, C = w.shape
    n_parts = parts.shape[0]
    tr = _pick(R, max(16, (2 << 20) // (4 * C) // 16 * 16), 16)

    def body(p_ref, w_ref, m_ref, v_ref, g_ref, d_ref, mo_ref, vo_ref):
        gr = p_ref[0].astype(F32)
        for k in range(1, n_parts):
            gr = gr + p_ref[k].astype(F32)
        g_ref[...] = gr
        d_ref[...], mo_ref[...], vo_ref[...] = _adamw_math(w_ref[...], gr, m_ref[...], v_ref[...])

    if R % 16 == 0:
        grid = (R // tr,)
        tile, p_spec = pl.BlockSpec((tr, C), lambda i: (i, 0)), pl.BlockSpec((n_parts, tr, C), lambda i: (0, i, 0))
    else:
        tc = _pick(C, max(128, (2 << 20) // (4 * R) // 128 * 128), 128)
        grid = (C // tc,)
        tile, p_spec = pl.BlockSpec((R, tc), lambda i: (0, i)), pl.BlockSpec((n_parts, R, tc), lambda i: (0, 0, i))
    return pl.pallas_call(
        body, grid=grid, in_specs=[p_spec, tile, tile, tile],
        out_specs=[tile] * 4, out_shape=[jax.ShapeDtypeStruct((R, C), F32)] * 4,
        compiler_params=_cparams(("parallel",)), name=name,
    )(parts, w, m, v)


def _sum8(parts, name):
    _, R, C = parts.shape

    def body(p_ref, g_ref):
        gr = p_ref[0]
        for k in range(1, N_DEV):
            gr = gr + p_ref[k]
        g_ref[...] = gr

    return pl.pallas_call(body, out_shape=jax.ShapeDtypeStruct((R, C), F32), name=name)(parts)


def _adamw(w, gr, m, v, name):
    def body(w_ref, g_ref, m_ref, v_ref, d_ref, mo_ref, vo_ref):
        d_ref[...], mo_ref[...], vo_ref[...] = _adamw_math(w_ref[...], g_ref[...], m_ref[...], v_ref[...])

    return pl.pallas_call(body, out_shape=[jax.ShapeDtypeStruct(w.shape, F32)] * 3, name=name)(w, gr, m, v)


def _to_padded(wt_in):
    z = jnp.zeros((32,) + wt_in.shape[1:], wt_in.dtype)
    return jnp.concatenate([wt_in[0:3072], wt_in[5152:8224], wt_in[3104:5152], wt_in[8288:10336],
                            wt_in[3072:3104], wt_in[8224:8288], z], axis=0)


def _from_padded(p):
    return jnp.concatenate([p[0:3072], p[10240:10272], p[6144:8192], p[3072:6144], p[10272:10336], p[8192:10240]], axis=0)


def _unshard_cols(gathered):
    n, r, c = gathered.shape
    return jnp.transpose(gathered, (1, 0, 2)).reshape(r, n * c)


def _lanes(vec, lo):
    return jnp.concatenate([jnp.zeros((1, lo), F32), vec, jnp.zeros((1, 128 - lo - vec.shape[1]), F32)], axis=1)


_SMALL = (("b_ada", 6 * D), ("c_ctx", D), ("norm1_w", D), ("gla_bias_f", 512), ("gla_bias_b", 512), ("gla_norm_w", 256),
          ("conv_b", 3072), ("dt_bias_f", 32), ("dt_bias_b", 32), ("a_log_f", 32), ("a_log_b", 32), ("d_skip", 32),
          ("ssm_norm_w", 2048), ("norm2_w", D), ("final_norm_w", D))
_SHARDED_SMALL = (("gla_up_f", 16 * 512), ("gla_up_b", 16 * 512), ("conv_w", 4 * 3072))


def _pack(vals, names):
    flat = jnp.concatenate([vals[n].reshape(-1).astype(F32) for n, _ in names])
    pad = (-flat.shape[0]) % 1024
    return jnp.concatenate([flat, jnp.zeros((pad,), F32)]).reshape(-1, 128)


def _unpack(packed, names, shapes):
    flat, out, off = packed.reshape(-1), {}, 0
    for n, size in names:
        out[n] = flat[off:off + size].reshape(shapes[n])
        off += size
    return out


def kernel(x, c, ctx, c_ctx, w_ada, b_ada, norm1_w, w_in, gla_up_f, gla_bias_f, gla_up_b, gla_bias_b, gla_norm_w, conv_w, conv_b, dt_bias_f, dt_bias_b, a_log_f, a_log_b, d_skip, ssm_norm_w, w_pa, w_pb, w_out, norm2_w, w_gate, w_up, w_down, final_norm_w, loss_target, m_c_ctx, m_w_ada, m_b_ada, m_norm1_w, m_w_in, m_gla_up_f, m_gla_bias_f, m_gla_up_b, m_gla_bias_b, m_gla_norm_w, m_conv_w, m_conv_b, m_dt_bias_f, m_dt_bias_b, m_a_log_f, m_a_log_b, m_d_skip, m_ssm_norm_w, m_w_pa, m_w_pb, m_w_out, m_norm2_w, m_w_gate, m_w_up, m_w_down, m_final_norm_w, v_c_ctx, v_w_ada, v_b_ada, v_norm1_w, v_w_in, v_gla_up_f, v_gla_bias_f, v_gla_up_b, v_gla_bias_b, v_gla_norm_w, v_conv_w, v_conv_b, v_dt_bias_f, v_dt_bias_b, v_a_log_f, v_a_log_b, v_d_skip, v_ssm_norm_w, v_w_pa, v_w_pb, v_w_out, v_norm2_w, v_w_gate, v_w_up, v_w_down, v_final_norm_w):
    args = dict(locals())
    me = 4 * lax.axis_index("x") + 2 * lax.axis_index("y") + lax.axis_index("c")

    t_names = ("w_in", "w_gate", "w_up")
    local = {n: (jnp.transpose(args[n][0]) if n in t_names else args[n][0]) for n in
             ("w_in", "w_pa", "w_pb", "w_out", "w_gate", "w_up", "w_down")}
    full = {"wt_in": _to_padded(_all_gather(local["w_in"].astype(BF16), "ag_w_in").reshape(D_IN, D))}
    shards = {n: local[n].astype(BF16) for n in ("w_pa", "w_pb", "w_out", "w_gate", "w_up", "w_down")}
    sm = _exchange(jnp.concatenate([gla_up_f.reshape(-1), gla_up_b.reshape(-1), conv_w.reshape(-1), c.reshape(-1)]).reshape(-1, 128),
                   False, "ag_small")
    sm = sm.reshape(N_DEV, -1)
    n_ada = w_ada.shape[2]
    cc = jnp.concatenate([sm[:, 3584:3584 + D], c_ctx[None], jnp.zeros((_ADA_ROWS - N_DEV - 1, D), F32)], axis=0)
    w_ada16 = w_ada[0].astype(BF16)
    ada_cols = _exchange(_ada_fwd(cc, w_ada16, lax.dynamic_slice(b_ada, (0, me * n_ada), (1, n_ada))), False, "ag_ada")
    ada = jnp.concatenate([lax.dynamic_index_in_dim(ada_cols, me, axis=1, keepdims=False).reshape(1, -1),
                           ada_cols[:, N_DEV].reshape(1, -1), jnp.zeros((6, N_DEV * n_ada), F32)], axis=0)
    up_f = _unshard_cols(sm[:, 0:1024].reshape(N_DEV, 16, 64))
    up_b = _unshard_cols(sm[:, 1024:2048].reshape(N_DEV, 16, 64))
    full["conv_w"] = _unshard_cols(sm[:, 2048:3584].reshape(N_DEV, 4, 384))
    full["up_f"] = jnp.concatenate([up_f, jnp.zeros((112, 512), F32)], axis=0)
    full["up_b"] = jnp.concatenate([jnp.zeros((16, 512), F32), up_b, jnp.zeros((96, 512), F32)], axis=0)
    full["dtb"] = jnp.concatenate([jnp.zeros((1, 32), F32), dt_bias_f, dt_bias_b, jnp.zeros((1, 32), F32)], axis=1)
    full["alog"] = jnp.concatenate([jnp.zeros((1, 32), F32), a_log_f, a_log_b, jnp.zeros((1, 32), F32)], axis=1)
    full["dsk8"] = jnp.concatenate([_lanes(d_skip, 0), jnp.zeros((7, 128), F32)], axis=0)
    for n in ("norm1_w", "gla_bias_f", "gla_bias_b", "gla_norm_w", "conv_b", "ssm_norm_w", "norm2_w"):
        full[n] = args[n]
    full["final_norm_w"] = final_norm_w.reshape(1, D)

    loss_blk, grad_x, g, got, d_ada = _local_step(x[0], ctx[0], loss_target[0], ada, full, shards)

    d_all = _exchange(jnp.concatenate([d[0:2] for d in d_ada], axis=0), False, "ag_d_ada").reshape(N_DEV * 6, N_DEV * n_ada)
    gw_ada, gb_ada, d_cc = _ada_bwd(cc, w_ada16, d_all, lax.dynamic_slice(d_all, (0, me * n_ada), (N_DEV * 6, n_ada)))

    gs = dict(g, c_ctx=d_cc[N_DEV], b_ada=jnp.zeros_like(b_ada))
    gs["dt_bias_f"], gs["dt_bias_b"] = g["dtb"][:, 32:64], g["dtb"][:, 64:96]
    gs["a_log_f"], gs["a_log_b"] = g["alog"][:, 32:64], g["alog"][:, 64:96]
    gs["d_skip"] = g["dsk8"][0:1, 0:32]
    gs["gla_up_f"], gs["gla_up_b"] = g["up_f"][0:16], g["up_b"][16:32]
    gs["loss"] = loss_blk[0:1, 0:1]
    names = _SMALL + _SHARDED_SMALL + (("loss", 1),)
    shapes = {n: (args[n].shape if n in args else (1, 1)) for n, _ in names}
    shapes.update({"gla_up_f": (16, 512), "gla_up_b": (16, 512), "conv_w": (4, 3072)})
    red = _unpack(_sum8(_exchange(_pack(gs, names), False, "ag_small_grads"), "sum_small_grads"), names, shapes)
    loss = red["loss"].reshape(())
    grads = {n: red[n] for n, _ in _SMALL}
    grads["b_ada"] = gb_ada
    grads["gla_up_f"] = lax.dynamic_slice(red["gla_up_f"], (0, me * 64), (16, 64))[None]
    grads["gla_up_b"] = lax.dynamic_slice(red["gla_up_b"], (0, me * 64), (16, 64))[None]
    grads["conv_w"] = lax.dynamic_slice(red["conv_w"], (0, me * 384), (4, 384))[None]
    upd_names = tuple((n, s) for n, s in _SMALL) + (("gla_up_f", 1024), ("gla_up_b", 1024), ("conv_w", 1536))
    pk = lambda prefix, src: _pack({n: src[prefix + n] for n, _ in upd_names}, upd_names)
    d_s, m_s, v_s = _adamw(pk("", args), _pack(grads, upd_names), pk("m_", args), pk("v_", args), "adamw_small")
    upd_shapes = {n: args[n].shape for n, _ in upd_names}
    delta = _unpack(d_s, upd_names, upd_shapes)
    new_m = _unpack(m_s, upd_names, upd_shapes)
    new_v = _unpack(v_s, upd_names, upd_shapes)

    got["w_ada"] = gw_ada[None]
    for n, parts in got.items():
        orient = jnp.transpose if n in t_names else (lambda a: a)
        res = _sum_adamw(parts, local[n] if n in local else args[n][0], orient(args["m_" + n][0]), orient(args["v_" + n][0]),
                         "adamw_" + n)
        grads[n], delta[n], new_m[n], new_v[n] = [orient(r)[None] for r in res]

    order = ["c_ctx", "w_ada", "b_ada", "norm1_w", "w_in", "gla_up_f", "gla_bias_f", "gla_up_b", "gla_bias_b", "gla_norm_w",
             "conv_w", "conv_b", "dt_bias_f", "dt_bias_b", "a_log_f", "a_log_b", "d_skip", "ssm_norm_w", "w_pa", "w_pb",
             "w_out", "norm2_w", "w_gate", "w_up", "w_down", "final_norm_w"]
    fix = lambda d: [d[n].reshape(args[n].shape) for n in order]
    return (loss, grad_x[None], *fix(grads), *fix(delta), *fix(new_m), *fix(new_v))
```

```python
import functools

import jax
import jax.numpy as jnp
import numpy as np
from jax import lax
from jax.experimental import pallas as pl
from jax.experimental.pallas import tpu as pltpu

F32 = jnp.float32
BF16 = jnp.bfloat16
N_DEV = 8
D = 1024
EPS = 1e-6
GRID_W = 64
GLA_H, GLA_DK, GLA_DV = 4, 128, 256
GLA_C = 64
GLA_TAU = 16.0
SSM_G, SSM_HPG, SSM_P, SSM_N = 4, 8, 64, 128
SSM_C = 128
SSM_INNER = 2048
D_FF = 2816
D_IN = 10336
D_INP = 10368
C_Q, C_K, C_V, C_R, C_U, C_Z, C_GA, C_GB, C_S = 0, 512, 1024, 2048, 3072, 6144, 8192, 9216, 10240
TM = 256
TS = 128

ADAM_LR, ADAM_B1, ADAM_B2, ADAM_EPS, ADAM_WD, ADAM_STEP = 0.001, 0.9, 0.999, 1e-08, 0.01, 10
VMEM_LIMIT = 56 << 20


def _cparams(sem):
    return pltpu.CompilerParams(dimension_semantics=sem or None, vmem_limit_bytes=VMEM_LIMIT)


def _pick(n, target, mult):
    best = None
    for t in range(mult, min(n, target) + 1, mult):
        if n % t == 0:
            best = t
    return best if best is not None else n


def _dot_impl(a, b, ta, tb):
    dims = (((0 if ta else 1,), (1 if tb else 0,)), ((), ()))
    return lax.dot_general(a.astype(BF16), b.astype(BF16), dims, preferred_element_type=F32)


@functools.partial(jax.custom_vjp, nondiff_argnums=(2, 3))
def _bdot(a, b, ta=False, tb=False):
    return _dot_impl(a, b, ta, tb)


def _bdot_fwd(a, b, ta, tb):
    return _dot_impl(a, b, ta, tb), (a, b)


def _bdot_bwd(ta, tb, res, g):
    a, b = res
    if not ta and not tb:
        return _dot_impl(g, b, False, True), _dot_impl(a, g, True, False)
    if not ta and tb:
        return _dot_impl(g, b, False, False), _dot_impl(g, a, True, False)
    if ta and not tb:
        return _dot_impl(b, g, False, True), _dot_impl(a, g, False, False)
    raise NotImplementedError


_bdot.defvjp(_bdot_fwd, _bdot_bwd)


_NN = (((1,), (0,)), ((), ()))
_NT = (((1,), (1,)), ((), ()))
_TN = (((0,), (0,)), ((), ()))


def _dg2(x, e, x_is_lhs, dims):
    hi = x.astype(BF16)
    lo = (x - hi.astype(F32)).astype(BF16)
    e = e.astype(BF16)
    if x_is_lhs:
        return (lax.dot_general(hi, e, dims, preferred_element_type=F32)
                + lax.dot_general(lo, e, dims, preferred_element_type=F32))
    return (lax.dot_general(e, hi, dims, preferred_element_type=F32)
            + lax.dot_general(e, lo, dims, preferred_element_type=F32))


_EDOT_FWD = {"xe": (True, _NN), "ex": (False, _NN), "ext": (False, _NT)}
_EDOT_BWD = {"xe": (True, _NT), "ex": (False, _TN), "ext": (True, _TN)}


@functools.partial(jax.custom_vjp, nondiff_argnums=(2,))
def _edot(x, e, mode):
    return _dg2(x, e, *_EDOT_FWD[mode])


def _edot_fwd(x, e, mode):
    return _dg2(x, e, *_EDOT_FWD[mode]), e


def _edot_bwd(mode, e, g):
    return _dg2(g, e, *_EDOT_BWD[mode]), None


_edot.defvjp(_edot_fwd, _edot_bwd)


def _shift_impl(u, pos, per, s):
    n = u.shape[0]
    rolled = u if s == 0 else pltpu.roll(u, (-s) % n, 0)
    ok = (pos + s >= 0) & (pos + s < per)
    return jnp.where(ok, rolled, 0.0)


def _rms(x, w):
    return x * lax.rsqrt(jnp.mean(x * x, axis=-1, keepdims=True) + EPS) * w


def _silu(x):
    return x * jax.nn.sigmoid(x)


def _softplus(x):
    return jnp.maximum(x, 0.0) + jnp.log(1.0 + jnp.exp(-jnp.abs(x)))


def _logsig(x):
    return jnp.minimum(x, 0.0) - jnp.log(1.0 + jnp.exp(-jnp.abs(x)))


def _tri(n, rev):
    t = lax.broadcasted_iota(jnp.int32, (n, n), 0)
    s = lax.broadcasted_iota(jnp.int32, (n, n), 1)
    return (s >= t) if rev else (t >= s)


def _head_expand(first_lane):
    l = lax.broadcasted_iota(jnp.int32, (128, SSM_INNER), 0)
    c = lax.broadcasted_iota(jnp.int32, (128, SSM_INNER), 1)
    return (l == first_lane + lax.shift_right_logical(c, 6)).astype(F32)


def _exchange_ops(x_ref, o_ref, send_sems, recv_sems, local_sem, a2a):
    mx, my, mc = lax.axis_index("x"), lax.axis_index("y"), lax.axis_index("c")
    me = 4 * mx + 2 * my + mc
    ops = []
    for k in range(1, N_DEV):
        px = 1 - mx if k & 4 else mx
        py = 1 - my if k & 2 else my
        pc = 1 - mc if k & 1 else mc
        ops.append(pltpu.make_async_remote_copy(
            src_ref=x_ref.at[4 * px + 2 * py + pc] if a2a else x_ref, dst_ref=o_ref.at[me],
            send_sem=send_sems.at[k - 1], recv_sem=recv_sems.at[k - 1],
            device_id=(px, py, pc), device_id_type=pl.DeviceIdType.MESH))
    ops.append(pltpu.make_async_copy(x_ref.at[me] if a2a else x_ref, o_ref.at[me], local_sem))
    return ops


def _call(body, *, grid, in_specs, out_specs, out_shape, scratch=(), sem, name, args, hosted=(), aliases=None):
    n_in, n_out, n_s, n_h = len(in_specs), len(out_specs), len(scratch), len(hosted)
    aliases = aliases or {}
    if not n_h:
        return pl.pallas_call(body, grid=grid, in_specs=list(in_specs), out_specs=list(out_specs),
                              out_shape=list(out_shape), scratch_shapes=list(scratch), input_output_aliases=aliases,
                              compiler_params=_cparams(sem), name=name)(*args)
    b0, c0 = n_in + n_h, n_in + n_h + n_out
    d0 = c0 + n_h

    def wrapped(*refs):
        hx, ho = refs[n_in:b0], refs[c0:d0]
        send_sems, recv_sems, local_sems = refs[d0 + n_s:]
        ids = [pl.program_id(ax) for ax in range(len(grid))]
        first = functools.reduce(jnp.logical_and, [i == 0 for i in ids])
        last = functools.reduce(jnp.logical_and, [i == n - 1 for i, n in zip(ids, grid)])

        def ops():
            return [op for n, (_, a2a) in enumerate(hosted)
                    for op in _exchange_ops(hx[n], ho[n], send_sems.at[n], recv_sems.at[n], local_sems.at[n], a2a)]

        @pl.when(first)
        def _():
            for op in ops():
                op.start()

        body(*refs[:n_in], *refs[b0:c0], *refs[d0:d0 + n_s])

        @pl.when(last)
        def _():
            for op in ops():
                op.wait()

    hbm = pl.BlockSpec(memory_space=pl.ANY)
    h_shapes = [jax.ShapeDtypeStruct((N_DEV,) + tuple(x.shape[1:] if a2a else x.shape), x.dtype) for x, a2a in hosted]
    return pl.pallas_call(
        wrapped, grid=grid, in_specs=list(in_specs) + [hbm] * n_h, out_specs=list(out_specs) + [hbm] * n_h,
        out_shape=list(out_shape) + h_shapes, input_output_aliases=aliases,
        scratch_shapes=list(scratch) + [pltpu.SemaphoreType.DMA((n_h, N_DEV - 1)), pltpu.SemaphoreType.DMA((n_h, N_DEV - 1)),
                                        pltpu.SemaphoreType.DMA((n_h,))],
        compiler_params=_cparams(("arbitrary",) * len(grid)), name=name,
    )(*args, *[x for x, _ in hosted])


def _mm(a, b, *, trans_b, out_dtype, name, tm_t=1408, tn_t=1024, tk_t=2816, hosted=()):
    M, K = a.shape
    N = b.shape[0] if trans_b else b.shape[1]
    tm, tn, tk = _pick(M, tm_t, 8), _pick(N, tn_t, 128), _pick(K, tk_t, 128)
    nk = K // tk
    dims = (((1,), (1,)), ((), ())) if trans_b else (((1,), (0,)), ((), ()))

    def body_one(a_ref, b_ref, o_ref):
        o_ref[...] = lax.dot_general(a_ref[...], b_ref[...], dims, preferred_element_type=F32).astype(out_dtype)

    def body_acc(a_ref, b_ref, o_ref, acc_ref):
        k = pl.program_id(2)
        p = lax.dot_general(a_ref[...], b_ref[...], dims, preferred_element_type=F32)

        @pl.when(k == 0)
        def _():
            acc_ref[...] = p

        @pl.when(k > 0)
        def _():
            acc_ref[...] += p

        @pl.when(k == nk - 1)
        def _():
            o_ref[...] = acc_ref[...].astype(out_dtype)

    b_spec = pl.BlockSpec((tn, tk), lambda j, i, k: (j, k)) if trans_b else pl.BlockSpec((tk, tn), lambda j, i, k: (k, j))
    outs = _call(
        body_one if nk == 1 else body_acc, grid=(N // tn, M // tm, nk),
        in_specs=[pl.BlockSpec((tm, tk), lambda j, i, k: (i, k)), b_spec],
        out_specs=[pl.BlockSpec((tm, tn), lambda j, i, k: (i, j))],
        out_shape=[jax.ShapeDtypeStruct((M, N), out_dtype)],
        scratch=[] if nk == 1 else [pltpu.VMEM((tm, tn), F32)],
        sem=("parallel", "parallel", "arbitrary"), name=name, args=(a, b), hosted=hosted)
    return outs if hosted else outs[0]


def _mm_tn(a, b, *, name, tm_t=2816, tr_t=1024, tn_t=1408, out_dtype=BF16):
    M, R = a.shape
    N = b.shape[1]
    tm, tr, tn = _pick(M, tm_t, 8), _pick(R, tr_t, 128), _pick(N, tn_t, 128)
    nm = M // tm

    def body(a_ref, b_ref, o_ref, acc_ref):
        m = pl.program_id(2)
        p = lax.dot_general(a_ref[...], b_ref[...], (((0,), (0,)), ((), ())), preferred_element_type=F32)

        @pl.when(m == 0)
        def _():
            acc_ref[...] = p

        @pl.when(m > 0)
        def _():
            acc_ref[...] += p

        @pl.when(m == nm - 1)
        def _():
            o_ref[...] = acc_ref[...].astype(out_dtype)

    return pl.pallas_call(
        body, grid=(R // tr, N // tn, nm),
        in_specs=[pl.BlockSpec((tm, tr), lambda r, j, m: (m, r)), pl.BlockSpec((tm, tn), lambda r, j, m: (m, j))],
        out_specs=pl.BlockSpec((tr, tn), lambda r, j, m: (r, j)),
        out_shape=jax.ShapeDtypeStruct((R, N), out_dtype),
        scratch_shapes=[pltpu.VMEM((tr, tn), F32)],
        compiler_params=_cparams(("parallel", "parallel", "arbitrary")), name=name,
    )(a, b)


def _col(arr, tm, width=None, col=0):
    width = arr.shape[1] if width is None else width
    return (arr, (tm, width), lambda i: (i, col))


def _lat(arr, tm, nct):
    return (arr, (tm, arr.shape[1]), lambda i: (jnp.maximum(i - nct, 0), 0))


def _ctx(arr, tm, nct):
    return (arr, (tm, arr.shape[1]), lambda i: (jnp.minimum(i, nct - 1), 0))


def _whole(p):
    return pl.BlockSpec(p.shape, lambda i, nd=p.ndim: (0,) * nd)


def _stage_fwd(f, n_tiles, ins, params, outs, name):
    ni, npar = len(ins), len(params)
    o_specs = [pl.BlockSpec((o[1], o[3]), lambda i: (0, i)) if len(o) > 4 else pl.BlockSpec((o[3], o[1]), lambda i: (i, 0))
               for o in outs]
    o_shapes = [jax.ShapeDtypeStruct((o[1], o[0]) if len(o) > 4 else (o[0], o[1]), o[2]) for o in outs]

    def body(*refs):
        i = pl.program_id(0)
        xs = [r[...].astype(F32) for r in refs[:ni]]
        ps = [r[...] for r in refs[ni:ni + npar]]
        for r, v in zip(refs[ni + npar:], f(i, xs, ps, False)):
            r[...] = v.astype(r.dtype)

    return pl.pallas_call(
        body, grid=(n_tiles,),
        in_specs=[pl.BlockSpec(bs, fn) for _, bs, fn in ins] + [_whole(p) for p in params],
        out_specs=o_specs, out_shape=o_shapes,
        compiler_params=_cparams(("parallel",)), name=name,
    )(*[a for a, _, _ in ins], *params)


def _stage_bwd(f, n_tiles, ins, params, cts, ct_fn, dins, name, hosted=(), pack=None):
    ni, npar, nc = len(ins), len(params), len(cts)
    want = [k for k, d in enumerate(dins) if d is not None]
    extras = [dins[k][1] for k in want if dins[k][1] is not None]
    n_buf = 0 if pack is None or pack[1] is None else 1
    n_main_in = ni + npar + nc + len(extras)

    def body(*refs):
        i = pl.program_id(0)
        xs = [r[...].astype(F32) for r in refs[:ni]]
        ps = [r[...] for r in refs[ni:ni + npar]]
        ct_tiles = [r[...].astype(F32) for r in refs[ni + npar:ni + npar + nc]]
        ex_refs = list(refs[ni + npar + nc:n_main_in])
        out_refs = refs[n_main_in + n_buf:]
        _, vjp = jax.vjp(lambda xs_, ps_: tuple(f(i, xs_, ps_, True)), xs, ps)
        dxs, dps = vjp(tuple(ct_fn(i, ct_tiles)))
        for n, k in enumerate(want):
            v = dxs[k]
            if dins[k][1] is not None:
                v = v + ex_refs.pop(0)[...].astype(F32)
            out_refs[n][...] = v.astype(out_refs[n].dtype)
        if pack is not None:
            packed = [dxs[k] for k in pack[0]]
            r = out_refs[len(want) + npar]
            r[...] = (packed[0] if len(packed) == 1 else jnp.concatenate(packed, axis=1)).astype(r.dtype)
        for r, v in zip(out_refs[len(want):len(want) + npar], dps):
            @pl.when(i == 0)
            def _(r=r, v=v):
                r[...] = v

            @pl.when(i > 0)
            def _(r=r, v=v):
                r[...] += v

    in_specs = ([pl.BlockSpec(bs, fn) for _, bs, fn in ins] + [_whole(p) for p in params]
                + [pl.BlockSpec(bs, fn) for _, bs, fn in cts] + [pl.BlockSpec(bs, fn) for _, bs, fn in extras])
    out_specs = ([pl.BlockSpec(ins[k][1], lambda i, fn=ins[k][2]: (fn(i)[0], 0)) for k in want] + [_whole(p) for p in params])
    out_shape = ([jax.ShapeDtypeStruct((ins[k][0].shape[0], ins[k][1][1]), dins[k][0]) for k in want]
                 + [jax.ShapeDtypeStruct(p.shape, F32) for p in params])
    args = [*[a for a, _, _ in ins], *params, *[a for a, _, _ in cts], *[a for a, _, _ in extras]]
    aliases = None
    if pack is not None:
        ks, buf, total_cols, block = pack
        width = sum(ins[k][1][1] for k in ks)
        out_specs.append(pl.BlockSpec((ins[ks[0]][1][0], width), lambda i, fn=ins[ks[0]][2]: (fn(i)[0], block)))
        out_shape.append(jax.ShapeDtypeStruct((ins[ks[0]][0].shape[0], total_cols), BF16))
        if buf is not None:
            in_specs.append(pl.BlockSpec(memory_space=pl.ANY))
            args.append(buf)
            aliases = {len(args) - 1: len(out_specs) - 1}
    return _call(body, grid=(n_tiles,), in_specs=in_specs, out_specs=out_specs, out_shape=out_shape,
                 sem=("arbitrary",), name=name, args=args, hosted=hosted, aliases=aliases)


def _mod_rows(i, nct, ada, lo):
    return jnp.where(i < nct, ada[1:2, lo:lo + D], ada[0:1, lo:lo + D])


def _f_norm1(nct):
    def f(i, xs, ps, diff):
        ctx, x = xs
        nw, ada = ps
        xt = jnp.where(i < nct, ctx, x)
        return (_rms(xt, nw) * (1.0 + _mod_rows(i, nct, ada, D)) + _mod_rows(i, nct, ada, 0),)
    return f


def _f_gates_dt(i, xs, ps, diff):
    small, xc = xs
    up_f, up_b, gb_f, gb_b, dtb, alog = ps
    dot = _bdot if diff else _dot_impl_nn
    lg_f = _logsig(dot(small, up_f) + gb_f) * (1.0 / GLA_TAU)
    lg_b = _logsig(dot(small, up_b) + gb_b) * (1.0 / GLA_TAU)
    dtp = _softplus(small + dtb)
    aa = -jnp.exp(alog) * dtp
    e_f, e_b = _head_expand(32), _head_expand(64)
    n = small.shape[0]
    t = lax.broadcasted_iota(jnp.int32, (n, n), 0)
    s = lax.broadcasted_iota(jnp.int32, (n, n), 1)
    same = lax.shift_right_logical(t, 6) == lax.shift_right_logical(s, 6)
    pre_g, suf_g = ((t >= s) & same).astype(F32), ((s >= t) & same).astype(F32)
    pre_s, suf_s = (t >= s).astype(F32), (s >= t).astype(F32)
    cum_f, cum_b = _edot(aa, pre_s, "ex"), _edot(aa, suf_s, "ex")
    h = lax.broadcasted_iota(jnp.int32, (SSM_G * SSM_HPG, 128), 0)
    l = lax.broadcasted_iota(jnp.int32, (SSM_G * SSM_HPG, 128), 1)
    cum_ft = _edot(cum_f, (l == h + 32).astype(F32), "ext")
    cum_bt = _edot(cum_b, (l == h + 64).astype(F32), "ext")
    return (_edot(lg_f, pre_g, "ex"), _edot(lg_b, suf_g, "ex"), xc * _bdot(dtp, e_f), xc * _bdot(dtp, e_b),
            _edot(cum_f, e_f, "xe"), _edot(cum_b, e_b, "xe"), cum_ft, cum_bt)


def _dot_impl_nn(a, b):
    return _dot_impl(a, b, False, False)


def _f_conv(nct, tc):
    def f(i, xs, ps, diff):
        (u,) = xs
        cw, cb = ps
        n = u.shape[0]
        t = lax.broadcasted_iota(jnp.int32, (n, 1), 0)
        per = jnp.where(i < nct, tc, GRID_W)
        pos = jnp.bitwise_and(t, per - 1)
        acc = cb + _shift_impl(u, pos, per, -2) * cw[0:1]
        for j in range(1, 4):
            acc = acc + _shift_impl(u, pos, per, j - 2) * cw[j:j + 1]
        return (_silu(acc), acc)
    return f


def _conv_bwd(proj, pre, cts, cw, buf, nct, tc, name):
    A = proj.shape[0]
    width = cw.shape[1]

    def body(u_ref, p_ref, c0, c1, c2, c3, cw_ref, buf_ref, dcw_ref, dcb_ref, o_ref):
        i = pl.program_id(0)
        t = lax.broadcasted_iota(jnp.int32, (TM, 1), 0)
        per = jnp.where(i < nct, tc, GRID_W)
        pos = jnp.bitwise_and(t, per - 1)
        pre_ = p_ref[...].astype(F32)
        s = jax.nn.sigmoid(pre_)
        g = jnp.concatenate([c0[...].astype(F32) + c1[...].astype(F32), c2[...].astype(F32), c3[...].astype(F32)], axis=1)
        g = g * (s * (1.0 + pre_ * (1.0 - s)))
        u = u_ref[...].astype(F32)
        w_ = cw_ref[...]
        du, rows = None, []
        for j in range(4):
            gj = g if j == 2 else _shift_impl(g, pos, per, 2 - j)
            du = gj * w_[j:j + 1] if du is None else du + gj * w_[j:j + 1]
            rows.append(jnp.sum(gj * u, axis=0, keepdims=True))
        o_ref[...] = du.astype(o_ref.dtype)
        for r, v in ((dcw_ref, jnp.concatenate(rows, axis=0)), (dcb_ref, jnp.sum(g, axis=0, keepdims=True))):
            @pl.when(i == 0)
            def _(r=r, v=v):
                r[...] = v

            @pl.when(i > 0)
            def _(r=r, v=v):
                r[...] += v

    blk = C_U // width
    tile = lambda a: pl.BlockSpec((TM, a.shape[1]), lambda i: (i, 0))
    return _call(
        body, grid=(A // TM,),
        in_specs=[pl.BlockSpec((TM, width), lambda i: (i, blk)), tile(pre)] + [tile(c) for c in cts]
        + [_whole(cw), pl.BlockSpec(memory_space=pl.ANY)],
        out_specs=[_whole(cw), pl.BlockSpec((1, width), lambda i: (0, 0)), pl.BlockSpec((TM, width), lambda i: (i, blk))],
        out_shape=[jax.ShapeDtypeStruct(cw.shape, F32), jax.ShapeDtypeStruct((1, width), F32),
                   jax.ShapeDtypeStruct(buf.shape, buf.dtype)],
        sem=("arbitrary",), name=name, args=(proj, pre, *cts, cw, buf), aliases={7: 2})


def _f_gla_post(i, xs, ps, diff):
    ogf, ogb, r = xs
    (gw,) = ps
    o = ogf + ogb
    parts = [_rms(o[:, h * GLA_DV:(h + 1) * GLA_DV], gw) for h in range(GLA_H)]
    return (jnp.concatenate(parts, axis=1) * _silu(r),)


def _f_ssd_post(i, xs, ps, diff):
    yf, yb, xc, z = xs
    dsk8, nw = ps
    dsk = _edot(dsk8, _head_expand(0), "xe")[0:1]
    y = (yf + yb + dsk * xc) * _silu(z)
    w = SSM_INNER // SSM_G
    parts = [_rms(y[:, g * w:(g + 1) * w], nw[:, g * w:(g + 1) * w]) for g in range(SSM_G)]
    return (jnp.concatenate(parts, axis=1),)


def _f_merge(i, xs, ps, diff):
    ga, gb, ya, yb = xs
    return (jax.nn.sigmoid(ga) * ya + jax.nn.sigmoid(gb) * yb,)


def _f_res1(nct):
    def f(i, xs, ps, diff):
        ctx, x, mix = xs
        ada, nw = ps
        h2 = jnp.where(i < nct, ctx, x) + _mod_rows(i, nct, ada, 2 * D) * mix
        return (h2, _rms(h2, nw) * (1.0 + _mod_rows(i, nct, ada, 4 * D)) + _mod_rows(i, nct, ada, 3 * D))
    return f


def _f_swiglu(i, xs, ps, diff):
    (gu,) = xs
    return (_silu(gu[:, :D_FF]) * gu[:, D_FF:],)


def _loss_head(h2, dn, tgt, ada, fw, nct):
    A = h2.shape[0]
    n_tiles = A // TM

    def tile_loss(i, h2t, dnt, tg, ada_, fw_):
        h3 = h2t + ada_[0:1, 5 * D:6 * D] * dnt
        err = _rms(h3, fw_) - tg
        row = 0.5 * jnp.mean(err * err, axis=-1, keepdims=True)
        return jnp.sum(row, axis=0, keepdims=True) * jnp.where(i < nct, 0.0, 1.0)

    def body(h2_ref, dn_ref, tg_ref, ada_ref, fw_ref, loss_ref, dh_ref, ddn_ref, dada_ref, dfw_ref):
        i = pl.program_id(0)
        val, vjp = jax.vjp(functools.partial(tile_loss, i), h2_ref[...], dn_ref[...].astype(F32), tg_ref[...], ada_ref[...],
                           fw_ref[...])
        dh, ddn, _, dada, dfw = vjp(jnp.ones((1, 1), F32))
        dh_ref[...] = dh
        ddn_ref[...] = ddn.astype(BF16)
        lv = jnp.broadcast_to(val, loss_ref.shape)
        for r, v in ((loss_ref, lv), (dada_ref, dada), (dfw_ref, dfw)):
            @pl.when(i == 0)
            def _(r=r, v=v):
                r[...] = v

            @pl.when(i > 0)
            def _(r=r, v=v):
                r[...] += v

    row = lambda i: (i, 0)
    return pl.pallas_call(
        body, grid=(n_tiles,),
        in_specs=[pl.BlockSpec((TM, D), row), pl.BlockSpec((TM, D), row),
                  pl.BlockSpec((TM, D), lambda i: (jnp.maximum(i - nct, 0), 0)), _whole(ada), _whole(fw)],
        out_specs=[pl.BlockSpec((8, 128), lambda i: (0, 0)), pl.BlockSpec((TM, D), row), pl.BlockSpec((TM, D), row),
                   _whole(ada), _whole(fw)],
        out_shape=[jax.ShapeDtypeStruct((8, 128), F32), jax.ShapeDtypeStruct((A, D), F32),
                   jax.ShapeDtypeStruct((A, D), BF16), jax.ShapeDtypeStruct(ada.shape, F32),
                   jax.ShapeDtypeStruct(fw.shape, F32)],
        compiler_params=_cparams(("arbitrary",)), name="loss_head",
    )(h2, dn, tgt, ada, fw)


def _chunk_of(step, n_chunks, n_ctx_chunks, rev):
    if not rev:
        return step
    return jnp.where(step < n_ctx_chunks, n_ctx_chunks - 1 - step, n_chunks - 1 - (step - n_ctx_chunks))


def _gla_step(st, q, k, v, b, rev):
    tri = _tri(GLA_C, rev).astype(F32)
    last = 0 if rev else GLA_C - 1
    outs, sts = [], []
    for h in range(GLA_H):
        kk = slice(h * GLA_DK, (h + 1) * GLA_DK)
        vv = slice(h * GLA_DV, (h + 1) * GLA_DV)
        qh, kh, vh, bh, sh = q[:, kk] * (GLA_DK ** -0.5), k[:, kk], v[:, vv], b[:, kk], st[vv]
        tot = bh[last:last + 1]
        mid = bh[GLA_C // 2:GLA_C // 2 + 1]
        att = _bdot(qh * jnp.exp(bh - mid), kh * jnp.exp(mid - bh), False, True) * tri
        outs.append(_bdot(att, vh) + _bdot(qh * jnp.exp(bh), sh, False, True))
        sts.append(sh * jnp.exp(tot) + _bdot(vh, kh * jnp.exp(tot - bh), True, False))
    return jnp.concatenate(sts, axis=0), jnp.concatenate(outs, axis=1)


_GQK, _GV = GLA_H * GLA_DK, GLA_H * GLA_DV


GLA_CPS = 4
_GB = GLA_C * GLA_CPS


def _gla_order(rev):
    return list(reversed(range(GLA_CPS))) if rev else list(range(GLA_CPS))


def _gla_fwd(proj, bg_f, bg_b, ncc, name, hosted=()):
    A = proj.shape[0]
    nb = A // _GB
    chs = [lambda s, rev=rev: _chunk_of(s, nb, ncc // GLA_CPS, rev) for rev in (False, True)]

    def body(*refs):
        ins, outs, sts = refs[:8], refs[8:12], refs[12:]

        for d, rev in enumerate((False, True)):
            q_ref, k_ref, v_ref, b_ref = ins[4 * d:4 * d + 4]
            o_ref, ss_ref = outs[2 * d:2 * d + 2]
            st_ref = sts[d]

            @pl.when(pl.program_id(0) == 0)
            def _(st_ref=st_ref):
                st_ref[...] = jnp.zeros_like(st_ref)

            st = st_ref[...]
            for pos, sub in enumerate(_gla_order(rev)):
                rows = slice(sub * GLA_C, (sub + 1) * GLA_C)
                ss_ref[0, pos] = st
                st, o = _gla_step(st, q_ref[rows].astype(F32), k_ref[rows].astype(F32), v_ref[rows].astype(F32),
                                  b_ref[rows], rev)
                o_ref[rows] = o.astype(o_ref.dtype)
            st_ref[...] = st

    in_specs, out_specs = [], []
    for ch in chs:
        in_specs += [pl.BlockSpec((_GB, _GQK), lambda s, ch=ch: (ch(s), C_Q // _GQK)),
                     pl.BlockSpec((_GB, _GQK), lambda s, ch=ch: (ch(s), C_K // _GQK)),
                     pl.BlockSpec((_GB, _GV), lambda s, ch=ch: (ch(s), C_V // _GV)),
                     pl.BlockSpec((_GB, _GQK), lambda s, ch=ch: (ch(s), 0))]
        out_specs += [pl.BlockSpec((_GB, _GV), lambda s, ch=ch: (ch(s), 0)),
                      pl.BlockSpec((1, GLA_CPS, _GV, GLA_DK), lambda s: (s, 0, 0, 0))]
    return _call(
        body, grid=(nb,), in_specs=in_specs, out_specs=out_specs,
        out_shape=[jax.ShapeDtypeStruct((A, _GV), BF16), jax.ShapeDtypeStruct((nb, GLA_CPS, _GV, GLA_DK), F32)] * 2,
        scratch=[pltpu.VMEM((_GV, GLA_DK), F32)] * 2, sem=("arbitrary",), name=name,
        args=(proj, proj, proj, bg_f, proj, proj, proj, bg_b), hosted=hosted)


def _gla_bwd(proj, lg, saved, d_o, prev, ncc, rev, name, pack=None):
    A = proj.shape[0]
    nb = A // _GB
    st_of = lambda r: nb - 1 - r
    ch = lambda r: _chunk_of(st_of(r), nb, ncc // GLA_CPS, rev)
    qs = pl.BlockSpec((_GB, _GQK), lambda r: (ch(r), 0))
    vs = pl.BlockSpec((_GB, _GV), lambda r: (ch(r), 0))
    n_prev = 0 if prev is None else 3

    n_buf = 0 if pack is None else 1

    def body(*refs):
        q_ref, k_ref, v_ref, b_ref, ss_ref, do_ref = refs[:6]
        p_refs = refs[6:6 + n_prev]
        out_refs = refs[6 + n_prev + n_buf:]
        ds_ref = out_refs[-1]

        @pl.when(pl.program_id(0) == 0)
        def _():
            ds_ref[...] = jnp.zeros_like(ds_ref)

        ds = ds_ref[...]
        for pos, sub in reversed(list(enumerate(_gla_order(rev)))):
            rows = slice(sub * GLA_C, (sub + 1) * GLA_C)
            _, vjp = jax.vjp(functools.partial(_gla_step, rev=rev), ss_ref[0, pos], q_ref[rows].astype(F32),
                             k_ref[rows].astype(F32), v_ref[rows].astype(F32), b_ref[rows])
            ds, dq, dk, dv, db = vjp((ds, do_ref[rows].astype(F32)))
            if n_prev:
                dq, dk, dv = [d + p[rows].astype(F32) for d, p in zip((dq, dk, dv), p_refs)]
            if pack is None:
                for r, d in zip(out_refs[:4], (dq, dk, dv, db)):
                    r[rows] = d.astype(r.dtype)
            else:
                out_refs[0][rows] = jnp.concatenate([dq, dk, dv], axis=1).astype(out_refs[0].dtype)
                out_refs[1][rows] = db
        ds_ref[...] = ds

    in_specs = [
        pl.BlockSpec((_GB, _GQK), lambda r: (ch(r), C_Q // _GQK)),
        pl.BlockSpec((_GB, _GQK), lambda r: (ch(r), C_K // _GQK)),
        pl.BlockSpec((_GB, _GV), lambda r: (ch(r), C_V // _GV)),
        qs,
        pl.BlockSpec((1, GLA_CPS, _GV, GLA_DK), lambda r: (st_of(r), 0, 0, 0)),
        vs,
    ] + ([qs, qs, vs] if n_prev else [])
    args = [proj, proj, proj, lg, saved, d_o, *(prev or ())]
    if pack is None:
        out_specs = [qs, qs, vs, qs]
        out_shape = [jax.ShapeDtypeStruct((A, _GQK), BF16), jax.ShapeDtypeStruct((A, _GQK), BF16),
                     jax.ShapeDtypeStruct((A, _GV), BF16), jax.ShapeDtypeStruct((A, _GQK), F32)]
        aliases = None
    else:
        in_specs.append(pl.BlockSpec(memory_space=pl.ANY))
        args.append(pack[0])
        out_specs = [pl.BlockSpec((_GB, 2 * _GQK + _GV), lambda r: (ch(r), 0)), qs]
        out_shape = [jax.ShapeDtypeStruct((A, pack[1]), BF16), jax.ShapeDtypeStruct((A, _GQK), F32)]
        aliases = {len(args) - 1: 0}
    return _call(body, grid=(nb,), in_specs=in_specs, out_specs=out_specs, out_shape=out_shape,
                 scratch=[pltpu.VMEM((_GV, GLA_DK), F32)], sem=("arbitrary",), name=name, args=args, aliases=aliases)


def _ssd_step(st, x, bm, cm, cum, cum_t, rev):
    mask = _tri(SSM_C, rev)
    last = 0 if rev else SSM_C - 1
    tot = cum[last:last + 1]
    cb = _bdot(cm, bm, False, True)
    ys = []
    for e in range(SSM_HPG):
        lm = jnp.exp(jnp.where(mask, cum[:, e * SSM_P:e * SSM_P + 1] - cum_t[e:e + 1], -jnp.inf))
        ys.append(_bdot(cb * lm, x[:, e * SSM_P:(e + 1) * SSM_P]))
    y = jnp.concatenate(ys, axis=1) + _bdot(cm, st) * jnp.exp(cum)
    st_new = st * jnp.exp(tot) + _bdot(bm, x * jnp.exp(tot - cum), True, False)
    return st_new, y


SSM_GPS = 4
_W = SSM_HPG * SSM_P
_XBC_B = SSM_INNER // (SSM_N * SSM_GPS)
_XBC_C = _XBC_B + SSM_G // SSM_GPS


def _ssd_steps(st, x, bm, cm, cum, cum_t, rev):
    outs = [_ssd_step(st[j * SSM_N:(j + 1) * SSM_N], x[:, j * _W:(j + 1) * _W], bm[:, j * SSM_N:(j + 1) * SSM_N],
                      cm[:, j * SSM_N:(j + 1) * SSM_N], cum[:, j * _W:(j + 1) * _W],
                      cum_t[j * SSM_HPG:(j + 1) * SSM_HPG], rev) for j in range(SSM_GPS)]
    return jnp.concatenate([o[0] for o in outs], axis=0), jnp.concatenate([o[1] for o in outs], axis=1)


SSM_CPS = 2
_SB = SSM_C * SSM_CPS


def _ssd_order(rev):
    return list(reversed(range(SSM_CPS))) if rev else list(range(SSM_CPS))


def _ssd_fwd(x_f, x_b, xbc, cum_f, cum_b, cum_tf, cum_tb, ncc, name, hosted=()):
    A = x_f.shape[0]
    nb = A // _SB
    ng = SSM_G // SSM_GPS
    chs = [lambda s, rev=rev: _chunk_of(s, nb, ncc // SSM_CPS, rev) for rev in (False, True)]

    def body(*refs):
        ins, outs, sts = refs[:10], refs[10:14], refs[14:]
        for d, rev in enumerate((False, True)):
            x_ref, b_ref, c_ref, a_ref, at_ref = ins[5 * d:5 * d + 5]
            y_ref, ss_ref = outs[2 * d:2 * d + 2]
            st_ref = sts[d]

            @pl.when(pl.program_id(1) == 0)
            def _(st_ref=st_ref):
                st_ref[...] = jnp.zeros_like(st_ref)

            st = st_ref[...]
            for pos, sub in enumerate(_ssd_order(rev)):
                rows = slice(sub * SSM_C, (sub + 1) * SSM_C)
                ss_ref[0, 0, pos] = st
                st, y = _ssd_steps(st, x_ref[rows].astype(F32), b_ref[rows].astype(F32), c_ref[rows].astype(F32),
                                   a_ref[rows], at_ref[:, rows], rev)
                y_ref[rows] = y.astype(y_ref.dtype)
            st_ref[...] = st

    in_specs, out_specs = [], []
    for ch in chs:
        gs = pl.BlockSpec((_SB, _W * SSM_GPS), lambda g, s, ch=ch: (ch(s), g))
        in_specs += [gs, pl.BlockSpec((_SB, SSM_N * SSM_GPS), lambda g, s, ch=ch: (ch(s), _XBC_B + g)),
                     pl.BlockSpec((_SB, SSM_N * SSM_GPS), lambda g, s, ch=ch: (ch(s), _XBC_C + g)), gs,
                     pl.BlockSpec((SSM_HPG * SSM_GPS, _SB), lambda g, s, ch=ch: (g, ch(s)))]
        out_specs += [gs, pl.BlockSpec((1, 1, SSM_CPS, SSM_N * SSM_GPS, _W), lambda g, s: (g, s, 0, 0, 0))]
    return _call(
        body, grid=(ng, nb), in_specs=in_specs, out_specs=out_specs,
        out_shape=[jax.ShapeDtypeStruct((A, SSM_INNER), BF16),
                   jax.ShapeDtypeStruct((ng, nb, SSM_CPS, SSM_N * SSM_GPS, _W), F32)] * 2,
        scratch=[pltpu.VMEM((SSM_N * SSM_GPS, _W), F32)] * 2,
        sem=("parallel", "arbitrary"), name=name,
        args=(x_f, xbc, xbc, cum_f, cum_tf, x_b, xbc, xbc, cum_b, cum_tb), hosted=hosted)


def _ssd_bwd(xd, xbc, cum, cum_t, saved, d_y, prev, ncc, rev, name):
    A = xd.shape[0]
    nc = A // SSM_C
    ng = SSM_G // SSM_GPS
    st_of = lambda r: nc - 1 - r
    ch = lambda r: _chunk_of(st_of(r), nc, ncc, rev)
    gs = pl.BlockSpec((SSM_C, _W * SSM_GPS), lambda g, r: (ch(r), g))
    ns = pl.BlockSpec((SSM_C, SSM_N * SSM_GPS), lambda g, r: (ch(r), g))
    ts = pl.BlockSpec((SSM_HPG * SSM_GPS, SSM_C), lambda g, r: (g, ch(r)))
    n_prev = 0 if prev is None else 2

    def body(*refs):
        x_ref, b_ref, c_ref, a_ref, at_ref, ss_ref, dy_ref = refs[:7]
        p_refs = refs[7:7 + n_prev]
        dx_ref, db_ref, dc_ref, da_ref, dat_ref, ds_ref = refs[7 + n_prev:]

        @pl.when(pl.program_id(1) == 0)
        def _():
            ds_ref[...] = jnp.zeros_like(ds_ref)

        _, vjp = jax.vjp(functools.partial(_ssd_steps, rev=rev), ss_ref[0, 0, 0], x_ref[...].astype(F32),
                         b_ref[...].astype(F32), c_ref[...].astype(F32), a_ref[...], at_ref[...])
        ds, dx, db, dc, da, dat = vjp((ds_ref[...], dy_ref[...].astype(F32)))
        if n_prev:
            db, dc = db + p_refs[0][...].astype(F32), dc + p_refs[1][...].astype(F32)
        for r, d in zip((dx_ref, db_ref, dc_ref, da_ref, dat_ref), (dx, db, dc, da, dat)):
            r[...] = d.astype(r.dtype)
        ds_ref[...] = ds

    in_specs = [gs, pl.BlockSpec((SSM_C, SSM_N * SSM_GPS), lambda g, r: (ch(r), _XBC_B + g)),
                pl.BlockSpec((SSM_C, SSM_N * SSM_GPS), lambda g, r: (ch(r), _XBC_C + g)), gs, ts,
                pl.BlockSpec((1, 1, 1, SSM_N * SSM_GPS, _W),
                             lambda g, r: (g, st_of(r) // SSM_CPS, st_of(r) % SSM_CPS, 0, 0)), gs]
    in_specs += [ns, ns] if n_prev else []
    return pl.pallas_call(
        body, grid=(ng, nc), in_specs=in_specs, out_specs=[gs, ns, ns, gs, ts],
        out_shape=[jax.ShapeDtypeStruct((A, SSM_INNER), BF16), jax.ShapeDtypeStruct((A, SSM_G * SSM_N), BF16),
                   jax.ShapeDtypeStruct((A, SSM_G * SSM_N), BF16), jax.ShapeDtypeStruct((A, SSM_INNER), F32),
                   jax.ShapeDtypeStruct((SSM_G * SSM_HPG, A), F32)],
        scratch_shapes=[pltpu.VMEM((SSM_N * SSM_GPS, _W), F32)],
        compiler_params=_cparams(("parallel", "arbitrary")), name=name,
    )(xd, xbc, xbc, cum, cum_t, saved, d_y, *(prev or ()))


_ADA_ROWS = 16


def _ada_fwd(cc, w_shard, b_shard):
    n = w_shard.shape[1]

    def body(cc_ref, w_ref, b_ref, o_ref):
        o_ref[...] = _dot_impl(_silu(cc_ref[...]), w_ref[...], False, False) + b_ref[...]

    return pl.pallas_call(body, out_shape=jax.ShapeDtypeStruct((_ADA_ROWS, n), F32),
                          compiler_params=_cparams(()), name="ada_fwd")(cc, w_shard, b_shard)


def _ada_bwd(cc, w_shard, d_all, d_mine):
    n = w_shard.shape[1]

    def rows(ref, r):
        parts = []
        for s in range(N_DEV):
            parts.append(ref[6 * s + r:6 * s + r + 1] + ref[6 * s + 2 + r:6 * s + 3 + r] + ref[6 * s + 4 + r:6 * s + 5 + r])
        return parts

    def total(parts):
        t = parts[0]
        for p in parts[1:]:
            t = t + p
        return t

    def body(cc_ref, w_ref, da_ref, dm_ref, dw_ref, db_ref, dcc_ref):
        dd = jnp.concatenate(rows(dm_ref, 0) + [total(rows(dm_ref, 1)), jnp.zeros((_ADA_ROWS - N_DEV - 1, n), F32)], axis=0)
        cc = cc_ref[...]
        dw_ref[...] = _dot_impl(_silu(cc), dd, True, False)
        db_ref[...] = total(rows(da_ref, 0)) + total(rows(da_ref, 1))
        _, vjp = jax.vjp(_silu, cc)
        dcc_ref[...] = vjp(_dot_impl(dd, w_ref[...], False, True))[0]

    return pl.pallas_call(
        body, out_shape=[jax.ShapeDtypeStruct((D, n), F32), jax.ShapeDtypeStruct((1, d_all.shape[1]), F32),
                         jax.ShapeDtypeStruct((_ADA_ROWS, D), F32)],
        compiler_params=_cparams(()), name="ada_bwd")(cc, w_shard, d_all, d_mine)


def _local_step(x, ctx, target, ada, w, shards):
    T, Tc = x.shape[0], ctx.shape[0]
    assert Tc == TM and T % TM == 0 and GRID_W == GLA_C and TS == SSM_C == 2 * GLA_C
    A = T + Tc
    nct = Tc // TM
    n_tm, n_ts = A // TM, A // TS
    tl = _pick(A, 3 * TM, TM)
    n_tl = A // tl
    g = {}

    x_in = [_ctx(ctx, TM, nct), _lat(x, TM, nct)]
    f_norm1 = _f_norm1(nct)
    p_norm1 = [w["norm1_w"], ada]
    (h1,) = _stage_fwd(f_norm1, n_tm, x_in, p_norm1, [(A, D, BF16, TM)], "norm1")
    proj, ag_pa, ag_pb, ag_out = _mm(h1, w["wt_in"], trans_b=True, out_dtype=BF16, name="mm_in", tn_t=1152, tm_t=2816,
                                     hosted=[(shards[n], False) for n in ("w_pa", "w_pb", "w_out")])
    small = _mm(h1, w["wt_in"][C_S:], trans_b=True, out_dtype=F32, name="mm_in_small")

    p_gd = [w["up_f"], w["up_b"], w["gla_bias_f"], w["gla_bias_b"], w["dtb"], w["alog"]]
    f_conv = _f_conv(nct, Tc)
    p_conv = [w["conv_w"], w["conv_b"]]
    in_conv = [_col(proj, TM, 3072, C_U // 3072)]
    xbc, conv_pre = _stage_fwd(f_conv, n_tm, in_conv, p_conv, [(A, 3072, BF16, TM)] * 2, "conv")
    in_gd = [_col(small, TS), _col(xbc, TS, SSM_INNER, 0)]
    n_heads = SSM_G * SSM_HPG
    lg_f, lg_b, xf, xb, axf, axb, atf, atb = _stage_fwd(
        _f_gates_dt, n_ts, in_gd, p_gd,
        [(A, 512, F32, TS)] * 2 + [(A, SSM_INNER, BF16, TS)] * 2 + [(A, SSM_INNER, F32, TS)] * 2
        + [(A, n_heads, F32, TS, True)] * 2,
        "gates_dt")

    ncc_g, ncc_s = Tc // GLA_C, Tc // SSM_C
    ogf, sv_gf, ogb, sv_gb, ag_down = _gla_fwd(proj, lg_f, lg_b, ncc_g, "gla_fb", hosted=[(shards["w_down"], False)])
    ysf, sv_sf, ysb, sv_sb, ag_gate, ag_up = _ssd_fwd(
        xf, xb, xbc, axf, axb, atf, atb, ncc_s, "ssd_fb", hosted=[(shards[n], False) for n in ("w_gate", "w_up")])
    w = dict(w, w_pa=ag_pa.reshape(D, D), w_pb=ag_pb.reshape(SSM_INNER, D), w_out=ag_out.reshape(D, D),
             wt_gu=jnp.concatenate([ag_gate.reshape(D_FF, D), ag_up.reshape(D_FF, D)], axis=0), w_down=ag_down.reshape(D_FF, D))
    got = {}

    in_gp = [_col(ogf, tl), _col(ogb, tl), _col(proj, tl, 1024, C_R // 1024)]
    p_gp = [w["gla_norm_w"]]
    (oa,) = _stage_fwd(_f_gla_post, n_tl, in_gp, p_gp, [(A, D, BF16, tl)], "gla_post")
    in_sp = [_col(ysf, TM), _col(ysb, TM), _col(xbc, TM, SSM_INNER, 0), _col(proj, TM, SSM_INNER, C_Z // SSM_INNER)]
    p_sp = [w["dsk8"], w["ssm_norm_w"]]
    (ob,) = _stage_fwd(_f_ssd_post, n_tm, in_sp, p_sp, [(A, SSM_INNER, BF16, TM)], "ssd_post")
    ya = _mm(oa, w["w_pa"], trans_b=False, out_dtype=BF16, name="mm_pa")
    yb = _mm(ob, w["w_pb"], trans_b=False, out_dtype=BF16, name="mm_pb")
    in_mg = [_col(proj, tl, 1024, C_GA // 1024), _col(proj, tl, 1024, C_GB // 1024), _col(ya, tl), _col(yb, tl)]
    (merged,) = _stage_fwd(_f_merge, n_tl, in_mg, [], [(A, D, BF16, tl)], "merge")
    mix = _mm(merged, w["w_out"], trans_b=False, out_dtype=BF16, name="mm_out")

    f_res1 = _f_res1(nct)
    in_r1 = x_in + [_col(mix, TM)]
    p_r1 = [ada, w["norm2_w"]]
    h2, hm2 = _stage_fwd(f_res1, n_tm, in_r1, p_r1, [(A, D, F32, TM), (A, D, BF16, TM)], "res1")
    gu = _mm(hm2, w["wt_gu"], trans_b=True, out_dtype=BF16, name="mm_gu", tn_t=1408)
    in_sw = [_col(gu, TM)]
    (act,) = _stage_fwd(_f_swiglu, n_tm, in_sw, [], [(A, D_FF, BF16, TM)], "swiglu")
    dn = _mm(act, w["w_down"], trans_b=False, out_dtype=BF16, name="mm_down")

    loss_blk, d_h2a, d_dn, d_ada3, g["final_norm_w"] = _loss_head(h2, dn, target, ada, w["final_norm_w"], nct)
    rows16 = lambda gw: (gw.reshape(N_DEV, gw.shape[0] // N_DEV, gw.shape[1]).astype(BF16), True)
    gw_down = _mm_tn(act, d_dn, name="dw_down", tr_t=1408)
    d_act = _mm(d_dn, w["w_down"], trans_b=True, out_dtype=BF16, name="dx_down", tn_t=1408)
    d_gu, got["w_down"] = _stage_bwd(_f_swiglu, n_tm, in_sw, [], [_col(d_act, TM)], lambda i, t: t, [(BF16, None)],
                                     "swiglu_b", hosted=[rows16(gw_down)])
    gwt_gu = _mm_tn(d_gu, hm2, name="dw_gu", tr_t=1408)
    d_hm2 = _mm(d_gu, w["wt_gu"], trans_b=False, out_dtype=BF16, name="dx_gu")
    d_x1, d_mix, d_ada2, g["norm2_w"], got["w_gate"] = _stage_bwd(
        f_res1, n_tm, in_r1, p_r1, [_col(d_h2a, TM), _col(d_hm2, TM)], lambda i, t: t,
        [None, (F32, None), (BF16, None)], "res1_b", hosted=[rows16(gwt_gu[:D_FF])])
    gw_out = _mm_tn(merged, d_mix, name="dw_out")
    d_merged = _mm(d_mix, w["w_out"], trans_b=True, out_dtype=BF16, name="dx_out")
    d_ya, d_yb, d_proj, got["w_up"] = _stage_bwd(
        _f_merge, n_tl, in_mg, [], [_col(d_merged, tl)], lambda i, t: t, [None, None, (BF16, None), (BF16, None)], "merge_b",
        hosted=[rows16(gwt_gu[D_FF:])], pack=([0, 1], None, D_INP, C_GA // (2 * D)))
    gw_pa = _mm_tn(oa, d_ya, name="dw_pa")
    gw_pb = _mm_tn(ob, d_yb, name="dw_pb")
    d_oa = _mm(d_ya, w["w_pa"], trans_b=True, out_dtype=BF16, name="dx_pa")
    d_ob = _mm(d_yb, w["w_pb"], trans_b=True, out_dtype=BF16, name="dx_pb")
    d_og, g["gla_norm_w"], d_proj, got["w_out"] = _stage_bwd(
        _f_gla_post, n_tl, in_gp, p_gp, [_col(d_oa, tl)], lambda i, t: t, [(BF16, None), None, None], "gla_post_b",
        hosted=[rows16(gw_out)], pack=([2], d_proj, D_INP, C_R // D))
    d_ys, d_xs_skip, g["dsk8"], g["ssm_norm_w"], d_proj, got["w_pa"], got["w_pb"] = _stage_bwd(
        _f_ssd_post, n_tm, in_sp, p_sp, [_col(d_ob, TM)], lambda i, t: t,
        [(BF16, None), None, (BF16, None), None], "ssd_post_b", hosted=[rows16(gw_pa), rows16(gw_pb)],
        pack=([3], d_proj, D_INP, C_Z // SSM_INNER))

    dq, dk, dv, d_lgf = _gla_bwd(proj, lg_f, sv_gf, d_og, None, ncc_g, False, "gla_f_b")
    d_proj, d_lgb = _gla_bwd(proj, lg_b, sv_gb, d_og, (dq, dk, dv), ncc_g, True, "gla_b_b", pack=(d_proj, D_INP))
    d_xf, d_bm, d_cm, d_axf, d_atf = _ssd_bwd(xf, xbc, axf, atf, sv_sf, d_ys, None, ncc_s, False, "ssd_f_b")
    d_xb, d_bm, d_cm, d_axb, d_atb = _ssd_bwd(xb, xbc, axb, atb, sv_sb, d_ys, (d_bm, d_cm), ncc_s, True, "ssd_b_b")

    cts_gd = [_col(a, TS) for a in (d_lgf, d_lgb, d_xf, d_xb, d_axf, d_axb)]
    cts_gd += [(a, (n_heads, TS), lambda i: (0, i)) for a in (d_atf, d_atb)]
    d_xs_dt, g["up_f"], g["up_b"], g["gla_bias_f"], g["gla_bias_b"], g["dtb"], g["alog"], d_proj = _stage_bwd(
        _f_gates_dt, n_ts, in_gd, p_gd, cts_gd, lambda i, t: t, [None, (BF16, None)], "gates_dt_b",
        pack=([0], d_proj, D_INP, C_S // 128))
    g["conv_w"], g["conv_b"], d_proj = _conv_bwd(proj, conv_pre, (d_xs_skip, d_xs_dt, d_bm, d_cm), w["conv_w"], d_proj,
                                                 nct, Tc, "conv_b")
    gwt_in = _mm_tn(d_proj, h1, name="dw_in", tr_t=1152)
    d_h1, got["w_in"] = _mm(d_proj, w["wt_in"], trans_b=False, out_dtype=BF16, name="dx_in", tm_t=768, tk_t=3456,
                            hosted=[rows16(_from_padded(gwt_in))])
    grad_x, g["norm1_w"], d_ada1 = _stage_bwd(
        f_norm1, n_tm, x_in, p_norm1, [_col(d_h1, TM)], lambda i, t: t,
        [None, (F32, (d_x1, (TM, D), x_in[1][2]))], "norm1_b")
    return loss_blk, grad_x, g, got, (d_ada1, d_ada2, d_ada3)


def _exchange(x, a2a, name):
    shp = x.shape[1:] if a2a else x.shape

    def body(x_ref, o_ref, send_sems, recv_sems, local_sem):
        ops = _exchange_ops(x_ref, o_ref, send_sems, recv_sems, local_sem, a2a)
        for op in ops:
            op.start()
        for op in ops:
            op.wait()

    return pl.pallas_call(
        body, out_shape=jax.ShapeDtypeStruct((N_DEV,) + tuple(shp), x.dtype),
        in_specs=[pl.BlockSpec(memory_space=pl.ANY)], out_specs=pl.BlockSpec(memory_space=pl.ANY),
        scratch_shapes=[pltpu.SemaphoreType.DMA((N_DEV - 1,)), pltpu.SemaphoreType.DMA((N_DEV - 1,)),
                        pltpu.SemaphoreType.DMA(())],
        name=name,
    )(x)


def _all_gather(x, name):
    def body(x_ref, o_ref, send_sems, recv_sems, local_sem):
        mx, my, mc = lax.axis_index("x"), lax.axis_index("y"), lax.axis_index("c")
        me, sibling = (mx, my, mc), (mx, my, 1 - mc)
        chips = [(1 - mx, my), (mx, 1 - my), (1 - mx, 1 - my)]

        def slot(px, py, pc):
            return o_ref.at[4 * px + 2 * py + pc]

        def copy(k, block, to, src=None):
            return pltpu.make_async_remote_copy(
                src_ref=slot(*block) if src is None else src, dst_ref=slot(*block),
                send_sem=send_sems.at[k], recv_sem=recv_sems.at[k], device_id=to, device_id_type=pl.DeviceIdType.MESH)

        own = pltpu.make_async_copy(x_ref, slot(*me), local_sem)
        own.start()
        first = [copy(0, me, sibling, src=x_ref)] + [copy(1 + j, me, (*chip, mc), src=x_ref) for j, chip in enumerate(chips)]
        for cp in first:
            cp.start()
        passed = [copy(4 + j, (*chip, mc), sibling) for j, chip in enumerate(chips)]
        for j, chip in enumerate(chips):
            copy(1 + j, (*chip, mc), me).wait_recv()
            passed[j].start()
        copy(0, sibling, me).wait_recv()
        for j, chip in enumerate(chips):
            copy(4 + j, (*chip, 1 - mc), me).wait_recv()
        for cp in first + passed:
            cp.wait_send()
        own.wait()

    return pl.pallas_call(
        body, out_shape=jax.ShapeDtypeStruct((N_DEV,) + tuple(x.shape), x.dtype),
        in_specs=[pl.BlockSpec(memory_space=pl.ANY)], out_specs=pl.BlockSpec(memory_space=pl.ANY),
        scratch_shapes=[pltpu.SemaphoreType.DMA((N_DEV - 1,)), pltpu.SemaphoreType.DMA((N_DEV - 1,)),
                        pltpu.SemaphoreType.DMA(())],
        name=name,
    )(x)


def _adamw_math(w, gr, m, v):
    m = ADAM_B1 * m + (1.0 - ADAM_B1) * gr
    v = ADAM_B2 * v + (1.0 - ADAM_B2) * (gr * gr)
    m_hat = m / np.float32(1.0 - ADAM_B1 ** ADAM_STEP)
    v_hat = v / np.float32(1.0 - ADAM_B2 ** ADAM_STEP)
    delta = -ADAM_LR * (m_hat / (jnp.sqrt(v_hat) + ADAM_EPS) + ADAM_WD * w)
    return delta, m, v


def _sum_adamw(parts, w, m, v, name):
    R, C = w.shape
    n_parts = parts.shape[0]
    tr = _pick(R, max(16, (2 << 20) // (4 * C) // 16 * 16), 16)

    def body(p_ref, w_ref, m_ref, v_ref, g_ref, d_ref, mo_ref, vo_ref):
        gr = p_ref[0].astype(F32)
        for k in range(1, n_parts):
            gr = gr + p_ref[k].astype(F32)
        g_ref[...] = gr
        d_ref[...], mo_ref[...], vo_ref[...] = _adamw_math(w_ref[...], gr, m_ref[...], v_ref[...])

    if R % 16 == 0:
        grid = (R // tr,)
        tile, p_spec = pl.BlockSpec((tr, C), lambda i: (i, 0)), pl.BlockSpec((n_parts, tr, C), lambda i: (0, i, 0))
    else:
        tc = _pick(C, max(128, (2 << 20) // (4 * R) // 128 * 128), 128)
        grid = (C // tc,)
        tile, p_spec = pl.BlockSpec((R, tc), lambda i: (0, i)), pl.BlockSpec((n_parts, R, tc), lambda i: (0, 0, i))
    return pl.pallas_call(
        body, grid=grid, in_specs=[p_spec, tile, tile, tile],
        out_specs=[tile] * 4, out_shape=[jax.ShapeDtypeStruct((R, C), F32)] * 4,
        compiler_params=_cparams(("parallel",)), name=name,
    )(parts, w, m, v)


def _sum8(parts, name):
    _, R, C = parts.shape

    def body(p_ref, g_ref):
        gr = p_ref[0]
        for k in range(1, N_DEV):
            gr = gr + p_ref[k]
        g_ref[...] = gr

    return pl.pallas_call(body, out_shape=jax.ShapeDtypeStruct((R, C), F32), name=name)(parts)


def _adamw(w, gr, m, v, name):
    def body(w_ref, g_ref, m_ref, v_ref, d_ref, mo_ref, vo_ref):
        d_ref[...], mo_ref[...], vo_ref[...] = _adamw_math(w_ref[...], g_ref[...], m_ref[...], v_ref[...])

    return pl.pallas_call(body, out_shape=[jax.ShapeDtypeStruct(w.shape, F32)] * 3, name=name)(w, gr, m, v)


def _to_padded(wt_in):
    z = jnp.zeros((32,) + wt_in.shape[1:], wt_in.dtype)
    return jnp.concatenate([wt_in[0:3072], wt_in[5152:8224], wt_in[3104:5152], wt_in[8288:10336],
                            wt_in[3072:3104], wt_in[8224:8288], z], axis=0)


def _from_padded(p):
    return jnp.concatenate([p[0:3072], p[10240:10272], p[6144:8192], p[3072:6144], p[10272:10336], p[8192:10240]], axis=0)


def _unshard_cols(gathered):
    n, r, c = gathered.shape
    return jnp.transpose(gathered, (1, 0, 2)).reshape(r, n * c)


def _lanes(vec, lo):
    return jnp.concatenate([jnp.zeros((1, lo), F32), vec, jnp.zeros((1, 128 - lo - vec.shape[1]), F32)], axis=1)


_SMALL = (("b_ada", 6 * D), ("c_ctx", D), ("norm1_w", D), ("gla_bias_f", 512), ("gla_bias_b", 512), ("gla_norm_w", 256),
          ("conv_b", 3072), ("dt_bias_f", 32), ("dt_bias_b", 32), ("a_log_f", 32), ("a_log_b", 32), ("d_skip", 32),
          ("ssm_norm_w", 2048), ("norm2_w", D), ("final_norm_w", D))
_SHARDED_SMALL = (("gla_up_f", 16 * 512), ("gla_up_b", 16 * 512), ("conv_w", 4 * 3072))


def _pack(vals, names):
    flat = jnp.concatenate([vals[n].reshape(-1).astype(F32) for n, _ in names])
    pad = (-flat.shape[0]) % 1024
    return jnp.concatenate([flat, jnp.zeros((pad,), F32)]).reshape(-1, 128)


def _unpack(packed, names, shapes):
    flat, out, off = packed.reshape(-1), {}, 0
    for n, size in names:
        out[n] = flat[off:off + size].reshape(shapes[n])
        off += size
    return out


def kernel(x, c, ctx, c_ctx, w_ada, b_ada, norm1_w, w_in, gla_up_f, gla_bias_f, gla_up_b, gla_bias_b, gla_norm_w, conv_w, conv_b, dt_bias_f, dt_bias_b, a_log_f, a_log_b, d_skip, ssm_norm_w, w_pa, w_pb, w_out, norm2_w, w_gate, w_up, w_down, final_norm_w, loss_target, m_c_ctx, m_w_ada, m_b_ada, m_norm1_w, m_w_in, m_gla_up_f, m_gla_bias_f, m_gla_up_b, m_gla_bias_b, m_gla_norm_w, m_conv_w, m_conv_b, m_dt_bias_f, m_dt_bias_b, m_a_log_f, m_a_log_b, m_d_skip, m_ssm_norm_w, m_w_pa, m_w_pb, m_w_out, m_norm2_w, m_w_gate, m_w_up, m_w_down, m_final_norm_w, v_c_ctx, v_w_ada, v_b_ada, v_norm1_w, v_w_in, v_gla_up_f, v_gla_bias_f, v_gla_up_b, v_gla_bias_b, v_gla_norm_w, v_conv_w, v_conv_b, v_dt_bias_f, v_dt_bias_b, v_a_log_f, v_a_log_b, v_d_skip, v_ssm_norm_w, v_w_pa, v_w_pb, v_w_out, v_norm2_w, v_w_gate, v_w_up, v_w_down, v_final_norm_w):
    args = dict(locals())
    me = 4 * lax.axis_index("x") + 2 * lax.axis_index("y") + lax.axis_index("c")

    t_names = ("w_in", "w_gate", "w_up")
    local = {n: (jnp.transpose(args[n][0]) if n in t_names else args[n][0]) for n in
             ("w_in", "w_pa", "w_pb", "w_out", "w_gate", "w_up", "w_down")}
    full = {"wt_in": _to_padded(_all_gather(local["w_in"].astype(BF16), "ag_w_in").reshape(D_IN, D))}
    shards = {n: local[n].astype(BF16) for n in ("w_pa", "w_pb", "w_out", "w_gate", "w_up", "w_down")}
    sm = _exchange(jnp.concatenate([gla_up_f.reshape(-1), gla_up_b.reshape(-1), conv_w.reshape(-1), c.reshape(-1)]).reshape(-1, 128),
                   False, "ag_small")
    sm = sm.reshape(N_DEV, -1)
    n_ada = w_ada.shape[2]
    cc = jnp.concatenate([sm[:, 3584:3584 + D], c_ctx[None], jnp.zeros((_ADA_ROWS - N_DEV - 1, D), F32)], axis=0)
    w_ada16 = w_ada[0].astype(BF16)
    ada_cols = _exchange(_ada_fwd(cc, w_ada16, lax.dynamic_slice(b_ada, (0, me * n_ada), (1, n_ada))), False, "ag_ada")
    ada = jnp.concatenate([lax.dynamic_index_in_dim(ada_cols, me, axis=1, keepdims=False).reshape(1, -1),
                           ada_cols[:, N_DEV].reshape(1, -1), jnp.zeros((6, N_DEV * n_ada), F32)], axis=0)
    up_f = _unshard_cols(sm[:, 0:1024].reshape(N_DEV, 16, 64))
    up_b = _unshard_cols(sm[:, 1024:2048].reshape(N_DEV, 16, 64))
    full["conv_w"] = _unshard_cols(sm[:, 2048:3584].reshape(N_DEV, 4, 384))
    full["up_f"] = jnp.concatenate([up_f, jnp.zeros((112, 512), F32)], axis=0)
    full["up_b"] = jnp.concatenate([jnp.zeros((16, 512), F32), up_b, jnp.zeros((96, 512), F32)], axis=0)
    full["dtb"] = jnp.concatenate([jnp.zeros((1, 32), F32), dt_bias_f, dt_bias_b, jnp.zeros((1, 32), F32)], axis=1)
    full["alog"] = jnp.concatenate([jnp.zeros((1, 32), F32), a_log_f, a_log_b, jnp.zeros((1, 32), F32)], axis=1)
    full["dsk8"] = jnp.concatenate([_lanes(d_skip, 0), jnp.zeros((7, 128), F32)], axis=0)
    for n in ("norm1_w", "gla_bias_f", "gla_bias_b", "gla_norm_w", "conv_b", "ssm_norm_w", "norm2_w"):
        full[n] = args[n]
    full["final_norm_w"] = final_norm_w.reshape(1, D)

    loss_blk, grad_x, g, got, d_ada = _local_step(x[0], ctx[0], loss_target[0], ada, full, shards)

    d_all = _exchange(jnp.concatenate([d[0:2] for d in d_ada], axis=0), False, "ag_d_ada").reshape(N_DEV * 6, N_DEV * n_ada)
    gw_ada, gb_ada, d_cc = _ada_bwd(cc, w_ada16, d_all, lax.dynamic_slice(d_all, (0, me * n_ada), (N_DEV * 6, n_ada)))

    gs = dict(g, c_ctx=d_cc[N_DEV], b_ada=jnp.zeros_like(b_ada))
    gs["dt_bias_f"], gs["dt_bias_b"] = g["dtb"][:, 32:64], g["dtb"][:, 64:96]
    gs["a_log_f"], gs["a_log_b"] = g["alog"][:, 32:64], g["alog"][:, 64:96]
    gs["d_skip"] = g["dsk8"][0:1, 0:32]
    gs["gla_up_f"], gs["gla_up_b"] = g["up_f"][0:16], g["up_b"][16:32]
    gs["loss"] = loss_blk[0:1, 0:1]
    names = _SMALL + _SHARDED_SMALL + (("loss", 1),)
    shapes = {n: (args[n].shape if n in args else (1, 1)) for n, _ in names}
    shapes.update({"gla_up_f": (16, 512), "gla_up_b": (16, 512), "conv_w": (4, 3072)})
    red = _unpack(_sum8(_exchange(_pack(gs, names), False, "ag_small_grads"), "sum_small_grads"), names, shapes)
    loss = red["loss"].reshape(())
    grads = {n: red[n] for n, _ in _SMALL}
    grads["b_ada"] = gb_ada
    grads["gla_up_f"] = lax.dynamic_slice(red["gla_up_f"], (0, me * 64), (16, 64))[None]
    grads["gla_up_b"] = lax.dynamic_slice(red["gla_up_b"], (0, me * 64), (16, 64))[None]
    grads["conv_w"] = lax.dynamic_slice(red["conv_w"], (0, me * 384), (4, 384))[None]
    upd_names = tuple((n, s) for n, s in _SMALL) + (("gla_up_f", 1024), ("gla_up_b", 1024), ("conv_w", 1536))
    pk = lambda prefix, src: _pack({n: src[prefix + n] for n, _ in upd_names}, upd_names)
    d_s, m_s, v_s = _adamw(pk("", args), _pack(grads, upd_names), pk("m_", args), pk("v_", args), "adamw_small")
    upd_shapes = {n: args[n].shape for n, _ in upd_names}
    delta = _unpack(d_s, upd_names, upd_shapes)
    new_m = _unpack(m_s, upd_names, upd_shapes)
    new_v = _unpack(v_s, upd_names, upd_shapes)

    got["w_ada"] = gw_ada[None]
    for n, parts in got.items():
        orient = jnp.transpose if n in t_names else (lambda a: a)
        res = _sum_adamw(parts, local[n] if n in local else args[n][0], orient(args["m_" + n][0]), orient(args["v_" + n][0]),
                         "adamw_" + n)
        grads[n], delta[n], new_m[n], new_v[n] = [orient(r)[None] for r in res]

    order = ["c_ctx", "w_ada", "b_ada", "norm1_w", "w_in", "gla_up_f", "gla_bias_f", "gla_up_b", "gla_bias_b", "gla_norm_w",
             "conv_w", "conv_b", "dt_bias_f", "dt_bias_b", "a_log_f", "a_log_b", "d_skip", "ssm_norm_w", "w_pa", "w_pb",
             "w_out", "norm2_w", "w_gate", "w_up", "w_down", "final_norm_w"]
    fix = lambda d: [d[n].reshape(args[n].shape) for n in order]
    return (loss, grad_x[None], *fix(grads), *fix(delta), *fix(new_m), *fix(new_v))
```
